```python
import math
import jax, jax.numpy as jnp
from jax import lax
import numpy as np

D_MODEL = 1024
BATCH = 8
SEQ = 4096
DEPTH = 1

N_META = 16
CONV_WIDTH = D_MODEL
CONV_KERNEL = 31
POOL_WIDTH = D_MODEL
POOL_WINDOWS = (2, 4, 8, 16)
N_POOL_GROUPS = len(POOL_WINDOWS)
POOL_GROUP_DIM = POOL_WIDTH // N_POOL_GROUPS
N_BRANCHES = 2
D_FF = int(math.ceil(8 * D_MODEL / 3 / 256) * 256)
D_IN = 2 * CONV_WIDTH + POOL_WIDTH + N_BRANCHES * D_MODEL
RMS_EPS = 1e-6
LN_EPS = 1e-5

kernel_name = "hybrid_conv_pool_gated_encoder_block"


def rms_norm(x, g):
    xf = x.astype(jnp.float32)
    y = xf * lax.rsqrt(jnp.mean(xf * xf, axis=-1, keepdims=True) + RMS_EPS)
    return (y * g.astype(jnp.float32)).astype(x.dtype)


def layer_norm(x, g, b):
    xf = x.astype(jnp.float32)
    mu = jnp.mean(xf, axis=-1, keepdims=True)
    xc = xf - mu
    var = jnp.mean(xc * xc, axis=-1, keepdims=True)
    y = xc * lax.rsqrt(var + LN_EPS)
    return (y * g.astype(jnp.float32) + b.astype(jnp.float32)).astype(x.dtype)


def conformer_conv(a_val, a_gate, w_dw, b_dw, ln_g, ln_b, w_conv_out):
    a = a_val * jax.nn.sigmoid(a_gate)
    pad = CONV_KERNEL // 2
    a = lax.conv_general_dilated(
        a, w_dw[:, None, :].astype(a.dtype),
        window_strides=(1,), padding=[(pad, pad)],
        dimension_numbers=("NWC", "WIO", "NWC"),
        feature_group_count=a.shape[-1]) + b_dw
    a = jax.nn.silu(layer_norm(a, ln_g, ln_b))
    return a @ w_conv_out


def multi_scale_pool(p, w_pool, pool_scale, w_pool_out):
    B, T, C = p.shape
    pf = p.astype(jnp.float32)
    cs = jnp.concatenate([jnp.zeros((B, 1, C), jnp.float32), jnp.cumsum(pf, axis=1)], axis=1)
    t = jnp.arange(T)
    outs = []
    for k, w in enumerate(POOL_WINDOWS):
        left = w // 2
        right = w - 1 - left
        lo = jnp.clip(t - left, 0, T)
        hi = jnp.clip(t + right + 1, 0, T)
        sl = slice(k * POOL_GROUP_DIM, (k + 1) * POOL_GROUP_DIM)
        csg = cs[..., sl]
        s = jnp.take(csg, hi, axis=1) - jnp.take(csg, lo, axis=1)
        cnt = (hi - lo).astype(jnp.float32)[None, :, None]
        outs.append(s / cnt - pf[..., sl])
    m = jnp.stack(outs, axis=2).astype(p.dtype)
    m = jnp.einsum("btgc,gcd->btgd", m, w_pool).reshape(B, T, C)
    return (m * pool_scale) @ w_pool_out


def _fwd_setup_inputs(seed: int = 0) -> dict:
    key = jax.random.key(seed)
    ks = jax.random.split(key, 24)
    f32 = jnp.float32
    L, D = DEPTH, D_MODEL

    def nrm(k, shape, fan_in):
        return jax.random.normal(k, shape, f32) * (fan_in ** -0.5)

    def gain(k, shape):
        return 1.0 + 0.05 * jax.random.normal(k, shape, f32)

    return {
        "x": jax.random.normal(ks[0], (BATCH, SEQ, D), f32),
        "meta_tokens": jax.random.normal(ks[1], (N_META, D), f32),
        "g_mix": gain(ks[2], (L, D)),
        "w_in": nrm(ks[3], (L, D, D_IN), D),
        "b_gate": 0.02 * jax.random.normal(ks[4], (L, N_BRANCHES * D), f32),
        "w_dw": nrm(ks[5], (L, CONV_KERNEL, CONV_WIDTH), CONV_KERNEL),
        "b_dw": 0.02 * jax.random.normal(ks[6], (L, CONV_WIDTH), f32),
        "ln_g": gain(ks[7], (L, CONV_WIDTH)),
        "ln_b": 0.02 * jax.random.normal(ks[8], (L, CONV_WIDTH), f32),
        "w_conv_out": nrm(ks[9], (L, CONV_WIDTH, D), CONV_WIDTH),
        "w_pool": nrm(ks[10], (L, N_POOL_GROUPS, POOL_GROUP_DIM, POOL_GROUP_DIM), POOL_GROUP_DIM),
        "pool_scale": gain(ks[11], (L, POOL_WIDTH)),
        "w_pool_out": nrm(ks[12], (L, POOL_WIDTH, D), POOL_WIDTH),
        "w_o": nrm(ks[13], (L, D, D), D),
        "g_ffn": gain(ks[14], (L, D)),
        "w_ffn_gate": nrm(ks[15], (L, D, D_FF), D),
        "w_ffn_up": nrm(ks[16], (L, D, D_FF), D),
        "w_ffn_down": nrm(ks[17], (L, D_FF, D), D_FF),
        "g_final": gain(ks[18], (D,)),
    }


def _fwd_reference(x, meta_tokens, g_mix, w_in, b_gate, w_dw, b_dw, ln_g, ln_b, w_conv_out,
              w_pool, pool_scale, w_pool_out, w_o, g_ffn, w_ffn_gate, w_ffn_up,
              w_ffn_down, g_final):
    B = x.shape[0]
    meta = jnp.broadcast_to(meta_tokens[None].astype(x.dtype), (B, N_META, D_MODEL))
    h = jnp.concatenate([meta, x], axis=1)
    c1 = CONV_WIDTH
    c2 = 2 * CONV_WIDTH
    c3 = c2 + POOL_WIDTH
    c4 = c3 + D_MODEL
    for l in range(DEPTH):
        u = rms_norm(h, g_mix[l])
        z = u @ w_in[l]
        gates = jax.nn.sigmoid(z[..., c3:] + b_gate[l])
        y_conv = conformer_conv(z[..., :c1], z[..., c1:c2], w_dw[l], b_dw[l],
                                ln_g[l], ln_b[l], w_conv_out[l])
        y_pool = multi_scale_pool(z[..., c2:c3], w_pool[l], pool_scale[l], w_pool_out[l])
        merged = gates[..., :D_MODEL] * y_conv + gates[..., D_MODEL:] * y_pool
        h = h + merged @ w_o[l]
        v = rms_norm(h, g_ffn[l])
        f = jax.nn.silu(v @ w_ffn_gate[l]) * (v @ w_ffn_up[l])
        h = h + f @ w_ffn_down[l]
    h = rms_norm(h, g_final)
    return h[:, N_META:, :]


import jax as _jax
import jax.numpy as _jnp

TWIN_FORMAT = 'train_step'
FWD_PARAMS = ['x', 'meta_tokens', 'g_mix', 'w_in', 'b_gate', 'w_dw', 'b_dw', 'ln_g', 'ln_b', 'w_conv_out', 'w_pool', 'pool_scale', 'w_pool_out', 'w_o', 'g_ffn', 'w_ffn_gate', 'w_ffn_up', 'w_ffn_down', 'g_final']
TWIN_WEIGHTS = ['meta_tokens', 'g_mix', 'w_in', 'b_gate', 'w_dw', 'b_dw', 'ln_g', 'ln_b', 'w_conv_out', 'w_pool', 'pool_scale', 'w_pool_out', 'w_o', 'g_ffn', 'w_ffn_gate', 'w_ffn_up', 'w_ffn_down', 'g_final']
TWIN_DIFF_INPUT = 'x'
TWIN_INPUTS = ['x', 'meta_tokens', 'g_mix', 'w_in', 'b_gate', 'w_dw', 'b_dw', 'ln_g', 'ln_b', 'w_conv_out', 'w_pool', 'pool_scale', 'w_pool_out', 'w_o', 'g_ffn', 'w_ffn_gate', 'w_ffn_up', 'w_ffn_down', 'g_final', 'loss_target', 'm_meta_tokens', 'm_g_mix', 'm_w_in', 'm_b_gate', 'm_w_dw', 'm_b_dw', 'm_ln_g', 'm_ln_b', 'm_w_conv_out', 'm_w_pool', 'm_pool_scale', 'm_w_pool_out', 'm_w_o', 'm_g_ffn', 'm_w_ffn_gate', 'm_w_ffn_up', 'm_w_ffn_down', 'm_g_final', 'v_meta_tokens', 'v_g_mix', 'v_w_in', 'v_b_gate', 'v_w_dw', 'v_b_dw', 'v_ln_g', 'v_ln_b', 'v_w_conv_out', 'v_w_pool', 'v_pool_scale', 'v_w_pool_out', 'v_w_o', 'v_g_ffn', 'v_w_ffn_gate', 'v_w_ffn_up', 'v_w_ffn_down', 'v_g_final']
TWIN_OUTPUTS = ['loss', 'grad_x', 'grad_meta_tokens', 'grad_g_mix', 'grad_w_in', 'grad_b_gate', 'grad_w_dw', 'grad_b_dw', 'grad_ln_g', 'grad_ln_b', 'grad_w_conv_out', 'grad_w_pool', 'grad_pool_scale', 'grad_w_pool_out', 'grad_w_o', 'grad_g_ffn', 'grad_w_ffn_gate', 'grad_w_ffn_up', 'grad_w_ffn_down', 'grad_g_final', 'delta_meta_tokens', 'delta_g_mix', 'delta_w_in', 'delta_b_gate', 'delta_w_dw', 'delta_b_dw', 'delta_ln_g', 'delta_ln_b', 'delta_w_conv_out', 'delta_w_pool', 'delta_pool_scale', 'delta_w_pool_out', 'delta_w_o', 'delta_g_ffn', 'delta_w_ffn_gate', 'delta_w_ffn_up', 'delta_w_ffn_down', 'delta_g_final', 'new_m_meta_tokens', 'new_m_g_mix', 'new_m_w_in', 'new_m_b_gate', 'new_m_w_dw', 'new_m_b_dw', 'new_m_ln_g', 'new_m_ln_b', 'new_m_w_conv_out', 'new_m_w_pool', 'new_m_pool_scale', 'new_m_w_pool_out', 'new_m_w_o', 'new_m_g_ffn', 'new_m_w_ffn_gate', 'new_m_w_ffn_up', 'new_m_w_ffn_down', 'new_m_g_final', 'new_v_meta_tokens', 'new_v_g_mix', 'new_v_w_in', 'new_v_b_gate', 'new_v_w_dw', 'new_v_b_dw', 'new_v_ln_g', 'new_v_ln_b', 'new_v_w_conv_out', 'new_v_w_pool', 'new_v_pool_scale', 'new_v_w_pool_out', 'new_v_w_o', 'new_v_g_ffn', 'new_v_w_ffn_gate', 'new_v_w_ffn_up', 'new_v_w_ffn_down', 'new_v_g_final']
TWIN_LEAF_KINDS = {'loss': 'loss', 'grad_x': 'grad_x', 'grad_meta_tokens': 'grad_w', 'grad_g_mix': 'grad_w', 'grad_w_in': 'grad_w', 'grad_b_gate': 'grad_w', 'grad_w_dw': 'grad_w', 'grad_b_dw': 'grad_w', 'grad_ln_g': 'grad_w', 'grad_ln_b': 'grad_w', 'grad_w_conv_out': 'grad_w', 'grad_w_pool': 'grad_w', 'grad_pool_scale': 'grad_w', 'grad_w_pool_out': 'grad_w', 'grad_w_o': 'grad_w', 'grad_g_ffn': 'grad_w', 'grad_w_ffn_gate': 'grad_w', 'grad_w_ffn_up': 'grad_w', 'grad_w_ffn_down': 'grad_w', 'grad_g_final': 'grad_w', 'delta_meta_tokens': 'delta_w', 'delta_g_mix': 'delta_w', 'delta_w_in': 'delta_w', 'delta_b_gate': 'delta_w', 'delta_w_dw': 'delta_w', 'delta_b_dw': 'delta_w', 'delta_ln_g': 'delta_w', 'delta_ln_b': 'delta_w', 'delta_w_conv_out': 'delta_w', 'delta_w_pool': 'delta_w', 'delta_pool_scale': 'delta_w', 'delta_w_pool_out': 'delta_w', 'delta_w_o': 'delta_w', 'delta_g_ffn': 'delta_w', 'delta_w_ffn_gate': 'delta_w', 'delta_w_ffn_up': 'delta_w', 'delta_w_ffn_down': 'delta_w', 'delta_g_final': 'delta_w', 'new_m_meta_tokens': 'new_m', 'new_m_g_mix': 'new_m', 'new_m_w_in': 'new_m', 'new_m_b_gate': 'new_m', 'new_m_w_dw': 'new_m', 'new_m_b_dw': 'new_m', 'new_m_ln_g': 'new_m', 'new_m_ln_b': 'new_m', 'new_m_w_conv_out': 'new_m', 'new_m_w_pool': 'new_m', 'new_m_pool_scale': 'new_m', 'new_m_w_pool_out': 'new_m', 'new_m_w_o': 'new_m', 'new_m_g_ffn': 'new_m', 'new_m_w_ffn_gate': 'new_m', 'new_m_w_ffn_up': 'new_m', 'new_m_w_ffn_down': 'new_m', 'new_m_g_final': 'new_m', 'new_v_meta_tokens': 'new_v', 'new_v_g_mix': 'new_v', 'new_v_w_in': 'new_v', 'new_v_b_gate': 'new_v', 'new_v_w_dw': 'new_v', 'new_v_b_dw': 'new_v', 'new_v_ln_g': 'new_v', 'new_v_ln_b': 'new_v', 'new_v_w_conv_out': 'new_v', 'new_v_w_pool': 'new_v', 'new_v_pool_scale': 'new_v', 'new_v_w_pool_out': 'new_v', 'new_v_w_o': 'new_v', 'new_v_g_ffn': 'new_v', 'new_v_w_ffn_gate': 'new_v', 'new_v_w_ffn_up': 'new_v', 'new_v_w_ffn_down': 'new_v', 'new_v_g_final': 'new_v'}


def _forward(args):
    return _fwd_reference(*[args[k] for k in FWD_PARAMS])


def _output_shape():
    def fwd():
        inp = _fwd_setup_inputs(0)
        return _fwd_reference(*[inp[k] for k in FWD_PARAMS])
    out = _jax.eval_shape(fwd)
    return out.shape, out.dtype

N_MICROBATCH = 1
ADAM_LR = 0.001
ADAM_B1 = 0.9
ADAM_B2 = 0.999
ADAM_EPS = 1e-08
ADAM_WD = 0.01
ADAM_STEP = 10
PER_EXAMPLE_BATCH_AXIS = {'x': 0, 'loss_target': 0}
SHARED_INPUTS = []
_WEIGHT_DTYPES = {'meta_tokens': _jnp.float32, 'g_mix': _jnp.float32, 'w_in': _jnp.float32, 'b_gate': _jnp.float32, 'w_dw': _jnp.float32, 'b_dw': _jnp.float32, 'ln_g': _jnp.float32, 'ln_b': _jnp.float32, 'w_conv_out': _jnp.float32, 'w_pool': _jnp.float32, 'pool_scale': _jnp.float32, 'w_pool_out': _jnp.float32, 'w_o': _jnp.float32, 'g_ffn': _jnp.float32, 'w_ffn_gate': _jnp.float32, 'w_ffn_up': _jnp.float32, 'w_ffn_down': _jnp.float32, 'g_final': _jnp.float32}
MOMENT_SCALE = {'meta_tokens': 2.021343e-03, 'g_mix': 1.116090e-01, 'w_in': 4.934132e-02, 'b_gate': 2.851630e-02, 'w_dw': 5.772599e-02, 'b_dw': 1.283222e-01, 'ln_g': 8.017598e-02, 'ln_b': 7.054623e-02, 'w_conv_out': 5.711149e-02, 'w_pool': 8.307347e-02, 'pool_scale': 8.251336e-02, 'w_pool_out': 8.379146e-02, 'w_o': 1.024417e-01, 'g_ffn': 1.219685e-01, 'w_ffn_gate': 5.112737e-02, 'w_ffn_up': 4.995319e-02, 'w_ffn_down': 8.313621e-02, 'g_final': 3.207642e+01}


def _to_microbatches(a, axis):
    t = _jnp.moveaxis(a, axis, 0)
    t = t.reshape((N_MICROBATCH, t.shape[0] // N_MICROBATCH) + t.shape[1:])
    return _jnp.moveaxis(t, 1, axis + 1)


def setup_inputs(seed: int = 0) -> dict:
    inp = _fwd_setup_inputs(seed)
    key = _jax.random.fold_in(_jax.random.key(seed), 7919)
    shape, _ = _output_shape()
    out = dict(inp)
    out["loss_target"] = _jax.random.normal(_jax.random.fold_in(key, 0), shape, _jnp.float32)
    for i, name in enumerate(TWIN_WEIGHTS):
        w = inp[name].astype(_jnp.float32)
        if MOMENT_SCALE is None:
            s = _jnp.sqrt(_jnp.mean(_jnp.square(w)) + 1e-30)
        else:
            s = MOMENT_SCALE[name]
        km, kv = _jax.random.split(_jax.random.fold_in(key, i + 1))
        out[name] = w
        out["m_" + name] = s * _jax.random.normal(km, w.shape, _jnp.float32)
        out["v_" + name] = (s * s) * _jax.random.uniform(kv, w.shape, _jnp.float32, 0.5, 1.5)
    if N_MICROBATCH > 1:
        for name, axis in PER_EXAMPLE_BATCH_AXIS.items():
            out[name] = _to_microbatches(out[name], axis)
    return {'x': out['x'], 'meta_tokens': out['meta_tokens'], 'g_mix': out['g_mix'], 'w_in': out['w_in'], 'b_gate': out['b_gate'], 'w_dw': out['w_dw'], 'b_dw': out['b_dw'], 'ln_g': out['ln_g'], 'ln_b': out['ln_b'], 'w_conv_out': out['w_conv_out'], 'w_pool': out['w_pool'], 'pool_scale': out['pool_scale'], 'w_pool_out': out['w_pool_out'], 'w_o': out['w_o'], 'g_ffn': out['g_ffn'], 'w_ffn_gate': out['w_ffn_gate'], 'w_ffn_up': out['w_ffn_up'], 'w_ffn_down': out['w_ffn_down'], 'g_final': out['g_final'], 'loss_target': out['loss_target'], 'm_meta_tokens': out['m_meta_tokens'], 'm_g_mix': out['m_g_mix'], 'm_w_in': out['m_w_in'], 'm_b_gate': out['m_b_gate'], 'm_w_dw': out['m_w_dw'], 'm_b_dw': out['m_b_dw'], 'm_ln_g': out['m_ln_g'], 'm_ln_b': out['m_ln_b'], 'm_w_conv_out': out['m_w_conv_out'], 'm_w_pool': out['m_w_pool'], 'm_pool_scale': out['m_pool_scale'], 'm_w_pool_out': out['m_w_pool_out'], 'm_w_o': out['m_w_o'], 'm_g_ffn': out['m_g_ffn'], 'm_w_ffn_gate': out['m_w_ffn_gate'], 'm_w_ffn_up': out['m_w_ffn_up'], 'm_w_ffn_down': out['m_w_ffn_down'], 'm_g_final': out['m_g_final'], 'v_meta_tokens': out['v_meta_tokens'], 'v_g_mix': out['v_g_mix'], 'v_w_in': out['v_w_in'], 'v_b_gate': out['v_b_gate'], 'v_w_dw': out['v_w_dw'], 'v_b_dw': out['v_b_dw'], 'v_ln_g': out['v_ln_g'], 'v_ln_b': out['v_ln_b'], 'v_w_conv_out': out['v_w_conv_out'], 'v_w_pool': out['v_w_pool'], 'v_pool_scale': out['v_pool_scale'], 'v_w_pool_out': out['v_w_pool_out'], 'v_w_o': out['v_w_o'], 'v_g_ffn': out['v_g_ffn'], 'v_w_ffn_gate': out['v_w_ffn_gate'], 'v_w_ffn_up': out['v_w_ffn_up'], 'v_w_ffn_down': out['v_w_ffn_down'], 'v_g_final': out['v_g_final']}


def _loss(weights, diff, rest, loss_target):
    with _jax.named_scope("forward"):
        args = {**rest, TWIN_DIFF_INPUT: diff, **{k: w.astype(_WEIGHT_DTYPES[k]) for k, w in weights.items()}}
        y = _forward(args)
    with _jax.named_scope("loss_head"):
        err = _jnp.square(y.astype(_jnp.float32) - loss_target)
        return 0.5 * _jnp.sum(_jnp.mean(err, axis=-1)) if err.ndim else 0.5 * err


def _adamw(w, g, m, v):
    m = ADAM_B1 * m + (1.0 - ADAM_B1) * g
    v = ADAM_B2 * v + (1.0 - ADAM_B2) * _jnp.square(g)
    m_hat = m / (1.0 - ADAM_B1 ** ADAM_STEP)
    v_hat = v / (1.0 - ADAM_B2 ** ADAM_STEP)
    delta = -ADAM_LR * (m_hat / (_jnp.sqrt(v_hat) + ADAM_EPS) + ADAM_WD * w)
    return delta, m, v


def reference(x, meta_tokens, g_mix, w_in, b_gate, w_dw, b_dw, ln_g, ln_b, w_conv_out, w_pool, pool_scale, w_pool_out, w_o, g_ffn, w_ffn_gate, w_ffn_up, w_ffn_down, g_final, loss_target, m_meta_tokens, m_g_mix, m_w_in, m_b_gate, m_w_dw, m_b_dw, m_ln_g, m_ln_b, m_w_conv_out, m_w_pool, m_pool_scale, m_w_pool_out, m_w_o, m_g_ffn, m_w_ffn_gate, m_w_ffn_up, m_w_ffn_down, m_g_final, v_meta_tokens, v_g_mix, v_w_in, v_b_gate, v_w_dw, v_b_dw, v_ln_g, v_ln_b, v_w_conv_out, v_w_pool, v_pool_scale, v_w_pool_out, v_w_o, v_g_ffn, v_w_ffn_gate, v_w_ffn_up, v_w_ffn_down, v_g_final):
    given = dict(x=x, meta_tokens=meta_tokens, g_mix=g_mix, w_in=w_in, b_gate=b_gate, w_dw=w_dw, b_dw=b_dw, ln_g=ln_g, ln_b=ln_b, w_conv_out=w_conv_out, w_pool=w_pool, pool_scale=pool_scale, w_pool_out=w_pool_out, w_o=w_o, g_ffn=g_ffn, w_ffn_gate=w_ffn_gate, w_ffn_up=w_ffn_up, w_ffn_down=w_ffn_down, g_final=g_final, loss_target=loss_target, m_meta_tokens=m_meta_tokens, m_g_mix=m_g_mix, m_w_in=m_w_in, m_b_gate=m_b_gate, m_w_dw=m_w_dw, m_b_dw=m_b_dw, m_ln_g=m_ln_g, m_ln_b=m_ln_b, m_w_conv_out=m_w_conv_out, m_w_pool=m_w_pool, m_pool_scale=m_pool_scale, m_w_pool_out=m_w_pool_out, m_w_o=m_w_o, m_g_ffn=m_g_ffn, m_w_ffn_gate=m_w_ffn_gate, m_w_ffn_up=m_w_ffn_up, m_w_ffn_down=m_w_ffn_down, m_g_final=m_g_final, v_meta_tokens=v_meta_tokens, v_g_mix=v_g_mix, v_w_in=v_w_in, v_b_gate=v_b_gate, v_w_dw=v_w_dw, v_b_dw=v_b_dw, v_ln_g=v_ln_g, v_ln_b=v_ln_b, v_w_conv_out=v_w_conv_out, v_w_pool=v_w_pool, v_pool_scale=v_pool_scale, v_w_pool_out=v_w_pool_out, v_w_o=v_w_o, v_g_ffn=v_g_ffn, v_w_ffn_gate=v_w_ffn_gate, v_w_ffn_up=v_w_ffn_up, v_w_ffn_down=v_w_ffn_down, v_g_final=v_g_final)
    weights = {n: given[n] for n in TWIN_WEIGHTS}
    shared = {n: given[n] for n in SHARED_INPUTS}
    per_example = {n: given[n] for n in ['x']}
    grad_fn = _jax.value_and_grad(_loss, argnums=(0, 1))

    def one_microbatch(ex, loss_target):
        ex = dict(ex)
        diff = ex.pop(TWIN_DIFF_INPUT)
        return grad_fn(weights, diff, {**shared, **ex}, loss_target)

    if N_MICROBATCH == 1:
        loss, (grad_w, grad_x) = one_microbatch(per_example, given["loss_target"])
    else:
        def body(carry, xs):
            loss_sum, grad_sum = carry
            l_k, (gw_k, gx_k) = one_microbatch(xs[0], xs[1])
            with _jax.named_scope("update"):
                return (loss_sum + l_k, _jax.tree.map(_jnp.add, grad_sum, gw_k)), gx_k

        init = (_jnp.zeros((), _jnp.float32), _jax.tree.map(_jnp.zeros_like, weights))
        (loss, grad_w), grad_x = _jax.lax.scan(body, init, (per_example, given["loss_target"]))
    with _jax.named_scope("update"):
        delta_w, new_m, new_v = {}, {}, {}
        for n in TWIN_WEIGHTS:
            delta_w[n], new_m[n], new_v[n] = _adamw(weights[n], grad_w[n], given["m_" + n], given["v_" + n])
    return (loss, grad_x, *[grad_w[n] for n in TWIN_WEIGHTS], *[delta_w[n] for n in TWIN_WEIGHTS],
            *[new_m[n] for n in TWIN_WEIGHTS], *[new_v[n] for n in TWIN_WEIGHTS])
```

```python
import functools

import jax
import jax.numpy as jnp
from jax import lax
from jax.experimental import pallas as pl
from jax.experimental.pallas import tpu as pltpu

F32 = jnp.float32
BF16 = jnp.bfloat16
MESH = pl.DeviceIdType.MESH

D = 1024
N_META = 16
KW = 31
CPAD = KW // 2
POOL_WINDOWS = (2, 4, 8, 16)
GD = 256
D_IN = 5 * D
D_FF = 2816
N_SHARD = 4
BR = 256
HALO = 16
PAD = BR - N_META
EXT = BR + 2 * HALO
RMS_EPS = 1e-6
LN_EPS = 1e-5
LR, B1, B2, ADAM_EPS, WD, STEP = 0.001, 0.9, 0.999, 1e-08, 0.01, 10
VMEM_LIMIT = 56 * 1024 * 1024


def _cparams(sem, vmem=VMEM_LIMIT):
    return pltpu.CompilerParams(dimension_semantics=sem, vmem_limit_bytes=vmem)


def _dot(a, b):
    return jnp.dot(a, b, preferred_element_type=F32)


def _dot_nt(a, b):
    return lax.dot_general(a, b, (((1,), (1,)), ((), ())), preferred_element_type=F32)


def _dot_tn(a, b):
    return lax.dot_general(a, b, (((0,), (0,)), ((), ())), preferred_element_type=F32)


def _sigmoid(x):
    return 1.0 / (1.0 + jnp.exp(-x))


def _row_ids(i, n, offset=0):
    return lax.broadcasted_iota(jnp.int32, (n, 1), 0) + (i * BR + offset - PAD)


def _pool_cnt(t, w, t_total):
    left = w // 2
    right = w - 1 - left
    lo = jnp.clip(t - left, 0, t_total)
    hi = jnp.clip(t + right + 1, 0, t_total)
    return jnp.maximum(hi - lo, 1).astype(F32)


def _halo_specs(col, nb):
    last = nb * (BR // HALO) - 1
    return [
        pl.BlockSpec((HALO, D), lambda i: (jnp.maximum(i * (BR // HALO) - 1, 0), col)),
        pl.BlockSpec((BR, D), lambda i: (i, col)),
        pl.BlockSpec((HALO, D), lambda i: (jnp.minimum((i + 1) * (BR // HALO), last), col)),
    ]


def _fill_ext(ext_ref, prev, cur, nxt, i, nb):
    ext_ref[0:HALO, :] = jnp.where(i > 0, prev, 0.0)
    ext_ref[HALO:HALO + BR, :] = cur
    ext_ref[HALO + BR:EXT, :] = jnp.where(i < nb - 1, nxt, 0.0)


def _row_spec(width=D):
    return pl.BlockSpec((BR, width), lambda i: (i, 0))


def _x_spec():
    return pl.BlockSpec((BR, D), lambda i: (jnp.maximum(i - 1, 0), 0))


def _const_spec(shape):
    nd = len(shape)
    return pl.BlockSpec(shape, lambda i: (0,) * nd)


def _rms_u(head, x, g_mix, nb):
    def body(head_ref, x_ref, g_ref, u_ref):
        i = pl.program_id(0)
        h = jnp.where(i == 0, head_ref[...], x_ref[...])
        r = lax.rsqrt(jnp.mean(h * h, axis=-1, keepdims=True) + RMS_EPS)
        u_ref[...] = ((h * r) * g_ref[...]).astype(BF16)

    return pl.pallas_call(
        body, name="rms_u",
        grid=(nb,),
        in_specs=[_const_spec((BR, D)), _x_spec(), _const_spec((1, D))],
        out_specs=_row_spec(),
        out_shape=jax.ShapeDtypeStruct((nb * BR, D), BF16),
        compiler_params=_cparams(("arbitrary",)),
    )(head, x, g_mix)


def _in_proj(u, w_in_b, nb):
    tp = nb * BR
    ns = w_in_b.shape[0]
    wcols = w_in_b.shape[2]

    def body(u_ref, w_ref, z_ref):
        z_ref[...] = _dot(u_ref[...], w_ref[...])

    return pl.pallas_call(
        body, name="in_proj",
        grid=(ns, nb),
        in_specs=[
            pl.BlockSpec((BR, D), lambda s, i: (i, 0)),
            pl.BlockSpec((None, D, wcols), lambda s, i: (s, 0, 0)),
        ],
        out_specs=pl.BlockSpec((BR, wcols), lambda s, i: (i, s)),
        out_shape=jax.ShapeDtypeStruct((tp, ns * wcols), F32),
        compiler_params=_cparams(("arbitrary", "arbitrary")),
    )(u, w_in_b)


def _mixers_fwd(z, head, x, b_gate, w_dw, b_dw, ln_g, ln_b, pool_scale, w_co, w_pool, w_po, w_o, nb, t_total):
    tp = nb * BR

    def body(avp, av, avn, agp, ag, agn, pp, pc, pn, za, zb, head_ref, x_ref, bg_ref, wdw_ref, bdw_ref,
             lng_ref, lnb_ref, ps_ref, wco_ref, wpool_ref, wpo_ref, wo_ref,
             h1_ref, yc_ref, yp_ref, mg_ref, ca_ref, cpre_ref, m_ref, mw_ref, m2b_ref, ext_ref, pext_ref):
        i = pl.program_id(0)
        _fill_ext(ext_ref, avp[...] * _sigmoid(agp[...]), av[...] * _sigmoid(ag[...]),
                  avn[...] * _sigmoid(agn[...]), i, nb)
        _fill_ext(pext_ref, pp[...], pc[...], pn[...], i, nb)

        def conv_chunk(c, carry):
            lanes = pl.ds(pl.multiple_of(c * 128, 128), 128)
            acc = jnp.broadcast_to(bdw_ref[:, lanes], (BR, 128))
            for k in range(KW):
                acc = acc + wdw_ref[k:k + 1, lanes] * ext_ref[pl.ds(1 + k, BR), lanes]
            cpre_ref[:, lanes] = acc
            return carry
        lax.fori_loop(0, D // 128, conv_chunk, 0)

        conv = cpre_ref[...]
        mu = jnp.mean(conv, axis=-1, keepdims=True)
        xc = conv - mu
        rstd = lax.rsqrt(jnp.mean(xc * xc, axis=-1, keepdims=True) + LN_EPS)
        ln = (xc * rstd) * lng_ref[...] + lnb_ref[...]
        cact = (ln * _sigmoid(ln)).astype(BF16)
        ca_ref[...] = cact
        y_conv = _dot(cact, wco_ref[...])
        yc_ref[...] = y_conv

        t = _row_ids(i, BR)
        for gi, w in enumerate(POOL_WINDOWS):
            left = w // 2
            right = w - 1 - left
            lanes = slice(gi * GD, (gi + 1) * GD)
            s = pext_ref[pl.ds(HALO - left, BR), lanes]
            for j in range(-left + 1, right + 1):
                s = s + pext_ref[pl.ds(HALO + j, BR), lanes]
            m = (s / _pool_cnt(t, w, t_total) - pext_ref[HALO:HALO + BR, lanes]).astype(BF16)
            m_ref[:, lanes] = m
            mw_ref[:, lanes] = _dot(m, wpool_ref[gi])
        mw = mw_ref[...]
        m2b = (mw * ps_ref[...]).astype(BF16)
        m2b_ref[...] = m2b
        y_pool = _dot(m2b, wpo_ref[...])
        yp_ref[...] = y_pool

        s_a = _sigmoid(za[...] + bg_ref[:, 0:D])
        s_b = _sigmoid(zb[...] + bg_ref[:, D:2 * D])
        merged = (s_a * y_conv + s_b * y_pool).astype(BF16)
        mg_ref[...] = merged
        h0 = jnp.where(i == 0, head_ref[...], x_ref[...])
        h1_ref[...] = h0 + _dot(merged, wo_ref[...])

    in_specs = (_halo_specs(0, nb) + _halo_specs(1, nb) + _halo_specs(2, nb)
                + [pl.BlockSpec((BR, D), lambda i: (i, 3)), pl.BlockSpec((BR, D), lambda i: (i, 4)),
                   _const_spec((BR, D)), _x_spec(), _const_spec((1, 2 * D)), _const_spec((32, D)),
                   _const_spec((1, D)), _const_spec((1, D)), _const_spec((1, D)), _const_spec((1, D)),
                   _const_spec((D, D)), _const_spec((4, GD, GD)), _const_spec((D, D)), _const_spec((D, D))])
    outs = [(F32, "h1"), (F32, "yc"), (F32, "yp"), (BF16, "mg"), (BF16, "ca"), (F32, "cpre"), (BF16, "m"), (F32, "mw"),
            (BF16, "m2b")]
    return pl.pallas_call(
        body, name="mixers_fwd",
        grid=(nb,),
        in_specs=in_specs,
        out_specs=[_row_spec() for _ in outs],
        out_shape=[jax.ShapeDtypeStruct((tp, D), dt) for dt, _ in outs],
        scratch_shapes=[pltpu.VMEM((EXT, D), F32), pltpu.VMEM((EXT, D), F32)],
        compiler_params=_cparams(("arbitrary",)),
    )(z, z, z, z, z, z, z, z, z, z, z, head, x, b_gate, w_dw, b_dw, ln_g, ln_b, pool_scale, w_co, w_pool, w_po, w_o)


def _ffn_fwd_bwd(h1, target, g_ffn, g_final, w_g, w_u, w_d, nb):
    tp = nb * BR

    def body(h1_ref, tgt_ref, gf_ref, gfin_ref, wg_hbm, wu_hbm, wd_hbm,
             dh1_ref, dh1b_ref, vb_ref, fb_ref, dgb_ref, dub_ref, dh2b_ref, loss_ref, dgf_ref, dgfin_ref,
             wg_ref, wu_ref, wd_ref, sem):
        i = pl.program_id(0)

        @pl.when(i == 0)
        def _():
            copies = [pltpu.make_async_copy(wg_hbm, wg_ref, sem.at[0]),
                      pltpu.make_async_copy(wu_hbm, wu_ref, sem.at[1]),
                      pltpu.make_async_copy(wd_hbm, wd_ref, sem.at[2])]
            for cp in copies:
                cp.start()
            loss_ref[...] = jnp.zeros_like(loss_ref)
            dgf_ref[...] = jnp.zeros_like(dgf_ref)
            dgfin_ref[...] = jnp.zeros_like(dgfin_ref)
            for cp in copies:
                cp.wait()

        h1 = h1_ref[...]
        r1 = lax.rsqrt(jnp.mean(h1 * h1, axis=-1, keepdims=True) + RMS_EPS)
        vn = h1 * r1
        vb = (vn * gf_ref[...]).astype(BF16)
        vb_ref[...] = vb
        g = _dot(vb, wg_ref[...])
        up = _dot(vb, wu_ref[...])
        sg = _sigmoid(g)
        sl = g * sg
        fb = (sl * up).astype(BF16)
        fb_ref[...] = fb
        h2 = h1 + _dot(fb, wd_ref[...])
        r2 = lax.rsqrt(jnp.mean(h2 * h2, axis=-1, keepdims=True) + RMS_EPS)
        yn = h2 * r2
        valid = i > 0
        diff = jnp.where(valid, yn * gfin_ref[...] - tgt_ref[...], 0.0)
        loss_ref[...] += 0.5 * jnp.sum(jnp.mean(diff * diff, axis=-1, keepdims=True))
        dy = diff * (1.0 / D)
        dgfin_ref[...] += jnp.sum(dy * yn, axis=0, keepdims=True)
        gd = dy * gfin_ref[...]
        dh2 = r2 * (gd - yn * jnp.mean(yn * gd, axis=-1, keepdims=True))
        dh2b = dh2.astype(BF16)
        dh2b_ref[...] = dh2b
        df = _dot_nt(dh2b, wd_ref[...])
        dub = (df * sl).astype(BF16)
        dgb = (df * up * (sg * (1.0 + g * (1.0 - sg)))).astype(BF16)
        dub_ref[...] = dub
        dgb_ref[...] = dgb
        dv = _dot_nt(dgb, wg_ref[...]) + _dot_nt(dub, wu_ref[...])
        dgf_ref[...] += jnp.sum(dv * vn, axis=0, keepdims=True)
        gd1 = dv * gf_ref[...]
        dh1 = dh2 + r1 * (gd1 - vn * jnp.mean(vn * gd1, axis=-1, keepdims=True))
        dh1_ref[...] = dh1
        dh1b_ref[...] = dh1.astype(BF16)

    any_spec = pl.BlockSpec(memory_space=pl.ANY)
    return pl.pallas_call(
        body, name="ffn_fwd_bwd",
        grid=(nb,),
        in_specs=[_row_spec(), _x_spec(), _const_spec((1, D)), _const_spec((1, D)), any_spec, any_spec, any_spec],
        out_specs=[_row_spec(), _row_spec(), _row_spec(), _row_spec(D_FF), _row_spec(D_FF), _row_spec(D_FF), _row_spec(),
                   _const_spec((1, 1)), _const_spec((1, D)), _const_spec((1, D))],
        out_shape=[jax.ShapeDtypeStruct((tp, D), F32), jax.ShapeDtypeStruct((tp, D), BF16),
                   jax.ShapeDtypeStruct((tp, D), BF16), jax.ShapeDtypeStruct((tp, D_FF), BF16),
                   jax.ShapeDtypeStruct((tp, D_FF), BF16), jax.ShapeDtypeStruct((tp, D_FF), BF16),
                   jax.ShapeDtypeStruct((tp, D), BF16), jax.ShapeDtypeStruct((1, 1), F32),
                   jax.ShapeDtypeStruct((1, D), F32), jax.ShapeDtypeStruct((1, D), F32)],
        scratch_shapes=[pltpu.VMEM((D, D_FF), BF16), pltpu.VMEM((D, D_FF), BF16), pltpu.VMEM((D_FF, D), BF16),
                        pltpu.SemaphoreType.DMA((3,))],
        compiler_params=_cparams(("arbitrary",)),
    )(h1, target, g_ffn, g_final, w_g, w_u, w_d)


def _mixers_bwd_rows(dh1b, yc, yp, z, b_gate, cpre, ln_g, ln_b, mw, pool_scale, w_o, w_co, w_po, w_pool, nb):
    tp = nb * BR

    def body(dh1b_ref, yc_ref, yp_ref, za, zb, bg_ref, cpre_ref, lng_ref, lnb_ref, mw_ref, ps_ref,
             wo_ref, wco_ref, wpo_ref, wpool_ref,
             dycb_ref, dypb_ref, dzg_ref, dconv_ref, dmwb_ref, dm_ref, dbg_ref, dlng_ref, dlnb_ref, dbdw_ref, dps_ref):
        i = pl.program_id(0)

        @pl.when(i == 0)
        def _():
            for r in (dbg_ref, dlng_ref, dlnb_ref, dbdw_ref, dps_ref):
                r[...] = jnp.zeros_like(r)

        dmg = _dot_nt(dh1b_ref[...], wo_ref[...])
        s_a = _sigmoid(za[...] + bg_ref[:, 0:D])
        s_b = _sigmoid(zb[...] + bg_ref[:, D:2 * D])
        dycb = (dmg * s_a).astype(BF16)
        dypb = (dmg * s_b).astype(BF16)
        dycb_ref[...] = dycb
        dypb_ref[...] = dypb
        dza = dmg * yc_ref[...] * (s_a * (1.0 - s_a))
        dzb = dmg * yp_ref[...] * (s_b * (1.0 - s_b))
        dzg_ref[:, 0:D] = dza.astype(BF16)
        dzg_ref[:, D:2 * D] = dzb.astype(BF16)
        dbg_ref[:, 0:D] += jnp.sum(dza, axis=0, keepdims=True)
        dbg_ref[:, D:2 * D] += jnp.sum(dzb, axis=0, keepdims=True)

        dca = _dot_nt(dycb, wco_ref[...])
        conv = cpre_ref[...]
        mu = jnp.mean(conv, axis=-1, keepdims=True)
        xc = conv - mu
        rstd = lax.rsqrt(jnp.mean(xc * xc, axis=-1, keepdims=True) + LN_EPS)
        xhat = xc * rstd
        ln = xhat * lng_ref[...] + lnb_ref[...]
        sg = _sigmoid(ln)
        dln = dca * (sg * (1.0 + ln * (1.0 - sg)))
        dlng_ref[...] += jnp.sum(dln * xhat, axis=0, keepdims=True)
        dlnb_ref[...] += jnp.sum(dln, axis=0, keepdims=True)
        dxh = dln * lng_ref[...]
        dconv = rstd * (dxh - jnp.mean(dxh, axis=-1, keepdims=True)
                        - xhat * jnp.mean(dxh * xhat, axis=-1, keepdims=True))
        dconv_ref[...] = dconv
        dbdw_ref[...] += jnp.sum(dconv, axis=0, keepdims=True)

        dm2 = _dot_nt(dypb, wpo_ref[...])
        dps_ref[...] += jnp.sum(dm2 * mw_ref[...], axis=0, keepdims=True)
        dmwb = (dm2 * ps_ref[...]).astype(BF16)
        dmwb_ref[...] = dmwb
        for gi in range(len(POOL_WINDOWS)):
            lanes = slice(gi * GD, (gi + 1) * GD)
            dm_ref[:, lanes] = _dot_nt(dmwb[:, lanes], wpool_ref[gi])

    in_specs = [_row_spec(), _row_spec(), _row_spec(),
                pl.BlockSpec((BR, D), lambda i: (i, 3)), pl.BlockSpec((BR, D), lambda i: (i, 4)),
                _const_spec((1, 2 * D)), _row_spec(), _const_spec((1, D)), _const_spec((1, D)), _row_spec(),
                _const_spec((1, D)), _const_spec((D, D)), _const_spec((D, D)), _const_spec((D, D)),
                _const_spec((4, GD, GD))]
    return pl.pallas_call(
        body, name="mixers_bwd_rows",
        grid=(nb,),
        in_specs=in_specs,
        out_specs=[_row_spec(), _row_spec(), _row_spec(2 * D), _row_spec(), _row_spec(), _row_spec(),
                   _const_spec((1, 2 * D)), _const_spec((1, D)), _const_spec((1, D)), _const_spec((1, D)),
                   _const_spec((1, D))],
        out_shape=[jax.ShapeDtypeStruct((tp, D), BF16), jax.ShapeDtypeStruct((tp, D), BF16),
                   jax.ShapeDtypeStruct((tp, 2 * D), BF16), jax.ShapeDtypeStruct((tp, D), F32),
                   jax.ShapeDtypeStruct((tp, D), BF16), jax.ShapeDtypeStruct((tp, D), F32),
                   jax.ShapeDtypeStruct((1, 2 * D), F32), jax.ShapeDtypeStruct((1, D), F32),
                   jax.ShapeDtypeStruct((1, D), F32), jax.ShapeDtypeStruct((1, D), F32),
                   jax.ShapeDtypeStruct((1, D), F32)],
        compiler_params=_cparams(("arbitrary",)),
    )(dh1b, yc, yp, z, z, b_gate, cpre, ln_g, ln_b, mw, pool_scale, w_o, w_co, w_po, w_pool)


def _mixers_bwd_halo(dconv, dm, z, dzg, w_dw, head, x, g_mix, dh1, w_in_b, nb, t_total):
    tp = nb * BR
    ns = w_in_b.shape[0]
    wcols = w_in_b.shape[2]
    seq = x.shape[0]

    def body(dcp, dcc, dcn, dmp, dmc, dmn, avp, av, avn, agp, ag, agn, dzg_ref, wdw_ref, head_ref, x_ref, g_ref,
             dh1_ref, w_hbm,
             dzb_ref, gx_ref, dhead_ref, dwdw_ref, dgmix_ref,
             w_ref, sem, aext_ref, dext_ref, qext_ref, da_ref):
        i = pl.program_id(0)

        @pl.when(i == 0)
        def _():
            cp = pltpu.make_async_copy(w_hbm, w_ref, sem.at[0])
            cp.start()
            dwdw_ref[...] = jnp.zeros_like(dwdw_ref)
            dgmix_ref[...] = jnp.zeros_like(dgmix_ref)
            cp.wait()

        sig_g = _sigmoid(ag[...])
        _fill_ext(aext_ref, avp[...] * _sigmoid(agp[...]), av[...] * sig_g, avn[...] * _sigmoid(agn[...]), i, nb)
        _fill_ext(dext_ref, dcp[...], dcc[...], dcn[...], i, nb)
        _fill_ext(qext_ref, dmp[...], dmc[...], dmn[...], i, nb)

        def conv_chunk(c, carry):
            lanes = pl.ds(pl.multiple_of(c * 128, 128), 128)
            dcv = dext_ref[HALO:HALO + BR, lanes]
            acc = jnp.zeros((BR, 128), F32)
            for k in range(KW):
                acc = acc + wdw_ref[k:k + 1, lanes] * dext_ref[pl.ds(KW - k, BR), lanes]
                dwdw_ref[k:k + 1, lanes] += jnp.sum(aext_ref[pl.ds(1 + k, BR), lanes] * dcv, axis=0, keepdims=True)
            da_ref[:, lanes] = acc
            return carry
        lax.fori_loop(0, D // 128, conv_chunk, 0)

        da = da_ref[...]
        a_val = av[...]
        dzb_ref[:, 0:D] = (da * sig_g).astype(BF16)
        dzb_ref[:, D:2 * D] = (da * a_val * (sig_g * (1.0 - sig_g))).astype(BF16)

        t_ext = _row_ids(i, EXT, -HALO)
        for gi, w in enumerate(POOL_WINDOWS):
            left = w // 2
            right = w - 1 - left
            lanes = slice(gi * GD, (gi + 1) * GD)
            qext_ref[:, lanes] = qext_ref[:, lanes] / _pool_cnt(t_ext, w, t_total)
            s = qext_ref[pl.ds(HALO - right, BR), lanes]
            for j in range(-right + 1, left + 1):
                s = s + qext_ref[pl.ds(HALO + j, BR), lanes]
            dzb_ref[:, 2 * D + gi * GD:2 * D + (gi + 1) * GD] = (s - dmc[:, lanes]).astype(BF16)
        dzb_ref[:, 3 * D:5 * D] = dzg_ref[...]

        du = _dot_nt(dzb_ref[:, 0:wcols], w_ref[0])
        for s_i in range(1, ns):
            du = du + _dot_nt(dzb_ref[:, s_i * wcols:(s_i + 1) * wcols], w_ref[s_i])
        h0 = jnp.where(i == 0, head_ref[...], x_ref[...])
        r0 = lax.rsqrt(jnp.mean(h0 * h0, axis=-1, keepdims=True) + RMS_EPS)
        un = h0 * r0
        dgmix_ref[...] += jnp.sum(du * un, axis=0, keepdims=True)
        gd = du * g_ref[...]
        dh0 = dh1_ref[...] + r0 * (gd - un * jnp.mean(un * gd, axis=-1, keepdims=True))
        gx_ref[...] = dh0

        @pl.when(i == 0)
        def _():
            dhead_ref[...] = dh0

    any_spec = pl.BlockSpec(memory_space=pl.ANY)
    in_specs = (_halo_specs(0, nb) + _halo_specs(0, nb) + _halo_specs(0, nb) + _halo_specs(1, nb)
                + [_row_spec(2 * D), _const_spec((32, D)), _const_spec((BR, D)), _x_spec(), _const_spec((1, D)),
                   _row_spec(), any_spec])
    return pl.pallas_call(
        body, name="mixers_bwd_halo",
        grid=(nb,),
        in_specs=in_specs,
        out_specs=[_row_spec(D_IN), _x_spec(), _const_spec((BR, D)), _const_spec((32, D)), _const_spec((1, D))],
        out_shape=[jax.ShapeDtypeStruct((tp, D_IN), BF16), jax.ShapeDtypeStruct((seq, D), F32),
                   jax.ShapeDtypeStruct((BR, D), F32), jax.ShapeDtypeStruct((32, D), F32),
                   jax.ShapeDtypeStruct((1, D), F32)],
        scratch_shapes=[pltpu.VMEM((ns, D, wcols), BF16), pltpu.SemaphoreType.DMA((1,)),
                        pltpu.VMEM((EXT, D), F32), pltpu.VMEM((EXT, D), F32), pltpu.VMEM((EXT, D), F32),
                        pltpu.VMEM((BR, D), F32)],
        compiler_params=_cparams(("arbitrary",)),
    )(dconv, dconv, dconv, dm, dm, dm, z, z, z, z, z, z, dzg, w_dw, head, x, g_mix, dh1, w_in_b)


def _wgrad(a, c, tm, tn, tk, name, diag=False, col_major=False):
    tp, m = a.shape
    n = c.shape[1]
    nk = tp // tk
    gm, gn = m // tm, n // tn

    def body(a_ref, c_ref, o_ref, ob_ref):
        k = pl.program_id(2)

        @pl.when(k == 0)
        def _():
            o_ref[...] = jnp.zeros_like(o_ref)

        o_ref[...] += _dot_tn(a_ref[...], c_ref[...])

        @pl.when(k == nk - 1)
        def _():
            ob_ref[...] = o_ref[...].astype(BF16)

    c_map = lambda i, j, k: (k, j)
    grid = (gm, gn, nk)
    if diag:
        grid = (gm, 1, nk)
        c_map = lambda i, j, k: (k, i)
        o_spec = pl.BlockSpec((tm, tn), lambda i, j, k: (i, 0))
        o_shape = (m, tn)
    elif col_major:
        o_spec = pl.BlockSpec((None, tm, tn), lambda i, j, k: (j, i, 0))
        o_shape = (gn, m, tn)
    else:
        o_spec = pl.BlockSpec((tm, tn), lambda i, j, k: (i, j))
        o_shape = (m, n)
    return pl.pallas_call(
        body, name=name,
        grid=grid,
        in_specs=[pl.BlockSpec((tk, tm), lambda i, j, k: (k, i)), pl.BlockSpec((tk, tn), c_map)],
        out_specs=[o_spec, o_spec],
        out_shape=[jax.ShapeDtypeStruct(o_shape, F32), jax.ShapeDtypeStruct(o_shape, BF16)],
        compiler_params=_cparams(("arbitrary", "arbitrary", "arbitrary")),
    )(a, c)


def _local_step(x, target, head, wb, small, nb, tk):
    t_total = x.shape[0] + N_META
    u = _rms_u(head, x, small["g_mix"], nb)
    z = _in_proj(u, wb["w_in"], nb)
    h1, yc, yp, mg, ca, cpre, m, mw, m2b = _mixers_fwd(
        z, head, x, small["b_gate"], small["w_dw"], small["b_dw"], small["ln_g"], small["ln_b"],
        small["pool_scale"], wb["w_conv_out"], wb["w_pool"], wb["w_pool_out"], wb["w_o"], nb, t_total)
    dh1, dh1b, vb, fb, dgb, dub, dh2b, loss, dg_ffn, dg_final = _ffn_fwd_bwd(
        h1, target, small["g_ffn"], small["g_final"], wb["w_ffn_gate"], wb["w_ffn_up"], wb["w_ffn_down"], nb)
    dycb, dypb, dzg, dconv, dmwb, dm, db_gate, dln_g, dln_b, db_dw, dps = _mixers_bwd_rows(
        dh1b, yc, yp, z, small["b_gate"], cpre, small["ln_g"], small["ln_b"], mw, small["pool_scale"],
        wb["w_o"], wb["w_conv_out"], wb["w_pool_out"], wb["w_pool"], nb)
    dzb, grad_x, dhead, dw_dw, dg_mix = _mixers_bwd_halo(
        dconv, dm, z, dzg, small["w_dw"], head, x, small["g_mix"], dh1, wb["w_in"], nb, t_total)

    half_ff = D_FF // 2
    grads = {
        "w_in": _wgrad(u, dzb, D, D_IN // N_SHARD, tk, "wgrad_in", col_major=True),
        "w_conv_out": _wgrad(ca, dycb, D, D, tk, "wgrad_conv_out"),
        "w_pool": _wgrad(m, dmwb, GD, GD, tk, "wgrad_pool", diag=True),
        "w_pool_out": _wgrad(m2b, dypb, D, D, tk, "wgrad_pool_out"),
        "w_o": _wgrad(mg, dh1b, D, D, tk, "wgrad_o"),
        "w_ffn_gate": _wgrad(vb, dgb, D, half_ff, tk, "wgrad_ffn_gate"),
        "w_ffn_up": _wgrad(vb, dub, D, half_ff, tk, "wgrad_ffn_up"),
        "w_ffn_down": _wgrad(fb, dh2b, half_ff, D, tk, "wgrad_ffn_down"),
    }
    small_grads = {"g_mix": dg_mix, "b_gate": db_gate, "w_dw": dw_dw, "b_dw": db_dw, "ln_g": dln_g, "ln_b": dln_b,
                   "pool_scale": dps, "g_ffn": dg_ffn, "g_final": dg_final}
    return loss, grad_x, dhead, grads, small_grads


def _place():
    x, y, c = lax.axis_index("x"), lax.axis_index("y"), lax.axis_index("c")
    others = [(1 - x, y), (x, 1 - y), (1 - x, 1 - y)]
    return x, y, c, others


def _rows_half(ref, h):
    r = ref.shape[0] // 2
    return ref.at[pl.ds(h * r, r)]


def _gather_weights(shards, tiny):
    n, nt = len(shards), len(tiny)
    any_spec = pl.BlockSpec(memory_space=pl.ANY)

    def body(*refs):
        ins, tins = refs[:n], refs[n:n + nt]
        outs, touts = refs[n + nt:2 * n + nt], refs[2 * n + nt:2 * (n + nt)]
        send_sems, recv_sems, local_sems = refs[2 * (n + nt):]
        x, y, c, others = _place()
        mine = 2 * x + y
        sibling = (x, y, 1 - c)

        def remote(src, dst, k, to):
            return pltpu.make_async_remote_copy(src_ref=src, dst_ref=dst, send_sem=send_sems.at[k],
                                                recv_sem=recv_sems.at[k], device_id=to, device_id_type=MESH)

        local = [pltpu.make_async_copy(ins[a], outs[a].at[mine], local_sems.at[a]) for a in range(n)]
        local += [pltpu.make_async_copy(tins[a], touts[a].at[mine], local_sems.at[n + a]) for a in range(nt)]
        for cp in local:
            cp.start()
        sends = []
        for a in range(n):
            for j, chip in enumerate(others):
                sends.append(remote(_rows_half(ins[a], c), _rows_half(outs[a].at[mine], c), a * 3 + j, (*chip, c)))
        for a in range(nt):
            for j, chip in enumerate(others):
                sends.append(remote(tins[a], touts[a].at[mine], 6 * n + a * 3 + j, (*chip, c)))
        for cp in sends:
            cp.start()
        passed = []
        for a in range(n):
            for j, chip in enumerate(others):
                landed = _rows_half(outs[a].at[2 * chip[0] + chip[1]], c)
                remote(landed, landed, a * 3 + j, (x, y, c)).wait_recv()
                fwd = remote(landed, landed, 3 * n + a * 3 + j, sibling)
                fwd.start()
                passed.append(fwd)
        for a in range(n):
            for j, chip in enumerate(others):
                landed = _rows_half(outs[a].at[2 * chip[0] + chip[1]], 1 - c)
                remote(landed, landed, 3 * n + a * 3 + j, (x, y, c)).wait_recv()
        for a in range(nt):
            for j, chip in enumerate(others):
                landed = touts[a].at[2 * chip[0] + chip[1]]
                remote(landed, landed, 6 * n + a * 3 + j, (x, y, c)).wait_recv()
        for cp in sends + passed:
            cp.wait_send()
        for cp in local:
            cp.wait()

    nsem = 6 * n + 3 * nt
    return pl.pallas_call(
        body, name="gather_weights",
        in_specs=[any_spec] * (n + nt),
        out_specs=[any_spec] * (n + nt),
        out_shape=[jax.ShapeDtypeStruct((N_SHARD,) + a.shape, a.dtype) for a in list(shards) + list(tiny)],
        scratch_shapes=[pltpu.SemaphoreType.DMA((nsem,)), pltpu.SemaphoreType.DMA((nsem,)),
                        pltpu.SemaphoreType.DMA((n + nt,))],
    )(*shards, *tiny)


def _swap_halves_bf16(gbs):
    n = len(gbs)
    any_spec = pl.BlockSpec(memory_space=pl.ANY)

    def body(*refs):
        ins, outs = refs[:n], refs[n:2 * n]
        send_sems, recv_sems = refs[2 * n:]
        x, y, c, _ = _place()
        copies = []
        for a in range(n):
            r2 = ins[a].shape[1] // 2
            src = ins[a].at[:, pl.ds((1 - c) * r2, r2), :]
            copies.append(pltpu.make_async_remote_copy(
                src_ref=src, dst_ref=outs[a], send_sem=send_sems.at[a], recv_sem=recv_sems.at[a],
                device_id=(x, y, 1 - c), device_id_type=MESH))
        for cp in copies:
            cp.start()
        for cp in copies:
            cp.wait()

    return pl.pallas_call(
        body, name="swap_halves_bf16",
        in_specs=[any_spec] * n, out_specs=[any_spec] * n,
        out_shape=[jax.ShapeDtypeStruct((g.shape[0], g.shape[1] // 2, g.shape[2]), g.dtype) for g in gbs],
        scratch_shapes=[pltpu.SemaphoreType.DMA((n,)), pltpu.SemaphoreType.DMA((n,))],
    )(*gbs)


def _scatter_slabs(hbs):
    n = len(hbs)
    any_spec = pl.BlockSpec(memory_space=pl.ANY)

    def body(*refs):
        ins, outs = refs[:n], refs[n:2 * n]
        send_sems, recv_sems = refs[2 * n:]
        x, y, c, others = _place()
        mine = 2 * x + y
        copies = []
        for a in range(n):
            for j, chip in enumerate(others):
                copies.append(pltpu.make_async_remote_copy(
                    src_ref=ins[a].at[2 * chip[0] + chip[1]], dst_ref=outs[a].at[mine],
                    send_sem=send_sems.at[a * 3 + j], recv_sem=recv_sems.at[a * 3 + j],
                    device_id=(*chip, c), device_id_type=MESH))
        for cp in copies:
            cp.start()
        for a in range(n):
            for j, chip in enumerate(others):
                landed = outs[a].at[2 * chip[0] + chip[1]]
                pltpu.make_async_remote_copy(
                    src_ref=landed, dst_ref=landed, send_sem=send_sems.at[a * 3 + j], recv_sem=recv_sems.at[a * 3 + j],
                    device_id=(x, y, c), device_id_type=MESH).wait_recv()
        for cp in copies:
            cp.wait_send()

    return pl.pallas_call(
        body, name="scatter_slabs",
        in_specs=[any_spec] * n, out_specs=[any_spec] * n,
        out_shape=[jax.ShapeDtypeStruct(h.shape, h.dtype) for h in hbs],
        scratch_shapes=[pltpu.SemaphoreType.DMA((3 * n,)), pltpu.SemaphoreType.DMA((3 * n,))],
    )(*hbs)


def _join_halves(rhs):
    n = len(rhs)
    any_spec = pl.BlockSpec(memory_space=pl.ANY)

    def body(*refs):
        ins, outs = refs[:n], refs[n:2 * n]
        send_sems, recv_sems, local_sems = refs[2 * n:]
        x, y, c, _ = _place()
        copies, local = [], []
        for a in range(n):
            r2 = ins[a].shape[0]
            local.append(pltpu.make_async_copy(ins[a], outs[a].at[pl.ds(c * r2, r2)], local_sems.at[a]))
            copies.append(pltpu.make_async_remote_copy(
                src_ref=ins[a], dst_ref=outs[a].at[pl.ds(c * r2, r2)], send_sem=send_sems.at[a],
                recv_sem=recv_sems.at[a], device_id=(x, y, 1 - c), device_id_type=MESH))
        for cp in local + copies:
            cp.start()
        for a in range(n):
            r2 = ins[a].shape[0]
            landed = outs[a].at[pl.ds((1 - c) * r2, r2)]
            pltpu.make_async_remote_copy(
                src_ref=landed, dst_ref=landed, send_sem=send_sems.at[a], recv_sem=recv_sems.at[a],
                device_id=(x, y, c), device_id_type=MESH).wait_recv()
        for cp in copies:
            cp.wait_send()
        for cp in local:
            cp.wait()

    return pl.pallas_call(
        body, name="join_halves",
        in_specs=[any_spec] * n, out_specs=[any_spec] * n,
        out_shape=[jax.ShapeDtypeStruct((2 * r.shape[0], r.shape[1]), r.dtype) for r in rhs],
        scratch_shapes=[pltpu.SemaphoreType.DMA((n,)), pltpu.SemaphoreType.DMA((n,)), pltpu.SemaphoreType.DMA((n,))],
    )(*rhs)


def _allreduce_small(v):
    rows, cols = v.shape
    vm = pl.BlockSpec(memory_space=pltpu.VMEM)
    flips = [(dx, dy, dc) for dx in (0, 1) for dy in (0, 1) for dc in (0, 1)][1:]

    def body(v_ref, out_ref, buf_ref, send_sems, recv_sems):
        x, y, c, _ = _place()
        mine = 4 * x + 2 * y + c
        copies = []
        for k, (dx, dy, dc) in enumerate(flips):
            px, py, pc = jnp.bitwise_xor(x, dx), jnp.bitwise_xor(y, dy), jnp.bitwise_xor(c, dc)
            copies.append(pltpu.make_async_remote_copy(
                src_ref=v_ref, dst_ref=buf_ref.at[mine], send_sem=send_sems.at[k], recv_sem=recv_sems.at[k],
                device_id=(px, py, pc), device_id_type=MESH))
        for cp in copies:
            cp.start()
        buf_ref[mine] = v_ref[...]
        for k, (dx, dy, dc) in enumerate(flips):
            src = 4 * jnp.bitwise_xor(x, dx) + 2 * jnp.bitwise_xor(y, dy) + jnp.bitwise_xor(c, dc)
            pltpu.make_async_remote_copy(
                src_ref=v_ref, dst_ref=buf_ref.at[src], send_sem=send_sems.at[k], recv_sem=recv_sems.at[k],
                device_id=(x, y, c), device_id_type=MESH).wait_recv()
        for cp in copies:
            cp.wait_send()
        acc = buf_ref[0]
        for d in range(1, 8):
            acc = acc + buf_ref[d]
        out_ref[...] = acc

    return pl.pallas_call(
        body, name="allreduce_small",
        in_specs=[vm], out_specs=vm,
        out_shape=jax.ShapeDtypeStruct((rows, cols), F32),
        scratch_shapes=[pltpu.VMEM((8, rows, cols), F32), pltpu.SemaphoreType.DMA((7,)), pltpu.SemaphoreType.DMA((7,))],
    )(v)


def _row_block(r):
    for cand in (512, 352, 256, 128, 64, 48, 16):
        if r % cand == 0:
            return cand
    return r


def _add_sibling_half(g, sb, core, name):
    ns, r, c = g.shape
    r2 = r // 2

    def body(core_ref, g_ref, sb_ref, h_ref, hb_ref):
        h = g_ref[...] + sb_ref[...].astype(F32)
        h_ref[...] = h
        hb_ref[...] = h.astype(BF16)

    spec = pl.BlockSpec((None, r2, c), lambda s, core_ref: (s, 0, 0))
    return pl.pallas_call(
        body, name=name,
        grid_spec=pltpu.PrefetchScalarGridSpec(
            num_scalar_prefetch=1, grid=(ns,),
            in_specs=[pl.BlockSpec((None, r2, c), lambda s, core_ref: (s, core_ref[0], 0)), spec],
            out_specs=[spec, spec]),
        out_shape=[jax.ShapeDtypeStruct((ns, r2, c), F32), jax.ShapeDtypeStruct((ns, r2, c), BF16)],
        compiler_params=_cparams(("arbitrary",)),
    )(core, g, sb)


def _add_chip_slabs(h, rb, idx, name):
    ns, r2, c = h.shape

    def body(idx_ref, h_ref, r0_ref, r1_ref, r2_ref, o_ref):
        o_ref[...] = ((h_ref[...] + r0_ref[...].astype(F32)) + r1_ref[...].astype(F32)) + r2_ref[...].astype(F32)

    def pick(k):
        return pl.BlockSpec((None, r2, c), lambda i, idx_ref: (idx_ref[k], 0, 0))

    return pl.pallas_call(
        body, name=name,
        grid_spec=pltpu.PrefetchScalarGridSpec(
            num_scalar_prefetch=1, grid=(1,),
            in_specs=[pick(0), pick(1), pick(2), pick(3)],
            out_specs=pl.BlockSpec((r2, c), lambda i, idx_ref: (0, 0))),
        out_shape=jax.ShapeDtypeStruct((r2, c), F32),
        compiler_params=_cparams(("arbitrary",)),
    )(idx, h, rb, rb, rb)


def _adamw(g, w, m, v, name):
    r, c = g.shape
    br = _row_block(r)

    def body(g_ref, w_ref, m_ref, v_ref, d_ref, nm_ref, nv_ref):
        gg = g_ref[...]
        nm = B1 * m_ref[...] + (1.0 - B1) * gg
        nv = B2 * v_ref[...] + (1.0 - B2) * jnp.square(gg)
        m_hat = nm / (1.0 - B1 ** STEP)
        v_hat = nv / (1.0 - B2 ** STEP)
        d_ref[...] = -LR * (m_hat / (jnp.sqrt(v_hat) + ADAM_EPS) + WD * w_ref[...])
        nm_ref[...] = nm
        nv_ref[...] = nv

    spec = pl.BlockSpec((br, c), lambda i: (i, 0))
    return pl.pallas_call(
        body, name=name,
        grid=(r // br,),
        in_specs=[spec] * 4, out_specs=[spec] * 3,
        out_shape=[jax.ShapeDtypeStruct((r, c), F32)] * 3,
        compiler_params=_cparams(("arbitrary",)),
    )(g, w, m, v)


BIG = ("w_in", "w_conv_out", "w_pool", "w_pool_out", "w_o", "w_ffn_gate", "w_ffn_up", "w_ffn_down")
REPL = ("g_mix", "b_gate", "b_dw", "ln_g", "ln_b", "pool_scale", "g_ffn", "g_final")
WEIGHT_ORDER = ("meta_tokens", "g_mix", "w_in", "b_gate", "w_dw", "b_dw", "ln_g", "ln_b", "w_conv_out", "w_pool",
                "pool_scale", "w_pool_out", "w_o", "g_ffn", "w_ffn_gate", "w_ffn_up", "w_ffn_down", "g_final")


def _shard2d(name, a):
    a = a[0]
    if name == "w_pool":
        return a.reshape(4 * 64, GD)
    return a


def _cols_to_slabs(a):
    m, n = a.shape
    return a.reshape(m, N_SHARD, n // N_SHARD).transpose(1, 0, 2)


def _slabs_to_cols(a):
    ns, m, c = a.shape
    return a.transpose(1, 0, 2).reshape(m, ns * c)


def kernel(x, meta_tokens, g_mix, w_in, b_gate, w_dw, b_dw, ln_g, ln_b, w_conv_out, w_pool, pool_scale, w_pool_out, w_o, g_ffn, w_ffn_gate, w_ffn_up, w_ffn_down, g_final, loss_target, m_meta_tokens, m_g_mix, m_w_in, m_b_gate, m_w_dw, m_b_dw, m_ln_g, m_ln_b, m_w_conv_out, m_w_pool, m_pool_scale, m_w_pool_out, m_w_o, m_g_ffn, m_w_ffn_gate, m_w_ffn_up, m_w_ffn_down, m_g_final, v_meta_tokens, v_g_mix, v_w_in, v_b_gate, v_w_dw, v_b_dw, v_ln_g, v_ln_b, v_w_conv_out, v_w_pool, v_pool_scale, v_w_pool_out, v_w_o, v_g_ffn, v_w_ffn_gate, v_w_ffn_up, v_w_ffn_down, v_g_final):
    args = dict(locals())
    w = {n: args[n] for n in WEIGHT_ORDER}
    mom = {n: args["m_" + n] for n in WEIGHT_ORDER}
    var = {n: args["v_" + n] for n in WEIGHT_ORDER}
    seq = x.shape[1]
    nb = seq // BR + 1
    cx, cy, cc = lax.axis_index("x"), lax.axis_index("y"), lax.axis_index("c")
    chip = 2 * cx + cy

    shards = [_shard2d(n, w[n]).astype(BF16) if n != "w_pool" else w[n][0].astype(BF16) for n in BIG]
    gathered = _gather_weights(shards, [w["w_dw"][0], w["meta_tokens"]])
    gw = dict(zip(BIG, gathered[:len(BIG)]))
    w_dw_full = _slabs_to_cols(gathered[len(BIG)])
    meta_full = _slabs_to_cols(gathered[len(BIG) + 1])
    wb = {
        "w_in": gw["w_in"],
        "w_conv_out": gw["w_conv_out"].reshape(D, D),
        "w_pool": gw["w_pool"].transpose(1, 0, 2, 3).reshape(4, GD, GD),
        "w_pool_out": gw["w_pool_out"].reshape(D, D),
        "w_o": gw["w_o"].reshape(D, D),
        "w_ffn_gate": _slabs_to_cols(gw["w_ffn_gate"]),
        "w_ffn_up": _slabs_to_cols(gw["w_ffn_up"]),
        "w_ffn_down": gw["w_ffn_down"].reshape(D_FF, D),
    }
    small = {n: w[n] for n in REPL if n != "g_final"}
    small["g_final"] = w["g_final"].reshape(1, D)
    small["w_dw"] = jnp.pad(w_dw_full, ((0, 1), (0, 0)))
    head = jnp.concatenate([jnp.zeros((PAD, D), F32), meta_full], axis=0)

    loss, grad_x, dhead, grads, sgrads = _local_step(x[0], loss_target[0], head, wb, small, nb, tk=nb * BR // 4 if (nb * BR // 4) % 16 == 0 else BR)
    loss = lax.psum(loss[0, 0], ("x", "y", "c"))

    def slabs(name, g):
        if name == "w_in":
            return g
        if name in ("w_ffn_gate", "w_ffn_up"):
            return _cols_to_slabs(g)
        if name == "w_pool":
            return g.reshape(4, N_SHARD, 64, GD).transpose(1, 0, 2, 3).reshape(N_SHARD, 4 * 64, GD)
        return g.reshape(N_SHARD, g.shape[0] // N_SHARD, g.shape[1])

    g32 = [slabs(n, grads[n][0]) for n in BIG]
    g16 = [slabs(n, grads[n][1]) for n in BIG]
    from_sibling = _swap_halves_bf16(g16)
    core = jnp.reshape(cc, (1,)).astype(jnp.int32)
    halves = [_add_sibling_half(g, sb, core, "add_sibling_" + n) for n, g, sb in zip(BIG, g32, from_sibling)]
    from_chips = _scatter_slabs([hb for _, hb in halves])
    others = jnp.sort(jnp.stack([2 * (1 - cx) + cy, 2 * cx + (1 - cy), 2 * (1 - cx) + (1 - cy)]))
    idx = jnp.concatenate([jnp.reshape(chip, (1,)), others]).astype(jnp.int32)
    reduced_half = [_add_chip_slabs(h, rb, idx, "add_chips_" + n) for n, (h, _), rb in zip(BIG, halves, from_chips)]
    reduced = dict(zip(BIG, _join_halves(reduced_half)))

    packed = jnp.concatenate(
        [sgrads["g_mix"], sgrads["b_gate"].reshape(2, D), sgrads["b_dw"], sgrads["ln_g"], sgrads["ln_b"],
         sgrads["pool_scale"], sgrads["g_ffn"], sgrads["g_final"], jnp.zeros((7, D), F32),
         dhead[PAD:], sgrads["w_dw"]], axis=0)
    summed = _allreduce_small(packed)

    def repl_stack(d):
        return jnp.concatenate([d["g_mix"], d["b_gate"].reshape(2, D), d["b_dw"], d["ln_g"], d["ln_b"],
                                d["pool_scale"], d["g_ffn"], d["g_final"].reshape(1, D), jnp.ones((7, D), F32)], axis=0)

    def shard_stack(d):
        return jnp.concatenate([d["meta_tokens"], d["w_dw"][0], jnp.ones((1, GD), F32)], axis=0)

    g_repl = summed[0:16]
    g_shard = lax.dynamic_slice_in_dim(summed[16:64], chip * GD, GD, axis=1)
    d_repl, m_repl, v_repl = _adamw(g_repl, repl_stack(w), repl_stack(mom), repl_stack(var), "adamw_repl")
    d_shard, m_shard, v_shard = _adamw(g_shard, shard_stack(w), shard_stack(mom), shard_stack(var), "adamw_cols")

    def unpack(name, repl, shard):
        if name == "meta_tokens":
            return shard[0:N_META]
        if name == "w_dw":
            return shard[N_META:N_META + KW].reshape(1, KW, GD)
        row = {"g_mix": 0, "b_gate": 1, "b_dw": 3, "ln_g": 4, "ln_b": 5, "pool_scale": 6, "g_ffn": 7, "g_final": 8}[name]
        if name == "b_gate":
            return repl[1:3].reshape(1, 2 * D)
        if name == "g_final":
            return repl[8]
        return repl[row:row + 1]

    out_g, out_d, out_m, out_v = {}, {}, {}, {}
    for n in WEIGHT_ORDER:
        if n in BIG:
            g = reduced[n]
            d_, m_, v_ = _adamw(g, _shard2d(n, w[n]), _shard2d(n, mom[n]), _shard2d(n, var[n]), "adamw_" + n)
            shape = w[n].shape
            out_g[n], out_d[n], out_m[n], out_v[n] = (a.reshape(shape) for a in (g, d_, m_, v_))
        else:
            out_g[n] = unpack(n, g_repl, g_shard)
            out_d[n] = unpack(n, d_repl, d_shard)
            out_m[n] = unpack(n, m_repl, m_shard)
            out_v[n] = unpack(n, v_repl, v_shard)
    return (loss, grad_x[None], *[out_g[n] for n in WEIGHT_ORDER], *[out_d[n] for n in WEIGHT_ORDER],
            *[out_m[n] for n in WEIGHT_ORDER], *[out_v[n] for n in WEIGHT_ORDER])
```

```python
import functools

import jax
import jax.numpy as jnp
from jax import lax
from jax.experimental import pallas as pl
from jax.experimental.pallas import tpu as pltpu

F32 = jnp.float32
BF16 = jnp.bfloat16
MESH = pl.DeviceIdType.MESH

D = 1024
N_META = 16
KW = 31
CPAD = KW // 2
POOL_WINDOWS = (2, 4, 8, 16)
GD = 256
D_IN = 5 * D
D_FF = 2816
N_SHARD = 4
BR = 256
HALO = 16
PAD = BR - N_META
EXT = BR + 2 * HALO
RMS_EPS = 1e-6
LN_EPS = 1e-5
LR, B1, B2, ADAM_EPS, WD, STEP = 0.001, 0.9, 0.999, 1e-08, 0.01, 10
VMEM_LIMIT = 56 * 1024 * 1024


def _cparams(sem, vmem=VMEM_LIMIT):
    return pltpu.CompilerParams(dimension_semantics=sem, vmem_limit_bytes=vmem)


def _dot(a, b):
    return jnp.dot(a, b, preferred_element_type=F32)


def _dot_nt(a, b):
    return lax.dot_general(a, b, (((1,), (1,)), ((), ())), preferred_element_type=F32)


def _dot_tn(a, b):
    return lax.dot_general(a, b, (((0,), (0,)), ((), ())), preferred_element_type=F32)


def _sigmoid(x):
    return 1.0 / (1.0 + jnp.exp(-x))


def _row_ids(i, n, offset=0):
    return lax.broadcasted_iota(jnp.int32, (n, 1), 0) + (i * BR + offset - PAD)


def _pool_cnt(t, w, t_total):
    left = w // 2
    right = w - 1 - left
    lo = jnp.clip(t - left, 0, t_total)
    hi = jnp.clip(t + right + 1, 0, t_total)
    return jnp.maximum(hi - lo, 1).astype(F32)


def _halo_specs(col, nb):
    last = nb * (BR // HALO) - 1
    return [
        pl.BlockSpec((HALO, D), lambda i: (jnp.maximum(i * (BR // HALO) - 1, 0), col)),
        pl.BlockSpec((BR, D), lambda i: (i, col)),
        pl.BlockSpec((HALO, D), lambda i: (jnp.minimum((i + 1) * (BR // HALO), last), col)),
    ]


def _fill_ext(ext_ref, prev, cur, nxt, i, nb):
    ext_ref[0:HALO, :] = jnp.where(i > 0, prev, 0.0)
    ext_ref[HALO:HALO + BR, :] = cur
    ext_ref[HALO + BR:EXT, :] = jnp.where(i < nb - 1, nxt, 0.0)


ROT_ROWS = EXT - 8


def _fill_rot(rot_ref, ext_ref, lanes):
    for r in range(1, 8):
        rot_ref[r] = ext_ref[pl.ds(r, ROT_ROWS), lanes]


def _tap(rot_ref, ext_ref, lanes, offset):
    q, r = divmod(offset, 8)
    if r == 0:
        return ext_ref[pl.ds(8 * q, BR), lanes]
    return rot_ref[r, pl.ds(8 * q, BR), :]


def _row_spec(width=D):
    return pl.BlockSpec((BR, width), lambda i: (i, 0))


def _x_spec():
    return pl.BlockSpec((BR, D), lambda i: (jnp.maximum(i - 1, 0), 0))


def _const_spec(shape):
    nd = len(shape)
    return pl.BlockSpec(shape, lambda i: (0,) * nd)


def _rms_u(head, x, g_mix, nb):
    def body(head_ref, x_ref, g_ref, u_ref):
        i = pl.program_id(0)
        h = jnp.where(i == 0, head_ref[...], x_ref[...])
        r = lax.rsqrt(jnp.mean(h * h, axis=-1, keepdims=True) + RMS_EPS)
        u_ref[...] = ((h * r) * g_ref[...]).astype(BF16)

    return pl.pallas_call(
        body, name="rms_u",
        grid=(nb,),
        in_specs=[_const_spec((BR, D)), _x_spec(), _const_spec((1, D))],
        out_specs=_row_spec(),
        out_shape=jax.ShapeDtypeStruct((nb * BR, D), BF16),
        compiler_params=_cparams(("arbitrary",)),
    )(head, x, g_mix)


def _in_proj(u, w_in_b, nb):
    tp = nb * BR
    ns = w_in_b.shape[0]
    wcols = w_in_b.shape[2]

    def body(u_ref, w_ref, z_ref):
        z_ref[...] = _dot(u_ref[...], w_ref[...])

    return pl.pallas_call(
        body, name="in_proj",
        grid=(ns, nb),
        in_specs=[
            pl.BlockSpec((BR, D), lambda s, i: (i, 0)),
            pl.BlockSpec((None, D, wcols), lambda s, i: (s, 0, 0)),
        ],
        out_specs=pl.BlockSpec((BR, wcols), lambda s, i: (i, s)),
        out_shape=jax.ShapeDtypeStruct((tp, ns * wcols), F32),
        compiler_params=_cparams(("arbitrary", "arbitrary")),
    )(u, w_in_b)


def _mixers_fwd(z, head, x, b_gate, w_dw, b_dw, ln_g, ln_b, pool_scale, w_co, w_pool, w_po, w_o, nb, t_total):
    tp = nb * BR

    def body(avp, av, avn, agp, ag, agn, pp, pc, pn, za, zb, head_ref, x_ref, bg_ref, wdw_ref, bdw_ref,
             lng_ref, lnb_ref, ps_ref, wco_ref, wpool_ref, wpo_ref, wo_ref,
             h1_ref, yc_ref, yp_ref, mg_ref, ca_ref, cpre_ref, m_ref, mw_ref, m2b_ref, ext_ref, pext_ref, rot_ref):
        i = pl.program_id(0)
        _fill_ext(ext_ref, avp[...] * _sigmoid(agp[...]), av[...] * _sigmoid(ag[...]),
                  avn[...] * _sigmoid(agn[...]), i, nb)
        _fill_ext(pext_ref, pp[...], pc[...], pn[...], i, nb)

        def conv_chunk(c, carry):
            lanes = pl.ds(pl.multiple_of(c * 128, 128), 128)
            _fill_rot(rot_ref, ext_ref, lanes)
            acc = jnp.broadcast_to(bdw_ref[:, lanes], (BR, 128))
            for k in range(KW):
                acc = acc + wdw_ref[k:k + 1, lanes] * _tap(rot_ref, ext_ref, lanes, 1 + k)
            cpre_ref[:, lanes] = acc
            return carry
        lax.fori_loop(0, D // 128, conv_chunk, 0)

        conv = cpre_ref[...]
        mu = jnp.mean(conv, axis=-1, keepdims=True)
        xc = conv - mu
        rstd = lax.rsqrt(jnp.mean(xc * xc, axis=-1, keepdims=True) + LN_EPS)
        ln = (xc * rstd) * lng_ref[...] + lnb_ref[...]
        cact = (ln * _sigmoid(ln)).astype(BF16)
        ca_ref[...] = cact
        y_conv = _dot(cact, wco_ref[...])
        yc_ref[...] = y_conv

        t = _row_ids(i, BR)
        for gi, w in enumerate(POOL_WINDOWS):
            left = w // 2
            right = w - 1 - left
            lanes = slice(gi * GD, (gi + 1) * GD)
            s = pext_ref[pl.ds(HALO - left, BR), lanes]
            for j in range(-left + 1, right + 1):
                s = s + pext_ref[pl.ds(HALO + j, BR), lanes]
            m = (s / _pool_cnt(t, w, t_total) - pext_ref[HALO:HALO + BR, lanes]).astype(BF16)
            m_ref[:, lanes] = m
            mw_ref[:, lanes] = _dot(m, wpool_ref[gi])
        mw = mw_ref[...]
        m2b = (mw * ps_ref[...]).astype(BF16)
        m2b_ref[...] = m2b
        y_pool = _dot(m2b, wpo_ref[...])
        yp_ref[...] = y_pool

        s_a = _sigmoid(za[...] + bg_ref[:, 0:D])
        s_b = _sigmoid(zb[...] + bg_ref[:, D:2 * D])
        merged = (s_a * y_conv + s_b * y_pool).astype(BF16)
        mg_ref[...] = merged
        h0 = jnp.where(i == 0, head_ref[...], x_ref[...])
        h1_ref[...] = h0 + _dot(merged, wo_ref[...])

    in_specs = (_halo_specs(0, nb) + _halo_specs(1, nb) + _halo_specs(2, nb)
                + [pl.BlockSpec((BR, D), lambda i: (i, 3)), pl.BlockSpec((BR, D), lambda i: (i, 4)),
                   _const_spec((BR, D)), _x_spec(), _const_spec((1, 2 * D)), _const_spec((32, D)),
                   _const_spec((1, D)), _const_spec((1, D)), _const_spec((1, D)), _const_spec((1, D)),
                   _const_spec((D, D)), _const_spec((4, GD, GD)), _const_spec((D, D)), _const_spec((D, D))])
    outs = [(F32, "h1"), (F32, "yc"), (F32, "yp"), (BF16, "mg"), (BF16, "ca"), (F32, "cpre"), (BF16, "m"), (F32, "mw"),
            (BF16, "m2b")]
    return pl.pallas_call(
        body, name="mixers_fwd",
        grid=(nb,),
        in_specs=in_specs,
        out_specs=[_row_spec() for _ in outs],
        out_shape=[jax.ShapeDtypeStruct((tp, D), dt) for dt, _ in outs],
        scratch_shapes=[pltpu.VMEM((EXT, D), F32), pltpu.VMEM((EXT, D), F32), pltpu.VMEM((8, ROT_ROWS, 128), F32)],
        compiler_params=_cparams(("arbitrary",)),
    )(z, z, z, z, z, z, z, z, z, z, z, head, x, b_gate, w_dw, b_dw, ln_g, ln_b, pool_scale, w_co, w_pool, w_po, w_o)


def _ffn_fwd_bwd(h1, target, g_ffn, g_final, w_g, w_u, w_d, nb):
    tp = nb * BR

    def body(h1_ref, tgt_ref, gf_ref, gfin_ref, wg_hbm, wu_hbm, wd_hbm,
             dh1_ref, dh1b_ref, vb_ref, fb_ref, dgb_ref, dub_ref, dh2b_ref, loss_ref, dgf_ref, dgfin_ref,
             wg_ref, wu_ref, wd_ref, sem):
        i = pl.program_id(0)

        @pl.when(i == 0)
        def _():
            copies = [pltpu.make_async_copy(wg_hbm, wg_ref, sem.at[0]),
                      pltpu.make_async_copy(wu_hbm, wu_ref, sem.at[1]),
                      pltpu.make_async_copy(wd_hbm, wd_ref, sem.at[2])]
            for cp in copies:
                cp.start()
            loss_ref[...] = jnp.zeros_like(loss_ref)
            dgf_ref[...] = jnp.zeros_like(dgf_ref)
            dgfin_ref[...] = jnp.zeros_like(dgfin_ref)
            for cp in copies:
                cp.wait()

        h1 = h1_ref[...]
        r1 = lax.rsqrt(jnp.mean(h1 * h1, axis=-1, keepdims=True) + RMS_EPS)
        vn = h1 * r1
        vb = (vn * gf_ref[...]).astype(BF16)
        vb_ref[...] = vb
        g = _dot(vb, wg_ref[...])
        up = _dot(vb, wu_ref[...])
        sg = _sigmoid(g)
        sl = g * sg
        fb = (sl * up).astype(BF16)
        fb_ref[...] = fb
        h2 = h1 + _dot(fb, wd_ref[...])
        r2 = lax.rsqrt(jnp.mean(h2 * h2, axis=-1, keepdims=True) + RMS_EPS)
        yn = h2 * r2
        valid = i > 0
        diff = jnp.where(valid, yn * gfin_ref[...] - tgt_ref[...], 0.0)
        loss_ref[...] += 0.5 * jnp.sum(jnp.mean(diff * diff, axis=-1, keepdims=True))
        dy = diff * (1.0 / D)
        dgfin_ref[...] += jnp.sum(dy * yn, axis=0, keepdims=True)
        gd = dy * gfin_ref[...]
        dh2 = r2 * (gd - yn * jnp.mean(yn * gd, axis=-1, keepdims=True))
        dh2b = dh2.astype(BF16)
        dh2b_ref[...] = dh2b
        df = _dot_nt(dh2b, wd_ref[...])
        dub = (df * sl).astype(BF16)
        dgb = (df * up * (sg * (1.0 + g * (1.0 - sg)))).astype(BF16)
        dub_ref[...] = dub
        dgb_ref[...] = dgb
        dv = _dot_nt(dgb, wg_ref[...]) + _dot_nt(dub, wu_ref[...])
        dgf_ref[...] += jnp.sum(dv * vn, axis=0, keepdims=True)
        gd1 = dv * gf_ref[...]
        dh1 = dh2 + r1 * (gd1 - vn * jnp.mean(vn * gd1, axis=-1, keepdims=True))
        dh1_ref[...] = dh1
        dh1b_ref[...] = dh1.astype(BF16)

    any_spec = pl.BlockSpec(memory_space=pl.ANY)
    return pl.pallas_call(
        body, name="ffn_fwd_bwd",
        grid=(nb,),
        in_specs=[_row_spec(), _x_spec(), _const_spec((1, D)), _const_spec((1, D)), any_spec, any_spec, any_spec],
        out_specs=[_row_spec(), _row_spec(), _row_spec(), _row_spec(D_FF), _row_spec(D_FF), _row_spec(D_FF), _row_spec(),
                   _const_spec((1, 1)), _const_spec((1, D)), _const_spec((1, D))],
        out_shape=[jax.ShapeDtypeStruct((tp, D), F32), jax.ShapeDtypeStruct((tp, D), BF16),
                   jax.ShapeDtypeStruct((tp, D), BF16), jax.ShapeDtypeStruct((tp, D_FF), BF16),
                   jax.ShapeDtypeStruct((tp, D_FF), BF16), jax.ShapeDtypeStruct((tp, D_FF), BF16),
                   jax.ShapeDtypeStruct((tp, D), BF16), jax.ShapeDtypeStruct((1, 1), F32),
                   jax.ShapeDtypeStruct((1, D), F32), jax.ShapeDtypeStruct((1, D), F32)],
        scratch_shapes=[pltpu.VMEM((D, D_FF), BF16), pltpu.VMEM((D, D_FF), BF16), pltpu.VMEM((D_FF, D), BF16),
                        pltpu.SemaphoreType.DMA((3,))],
        compiler_params=_cparams(("arbitrary",)),
    )(h1, target, g_ffn, g_final, w_g, w_u, w_d)


def _mixers_bwd_rows(dh1b, yc, yp, z, b_gate, cpre, ln_g, ln_b, mw, pool_scale, w_o, w_co, w_po, w_pool, nb):
    tp = nb * BR

    def body(dh1b_ref, yc_ref, yp_ref, za, zb, bg_ref, cpre_ref, lng_ref, lnb_ref, mw_ref, ps_ref,
             wo_ref, wco_ref, wpo_ref, wpool_ref,
             dycb_ref, dypb_ref, dzg_ref, dconv_ref, dmwb_ref, dm_ref, dbg_ref, dlng_ref, dlnb_ref, dbdw_ref, dps_ref):
        i = pl.program_id(0)

        @pl.when(i == 0)
        def _():
            for r in (dbg_ref, dlng_ref, dlnb_ref, dbdw_ref, dps_ref):
                r[...] = jnp.zeros_like(r)

        dmg = _dot_nt(dh1b_ref[...], wo_ref[...])
        s_a = _sigmoid(za[...] + bg_ref[:, 0:D])
        s_b = _sigmoid(zb[...] + bg_ref[:, D:2 * D])
        dycb = (dmg * s_a).astype(BF16)
        dypb = (dmg * s_b).astype(BF16)
        dycb_ref[...] = dycb
        dypb_ref[...] = dypb
        dza = dmg * yc_ref[...] * (s_a * (1.0 - s_a))
        dzb = dmg * yp_ref[...] * (s_b * (1.0 - s_b))
        dzg_ref[:, 0:D] = dza.astype(BF16)
        dzg_ref[:, D:2 * D] = dzb.astype(BF16)
        dbg_ref[:, 0:D] += jnp.sum(dza, axis=0, keepdims=True)
        dbg_ref[:, D:2 * D] += jnp.sum(dzb, axis=0, keepdims=True)

        dca = _dot_nt(dycb, wco_ref[...])
        conv = cpre_ref[...]
        mu = jnp.mean(conv, axis=-1, keepdims=True)
        xc = conv - mu
        rstd = lax.rsqrt(jnp.mean(xc * xc, axis=-1, keepdims=True) + LN_EPS)
        xhat = xc * rstd
        ln = xhat * lng_ref[...] + lnb_ref[...]
        sg = _sigmoid(ln)
        dln = dca * (sg * (1.0 + ln * (1.0 - sg)))
        dlng_ref[...] += jnp.sum(dln * xhat, axis=0, keepdims=True)
        dlnb_ref[...] += jnp.sum(dln, axis=0, keepdims=True)
        dxh = dln * lng_ref[...]
        dconv = rstd * (dxh - jnp.mean(dxh, axis=-1, keepdims=True)
                        - xhat * jnp.mean(dxh * xhat, axis=-1, keepdims=True))
        dconv_ref[...] = dconv
        dbdw_ref[...] += jnp.sum(dconv, axis=0, keepdims=True)

        dm2 = _dot_nt(dypb, wpo_ref[...])
        dps_ref[...] += jnp.sum(dm2 * mw_ref[...], axis=0, keepdims=True)
        dmwb = (dm2 * ps_ref[...]).astype(BF16)
        dmwb_ref[...] = dmwb
        for gi in range(len(POOL_WINDOWS)):
            lanes = slice(gi * GD, (gi + 1) * GD)
            dm_ref[:, lanes] = _dot_nt(dmwb[:, lanes], wpool_ref[gi])

    in_specs = [_row_spec(), _row_spec(), _row_spec(),
                pl.BlockSpec((BR, D), lambda i: (i, 3)), pl.BlockSpec((BR, D), lambda i: (i, 4)),
                _const_spec((1, 2 * D)), _row_spec(), _const_spec((1, D)), _const_spec((1, D)), _row_spec(),
                _const_spec((1, D)), _const_spec((D, D)), _const_spec((D, D)), _const_spec((D, D)),
                _const_spec((4, GD, GD))]
    return pl.pallas_call(
        body, name="mixers_bwd_rows",
        grid=(nb,),
        in_specs=in_specs,
        out_specs=[_row_spec(), _row_spec(), _row_spec(2 * D), _row_spec(), _row_spec(), _row_spec(),
                   _const_spec((1, 2 * D)), _const_spec((1, D)), _const_spec((1, D)), _const_spec((1, D)),
                   _const_spec((1, D))],
        out_shape=[jax.ShapeDtypeStruct((tp, D), BF16), jax.ShapeDtypeStruct((tp, D), BF16),
                   jax.ShapeDtypeStruct((tp, 2 * D), BF16), jax.ShapeDtypeStruct((tp, D), F32),
                   jax.ShapeDtypeStruct((tp, D), BF16), jax.ShapeDtypeStruct((tp, D), F32),
                   jax.ShapeDtypeStruct((1, 2 * D), F32), jax.ShapeDtypeStruct((1, D), F32),
                   jax.ShapeDtypeStruct((1, D), F32), jax.ShapeDtypeStruct((1, D), F32),
                   jax.ShapeDtypeStruct((1, D), F32)],
        compiler_params=_cparams(("arbitrary",)),
    )(dh1b, yc, yp, z, z, b_gate, cpre, ln_g, ln_b, mw, pool_scale, w_o, w_co, w_po, w_pool)


def _mixers_bwd_halo(dconv, dm, z, dzg, w_dw, head, x, g_mix, dh1, w_in_b, nb, t_total):
    tp = nb * BR
    ns = w_in_b.shape[0]
    wcols = w_in_b.shape[2]
    seq = x.shape[0]

    def body(dcp, dcc, dcn, dmp, dmc, dmn, avp, av, avn, agp, ag, agn, dzg_ref, wdw_ref, head_ref, x_ref, g_ref,
             dh1_ref, w_hbm,
             dzb_ref, gx_ref, dhead_ref, dwdw_ref, dgmix_ref,
             w_ref, sem, aext_ref, dext_ref, qext_ref, da_ref, rot_ref, dwp_ref):
        i = pl.program_id(0)

        @pl.when(i == 0)
        def _():
            cp = pltpu.make_async_copy(w_hbm, w_ref, sem.at[0])
            cp.start()
            dwp_ref[...] = jnp.zeros_like(dwp_ref)
            dgmix_ref[...] = jnp.zeros_like(dgmix_ref)
            cp.wait()

        sig_g = _sigmoid(ag[...])
        _fill_ext(aext_ref, avp[...] * _sigmoid(agp[...]), av[...] * sig_g, avn[...] * _sigmoid(agn[...]), i, nb)
        _fill_ext(dext_ref, dcp[...], dcc[...], dcn[...], i, nb)
        _fill_ext(qext_ref, dmp[...], dmc[...], dmn[...], i, nb)

        def conv_chunk(c, carry):
            lanes = pl.ds(pl.multiple_of(c * 128, 128), 128)
            _fill_rot(rot_ref, dext_ref, lanes)
            acc = jnp.zeros((BR, 128), F32)
            for k in range(KW):
                acc = acc + wdw_ref[k:k + 1, lanes] * _tap(rot_ref, dext_ref, lanes, KW - k)
            da_ref[:, lanes] = acc
            _fill_rot(rot_ref, aext_ref, lanes)
            dcv = dext_ref[HALO:HALO + BR, lanes]
            for k in range(KW):
                prod = _tap(rot_ref, aext_ref, lanes, 1 + k) * dcv
                dwp_ref[k, :, lanes] += jnp.sum(prod.reshape(BR // 8, 8, 128), axis=0)
            return carry
        lax.fori_loop(0, D // 128, conv_chunk, 0)

        @pl.when(i == nb - 1)
        def _():
            dwdw_ref[...] = jnp.sum(dwp_ref[...], axis=1)

        da = da_ref[...]
        a_val = av[...]
        dzb_ref[:, 0:D] = (da * sig_g).astype(BF16)
        dzb_ref[:, D:2 * D] = (da * a_val * (sig_g * (1.0 - sig_g))).astype(BF16)

        t_ext = _row_ids(i, EXT, -HALO)
        for gi, w in enumerate(POOL_WINDOWS):
            left = w // 2
            right = w - 1 - left
            lanes = slice(gi * GD, (gi + 1) * GD)
            qext_ref[:, lanes] = qext_ref[:, lanes] / _pool_cnt(t_ext, w, t_total)
            s = qext_ref[pl.ds(HALO - right, BR), lanes]
            for j in range(-right + 1, left + 1):
                s = s + qext_ref[pl.ds(HALO + j, BR), lanes]
            dzb_ref[:, 2 * D + gi * GD:2 * D + (gi + 1) * GD] = (s - dmc[:, lanes]).astype(BF16)
        dzb_ref[:, 3 * D:5 * D] = dzg_ref[...]

        du = _dot_nt(dzb_ref[:, 0:wcols], w_ref[0])
        for s_i in range(1, ns):
            du = du + _dot_nt(dzb_ref[:, s_i * wcols:(s_i + 1) * wcols], w_ref[s_i])
        h0 = jnp.where(i == 0, head_ref[...], x_ref[...])
        r0 = lax.rsqrt(jnp.mean(h0 * h0, axis=-1, keepdims=True) + RMS_EPS)
        un = h0 * r0
        dgmix_ref[...] += jnp.sum(du * un, axis=0, keepdims=True)
        gd = du * g_ref[...]
        dh0 = dh1_ref[...] + r0 * (gd - un * jnp.mean(un * gd, axis=-1, keepdims=True))
        gx_ref[...] = dh0

        @pl.when(i == 0)
        def _():
            dhead_ref[...] = dh0

    any_spec = pl.BlockSpec(memory_space=pl.ANY)
    in_specs = (_halo_specs(0, nb) + _halo_specs(0, nb) + _halo_specs(0, nb) + _halo_specs(1, nb)
                + [_row_spec(2 * D), _const_spec((32, D)), _const_spec((BR, D)), _x_spec(), _const_spec((1, D)),
                   _row_spec(), any_spec])
    return pl.pallas_call(
        body, name="mixers_bwd_halo",
        grid=(nb,),
        in_specs=in_specs,
        out_specs=[_row_spec(D_IN), _x_spec(), _const_spec((BR, D)), _const_spec((32, D)), _const_spec((1, D))],
        out_shape=[jax.ShapeDtypeStruct((tp, D_IN), BF16), jax.ShapeDtypeStruct((seq, D), F32),
                   jax.ShapeDtypeStruct((BR, D), F32), jax.ShapeDtypeStruct((32, D), F32),
                   jax.ShapeDtypeStruct((1, D), F32)],
        scratch_shapes=[pltpu.VMEM((ns, D, wcols), BF16), pltpu.SemaphoreType.DMA((1,)),
                        pltpu.VMEM((EXT, D), F32), pltpu.VMEM((EXT, D), F32), pltpu.VMEM((EXT, D), F32),
                        pltpu.VMEM((BR, D), F32), pltpu.VMEM((8, ROT_ROWS, 128), F32), pltpu.VMEM((32, 8, D), F32)],
        compiler_params=_cparams(("arbitrary",)),
    )(dconv, dconv, dconv, dm, dm, dm, z, z, z, z, z, z, dzg, w_dw, head, x, g_mix, dh1, w_in_b)


def _wgrad(a, c, tm, tn, tk, name, diag=False, col_major=False):
    tp, m = a.shape
    n = c.shape[1]
    nk = tp // tk
    gm, gn = m // tm, n // tn

    def body(a_ref, c_ref, o_ref, ob_ref):
        k = pl.program_id(2)

        @pl.when(k == 0)
        def _():
            o_ref[...] = jnp.zeros_like(o_ref)

        o_ref[...] += _dot_tn(a_ref[...], c_ref[...])

        @pl.when(k == nk - 1)
        def _():
            ob_ref[...] = o_ref[...].astype(BF16)

    c_map = lambda i, j, k: (k, j)
    grid = (gm, gn, nk)
    if diag:
        grid = (gm, 1, nk)
        c_map = lambda i, j, k: (k, i)
        o_spec = pl.BlockSpec((tm, tn), lambda i, j, k: (i, 0))
        o_shape = (m, tn)
    elif col_major:
        o_spec = pl.BlockSpec((None, tm, tn), lambda i, j, k: (j, i, 0))
        o_shape = (gn, m, tn)
    else:
        o_spec = pl.BlockSpec((tm, tn), lambda i, j, k: (i, j))
        o_shape = (m, n)
    return pl.pallas_call(
        body, name=name,
        grid=grid,
        in_specs=[pl.BlockSpec((tk, tm), lambda i, j, k: (k, i)), pl.BlockSpec((tk, tn), c_map)],
        out_specs=[o_spec, o_spec],
        out_shape=[jax.ShapeDtypeStruct(o_shape, F32), jax.ShapeDtypeStruct(o_shape, BF16)],
        compiler_params=_cparams(("arbitrary", "arbitrary", "arbitrary")),
    )(a, c)


def _local_step(x, target, head, wb, small, nb, tk):
    t_total = x.shape[0] + N_META
    u = _rms_u(head, x, small["g_mix"], nb)
    z = _in_proj(u, wb["w_in"], nb)
    h1, yc, yp, mg, ca, cpre, m, mw, m2b = _mixers_fwd(
        z, head, x, small["b_gate"], small["w_dw"], small["b_dw"], small["ln_g"], small["ln_b"],
        small["pool_scale"], wb["w_conv_out"], wb["w_pool"], wb["w_pool_out"], wb["w_o"], nb, t_total)
    dh1, dh1b, vb, fb, dgb, dub, dh2b, loss, dg_ffn, dg_final = _ffn_fwd_bwd(
        h1, target, small["g_ffn"], small["g_final"], wb["w_ffn_gate"], wb["w_ffn_up"], wb["w_ffn_down"], nb)
    dycb, dypb, dzg, dconv, dmwb, dm, db_gate, dln_g, dln_b, db_dw, dps = _mixers_bwd_rows(
        dh1b, yc, yp, z, small["b_gate"], cpre, small["ln_g"], small["ln_b"], mw, small["pool_scale"],
        wb["w_o"], wb["w_conv_out"], wb["w_pool_out"], wb["w_pool"], nb)
    dzb, grad_x, dhead, dw_dw, dg_mix = _mixers_bwd_halo(
        dconv, dm, z, dzg, small["w_dw"], head, x, small["g_mix"], dh1, wb["w_in"], nb, t_total)

    half_ff = D_FF // 2
    grads = {
        "w_in": _wgrad(u, dzb, D, D_IN // N_SHARD, tk, "wgrad_in", col_major=True),
        "w_conv_out": _wgrad(ca, dycb, D, D, tk, "wgrad_conv_out"),
        "w_pool": _wgrad(m, dmwb, GD, GD, tk, "wgrad_pool", diag=True),
        "w_pool_out": _wgrad(m2b, dypb, D, D, tk, "wgrad_pool_out"),
        "w_o": _wgrad(mg, dh1b, D, D, tk, "wgrad_o"),
        "w_ffn_gate": _wgrad(vb, dgb, D, half_ff, tk, "wgrad_ffn_gate"),
        "w_ffn_up": _wgrad(vb, dub, D, half_ff, tk, "wgrad_ffn_up"),
        "w_ffn_down": _wgrad(fb, dh2b, half_ff, D, tk, "wgrad_ffn_down"),
    }
    small_grads = {"g_mix": dg_mix, "b_gate": db_gate, "w_dw": dw_dw, "b_dw": db_dw, "ln_g": dln_g, "ln_b": dln_b,
                   "pool_scale": dps, "g_ffn": dg_ffn, "g_final": dg_final}
    return loss, grad_x, dhead, grads, small_grads


def _place():
    x, y, c = lax.axis_index("x"), lax.axis_index("y"), lax.axis_index("c")
    others = [(1 - x, y), (x, 1 - y), (1 - x, 1 - y)]
    return x, y, c, others


def _split2(a, axis=0):
    return a.reshape(a.shape[:axis] + (2, a.shape[axis] // 2) + a.shape[axis + 1:])


def _merge2(a, axis=0):
    return a.reshape(a.shape[:axis] + (2 * a.shape[axis + 1],) + a.shape[axis + 2:])


def _gather_weights(shards, tiny):
    n, nt = len(shards), len(tiny)
    any_spec = pl.BlockSpec(memory_space=pl.ANY)

    def body(*refs):
        ins, tins = refs[:n], refs[n:n + nt]
        outs, touts = refs[n + nt:2 * n + nt], refs[2 * n + nt:2 * (n + nt)]
        send_sems, recv_sems, local_sems = refs[2 * (n + nt):]
        x, y, c, others = _place()
        mine = 2 * x + y
        sibling = (x, y, 1 - c)

        def remote(src, dst, k, to):
            return pltpu.make_async_remote_copy(src_ref=src, dst_ref=dst, send_sem=send_sems.at[k],
                                                recv_sem=recv_sems.at[k], device_id=to, device_id_type=MESH)

        local = [pltpu.make_async_copy(ins[a], outs[a].at[mine], local_sems.at[a]) for a in range(n)]
        local += [pltpu.make_async_copy(tins[a], touts[a].at[mine], local_sems.at[n + a]) for a in range(nt)]
        for cp in local:
            cp.start()
        sends = []
        for a in range(n):
            for j, chip in enumerate(others):
                sends.append(remote(ins[a].at[c], outs[a].at[mine, c], a * 3 + j, (*chip, c)))
        for a in range(nt):
            for j, chip in enumerate(others):
                sends.append(remote(tins[a], touts[a].at[mine], 6 * n + a * 3 + j, (*chip, c)))
        for cp in sends:
            cp.start()
        passed = []
        for a in range(n):
            for j, chip in enumerate(others):
                landed = outs[a].at[2 * chip[0] + chip[1], c]
                remote(landed, landed, a * 3 + j, (x, y, c)).wait_recv()
                fwd = remote(landed, landed, 3 * n + a * 3 + j, sibling)
                fwd.start()
                passed.append(fwd)
        for a in range(n):
            for j, chip in enumerate(others):
                landed = outs[a].at[2 * chip[0] + chip[1], 1 - c]
                remote(landed, landed, 3 * n + a * 3 + j, (x, y, c)).wait_recv()
        for a in range(nt):
            for j, chip in enumerate(others):
                landed = touts[a].at[2 * chip[0] + chip[1]]
                remote(landed, landed, 6 * n + a * 3 + j, (x, y, c)).wait_recv()
        for cp in sends + passed:
            cp.wait_send()
        for cp in local:
            cp.wait()

    nsem = 6 * n + 3 * nt
    halves = [_split2(a) for a in shards]
    out = pl.pallas_call(
        body, name="gather_weights",
        in_specs=[any_spec] * (n + nt),
        out_specs=[any_spec] * (n + nt),
        out_shape=[jax.ShapeDtypeStruct((N_SHARD,) + a.shape, a.dtype) for a in halves + list(tiny)],
        scratch_shapes=[pltpu.SemaphoreType.DMA((nsem,)), pltpu.SemaphoreType.DMA((nsem,)),
                        pltpu.SemaphoreType.DMA((n + nt,))],
    )(*halves, *tiny)
    return [_merge2(o, 1) for o in out[:n]] + list(out[n:])


def _swap_halves_bf16(gbs):
    n = len(gbs)
    any_spec = pl.BlockSpec(memory_space=pl.ANY)

    def body(*refs):
        ins, outs = refs[:n], refs[n:2 * n]
        send_sems, recv_sems = refs[2 * n:]
        x, y, c, _ = _place()
        copies = []
        for a in range(n):
            copies.append(pltpu.make_async_remote_copy(
                src_ref=ins[a].at[:, 1 - c], dst_ref=outs[a], send_sem=send_sems.at[a], recv_sem=recv_sems.at[a],
                device_id=(x, y, 1 - c), device_id_type=MESH))
        for cp in copies:
            cp.start()
        for cp in copies:
            cp.wait()

    return pl.pallas_call(
        body, name="swap_halves_bf16",
        in_specs=[any_spec] * n, out_specs=[any_spec] * n,
        out_shape=[jax.ShapeDtypeStruct((g.shape[0], g.shape[1] // 2, g.shape[2]), g.dtype) for g in gbs],
        scratch_shapes=[pltpu.SemaphoreType.DMA((n,)), pltpu.SemaphoreType.DMA((n,))],
    )(*[_split2(g, 1) for g in gbs])


def _scatter_slabs(hbs):
    n = len(hbs)
    any_spec = pl.BlockSpec(memory_space=pl.ANY)

    def body(*refs):
        ins, outs = refs[:n], refs[n:2 * n]
        send_sems, recv_sems = refs[2 * n:]
        x, y, c, others = _place()
        mine = 2 * x + y
        copies = []
        for a in range(n):
            for j, chip in enumerate(others):
                copies.append(pltpu.make_async_remote_copy(
                    src_ref=ins[a].at[2 * chip[0] + chip[1]], dst_ref=outs[a].at[mine],
                    send_sem=send_sems.at[a * 3 + j], recv_sem=recv_sems.at[a * 3 + j],
                    device_id=(*chip, c), device_id_type=MESH))
        for cp in copies:
            cp.start()
        for a in range(n):
            for j, chip in enumerate(others):
                landed = outs[a].at[2 * chip[0] + chip[1]]
                pltpu.make_async_remote_copy(
                    src_ref=landed, dst_ref=landed, send_sem=send_sems.at[a * 3 + j], recv_sem=recv_sems.at[a * 3 + j],
                    device_id=(x, y, c), device_id_type=MESH).wait_recv()
        for cp in copies:
            cp.wait_send()

    return pl.pallas_call(
        body, name="scatter_slabs",
        in_specs=[any_spec] * n, out_specs=[any_spec] * n,
        out_shape=[jax.ShapeDtypeStruct(h.shape, h.dtype) for h in hbs],
        scratch_shapes=[pltpu.SemaphoreType.DMA((3 * n,)), pltpu.SemaphoreType.DMA((3 * n,))],
    )(*hbs)


def _join_halves(rhs):
    n = len(rhs)
    any_spec = pl.BlockSpec(memory_space=pl.ANY)

    def body(*refs):
        ins, outs = refs[:n], refs[n:2 * n]
        send_sems, recv_sems, local_sems = refs[2 * n:]
        x, y, c, _ = _place()
        copies, local = [], []
        for a in range(n):
            local.append(pltpu.make_async_copy(ins[a], outs[a].at[c], local_sems.at[a]))
            copies.append(pltpu.make_async_remote_copy(
                src_ref=ins[a], dst_ref=outs[a].at[c], send_sem=send_sems.at[a],
                recv_sem=recv_sems.at[a], device_id=(x, y, 1 - c), device_id_type=MESH))
        for cp in local + copies:
            cp.start()
        for a in range(n):
            landed = outs[a].at[1 - c]
            pltpu.make_async_remote_copy(
                src_ref=landed, dst_ref=landed, send_sem=send_sems.at[a], recv_sem=recv_sems.at[a],
                device_id=(x, y, c), device_id_type=MESH).wait_recv()
        for cp in copies:
            cp.wait_send()
        for cp in local:
            cp.wait()

    out = pl.pallas_call(
        body, name="join_halves",
        in_specs=[any_spec] * n, out_specs=[any_spec] * n,
        out_shape=[jax.ShapeDtypeStruct((2,) + r.shape, r.dtype) for r in rhs],
        scratch_shapes=[pltpu.SemaphoreType.DMA((n,)), pltpu.SemaphoreType.DMA((n,)), pltpu.SemaphoreType.DMA((n,))],
    )(*rhs)
    return [_merge2(o) for o in out]


def _allreduce_small(v):
    rows, cols = v.shape
    vm = pl.BlockSpec(memory_space=pltpu.VMEM)
    flips = [(dx, dy, dc) for dx in (0, 1) for dy in (0, 1) for dc in (0, 1)][1:]

    def body(v_ref, out_ref, buf_ref, send_sems, recv_sems):
        x, y, c, _ = _place()
        mine = 4 * x + 2 * y + c
        copies = []
        for k, (dx, dy, dc) in enumerate(flips):
            px, py, pc = jnp.bitwise_xor(x, dx), jnp.bitwise_xor(y, dy), jnp.bitwise_xor(c, dc)
            copies.append(pltpu.make_async_remote_copy(
                src_ref=v_ref, dst_ref=buf_ref.at[mine], send_sem=send_sems.at[k], recv_sem=recv_sems.at[k],
                device_id=(px, py, pc), device_id_type=MESH))
        for cp in copies:
            cp.start()
        buf_ref[mine] = v_ref[...]
        for k, (dx, dy, dc) in enumerate(flips):
            src = 4 * jnp.bitwise_xor(x, dx) + 2 * jnp.bitwise_xor(y, dy) + jnp.bitwise_xor(c, dc)
            pltpu.make_async_remote_copy(
                src_ref=v_ref, dst_ref=buf_ref.at[src], send_sem=send_sems.at[k], recv_sem=recv_sems.at[k],
                device_id=(x, y, c), device_id_type=MESH).wait_recv()
        for cp in copies:
            cp.wait_send()
        acc = buf_ref[0]
        for d in range(1, 8):
            acc = acc + buf_ref[d]
        out_ref[...] = acc

    return pl.pallas_call(
        body, name="allreduce_small",
        in_specs=[vm], out_specs=vm,
        out_shape=jax.ShapeDtypeStruct((rows, cols), F32),
        scratch_shapes=[pltpu.VMEM((8, rows, cols), F32), pltpu.SemaphoreType.DMA((7,)), pltpu.SemaphoreType.DMA((7,))],
    )(v)


def _row_block(r):
    for cand in (512, 352, 256, 128, 64, 48, 16):
        if r % cand == 0:
            return cand
    return r


def _add_sibling_half(g, sb, core, name):
    ns, r, c = g.shape
    r2 = r // 2

    def body(core_ref, g_ref, sb_ref, h_ref, hb_ref):
        h = g_ref[...] + sb_ref[...].astype(F32)
        h_ref[...] = h
        hb_ref[...] = h.astype(BF16)

    spec = pl.BlockSpec((None, r2, c), lambda s, core_ref: (s, 0, 0))
    return pl.pallas_call(
        body, name=name,
        grid_spec=pltpu.PrefetchScalarGridSpec(
            num_scalar_prefetch=1, grid=(ns,),
            in_specs=[pl.BlockSpec((None, r2, c), lambda s, core_ref: (s, core_ref[0], 0)), spec],
            out_specs=[spec, spec]),
        out_shape=[jax.ShapeDtypeStruct((ns, r2, c), F32), jax.ShapeDtypeStruct((ns, r2, c), BF16)],
        compiler_params=_cparams(("arbitrary",)),
    )(core, g, sb)


def _add_chip_slabs(h, rb, idx, name):
    ns, r2, c = h.shape

    def body(idx_ref, h_ref, r0_ref, r1_ref, r2_ref, o_ref):
        o_ref[...] = ((h_ref[...] + r0_ref[...].astype(F32)) + r1_ref[...].astype(F32)) + r2_ref[...].astype(F32)

    def pick(k):
        return pl.BlockSpec((None, r2, c), lambda i, idx_ref: (idx_ref[k], 0, 0))

    return pl.pallas_call(
        body, name=name,
        grid_spec=pltpu.PrefetchScalarGridSpec(
            num_scalar_prefetch=1, grid=(1,),
            in_specs=[pick(0), pick(1), pick(2), pick(3)],
            out_specs=pl.BlockSpec((r2, c), lambda i, idx_ref: (0, 0))),
        out_shape=jax.ShapeDtypeStruct((r2, c), F32),
        compiler_params=_cparams(("arbitrary",)),
    )(idx, h, rb, rb, rb)


def _adamw(g, w, m, v, name):
    r, c = g.shape
    br = _row_block(r)

    def body(g_ref, w_ref, m_ref, v_ref, d_ref, nm_ref, nv_ref):
        gg = g_ref[...]
        nm = B1 * m_ref[...] + (1.0 - B1) * gg
        nv = B2 * v_ref[...] + (1.0 - B2) * jnp.square(gg)
        m_hat = nm / (1.0 - B1 ** STEP)
        v_hat = nv / (1.0 - B2 ** STEP)
        d_ref[...] = -LR * (m_hat / (jnp.sqrt(v_hat) + ADAM_EPS) + WD * w_ref[...])
        nm_ref[...] = nm
        nv_ref[...] = nv

    spec = pl.BlockSpec((br, c), lambda i: (i, 0))
    return pl.pallas_call(
        body, name=name,
        grid=(r // br,),
        in_specs=[spec] * 4, out_specs=[spec] * 3,
        out_shape=[jax.ShapeDtypeStruct((r, c), F32)] * 3,
        compiler_params=_cparams(("arbitrary",)),
    )(g, w, m, v)


BIG = ("w_in", "w_conv_out", "w_pool", "w_pool_out", "w_o", "w_ffn_gate", "w_ffn_up", "w_ffn_down")
REPL = ("g_mix", "b_gate", "b_dw", "ln_g", "ln_b", "pool_scale", "g_ffn", "g_final")
WEIGHT_ORDER = ("meta_tokens", "g_mix", "w_in", "b_gate", "w_dw", "b_dw", "ln_g", "ln_b", "w_conv_out", "w_pool",
                "pool_scale", "w_pool_out", "w_o", "g_ffn", "w_ffn_gate", "w_ffn_up", "w_ffn_down", "g_final")


def _shard2d(name, a):
    a = a[0]
    if name == "w_pool":
        return a.reshape(4 * 64, GD)
    return a


def _cols_to_slabs(a):
    m, n = a.shape
    return a.reshape(m, N_SHARD, n // N_SHARD).transpose(1, 0, 2)


def _slabs_to_cols(a):
    ns, m, c = a.shape
    return a.transpose(1, 0, 2).reshape(m, ns * c)


def kernel(x, meta_tokens, g_mix, w_in, b_gate, w_dw, b_dw, ln_g, ln_b, w_conv_out, w_pool, pool_scale, w_pool_out, w_o, g_ffn, w_ffn_gate, w_ffn_up, w_ffn_down, g_final, loss_target, m_meta_tokens, m_g_mix, m_w_in, m_b_gate, m_w_dw, m_b_dw, m_ln_g, m_ln_b, m_w_conv_out, m_w_pool, m_pool_scale, m_w_pool_out, m_w_o, m_g_ffn, m_w_ffn_gate, m_w_ffn_up, m_w_ffn_down, m_g_final, v_meta_tokens, v_g_mix, v_w_in, v_b_gate, v_w_dw, v_b_dw, v_ln_g, v_ln_b, v_w_conv_out, v_w_pool, v_pool_scale, v_w_pool_out, v_w_o, v_g_ffn, v_w_ffn_gate, v_w_ffn_up, v_w_ffn_down, v_g_final):
    args = dict(locals())
    w = {n: args[n] for n in WEIGHT_ORDER}
    mom = {n: args["m_" + n] for n in WEIGHT_ORDER}
    var = {n: args["v_" + n] for n in WEIGHT_ORDER}
    seq = x.shape[1]
    nb = seq // BR + 1
    cx, cy, cc = lax.axis_index("x"), lax.axis_index("y"), lax.axis_index("c")
    chip = 2 * cx + cy

    shards = [_shard2d(n, w[n]).astype(BF16) if n != "w_pool" else w[n][0].astype(BF16) for n in BIG]
    gathered = _gather_weights(shards, [w["w_dw"][0], w["meta_tokens"]])
    gw = dict(zip(BIG, gathered[:len(BIG)]))
    w_dw_full = _slabs_to_cols(gathered[len(BIG)])
    meta_full = _slabs_to_cols(gathered[len(BIG) + 1])
    wb = {
        "w_in": gw["w_in"],
        "w_conv_out": gw["w_conv_out"].reshape(D, D),
        "w_pool": gw["w_pool"].transpose(1, 0, 2, 3).reshape(4, GD, GD),
        "w_pool_out": gw["w_pool_out"].reshape(D, D),
        "w_o": gw["w_o"].reshape(D, D),
        "w_ffn_gate": _slabs_to_cols(gw["w_ffn_gate"]),
        "w_ffn_up": _slabs_to_cols(gw["w_ffn_up"]),
        "w_ffn_down": gw["w_ffn_down"].reshape(D_FF, D),
    }
    small = {n: w[n] for n in REPL if n != "g_final"}
    small["g_final"] = w["g_final"].reshape(1, D)
    small["w_dw"] = jnp.pad(w_dw_full, ((0, 1), (0, 0)))
    head = jnp.concatenate([jnp.zeros((PAD, D), F32), meta_full], axis=0)

    loss, grad_x, dhead, grads, sgrads = _local_step(x[0], loss_target[0], head, wb, small, nb, tk=nb * BR // 4 if (nb * BR // 4) % 16 == 0 else BR)
    loss = lax.psum(loss[0, 0], ("x", "y", "c"))

    def slabs(name, g):
        if name == "w_in":
            return g
        if name in ("w_ffn_gate", "w_ffn_up"):
            return _cols_to_slabs(g)
        if name == "w_pool":
            return g.reshape(4, N_SHARD, 64, GD).transpose(1, 0, 2, 3).reshape(N_SHARD, 4 * 64, GD)
        return g.reshape(N_SHARD, g.shape[0] // N_SHARD, g.shape[1])

    g32 = [slabs(n, grads[n][0]) for n in BIG]
    g16 = [slabs(n, grads[n][1]) for n in BIG]
    from_sibling = _swap_halves_bf16(g16)
    core = jnp.reshape(cc, (1,)).astype(jnp.int32)
    halves = [_add_sibling_half(g, sb, core, "add_sibling_" + n) for n, g, sb in zip(BIG, g32, from_sibling)]
    from_chips = _scatter_slabs([hb for _, hb in halves])
    others = jnp.sort(jnp.stack([2 * (1 - cx) + cy, 2 * cx + (1 - cy), 2 * (1 - cx) + (1 - cy)]))
    idx = jnp.concatenate([jnp.reshape(chip, (1,)), others]).astype(jnp.int32)
    reduced_half = [_add_chip_slabs(h, rb, idx, "add_chips_" + n) for n, (h, _), rb in zip(BIG, halves, from_chips)]
    reduced = dict(zip(BIG, _join_halves(reduced_half)))

    packed = jnp.concatenate(
        [sgrads["g_mix"], sgrads["b_gate"].reshape(2, D), sgrads["b_dw"], sgrads["ln_g"], sgrads["ln_b"],
         sgrads["pool_scale"], sgrads["g_ffn"], sgrads["g_final"], jnp.zeros((7, D), F32),
         dhead[PAD:], sgrads["w_dw"]], axis=0)
    summed = _allreduce_small(packed)

    def repl_stack(d):
        return jnp.concatenate([d["g_mix"], d["b_gate"].reshape(2, D), d["b_dw"], d["ln_g"], d["ln_b"],
                                d["pool_scale"], d["g_ffn"], d["g_final"].reshape(1, D), jnp.ones((7, D), F32)], axis=0)

    def shard_stack(d):
        return jnp.concatenate([d["meta_tokens"], d["w_dw"][0], jnp.ones((1, GD), F32)], axis=0)

    g_repl = summed[0:16]
    g_shard = lax.dynamic_slice_in_dim(summed[16:64], chip * GD, GD, axis=1)
    d_repl, m_repl, v_repl = _adamw(g_repl, repl_stack(w), repl_stack(mom), repl_stack(var), "adamw_repl")
    d_shard, m_shard, v_shard = _adamw(g_shard, shard_stack(w), shard_stack(mom), shard_stack(var), "adamw_cols")

    def unpack(name, repl, shard):
        if name == "meta_tokens":
            return shard[0:N_META]
        if name == "w_dw":
            return shard[N_META:N_META + KW].reshape(1, KW, GD)
        row = {"g_mix": 0, "b_gate": 1, "b_dw": 3, "ln_g": 4, "ln_b": 5, "pool_scale": 6, "g_ffn": 7, "g_final": 8}[name]
        if name == "b_gate":
            return repl[1:3].reshape(1, 2 * D)
        if name == "g_final":
            return repl[8]
        return repl[row:row + 1]

    out_g, out_d, out_m, out_v = {}, {}, {}, {}
    for n in WEIGHT_ORDER:
        if n in BIG:
            g = reduced[n]
            d_, m_, v_ = _adamw(g, _shard2d(n, w[n]), _shard2d(n, mom[n]), _shard2d(n, var[n]), "adamw_" + n)
            shape = w[n].shape
            out_g[n], out_d[n], out_m[n], out_v[n] = (a.reshape(shape) for a in (g, d_, m_, v_))
        else:
            out_g[n] = unpack(n, g_repl, g_shard)
            out_d[n] = unpack(n, d_repl, d_shard)
            out_m[n] = unpack(n, m_repl, m_shard)
            out_v[n] = unpack(n, v_repl, v_shard)
    return (loss, grad_x[None], *[out_g[n] for n in WEIGHT_ORDER], *[out_d[n] for n in WEIGHT_ORDER],
            *[out_m[n] for n in WEIGHT_ORDER], *[out_v[n] for n in WEIGHT_ORDER])
```

```python
import functools

import jax
import jax.numpy as jnp
from jax import lax
from jax.experimental import pallas as pl
from jax.experimental.pallas import tpu as pltpu

F32 = jnp.float32
BF16 = jnp.bfloat16
MESH = pl.DeviceIdType.MESH

D = 1024
N_META = 16
KW = 31
CPAD = KW // 2
POOL_WINDOWS = (2, 4, 8, 16)
GD = 256
D_IN = 5 * D
D_FF = 2816
N_SHARD = 4
BR = 256
HALO = 16
PAD = BR - N_META
EXT = BR + 2 * HALO
RMS_EPS = 1e-6
LN_EPS = 1e-5
LR, B1, B2, ADAM_EPS, WD, STEP = 0.001, 0.9, 0.999, 1e-08, 0.01, 10
VMEM_LIMIT = 56 * 1024 * 1024


def _cparams(sem, vmem=VMEM_LIMIT):
    return pltpu.CompilerParams(dimension_semantics=sem, vmem_limit_bytes=vmem)


def _dot(a, b):
    return jnp.dot(a, b, preferred_element_type=F32)


def _dot_nt(a, b):
    return lax.dot_general(a, b, (((1,), (1,)), ((), ())), preferred_element_type=F32)


def _dot_tn(a, b):
    return lax.dot_general(a, b, (((0,), (0,)), ((), ())), preferred_element_type=F32)


def _sigmoid(x):
    return 1.0 / (1.0 + jnp.exp(-x))


def _row_ids(i, n, offset=0):
    return lax.broadcasted_iota(jnp.int32, (n, 1), 0) + (i * BR + offset - PAD)


def _pool_cnt(t, w, t_total):
    left = w // 2
    right = w - 1 - left
    lo = jnp.clip(t - left, 0, t_total)
    hi = jnp.clip(t + right + 1, 0, t_total)
    return jnp.maximum(hi - lo, 1).astype(F32)


def _halo_specs(col, nb):
    last = nb * (BR // HALO) - 1
    return [
        pl.BlockSpec((HALO, D), lambda i: (jnp.maximum(i * (BR // HALO) - 1, 0), col)),
        pl.BlockSpec((BR, D), lambda i: (i, col)),
        pl.BlockSpec((HALO, D), lambda i: (jnp.minimum((i + 1) * (BR // HALO), last), col)),
    ]


def _fill_ext(ext_ref, prev, cur, nxt, i, nb):
    ext_ref[0:HALO, :] = jnp.where(i > 0, prev, 0.0)
    ext_ref[HALO:HALO + BR, :] = cur
    ext_ref[HALO + BR:EXT, :] = jnp.where(i < nb - 1, nxt, 0.0)


ROT_ROWS = EXT - 8


def _fill_rot(rot_ref, ext_ref, lanes):
    for r in range(1, 8):
        rot_ref[r] = ext_ref[pl.ds(r, ROT_ROWS), lanes]


def _tap(rot_ref, ext_ref, lanes, offset):
    q, r = divmod(offset, 8)
    if r == 0:
        return ext_ref[pl.ds(8 * q, BR), lanes]
    return rot_ref[r, pl.ds(8 * q, BR), :]


def _row_spec(width=D):
    return pl.BlockSpec((BR, width), lambda i: (i, 0))


def _x_spec():
    return pl.BlockSpec((BR, D), lambda i: (jnp.maximum(i - 1, 0), 0))


def _const_spec(shape):
    nd = len(shape)
    return pl.BlockSpec(shape, lambda i: (0,) * nd)


def _rms_u(head, x, g_mix, nb):
    def body(head_ref, x_ref, g_ref, u_ref):
        i = pl.program_id(0)
        h = jnp.where(i == 0, head_ref[...], x_ref[...])
        r = lax.rsqrt(jnp.mean(h * h, axis=-1, keepdims=True) + RMS_EPS)
        u_ref[...] = ((h * r) * g_ref[...]).astype(BF16)

    return pl.pallas_call(
        body, name="rms_u",
        grid=(nb,),
        in_specs=[_const_spec((BR, D)), _x_spec(), _const_spec((1, D))],
        out_specs=_row_spec(),
        out_shape=jax.ShapeDtypeStruct((nb * BR, D), BF16),
        compiler_params=_cparams(("arbitrary",)),
    )(head, x, g_mix)


def _in_proj(u, w_in_b, nb):
    tp = nb * BR
    ns = w_in_b.shape[0]
    wcols = w_in_b.shape[2]

    def body(u_ref, w_ref, z_ref):
        z_ref[...] = _dot(u_ref[...], w_ref[...])

    return pl.pallas_call(
        body, name="in_proj",
        grid=(ns, nb),
        in_specs=[
            pl.BlockSpec((BR, D), lambda s, i: (i, 0)),
            pl.BlockSpec((None, D, wcols), lambda s, i: (s, 0, 0)),
        ],
        out_specs=pl.BlockSpec((BR, wcols), lambda s, i: (i, s)),
        out_shape=jax.ShapeDtypeStruct((tp, ns * wcols), F32),
        compiler_params=_cparams(("arbitrary", "arbitrary")),
    )(u, w_in_b)


def _mixers_fwd(z, head, x, b_gate, w_dw, b_dw, ln_g, ln_b, pool_scale, w_co, w_pool, w_po, w_o, nb, t_total):
    tp = nb * BR

    def body(avp, av, avn, agp, ag, agn, pp, pc, pn, za, zb, head_ref, x_ref, bg_ref, wdw_ref, bdw_ref,
             lng_ref, lnb_ref, ps_ref, wco_ref, wpool_ref, wpo_ref, wo_ref,
             h1_ref, yc_ref, yp_ref, mg_ref, ca_ref, cpre_ref, m_ref, mw_ref, m2b_ref, ext_ref, pext_ref, rot_ref):
        i = pl.program_id(0)
        _fill_ext(ext_ref, avp[...] * _sigmoid(agp[...]), av[...] * _sigmoid(ag[...]),
                  avn[...] * _sigmoid(agn[...]), i, nb)
        _fill_ext(pext_ref, pp[...], pc[...], pn[...], i, nb)

        def conv_chunk(c, carry):
            lanes = pl.ds(pl.multiple_of(c * 128, 128), 128)
            _fill_rot(rot_ref, ext_ref, lanes)
            acc = jnp.broadcast_to(bdw_ref[:, lanes], (BR, 128))
            for k in range(KW):
                acc = acc + wdw_ref[k:k + 1, lanes] * _tap(rot_ref, ext_ref, lanes, 1 + k)
            cpre_ref[:, lanes] = acc
            return carry
        lax.fori_loop(0, D // 128, conv_chunk, 0)

        conv = cpre_ref[...]
        mu = jnp.mean(conv, axis=-1, keepdims=True)
        xc = conv - mu
        rstd = lax.rsqrt(jnp.mean(xc * xc, axis=-1, keepdims=True) + LN_EPS)
        ln = (xc * rstd) * lng_ref[...] + lnb_ref[...]
        cact = (ln * _sigmoid(ln)).astype(BF16)
        ca_ref[...] = cact
        y_conv = _dot(cact, wco_ref[...])
        yc_ref[...] = y_conv

        t = _row_ids(i, BR)
        for gi, w in enumerate(POOL_WINDOWS):
            left = w // 2
            right = w - 1 - left
            lanes = slice(gi * GD, (gi + 1) * GD)
            s = pext_ref[pl.ds(HALO - left, BR), lanes]
            for j in range(-left + 1, right + 1):
                s = s + pext_ref[pl.ds(HALO + j, BR), lanes]
            m = (s / _pool_cnt(t, w, t_total) - pext_ref[HALO:HALO + BR, lanes]).astype(BF16)
            m_ref[:, lanes] = m
            mw_ref[:, lanes] = _dot(m, wpool_ref[gi])
        mw = mw_ref[...]
        m2b = (mw * ps_ref[...]).astype(BF16)
        m2b_ref[...] = m2b
        y_pool = _dot(m2b, wpo_ref[...])
        yp_ref[...] = y_pool

        s_a = _sigmoid(za[...] + bg_ref[:, 0:D])
        s_b = _sigmoid(zb[...] + bg_ref[:, D:2 * D])
        merged = (s_a * y_conv + s_b * y_pool).astype(BF16)
        mg_ref[...] = merged
        h0 = jnp.where(i == 0, head_ref[...], x_ref[...])
        h1_ref[...] = h0 + _dot(merged, wo_ref[...])

    in_specs = (_halo_specs(0, nb) + _halo_specs(1, nb) + _halo_specs(2, nb)
                + [pl.BlockSpec((BR, D), lambda i: (i, 3)), pl.BlockSpec((BR, D), lambda i: (i, 4)),
                   _const_spec((BR, D)), _x_spec(), _const_spec((1, 2 * D)), _const_spec((32, D)),
                   _const_spec((1, D)), _const_spec((1, D)), _const_spec((1, D)), _const_spec((1, D)),
                   _const_spec((D, D)), _const_spec((4, GD, GD)), _const_spec((D, D)), _const_spec((D, D))])
    outs = [(F32, "h1"), (F32, "yc"), (F32, "yp"), (BF16, "mg"), (BF16, "ca"), (F32, "cpre"), (BF16, "m"), (F32, "mw"),
            (BF16, "m2b")]
    return pl.pallas_call(
        body, name="mixers_fwd",
        grid=(nb,),
        in_specs=in_specs,
        out_specs=[_row_spec() for _ in outs],
        out_shape=[jax.ShapeDtypeStruct((tp, D), dt) for dt, _ in outs],
        scratch_shapes=[pltpu.VMEM((EXT, D), F32), pltpu.VMEM((EXT, D), F32), pltpu.VMEM((8, ROT_ROWS, 128), F32)],
        compiler_params=_cparams(("arbitrary",)),
    )(z, z, z, z, z, z, z, z, z, z, z, head, x, b_gate, w_dw, b_dw, ln_g, ln_b, pool_scale, w_co, w_pool, w_po, w_o)


def _ffn_fwd_bwd(h1, target, g_ffn, g_final, w_g, w_u, w_d, nb):
    tp = nb * BR

    def body(h1_ref, tgt_ref, gf_ref, gfin_ref, wg_hbm, wu_hbm, wd_hbm,
             dh1_ref, dh1b_ref, vb_ref, fb_ref, dgb_ref, dub_ref, dh2b_ref, loss_ref, dgf_ref, dgfin_ref,
             wg_ref, wu_ref, wd_ref, sem):
        i = pl.program_id(0)

        @pl.when(i == 0)
        def _():
            copies = [pltpu.make_async_copy(wg_hbm, wg_ref, sem.at[0]),
                      pltpu.make_async_copy(wu_hbm, wu_ref, sem.at[1]),
                      pltpu.make_async_copy(wd_hbm, wd_ref, sem.at[2])]
            for cp in copies:
                cp.start()
            loss_ref[...] = jnp.zeros_like(loss_ref)
            dgf_ref[...] = jnp.zeros_like(dgf_ref)
            dgfin_ref[...] = jnp.zeros_like(dgfin_ref)
            for cp in copies:
                cp.wait()

        h1 = h1_ref[...]
        r1 = lax.rsqrt(jnp.mean(h1 * h1, axis=-1, keepdims=True) + RMS_EPS)
        vn = h1 * r1
        vb = (vn * gf_ref[...]).astype(BF16)
        vb_ref[...] = vb
        g = _dot(vb, wg_ref[...])
        up = _dot(vb, wu_ref[...])
        sg = _sigmoid(g)
        sl = g * sg
        fb = (sl * up).astype(BF16)
        fb_ref[...] = fb
        h2 = h1 + _dot(fb, wd_ref[...])
        r2 = lax.rsqrt(jnp.mean(h2 * h2, axis=-1, keepdims=True) + RMS_EPS)
        yn = h2 * r2
        valid = i > 0
        diff = jnp.where(valid, yn * gfin_ref[...] - tgt_ref[...], 0.0)
        loss_ref[...] += 0.5 * jnp.sum(jnp.mean(diff * diff, axis=-1, keepdims=True))
        dy = diff * (1.0 / D)
        dgfin_ref[...] += jnp.sum(dy * yn, axis=0, keepdims=True)
        gd = dy * gfin_ref[...]
        dh2 = r2 * (gd - yn * jnp.mean(yn * gd, axis=-1, keepdims=True))
        dh2b = dh2.astype(BF16)
        dh2b_ref[...] = dh2b
        df = _dot_nt(dh2b, wd_ref[...])
        dub = (df * sl).astype(BF16)
        dgb = (df * up * (sg * (1.0 + g * (1.0 - sg)))).astype(BF16)
        dub_ref[...] = dub
        dgb_ref[...] = dgb
        dv = _dot_nt(dgb, wg_ref[...]) + _dot_nt(dub, wu_ref[...])
        dgf_ref[...] += jnp.sum(dv * vn, axis=0, keepdims=True)
        gd1 = dv * gf_ref[...]
        dh1 = dh2 + r1 * (gd1 - vn * jnp.mean(vn * gd1, axis=-1, keepdims=True))
        dh1_ref[...] = dh1
        dh1b_ref[...] = dh1.astype(BF16)

    any_spec = pl.BlockSpec(memory_space=pl.ANY)
    return pl.pallas_call(
        body, name="ffn_fwd_bwd",
        grid=(nb,),
        in_specs=[_row_spec(), _x_spec(), _const_spec((1, D)), _const_spec((1, D)), any_spec, any_spec, any_spec],
        out_specs=[_row_spec(), _row_spec(), _row_spec(), _row_spec(D_FF), _row_spec(D_FF), _row_spec(D_FF), _row_spec(),
                   _const_spec((1, 1)), _const_spec((1, D)), _const_spec((1, D))],
        out_shape=[jax.ShapeDtypeStruct((tp, D), F32), jax.ShapeDtypeStruct((tp, D), BF16),
                   jax.ShapeDtypeStruct((tp, D), BF16), jax.ShapeDtypeStruct((tp, D_FF), BF16),
                   jax.ShapeDtypeStruct((tp, D_FF), BF16), jax.ShapeDtypeStruct((tp, D_FF), BF16),
                   jax.ShapeDtypeStruct((tp, D), BF16), jax.ShapeDtypeStruct((1, 1), F32),
                   jax.ShapeDtypeStruct((1, D), F32), jax.ShapeDtypeStruct((1, D), F32)],
        scratch_shapes=[pltpu.VMEM((D, D_FF), BF16), pltpu.VMEM((D, D_FF), BF16), pltpu.VMEM((D_FF, D), BF16),
                        pltpu.SemaphoreType.DMA((3,))],
        compiler_params=_cparams(("arbitrary",)),
    )(h1, target, g_ffn, g_final, w_g, w_u, w_d)


def _mixers_bwd_rows(dh1b, yc, yp, z, b_gate, cpre, ln_g, ln_b, mw, pool_scale, w_o, w_co, w_po, w_pool, nb):
    tp = nb * BR

    def body(dh1b_ref, yc_ref, yp_ref, za, zb, bg_ref, cpre_ref, lng_ref, lnb_ref, mw_ref, ps_ref,
             wo_ref, wco_ref, wpo_ref, wpool_ref,
             dycb_ref, dypb_ref, dzg_ref, dconv_ref, dmwb_ref, dm_ref, dbg_ref, dlng_ref, dlnb_ref, dbdw_ref, dps_ref):
        i = pl.program_id(0)

        @pl.when(i == 0)
        def _():
            for r in (dbg_ref, dlng_ref, dlnb_ref, dbdw_ref, dps_ref):
                r[...] = jnp.zeros_like(r)

        dmg = _dot_nt(dh1b_ref[...], wo_ref[...])
        s_a = _sigmoid(za[...] + bg_ref[:, 0:D])
        s_b = _sigmoid(zb[...] + bg_ref[:, D:2 * D])
        dycb = (dmg * s_a).astype(BF16)
        dypb = (dmg * s_b).astype(BF16)
        dycb_ref[...] = dycb
        dypb_ref[...] = dypb
        dza = dmg * yc_ref[...] * (s_a * (1.0 - s_a))
        dzb = dmg * yp_ref[...] * (s_b * (1.0 - s_b))
        dzg_ref[:, 0:D] = dza.astype(BF16)
        dzg_ref[:, D:2 * D] = dzb.astype(BF16)
        dbg_ref[:, 0:D] += jnp.sum(dza, axis=0, keepdims=True)
        dbg_ref[:, D:2 * D] += jnp.sum(dzb, axis=0, keepdims=True)

        dca = _dot_nt(dycb, wco_ref[...])
        conv = cpre_ref[...]
        mu = jnp.mean(conv, axis=-1, keepdims=True)
        xc = conv - mu
        rstd = lax.rsqrt(jnp.mean(xc * xc, axis=-1, keepdims=True) + LN_EPS)
        xhat = xc * rstd
        ln = xhat * lng_ref[...] + lnb_ref[...]
        sg = _sigmoid(ln)
        dln = dca * (sg * (1.0 + ln * (1.0 - sg)))
        dlng_ref[...] += jnp.sum(dln * xhat, axis=0, keepdims=True)
        dlnb_ref[...] += jnp.sum(dln, axis=0, keepdims=True)
        dxh = dln * lng_ref[...]
        dconv = rstd * (dxh - jnp.mean(dxh, axis=-1, keepdims=True)
                        - xhat * jnp.mean(dxh * xhat, axis=-1, keepdims=True))
        dconv_ref[...] = dconv
        dbdw_ref[...] += jnp.sum(dconv, axis=0, keepdims=True)

        dm2 = _dot_nt(dypb, wpo_ref[...])
        dps_ref[...] += jnp.sum(dm2 * mw_ref[...], axis=0, keepdims=True)
        dmwb = (dm2 * ps_ref[...]).astype(BF16)
        dmwb_ref[...] = dmwb
        for gi in range(len(POOL_WINDOWS)):
            lanes = slice(gi * GD, (gi + 1) * GD)
            dm_ref[:, lanes] = _dot_nt(dmwb[:, lanes], wpool_ref[gi])

    in_specs = [_row_spec(), _row_spec(), _row_spec(),
                pl.BlockSpec((BR, D), lambda i: (i, 3)), pl.BlockSpec((BR, D), lambda i: (i, 4)),
                _const_spec((1, 2 * D)), _row_spec(), _const_spec((1, D)), _const_spec((1, D)), _row_spec(),
                _const_spec((1, D)), _const_spec((D, D)), _const_spec((D, D)), _const_spec((D, D)),
                _const_spec((4, GD, GD))]
    return pl.pallas_call(
        body, name="mixers_bwd_rows",
        grid=(nb,),
        in_specs=in_specs,
        out_specs=[_row_spec(), _row_spec(), _row_spec(2 * D), _row_spec(), _row_spec(), _row_spec(),
                   _const_spec((1, 2 * D)), _const_spec((1, D)), _const_spec((1, D)), _const_spec((1, D)),
                   _const_spec((1, D))],
        out_shape=[jax.ShapeDtypeStruct((tp, D), BF16), jax.ShapeDtypeStruct((tp, D), BF16),
                   jax.ShapeDtypeStruct((tp, 2 * D), BF16), jax.ShapeDtypeStruct((tp, D), F32),
                   jax.ShapeDtypeStruct((tp, D), BF16), jax.ShapeDtypeStruct((tp, D), F32),
                   jax.ShapeDtypeStruct((1, 2 * D), F32), jax.ShapeDtypeStruct((1, D), F32),
                   jax.ShapeDtypeStruct((1, D), F32), jax.ShapeDtypeStruct((1, D), F32),
                   jax.ShapeDtypeStruct((1, D), F32)],
        compiler_params=_cparams(("arbitrary",)),
    )(dh1b, yc, yp, z, z, b_gate, cpre, ln_g, ln_b, mw, pool_scale, w_o, w_co, w_po, w_pool)


def _mixers_bwd_halo(dconv, dm, z, dzg, w_dw, head, x, g_mix, dh1, w_in_b, nb, t_total):
    tp = nb * BR
    ns = w_in_b.shape[0]
    wcols = w_in_b.shape[2]
    seq = x.shape[0]

    def body(dcp, dcc, dcn, dmp, dmc, dmn, avp, av, avn, agp, ag, agn, dzg_ref, wdw_ref, head_ref, x_ref, g_ref,
             dh1_ref, w_hbm,
             dzb_ref, gx_ref, dhead_ref, dwdw_ref, dgmix_ref,
             w_ref, sem, aext_ref, dext_ref, qext_ref, da_ref, rot_ref, dwp_ref):
        i = pl.program_id(0)

        @pl.when(i == 0)
        def _():
            cp = pltpu.make_async_copy(w_hbm, w_ref, sem.at[0])
            cp.start()
            dwp_ref[...] = jnp.zeros_like(dwp_ref)
            dgmix_ref[...] = jnp.zeros_like(dgmix_ref)
            cp.wait()

        sig_g = _sigmoid(ag[...])
        _fill_ext(aext_ref, avp[...] * _sigmoid(agp[...]), av[...] * sig_g, avn[...] * _sigmoid(agn[...]), i, nb)
        _fill_ext(dext_ref, dcp[...], dcc[...], dcn[...], i, nb)
        _fill_ext(qext_ref, dmp[...], dmc[...], dmn[...], i, nb)

        def conv_chunk(c, carry):
            lanes = pl.ds(pl.multiple_of(c * 128, 128), 128)
            _fill_rot(rot_ref, dext_ref, lanes)
            acc = jnp.zeros((BR, 128), F32)
            for k in range(KW):
                acc = acc + wdw_ref[k:k + 1, lanes] * _tap(rot_ref, dext_ref, lanes, KW - k)
            da_ref[:, lanes] = acc
            _fill_rot(rot_ref, aext_ref, lanes)
            dcv = dext_ref[HALO:HALO + BR, lanes]
            for k in range(KW):
                prod = _tap(rot_ref, aext_ref, lanes, 1 + k) * dcv
                dwp_ref[k, :, lanes] += jnp.sum(prod.reshape(BR // 8, 8, 128), axis=0)
            return carry
        lax.fori_loop(0, D // 128, conv_chunk, 0)

        @pl.when(i == nb - 1)
        def _():
            dwdw_ref[...] = jnp.sum(dwp_ref[...], axis=1)

        da = da_ref[...]
        a_val = av[...]
        dzb_ref[:, 0:D] = (da * sig_g).astype(BF16)
        dzb_ref[:, D:2 * D] = (da * a_val * (sig_g * (1.0 - sig_g))).astype(BF16)

        t_ext = _row_ids(i, EXT, -HALO)
        for gi, w in enumerate(POOL_WINDOWS):
            left = w // 2
            right = w - 1 - left
            lanes = slice(gi * GD, (gi + 1) * GD)
            qext_ref[:, lanes] = qext_ref[:, lanes] / _pool_cnt(t_ext, w, t_total)
            s = qext_ref[pl.ds(HALO - right, BR), lanes]
            for j in range(-right + 1, left + 1):
                s = s + qext_ref[pl.ds(HALO + j, BR), lanes]
            dzb_ref[:, 2 * D + gi * GD:2 * D + (gi + 1) * GD] = (s - dmc[:, lanes]).astype(BF16)
        dzb_ref[:, 3 * D:5 * D] = dzg_ref[...]

        du = _dot_nt(dzb_ref[:, 0:wcols], w_ref[0])
        for s_i in range(1, ns):
            du = du + _dot_nt(dzb_ref[:, s_i * wcols:(s_i + 1) * wcols], w_ref[s_i])
        h0 = jnp.where(i == 0, head_ref[...], x_ref[...])
        r0 = lax.rsqrt(jnp.mean(h0 * h0, axis=-1, keepdims=True) + RMS_EPS)
        un = h0 * r0
        dgmix_ref[...] += jnp.sum(du * un, axis=0, keepdims=True)
        gd = du * g_ref[...]
        dh0 = dh1_ref[...] + r0 * (gd - un * jnp.mean(un * gd, axis=-1, keepdims=True))
        gx_ref[...] = dh0

        @pl.when(i == 0)
        def _():
            dhead_ref[...] = dh0

    any_spec = pl.BlockSpec(memory_space=pl.ANY)
    in_specs = (_halo_specs(0, nb) + _halo_specs(0, nb) + _halo_specs(0, nb) + _halo_specs(1, nb)
                + [_row_spec(2 * D), _const_spec((32, D)), _const_spec((BR, D)), _x_spec(), _const_spec((1, D)),
                   _row_spec(), any_spec])
    return pl.pallas_call(
        body, name="mixers_bwd_halo",
        grid=(nb,),
        in_specs=in_specs,
        out_specs=[_row_spec(D_IN), _x_spec(), _const_spec((BR, D)), _const_spec((32, D)), _const_spec((1, D))],
        out_shape=[jax.ShapeDtypeStruct((tp, D_IN), BF16), jax.ShapeDtypeStruct((seq, D), F32),
                   jax.ShapeDtypeStruct((BR, D), F32), jax.ShapeDtypeStruct((32, D), F32),
                   jax.ShapeDtypeStruct((1, D), F32)],
        scratch_shapes=[pltpu.VMEM((ns, D, wcols), BF16), pltpu.SemaphoreType.DMA((1,)),
                        pltpu.VMEM((EXT, D), F32), pltpu.VMEM((EXT, D), F32), pltpu.VMEM((EXT, D), F32),
                        pltpu.VMEM((BR, D), F32), pltpu.VMEM((8, ROT_ROWS, 128), F32), pltpu.VMEM((32, 8, D), F32)],
        compiler_params=_cparams(("arbitrary",)),
    )(dconv, dconv, dconv, dm, dm, dm, z, z, z, z, z, z, dzg, w_dw, head, x, g_mix, dh1, w_in_b)


def _wgrad(a, c, tm, tn, tk, name, diag=False, col_major=False):
    tp, m = a.shape
    n = c.shape[1]
    nk = tp // tk
    gm, gn = m // tm, n // tn

    def body(a_ref, c_ref, o_ref, ob_ref):
        k = pl.program_id(2)

        @pl.when(k == 0)
        def _():
            o_ref[...] = jnp.zeros_like(o_ref)

        o_ref[...] += _dot_tn(a_ref[...], c_ref[...])

        @pl.when(k == nk - 1)
        def _():
            ob_ref[...] = o_ref[...].astype(BF16)

    c_map = lambda i, j, k: (k, j)
    grid = (gm, gn, nk)
    if diag:
        grid = (gm, 1, nk)
        c_map = lambda i, j, k: (k, i)
        o_spec = pl.BlockSpec((tm, tn), lambda i, j, k: (i, 0))
        o_shape = (m, tn)
    elif col_major:
        o_spec = pl.BlockSpec((None, tm, tn), lambda i, j, k: (j, i, 0))
        o_shape = (gn, m, tn)
    else:
        o_spec = pl.BlockSpec((tm, tn), lambda i, j, k: (i, j))
        o_shape = (m, n)
    return pl.pallas_call(
        body, name=name,
        grid=grid,
        in_specs=[pl.BlockSpec((tk, tm), lambda i, j, k: (k, i)), pl.BlockSpec((tk, tn), c_map)],
        out_specs=[o_spec, o_spec],
        out_shape=[jax.ShapeDtypeStruct(o_shape, F32), jax.ShapeDtypeStruct(o_shape, BF16)],
        compiler_params=_cparams(("arbitrary", "arbitrary", "arbitrary")),
    )(a, c)


def _local_step(x, target, head, wb, small, nb, tk):
    t_total = x.shape[0] + N_META
    u = _rms_u(head, x, small["g_mix"], nb)
    z = _in_proj(u, wb["w_in"], nb)
    h1, yc, yp, mg, ca, cpre, m, mw, m2b = _mixers_fwd(
        z, head, x, small["b_gate"], small["w_dw"], small["b_dw"], small["ln_g"], small["ln_b"],
        small["pool_scale"], wb["w_conv_out"], wb["w_pool"], wb["w_pool_out"], wb["w_o"], nb, t_total)
    dh1, dh1b, vb, fb, dgb, dub, dh2b, loss, dg_ffn, dg_final = _ffn_fwd_bwd(
        h1, target, small["g_ffn"], small["g_final"], wb["w_ffn_gate"], wb["w_ffn_up"], wb["w_ffn_down"], nb)
    dycb, dypb, dzg, dconv, dmwb, dm, db_gate, dln_g, dln_b, db_dw, dps = _mixers_bwd_rows(
        dh1b, yc, yp, z, small["b_gate"], cpre, small["ln_g"], small["ln_b"], mw, small["pool_scale"],
        wb["w_o"], wb["w_conv_out"], wb["w_pool_out"], wb["w_pool"], nb)
    dzb, grad_x, dhead, dw_dw, dg_mix = _mixers_bwd_halo(
        dconv, dm, z, dzg, small["w_dw"], head, x, small["g_mix"], dh1, wb["w_in"], nb, t_total)

    half_ff = D_FF // 2
    grads = {
        "w_in": _wgrad(u, dzb, D, D_IN // N_SHARD, tk, "wgrad_in", col_major=True),
        "w_conv_out": _wgrad(ca, dycb, D, D, tk, "wgrad_conv_out"),
        "w_pool": _wgrad(m, dmwb, GD, GD, tk, "wgrad_pool", diag=True),
        "w_pool_out": _wgrad(m2b, dypb, D, D, tk, "wgrad_pool_out"),
        "w_o": _wgrad(mg, dh1b, D, D, tk, "wgrad_o"),
        "w_ffn_gate": _wgrad(vb, dgb, D, half_ff, tk, "wgrad_ffn_gate"),
        "w_ffn_up": _wgrad(vb, dub, D, half_ff, tk, "wgrad_ffn_up"),
        "w_ffn_down": _wgrad(fb, dh2b, half_ff, D, tk, "wgrad_ffn_down"),
    }
    small_grads = {"g_mix": dg_mix, "b_gate": db_gate, "w_dw": dw_dw, "b_dw": db_dw, "ln_g": dln_g, "ln_b": dln_b,
                   "pool_scale": dps, "g_ffn": dg_ffn, "g_final": dg_final}
    return loss, grad_x, dhead, grads, small_grads


def _place():
    x, y, c = lax.axis_index("x"), lax.axis_index("y"), lax.axis_index("c")
    others = [(1 - x, y), (x, 1 - y), (1 - x, 1 - y)]
    return x, y, c, others


def _split2(a, axis=0):
    return a.reshape(a.shape[:axis] + (2, a.shape[axis] // 2) + a.shape[axis + 1:])


def _merge2(a, axis=0):
    return a.reshape(a.shape[:axis] + (2 * a.shape[axis + 1],) + a.shape[axis + 2:])


def _cast_into_slot(w2d, chip, name):
    r, c = w2d.shape
    r2 = r // 2

    def body(chip_ref, w_ref, o_ref):
        o_ref[...] = w_ref[...].astype(BF16)

    return pl.pallas_call(
        body, name=name,
        grid_spec=pltpu.PrefetchScalarGridSpec(
            num_scalar_prefetch=1, grid=(2,),
            in_specs=[pl.BlockSpec((r2, c), lambda h, chip_ref: (h, 0))],
            out_specs=pl.BlockSpec((None, None, r2, c), lambda h, chip_ref: (chip_ref[0], h, 0, 0))),
        out_shape=jax.ShapeDtypeStruct((N_SHARD, 2, r2, c), BF16),
        compiler_params=_cparams(("arbitrary",)),
    )(chip, w2d)


def _gather_weights(bufs, tiny):
    n, nt = len(bufs), len(tiny)
    any_spec = pl.BlockSpec(memory_space=pl.ANY)

    def body(*refs):
        tins = refs[n:n + nt]
        outs, touts = refs[n + nt:2 * n + nt], refs[2 * n + nt:2 * (n + nt)]
        send_sems, recv_sems, local_sems = refs[2 * (n + nt):]
        x, y, c, others = _place()
        mine = 2 * x + y
        sibling = (x, y, 1 - c)

        def remote(src, dst, k, to):
            return pltpu.make_async_remote_copy(src_ref=src, dst_ref=dst, send_sem=send_sems.at[k],
                                                recv_sem=recv_sems.at[k], device_id=to, device_id_type=MESH)

        local = [pltpu.make_async_copy(tins[a], touts[a].at[mine], local_sems.at[a]) for a in range(nt)]
        for cp in local:
            cp.start()
        sends = []
        for a in range(n):
            for j, chip in enumerate(others):
                sends.append(remote(outs[a].at[mine, c], outs[a].at[mine, c], a * 3 + j, (*chip, c)))
        for a in range(nt):
            for j, chip in enumerate(others):
                sends.append(remote(tins[a], touts[a].at[mine], 6 * n + a * 3 + j, (*chip, c)))
        for cp in sends:
            cp.start()
        passed = []
        for a in range(n):
            for j, chip in enumerate(others):
                landed = outs[a].at[2 * chip[0] + chip[1], c]
                remote(landed, landed, a * 3 + j, (x, y, c)).wait_recv()
                fwd = remote(landed, landed, 3 * n + a * 3 + j, sibling)
                fwd.start()
                passed.append(fwd)
        for a in range(n):
            for j, chip in enumerate(others):
                landed = outs[a].at[2 * chip[0] + chip[1], 1 - c]
                remote(landed, landed, 3 * n + a * 3 + j, (x, y, c)).wait_recv()
        for a in range(nt):
            for j, chip in enumerate(others):
                landed = touts[a].at[2 * chip[0] + chip[1]]
                remote(landed, landed, 6 * n + a * 3 + j, (x, y, c)).wait_recv()
        for cp in sends + passed:
            cp.wait_send()
        for cp in local:
            cp.wait()

    nsem = 6 * n + 3 * nt
    out = pl.pallas_call(
        body, name="gather_weights",
        in_specs=[any_spec] * (n + nt),
        out_specs=[any_spec] * (n + nt),
        out_shape=([jax.ShapeDtypeStruct(b.shape, b.dtype) for b in bufs]
                   + [jax.ShapeDtypeStruct((N_SHARD,) + a.shape, a.dtype) for a in tiny]),
        input_output_aliases={a: a for a in range(n)},
        scratch_shapes=[pltpu.SemaphoreType.DMA((nsem,)), pltpu.SemaphoreType.DMA((nsem,)),
                        pltpu.SemaphoreType.DMA((nt,))],
    )(*bufs, *tiny)
    return [_merge2(o, 1) for o in out[:n]] + list(out[n:])


def _swap_halves_bf16(gbs):
    n = len(gbs)
    any_spec = pl.BlockSpec(memory_space=pl.ANY)

    def body(*refs):
        ins, outs = refs[:n], refs[n:2 * n]
        send_sems, recv_sems = refs[2 * n:]
        x, y, c, _ = _place()
        copies = []
        for a in range(n):
            copies.append(pltpu.make_async_remote_copy(
                src_ref=ins[a].at[:, 1 - c], dst_ref=outs[a], send_sem=send_sems.at[a], recv_sem=recv_sems.at[a],
                device_id=(x, y, 1 - c), device_id_type=MESH))
        for cp in copies:
            cp.start()
        for cp in copies:
            cp.wait()

    return pl.pallas_call(
        body, name="swap_halves_bf16",
        in_specs=[any_spec] * n, out_specs=[any_spec] * n,
        out_shape=[jax.ShapeDtypeStruct((g.shape[0], g.shape[1] // 2, g.shape[2]), g.dtype) for g in gbs],
        scratch_shapes=[pltpu.SemaphoreType.DMA((n,)), pltpu.SemaphoreType.DMA((n,))],
    )(*[_split2(g, 1) for g in gbs])


def _scatter_slabs(hbs):
    n = len(hbs)
    any_spec = pl.BlockSpec(memory_space=pl.ANY)

    def body(*refs):
        ins, outs = refs[:n], refs[n:2 * n]
        send_sems, recv_sems = refs[2 * n:]
        x, y, c, others = _place()
        mine = 2 * x + y
        copies = []
        for a in range(n):
            for j, chip in enumerate(others):
                copies.append(pltpu.make_async_remote_copy(
                    src_ref=ins[a].at[2 * chip[0] + chip[1]], dst_ref=outs[a].at[mine],
                    send_sem=send_sems.at[a * 3 + j], recv_sem=recv_sems.at[a * 3 + j],
                    device_id=(*chip, c), device_id_type=MESH))
        for cp in copies:
            cp.start()
        for a in range(n):
            for j, chip in enumerate(others):
                landed = outs[a].at[2 * chip[0] + chip[1]]
                pltpu.make_async_remote_copy(
                    src_ref=landed, dst_ref=landed, send_sem=send_sems.at[a * 3 + j], recv_sem=recv_sems.at[a * 3 + j],
                    device_id=(x, y, c), device_id_type=MESH).wait_recv()
        for cp in copies:
            cp.wait_send()

    return pl.pallas_call(
        body, name="scatter_slabs",
        in_specs=[any_spec] * n, out_specs=[any_spec] * n,
        out_shape=[jax.ShapeDtypeStruct(h.shape, h.dtype) for h in hbs],
        scratch_shapes=[pltpu.SemaphoreType.DMA((3 * n,)), pltpu.SemaphoreType.DMA((3 * n,))],
    )(*hbs)


def _join_halves(rhs):
    n = len(rhs)
    any_spec = pl.BlockSpec(memory_space=pl.ANY)

    def body(*refs):
        outs = refs[n:2 * n]
        send_sems, recv_sems = refs[2 * n:]
        x, y, c, _ = _place()
        copies = []
        for a in range(n):
            copies.append(pltpu.make_async_remote_copy(
                src_ref=outs[a].at[c], dst_ref=outs[a].at[c], send_sem=send_sems.at[a],
                recv_sem=recv_sems.at[a], device_id=(x, y, 1 - c), device_id_type=MESH))
        for cp in copies:
            cp.start()
        for a in range(n):
            landed = outs[a].at[1 - c]
            pltpu.make_async_remote_copy(
                src_ref=landed, dst_ref=landed, send_sem=send_sems.at[a], recv_sem=recv_sems.at[a],
                device_id=(x, y, c), device_id_type=MESH).wait_recv()
        for cp in copies:
            cp.wait_send()

    out = pl.pallas_call(
        body, name="join_halves",
        in_specs=[any_spec] * n, out_specs=[any_spec] * n,
        out_shape=[jax.ShapeDtypeStruct(r.shape, r.dtype) for r in rhs],
        input_output_aliases={a: a for a in range(n)},
        scratch_shapes=[pltpu.SemaphoreType.DMA((n,)), pltpu.SemaphoreType.DMA((n,))],
    )(*rhs)
    return [_merge2(o) for o in out]


def _allreduce_small(v):
    rows, cols = v.shape
    vm = pl.BlockSpec(memory_space=pltpu.VMEM)
    flips = [(dx, dy, dc) for dx in (0, 1) for dy in (0, 1) for dc in (0, 1)][1:]

    def body(v_ref, out_ref, buf_ref, send_sems, recv_sems):
        x, y, c, _ = _place()
        mine = 4 * x + 2 * y + c
        copies = []
        for k, (dx, dy, dc) in enumerate(flips):
            px, py, pc = jnp.bitwise_xor(x, dx), jnp.bitwise_xor(y, dy), jnp.bitwise_xor(c, dc)
            copies.append(pltpu.make_async_remote_copy(
                src_ref=v_ref, dst_ref=buf_ref.at[mine], send_sem=send_sems.at[k], recv_sem=recv_sems.at[k],
                device_id=(px, py, pc), device_id_type=MESH))
        for cp in copies:
            cp.start()
        buf_ref[mine] = v_ref[...]
        for k, (dx, dy, dc) in enumerate(flips):
            src = 4 * jnp.bitwise_xor(x, dx) + 2 * jnp.bitwise_xor(y, dy) + jnp.bitwise_xor(c, dc)
            pltpu.make_async_remote_copy(
                src_ref=v_ref, dst_ref=buf_ref.at[src], send_sem=send_sems.at[k], recv_sem=recv_sems.at[k],
                device_id=(x, y, c), device_id_type=MESH).wait_recv()
        for cp in copies:
            cp.wait_send()
        acc = buf_ref[0]
        for d in range(1, 8):
            acc = acc + buf_ref[d]
        out_ref[...] = acc

    return pl.pallas_call(
        body, name="allreduce_small",
        in_specs=[vm], out_specs=vm,
        out_shape=jax.ShapeDtypeStruct((rows, cols), F32),
        scratch_shapes=[pltpu.VMEM((8, rows, cols), F32), pltpu.SemaphoreType.DMA((7,)), pltpu.SemaphoreType.DMA((7,))],
    )(v)


def _row_block(r):
    for cand in (512, 352, 256, 128, 64, 48, 16):
        if r % cand == 0:
            return cand
    return r


def _add_sibling_half(g, sb, core, name):
    ns, r, c = g.shape
    r2 = r // 2

    def body(core_ref, g_ref, sb_ref, h_ref, hb_ref):
        h = g_ref[...] + sb_ref[...].astype(F32)
        h_ref[...] = h
        hb_ref[...] = h.astype(BF16)

    spec = pl.BlockSpec((None, r2, c), lambda s, core_ref: (s, 0, 0))
    return pl.pallas_call(
        body, name=name,
        grid_spec=pltpu.PrefetchScalarGridSpec(
            num_scalar_prefetch=1, grid=(ns,),
            in_specs=[pl.BlockSpec((None, r2, c), lambda s, core_ref: (s, core_ref[0], 0)), spec],
            out_specs=[spec, spec]),
        out_shape=[jax.ShapeDtypeStruct((ns, r2, c), F32), jax.ShapeDtypeStruct((ns, r2, c), BF16)],
        compiler_params=_cparams(("arbitrary",)),
    )(core, g, sb)


def _add_chip_slabs(h, rb, idx, name):
    ns, r2, c = h.shape

    def body(idx_ref, h_ref, r0_ref, r1_ref, r2_ref, o_ref):
        o_ref[...] = ((h_ref[...] + r0_ref[...].astype(F32)) + r1_ref[...].astype(F32)) + r2_ref[...].astype(F32)

    def pick(k):
        return pl.BlockSpec((None, r2, c), lambda i, idx_ref: (idx_ref[k], 0, 0))

    return pl.pallas_call(
        body, name=name,
        grid_spec=pltpu.PrefetchScalarGridSpec(
            num_scalar_prefetch=1, grid=(1,),
            in_specs=[pick(0), pick(1), pick(2), pick(3)],
            out_specs=pl.BlockSpec((None, r2, c), lambda i, idx_ref: (idx_ref[4], 0, 0))),
        out_shape=jax.ShapeDtypeStruct((2, r2, c), F32),
        compiler_params=_cparams(("arbitrary",)),
    )(idx, h, rb, rb, rb)


def _adamw(g, w, m, v, name):
    r, c = g.shape
    br = _row_block(r)

    def body(g_ref, w_ref, m_ref, v_ref, d_ref, nm_ref, nv_ref):
        gg = g_ref[...]
        nm = B1 * m_ref[...] + (1.0 - B1) * gg
        nv = B2 * v_ref[...] + (1.0 - B2) * jnp.square(gg)
        m_hat = nm / (1.0 - B1 ** STEP)
        v_hat = nv / (1.0 - B2 ** STEP)
        d_ref[...] = -LR * (m_hat / (jnp.sqrt(v_hat) + ADAM_EPS) + WD * w_ref[...])
        nm_ref[...] = nm
        nv_ref[...] = nv

    spec = pl.BlockSpec((br, c), lambda i: (i, 0))
    return pl.pallas_call(
        body, name=name,
        grid=(r // br,),
        in_specs=[spec] * 4, out_specs=[spec] * 3,
        out_shape=[jax.ShapeDtypeStruct((r, c), F32)] * 3,
        compiler_params=_cparams(("arbitrary",)),
    )(g, w, m, v)


BIG = ("w_in", "w_conv_out", "w_pool", "w_pool_out", "w_o", "w_ffn_gate", "w_ffn_up", "w_ffn_down")
REPL = ("g_mix", "b_gate", "b_dw", "ln_g", "ln_b", "pool_scale", "g_ffn", "g_final")
WEIGHT_ORDER = ("meta_tokens", "g_mix", "w_in", "b_gate", "w_dw", "b_dw", "ln_g", "ln_b", "w_conv_out", "w_pool",
                "pool_scale", "w_pool_out", "w_o", "g_ffn", "w_ffn_gate", "w_ffn_up", "w_ffn_down", "g_final")


def _shard2d(name, a):
    a = a[0]
    if name == "w_pool":
        return a.reshape(4 * 64, GD)
    return a


def _cols_to_slabs(a):
    m, n = a.shape
    return a.reshape(m, N_SHARD, n // N_SHARD).transpose(1, 0, 2)


def _slabs_to_cols(a):
    ns, m, c = a.shape
    return a.transpose(1, 0, 2).reshape(m, ns * c)


def kernel(x, meta_tokens, g_mix, w_in, b_gate, w_dw, b_dw, ln_g, ln_b, w_conv_out, w_pool, pool_scale, w_pool_out, w_o, g_ffn, w_ffn_gate, w_ffn_up, w_ffn_down, g_final, loss_target, m_meta_tokens, m_g_mix, m_w_in, m_b_gate, m_w_dw, m_b_dw, m_ln_g, m_ln_b, m_w_conv_out, m_w_pool, m_pool_scale, m_w_pool_out, m_w_o, m_g_ffn, m_w_ffn_gate, m_w_ffn_up, m_w_ffn_down, m_g_final, v_meta_tokens, v_g_mix, v_w_in, v_b_gate, v_w_dw, v_b_dw, v_ln_g, v_ln_b, v_w_conv_out, v_w_pool, v_pool_scale, v_w_pool_out, v_w_o, v_g_ffn, v_w_ffn_gate, v_w_ffn_up, v_w_ffn_down, v_g_final):
    args = dict(locals())
    w = {n: args[n] for n in WEIGHT_ORDER}
    mom = {n: args["m_" + n] for n in WEIGHT_ORDER}
    var = {n: args["v_" + n] for n in WEIGHT_ORDER}
    seq = x.shape[1]
    nb = seq // BR + 1
    cx, cy, cc = lax.axis_index("x"), lax.axis_index("y"), lax.axis_index("c")
    chip = 2 * cx + cy

    chip1 = jnp.reshape(chip, (1,)).astype(jnp.int32)
    bufs = [_cast_into_slot(_shard2d(n, w[n]), chip1, "cast_" + n) for n in BIG]
    gathered = _gather_weights(bufs, [w["w_dw"][0], w["meta_tokens"]])
    gw = dict(zip(BIG, gathered[:len(BIG)]))
    gw["w_pool"] = gw["w_pool"].reshape(N_SHARD, 4, 64, GD)
    w_dw_full = _slabs_to_cols(gathered[len(BIG)])
    meta_full = _slabs_to_cols(gathered[len(BIG) + 1])
    wb = {
        "w_in": gw["w_in"],
        "w_conv_out": gw["w_conv_out"].reshape(D, D),
        "w_pool": gw["w_pool"].transpose(1, 0, 2, 3).reshape(4, GD, GD),
        "w_pool_out": gw["w_pool_out"].reshape(D, D),
        "w_o": gw["w_o"].reshape(D, D),
        "w_ffn_gate": _slabs_to_cols(gw["w_ffn_gate"]),
        "w_ffn_up": _slabs_to_cols(gw["w_ffn_up"]),
        "w_ffn_down": gw["w_ffn_down"].reshape(D_FF, D),
    }
    small = {n: w[n] for n in REPL if n != "g_final"}
    small["g_final"] = w["g_final"].reshape(1, D)
    small["w_dw"] = jnp.pad(w_dw_full, ((0, 1), (0, 0)))
    head = jnp.concatenate([jnp.zeros((PAD, D), F32), meta_full], axis=0)

    loss, grad_x, dhead, grads, sgrads = _local_step(x[0], loss_target[0], head, wb, small, nb, tk=nb * BR // 4 if (nb * BR // 4) % 16 == 0 else BR)
    loss = lax.psum(loss[0, 0], ("x", "y", "c"))

    def slabs(name, g):
        if name == "w_in":
            return g
        if name in ("w_ffn_gate", "w_ffn_up"):
            return _cols_to_slabs(g)
        if name == "w_pool":
            return g.reshape(4, N_SHARD, 64, GD).transpose(1, 0, 2, 3).reshape(N_SHARD, 4 * 64, GD)
        return g.reshape(N_SHARD, g.shape[0] // N_SHARD, g.shape[1])

    g32 = [slabs(n, grads[n][0]) for n in BIG]
    g16 = [slabs(n, grads[n][1]) for n in BIG]
    from_sibling = _swap_halves_bf16(g16)
    core = jnp.reshape(cc, (1,)).astype(jnp.int32)
    halves = [_add_sibling_half(g, sb, core, "add_sibling_" + n) for n, g, sb in zip(BIG, g32, from_sibling)]
    from_chips = _scatter_slabs([hb for _, hb in halves])
    others = jnp.sort(jnp.stack([2 * (1 - cx) + cy, 2 * cx + (1 - cy), 2 * (1 - cx) + (1 - cy)]))
    idx = jnp.concatenate([jnp.reshape(chip, (1,)), others, jnp.reshape(cc, (1,))]).astype(jnp.int32)
    reduced_half = [_add_chip_slabs(h, rb, idx, "add_chips_" + n) for n, (h, _), rb in zip(BIG, halves, from_chips)]
    reduced = dict(zip(BIG, _join_halves(reduced_half)))

    packed = jnp.concatenate(
        [sgrads["g_mix"], sgrads["b_gate"].reshape(2, D), sgrads["b_dw"], sgrads["ln_g"], sgrads["ln_b"],
         sgrads["pool_scale"], sgrads["g_ffn"], sgrads["g_final"], jnp.zeros((7, D), F32),
         dhead[PAD:], sgrads["w_dw"]], axis=0)
    summed = _allreduce_small(packed)

    def repl_stack(d):
        return jnp.concatenate([d["g_mix"], d["b_gate"].reshape(2, D), d["b_dw"], d["ln_g"], d["ln_b"],
                                d["pool_scale"], d["g_ffn"], d["g_final"].reshape(1, D), jnp.ones((7, D), F32)], axis=0)

    def shard_stack(d):
        return jnp.concatenate([d["meta_tokens"], d["w_dw"][0], jnp.ones((1, GD), F32)], axis=0)

    g_repl = summed[0:16]
    g_shard = lax.dynamic_slice_in_dim(summed[16:64], chip * GD, GD, axis=1)
    d_repl, m_repl, v_repl = _adamw(g_repl, repl_stack(w), repl_stack(mom), repl_stack(var), "adamw_repl")
    d_shard, m_shard, v_shard = _adamw(g_shard, shard_stack(w), shard_stack(mom), shard_stack(var), "adamw_cols")

    def unpack(name, repl, shard):
        if name == "meta_tokens":
            return shard[0:N_META]
        if name == "w_dw":
            return shard[N_META:N_META + KW].reshape(1, KW, GD)
        row = {"g_mix": 0, "b_gate": 1, "b_dw": 3, "ln_g": 4, "ln_b": 5, "pool_scale": 6, "g_ffn": 7, "g_final": 8}[name]
        if name == "b_gate":
            return repl[1:3].reshape(1, 2 * D)
        if name == "g_final":
            return repl[8]
        return repl[row:row + 1]

    out_g, out_d, out_m, out_v = {}, {}, {}, {}
    for n in WEIGHT_ORDER:
        if n in BIG:
            g = reduced[n]
            d_, m_, v_ = _adamw(g, _shard2d(n, w[n]), _shard2d(n, mom[n]), _shard2d(n, var[n]), "adamw_" + n)
            shape = w[n].shape
            out_g[n], out_d[n], out_m[n], out_v[n] = (a.reshape(shape) for a in (g, d_, m_, v_))
        else:
            out_g[n] = unpack(n, g_repl, g_shard)
            out_d[n] = unpack(n, d_repl, d_shard)
            out_m[n] = unpack(n, m_repl, m_shard)
            out_v[n] = unpack(n, v_repl, v_shard)
    return (loss, grad_x[None], *[out_g[n] for n in WEIGHT_ORDER], *[out_d[n] for n in WEIGHT_ORDER],
            *[out_m[n] for n in WEIGHT_ORDER], *[out_v[n] for n in WEIGHT_ORDER])
```

```python
import functools

import jax
import jax.numpy as jnp
from jax import lax
from jax.experimental import pallas as pl
from jax.experimental.pallas import tpu as pltpu

F32 = jnp.float32
BF16 = jnp.bfloat16
MESH = pl.DeviceIdType.MESH

D = 1024
N_META = 16
KW = 31
CPAD = KW // 2
POOL_WINDOWS = (2, 4, 8, 16)
GD = 256
D_IN = 5 * D
D_FF = 2816
N_SHARD = 4
BR = 256
HALO = 16
PAD = BR - N_META
EXT = BR + 2 * HALO
RMS_EPS = 1e-6
LN_EPS = 1e-5
LR, B1, B2, ADAM_EPS, WD, STEP = 0.001, 0.9, 0.999, 1e-08, 0.01, 10
VMEM_LIMIT = 56 * 1024 * 1024


def _cparams(sem, vmem=VMEM_LIMIT):
    return pltpu.CompilerParams(dimension_semantics=sem, vmem_limit_bytes=vmem)


def _dot(a, b):
    return jnp.dot(a, b, preferred_element_type=F32)


def _dot_nt(a, b):
    return lax.dot_general(a, b, (((1,), (1,)), ((), ())), preferred_element_type=F32)


def _dot_tn(a, b):
    return lax.dot_general(a, b, (((0,), (0,)), ((), ())), preferred_element_type=F32)


def _sigmoid(x):
    return 1.0 / (1.0 + jnp.exp(-x))


def _row_ids(i, n, offset=0):
    return lax.broadcasted_iota(jnp.int32, (n, 1), 0) + (i * BR + offset - PAD)


def _pool_cnt(t, w, t_total):
    left = w // 2
    right = w - 1 - left
    lo = jnp.clip(t - left, 0, t_total)
    hi = jnp.clip(t + right + 1, 0, t_total)
    return jnp.maximum(hi - lo, 1).astype(F32)


def _halo_specs(col, nb):
    last = nb * (BR // HALO) - 1
    return [
        pl.BlockSpec((HALO, D), lambda i: (jnp.maximum(i * (BR // HALO) - 1, 0), col)),
        pl.BlockSpec((BR, D), lambda i: (i, col)),
        pl.BlockSpec((HALO, D), lambda i: (jnp.minimum((i + 1) * (BR // HALO), last), col)),
    ]


def _fill_ext(ext_ref, prev, cur, nxt, i, nb):
    ext_ref[0:HALO, :] = jnp.where(i > 0, prev, 0.0)
    ext_ref[HALO:HALO + BR, :] = cur
    ext_ref[HALO + BR:EXT, :] = jnp.where(i < nb - 1, nxt, 0.0)


ROT_ROWS = EXT - 8


def _fill_rot(rot_ref, ext_ref, lanes):
    for r in range(1, 8):
        rot_ref[r] = ext_ref[pl.ds(r, ROT_ROWS), lanes]


def _tap(rot_ref, ext_ref, lanes, offset):
    q, r = divmod(offset, 8)
    if r == 0:
        return ext_ref[pl.ds(8 * q, BR), lanes]
    return rot_ref[r, pl.ds(8 * q, BR), :]


def _row_spec(width=D):
    return pl.BlockSpec((BR, width), lambda i: (i, 0))


def _x_spec():
    return pl.BlockSpec((BR, D), lambda i: (jnp.maximum(i - 1, 0), 0))


def _const_spec(shape):
    nd = len(shape)
    return pl.BlockSpec(shape, lambda i: (0,) * nd)


def _rms_u(head, x, g_mix, nb):
    def body(head_ref, x_ref, g_ref, u_ref):
        i = pl.program_id(0)
        h = jnp.where(i == 0, head_ref[...], x_ref[...])
        r = lax.rsqrt(jnp.mean(h * h, axis=-1, keepdims=True) + RMS_EPS)
        u_ref[...] = ((h * r) * g_ref[...]).astype(BF16)

    return pl.pallas_call(
        body, name="rms_u",
        grid=(nb,),
        in_specs=[_const_spec((BR, D)), _x_spec(), _const_spec((1, D))],
        out_specs=_row_spec(),
        out_shape=jax.ShapeDtypeStruct((nb * BR, D), BF16),
        compiler_params=_cparams(("arbitrary",)),
    )(head, x, g_mix)


def _in_proj(u, w_in_b, dep, nb):
    tp = nb * BR
    ns = w_in_b.shape[0]
    wcols = w_in_b.shape[2]

    def body(u_ref, w_ref, dep_ref, z_ref):
        z_ref[...] = _dot(u_ref[...], w_ref[...])

    return pl.pallas_call(
        body, name="in_proj",
        grid=(ns, nb),
        in_specs=[
            pl.BlockSpec((BR, D), lambda s, i: (i, 0)),
            pl.BlockSpec((None, D, wcols), lambda s, i: (s, 0, 0)),
            pl.BlockSpec((8, 128), lambda s, i: (0, 0)),
        ],
        out_specs=pl.BlockSpec((BR, wcols), lambda s, i: (i, s)),
        out_shape=jax.ShapeDtypeStruct((tp, ns * wcols), F32),
        compiler_params=_cparams(("arbitrary", "arbitrary")),
    )(u, w_in_b, dep)


def _mixers_fwd(z, head, x, b_gate, w_dw, b_dw, ln_g, ln_b, pool_scale, w_co, w_pool, w_po, w_o, nb, t_total):
    tp = nb * BR

    def body(avp, av, avn, agp, ag, agn, pp, pc, pn, za, zb, head_ref, x_ref, bg_ref, wdw_ref, bdw_ref,
             lng_ref, lnb_ref, ps_ref, wco_ref, wpool_ref, wpo_ref, wo_ref,
             h1_ref, yc_ref, yp_ref, mg_ref, ca_ref, cpre_ref, m_ref, mw_ref, m2b_ref, ext_ref, pext_ref, rot_ref):
        i = pl.program_id(0)
        _fill_ext(ext_ref, avp[...] * _sigmoid(agp[...]), av[...] * _sigmoid(ag[...]),
                  avn[...] * _sigmoid(agn[...]), i, nb)
        _fill_ext(pext_ref, pp[...], pc[...], pn[...], i, nb)

        def conv_chunk(c, carry):
            lanes = pl.ds(pl.multiple_of(c * 128, 128), 128)
            _fill_rot(rot_ref, ext_ref, lanes)
            acc = jnp.broadcast_to(bdw_ref[:, lanes], (BR, 128))
            for k in range(KW):
                acc = acc + wdw_ref[k:k + 1, lanes] * _tap(rot_ref, ext_ref, lanes, 1 + k)
            cpre_ref[:, lanes] = acc
            return carry
        lax.fori_loop(0, D // 128, conv_chunk, 0)

        conv = cpre_ref[...]
        mu = jnp.mean(conv, axis=-1, keepdims=True)
        xc = conv - mu
        rstd = lax.rsqrt(jnp.mean(xc * xc, axis=-1, keepdims=True) + LN_EPS)
        ln = (xc * rstd) * lng_ref[...] + lnb_ref[...]
        cact = (ln * _sigmoid(ln)).astype(BF16)
        ca_ref[...] = cact
        y_conv = _dot(cact, wco_ref[...])
        yc_ref[...] = y_conv

        t = _row_ids(i, BR)
        for gi, w in enumerate(POOL_WINDOWS):
            left = w // 2
            right = w - 1 - left
            lanes = slice(gi * GD, (gi + 1) * GD)
            s = pext_ref[pl.ds(HALO - left, BR), lanes]
            for j in range(-left + 1, right + 1):
                s = s + pext_ref[pl.ds(HALO + j, BR), lanes]
            m = (s / _pool_cnt(t, w, t_total) - pext_ref[HALO:HALO + BR, lanes]).astype(BF16)
            m_ref[:, lanes] = m
            mw_ref[:, lanes] = _dot(m, wpool_ref[gi])
        mw = mw_ref[...]
        m2b = (mw * ps_ref[...]).astype(BF16)
        m2b_ref[...] = m2b
        y_pool = _dot(m2b, wpo_ref[...])
        yp_ref[...] = y_pool

        s_a = _sigmoid(za[...] + bg_ref[:, 0:D])
        s_b = _sigmoid(zb[...] + bg_ref[:, D:2 * D])
        merged = (s_a * y_conv + s_b * y_pool).astype(BF16)
        mg_ref[...] = merged
        h0 = jnp.where(i == 0, head_ref[...], x_ref[...])
        h1_ref[...] = h0 + _dot(merged, wo_ref[...])

    in_specs = (_halo_specs(0, nb) + _halo_specs(1, nb) + _halo_specs(2, nb)
                + [pl.BlockSpec((BR, D), lambda i: (i, 3)), pl.BlockSpec((BR, D), lambda i: (i, 4)),
                   _const_spec((BR, D)), _x_spec(), _const_spec((1, 2 * D)), _const_spec((32, D)),
                   _const_spec((1, D)), _const_spec((1, D)), _const_spec((1, D)), _const_spec((1, D)),
                   _const_spec((D, D)), _const_spec((4, GD, GD)), _const_spec((D, D)), _const_spec((D, D))])
    outs = [(F32, "h1"), (F32, "yc"), (F32, "yp"), (BF16, "mg"), (BF16, "ca"), (F32, "cpre"), (BF16, "m"), (F32, "mw"),
            (BF16, "m2b")]
    return pl.pallas_call(
        body, name="mixers_fwd",
        grid=(nb,),
        in_specs=in_specs,
        out_specs=[_row_spec() for _ in outs],
        out_shape=[jax.ShapeDtypeStruct((tp, D), dt) for dt, _ in outs],
        scratch_shapes=[pltpu.VMEM((EXT, D), F32), pltpu.VMEM((EXT, D), F32), pltpu.VMEM((8, ROT_ROWS, 128), F32)],
        compiler_params=_cparams(("arbitrary",)),
    )(z, z, z, z, z, z, z, z, z, z, z, head, x, b_gate, w_dw, b_dw, ln_g, ln_b, pool_scale, w_co, w_pool, w_po, w_o)


def _ffn_fwd_bwd(h1, target, g_ffn, g_final, w_g, w_u, w_d, nb):
    tp = nb * BR

    def body(h1_ref, tgt_ref, gf_ref, gfin_ref, wg_hbm, wu_hbm, wd_hbm,
             dh1_ref, dh1b_ref, vb_ref, fb_ref, dgb_ref, dub_ref, dh2b_ref, loss_ref, dgf_ref, dgfin_ref,
             wg_ref, wu_ref, wd_ref, sem):
        i = pl.program_id(0)

        @pl.when(i == 0)
        def _():
            copies = [pltpu.make_async_copy(wg_hbm, wg_ref, sem.at[0]),
                      pltpu.make_async_copy(wu_hbm, wu_ref, sem.at[1]),
                      pltpu.make_async_copy(wd_hbm, wd_ref, sem.at[2])]
            for cp in copies:
                cp.start()
            loss_ref[...] = jnp.zeros_like(loss_ref)
            dgf_ref[...] = jnp.zeros_like(dgf_ref)
            dgfin_ref[...] = jnp.zeros_like(dgfin_ref)
            for cp in copies:
                cp.wait()

        h1 = h1_ref[...]
        r1 = lax.rsqrt(jnp.mean(h1 * h1, axis=-1, keepdims=True) + RMS_EPS)
        vn = h1 * r1
        vb = (vn * gf_ref[...]).astype(BF16)
        vb_ref[...] = vb
        g = _dot(vb, wg_ref[...])
        up = _dot(vb, wu_ref[...])
        sg = _sigmoid(g)
        sl = g * sg
        fb = (sl * up).astype(BF16)
        fb_ref[...] = fb
        h2 = h1 + _dot(fb, wd_ref[...])
        r2 = lax.rsqrt(jnp.mean(h2 * h2, axis=-1, keepdims=True) + RMS_EPS)
        yn = h2 * r2
        valid = i > 0
        diff = jnp.where(valid, yn * gfin_ref[...] - tgt_ref[...], 0.0)
        loss_ref[...] += 0.5 * jnp.sum(jnp.mean(diff * diff, axis=-1, keepdims=True))
        dy = diff * (1.0 / D)
        dgfin_ref[...] += jnp.sum(dy * yn, axis=0, keepdims=True)
        gd = dy * gfin_ref[...]
        dh2 = r2 * (gd - yn * jnp.mean(yn * gd, axis=-1, keepdims=True))
        dh2b = dh2.astype(BF16)
        dh2b_ref[...] = dh2b
        df = _dot_nt(dh2b, wd_ref[...])
        dub = (df * sl).astype(BF16)
        dgb = (df * up * (sg * (1.0 + g * (1.0 - sg)))).astype(BF16)
        dub_ref[...] = dub
        dgb_ref[...] = dgb
        dv = _dot_nt(dgb, wg_ref[...]) + _dot_nt(dub, wu_ref[...])
        dgf_ref[...] += jnp.sum(dv * vn, axis=0, keepdims=True)
        gd1 = dv * gf_ref[...]
        dh1 = dh2 + r1 * (gd1 - vn * jnp.mean(vn * gd1, axis=-1, keepdims=True))
        dh1_ref[...] = dh1
        dh1b_ref[...] = dh1.astype(BF16)

    any_spec = pl.BlockSpec(memory_space=pl.ANY)
    return pl.pallas_call(
        body, name="ffn_fwd_bwd",
        grid=(nb,),
        in_specs=[_row_spec(), _x_spec(), _const_spec((1, D)), _const_spec((1, D)), any_spec, any_spec, any_spec],
        out_specs=[_row_spec(), _row_spec(), _row_spec(), _row_spec(D_FF), _row_spec(D_FF), _row_spec(D_FF), _row_spec(),
                   _const_spec((1, 1)), _const_spec((1, D)), _const_spec((1, D))],
        out_shape=[jax.ShapeDtypeStruct((tp, D), F32), jax.ShapeDtypeStruct((tp, D), BF16),
                   jax.ShapeDtypeStruct((tp, D), BF16), jax.ShapeDtypeStruct((tp, D_FF), BF16),
                   jax.ShapeDtypeStruct((tp, D_FF), BF16), jax.ShapeDtypeStruct((tp, D_FF), BF16),
                   jax.ShapeDtypeStruct((tp, D), BF16), jax.ShapeDtypeStruct((1, 1), F32),
                   jax.ShapeDtypeStruct((1, D), F32), jax.ShapeDtypeStruct((1, D), F32)],
        scratch_shapes=[pltpu.VMEM((D, D_FF), BF16), pltpu.VMEM((D, D_FF), BF16), pltpu.VMEM((D_FF, D), BF16),
                        pltpu.SemaphoreType.DMA((3,))],
        compiler_params=_cparams(("arbitrary",)),
    )(h1, target, g_ffn, g_final, w_g, w_u, w_d)


def _mixers_bwd_rows(dh1b, yc, yp, z, b_gate, cpre, ln_g, ln_b, mw, pool_scale, w_o, w_co, w_po, w_pool, dep, nb):
    tp = nb * BR

    def body(dh1b_ref, yc_ref, yp_ref, za, zb, bg_ref, cpre_ref, lng_ref, lnb_ref, mw_ref, ps_ref,
             wo_ref, wco_ref, wpo_ref, wpool_ref, dep_ref,
             dycb_ref, dypb_ref, dzg_ref, dconv_ref, dmwb_ref, dm_ref, dbg_ref, dlng_ref, dlnb_ref, dbdw_ref, dps_ref):
        i = pl.program_id(0)

        @pl.when(i == 0)
        def _():
            for r in (dbg_ref, dlng_ref, dlnb_ref, dbdw_ref, dps_ref):
                r[...] = jnp.zeros_like(r)

        dmg = _dot_nt(dh1b_ref[...], wo_ref[...])
        s_a = _sigmoid(za[...] + bg_ref[:, 0:D])
        s_b = _sigmoid(zb[...] + bg_ref[:, D:2 * D])
        dycb = (dmg * s_a).astype(BF16)
        dypb = (dmg * s_b).astype(BF16)
        dycb_ref[...] = dycb
        dypb_ref[...] = dypb
        dza = dmg * yc_ref[...] * (s_a * (1.0 - s_a))
        dzb = dmg * yp_ref[...] * (s_b * (1.0 - s_b))
        dzg_ref[:, 0:D] = dza.astype(BF16)
        dzg_ref[:, D:2 * D] = dzb.astype(BF16)
        dbg_ref[:, 0:D] += jnp.sum(dza, axis=0, keepdims=True)
        dbg_ref[:, D:2 * D] += jnp.sum(dzb, axis=0, keepdims=True)

        dca = _dot_nt(dycb, wco_ref[...])
        conv = cpre_ref[...]
        mu = jnp.mean(conv, axis=-1, keepdims=True)
        xc = conv - mu
        rstd = lax.rsqrt(jnp.mean(xc * xc, axis=-1, keepdims=True) + LN_EPS)
        xhat = xc * rstd
        ln = xhat * lng_ref[...] + lnb_ref[...]
        sg = _sigmoid(ln)
        dln = dca * (sg * (1.0 + ln * (1.0 - sg)))
        dlng_ref[...] += jnp.sum(dln * xhat, axis=0, keepdims=True)
        dlnb_ref[...] += jnp.sum(dln, axis=0, keepdims=True)
        dxh = dln * lng_ref[...]
        dconv = rstd * (dxh - jnp.mean(dxh, axis=-1, keepdims=True)
                        - xhat * jnp.mean(dxh * xhat, axis=-1, keepdims=True))
        dconv_ref[...] = dconv
        dbdw_ref[...] += jnp.sum(dconv, axis=0, keepdims=True)

        dm2 = _dot_nt(dypb, wpo_ref[...])
        dps_ref[...] += jnp.sum(dm2 * mw_ref[...], axis=0, keepdims=True)
        dmwb = (dm2 * ps_ref[...]).astype(BF16)
        dmwb_ref[...] = dmwb
        for gi in range(len(POOL_WINDOWS)):
            lanes = slice(gi * GD, (gi + 1) * GD)
            dm_ref[:, lanes] = _dot_nt(dmwb[:, lanes], wpool_ref[gi])

    in_specs = [_row_spec(), _row_spec(), _row_spec(),
                pl.BlockSpec((BR, D), lambda i: (i, 3)), pl.BlockSpec((BR, D), lambda i: (i, 4)),
                _const_spec((1, 2 * D)), _row_spec(), _const_spec((1, D)), _const_spec((1, D)), _row_spec(),
                _const_spec((1, D)), _const_spec((D, D)), _const_spec((D, D)), _const_spec((D, D)),
                _const_spec((4, GD, GD)), _const_spec((8, 128))]
    return pl.pallas_call(
        body, name="mixers_bwd_rows",
        grid=(nb,),
        in_specs=in_specs,
        out_specs=[_row_spec(), _row_spec(), _row_spec(2 * D), _row_spec(), _row_spec(), _row_spec(),
                   _const_spec((1, 2 * D)), _const_spec((1, D)), _const_spec((1, D)), _const_spec((1, D)),
                   _const_spec((1, D))],
        out_shape=[jax.ShapeDtypeStruct((tp, D), BF16), jax.ShapeDtypeStruct((tp, D), BF16),
                   jax.ShapeDtypeStruct((tp, 2 * D), BF16), jax.ShapeDtypeStruct((tp, D), F32),
                   jax.ShapeDtypeStruct((tp, D), BF16), jax.ShapeDtypeStruct((tp, D), F32),
                   jax.ShapeDtypeStruct((1, 2 * D), F32), jax.ShapeDtypeStruct((1, D), F32),
                   jax.ShapeDtypeStruct((1, D), F32), jax.ShapeDtypeStruct((1, D), F32),
                   jax.ShapeDtypeStruct((1, D), F32)],
        compiler_params=_cparams(("arbitrary",)),
    )(dh1b, yc, yp, z, z, b_gate, cpre, ln_g, ln_b, mw, pool_scale, w_o, w_co, w_po, w_pool, dep)


def _mixers_bwd_halo(dconv, dm, z, dzg, w_dw, head, x, g_mix, dh1, w_in_b, dep, nb, t_total):
    tp = nb * BR
    ns = w_in_b.shape[0]
    wcols = w_in_b.shape[2]
    seq = x.shape[0]

    def body(dcp, dcc, dcn, dmp, dmc, dmn, avp, av, avn, agp, ag, agn, dzg_ref, wdw_ref, head_ref, x_ref, g_ref,
             dh1_ref, w_hbm, dep_ref,
             dzb_ref, gx_ref, dhead_ref, dwdw_ref, dgmix_ref,
             w_ref, sem, aext_ref, dext_ref, qext_ref, da_ref, rot_ref, dwp_ref):
        i = pl.program_id(0)

        @pl.when(i == 0)
        def _():
            cp = pltpu.make_async_copy(w_hbm, w_ref, sem.at[0])
            cp.start()
            dwp_ref[...] = jnp.zeros_like(dwp_ref)
            dgmix_ref[...] = jnp.zeros_like(dgmix_ref)
            cp.wait()

        sig_g = _sigmoid(ag[...])
        _fill_ext(aext_ref, avp[...] * _sigmoid(agp[...]), av[...] * sig_g, avn[...] * _sigmoid(agn[...]), i, nb)
        _fill_ext(dext_ref, dcp[...], dcc[...], dcn[...], i, nb)
        _fill_ext(qext_ref, dmp[...], dmc[...], dmn[...], i, nb)

        def conv_chunk(c, carry):
            lanes = pl.ds(pl.multiple_of(c * 128, 128), 128)
            _fill_rot(rot_ref, dext_ref, lanes)
            acc = jnp.zeros((BR, 128), F32)
            for k in range(KW):
                acc = acc + wdw_ref[k:k + 1, lanes] * _tap(rot_ref, dext_ref, lanes, KW - k)
            da_ref[:, lanes] = acc
            _fill_rot(rot_ref, aext_ref, lanes)
            dcv = dext_ref[HALO:HALO + BR, lanes]
            for k in range(KW):
                prod = _tap(rot_ref, aext_ref, lanes, 1 + k) * dcv
                dwp_ref[k, :, lanes] += jnp.sum(prod.reshape(BR // 8, 8, 128), axis=0)
            return carry
        lax.fori_loop(0, D // 128, conv_chunk, 0)

        @pl.when(i == nb - 1)
        def _():
            dwdw_ref[...] = jnp.sum(dwp_ref[...], axis=1)

        da = da_ref[...]
        a_val = av[...]
        dzb_ref[:, 0:D] = (da * sig_g).astype(BF16)
        dzb_ref[:, D:2 * D] = (da * a_val * (sig_g * (1.0 - sig_g))).astype(BF16)

        t_ext = _row_ids(i, EXT, -HALO)
        for gi, w in enumerate(POOL_WINDOWS):
            left = w // 2
            right = w - 1 - left
            lanes = slice(gi * GD, (gi + 1) * GD)
            qext_ref[:, lanes] = qext_ref[:, lanes] / _pool_cnt(t_ext, w, t_total)
            s = qext_ref[pl.ds(HALO - right, BR), lanes]
            for j in range(-right + 1, left + 1):
                s = s + qext_ref[pl.ds(HALO + j, BR), lanes]
            dzb_ref[:, 2 * D + gi * GD:2 * D + (gi + 1) * GD] = (s - dmc[:, lanes]).astype(BF16)
        dzb_ref[:, 3 * D:5 * D] = dzg_ref[...]

        du = _dot_nt(dzb_ref[:, 0:wcols], w_ref[0])
        for s_i in range(1, ns):
            du = du + _dot_nt(dzb_ref[:, s_i * wcols:(s_i + 1) * wcols], w_ref[s_i])
        h0 = jnp.where(i == 0, head_ref[...], x_ref[...])
        r0 = lax.rsqrt(jnp.mean(h0 * h0, axis=-1, keepdims=True) + RMS_EPS)
        un = h0 * r0
        dgmix_ref[...] += jnp.sum(du * un, axis=0, keepdims=True)
        gd = du * g_ref[...]
        dh0 = dh1_ref[...] + r0 * (gd - un * jnp.mean(un * gd, axis=-1, keepdims=True))
        gx_ref[...] = dh0

        @pl.when(i == 0)
        def _():
            dhead_ref[...] = dh0

    any_spec = pl.BlockSpec(memory_space=pl.ANY)
    in_specs = (_halo_specs(0, nb) + _halo_specs(0, nb) + _halo_specs(0, nb) + _halo_specs(1, nb)
                + [_row_spec(2 * D), _const_spec((32, D)), _const_spec((BR, D)), _x_spec(), _const_spec((1, D)),
                   _row_spec(), any_spec, _const_spec((8, 128))])
    return pl.pallas_call(
        body, name="mixers_bwd_halo",
        grid=(nb,),
        in_specs=in_specs,
        out_specs=[_row_spec(D_IN), _x_spec(), _const_spec((BR, D)), _const_spec((32, D)), _const_spec((1, D))],
        out_shape=[jax.ShapeDtypeStruct((tp, D_IN), BF16), jax.ShapeDtypeStruct((seq, D), F32),
                   jax.ShapeDtypeStruct((BR, D), F32), jax.ShapeDtypeStruct((32, D), F32),
                   jax.ShapeDtypeStruct((1, D), F32)],
        scratch_shapes=[pltpu.VMEM((ns, D, wcols), BF16), pltpu.SemaphoreType.DMA((1,)),
                        pltpu.VMEM((EXT, D), F32), pltpu.VMEM((EXT, D), F32), pltpu.VMEM((EXT, D), F32),
                        pltpu.VMEM((BR, D), F32), pltpu.VMEM((8, ROT_ROWS, 128), F32), pltpu.VMEM((32, 8, D), F32)],
        compiler_params=_cparams(("arbitrary",)),
    )(dconv, dconv, dconv, dm, dm, dm, z, z, z, z, z, z, dzg, w_dw, head, x, g_mix, dh1, w_in_b, dep)


def _wgrad(a, c, tm, tn, tk, name, diag=False, col_major=False):
    tp, m = a.shape
    n = c.shape[1]
    nk = tp // tk
    gm, gn = m // tm, n // tn

    def body(a_ref, c_ref, o_ref, ob_ref):
        k = pl.program_id(2)

        @pl.when(k == 0)
        def _():
            o_ref[...] = jnp.zeros_like(o_ref)

        o_ref[...] += _dot_tn(a_ref[...], c_ref[...])

        @pl.when(k == nk - 1)
        def _():
            ob_ref[...] = o_ref[...].astype(BF16)

    c_map = lambda i, j, k: (k, j)
    grid = (gm, gn, nk)
    if diag:
        grid = (gm, 1, nk)
        c_map = lambda i, j, k: (k, i)
        o_spec = pl.BlockSpec((tm, tn), lambda i, j, k: (i, 0))
        o_shape = (m, tn)
    elif col_major:
        o_spec = pl.BlockSpec((None, tm, tn), lambda i, j, k: (j, i, 0))
        o_shape = (gn, m, tn)
    else:
        o_spec = pl.BlockSpec((tm, tn), lambda i, j, k: (i, j))
        o_shape = (m, n)
    return pl.pallas_call(
        body, name=name,
        grid=grid,
        in_specs=[pl.BlockSpec((tk, tm), lambda i, j, k: (k, i)), pl.BlockSpec((tk, tn), c_map)],
        out_specs=[o_spec, o_spec],
        out_shape=[jax.ShapeDtypeStruct(o_shape, F32), jax.ShapeDtypeStruct(o_shape, BF16)],
        compiler_params=_cparams(("arbitrary", "arbitrary", "arbitrary")),
    )(a, c)


def _place():
    x, y, c = lax.axis_index("x"), lax.axis_index("y"), lax.axis_index("c")
    others = [(1 - x, y), (x, 1 - y), (1 - x, 1 - y)]
    return x, y, c, others


def _split2(a, axis=0):
    return a.reshape(a.shape[:axis] + (2, a.shape[axis] // 2) + a.shape[axis + 1:])


def _merge2(a, axis=0):
    return a.reshape(a.shape[:axis] + (2 * a.shape[axis + 1],) + a.shape[axis + 2:])


def _cast_into_slot(w2d, chip, name):
    r, c = w2d.shape
    r2 = r // 2

    def body(chip_ref, w_ref, o_ref):
        o_ref[...] = w_ref[...].astype(BF16)

    return pl.pallas_call(
        body, name=name,
        grid_spec=pltpu.PrefetchScalarGridSpec(
            num_scalar_prefetch=1, grid=(2,),
            in_specs=[pl.BlockSpec((r2, c), lambda h, chip_ref: (h, 0))],
            out_specs=pl.BlockSpec((None, None, r2, c), lambda h, chip_ref: (chip_ref[0], h, 0, 0))),
        out_shape=jax.ShapeDtypeStruct((N_SHARD, 2, r2, c), BF16),
        compiler_params=_cparams(("arbitrary",)),
    )(chip, w2d)


HBM_SPEC = pl.BlockSpec(memory_space=pltpu.HBM)
SEM_SPEC = pl.BlockSpec(memory_space=pltpu.SEMAPHORE)
DATAFLOW = pltpu.SideEffectType.DATAFLOW_SIDE_EFFECTING
TOKEN = jax.ShapeDtypeStruct((8, 128), F32)


def _in_hbm(a):
    return pltpu.with_memory_space_constraint(a, pltpu.HBM)


def _gather_tiny(tiny):
    nt = len(tiny)
    any_spec = pl.BlockSpec(memory_space=pl.ANY)

    def body(*refs):
        tins, touts = refs[:nt], refs[nt:2 * nt]
        send_sems, recv_sems, local_sems = refs[2 * nt:]
        x, y, c, others = _place()
        mine = 2 * x + y
        local = [pltpu.make_async_copy(tins[a], touts[a].at[mine], local_sems.at[a]) for a in range(nt)]
        sends = [pltpu.make_async_remote_copy(
            src_ref=tins[a], dst_ref=touts[a].at[mine], send_sem=send_sems.at[a * 3 + j],
            recv_sem=recv_sems.at[a * 3 + j], device_id=(*chip, c), device_id_type=MESH)
            for a in range(nt) for j, chip in enumerate(others)]
        for cp in local + sends:
            cp.start()
        for a in range(nt):
            for j, chip in enumerate(others):
                landed = touts[a].at[2 * chip[0] + chip[1]]
                pltpu.make_async_remote_copy(
                    src_ref=landed, dst_ref=landed, send_sem=send_sems.at[a * 3 + j], recv_sem=recv_sems.at[a * 3 + j],
                    device_id=(x, y, c), device_id_type=MESH).wait_recv()
        for cp in sends:
            cp.wait_send()
        for cp in local:
            cp.wait()

    return pl.pallas_call(
        body, name="gather_tiny",
        in_specs=[any_spec] * nt, out_specs=[any_spec] * nt,
        out_shape=[jax.ShapeDtypeStruct((N_SHARD,) + a.shape, a.dtype) for a in tiny],
        scratch_shapes=[pltpu.SemaphoreType.DMA((3 * nt,)), pltpu.SemaphoreType.DMA((3 * nt,)),
                        pltpu.SemaphoreType.DMA((nt,))],
    )(*tiny)


def _ici_copies(srcs, dsts, send_sems, recv_sems, started):
    x, y, c, others = _place()
    mine = 2 * x + y
    copies = []
    for a in range(len(srcs)):
        for j, chip in enumerate(others):
            there = 2 * chip[0] + chip[1]
            src, dst = srcs[a](mine, there, c), dsts[a](mine, there, c)
            if not started:
                dst = dsts[a](there, mine, c)
            copies.append(pltpu.make_async_remote_copy(
                src_ref=src, dst_ref=dst, send_sem=send_sems.at[a * 3 + j], recv_sem=recv_sems.at[a * 3 + j],
                device_id=(*chip, c), device_id_type=MESH))
    return copies


def _split_start(srcs_of, dsts_of, arrays, n_src, name):
    n = len(arrays)

    def body(*refs):
        ins = refs[:n]
        send_sems, recv_sems = refs[n], refs[n + 1]
        token = refs[2 * n + 2]
        for cp in _ici_copies(srcs_of(ins), dsts_of(ins), send_sems, recv_sems, True):
            cp.start()
        token[...] = jnp.zeros_like(token)

    out = pl.pallas_call(
        body, name=name,
        in_specs=[HBM_SPEC] * n,
        out_specs=(SEM_SPEC, SEM_SPEC, *([HBM_SPEC] * n), pl.BlockSpec(memory_space=pltpu.VMEM)),
        out_shape=(pltpu.SemaphoreType.DMA((3 * n_src,)), pltpu.SemaphoreType.DMA((3 * n_src,)),
                   *[pltpu.HBM(a.shape, a.dtype) for a in arrays], TOKEN),
        input_output_aliases={a: 2 + a for a in range(n)},
        compiler_params=pltpu.CompilerParams(has_side_effects=DATAFLOW),
    )(*[_in_hbm(a) for a in arrays])
    return out[0], out[1], list(out[2:2 + n]), out[2 + n]


def _split_wait(srcs_of, dsts_of, send_sems, recv_sems, arrays, after, name):
    n = len(arrays)

    def body(*refs):
        ins = refs[:n]
        send_sems, recv_sems = refs[n], refs[n + 1]
        for cp in _ici_copies(srcs_of(ins), dsts_of(ins), send_sems, recv_sems, False):
            cp.wait_send()
            cp.wait_recv()

    return pl.pallas_call(
        body, name=name,
        in_specs=[HBM_SPEC] * n + [SEM_SPEC, SEM_SPEC, pl.BlockSpec(memory_space=pl.ANY)],
        out_specs=[HBM_SPEC] * n,
        out_shape=[pltpu.HBM(a.shape, a.dtype) for a in arrays],
        input_output_aliases={a: a for a in range(n)},
        compiler_params=pltpu.CompilerParams(has_side_effects=DATAFLOW),
    )(*arrays, send_sems, recv_sems, after)


def _gather_views(ins):
    view = [lambda frm, to, c, r=r: r.at[frm, c] for r in ins]
    return view


def _gather_start(bufs, name):
    return _split_start(_gather_views, _gather_views, bufs, len(bufs), name)


def _gather_wait(send_sems, recv_sems, bufs, after, name):
    return _split_wait(_gather_views, _gather_views, send_sems, recv_sems, bufs, after, name)


def _forward_halves(bufs, name):
    n = len(bufs)
    any_spec = pl.BlockSpec(memory_space=pl.ANY)

    def body(*refs):
        outs = refs[n:2 * n]
        send_sems, recv_sems = refs[2 * n:]
        x, y, c, others = _place()
        copies = []
        for a in range(n):
            for j, chip in enumerate(others):
                landed = outs[a].at[2 * chip[0] + chip[1], c]
                copies.append(pltpu.make_async_remote_copy(
                    src_ref=landed, dst_ref=landed, send_sem=send_sems.at[a * 3 + j], recv_sem=recv_sems.at[a * 3 + j],
                    device_id=(x, y, 1 - c), device_id_type=MESH))
        for cp in copies:
            cp.start()
        for a in range(n):
            for j, chip in enumerate(others):
                landed = outs[a].at[2 * chip[0] + chip[1], 1 - c]
                pltpu.make_async_remote_copy(
                    src_ref=landed, dst_ref=landed, send_sem=send_sems.at[a * 3 + j], recv_sem=recv_sems.at[a * 3 + j],
                    device_id=(x, y, c), device_id_type=MESH).wait_recv()
        for cp in copies:
            cp.wait_send()

    out = pl.pallas_call(
        body, name=name,
        in_specs=[any_spec] * n, out_specs=[any_spec] * n,
        out_shape=[jax.ShapeDtypeStruct(b.shape, b.dtype) for b in bufs],
        input_output_aliases={a: a for a in range(n)},
        scratch_shapes=[pltpu.SemaphoreType.DMA((3 * n,)), pltpu.SemaphoreType.DMA((3 * n,))],
    )(*bufs)
    return [_merge2(o, 1) for o in out]


def _swap_halves_bf16(gbs, name):
    n = len(gbs)
    any_spec = pl.BlockSpec(memory_space=pl.ANY)

    def body(*refs):
        ins, outs = refs[:n], refs[n:2 * n]
        send_sems, recv_sems = refs[2 * n:]
        x, y, c, _ = _place()
        copies = []
        for a in range(n):
            copies.append(pltpu.make_async_remote_copy(
                src_ref=ins[a].at[:, 1 - c], dst_ref=outs[a], send_sem=send_sems.at[a], recv_sem=recv_sems.at[a],
                device_id=(x, y, 1 - c), device_id_type=MESH))
        for cp in copies:
            cp.start()
        for cp in copies:
            cp.wait()

    return pl.pallas_call(
        body, name=name,
        in_specs=[any_spec] * n, out_specs=[any_spec] * n,
        out_shape=[jax.ShapeDtypeStruct((g.shape[0], g.shape[1] // 2, g.shape[2]), g.dtype) for g in gbs],
        scratch_shapes=[pltpu.SemaphoreType.DMA((n,)), pltpu.SemaphoreType.DMA((n,))],
    )(*[_split2(g, 1) for g in gbs])


def _scatter_srcs(n):
    return lambda ins: [lambda frm, to, c, r=r: r.at[to] for r in ins[:n]]


def _scatter_dsts(n):
    return lambda ins: [lambda frm, to, c, r=r: r.at[frm] for r in ins[n:]]


def _scatter_start(hbs, name):
    n = len(hbs)
    lands = [lax.empty(h.shape, h.dtype) for h in hbs]
    return _split_start(_scatter_srcs(n), _scatter_dsts(n), list(hbs) + lands, n, name)


def _scatter_wait(send_sems, recv_sems, arrays, after, name):
    n = len(arrays) // 2
    return _split_wait(_scatter_srcs(n), _scatter_dsts(n), send_sems, recv_sems, arrays, after, name)[n:]


def _join_halves(rhs):
    n = len(rhs)
    any_spec = pl.BlockSpec(memory_space=pl.ANY)

    def body(*refs):
        outs = refs[n:2 * n]
        send_sems, recv_sems = refs[2 * n:]
        x, y, c, _ = _place()
        copies = []
        for a in range(n):
            copies.append(pltpu.make_async_remote_copy(
                src_ref=outs[a].at[c], dst_ref=outs[a].at[c], send_sem=send_sems.at[a],
                recv_sem=recv_sems.at[a], device_id=(x, y, 1 - c), device_id_type=MESH))
        for cp in copies:
            cp.start()
        for a in range(n):
            landed = outs[a].at[1 - c]
            pltpu.make_async_remote_copy(
                src_ref=landed, dst_ref=landed, send_sem=send_sems.at[a], recv_sem=recv_sems.at[a],
                device_id=(x, y, c), device_id_type=MESH).wait_recv()
        for cp in copies:
            cp.wait_send()

    out = pl.pallas_call(
        body, name="join_halves",
        in_specs=[any_spec] * n, out_specs=[any_spec] * n,
        out_shape=[jax.ShapeDtypeStruct(r.shape, r.dtype) for r in rhs],
        input_output_aliases={a: a for a in range(n)},
        scratch_shapes=[pltpu.SemaphoreType.DMA((n,)), pltpu.SemaphoreType.DMA((n,))],
    )(*rhs)
    return [_merge2(o) for o in out]


def _allreduce_small(v):
    rows, cols = v.shape
    vm = pl.BlockSpec(memory_space=pltpu.VMEM)
    flips = [(dx, dy, dc) for dx in (0, 1) for dy in (0, 1) for dc in (0, 1)][1:]

    def body(v_ref, out_ref, buf_ref, send_sems, recv_sems):
        x, y, c, _ = _place()
        mine = 4 * x + 2 * y + c
        copies = []
        for k, (dx, dy, dc) in enumerate(flips):
            px, py, pc = jnp.bitwise_xor(x, dx), jnp.bitwise_xor(y, dy), jnp.bitwise_xor(c, dc)
            copies.append(pltpu.make_async_remote_copy(
                src_ref=v_ref, dst_ref=buf_ref.at[mine], send_sem=send_sems.at[k], recv_sem=recv_sems.at[k],
                device_id=(px, py, pc), device_id_type=MESH))
        for cp in copies:
            cp.start()
        buf_ref[mine] = v_ref[...]
        for k, (dx, dy, dc) in enumerate(flips):
            src = 4 * jnp.bitwise_xor(x, dx) + 2 * jnp.bitwise_xor(y, dy) + jnp.bitwise_xor(c, dc)
            pltpu.make_async_remote_copy(
                src_ref=v_ref, dst_ref=buf_ref.at[src], send_sem=send_sems.at[k], recv_sem=recv_sems.at[k],
                device_id=(x, y, c), device_id_type=MESH).wait_recv()
        for cp in copies:
            cp.wait_send()
        acc = buf_ref[0]
        for d in range(1, 8):
            acc = acc + buf_ref[d]
        out_ref[...] = acc

    return pl.pallas_call(
        body, name="allreduce_small",
        in_specs=[vm], out_specs=vm,
        out_shape=jax.ShapeDtypeStruct((rows, cols), F32),
        scratch_shapes=[pltpu.VMEM((8, rows, cols), F32), pltpu.SemaphoreType.DMA((7,)), pltpu.SemaphoreType.DMA((7,))],
    )(v)


def _row_block(r):
    for cand in (512, 352, 256, 128, 64, 48, 16):
        if r % cand == 0:
            return cand
    return r


def _add_sibling_half(g, sb, core, name):
    ns, r, c = g.shape
    r2 = r // 2

    def body(core_ref, g_ref, sb_ref, h_ref, hb_ref):
        h = g_ref[...] + sb_ref[...].astype(F32)
        h_ref[...] = h
        hb_ref[...] = h.astype(BF16)

    spec = pl.BlockSpec((None, r2, c), lambda s, core_ref: (s, 0, 0))
    return pl.pallas_call(
        body, name=name,
        grid_spec=pltpu.PrefetchScalarGridSpec(
            num_scalar_prefetch=1, grid=(ns,),
            in_specs=[pl.BlockSpec((None, r2, c), lambda s, core_ref: (s, core_ref[0], 0)), spec],
            out_specs=[spec, spec]),
        out_shape=[jax.ShapeDtypeStruct((ns, r2, c), F32), jax.ShapeDtypeStruct((ns, r2, c), BF16)],
        compiler_params=_cparams(("arbitrary",)),
    )(core, g, sb)


def _add_chip_slabs(h, rb, idx, name):
    ns, r2, c = h.shape

    def body(idx_ref, h_ref, r0_ref, r1_ref, r2_ref, o_ref):
        o_ref[...] = ((h_ref[...] + r0_ref[...].astype(F32)) + r1_ref[...].astype(F32)) + r2_ref[...].astype(F32)

    def pick(k):
        return pl.BlockSpec((None, r2, c), lambda i, idx_ref: (idx_ref[k], 0, 0))

    return pl.pallas_call(
        body, name=name,
        grid_spec=pltpu.PrefetchScalarGridSpec(
            num_scalar_prefetch=1, grid=(1,),
            in_specs=[pick(0), pick(1), pick(2), pick(3)],
            out_specs=pl.BlockSpec((None, r2, c), lambda i, idx_ref: (idx_ref[4], 0, 0))),
        out_shape=jax.ShapeDtypeStruct((2, r2, c), F32),
        compiler_params=_cparams(("arbitrary",)),
    )(idx, h, rb, rb, rb)


def _adamw(g, w, m, v, name):
    r, c = g.shape
    br = _row_block(r)

    def body(g_ref, w_ref, m_ref, v_ref, d_ref, nm_ref, nv_ref):
        gg = g_ref[...]
        nm = B1 * m_ref[...] + (1.0 - B1) * gg
        nv = B2 * v_ref[...] + (1.0 - B2) * jnp.square(gg)
        m_hat = nm / (1.0 - B1 ** STEP)
        v_hat = nv / (1.0 - B2 ** STEP)
        d_ref[...] = -LR * (m_hat / (jnp.sqrt(v_hat) + ADAM_EPS) + WD * w_ref[...])
        nm_ref[...] = nm
        nv_ref[...] = nv

    spec = pl.BlockSpec((br, c), lambda i: (i, 0))
    return pl.pallas_call(
        body, name=name,
        grid=(r // br,),
        in_specs=[spec] * 4, out_specs=[spec] * 3,
        out_shape=[jax.ShapeDtypeStruct((r, c), F32)] * 3,
        compiler_params=_cparams(("arbitrary",)),
    )(g, w, m, v)


BIG = ("w_in", "w_conv_out", "w_pool", "w_pool_out", "w_o", "w_ffn_gate", "w_ffn_up", "w_ffn_down")
REPL = ("g_mix", "b_gate", "b_dw", "ln_g", "ln_b", "pool_scale", "g_ffn", "g_final")
GROUP_MIX = ("w_conv_out", "w_pool", "w_pool_out", "w_o")
GROUP_FFN = ("w_ffn_gate", "w_ffn_up", "w_ffn_down")
WEIGHT_ORDER = ("meta_tokens", "g_mix", "w_in", "b_gate", "w_dw", "b_dw", "ln_g", "ln_b", "w_conv_out", "w_pool",
                "pool_scale", "w_pool_out", "w_o", "g_ffn", "w_ffn_gate", "w_ffn_up", "w_ffn_down", "g_final")


def _shard2d(name, a):
    a = a[0]
    if name == "w_pool":
        return a.reshape(4 * 64, GD)
    return a


def _cols_to_slabs(a):
    m, n = a.shape
    return a.reshape(m, N_SHARD, n // N_SHARD).transpose(1, 0, 2)


def _slabs_to_cols(a):
    ns, m, c = a.shape
    return a.transpose(1, 0, 2).reshape(m, ns * c)


def kernel(x, meta_tokens, g_mix, w_in, b_gate, w_dw, b_dw, ln_g, ln_b, w_conv_out, w_pool, pool_scale, w_pool_out, w_o, g_ffn, w_ffn_gate, w_ffn_up, w_ffn_down, g_final, loss_target, m_meta_tokens, m_g_mix, m_w_in, m_b_gate, m_w_dw, m_b_dw, m_ln_g, m_ln_b, m_w_conv_out, m_w_pool, m_pool_scale, m_w_pool_out, m_w_o, m_g_ffn, m_w_ffn_gate, m_w_ffn_up, m_w_ffn_down, m_g_final, v_meta_tokens, v_g_mix, v_w_in, v_b_gate, v_w_dw, v_b_dw, v_ln_g, v_ln_b, v_w_conv_out, v_w_pool, v_pool_scale, v_w_pool_out, v_w_o, v_g_ffn, v_w_ffn_gate, v_w_ffn_up, v_w_ffn_down, v_g_final):
    args = dict(locals())
    w = {n: args[n] for n in WEIGHT_ORDER}
    mom = {n: args["m_" + n] for n in WEIGHT_ORDER}
    var = {n: args["v_" + n] for n in WEIGHT_ORDER}
    seq = x.shape[1]
    nb = seq // BR + 1
    tp = nb * BR
    tk = tp // 2 if (tp // 2) % 16 == 0 else BR
    t_total = seq + N_META
    cx, cy, cc = lax.axis_index("x"), lax.axis_index("y"), lax.axis_index("c")
    chip = 2 * cx + cy
    chip1 = jnp.reshape(chip, (1,)).astype(jnp.int32)
    core = jnp.reshape(cc, (1,)).astype(jnp.int32)
    others = jnp.sort(jnp.stack([2 * (1 - cx) + cy, 2 * cx + (1 - cy), 2 * (1 - cx) + (1 - cy)]))
    idx = jnp.concatenate([chip1, others.astype(jnp.int32), core])
    xs, target = x[0], loss_target[0]

    tiny = _gather_tiny([w["w_dw"][0], w["meta_tokens"]])
    small = {n: w[n] for n in REPL if n != "g_final"}
    small["g_final"] = w["g_final"].reshape(1, D)
    small["w_dw"] = jnp.pad(_slabs_to_cols(tiny[0]), ((0, 1), (0, 0)))
    head = jnp.concatenate([jnp.zeros((PAD, D), F32), _slabs_to_cols(tiny[1])], axis=0)
    bufs = {n: _cast_into_slot(_shard2d(n, w[n]), chip1, "cast_" + n) for n in BIG}

    def gather_start(group, name):
        return _gather_start([bufs[n] for n in group], "gather_start_" + name)

    def gather_finish(group, start, after, name):
        landed = _gather_wait(start[0], start[1], start[2], after, "gather_wait_" + name)
        return dict(zip(group, _forward_halves(landed, "forward_" + name)))

    st_in = gather_start(("w_in",), "in")
    u = _rms_u(head, xs, small["g_mix"], nb)
    gw = gather_finish(("w_in",), st_in, u, "in")
    st_mix = gather_start(GROUP_MIX, "mix")
    st_ffn = gather_start(GROUP_FFN, "ffn")
    z = _in_proj(u, gw["w_in"], st_mix[3] + st_ffn[3], nb)
    gw.update(gather_finish(GROUP_MIX, st_mix, z, "mix"))
    w_pool_b = gw["w_pool"].reshape(N_SHARD, 4, 64, GD).transpose(1, 0, 2, 3).reshape(4, GD, GD)
    w_co_b, w_po_b, w_o_b = (gw[n].reshape(D, D) for n in ("w_conv_out", "w_pool_out", "w_o"))
    h1, yc, yp, mg, ca, cpre, m, mw, m2b = _mixers_fwd(
        z, head, xs, small["b_gate"], small["w_dw"], small["b_dw"], small["ln_g"], small["ln_b"],
        small["pool_scale"], w_co_b, w_pool_b, w_po_b, w_o_b, nb, t_total)
    gw.update(gather_finish(GROUP_FFN, st_ffn, h1, "ffn"))

    dh1, dh1b, vb, fb, dgb, dub, dh2b, loss, dg_ffn, dg_final = _ffn_fwd_bwd(
        h1, target, small["g_ffn"], small["g_final"], _slabs_to_cols(gw["w_ffn_gate"]),
        _slabs_to_cols(gw["w_ffn_up"]), gw["w_ffn_down"].reshape(D_FF, D), nb)
    loss = lax.psum(loss[0, 0], ("x", "y", "c"))

    def slabs(name, g):
        if name == "w_in":
            return g
        if name in ("w_ffn_gate", "w_ffn_up"):
            return _cols_to_slabs(g)
        if name == "w_pool":
            return g.reshape(4, N_SHARD, 64, GD).transpose(1, 0, 2, 3).reshape(N_SHARD, 4 * 64, GD)
        return g.reshape(N_SHARD, g.shape[0] // N_SHARD, g.shape[1])

    def reduce_start(group, grads, name):
        g32 = [slabs(n, grads[n][0]) for n in group]
        g16 = [slabs(n, grads[n][1]) for n in group]
        from_sibling = _swap_halves_bf16(g16, "swap_halves_" + name)
        halves = [_add_sibling_half(g, sb, core, "add_sibling_" + n) for n, g, sb in zip(group, g32, from_sibling)]
        return [h for h, _ in halves], _scatter_start([hb for _, hb in halves], "scatter_start_" + name)

    def reduce_finish(group, halves, start, after, name):
        from_chips = _scatter_wait(start[0], start[1], start[2], after, "scatter_wait_" + name)
        return [_add_chip_slabs(h, rb, idx, "add_chips_" + n) for n, h, rb in zip(group, halves, from_chips)]

    half_ff = D_FF // 2
    grads_ffn = {
        "w_ffn_gate": _wgrad(vb, dgb, D, half_ff, tk, "wgrad_ffn_gate"),
        "w_ffn_up": _wgrad(vb, dub, D, half_ff, tk, "wgrad_ffn_up"),
        "w_ffn_down": _wgrad(fb, dh2b, half_ff, D, tk, "wgrad_ffn_down"),
    }
    halves_ffn, sc_ffn = reduce_start(GROUP_FFN, grads_ffn, "ffn")

    dycb, dypb, dzg, dconv, dmwb, dm, db_gate, dln_g, dln_b, db_dw, dps = _mixers_bwd_rows(
        dh1b, yc, yp, z, small["b_gate"], cpre, small["ln_g"], small["ln_b"], mw, small["pool_scale"],
        w_o_b, w_co_b, w_po_b, w_pool_b, sc_ffn[3], nb)
    grads_mix = {
        "w_conv_out": _wgrad(ca, dycb, D, D, tk, "wgrad_conv_out"),
        "w_pool": _wgrad(m, dmwb, GD, GD, tk, "wgrad_pool", diag=True),
        "w_pool_out": _wgrad(m2b, dypb, D, D, tk, "wgrad_pool_out"),
        "w_o": _wgrad(mg, dh1b, D, D, tk, "wgrad_o"),
    }
    halves_mix, sc_mix = reduce_start(GROUP_MIX, grads_mix, "mix")
    dzb, grad_x, dhead, dw_dw, dg_mix = _mixers_bwd_halo(
        dconv, dm, z, dzg, small["w_dw"], head, xs, small["g_mix"], dh1, gw["w_in"], sc_mix[3], nb, t_total)
    grads_in = {"w_in": _wgrad(u, dzb, D, D_IN // N_SHARD, tk, "wgrad_in", col_major=True)}
    halves_in, sc_in = reduce_start(("w_in",), grads_in, "in")

    packed = jnp.concatenate(
        [dg_mix, db_gate.reshape(2, D), db_dw, dln_g, dln_b, dps, dg_ffn, dg_final, jnp.zeros((7, D), F32),
         dhead[PAD:], dw_dw], axis=0)
    summed = _allreduce_small(packed)

    reduced_half = dict(zip(GROUP_FFN, reduce_finish(GROUP_FFN, halves_ffn, sc_ffn, summed, "ffn")))
    reduced_half.update(zip(GROUP_MIX, reduce_finish(GROUP_MIX, halves_mix, sc_mix, summed, "mix")))
    reduced_half.update(zip(("w_in",), reduce_finish(("w_in",), halves_in, sc_in, summed, "in")))
    reduced = dict(zip(BIG, _join_halves([reduced_half[n] for n in BIG])))

    def repl_stack(d):
        return jnp.concatenate([d["g_mix"], d["b_gate"].reshape(2, D), d["b_dw"], d["ln_g"], d["ln_b"],
                                d["pool_scale"], d["g_ffn"], d["g_final"].reshape(1, D), jnp.ones((7, D), F32)], axis=0)

    def shard_stack(d):
        return jnp.concatenate([d["meta_tokens"], d["w_dw"][0], jnp.ones((1, GD), F32)], axis=0)

    g_repl = summed[0:16]
    g_shard = lax.dynamic_slice_in_dim(summed[16:64], chip * GD, GD, axis=1)
    d_repl, m_repl, v_repl = _adamw(g_repl, repl_stack(w), repl_stack(mom), repl_stack(var), "adamw_repl")
    d_shard, m_shard, v_shard = _adamw(g_shard, shard_stack(w), shard_stack(mom), shard_stack(var), "adamw_cols")

    def unpack(name, repl, shard):
        if name == "meta_tokens":
            return shard[0:N_META]
        if name == "w_dw":
            return shard[N_META:N_META + KW].reshape(1, KW, GD)
        row = {"g_mix": 0, "b_gate": 1, "b_dw": 3, "ln_g": 4, "ln_b": 5, "pool_scale": 6, "g_ffn": 7, "g_final": 8}[name]
        if name == "b_gate":
            return repl[1:3].reshape(1, 2 * D)
        if name == "g_final":
            return repl[8]
        return repl[row:row + 1]

    out_g, out_d, out_m, out_v = {}, {}, {}, {}
    for n in WEIGHT_ORDER:
        if n in BIG:
            g = reduced[n]
            d_, m_, v_ = _adamw(g, _shard2d(n, w[n]), _shard2d(n, mom[n]), _shard2d(n, var[n]), "adamw_" + n)
            shape = w[n].shape
            out_g[n], out_d[n], out_m[n], out_v[n] = (a.reshape(shape) for a in (g, d_, m_, v_))
        else:
            out_g[n] = unpack(n, g_repl, g_shard)
            out_d[n] = unpack(n, d_repl, d_shard)
            out_m[n] = unpack(n, m_repl, m_shard)
            out_v[n] = unpack(n, v_repl, v_shard)
    return (loss, grad_x[None], *[out_g[n] for n in WEIGHT_ORDER], *[out_d[n] for n in WEIGHT_ORDER],
            *[out_m[n] for n in WEIGHT_ORDER], *[out_v[n] for n in WEIGHT_ORDER])
```

```python
import functools

import jax
import jax.numpy as jnp
from jax import lax
from jax.experimental import pallas as pl
from jax.experimental.pallas import tpu as pltpu

F32 = jnp.float32
BF16 = jnp.bfloat16
MESH = pl.DeviceIdType.MESH

D = 1024
N_META = 16
KW = 31
CPAD = KW // 2
POOL_WINDOWS = (2, 4, 8, 16)
GD = 256
D_IN = 5 * D
D_FF = 2816
N_SHARD = 4
BR = 256
HALO = 16
PAD = BR - N_META
EXT = BR + 2 * HALO
RMS_EPS = 1e-6
LN_EPS = 1e-5
LR, B1, B2, ADAM_EPS, WD, STEP = 0.001, 0.9, 0.999, 1e-08, 0.01, 10
VMEM_LIMIT = 56 * 1024 * 1024


def _cparams(sem, vmem=VMEM_LIMIT):
    return pltpu.CompilerParams(dimension_semantics=sem, vmem_limit_bytes=vmem)


def _dot(a, b):
    return jnp.dot(a, b, preferred_element_type=F32)


def _dot_nt(a, b):
    return lax.dot_general(a, b, (((1,), (1,)), ((), ())), preferred_element_type=F32)


def _dot_tn(a, b):
    return lax.dot_general(a, b, (((0,), (0,)), ((), ())), preferred_element_type=F32)


def _sigmoid(x):
    return 1.0 / (1.0 + jnp.exp(-x))


def _row_ids(i, n, offset=0):
    return lax.broadcasted_iota(jnp.int32, (n, 1), 0) + (i * BR + offset - PAD)


def _pool_cnt(t, w, t_total):
    left = w // 2
    right = w - 1 - left
    lo = jnp.clip(t - left, 0, t_total)
    hi = jnp.clip(t + right + 1, 0, t_total)
    return jnp.maximum(hi - lo, 1).astype(F32)


def _halo_specs(col, nb):
    last = nb * (BR // HALO) - 1
    return [
        pl.BlockSpec((HALO, D), lambda i: (jnp.maximum(i * (BR // HALO) - 1, 0), col)),
        pl.BlockSpec((BR, D), lambda i: (i, col)),
        pl.BlockSpec((HALO, D), lambda i: (jnp.minimum((i + 1) * (BR // HALO), last), col)),
    ]


def _fill_ext(ext_ref, prev, cur, nxt, i, nb):
    ext_ref[0:HALO, :] = jnp.where(i > 0, prev, 0.0)
    ext_ref[HALO:HALO + BR, :] = cur
    ext_ref[HALO + BR:EXT, :] = jnp.where(i < nb - 1, nxt, 0.0)


ROT_ROWS = EXT - 8


def _fill_rot(rot_ref, ext_ref, lanes):
    for r in range(1, 8):
        rot_ref[r] = ext_ref[pl.ds(r, ROT_ROWS), lanes]


def _tap(rot_ref, ext_ref, lanes, offset):
    q, r = divmod(offset, 8)
    if r == 0:
        return ext_ref[pl.ds(8 * q, BR), lanes]
    return rot_ref[r, pl.ds(8 * q, BR), :]


def _row_spec(width=D):
    return pl.BlockSpec((BR, width), lambda i: (i, 0))


def _x_spec():
    return pl.BlockSpec((BR, D), lambda i: (jnp.maximum(i - 1, 0), 0))


def _const_spec(shape):
    nd = len(shape)
    return pl.BlockSpec(shape, lambda i: (0,) * nd)


def _rms_u(head, x, g_mix, nb):
    def body(head_ref, x_ref, g_ref, u_ref):
        i = pl.program_id(0)
        h = jnp.where(i == 0, head_ref[...], x_ref[...])
        r = lax.rsqrt(jnp.mean(h * h, axis=-1, keepdims=True) + RMS_EPS)
        u_ref[...] = ((h * r) * g_ref[...]).astype(BF16)

    return pl.pallas_call(
        body, name="rms_u",
        grid=(nb,),
        in_specs=[_const_spec((BR, D)), _x_spec(), _const_spec((1, D))],
        out_specs=_row_spec(),
        out_shape=jax.ShapeDtypeStruct((nb * BR, D), BF16),
        compiler_params=_cparams(("arbitrary",)),
    )(head, x, g_mix)


def _in_proj(u, w_in_b, dep, nb):
    tp = nb * BR
    ns = w_in_b.shape[0]
    wcols = w_in_b.shape[2]

    def body(u_ref, w_ref, dep_ref, z_ref):
        z_ref[...] = _dot(u_ref[...], w_ref[...])

    return pl.pallas_call(
        body, name="in_proj",
        grid=(ns, nb),
        in_specs=[
            pl.BlockSpec((BR, D), lambda s, i: (i, 0)),
            pl.BlockSpec((None, D, wcols), lambda s, i: (s, 0, 0)),
            pl.BlockSpec((8, 128), lambda s, i: (0, 0)),
        ],
        out_specs=pl.BlockSpec((BR, wcols), lambda s, i: (i, s)),
        out_shape=jax.ShapeDtypeStruct((tp, ns * wcols), F32),
        compiler_params=_cparams(("arbitrary", "arbitrary")),
    )(u, w_in_b, dep)


def _mixers_fwd(z, head, x, b_gate, w_dw, b_dw, ln_g, ln_b, pool_scale, w_co, w_pool, w_po, w_o, nb, t_total):
    tp = nb * BR

    def body(avp, av, avn, agp, ag, agn, pp, pc, pn, za, zb, head_ref, x_ref, bg_ref, wdw_ref, bdw_ref,
             lng_ref, lnb_ref, ps_ref, wco_ref, wpool_ref, wpo_ref, wo_ref,
             h1_ref, yc_ref, yp_ref, mg_ref, ca_ref, cpre_ref, m_ref, mw_ref, m2b_ref, ext_ref, pext_ref, rot_ref):
        i = pl.program_id(0)
        _fill_ext(ext_ref, avp[...] * _sigmoid(agp[...]), av[...] * _sigmoid(ag[...]),
                  avn[...] * _sigmoid(agn[...]), i, nb)
        _fill_ext(pext_ref, pp[...], pc[...], pn[...], i, nb)

        def conv_chunk(c, carry):
            lanes = pl.ds(pl.multiple_of(c * 128, 128), 128)
            _fill_rot(rot_ref, ext_ref, lanes)
            acc = jnp.broadcast_to(bdw_ref[:, lanes], (BR, 128))
            for k in range(KW):
                acc = acc + wdw_ref[k:k + 1, lanes] * _tap(rot_ref, ext_ref, lanes, 1 + k)
            cpre_ref[:, lanes] = acc
            return carry
        lax.fori_loop(0, D // 128, conv_chunk, 0)

        conv = cpre_ref[...]
        mu = jnp.mean(conv, axis=-1, keepdims=True)
        xc = conv - mu
        rstd = lax.rsqrt(jnp.mean(xc * xc, axis=-1, keepdims=True) + LN_EPS)
        ln = (xc * rstd) * lng_ref[...] + lnb_ref[...]
        cact = (ln * _sigmoid(ln)).astype(BF16)
        ca_ref[...] = cact
        y_conv = _dot(cact, wco_ref[...])
        yc_ref[...] = y_conv

        t = _row_ids(i, BR)
        for gi, w in enumerate(POOL_WINDOWS):
            left = w // 2
            right = w - 1 - left
            lanes = slice(gi * GD, (gi + 1) * GD)
            s = pext_ref[pl.ds(HALO - left, BR), lanes]
            for j in range(-left + 1, right + 1):
                s = s + pext_ref[pl.ds(HALO + j, BR), lanes]
            m = (s / _pool_cnt(t, w, t_total) - pext_ref[HALO:HALO + BR, lanes]).astype(BF16)
            m_ref[:, lanes] = m
            mw_ref[:, lanes] = _dot(m, wpool_ref[gi])
        mw = mw_ref[...]
        m2b = (mw * ps_ref[...]).astype(BF16)
        m2b_ref[...] = m2b
        y_pool = _dot(m2b, wpo_ref[...])
        yp_ref[...] = y_pool

        s_a = _sigmoid(za[...] + bg_ref[:, 0:D])
        s_b = _sigmoid(zb[...] + bg_ref[:, D:2 * D])
        merged = (s_a * y_conv + s_b * y_pool).astype(BF16)
        mg_ref[...] = merged
        h0 = jnp.where(i == 0, head_ref[...], x_ref[...])
        h1_ref[...] = h0 + _dot(merged, wo_ref[...])

    in_specs = (_halo_specs(0, nb) + _halo_specs(1, nb) + _halo_specs(2, nb)
                + [pl.BlockSpec((BR, D), lambda i: (i, 3)), pl.BlockSpec((BR, D), lambda i: (i, 4)),
                   _const_spec((BR, D)), _x_spec(), _const_spec((1, 2 * D)), _const_spec((32, D)),
                   _const_spec((1, D)), _const_spec((1, D)), _const_spec((1, D)), _const_spec((1, D)),
                   _const_spec((D, D)), _const_spec((4, GD, GD)), _const_spec((D, D)), _const_spec((D, D))])
    outs = [(F32, "h1"), (F32, "yc"), (F32, "yp"), (BF16, "mg"), (BF16, "ca"), (F32, "cpre"), (BF16, "m"), (F32, "mw"),
            (BF16, "m2b")]
    return pl.pallas_call(
        body, name="mixers_fwd",
        grid=(nb,),
        in_specs=in_specs,
        out_specs=[_row_spec() for _ in outs],
        out_shape=[jax.ShapeDtypeStruct((tp, D), dt) for dt, _ in outs],
        scratch_shapes=[pltpu.VMEM((EXT, D), F32), pltpu.VMEM((EXT, D), F32), pltpu.VMEM((8, ROT_ROWS, 128), F32)],
        compiler_params=_cparams(("arbitrary",)),
    )(z, z, z, z, z, z, z, z, z, z, z, head, x, b_gate, w_dw, b_dw, ln_g, ln_b, pool_scale, w_co, w_pool, w_po, w_o)


def _ffn_fwd_bwd(h1, target, g_ffn, g_final, w_g, w_u, w_d, nb):
    tp = nb * BR

    def body(h1_ref, tgt_ref, gf_ref, gfin_ref, wg_hbm, wu_hbm, wd_hbm,
             dh1_ref, dh1b_ref, vb_ref, fb_ref, dgb_ref, dub_ref, dh2b_ref, loss_ref, dgf_ref, dgfin_ref,
             wg_ref, wu_ref, wd_ref, sem):
        i = pl.program_id(0)

        @pl.when(i == 0)
        def _():
            copies = [pltpu.make_async_copy(wg_hbm, wg_ref, sem.at[0]),
                      pltpu.make_async_copy(wu_hbm, wu_ref, sem.at[1]),
                      pltpu.make_async_copy(wd_hbm, wd_ref, sem.at[2])]
            for cp in copies:
                cp.start()
            loss_ref[...] = jnp.zeros_like(loss_ref)
            dgf_ref[...] = jnp.zeros_like(dgf_ref)
            dgfin_ref[...] = jnp.zeros_like(dgfin_ref)
            for cp in copies:
                cp.wait()

        h1 = h1_ref[...]
        r1 = lax.rsqrt(jnp.mean(h1 * h1, axis=-1, keepdims=True) + RMS_EPS)
        vn = h1 * r1
        vb = (vn * gf_ref[...]).astype(BF16)
        vb_ref[...] = vb
        g = _dot(vb, wg_ref[...])
        up = _dot(vb, wu_ref[...])
        sg = _sigmoid(g)
        sl = g * sg
        fb = (sl * up).astype(BF16)
        fb_ref[...] = fb
        h2 = h1 + _dot(fb, wd_ref[...])
        r2 = lax.rsqrt(jnp.mean(h2 * h2, axis=-1, keepdims=True) + RMS_EPS)
        yn = h2 * r2
        valid = i > 0
        diff = jnp.where(valid, yn * gfin_ref[...] - tgt_ref[...], 0.0)
        loss_ref[...] += 0.5 * jnp.sum(jnp.mean(diff * diff, axis=-1, keepdims=True))
        dy = diff * (1.0 / D)
        dgfin_ref[...] += jnp.sum(dy * yn, axis=0, keepdims=True)
        gd = dy * gfin_ref[...]
        dh2 = r2 * (gd - yn * jnp.mean(yn * gd, axis=-1, keepdims=True))
        dh2b = dh2.astype(BF16)
        dh2b_ref[...] = dh2b
        df = _dot_nt(dh2b, wd_ref[...])
        dub = (df * sl).astype(BF16)
        dgb = (df * up * (sg * (1.0 + g * (1.0 - sg)))).astype(BF16)
        dub_ref[...] = dub
        dgb_ref[...] = dgb
        dv = _dot_nt(dgb, wg_ref[...]) + _dot_nt(dub, wu_ref[...])
        dgf_ref[...] += jnp.sum(dv * vn, axis=0, keepdims=True)
        gd1 = dv * gf_ref[...]
        dh1 = dh2 + r1 * (gd1 - vn * jnp.mean(vn * gd1, axis=-1, keepdims=True))
        dh1_ref[...] = dh1
        dh1b_ref[...] = dh1.astype(BF16)

    any_spec = pl.BlockSpec(memory_space=pl.ANY)
    return pl.pallas_call(
        body, name="ffn_fwd_bwd",
        grid=(nb,),
        in_specs=[_row_spec(), _x_spec(), _const_spec((1, D)), _const_spec((1, D)), any_spec, any_spec, any_spec],
        out_specs=[_row_spec(), _row_spec(), _row_spec(), _row_spec(D_FF), _row_spec(D_FF), _row_spec(D_FF), _row_spec(),
                   _const_spec((1, 1)), _const_spec((1, D)), _const_spec((1, D))],
        out_shape=[jax.ShapeDtypeStruct((tp, D), F32), jax.ShapeDtypeStruct((tp, D), BF16),
                   jax.ShapeDtypeStruct((tp, D), BF16), jax.ShapeDtypeStruct((tp, D_FF), BF16),
                   jax.ShapeDtypeStruct((tp, D_FF), BF16), jax.ShapeDtypeStruct((tp, D_FF), BF16),
                   jax.ShapeDtypeStruct((tp, D), BF16), jax.ShapeDtypeStruct((1, 1), F32),
                   jax.ShapeDtypeStruct((1, D), F32), jax.ShapeDtypeStruct((1, D), F32)],
        scratch_shapes=[pltpu.VMEM((D, D_FF), BF16), pltpu.VMEM((D, D_FF), BF16), pltpu.VMEM((D_FF, D), BF16),
                        pltpu.SemaphoreType.DMA((3,))],
        compiler_params=_cparams(("arbitrary",)),
    )(h1, target, g_ffn, g_final, w_g, w_u, w_d)


def _mixers_bwd_rows(dh1b, yc, yp, z, b_gate, cpre, ln_g, ln_b, mw, pool_scale, w_o, w_co, w_po, w_pool, dep, nb):
    tp = nb * BR

    def body(dh1b_ref, yc_ref, yp_ref, za, zb, bg_ref, cpre_ref, lng_ref, lnb_ref, mw_ref, ps_ref,
             wo_ref, wco_ref, wpo_ref, wpool_ref, dep_ref,
             dycb_ref, dypb_ref, dzg_ref, dconv_ref, dmwb_ref, dm_ref, dbg_ref, dlng_ref, dlnb_ref, dbdw_ref, dps_ref):
        i = pl.program_id(0)

        @pl.when(i == 0)
        def _():
            for r in (dbg_ref, dlng_ref, dlnb_ref, dbdw_ref, dps_ref):
                r[...] = jnp.zeros_like(r)

        dmg = _dot_nt(dh1b_ref[...], wo_ref[...])
        s_a = _sigmoid(za[...] + bg_ref[:, 0:D])
        s_b = _sigmoid(zb[...] + bg_ref[:, D:2 * D])
        dycb = (dmg * s_a).astype(BF16)
        dypb = (dmg * s_b).astype(BF16)
        dycb_ref[...] = dycb
        dypb_ref[...] = dypb
        dza = dmg * yc_ref[...] * (s_a * (1.0 - s_a))
        dzb = dmg * yp_ref[...] * (s_b * (1.0 - s_b))
        dzg_ref[:, 0:D] = dza.astype(BF16)
        dzg_ref[:, D:2 * D] = dzb.astype(BF16)
        dbg_ref[:, 0:D] += jnp.sum(dza, axis=0, keepdims=True)
        dbg_ref[:, D:2 * D] += jnp.sum(dzb, axis=0, keepdims=True)

        dca = _dot_nt(dycb, wco_ref[...])
        conv = cpre_ref[...]
        mu = jnp.mean(conv, axis=-1, keepdims=True)
        xc = conv - mu
        rstd = lax.rsqrt(jnp.mean(xc * xc, axis=-1, keepdims=True) + LN_EPS)
        xhat = xc * rstd
        ln = xhat * lng_ref[...] + lnb_ref[...]
        sg = _sigmoid(ln)
        dln = dca * (sg * (1.0 + ln * (1.0 - sg)))
        dlng_ref[...] += jnp.sum(dln * xhat, axis=0, keepdims=True)
        dlnb_ref[...] += jnp.sum(dln, axis=0, keepdims=True)
        dxh = dln * lng_ref[...]
        dconv = rstd * (dxh - jnp.mean(dxh, axis=-1, keepdims=True)
                        - xhat * jnp.mean(dxh * xhat, axis=-1, keepdims=True))
        dconv_ref[...] = dconv
        dbdw_ref[...] += jnp.sum(dconv, axis=0, keepdims=True)

        dm2 = _dot_nt(dypb, wpo_ref[...])
        dps_ref[...] += jnp.sum(dm2 * mw_ref[...], axis=0, keepdims=True)
        dmwb = (dm2 * ps_ref[...]).astype(BF16)
        dmwb_ref[...] = dmwb
        for gi in range(len(POOL_WINDOWS)):
            lanes = slice(gi * GD, (gi + 1) * GD)
            dm_ref[:, lanes] = _dot_nt(dmwb[:, lanes], wpool_ref[gi])

    in_specs = [_row_spec(), _row_spec(), _row_spec(),
                pl.BlockSpec((BR, D), lambda i: (i, 3)), pl.BlockSpec((BR, D), lambda i: (i, 4)),
                _const_spec((1, 2 * D)), _row_spec(), _const_spec((1, D)), _const_spec((1, D)), _row_spec(),
                _const_spec((1, D)), _const_spec((D, D)), _const_spec((D, D)), _const_spec((D, D)),
                _const_spec((4, GD, GD)), _const_spec((8, 128))]
    return pl.pallas_call(
        body, name="mixers_bwd_rows",
        grid=(nb,),
        in_specs=in_specs,
        out_specs=[_row_spec(), _row_spec(), _row_spec(2 * D), _row_spec(), _row_spec(), _row_spec(),
                   _const_spec((1, 2 * D)), _const_spec((1, D)), _const_spec((1, D)), _const_spec((1, D)),
                   _const_spec((1, D))],
        out_shape=[jax.ShapeDtypeStruct((tp, D), BF16), jax.ShapeDtypeStruct((tp, D), BF16),
                   jax.ShapeDtypeStruct((tp, 2 * D), BF16), jax.ShapeDtypeStruct((tp, D), F32),
                   jax.ShapeDtypeStruct((tp, D), BF16), jax.ShapeDtypeStruct((tp, D), F32),
                   jax.ShapeDtypeStruct((1, 2 * D), F32), jax.ShapeDtypeStruct((1, D), F32),
                   jax.ShapeDtypeStruct((1, D), F32), jax.ShapeDtypeStruct((1, D), F32),
                   jax.ShapeDtypeStruct((1, D), F32)],
        compiler_params=_cparams(("arbitrary",)),
    )(dh1b, yc, yp, z, z, b_gate, cpre, ln_g, ln_b, mw, pool_scale, w_o, w_co, w_po, w_pool, dep)


def _mixers_bwd_halo(dconv, dm, z, dzg, w_dw, head, x, g_mix, dh1, w_in_b, dep, nb, t_total):
    tp = nb * BR
    ns = w_in_b.shape[0]
    wcols = w_in_b.shape[2]
    seq = x.shape[0]

    def body(dcp, dcc, dcn, dmp, dmc, dmn, avp, av, avn, agp, ag, agn, dzg_ref, wdw_ref, head_ref, x_ref, g_ref,
             dh1_ref, w_hbm, dep_ref,
             dzb_ref, gx_ref, dhead_ref, dwdw_ref, dgmix_ref,
             w_ref, sem, aext_ref, dext_ref, qext_ref, da_ref, rot_ref, dwp_ref):
        i = pl.program_id(0)

        @pl.when(i == 0)
        def _():
            cp = pltpu.make_async_copy(w_hbm, w_ref, sem.at[0])
            cp.start()
            dwp_ref[...] = jnp.zeros_like(dwp_ref)
            dgmix_ref[...] = jnp.zeros_like(dgmix_ref)
            cp.wait()

        sig_g = _sigmoid(ag[...])
        _fill_ext(aext_ref, avp[...] * _sigmoid(agp[...]), av[...] * sig_g, avn[...] * _sigmoid(agn[...]), i, nb)
        _fill_ext(dext_ref, dcp[...], dcc[...], dcn[...], i, nb)
        _fill_ext(qext_ref, dmp[...], dmc[...], dmn[...], i, nb)

        def conv_chunk(c, carry):
            lanes = pl.ds(pl.multiple_of(c * 128, 128), 128)
            _fill_rot(rot_ref, dext_ref, lanes)
            acc = jnp.zeros((BR, 128), F32)
            for k in range(KW):
                acc = acc + wdw_ref[k:k + 1, lanes] * _tap(rot_ref, dext_ref, lanes, KW - k)
            da_ref[:, lanes] = acc
            _fill_rot(rot_ref, aext_ref, lanes)
            dcv = dext_ref[HALO:HALO + BR, lanes]
            for k in range(KW):
                prod = _tap(rot_ref, aext_ref, lanes, 1 + k) * dcv
                dwp_ref[k, :, lanes] += jnp.sum(prod.reshape(BR // 8, 8, 128), axis=0)
            return carry
        lax.fori_loop(0, D // 128, conv_chunk, 0)

        @pl.when(i == nb - 1)
        def _():
            dwdw_ref[...] = jnp.sum(dwp_ref[...], axis=1)

        da = da_ref[...]
        a_val = av[...]
        dzb_ref[:, 0:D] = (da * sig_g).astype(BF16)
        dzb_ref[:, D:2 * D] = (da * a_val * (sig_g * (1.0 - sig_g))).astype(BF16)

        t_ext = _row_ids(i, EXT, -HALO)
        for gi, w in enumerate(POOL_WINDOWS):
            left = w // 2
            right = w - 1 - left
            lanes = slice(gi * GD, (gi + 1) * GD)
            qext_ref[:, lanes] = qext_ref[:, lanes] / _pool_cnt(t_ext, w, t_total)
            s = qext_ref[pl.ds(HALO - right, BR), lanes]
            for j in range(-right + 1, left + 1):
                s = s + qext_ref[pl.ds(HALO + j, BR), lanes]
            dzb_ref[:, 2 * D + gi * GD:2 * D + (gi + 1) * GD] = (s - dmc[:, lanes]).astype(BF16)
        dzb_ref[:, 3 * D:5 * D] = dzg_ref[...]

        du = _dot_nt(dzb_ref[:, 0:wcols], w_ref[0])
        for s_i in range(1, ns):
            du = du + _dot_nt(dzb_ref[:, s_i * wcols:(s_i + 1) * wcols], w_ref[s_i])
        h0 = jnp.where(i == 0, head_ref[...], x_ref[...])
        r0 = lax.rsqrt(jnp.mean(h0 * h0, axis=-1, keepdims=True) + RMS_EPS)
        un = h0 * r0
        dgmix_ref[...] += jnp.sum(du * un, axis=0, keepdims=True)
        gd = du * g_ref[...]
        dh0 = dh1_ref[...] + r0 * (gd - un * jnp.mean(un * gd, axis=-1, keepdims=True))
        gx_ref[...] = dh0

        @pl.when(i == 0)
        def _():
            dhead_ref[...] = dh0

    any_spec = pl.BlockSpec(memory_space=pl.ANY)
    in_specs = (_halo_specs(0, nb) + _halo_specs(0, nb) + _halo_specs(0, nb) + _halo_specs(1, nb)
                + [_row_spec(2 * D), _const_spec((32, D)), _const_spec((BR, D)), _x_spec(), _const_spec((1, D)),
                   _row_spec(), any_spec, _const_spec((8, 128))])
    return pl.pallas_call(
        body, name="mixers_bwd_halo",
        grid=(nb,),
        in_specs=in_specs,
        out_specs=[_row_spec(D_IN), _x_spec(), _const_spec((BR, D)), _const_spec((32, D)), _const_spec((1, D))],
        out_shape=[jax.ShapeDtypeStruct((tp, D_IN), BF16), jax.ShapeDtypeStruct((seq, D), F32),
                   jax.ShapeDtypeStruct((BR, D), F32), jax.ShapeDtypeStruct((32, D), F32),
                   jax.ShapeDtypeStruct((1, D), F32)],
        scratch_shapes=[pltpu.VMEM((ns, D, wcols), BF16), pltpu.SemaphoreType.DMA((1,)),
                        pltpu.VMEM((EXT, D), F32), pltpu.VMEM((EXT, D), F32), pltpu.VMEM((EXT, D), F32),
                        pltpu.VMEM((BR, D), F32), pltpu.VMEM((8, ROT_ROWS, 128), F32), pltpu.VMEM((32, 8, D), F32)],
        compiler_params=_cparams(("arbitrary",)),
    )(dconv, dconv, dconv, dm, dm, dm, z, z, z, z, z, z, dzg, w_dw, head, x, g_mix, dh1, w_in_b, dep)


def _wgrad(a, c, tm, tn, tk, name, diag=False, col_major=False):
    tp, m = a.shape
    n = c.shape[1]
    nk = tp // tk
    gm, gn = m // tm, n // tn

    def body(a_ref, c_ref, o_ref, ob_ref):
        k = pl.program_id(2)

        @pl.when(k == 0)
        def _():
            o_ref[...] = jnp.zeros_like(o_ref)

        o_ref[...] += _dot_tn(a_ref[...], c_ref[...])

        @pl.when(k == nk - 1)
        def _():
            ob_ref[...] = o_ref[...].astype(BF16)

    c_map = lambda i, j, k: (k, j)
    grid = (gm, gn, nk)
    if diag:
        grid = (gm, 1, nk)
        c_map = lambda i, j, k: (k, i)
        o_spec = pl.BlockSpec((tm, tn), lambda i, j, k: (i, 0))
        o_shape = (m, tn)
    elif col_major:
        o_spec = pl.BlockSpec((None, tm, tn), lambda i, j, k: (j, i, 0))
        o_shape = (gn, m, tn)
    else:
        o_spec = pl.BlockSpec((tm, tn), lambda i, j, k: (i, j))
        o_shape = (m, n)
    return pl.pallas_call(
        body, name=name,
        grid=grid,
        in_specs=[pl.BlockSpec((tk, tm), lambda i, j, k: (k, i)), pl.BlockSpec((tk, tn), c_map)],
        out_specs=[o_spec, o_spec],
        out_shape=[jax.ShapeDtypeStruct(o_shape, F32), jax.ShapeDtypeStruct(o_shape, BF16)],
        compiler_params=_cparams(("arbitrary", "arbitrary", "arbitrary")),
    )(a, c)


def _place():
    x, y, c = lax.axis_index("x"), lax.axis_index("y"), lax.axis_index("c")
    others = [(1 - x, y), (x, 1 - y), (1 - x, 1 - y)]
    return x, y, c, others


def _split2(a, axis=0):
    return a.reshape(a.shape[:axis] + (2, a.shape[axis] // 2) + a.shape[axis + 1:])


def _merge2(a, axis=0):
    return a.reshape(a.shape[:axis] + (2 * a.shape[axis + 1],) + a.shape[axis + 2:])


def _cast_into_slot(w2d, chip, name):
    r, c = w2d.shape
    r2 = r // 2

    def body(chip_ref, w_ref, o_ref):
        o_ref[...] = w_ref[...].astype(BF16)

    return pl.pallas_call(
        body, name=name,
        grid_spec=pltpu.PrefetchScalarGridSpec(
            num_scalar_prefetch=1, grid=(2,),
            in_specs=[pl.BlockSpec((r2, c), lambda h, chip_ref: (h, 0))],
            out_specs=pl.BlockSpec((None, None, r2, c), lambda h, chip_ref: (chip_ref[0], h, 0, 0))),
        out_shape=jax.ShapeDtypeStruct((N_SHARD, 2, r2, c), BF16),
        compiler_params=_cparams(("arbitrary",)),
    )(chip, w2d)


HBM_SPEC = pl.BlockSpec(memory_space=pltpu.HBM)
SEM_SPEC = pl.BlockSpec(memory_space=pltpu.SEMAPHORE)
DATAFLOW = pltpu.SideEffectType.DATAFLOW_SIDE_EFFECTING
TOKEN = jax.ShapeDtypeStruct((8, 128), F32)


def _in_hbm(a):
    return pltpu.with_memory_space_constraint(a, pltpu.HBM)


def _gather_tiny(v):
    vm = pl.BlockSpec(memory_space=pltpu.VMEM)

    def body(v_ref, out_ref, send_sems, recv_sems):
        x, y, c, others = _place()
        mine = 2 * x + y
        sends = [pltpu.make_async_remote_copy(
            src_ref=v_ref, dst_ref=out_ref.at[mine], send_sem=send_sems.at[j], recv_sem=recv_sems.at[j],
            device_id=(*chip, c), device_id_type=MESH) for j, chip in enumerate(others)]
        for cp in sends:
            cp.start()
        out_ref[mine] = v_ref[...]
        for j, chip in enumerate(others):
            landed = out_ref.at[2 * chip[0] + chip[1]]
            pltpu.make_async_remote_copy(
                src_ref=landed, dst_ref=landed, send_sem=send_sems.at[j], recv_sem=recv_sems.at[j],
                device_id=(x, y, c), device_id_type=MESH).wait_recv()
        for cp in sends:
            cp.wait_send()

    return pl.pallas_call(
        body, name="gather_tiny",
        in_specs=[vm], out_specs=vm,
        out_shape=jax.ShapeDtypeStruct((N_SHARD,) + v.shape, v.dtype),
        scratch_shapes=[pltpu.SemaphoreType.DMA((3,)), pltpu.SemaphoreType.DMA((3,))],
    )(v)


def _ici_copies(srcs, dsts, send_sems, recv_sems, started):
    x, y, c, others = _place()
    mine = 2 * x + y
    copies = []
    for a in range(len(srcs)):
        for j, chip in enumerate(others):
            there = 2 * chip[0] + chip[1]
            src, dst = srcs[a](mine, there, c), dsts[a](mine, there, c)
            if not started:
                dst = dsts[a](there, mine, c)
            copies.append(pltpu.make_async_remote_copy(
                src_ref=src, dst_ref=dst, send_sem=send_sems.at[a * 3 + j], recv_sem=recv_sems.at[a * 3 + j],
                device_id=(*chip, c), device_id_type=MESH))
    return copies


def _split_start(srcs_of, dsts_of, arrays, n_src, name):
    n = len(arrays)

    def body(*refs):
        ins = refs[:n]
        send_sems, recv_sems = refs[n], refs[n + 1]
        token = refs[2 * n + 2]
        for cp in _ici_copies(srcs_of(ins), dsts_of(ins), send_sems, recv_sems, True):
            cp.start()
        token[...] = jnp.zeros_like(token)

    out = pl.pallas_call(
        body, name=name,
        in_specs=[HBM_SPEC] * n,
        out_specs=(SEM_SPEC, SEM_SPEC, *([HBM_SPEC] * n), pl.BlockSpec(memory_space=pltpu.VMEM)),
        out_shape=(pltpu.SemaphoreType.DMA((3 * n_src,)), pltpu.SemaphoreType.DMA((3 * n_src,)),
                   *[pltpu.HBM(a.shape, a.dtype) for a in arrays], TOKEN),
        input_output_aliases={a: 2 + a for a in range(n)},
        compiler_params=pltpu.CompilerParams(has_side_effects=DATAFLOW),
    )(*[_in_hbm(a) for a in arrays])
    return out[0], out[1], list(out[2:2 + n]), out[2 + n]


def _split_wait(srcs_of, dsts_of, send_sems, recv_sems, arrays, after, name):
    n = len(arrays)

    def body(*refs):
        ins = refs[:n]
        send_sems, recv_sems = refs[n], refs[n + 1]
        for cp in _ici_copies(srcs_of(ins), dsts_of(ins), send_sems, recv_sems, False):
            cp.wait_send()
            cp.wait_recv()

    return pl.pallas_call(
        body, name=name,
        in_specs=[HBM_SPEC] * n + [SEM_SPEC, SEM_SPEC, pl.BlockSpec(memory_space=pl.ANY)],
        out_specs=[HBM_SPEC] * n,
        out_shape=[pltpu.HBM(a.shape, a.dtype) for a in arrays],
        input_output_aliases={a: a for a in range(n)},
        compiler_params=pltpu.CompilerParams(has_side_effects=DATAFLOW),
    )(*arrays, send_sems, recv_sems, after)


def _gather_views(ins):
    view = [lambda frm, to, c, r=r: r.at[frm, c] for r in ins]
    return view


def _gather_start(bufs, name):
    return _split_start(_gather_views, _gather_views, bufs, len(bufs), name)


def _gather_wait(send_sems, recv_sems, bufs, after, name):
    return _split_wait(_gather_views, _gather_views, send_sems, recv_sems, bufs, after, name)


def _forward_halves(bufs, name):
    n = len(bufs)
    any_spec = pl.BlockSpec(memory_space=pl.ANY)

    def body(*refs):
        outs = refs[n:2 * n]
        send_sems, recv_sems = refs[2 * n:]
        x, y, c, others = _place()
        copies = []
        for a in range(n):
            for j, chip in enumerate(others):
                landed = outs[a].at[2 * chip[0] + chip[1], c]
                copies.append(pltpu.make_async_remote_copy(
                    src_ref=landed, dst_ref=landed, send_sem=send_sems.at[a * 3 + j], recv_sem=recv_sems.at[a * 3 + j],
                    device_id=(x, y, 1 - c), device_id_type=MESH))
        for cp in copies:
            cp.start()
        for a in range(n):
            for j, chip in enumerate(others):
                landed = outs[a].at[2 * chip[0] + chip[1], 1 - c]
                pltpu.make_async_remote_copy(
                    src_ref=landed, dst_ref=landed, send_sem=send_sems.at[a * 3 + j], recv_sem=recv_sems.at[a * 3 + j],
                    device_id=(x, y, c), device_id_type=MESH).wait_recv()
        for cp in copies:
            cp.wait_send()

    out = pl.pallas_call(
        body, name=name,
        in_specs=[any_spec] * n, out_specs=[any_spec] * n,
        out_shape=[jax.ShapeDtypeStruct(b.shape, b.dtype) for b in bufs],
        input_output_aliases={a: a for a in range(n)},
        scratch_shapes=[pltpu.SemaphoreType.DMA((3 * n,)), pltpu.SemaphoreType.DMA((3 * n,))],
    )(*bufs)
    return [_merge2(o, 1) for o in out]


def _swap_halves_bf16(gbs, name):
    n = len(gbs)
    any_spec = pl.BlockSpec(memory_space=pl.ANY)

    def body(*refs):
        ins, outs = refs[:n], refs[n:2 * n]
        send_sems, recv_sems = refs[2 * n:]
        x, y, c, _ = _place()
        copies = []
        for a in range(n):
            copies.append(pltpu.make_async_remote_copy(
                src_ref=ins[a].at[:, 1 - c], dst_ref=outs[a], send_sem=send_sems.at[a], recv_sem=recv_sems.at[a],
                device_id=(x, y, 1 - c), device_id_type=MESH))
        for cp in copies:
            cp.start()
        for cp in copies:
            cp.wait()

    return pl.pallas_call(
        body, name=name,
        in_specs=[any_spec] * n, out_specs=[any_spec] * n,
        out_shape=[jax.ShapeDtypeStruct((g.shape[0], g.shape[1] // 2, g.shape[2]), g.dtype) for g in gbs],
        scratch_shapes=[pltpu.SemaphoreType.DMA((n,)), pltpu.SemaphoreType.DMA((n,))],
    )(*[_split2(g, 1) for g in gbs])


def _scatter_srcs(n):
    return lambda ins: [lambda frm, to, c, r=r: r.at[to] for r in ins[:n]]


def _scatter_dsts(n):
    return lambda ins: [lambda frm, to, c, r=r: r.at[frm] for r in ins[n:]]


def _scatter_start(hbs, name):
    n = len(hbs)
    lands = [lax.empty(h.shape, h.dtype) for h in hbs]
    return _split_start(_scatter_srcs(n), _scatter_dsts(n), list(hbs) + lands, n, name)


def _scatter_wait(send_sems, recv_sems, arrays, after, name):
    n = len(arrays) // 2
    return _split_wait(_scatter_srcs(n), _scatter_dsts(n), send_sems, recv_sems, arrays, after, name)[n:]


def _join_halves(rhs, name):
    n = len(rhs)
    any_spec = pl.BlockSpec(memory_space=pl.ANY)

    def body(*refs):
        outs = refs[n:2 * n]
        send_sems, recv_sems = refs[2 * n:]
        x, y, c, _ = _place()
        copies = []
        for a in range(n):
            copies.append(pltpu.make_async_remote_copy(
                src_ref=outs[a].at[c], dst_ref=outs[a].at[c], send_sem=send_sems.at[a],
                recv_sem=recv_sems.at[a], device_id=(x, y, 1 - c), device_id_type=MESH))
        for cp in copies:
            cp.start()
        for a in range(n):
            landed = outs[a].at[1 - c]
            pltpu.make_async_remote_copy(
                src_ref=landed, dst_ref=landed, send_sem=send_sems.at[a], recv_sem=recv_sems.at[a],
                device_id=(x, y, c), device_id_type=MESH).wait_recv()
        for cp in copies:
            cp.wait_send()

    out = pl.pallas_call(
        body, name=name,
        in_specs=[any_spec] * n, out_specs=[any_spec] * n,
        out_shape=[jax.ShapeDtypeStruct(r.shape, r.dtype) for r in rhs],
        input_output_aliases={a: a for a in range(n)},
        scratch_shapes=[pltpu.SemaphoreType.DMA((n,)), pltpu.SemaphoreType.DMA((n,))],
    )(*rhs)
    return [_merge2(o) for o in out]


def _allreduce_small(v):
    rows, cols = v.shape
    vm = pl.BlockSpec(memory_space=pltpu.VMEM)
    flips = [(dx, dy, dc) for dx in (0, 1) for dy in (0, 1) for dc in (0, 1)][1:]

    def body(v_ref, out_ref, buf_ref, send_sems, recv_sems):
        x, y, c, _ = _place()
        mine = 4 * x + 2 * y + c
        copies = []
        for k, (dx, dy, dc) in enumerate(flips):
            px, py, pc = jnp.bitwise_xor(x, dx), jnp.bitwise_xor(y, dy), jnp.bitwise_xor(c, dc)
            copies.append(pltpu.make_async_remote_copy(
                src_ref=v_ref, dst_ref=buf_ref.at[mine], send_sem=send_sems.at[k], recv_sem=recv_sems.at[k],
                device_id=(px, py, pc), device_id_type=MESH))
        for cp in copies:
            cp.start()
        buf_ref[mine] = v_ref[...]
        for k, (dx, dy, dc) in enumerate(flips):
            src = 4 * jnp.bitwise_xor(x, dx) + 2 * jnp.bitwise_xor(y, dy) + jnp.bitwise_xor(c, dc)
            pltpu.make_async_remote_copy(
                src_ref=v_ref, dst_ref=buf_ref.at[src], send_sem=send_sems.at[k], recv_sem=recv_sems.at[k],
                device_id=(x, y, c), device_id_type=MESH).wait_recv()
        for cp in copies:
            cp.wait_send()
        acc = buf_ref[0]
        for d in range(1, 8):
            acc = acc + buf_ref[d]
        out_ref[...] = acc

    return pl.pallas_call(
        body, name="allreduce_small",
        in_specs=[vm], out_specs=vm,
        out_shape=jax.ShapeDtypeStruct((rows, cols), F32),
        scratch_shapes=[pltpu.VMEM((8, rows, cols), F32), pltpu.SemaphoreType.DMA((7,)), pltpu.SemaphoreType.DMA((7,))],
    )(v)


def _row_block(r):
    for cand in (512, 352, 256, 128, 64, 48, 16):
        if r % cand == 0:
            return cand
    return r


def _add_sibling_half(g, sb, core, name):
    ns, r, c = g.shape
    r2 = r // 2

    def body(core_ref, g_ref, sb_ref, h_ref, hb_ref):
        h = g_ref[...] + sb_ref[...].astype(F32)
        h_ref[...] = h
        hb_ref[...] = h.astype(BF16)

    spec = pl.BlockSpec((None, r2, c), lambda s, core_ref: (s, 0, 0))
    return pl.pallas_call(
        body, name=name,
        grid_spec=pltpu.PrefetchScalarGridSpec(
            num_scalar_prefetch=1, grid=(ns,),
            in_specs=[pl.BlockSpec((None, r2, c), lambda s, core_ref: (s, core_ref[0], 0)), spec],
            out_specs=[spec, spec]),
        out_shape=[jax.ShapeDtypeStruct((ns, r2, c), F32), jax.ShapeDtypeStruct((ns, r2, c), BF16)],
        compiler_params=_cparams(("arbitrary",)),
    )(core, g, sb)


def _add_chip_slabs(h, rb, idx, name):
    ns, r2, c = h.shape

    def body(idx_ref, h_ref, r0_ref, r1_ref, r2_ref, o_ref):
        o_ref[...] = ((h_ref[...] + r0_ref[...].astype(F32)) + r1_ref[...].astype(F32)) + r2_ref[...].astype(F32)

    def pick(k):
        return pl.BlockSpec((None, r2, c), lambda i, idx_ref: (idx_ref[k], 0, 0))

    return pl.pallas_call(
        body, name=name,
        grid_spec=pltpu.PrefetchScalarGridSpec(
            num_scalar_prefetch=1, grid=(1,),
            in_specs=[pick(0), pick(1), pick(2), pick(3)],
            out_specs=pl.BlockSpec((None, r2, c), lambda i, idx_ref: (idx_ref[4], 0, 0))),
        out_shape=jax.ShapeDtypeStruct((2, r2, c), F32),
        compiler_params=_cparams(("arbitrary",)),
    )(idx, h, rb, rb, rb)


def _adamw(g, w, m, v, name):
    r, c = g.shape
    br = _row_block(r)

    def body(g_ref, w_ref, m_ref, v_ref, d_ref, nm_ref, nv_ref):
        gg = g_ref[...]
        nm = B1 * m_ref[...] + (1.0 - B1) * gg
        nv = B2 * v_ref[...] + (1.0 - B2) * jnp.square(gg)
        m_hat = nm / (1.0 - B1 ** STEP)
        v_hat = nv / (1.0 - B2 ** STEP)
        d_ref[...] = -LR * (m_hat / (jnp.sqrt(v_hat) + ADAM_EPS) + WD * w_ref[...])
        nm_ref[...] = nm
        nv_ref[...] = nv

    spec = pl.BlockSpec((br, c), lambda i: (i, 0))
    return pl.pallas_call(
        body, name=name,
        grid=(r // br,),
        in_specs=[spec] * 4, out_specs=[spec] * 3,
        out_shape=[jax.ShapeDtypeStruct((r, c), F32)] * 3,
        compiler_params=_cparams(("arbitrary",)),
    )(g, w, m, v)


BIG = ("w_in", "w_conv_out", "w_pool", "w_pool_out", "w_o", "w_ffn_gate", "w_ffn_up", "w_ffn_down")
REPL = ("g_mix", "b_gate", "b_dw", "ln_g", "ln_b", "pool_scale", "g_ffn", "g_final")
GROUP_MIX = ("w_conv_out", "w_pool", "w_pool_out", "w_o")
GROUP_FFN = ("w_ffn_gate", "w_ffn_up", "w_ffn_down")
WEIGHT_ORDER = ("meta_tokens", "g_mix", "w_in", "b_gate", "w_dw", "b_dw", "ln_g", "ln_b", "w_conv_out", "w_pool",
                "pool_scale", "w_pool_out", "w_o", "g_ffn", "w_ffn_gate", "w_ffn_up", "w_ffn_down", "g_final")


def _shard2d(name, a):
    a = a[0]
    if name == "w_pool":
        return a.reshape(4 * 64, GD)
    return a


def _cols_to_slabs(a):
    m, n = a.shape
    return a.reshape(m, N_SHARD, n // N_SHARD).transpose(1, 0, 2)


def _slabs_to_cols(a):
    ns, m, c = a.shape
    return a.transpose(1, 0, 2).reshape(m, ns * c)


def kernel(x, meta_tokens, g_mix, w_in, b_gate, w_dw, b_dw, ln_g, ln_b, w_conv_out, w_pool, pool_scale, w_pool_out, w_o, g_ffn, w_ffn_gate, w_ffn_up, w_ffn_down, g_final, loss_target, m_meta_tokens, m_g_mix, m_w_in, m_b_gate, m_w_dw, m_b_dw, m_ln_g, m_ln_b, m_w_conv_out, m_w_pool, m_pool_scale, m_w_pool_out, m_w_o, m_g_ffn, m_w_ffn_gate, m_w_ffn_up, m_w_ffn_down, m_g_final, v_meta_tokens, v_g_mix, v_w_in, v_b_gate, v_w_dw, v_b_dw, v_ln_g, v_ln_b, v_w_conv_out, v_w_pool, v_pool_scale, v_w_pool_out, v_w_o, v_g_ffn, v_w_ffn_gate, v_w_ffn_up, v_w_ffn_down, v_g_final):
    args = dict(locals())
    w = {n: args[n] for n in WEIGHT_ORDER}
    mom = {n: args["m_" + n] for n in WEIGHT_ORDER}
    var = {n: args["v_" + n] for n in WEIGHT_ORDER}
    seq = x.shape[1]
    nb = seq // BR + 1
    tp = nb * BR
    tk = tp // 2 if (tp // 2) % 16 == 0 else BR
    t_total = seq + N_META
    cx, cy, cc = lax.axis_index("x"), lax.axis_index("y"), lax.axis_index("c")
    chip = 2 * cx + cy
    chip1 = jnp.reshape(chip, (1,)).astype(jnp.int32)
    core = jnp.reshape(cc, (1,)).astype(jnp.int32)
    others = jnp.sort(jnp.stack([2 * (1 - cx) + cy, 2 * cx + (1 - cy), 2 * (1 - cx) + (1 - cy)]))
    idx = jnp.concatenate([chip1, others.astype(jnp.int32), core])
    xs, target = x[0], loss_target[0]

    tiny = _gather_tiny(jnp.concatenate([w["meta_tokens"], w["w_dw"][0], jnp.zeros((1, GD), F32)], axis=0))
    small = {n: w[n] for n in REPL if n != "g_final"}
    small["g_final"] = w["g_final"].reshape(1, D)
    small["w_dw"] = _slabs_to_cols(tiny[:, N_META:])
    head = jnp.concatenate([jnp.zeros((PAD, D), F32), _slabs_to_cols(tiny[:, :N_META])], axis=0)
    bufs = {n: _cast_into_slot(_shard2d(n, w[n]), chip1, "cast_" + n) for n in BIG}

    def gather_start(group, name):
        return _gather_start([bufs[n] for n in group], "gather_start_" + name)

    def gather_finish(group, start, after, name):
        landed = _gather_wait(start[0], start[1], start[2], after, "gather_wait_" + name)
        return dict(zip(group, _forward_halves(landed, "forward_" + name)))

    st_in = gather_start(("w_in",), "in")
    u = _rms_u(head, xs, small["g_mix"], nb)
    gw = gather_finish(("w_in",), st_in, u, "in")
    st_mix = gather_start(GROUP_MIX, "mix")
    z = _in_proj(u, gw["w_in"], st_mix[3], nb)
    gw.update(gather_finish(GROUP_MIX, st_mix, z, "mix"))
    st_ffn = gather_start(GROUP_FFN, "ffn")
    w_pool_b = gw["w_pool"].reshape(N_SHARD, 4, 64, GD).transpose(1, 0, 2, 3).reshape(4, GD, GD)
    w_co_b, w_po_b, w_o_b = (gw[n].reshape(D, D) for n in ("w_conv_out", "w_pool_out", "w_o"))
    h1, yc, yp, mg, ca, cpre, m, mw, m2b = _mixers_fwd(
        z, head, xs, small["b_gate"] + st_ffn[3][0, 0], small["w_dw"], small["b_dw"], small["ln_g"], small["ln_b"],
        small["pool_scale"], w_co_b, w_pool_b, w_po_b, w_o_b, nb, t_total)
    gw.update(gather_finish(GROUP_FFN, st_ffn, h1, "ffn"))

    dh1, dh1b, vb, fb, dgb, dub, dh2b, loss, dg_ffn, dg_final = _ffn_fwd_bwd(
        h1, target, small["g_ffn"], small["g_final"], _slabs_to_cols(gw["w_ffn_gate"]),
        _slabs_to_cols(gw["w_ffn_up"]), gw["w_ffn_down"].reshape(D_FF, D), nb)
    loss = lax.psum(loss[0, 0], ("x", "y", "c"))

    def slabs(name, g):
        if name == "w_in":
            return g
        if name in ("w_ffn_gate", "w_ffn_up"):
            return _cols_to_slabs(g)
        if name == "w_pool":
            return g.reshape(4, N_SHARD, 64, GD).transpose(1, 0, 2, 3).reshape(N_SHARD, 4 * 64, GD)
        return g.reshape(N_SHARD, g.shape[0] // N_SHARD, g.shape[1])

    def reduce_start(group, grads, name):
        g32 = [slabs(n, grads[n][0]) for n in group]
        g16 = [slabs(n, grads[n][1]) for n in group]
        from_sibling = _swap_halves_bf16(g16, "swap_halves_" + name)
        halves = [_add_sibling_half(g, sb, core, "add_sibling_" + n) for n, g, sb in zip(group, g32, from_sibling)]
        return [h for h, _ in halves], _scatter_start([hb for _, hb in halves], "scatter_start_" + name)

    def reduce_finish(group, halves, start, after, name):
        from_chips = _scatter_wait(start[0], start[1], start[2], after, "scatter_wait_" + name)
        return [_add_chip_slabs(h, rb, idx, "add_chips_" + n) for n, h, rb in zip(group, halves, from_chips)]

    half_ff = D_FF // 2
    grads_ffn = {
        "w_ffn_gate": _wgrad(vb, dgb, D, half_ff, tk, "wgrad_ffn_gate"),
        "w_ffn_up": _wgrad(vb, dub, D, half_ff, tk, "wgrad_ffn_up"),
        "w_ffn_down": _wgrad(fb, dh2b, half_ff, D, tk, "wgrad_ffn_down"),
    }
    halves_ffn, sc_ffn = reduce_start(GROUP_FFN, grads_ffn, "ffn")

    dycb, dypb, dzg, dconv, dmwb, dm, db_gate, dln_g, dln_b, db_dw, dps = _mixers_bwd_rows(
        dh1b, yc, yp, z, small["b_gate"], cpre, small["ln_g"], small["ln_b"], mw, small["pool_scale"],
        w_o_b, w_co_b, w_po_b, w_pool_b, sc_ffn[3], nb)
    grads_mix = {
        "w_conv_out": _wgrad(ca, dycb, D, D, tk, "wgrad_conv_out"),
        "w_pool": _wgrad(m, dmwb, GD, GD, tk, "wgrad_pool", diag=True),
        "w_pool_out": _wgrad(m2b, dypb, D, D, tk, "wgrad_pool_out"),
        "w_o": _wgrad(mg, dh1b, D, D, tk, "wgrad_o"),
    }
    halves_mix, sc_mix = reduce_start(GROUP_MIX, grads_mix, "mix")
    dzb, grad_x, dhead, dw_dw, dg_mix = _mixers_bwd_halo(
        dconv, dm, z, dzg, small["w_dw"], head, xs, small["g_mix"], dh1, gw["w_in"], sc_mix[3], nb, t_total)
    grads_in = {"w_in": _wgrad(u, dzb, D, D_IN // N_SHARD, tk, "wgrad_in", col_major=True)}
    halves_in, sc_in = reduce_start(("w_in",), grads_in, "in")

    packed = jnp.concatenate(
        [dg_mix, db_gate.reshape(2, D), db_dw, dln_g, dln_b, dps, dg_ffn, dg_final, jnp.zeros((7, D), F32),
         dhead[PAD:], dw_dw], axis=0)
    summed = _allreduce_small(packed)

    first = GROUP_FFN + GROUP_MIX
    reduced_half = reduce_finish(GROUP_FFN, halves_ffn, sc_ffn, summed, "ffn")
    reduced_half += reduce_finish(GROUP_MIX, halves_mix, sc_mix, summed, "mix")
    reduced = dict(zip(first, _join_halves(reduced_half, "join_halves_first")))
    updates = {n: _adamw(reduced[n], _shard2d(n, w[n]), _shard2d(n, mom[n]), _shard2d(n, var[n]), "adamw_" + n)
               for n in first}
    last_half = reduce_finish(("w_in",), halves_in, sc_in, updates["w_o"][0], "in")
    reduced["w_in"] = _join_halves(last_half, "join_halves_in")[0]
    updates["w_in"] = _adamw(reduced["w_in"], w["w_in"][0], mom["w_in"][0], var["w_in"][0], "adamw_w_in")

    def repl_stack(d):
        return jnp.concatenate([d["g_mix"], d["b_gate"].reshape(2, D), d["b_dw"], d["ln_g"], d["ln_b"],
                                d["pool_scale"], d["g_ffn"], d["g_final"].reshape(1, D), jnp.ones((7, D), F32)], axis=0)

    def shard_stack(d):
        return jnp.concatenate([d["meta_tokens"], d["w_dw"][0], jnp.ones((1, GD), F32)], axis=0)

    g_repl = summed[0:16]
    g_shard = lax.dynamic_slice_in_dim(summed[16:64], chip * GD, GD, axis=1)
    d_repl, m_repl, v_repl = _adamw(g_repl, repl_stack(w), repl_stack(mom), repl_stack(var), "adamw_repl")
    d_shard, m_shard, v_shard = _adamw(g_shard, shard_stack(w), shard_stack(mom), shard_stack(var), "adamw_cols")

    def unpack(name, repl, shard):
        if name == "meta_tokens":
            return shard[0:N_META]
        if name == "w_dw":
            return shard[N_META:N_META + KW].reshape(1, KW, GD)
        row = {"g_mix": 0, "b_gate": 1, "b_dw": 3, "ln_g": 4, "ln_b": 5, "pool_scale": 6, "g_ffn": 7, "g_final": 8}[name]
        if name == "b_gate":
            return repl[1:3].reshape(1, 2 * D)
        if name == "g_final":
            return repl[8]
        return repl[row:row + 1]

    out_g, out_d, out_m, out_v = {}, {}, {}, {}
    for n in WEIGHT_ORDER:
        if n in BIG:
            g = reduced[n]
            d_, m_, v_ = updates[n]
            shape = w[n].shape
            out_g[n], out_d[n], out_m[n], out_v[n] = (a.reshape(shape) for a in (g, d_, m_, v_))
        else:
            out_g[n] = unpack(n, g_repl, g_shard)
            out_d[n] = unpack(n, d_repl, d_shard)
            out_m[n] = unpack(n, m_repl, m_shard)
            out_v[n] = unpack(n, v_repl, v_shard)
    return (loss, grad_x[None], *[out_g[n] for n in WEIGHT_ORDER], *[out_d[n] for n in WEIGHT_ORDER],
            *[out_m[n] for n in WEIGHT_ORDER], *[out_v[n] for n in WEIGHT_ORDER])
```

```python
import functools

import jax
import jax.numpy as jnp
from jax import lax
from jax.experimental import pallas as pl
from jax.experimental.pallas import tpu as pltpu

F32 = jnp.float32
BF16 = jnp.bfloat16
MESH = pl.DeviceIdType.MESH

D = 1024
N_META = 16
KW = 31
CPAD = KW // 2
POOL_WINDOWS = (2, 4, 8, 16)
GD = 256
D_IN = 5 * D
D_FF = 2816
N_SHARD = 4
BR = 256
HALO = 16
PAD = BR - N_META
EXT = BR + 2 * HALO
RMS_EPS = 1e-6
LN_EPS = 1e-5
LR, B1, B2, ADAM_EPS, WD, STEP = 0.001, 0.9, 0.999, 1e-08, 0.01, 10
VMEM_LIMIT = 56 * 1024 * 1024


def _cparams(sem, vmem=VMEM_LIMIT):
    return pltpu.CompilerParams(dimension_semantics=sem, vmem_limit_bytes=vmem)


def _dot(a, b):
    return jnp.dot(a, b, preferred_element_type=F32)


def _dot_nt(a, b):
    return lax.dot_general(a, b, (((1,), (1,)), ((), ())), preferred_element_type=F32)


def _dot_tn(a, b):
    return lax.dot_general(a, b, (((0,), (0,)), ((), ())), preferred_element_type=F32)


def _sigmoid(x):
    return 1.0 / (1.0 + jnp.exp(-x))


def _row_ids(i, n, offset=0):
    return lax.broadcasted_iota(jnp.int32, (n, 1), 0) + (i * BR + offset - PAD)


def _pool_cnt(t, w, t_total):
    left = w // 2
    right = w - 1 - left
    lo = jnp.clip(t - left, 0, t_total)
    hi = jnp.clip(t + right + 1, 0, t_total)
    return jnp.maximum(hi - lo, 1).astype(F32)


def _halo_specs(col, nb):
    last = nb * (BR // HALO) - 1
    return [
        pl.BlockSpec((HALO, D), lambda i: (jnp.maximum(i * (BR // HALO) - 1, 0), col)),
        pl.BlockSpec((BR, D), lambda i: (i, col)),
        pl.BlockSpec((HALO, D), lambda i: (jnp.minimum((i + 1) * (BR // HALO), last), col)),
    ]


def _fill_ext(ext_ref, prev, cur, nxt, i, nb):
    ext_ref[0:HALO, :] = jnp.where(i > 0, prev, 0.0)
    ext_ref[HALO:HALO + BR, :] = cur
    ext_ref[HALO + BR:EXT, :] = jnp.where(i < nb - 1, nxt, 0.0)


ROT_ROWS = EXT - 8


def _fill_rot(rot_ref, ext_ref, lanes):
    for r in range(1, 8):
        rot_ref[r] = ext_ref[pl.ds(r, ROT_ROWS), lanes]


def _tap(rot_ref, ext_ref, lanes, offset):
    q, r = divmod(offset, 8)
    if r == 0:
        return ext_ref[pl.ds(8 * q, BR), lanes]
    return rot_ref[r, pl.ds(8 * q, BR), :]


def _row_spec(width=D):
    return pl.BlockSpec((BR, width), lambda i: (i, 0))


def _x_spec():
    return pl.BlockSpec((BR, D), lambda i: (jnp.maximum(i - 1, 0), 0))


def _const_spec(shape):
    nd = len(shape)
    return pl.BlockSpec(shape, lambda i: (0,) * nd)


def _rms_u(head, x, g_mix, nb):
    def body(head_ref, x_ref, g_ref, u_ref):
        i = pl.program_id(0)
        h = jnp.where(i == 0, head_ref[...], x_ref[...])
        r = lax.rsqrt(jnp.mean(h * h, axis=-1, keepdims=True) + RMS_EPS)
        u_ref[...] = ((h * r) * g_ref[...]).astype(BF16)

    return pl.pallas_call(
        body, name="rms_u",
        grid=(nb,),
        in_specs=[_const_spec((BR, D)), _x_spec(), _const_spec((1, D))],
        out_specs=_row_spec(),
        out_shape=jax.ShapeDtypeStruct((nb * BR, D), BF16),
        compiler_params=_cparams(("arbitrary",)),
    )(head, x, g_mix)


def _in_proj(u, w_in_b, dep, nb):
    tp = nb * BR
    ns = w_in_b.shape[0]
    wcols = w_in_b.shape[2]

    def body(u_ref, w_ref, dep_ref, z_ref):
        z_ref[...] = _dot(u_ref[...], w_ref[...])

    return pl.pallas_call(
        body, name="in_proj",
        grid=(ns, nb),
        in_specs=[
            pl.BlockSpec((BR, D), lambda s, i: (i, 0)),
            pl.BlockSpec((None, D, wcols), lambda s, i: (s, 0, 0)),
            pl.BlockSpec((8, 128), lambda s, i: (0, 0)),
        ],
        out_specs=pl.BlockSpec((BR, wcols), lambda s, i: (i, s)),
        out_shape=jax.ShapeDtypeStruct((tp, ns * wcols), F32),
        compiler_params=_cparams(("arbitrary", "arbitrary")),
    )(u, w_in_b, dep)


def _mixers_fwd(z, head, x, b_gate, w_dw, b_dw, ln_g, ln_b, pool_scale, w_co, w_pool, w_po, w_o, nb, t_total):
    tp = nb * BR

    def body(avp, av, avn, agp, ag, agn, pp, pc, pn, za, zb, head_ref, x_ref, bg_ref, wdw_ref, bdw_ref,
             lng_ref, lnb_ref, ps_ref, wco_ref, wpool_ref, wpo_ref, wo_ref,
             h1_ref, yc_ref, yp_ref, mg_ref, ca_ref, cpre_ref, m_ref, mw_ref, m2b_ref, ext_ref, pext_ref, rot_ref):
        i = pl.program_id(0)
        _fill_ext(ext_ref, avp[...] * _sigmoid(agp[...]), av[...] * _sigmoid(ag[...]),
                  avn[...] * _sigmoid(agn[...]), i, nb)
        _fill_ext(pext_ref, pp[...], pc[...], pn[...], i, nb)

        def conv_chunk(c, carry):
            lanes = pl.ds(pl.multiple_of(c * 128, 128), 128)
            _fill_rot(rot_ref, ext_ref, lanes)
            acc = jnp.broadcast_to(bdw_ref[:, lanes], (BR, 128))
            for k in range(KW):
                acc = acc + wdw_ref[k:k + 1, lanes] * _tap(rot_ref, ext_ref, lanes, 1 + k)
            cpre_ref[:, lanes] = acc
            return carry
        lax.fori_loop(0, D // 128, conv_chunk, 0)

        conv = cpre_ref[...]
        mu = jnp.mean(conv, axis=-1, keepdims=True)
        xc = conv - mu
        rstd = lax.rsqrt(jnp.mean(xc * xc, axis=-1, keepdims=True) + LN_EPS)
        ln = (xc * rstd) * lng_ref[...] + lnb_ref[...]
        cact = (ln * _sigmoid(ln)).astype(BF16)
        ca_ref[...] = cact
        y_conv = _dot(cact, wco_ref[...])
        yc_ref[...] = y_conv

        t = _row_ids(i, BR)
        for gi, w in enumerate(POOL_WINDOWS):
            left = w // 2
            right = w - 1 - left
            lanes = slice(gi * GD, (gi + 1) * GD)
            s = pext_ref[pl.ds(HALO - left, BR), lanes]
            for j in range(-left + 1, right + 1):
                s = s + pext_ref[pl.ds(HALO + j, BR), lanes]
            m = (s / _pool_cnt(t, w, t_total) - pext_ref[HALO:HALO + BR, lanes]).astype(BF16)
            m_ref[:, lanes] = m
            mw_ref[:, lanes] = _dot(m, wpool_ref[gi])
        mw = mw_ref[...]
        m2b = (mw * ps_ref[...]).astype(BF16)
        m2b_ref[...] = m2b
        y_pool = _dot(m2b, wpo_ref[...])
        yp_ref[...] = y_pool

        s_a = _sigmoid(za[...] + bg_ref[:, 0:D])
        s_b = _sigmoid(zb[...] + bg_ref[:, D:2 * D])
        merged = (s_a * y_conv + s_b * y_pool).astype(BF16)
        mg_ref[...] = merged
        h0 = jnp.where(i == 0, head_ref[...], x_ref[...])
        h1_ref[...] = h0 + _dot(merged, wo_ref[...])

    in_specs = (_halo_specs(0, nb) + _halo_specs(1, nb) + _halo_specs(2, nb)
                + [pl.BlockSpec((BR, D), lambda i: (i, 3)), pl.BlockSpec((BR, D), lambda i: (i, 4)),
                   _const_spec((BR, D)), _x_spec(), _const_spec((1, 2 * D)), _const_spec((32, D)),
                   _const_spec((1, D)), _const_spec((1, D)), _const_spec((1, D)), _const_spec((1, D)),
                   _const_spec((D, D)), _const_spec((4, GD, GD)), _const_spec((D, D)), _const_spec((D, D))])
    outs = [(F32, "h1"), (F32, "yc"), (F32, "yp"), (BF16, "mg"), (BF16, "ca"), (F32, "cpre"), (BF16, "m"), (F32, "mw"),
            (BF16, "m2b")]
    return pl.pallas_call(
        body, name="mixers_fwd",
        grid=(nb,),
        in_specs=in_specs,
        out_specs=[_row_spec() for _ in outs],
        out_shape=[jax.ShapeDtypeStruct((tp, D), dt) for dt, _ in outs],
        scratch_shapes=[pltpu.VMEM((EXT, D), F32), pltpu.VMEM((EXT, D), F32), pltpu.VMEM((8, ROT_ROWS, 128), F32)],
        compiler_params=_cparams(("arbitrary",)),
    )(z, z, z, z, z, z, z, z, z, z, z, head, x, b_gate, w_dw, b_dw, ln_g, ln_b, pool_scale, w_co, w_pool, w_po, w_o)


def _ffn_fwd_bwd(h1, target, g_ffn, g_final, w_g, w_u, w_d, nb):
    tp = nb * BR

    def body(h1_ref, tgt_ref, gf_ref, gfin_ref, wg_hbm, wu_hbm, wd_hbm,
             dh1_ref, dh1b_ref, vb_ref, fb_ref, dgb_ref, dub_ref, dh2b_ref, loss_ref, dgf_ref, dgfin_ref,
             wg_ref, wu_ref, wd_ref, sem):
        i = pl.program_id(0)

        @pl.when(i == 0)
        def _():
            copies = [pltpu.make_async_copy(wg_hbm, wg_ref, sem.at[0]),
                      pltpu.make_async_copy(wu_hbm, wu_ref, sem.at[1]),
                      pltpu.make_async_copy(wd_hbm, wd_ref, sem.at[2])]
            for cp in copies:
                cp.start()
            loss_ref[...] = jnp.zeros_like(loss_ref)
            dgf_ref[...] = jnp.zeros_like(dgf_ref)
            dgfin_ref[...] = jnp.zeros_like(dgfin_ref)
            for cp in copies:
                cp.wait()

        h1 = h1_ref[...]
        r1 = lax.rsqrt(jnp.mean(h1 * h1, axis=-1, keepdims=True) + RMS_EPS)
        vn = h1 * r1
        vb = (vn * gf_ref[...]).astype(BF16)
        vb_ref[...] = vb
        g = _dot_nt(vb, wg_ref[...])
        up = _dot_nt(vb, wu_ref[...])
        sg = _sigmoid(g)
        sl = g * sg
        fb = (sl * up).astype(BF16)
        fb_ref[...] = fb
        h2 = h1 + _dot(fb, wd_ref[...])
        r2 = lax.rsqrt(jnp.mean(h2 * h2, axis=-1, keepdims=True) + RMS_EPS)
        yn = h2 * r2
        valid = i > 0
        diff = jnp.where(valid, yn * gfin_ref[...] - tgt_ref[...], 0.0)
        loss_ref[...] += 0.5 * jnp.sum(jnp.mean(diff * diff, axis=-1, keepdims=True))
        dy = diff * (1.0 / D)
        dgfin_ref[...] += jnp.sum(dy * yn, axis=0, keepdims=True)
        gd = dy * gfin_ref[...]
        dh2 = r2 * (gd - yn * jnp.mean(yn * gd, axis=-1, keepdims=True))
        dh2b = dh2.astype(BF16)
        dh2b_ref[...] = dh2b
        df = _dot_nt(dh2b, wd_ref[...])
        dub = (df * sl).astype(BF16)
        dgb = (df * up * (sg * (1.0 + g * (1.0 - sg)))).astype(BF16)
        dub_ref[...] = dub
        dgb_ref[...] = dgb
        dv = _dot(dgb, wg_ref[...]) + _dot(dub, wu_ref[...])
        dgf_ref[...] += jnp.sum(dv * vn, axis=0, keepdims=True)
        gd1 = dv * gf_ref[...]
        dh1 = dh2 + r1 * (gd1 - vn * jnp.mean(vn * gd1, axis=-1, keepdims=True))
        dh1_ref[...] = dh1
        dh1b_ref[...] = dh1.astype(BF16)

    any_spec = pl.BlockSpec(memory_space=pl.ANY)
    return pl.pallas_call(
        body, name="ffn_fwd_bwd",
        grid=(nb,),
        in_specs=[_row_spec(), _x_spec(), _const_spec((1, D)), _const_spec((1, D)), any_spec, any_spec, any_spec],
        out_specs=[_row_spec(), _row_spec(), _row_spec(), _row_spec(D_FF), _row_spec(D_FF), _row_spec(D_FF), _row_spec(),
                   _const_spec((1, 1)), _const_spec((1, D)), _const_spec((1, D))],
        out_shape=[jax.ShapeDtypeStruct((tp, D), F32), jax.ShapeDtypeStruct((tp, D), BF16),
                   jax.ShapeDtypeStruct((tp, D), BF16), jax.ShapeDtypeStruct((tp, D_FF), BF16),
                   jax.ShapeDtypeStruct((tp, D_FF), BF16), jax.ShapeDtypeStruct((tp, D_FF), BF16),
                   jax.ShapeDtypeStruct((tp, D), BF16), jax.ShapeDtypeStruct((1, 1), F32),
                   jax.ShapeDtypeStruct((1, D), F32), jax.ShapeDtypeStruct((1, D), F32)],
        scratch_shapes=[pltpu.VMEM((D_FF, D), BF16), pltpu.VMEM((D_FF, D), BF16), pltpu.VMEM((D_FF, D), BF16),
                        pltpu.SemaphoreType.DMA((3,))],
        compiler_params=_cparams(("arbitrary",)),
    )(h1, target, g_ffn, g_final, w_g, w_u, w_d)


def _mixers_bwd_rows(dh1b, yc, yp, z, b_gate, cpre, ln_g, ln_b, mw, pool_scale, w_o, w_co, w_po, w_pool, dep, nb):
    tp = nb * BR

    def body(dh1b_ref, yc_ref, yp_ref, za, zb, bg_ref, cpre_ref, lng_ref, lnb_ref, mw_ref, ps_ref,
             wo_ref, wco_ref, wpo_ref, wpool_ref, dep_ref,
             dycb_ref, dypb_ref, dzg_ref, dconv_ref, dmwb_ref, dm_ref, dbg_ref, dlng_ref, dlnb_ref, dbdw_ref, dps_ref):
        i = pl.program_id(0)

        @pl.when(i == 0)
        def _():
            for r in (dbg_ref, dlng_ref, dlnb_ref, dbdw_ref, dps_ref):
                r[...] = jnp.zeros_like(r)

        dmg = _dot_nt(dh1b_ref[...], wo_ref[...])
        s_a = _sigmoid(za[...] + bg_ref[:, 0:D])
        s_b = _sigmoid(zb[...] + bg_ref[:, D:2 * D])
        dycb = (dmg * s_a).astype(BF16)
        dypb = (dmg * s_b).astype(BF16)
        dycb_ref[...] = dycb
        dypb_ref[...] = dypb
        dza = dmg * yc_ref[...] * (s_a * (1.0 - s_a))
        dzb = dmg * yp_ref[...] * (s_b * (1.0 - s_b))
        dzg_ref[:, 0:D] = dza.astype(BF16)
        dzg_ref[:, D:2 * D] = dzb.astype(BF16)
        dbg_ref[:, 0:D] += jnp.sum(dza, axis=0, keepdims=True)
        dbg_ref[:, D:2 * D] += jnp.sum(dzb, axis=0, keepdims=True)

        dca = _dot_nt(dycb, wco_ref[...])
        conv = cpre_ref[...]
        mu = jnp.mean(conv, axis=-1, keepdims=True)
        xc = conv - mu
        rstd = lax.rsqrt(jnp.mean(xc * xc, axis=-1, keepdims=True) + LN_EPS)
        xhat = xc * rstd
        ln = xhat * lng_ref[...] + lnb_ref[...]
        sg = _sigmoid(ln)
        dln = dca * (sg * (1.0 + ln * (1.0 - sg)))
        dlng_ref[...] += jnp.sum(dln * xhat, axis=0, keepdims=True)
        dlnb_ref[...] += jnp.sum(dln, axis=0, keepdims=True)
        dxh = dln * lng_ref[...]
        dconv = rstd * (dxh - jnp.mean(dxh, axis=-1, keepdims=True)
                        - xhat * jnp.mean(dxh * xhat, axis=-1, keepdims=True))
        dconv_ref[...] = dconv
        dbdw_ref[...] += jnp.sum(dconv, axis=0, keepdims=True)

        dm2 = _dot_nt(dypb, wpo_ref[...])
        dps_ref[...] += jnp.sum(dm2 * mw_ref[...], axis=0, keepdims=True)
        dmwb = (dm2 * ps_ref[...]).astype(BF16)
        dmwb_ref[...] = dmwb
        for gi in range(len(POOL_WINDOWS)):
            lanes = slice(gi * GD, (gi + 1) * GD)
            dm_ref[:, lanes] = _dot_nt(dmwb[:, lanes], wpool_ref[gi])

    in_specs = [_row_spec(), _row_spec(), _row_spec(),
                pl.BlockSpec((BR, D), lambda i: (i, 3)), pl.BlockSpec((BR, D), lambda i: (i, 4)),
                _const_spec((1, 2 * D)), _row_spec(), _const_spec((1, D)), _const_spec((1, D)), _row_spec(),
                _const_spec((1, D)), _const_spec((D, D)), _const_spec((D, D)), _const_spec((D, D)),
                _const_spec((4, GD, GD)), _const_spec((8, 128))]
    return pl.pallas_call(
        body, name="mixers_bwd_rows",
        grid=(nb,),
        in_specs=in_specs,
        out_specs=[_row_spec(), _row_spec(), _row_spec(2 * D), _row_spec(), _row_spec(), _row_spec(),
                   _const_spec((1, 2 * D)), _const_spec((1, D)), _const_spec((1, D)), _const_spec((1, D)),
                   _const_spec((1, D))],
        out_shape=[jax.ShapeDtypeStruct((tp, D), BF16), jax.ShapeDtypeStruct((tp, D), BF16),
                   jax.ShapeDtypeStruct((tp, 2 * D), BF16), jax.ShapeDtypeStruct((tp, D), F32),
                   jax.ShapeDtypeStruct((tp, D), BF16), jax.ShapeDtypeStruct((tp, D), F32),
                   jax.ShapeDtypeStruct((1, 2 * D), F32), jax.ShapeDtypeStruct((1, D), F32),
                   jax.ShapeDtypeStruct((1, D), F32), jax.ShapeDtypeStruct((1, D), F32),
                   jax.ShapeDtypeStruct((1, D), F32)],
        compiler_params=_cparams(("arbitrary",)),
    )(dh1b, yc, yp, z, z, b_gate, cpre, ln_g, ln_b, mw, pool_scale, w_o, w_co, w_po, w_pool, dep)


def _mixers_bwd_halo(dconv, dm, z, dzg, w_dw, head, x, g_mix, dh1, w_in_b, dep, nb, t_total):
    tp = nb * BR
    ns = w_in_b.shape[0]
    wcols = w_in_b.shape[2]
    seq = x.shape[0]

    def body(dcp, dcc, dcn, dmp, dmc, dmn, avp, av, avn, agp, ag, agn, dzg_ref, wdw_ref, head_ref, x_ref, g_ref,
             dh1_ref, w_hbm, dep_ref,
             dzb_ref, gx_ref, dhead_ref, dwdw_ref, dgmix_ref,
             w_ref, sem, aext_ref, dext_ref, qext_ref, da_ref, rot_ref, dwp_ref):
        i = pl.program_id(0)

        @pl.when(i == 0)
        def _():
            cp = pltpu.make_async_copy(w_hbm, w_ref, sem.at[0])
            cp.start()
            dwp_ref[...] = jnp.zeros_like(dwp_ref)
            dgmix_ref[...] = jnp.zeros_like(dgmix_ref)
            cp.wait()

        sig_g = _sigmoid(ag[...])
        _fill_ext(aext_ref, avp[...] * _sigmoid(agp[...]), av[...] * sig_g, avn[...] * _sigmoid(agn[...]), i, nb)
        _fill_ext(dext_ref, dcp[...], dcc[...], dcn[...], i, nb)
        _fill_ext(qext_ref, dmp[...], dmc[...], dmn[...], i, nb)

        def conv_chunk(c, carry):
            lanes = pl.ds(pl.multiple_of(c * 128, 128), 128)
            _fill_rot(rot_ref, dext_ref, lanes)
            acc = jnp.zeros((BR, 128), F32)
            for k in range(KW):
                acc = acc + wdw_ref[k:k + 1, lanes] * _tap(rot_ref, dext_ref, lanes, KW - k)
            da_ref[:, lanes] = acc
            _fill_rot(rot_ref, aext_ref, lanes)
            dcv = dext_ref[HALO:HALO + BR, lanes]
            for k in range(KW):
                prod = _tap(rot_ref, aext_ref, lanes, 1 + k) * dcv
                dwp_ref[k, :, lanes] += jnp.sum(prod.reshape(BR // 8, 8, 128), axis=0)
            return carry
        lax.fori_loop(0, D // 128, conv_chunk, 0)

        @pl.when(i == nb - 1)
        def _():
            dwdw_ref[...] = jnp.sum(dwp_ref[...], axis=1)

        da = da_ref[...]
        a_val = av[...]
        dzb_ref[:, 0:D] = (da * sig_g).astype(BF16)
        dzb_ref[:, D:2 * D] = (da * a_val * (sig_g * (1.0 - sig_g))).astype(BF16)

        t_ext = _row_ids(i, EXT, -HALO)
        for gi, w in enumerate(POOL_WINDOWS):
            left = w // 2
            right = w - 1 - left
            lanes = slice(gi * GD, (gi + 1) * GD)
            qext_ref[:, lanes] = qext_ref[:, lanes] / _pool_cnt(t_ext, w, t_total)
            s = qext_ref[pl.ds(HALO - right, BR), lanes]
            for j in range(-right + 1, left + 1):
                s = s + qext_ref[pl.ds(HALO + j, BR), lanes]
            dzb_ref[:, 2 * D + gi * GD:2 * D + (gi + 1) * GD] = (s - dmc[:, lanes]).astype(BF16)
        dzb_ref[:, 3 * D:5 * D] = dzg_ref[...]

        du = _dot_nt(dzb_ref[:, 0:wcols], w_ref[0])
        for s_i in range(1, ns):
            du = du + _dot_nt(dzb_ref[:, s_i * wcols:(s_i + 1) * wcols], w_ref[s_i])
        h0 = jnp.where(i == 0, head_ref[...], x_ref[...])
        r0 = lax.rsqrt(jnp.mean(h0 * h0, axis=-1, keepdims=True) + RMS_EPS)
        un = h0 * r0
        dgmix_ref[...] += jnp.sum(du * un, axis=0, keepdims=True)
        gd = du * g_ref[...]
        dh0 = dh1_ref[...] + r0 * (gd - un * jnp.mean(un * gd, axis=-1, keepdims=True))
        gx_ref[...] = dh0

        @pl.when(i == 0)
        def _():
            dhead_ref[...] = dh0

    any_spec = pl.BlockSpec(memory_space=pl.ANY)
    in_specs = (_halo_specs(0, nb) + _halo_specs(0, nb) + _halo_specs(0, nb) + _halo_specs(1, nb)
                + [_row_spec(2 * D), _const_spec((32, D)), _const_spec((BR, D)), _x_spec(), _const_spec((1, D)),
                   _row_spec(), any_spec, _const_spec((8, 128))])
    return pl.pallas_call(
        body, name="mixers_bwd_halo",
        grid=(nb,),
        in_specs=in_specs,
        out_specs=[_row_spec(D_IN), _x_spec(), _const_spec((BR, D)), _const_spec((32, D)), _const_spec((1, D))],
        out_shape=[jax.ShapeDtypeStruct((tp, D_IN), BF16), jax.ShapeDtypeStruct((seq, D), F32),
                   jax.ShapeDtypeStruct((BR, D), F32), jax.ShapeDtypeStruct((32, D), F32),
                   jax.ShapeDtypeStruct((1, D), F32)],
        scratch_shapes=[pltpu.VMEM((ns, D, wcols), BF16), pltpu.SemaphoreType.DMA((1,)),
                        pltpu.VMEM((EXT, D), F32), pltpu.VMEM((EXT, D), F32), pltpu.VMEM((EXT, D), F32),
                        pltpu.VMEM((BR, D), F32), pltpu.VMEM((8, ROT_ROWS, 128), F32), pltpu.VMEM((32, 8, D), F32)],
        compiler_params=_cparams(("arbitrary",)),
    )(dconv, dconv, dconv, dm, dm, dm, z, z, z, z, z, z, dzg, w_dw, head, x, g_mix, dh1, w_in_b, dep)


def _wgrad(a, c, tm, tn, tk, name, diag=False, col_major=False):
    tp, m = a.shape
    n = c.shape[1]
    nk = tp // tk
    gm, gn = m // tm, n // tn

    def body(a_ref, c_ref, o_ref, ob_ref):
        k = pl.program_id(2)

        @pl.when(k == 0)
        def _():
            o_ref[...] = jnp.zeros_like(o_ref)

        o_ref[...] += _dot_tn(a_ref[...], c_ref[...])

        @pl.when(k == nk - 1)
        def _():
            ob_ref[...] = o_ref[...].astype(BF16)

    c_map = lambda i, j, k: (k, j)
    grid = (gm, gn, nk)
    if diag:
        grid = (gm, 1, nk)
        c_map = lambda i, j, k: (k, i)
        o_spec = pl.BlockSpec((tm, tn), lambda i, j, k: (i, 0))
        o_shape = (m, tn)
    elif col_major:
        o_spec = pl.BlockSpec((None, tm, tn), lambda i, j, k: (j, i, 0))
        o_shape = (gn, m, tn)
    else:
        o_spec = pl.BlockSpec((tm, tn), lambda i, j, k: (i, j))
        o_shape = (m, n)
    return pl.pallas_call(
        body, name=name,
        grid=grid,
        in_specs=[pl.BlockSpec((tk, tm), lambda i, j, k: (k, i)), pl.BlockSpec((tk, tn), c_map)],
        out_specs=[o_spec, o_spec],
        out_shape=[jax.ShapeDtypeStruct(o_shape, F32), jax.ShapeDtypeStruct(o_shape, BF16)],
        compiler_params=_cparams(("arbitrary", "arbitrary", "arbitrary")),
    )(a, c)


def _place():
    x, y, c = lax.axis_index("x"), lax.axis_index("y"), lax.axis_index("c")
    others = [(1 - x, y), (x, 1 - y), (1 - x, 1 - y)]
    return x, y, c, others


def _split2(a, axis=0):
    return a.reshape(a.shape[:axis] + (2, a.shape[axis] // 2) + a.shape[axis + 1:])


def _merge2(a, axis=0):
    return a.reshape(a.shape[:axis] + (2 * a.shape[axis + 1],) + a.shape[axis + 2:])


def _cast_into_slot(w2d, chip, name):
    r, c = w2d.shape
    r2 = r // 2

    def body(chip_ref, w_ref, o_ref):
        o_ref[...] = w_ref[...].astype(BF16)

    return pl.pallas_call(
        body, name=name,
        grid_spec=pltpu.PrefetchScalarGridSpec(
            num_scalar_prefetch=1, grid=(2,),
            in_specs=[pl.BlockSpec((r2, c), lambda h, chip_ref: (h, 0))],
            out_specs=pl.BlockSpec((None, None, r2, c), lambda h, chip_ref: (chip_ref[0], h, 0, 0))),
        out_shape=jax.ShapeDtypeStruct((N_SHARD, 2, r2, c), BF16),
        compiler_params=_cparams(("arbitrary",)),
    )(chip, w2d)


HBM_SPEC = pl.BlockSpec(memory_space=pltpu.HBM)
SEM_SPEC = pl.BlockSpec(memory_space=pltpu.SEMAPHORE)
DATAFLOW = pltpu.SideEffectType.DATAFLOW_SIDE_EFFECTING
TOKEN = jax.ShapeDtypeStruct((8, 128), F32)


def _in_hbm(a):
    return pltpu.with_memory_space_constraint(a, pltpu.HBM)


def _gather_tiny(v):
    vm = pl.BlockSpec(memory_space=pltpu.VMEM)

    def body(v_ref, out_ref, send_sems, recv_sems):
        x, y, c, others = _place()
        mine = 2 * x + y
        sends = [pltpu.make_async_remote_copy(
            src_ref=v_ref, dst_ref=out_ref.at[mine], send_sem=send_sems.at[j], recv_sem=recv_sems.at[j],
            device_id=(*chip, c), device_id_type=MESH) for j, chip in enumerate(others)]
        for cp in sends:
            cp.start()
        out_ref[mine] = v_ref[...]
        for j, chip in enumerate(others):
            landed = out_ref.at[2 * chip[0] + chip[1]]
            pltpu.make_async_remote_copy(
                src_ref=landed, dst_ref=landed, send_sem=send_sems.at[j], recv_sem=recv_sems.at[j],
                device_id=(x, y, c), device_id_type=MESH).wait_recv()
        for cp in sends:
            cp.wait_send()

    return pl.pallas_call(
        body, name="gather_tiny",
        in_specs=[vm], out_specs=vm,
        out_shape=jax.ShapeDtypeStruct((N_SHARD,) + v.shape, v.dtype),
        scratch_shapes=[pltpu.SemaphoreType.DMA((3,)), pltpu.SemaphoreType.DMA((3,))],
    )(v)


def _ici_copies(srcs, dsts, send_sems, recv_sems, started):
    x, y, c, others = _place()
    mine = 2 * x + y
    copies = []
    for a in range(len(srcs)):
        for j, chip in enumerate(others):
            there = 2 * chip[0] + chip[1]
            src, dst = srcs[a](mine, there, c), dsts[a](mine, there, c)
            if not started:
                dst = dsts[a](there, mine, c)
            copies.append(pltpu.make_async_remote_copy(
                src_ref=src, dst_ref=dst, send_sem=send_sems.at[a * 3 + j], recv_sem=recv_sems.at[a * 3 + j],
                device_id=(*chip, c), device_id_type=MESH))
    return copies


def _split_start(srcs_of, dsts_of, arrays, n_src, name):
    n = len(arrays)

    def body(*refs):
        ins = refs[:n]
        send_sems, recv_sems = refs[n], refs[n + 1]
        token = refs[2 * n + 2]
        for cp in _ici_copies(srcs_of(ins), dsts_of(ins), send_sems, recv_sems, True):
            cp.start()
        token[...] = jnp.zeros_like(token)

    out = pl.pallas_call(
        body, name=name,
        in_specs=[HBM_SPEC] * n,
        out_specs=(SEM_SPEC, SEM_SPEC, *([HBM_SPEC] * n), pl.BlockSpec(memory_space=pltpu.VMEM)),
        out_shape=(pltpu.SemaphoreType.DMA((3 * n_src,)), pltpu.SemaphoreType.DMA((3 * n_src,)),
                   *[pltpu.HBM(a.shape, a.dtype) for a in arrays], TOKEN),
        input_output_aliases={a: 2 + a for a in range(n)},
        compiler_params=pltpu.CompilerParams(has_side_effects=DATAFLOW),
    )(*[_in_hbm(a) for a in arrays])
    return out[0], out[1], list(out[2:2 + n]), out[2 + n]


def _split_wait(srcs_of, dsts_of, send_sems, recv_sems, arrays, after, name):
    n = len(arrays)

    def body(*refs):
        ins = refs[:n]
        send_sems, recv_sems = refs[n], refs[n + 1]
        for cp in _ici_copies(srcs_of(ins), dsts_of(ins), send_sems, recv_sems, False):
            cp.wait_send()
            cp.wait_recv()

    return pl.pallas_call(
        body, name=name,
        in_specs=[HBM_SPEC] * n + [SEM_SPEC, SEM_SPEC, pl.BlockSpec(memory_space=pl.ANY)],
        out_specs=[HBM_SPEC] * n,
        out_shape=[pltpu.HBM(a.shape, a.dtype) for a in arrays],
        input_output_aliases={a: a for a in range(n)},
        compiler_params=pltpu.CompilerParams(has_side_effects=DATAFLOW),
    )(*arrays, send_sems, recv_sems, after)


def _gather_views(ins):
    view = [lambda frm, to, c, r=r: r.at[frm, c] for r in ins]
    return view


def _gather_start(bufs, name):
    return _split_start(_gather_views, _gather_views, bufs, len(bufs), name)


def _gather_wait(send_sems, recv_sems, bufs, after, name):
    return _split_wait(_gather_views, _gather_views, send_sems, recv_sems, bufs, after, name)


def _forward_halves(bufs, name):
    n = len(bufs)
    any_spec = pl.BlockSpec(memory_space=pl.ANY)

    def body(*refs):
        outs = refs[n:2 * n]
        send_sems, recv_sems = refs[2 * n:]
        x, y, c, others = _place()
        copies = []
        for a in range(n):
            for j, chip in enumerate(others):
                landed = outs[a].at[2 * chip[0] + chip[1], c]
                copies.append(pltpu.make_async_remote_copy(
                    src_ref=landed, dst_ref=landed, send_sem=send_sems.at[a * 3 + j], recv_sem=recv_sems.at[a * 3 + j],
                    device_id=(x, y, 1 - c), device_id_type=MESH))
        for cp in copies:
            cp.start()
        for a in range(n):
            for j, chip in enumerate(others):
                landed = outs[a].at[2 * chip[0] + chip[1], 1 - c]
                pltpu.make_async_remote_copy(
                    src_ref=landed, dst_ref=landed, send_sem=send_sems.at[a * 3 + j], recv_sem=recv_sems.at[a * 3 + j],
                    device_id=(x, y, c), device_id_type=MESH).wait_recv()
        for cp in copies:
            cp.wait_send()

    out = pl.pallas_call(
        body, name=name,
        in_specs=[any_spec] * n, out_specs=[any_spec] * n,
        out_shape=[jax.ShapeDtypeStruct(b.shape, b.dtype) for b in bufs],
        input_output_aliases={a: a for a in range(n)},
        scratch_shapes=[pltpu.SemaphoreType.DMA((3 * n,)), pltpu.SemaphoreType.DMA((3 * n,))],
    )(*bufs)
    return [_merge2(o, 1) for o in out]


def _swap_halves_bf16(gbs, name):
    n = len(gbs)
    any_spec = pl.BlockSpec(memory_space=pl.ANY)

    def body(*refs):
        ins, outs = refs[:n], refs[n:2 * n]
        send_sems, recv_sems = refs[2 * n:]
        x, y, c, _ = _place()
        copies = []
        for a in range(n):
            copies.append(pltpu.make_async_remote_copy(
                src_ref=ins[a].at[:, 1 - c], dst_ref=outs[a], send_sem=send_sems.at[a], recv_sem=recv_sems.at[a],
                device_id=(x, y, 1 - c), device_id_type=MESH))
        for cp in copies:
            cp.start()
        for cp in copies:
            cp.wait()

    return pl.pallas_call(
        body, name=name,
        in_specs=[any_spec] * n, out_specs=[any_spec] * n,
        out_shape=[jax.ShapeDtypeStruct((g.shape[0], g.shape[1] // 2, g.shape[2]), g.dtype) for g in gbs],
        scratch_shapes=[pltpu.SemaphoreType.DMA((n,)), pltpu.SemaphoreType.DMA((n,))],
    )(*[_split2(g, 1) for g in gbs])


def _scatter_srcs(n):
    return lambda ins: [lambda frm, to, c, r=r: r.at[to] for r in ins[:n]]


def _scatter_dsts(n):
    return lambda ins: [lambda frm, to, c, r=r: r.at[frm] for r in ins[n:]]


def _scatter_start(hbs, name):
    n = len(hbs)
    lands = [lax.empty(h.shape, h.dtype) for h in hbs]
    return _split_start(_scatter_srcs(n), _scatter_dsts(n), list(hbs) + lands, n, name)


def _scatter_wait(send_sems, recv_sems, arrays, after, name):
    n = len(arrays) // 2
    return _split_wait(_scatter_srcs(n), _scatter_dsts(n), send_sems, recv_sems, arrays, after, name)[n:]


def _join_halves(rhs, name):
    n = len(rhs)
    any_spec = pl.BlockSpec(memory_space=pl.ANY)

    def body(*refs):
        outs = refs[n:2 * n]
        send_sems, recv_sems = refs[2 * n:]
        x, y, c, _ = _place()
        copies = []
        for a in range(n):
            copies.append(pltpu.make_async_remote_copy(
                src_ref=outs[a].at[c], dst_ref=outs[a].at[c], send_sem=send_sems.at[a],
                recv_sem=recv_sems.at[a], device_id=(x, y, 1 - c), device_id_type=MESH))
        for cp in copies:
            cp.start()
        for a in range(n):
            landed = outs[a].at[1 - c]
            pltpu.make_async_remote_copy(
                src_ref=landed, dst_ref=landed, send_sem=send_sems.at[a], recv_sem=recv_sems.at[a],
                device_id=(x, y, c), device_id_type=MESH).wait_recv()
        for cp in copies:
            cp.wait_send()

    out = pl.pallas_call(
        body, name=name,
        in_specs=[any_spec] * n, out_specs=[any_spec] * n,
        out_shape=[jax.ShapeDtypeStruct(r.shape, r.dtype) for r in rhs],
        input_output_aliases={a: a for a in range(n)},
        scratch_shapes=[pltpu.SemaphoreType.DMA((n,)), pltpu.SemaphoreType.DMA((n,))],
    )(*rhs)
    return [_merge2(o) for o in out]


def _allreduce_small(v):
    rows, cols = v.shape
    vm = pl.BlockSpec(memory_space=pltpu.VMEM)
    flips = [(dx, dy, dc) for dx in (0, 1) for dy in (0, 1) for dc in (0, 1)][1:]

    def body(v_ref, out_ref, buf_ref, send_sems, recv_sems):
        x, y, c, _ = _place()
        mine = 4 * x + 2 * y + c
        copies = []
        for k, (dx, dy, dc) in enumerate(flips):
            px, py, pc = jnp.bitwise_xor(x, dx), jnp.bitwise_xor(y, dy), jnp.bitwise_xor(c, dc)
            copies.append(pltpu.make_async_remote_copy(
                src_ref=v_ref, dst_ref=buf_ref.at[mine], send_sem=send_sems.at[k], recv_sem=recv_sems.at[k],
                device_id=(px, py, pc), device_id_type=MESH))
        for cp in copies:
            cp.start()
        buf_ref[mine] = v_ref[...]
        for k, (dx, dy, dc) in enumerate(flips):
            src = 4 * jnp.bitwise_xor(x, dx) + 2 * jnp.bitwise_xor(y, dy) + jnp.bitwise_xor(c, dc)
            pltpu.make_async_remote_copy(
                src_ref=v_ref, dst_ref=buf_ref.at[src], send_sem=send_sems.at[k], recv_sem=recv_sems.at[k],
                device_id=(x, y, c), device_id_type=MESH).wait_recv()
        for cp in copies:
            cp.wait_send()
        acc = buf_ref[0]
        for d in range(1, 8):
            acc = acc + buf_ref[d]
        out_ref[...] = acc

    return pl.pallas_call(
        body, name="allreduce_small",
        in_specs=[vm], out_specs=vm,
        out_shape=jax.ShapeDtypeStruct((rows, cols), F32),
        scratch_shapes=[pltpu.VMEM((8, rows, cols), F32), pltpu.SemaphoreType.DMA((7,)), pltpu.SemaphoreType.DMA((7,))],
    )(v)


def _row_block(r):
    for cand in (512, 352, 256, 128, 64, 48, 16):
        if r % cand == 0:
            return cand
    return r


def _add_sibling_half(g, sb, core, name):
    ns, r, c = g.shape
    r2 = r // 2

    def body(core_ref, g_ref, sb_ref, h_ref, hb_ref):
        h = g_ref[...] + sb_ref[...].astype(F32)
        h_ref[...] = h
        hb_ref[...] = h.astype(BF16)

    spec = pl.BlockSpec((None, r2, c), lambda s, core_ref: (s, 0, 0))
    return pl.pallas_call(
        body, name=name,
        grid_spec=pltpu.PrefetchScalarGridSpec(
            num_scalar_prefetch=1, grid=(ns,),
            in_specs=[pl.BlockSpec((None, r2, c), lambda s, core_ref: (s, core_ref[0], 0)), spec],
            out_specs=[spec, spec]),
        out_shape=[jax.ShapeDtypeStruct((ns, r2, c), F32), jax.ShapeDtypeStruct((ns, r2, c), BF16)],
        compiler_params=_cparams(("arbitrary",)),
    )(core, g, sb)


def _add_chip_slabs(h, rb, idx, name):
    ns, r2, c = h.shape

    def body(idx_ref, h_ref, r0_ref, r1_ref, r2_ref, o_ref):
        o_ref[...] = ((h_ref[...] + r0_ref[...].astype(F32)) + r1_ref[...].astype(F32)) + r2_ref[...].astype(F32)

    def pick(k):
        return pl.BlockSpec((None, r2, c), lambda i, idx_ref: (idx_ref[k], 0, 0))

    return pl.pallas_call(
        body, name=name,
        grid_spec=pltpu.PrefetchScalarGridSpec(
            num_scalar_prefetch=1, grid=(1,),
            in_specs=[pick(0), pick(1), pick(2), pick(3)],
            out_specs=pl.BlockSpec((None, r2, c), lambda i, idx_ref: (idx_ref[4], 0, 0))),
        out_shape=jax.ShapeDtypeStruct((2, r2, c), F32),
        compiler_params=_cparams(("arbitrary",)),
    )(idx, h, rb, rb, rb)


def _adamw(g, w, m, v, name):
    r, c = g.shape
    br = _row_block(r)

    def body(g_ref, w_ref, m_ref, v_ref, go_ref, d_ref, nm_ref, nv_ref):
        gg = g_ref[...]
        go_ref[...] = gg
        nm = B1 * m_ref[...] + (1.0 - B1) * gg
        nv = B2 * v_ref[...] + (1.0 - B2) * jnp.square(gg)
        m_hat = nm / (1.0 - B1 ** STEP)
        v_hat = nv / (1.0 - B2 ** STEP)
        d_ref[...] = -LR * (m_hat / (jnp.sqrt(v_hat) + ADAM_EPS) + WD * w_ref[...])
        nm_ref[...] = nm
        nv_ref[...] = nv

    spec = pl.BlockSpec((br, c), lambda i: (i, 0))
    return pl.pallas_call(
        body, name=name,
        grid=(r // br,),
        in_specs=[spec] * 4, out_specs=[spec] * 4,
        out_shape=[jax.ShapeDtypeStruct((r, c), F32)] * 4,
        compiler_params=_cparams(("arbitrary",)),
    )(g, w, m, v)


BIG = ("w_in", "w_conv_out", "w_pool", "w_pool_out", "w_o", "w_ffn_gate", "w_ffn_up", "w_ffn_down")
REPL = ("g_mix", "b_gate", "b_dw", "ln_g", "ln_b", "pool_scale", "g_ffn", "g_final")
GROUP_MIX = ("w_conv_out", "w_pool", "w_pool_out", "w_o")
GROUP_FFN = ("w_ffn_gate", "w_ffn_up", "w_ffn_down")
TRANSPOSED = ("w_ffn_gate", "w_ffn_up")
WEIGHT_ORDER = ("meta_tokens", "g_mix", "w_in", "b_gate", "w_dw", "b_dw", "ln_g", "ln_b", "w_conv_out", "w_pool",
                "pool_scale", "w_pool_out", "w_o", "g_ffn", "w_ffn_gate", "w_ffn_up", "w_ffn_down", "g_final")


def _shard2d(name, a):
    a = a[0]
    if name == "w_pool":
        return a.reshape(4 * 64, GD)
    if name in TRANSPOSED:
        return a.T
    return a


def _unshard2d(name, a, shape):
    return a.T.reshape(shape) if name in TRANSPOSED else a.reshape(shape)


def _cols_to_slabs(a):
    m, n = a.shape
    return a.reshape(m, N_SHARD, n // N_SHARD).transpose(1, 0, 2)


def _slabs_to_cols(a):
    ns, m, c = a.shape
    return a.transpose(1, 0, 2).reshape(m, ns * c)


def kernel(x, meta_tokens, g_mix, w_in, b_gate, w_dw, b_dw, ln_g, ln_b, w_conv_out, w_pool, pool_scale, w_pool_out, w_o, g_ffn, w_ffn_gate, w_ffn_up, w_ffn_down, g_final, loss_target, m_meta_tokens, m_g_mix, m_w_in, m_b_gate, m_w_dw, m_b_dw, m_ln_g, m_ln_b, m_w_conv_out, m_w_pool, m_pool_scale, m_w_pool_out, m_w_o, m_g_ffn, m_w_ffn_gate, m_w_ffn_up, m_w_ffn_down, m_g_final, v_meta_tokens, v_g_mix, v_w_in, v_b_gate, v_w_dw, v_b_dw, v_ln_g, v_ln_b, v_w_conv_out, v_w_pool, v_pool_scale, v_w_pool_out, v_w_o, v_g_ffn, v_w_ffn_gate, v_w_ffn_up, v_w_ffn_down, v_g_final):
    args = dict(locals())
    w = {n: args[n] for n in WEIGHT_ORDER}
    mom = {n: args["m_" + n] for n in WEIGHT_ORDER}
    var = {n: args["v_" + n] for n in WEIGHT_ORDER}
    seq = x.shape[1]
    nb = seq // BR + 1
    tp = nb * BR
    tk = tp // 2 if (tp // 2) % 16 == 0 else BR
    t_total = seq + N_META
    cx, cy, cc = lax.axis_index("x"), lax.axis_index("y"), lax.axis_index("c")
    chip = 2 * cx + cy
    chip1 = jnp.reshape(chip, (1,)).astype(jnp.int32)
    core = jnp.reshape(cc, (1,)).astype(jnp.int32)
    others = jnp.sort(jnp.stack([2 * (1 - cx) + cy, 2 * cx + (1 - cy), 2 * (1 - cx) + (1 - cy)]))
    idx = jnp.concatenate([chip1, others.astype(jnp.int32), core])
    xs, target = x[0], loss_target[0]

    tiny = _gather_tiny(jnp.concatenate([w["meta_tokens"], w["w_dw"][0], jnp.zeros((1, GD), F32)], axis=0))
    small = {n: w[n] for n in REPL if n != "g_final"}
    small["g_final"] = w["g_final"].reshape(1, D)
    small["w_dw"] = _slabs_to_cols(tiny[:, N_META:])
    head = jnp.concatenate([jnp.zeros((PAD, D), F32), _slabs_to_cols(tiny[:, :N_META])], axis=0)
    bufs = {n: _cast_into_slot(_shard2d(n, w[n]), chip1, "cast_" + n) for n in BIG}

    def gather_start(group, name):
        return _gather_start([bufs[n] for n in group], "gather_start_" + name)

    def gather_finish(group, start, after, name):
        landed = _gather_wait(start[0], start[1], start[2], after, "gather_wait_" + name)
        return dict(zip(group, _forward_halves(landed, "forward_" + name)))

    st_in = gather_start(("w_in",), "in")
    u = _rms_u(head, xs, small["g_mix"], nb)
    gw = gather_finish(("w_in",), st_in, u, "in")
    st_mix = gather_start(GROUP_MIX, "mix")
    z = _in_proj(u, gw["w_in"], st_mix[3], nb)
    gw.update(gather_finish(GROUP_MIX, st_mix, z, "mix"))
    st_ffn = gather_start(GROUP_FFN, "ffn")
    w_pool_b = gw["w_pool"].reshape(N_SHARD, 4, 64, GD).transpose(1, 0, 2, 3).reshape(4, GD, GD)
    w_co_b, w_po_b, w_o_b = (gw[n].reshape(D, D) for n in ("w_conv_out", "w_pool_out", "w_o"))
    h1, yc, yp, mg, ca, cpre, m, mw, m2b = _mixers_fwd(
        z, head, xs, small["b_gate"] + st_ffn[3][0, 0], small["w_dw"], small["b_dw"], small["ln_g"], small["ln_b"],
        small["pool_scale"], w_co_b, w_pool_b, w_po_b, w_o_b, nb, t_total)
    gw.update(gather_finish(GROUP_FFN, st_ffn, h1, "ffn"))

    dh1, dh1b, vb, fb, dgb, dub, dh2b, loss, dg_ffn, dg_final = _ffn_fwd_bwd(
        h1, target, small["g_ffn"], small["g_final"], gw["w_ffn_gate"].reshape(D_FF, D),
        gw["w_ffn_up"].reshape(D_FF, D), gw["w_ffn_down"].reshape(D_FF, D), nb)
    loss = lax.psum(loss[0, 0], ("x", "y", "c"))

    def slabs(name, g):
        if name == "w_in":
            return g
        if name == "w_pool":
            return g.reshape(4, N_SHARD, 64, GD).transpose(1, 0, 2, 3).reshape(N_SHARD, 4 * 64, GD)
        return g.reshape(N_SHARD, g.shape[0] // N_SHARD, g.shape[1])

    def reduce_start(group, grads, name):
        g32 = [slabs(n, grads[n][0]) for n in group]
        g16 = [slabs(n, grads[n][1]) for n in group]
        from_sibling = _swap_halves_bf16(g16, "swap_halves_" + name)
        halves = [_add_sibling_half(g, sb, core, "add_sibling_" + n) for n, g, sb in zip(group, g32, from_sibling)]
        return [h for h, _ in halves], _scatter_start([hb for _, hb in halves], "scatter_start_" + name)

    def reduce_finish(group, halves, start, after, name):
        from_chips = _scatter_wait(start[0], start[1], start[2], after, "scatter_wait_" + name)
        return [_add_chip_slabs(h, rb, idx, "add_chips_" + n) for n, h, rb in zip(group, halves, from_chips)]

    half_ff = D_FF // 2
    grads_ffn = {
        "w_ffn_gate": _wgrad(dgb, vb, half_ff, D, tk, "wgrad_ffn_gate"),
        "w_ffn_up": _wgrad(dub, vb, half_ff, D, tk, "wgrad_ffn_up"),
        "w_ffn_down": _wgrad(fb, dh2b, half_ff, D, tk, "wgrad_ffn_down"),
    }
    halves_ffn, sc_ffn = reduce_start(GROUP_FFN, grads_ffn, "ffn")

    dycb, dypb, dzg, dconv, dmwb, dm, db_gate, dln_g, dln_b, db_dw, dps = _mixers_bwd_rows(
        dh1b, yc, yp, z, small["b_gate"], cpre, small["ln_g"], small["ln_b"], mw, small["pool_scale"],
        w_o_b, w_co_b, w_po_b, w_pool_b, sc_ffn[3], nb)
    grads_mix = {
        "w_conv_out": _wgrad(ca, dycb, D, D, tk, "wgrad_conv_out"),
        "w_pool": _wgrad(m, dmwb, GD, GD, tk, "wgrad_pool", diag=True),
        "w_pool_out": _wgrad(m2b, dypb, D, D, tk, "wgrad_pool_out"),
        "w_o": _wgrad(mg, dh1b, D, D, tk, "wgrad_o"),
    }
    halves_mix, sc_mix = reduce_start(GROUP_MIX, grads_mix, "mix")
    dzb, grad_x, dhead, dw_dw, dg_mix = _mixers_bwd_halo(
        dconv, dm, z, dzg, small["w_dw"], head, xs, small["g_mix"], dh1, gw["w_in"], sc_mix[3], nb, t_total)
    grads_in = {"w_in": _wgrad(u, dzb, D, D_IN // N_SHARD, tk, "wgrad_in", col_major=True)}
    halves_in, sc_in = reduce_start(("w_in",), grads_in, "in")

    packed = jnp.concatenate(
        [dg_mix, db_gate.reshape(2, D), db_dw, dln_g, dln_b, dps, dg_ffn, dg_final, jnp.zeros((7, D), F32),
         dhead[PAD:], dw_dw], axis=0)
    summed = _allreduce_small(packed)

    first = GROUP_FFN + GROUP_MIX
    reduced_half = reduce_finish(GROUP_FFN, halves_ffn, sc_ffn, summed, "ffn")
    reduced_half += reduce_finish(GROUP_MIX, halves_mix, sc_mix, summed, "mix")
    reduced = dict(zip(first, _join_halves(reduced_half, "join_halves_first")))
    updates = {n: _adamw(reduced[n], _shard2d(n, w[n]), _shard2d(n, mom[n]), _shard2d(n, var[n]), "adamw_" + n)
               for n in first}
    last_half = reduce_finish(("w_in",), halves_in, sc_in, updates["w_o"][1], "in")
    reduced["w_in"] = _join_halves(last_half, "join_halves_in")[0]
    updates["w_in"] = _adamw(reduced["w_in"], w["w_in"][0], mom["w_in"][0], var["w_in"][0], "adamw_w_in")

    def repl_stack(d):
        return jnp.concatenate([d["g_mix"], d["b_gate"].reshape(2, D), d["b_dw"], d["ln_g"], d["ln_b"],
                                d["pool_scale"], d["g_ffn"], d["g_final"].reshape(1, D), jnp.ones((7, D), F32)], axis=0)

    def shard_stack(d):
        return jnp.concatenate([d["meta_tokens"], d["w_dw"][0], jnp.ones((1, GD), F32)], axis=0)

    g_repl = summed[0:16]
    g_shard = lax.dynamic_slice_in_dim(summed[16:64], chip * GD, GD, axis=1)
    g_repl, d_repl, m_repl, v_repl = _adamw(g_repl, repl_stack(w), repl_stack(mom), repl_stack(var), "adamw_repl")
    g_shard, d_shard, m_shard, v_shard = _adamw(g_shard, shard_stack(w), shard_stack(mom), shard_stack(var), "adamw_cols")

    def unpack(name, repl, shard):
        if name == "meta_tokens":
            return shard[0:N_META]
        if name == "w_dw":
            return shard[N_META:N_META + KW].reshape(1, KW, GD)
        row = {"g_mix": 0, "b_gate": 1, "b_dw": 3, "ln_g": 4, "ln_b": 5, "pool_scale": 6, "g_ffn": 7, "g_final": 8}[name]
        if name == "b_gate":
            return repl[1:3].reshape(1, 2 * D)
        if name == "g_final":
            return repl[8]
        return repl[row:row + 1]

    out_g, out_d, out_m, out_v = {}, {}, {}, {}
    for n in WEIGHT_ORDER:
        if n in BIG:
            g, d_, m_, v_ = updates[n]
            shape = w[n].shape
            out_g[n], out_d[n], out_m[n], out_v[n] = (_unshard2d(n, a, shape) for a in (g, d_, m_, v_))
        else:
            out_g[n] = unpack(n, g_repl, g_shard)
            out_d[n] = unpack(n, d_repl, d_shard)
            out_m[n] = unpack(n, m_repl, m_shard)
            out_v[n] = unpack(n, v_repl, v_shard)
    return (loss, grad_x[None], *[out_g[n] for n in WEIGHT_ORDER], *[out_d[n] for n in WEIGHT_ORDER],
            *[out_m[n] for n in WEIGHT_ORDER], *[out_v[n] for n in WEIGHT_ORDER])
```

```python
import functools

import jax
import jax.numpy as jnp
from jax import lax
from jax.experimental import pallas as pl
from jax.experimental.pallas import tpu as pltpu

F32 = jnp.float32
BF16 = jnp.bfloat16
MESH = pl.DeviceIdType.MESH

D = 1024
N_META = 16
KW = 31
CPAD = KW // 2
POOL_WINDOWS = (2, 4, 8, 16)
GD = 256
D_IN = 5 * D
D_FF = 2816
N_SHARD = 4
BR = 256
HALO = 16
PAD = BR - N_META
EXT = BR + 2 * HALO
RMS_EPS = 1e-6
LN_EPS = 1e-5
LR, B1, B2, ADAM_EPS, WD, STEP = 0.001, 0.9, 0.999, 1e-08, 0.01, 10
VMEM_LIMIT = 56 * 1024 * 1024


def _cparams(sem, vmem=VMEM_LIMIT):
    return pltpu.CompilerParams(dimension_semantics=sem, vmem_limit_bytes=vmem)


def _dot(a, b):
    return jnp.dot(a, b, preferred_element_type=F32)


def _dot_nt(a, b):
    return lax.dot_general(a, b, (((1,), (1,)), ((), ())), preferred_element_type=F32)


def _dot_tn(a, b):
    return lax.dot_general(a, b, (((0,), (0,)), ((), ())), preferred_element_type=F32)


def _sigmoid(x):
    return 1.0 / (1.0 + jnp.exp(-x))


def _row_ids(i, n, offset=0):
    return lax.broadcasted_iota(jnp.int32, (n, 1), 0) + (i * BR + offset - PAD)


def _pool_cnt(t, w, t_total):
    left = w // 2
    right = w - 1 - left
    lo = jnp.clip(t - left, 0, t_total)
    hi = jnp.clip(t + right + 1, 0, t_total)
    return jnp.maximum(hi - lo, 1).astype(F32)


def _halo_specs(col, nb):
    last = nb * (BR // HALO) - 1
    return [
        pl.BlockSpec((HALO, D), lambda i: (jnp.maximum(i * (BR // HALO) - 1, 0), col)),
        pl.BlockSpec((BR, D), lambda i: (i, col)),
        pl.BlockSpec((HALO, D), lambda i: (jnp.minimum((i + 1) * (BR // HALO), last), col)),
    ]


def _fill_ext(ext_ref, prev, cur, nxt, i, nb):
    ext_ref[0:HALO, :] = jnp.where(i > 0, prev, 0.0)
    ext_ref[HALO:HALO + BR, :] = cur
    ext_ref[HALO + BR:EXT, :] = jnp.where(i < nb - 1, nxt, 0.0)


ROT_ROWS = EXT - 8


def _fill_rot(rot_ref, ext_ref, lanes):
    for r in range(1, 8):
        rot_ref[r] = ext_ref[pl.ds(r, ROT_ROWS), lanes]


def _tap(rot_ref, ext_ref, lanes, offset):
    q, r = divmod(offset, 8)
    if r == 0:
        return ext_ref[pl.ds(8 * q, BR), lanes]
    return rot_ref[r, pl.ds(8 * q, BR), :]


def _row_spec(width=D):
    return pl.BlockSpec((BR, width), lambda i: (i, 0))


def _x_spec():
    return pl.BlockSpec((BR, D), lambda i: (jnp.maximum(i - 1, 0), 0))


def _const_spec(shape):
    nd = len(shape)
    return pl.BlockSpec(shape, lambda i: (0,) * nd)


def _rms_u(head, x, g_mix, nb):
    def body(head_ref, x_ref, g_ref, u_ref):
        i = pl.program_id(0)
        h = jnp.where(i == 0, head_ref[...], x_ref[...])
        r = lax.rsqrt(jnp.mean(h * h, axis=-1, keepdims=True) + RMS_EPS)
        u_ref[...] = ((h * r) * g_ref[...]).astype(BF16)

    return pl.pallas_call(
        body, name="rms_u",
        grid=(nb,),
        in_specs=[_const_spec((BR, D)), _x_spec(), _const_spec((1, D))],
        out_specs=_row_spec(),
        out_shape=jax.ShapeDtypeStruct((nb * BR, D), BF16),
        compiler_params=_cparams(("arbitrary",)),
    )(head, x, g_mix)


def _in_proj(u, w_in_b, dep, nb):
    tp = nb * BR
    ns = w_in_b.shape[0]
    wcols = w_in_b.shape[2]

    def body(u_ref, w_ref, dep_ref, z_ref):
        z_ref[...] = _dot(u_ref[...], w_ref[...])

    return pl.pallas_call(
        body, name="in_proj",
        grid=(ns, nb),
        in_specs=[
            pl.BlockSpec((BR, D), lambda s, i: (i, 0)),
            pl.BlockSpec((None, D, wcols), lambda s, i: (s, 0, 0)),
            pl.BlockSpec(memory_space=pl.ANY),
        ],
        out_specs=pl.BlockSpec((BR, wcols), lambda s, i: (i, s)),
        out_shape=jax.ShapeDtypeStruct((tp, ns * wcols), F32),
        compiler_params=_cparams(("arbitrary", "arbitrary")),
    )(u, w_in_b, dep)


def _mixers_fwd(z, head, x, b_gate, w_dw, b_dw, ln_g, ln_b, pool_scale, w_co, w_pool, w_po, w_o, nb, t_total):
    tp = nb * BR

    def body(avp, av, avn, agp, ag, agn, pp, pc, pn, za, zb, head_ref, x_ref, bg_ref, wdw_ref, bdw_ref,
             lng_ref, lnb_ref, ps_ref, wco_ref, wpool_ref, wpo_ref, wo_ref,
             h1_ref, yc_ref, yp_ref, mg_ref, ca_ref, cpre_ref, m_ref, mw_ref, m2b_ref, ext_ref, pext_ref, rot_ref):
        i = pl.program_id(0)
        _fill_ext(ext_ref, avp[...] * _sigmoid(agp[...]), av[...] * _sigmoid(ag[...]),
                  avn[...] * _sigmoid(agn[...]), i, nb)
        _fill_ext(pext_ref, pp[...], pc[...], pn[...], i, nb)

        def conv_chunk(c, carry):
            lanes = pl.ds(pl.multiple_of(c * 128, 128), 128)
            _fill_rot(rot_ref, ext_ref, lanes)
            acc = jnp.broadcast_to(bdw_ref[:, lanes], (BR, 128))
            for k in range(KW):
                acc = acc + wdw_ref[k:k + 1, lanes] * _tap(rot_ref, ext_ref, lanes, 1 + k)
            cpre_ref[:, lanes] = acc
            return carry
        lax.fori_loop(0, D // 128, conv_chunk, 0)

        conv = cpre_ref[...]
        mu = jnp.mean(conv, axis=-1, keepdims=True)
        xc = conv - mu
        rstd = lax.rsqrt(jnp.mean(xc * xc, axis=-1, keepdims=True) + LN_EPS)
        ln = (xc * rstd) * lng_ref[...] + lnb_ref[...]
        cact = (ln * _sigmoid(ln)).astype(BF16)
        ca_ref[...] = cact
        y_conv = _dot(cact, wco_ref[...])
        yc_ref[...] = y_conv

        t = _row_ids(i, BR)
        for gi, w in enumerate(POOL_WINDOWS):
            left = w // 2
            right = w - 1 - left
            lanes = slice(gi * GD, (gi + 1) * GD)
            s = pext_ref[pl.ds(HALO - left, BR), lanes]
            for j in range(-left + 1, right + 1):
                s = s + pext_ref[pl.ds(HALO + j, BR), lanes]
            m = (s / _pool_cnt(t, w, t_total) - pext_ref[HALO:HALO + BR, lanes]).astype(BF16)
            m_ref[:, lanes] = m
            mw_ref[:, lanes] = _dot(m, wpool_ref[gi])
        mw = mw_ref[...]
        m2b = (mw * ps_ref[...]).astype(BF16)
        m2b_ref[...] = m2b
        y_pool = _dot(m2b, wpo_ref[...])
        yp_ref[...] = y_pool

        s_a = _sigmoid(za[...] + bg_ref[:, 0:D])
        s_b = _sigmoid(zb[...] + bg_ref[:, D:2 * D])
        merged = (s_a * y_conv + s_b * y_pool).astype(BF16)
        mg_ref[...] = merged
        h0 = jnp.where(i == 0, head_ref[...], x_ref[...])
        h1_ref[...] = h0 + _dot(merged, wo_ref[...])

    in_specs = (_halo_specs(0, nb) + _halo_specs(1, nb) + _halo_specs(2, nb)
                + [pl.BlockSpec((BR, D), lambda i: (i, 3)), pl.BlockSpec((BR, D), lambda i: (i, 4)),
                   _const_spec((BR, D)), _x_spec(), _const_spec((1, 2 * D)), _const_spec((32, D)),
                   _const_spec((1, D)), _const_spec((1, D)), _const_spec((1, D)), _const_spec((1, D)),
                   _const_spec((D, D)), _const_spec((4, GD, GD)), _const_spec((D, D)), _const_spec((D, D))])
    outs = [(F32, "h1"), (F32, "yc"), (F32, "yp"), (BF16, "mg"), (BF16, "ca"), (F32, "cpre"), (BF16, "m"), (F32, "mw"),
            (BF16, "m2b")]
    return pl.pallas_call(
        body, name="mixers_fwd",
        grid=(nb,),
        in_specs=in_specs,
        out_specs=[_row_spec() for _ in outs],
        out_shape=[jax.ShapeDtypeStruct((tp, D), dt) for dt, _ in outs],
        scratch_shapes=[pltpu.VMEM((EXT, D), F32), pltpu.VMEM((EXT, D), F32), pltpu.VMEM((8, ROT_ROWS, 128), F32)],
        compiler_params=_cparams(("arbitrary",)),
    )(z, z, z, z, z, z, z, z, z, z, z, head, x, b_gate, w_dw, b_dw, ln_g, ln_b, pool_scale, w_co, w_pool, w_po, w_o)


def _ffn_fwd_bwd(h1, target, g_ffn, g_final, w_g, w_u, w_d, nb):
    tp = nb * BR

    def body(h1_ref, tgt_ref, gf_ref, gfin_ref, wg_hbm, wu_hbm, wd_hbm,
             dh1_ref, dh1b_ref, vb_ref, fb_ref, dgb_ref, dub_ref, dh2b_ref, loss_ref, dgf_ref, dgfin_ref,
             wg_ref, wu_ref, wd_ref, sem):
        i = pl.program_id(0)

        @pl.when(i == 0)
        def _():
            copies = [pltpu.make_async_copy(wg_hbm, wg_ref, sem.at[0]),
                      pltpu.make_async_copy(wu_hbm, wu_ref, sem.at[1]),
                      pltpu.make_async_copy(wd_hbm, wd_ref, sem.at[2])]
            for cp in copies:
                cp.start()
            loss_ref[...] = jnp.zeros_like(loss_ref)
            dgf_ref[...] = jnp.zeros_like(dgf_ref)
            dgfin_ref[...] = jnp.zeros_like(dgfin_ref)
            for cp in copies:
                cp.wait()

        h1 = h1_ref[...]
        r1 = lax.rsqrt(jnp.mean(h1 * h1, axis=-1, keepdims=True) + RMS_EPS)
        vn = h1 * r1
        vb = (vn * gf_ref[...]).astype(BF16)
        vb_ref[...] = vb
        g = _dot_nt(vb, wg_ref[...])
        up = _dot_nt(vb, wu_ref[...])
        sg = _sigmoid(g)
        sl = g * sg
        fb = (sl * up).astype(BF16)
        fb_ref[...] = fb
        h2 = h1 + _dot(fb, wd_ref[...])
        r2 = lax.rsqrt(jnp.mean(h2 * h2, axis=-1, keepdims=True) + RMS_EPS)
        yn = h2 * r2
        valid = i > 0
        diff = jnp.where(valid, yn * gfin_ref[...] - tgt_ref[...], 0.0)
        loss_ref[...] += 0.5 * jnp.sum(jnp.mean(diff * diff, axis=-1, keepdims=True))
        dy = diff * (1.0 / D)
        dgfin_ref[...] += jnp.sum(dy * yn, axis=0, keepdims=True)
        gd = dy * gfin_ref[...]
        dh2 = r2 * (gd - yn * jnp.mean(yn * gd, axis=-1, keepdims=True))
        dh2b = dh2.astype(BF16)
        dh2b_ref[...] = dh2b
        df = _dot_nt(dh2b, wd_ref[...])
        dub = (df * sl).astype(BF16)
        dgb = (df * up * (sg * (1.0 + g * (1.0 - sg)))).astype(BF16)
        dub_ref[...] = dub
        dgb_ref[...] = dgb
        dv = _dot(dgb, wg_ref[...]) + _dot(dub, wu_ref[...])
        dgf_ref[...] += jnp.sum(dv * vn, axis=0, keepdims=True)
        gd1 = dv * gf_ref[...]
        dh1 = dh2 + r1 * (gd1 - vn * jnp.mean(vn * gd1, axis=-1, keepdims=True))
        dh1_ref[...] = dh1
        dh1b_ref[...] = dh1.astype(BF16)

    any_spec = pl.BlockSpec(memory_space=pl.ANY)
    return pl.pallas_call(
        body, name="ffn_fwd_bwd",
        grid=(nb,),
        in_specs=[_row_spec(), _x_spec(), _const_spec((1, D)), _const_spec((1, D)), any_spec, any_spec, any_spec],
        out_specs=[_row_spec(), _row_spec(), _row_spec(), _row_spec(D_FF), _row_spec(D_FF), _row_spec(D_FF), _row_spec(),
                   _const_spec((1, 1)), _const_spec((1, D)), _const_spec((1, D))],
        out_shape=[jax.ShapeDtypeStruct((tp, D), F32), jax.ShapeDtypeStruct((tp, D), BF16),
                   jax.ShapeDtypeStruct((tp, D), BF16), jax.ShapeDtypeStruct((tp, D_FF), BF16),
                   jax.ShapeDtypeStruct((tp, D_FF), BF16), jax.ShapeDtypeStruct((tp, D_FF), BF16),
                   jax.ShapeDtypeStruct((tp, D), BF16), jax.ShapeDtypeStruct((1, 1), F32),
                   jax.ShapeDtypeStruct((1, D), F32), jax.ShapeDtypeStruct((1, D), F32)],
        scratch_shapes=[pltpu.VMEM((D_FF, D), BF16), pltpu.VMEM((D_FF, D), BF16), pltpu.VMEM((D_FF, D), BF16),
                        pltpu.SemaphoreType.DMA((3,))],
        compiler_params=_cparams(("arbitrary",)),
    )(h1, target, g_ffn, g_final, w_g, w_u, w_d)


def _mixers_bwd_rows(dh1b, yc, yp, z, b_gate, cpre, ln_g, ln_b, mw, pool_scale, w_o, w_co, w_po, w_pool, dep, nb):
    tp = nb * BR

    def body(dh1b_ref, yc_ref, yp_ref, za, zb, bg_ref, cpre_ref, lng_ref, lnb_ref, mw_ref, ps_ref,
             wo_ref, wco_ref, wpo_ref, wpool_ref, dep_ref,
             dycb_ref, dypb_ref, dzg_ref, dconv_ref, dmwb_ref, dm_ref, dbg_ref, dlng_ref, dlnb_ref, dbdw_ref, dps_ref):
        i = pl.program_id(0)

        @pl.when(i == 0)
        def _():
            for r in (dbg_ref, dlng_ref, dlnb_ref, dbdw_ref, dps_ref):
                r[...] = jnp.zeros_like(r)

        dmg = _dot_nt(dh1b_ref[...], wo_ref[...])
        s_a = _sigmoid(za[...] + bg_ref[:, 0:D])
        s_b = _sigmoid(zb[...] + bg_ref[:, D:2 * D])
        dycb = (dmg * s_a).astype(BF16)
        dypb = (dmg * s_b).astype(BF16)
        dycb_ref[...] = dycb
        dypb_ref[...] = dypb
        dza = dmg * yc_ref[...] * (s_a * (1.0 - s_a))
        dzb = dmg * yp_ref[...] * (s_b * (1.0 - s_b))
        dzg_ref[:, 0:D] = dza.astype(BF16)
        dzg_ref[:, D:2 * D] = dzb.astype(BF16)
        dbg_ref[:, 0:D] += jnp.sum(dza, axis=0, keepdims=True)
        dbg_ref[:, D:2 * D] += jnp.sum(dzb, axis=0, keepdims=True)

        dca = _dot_nt(dycb, wco_ref[...])
        conv = cpre_ref[...]
        mu = jnp.mean(conv, axis=-1, keepdims=True)
        xc = conv - mu
        rstd = lax.rsqrt(jnp.mean(xc * xc, axis=-1, keepdims=True) + LN_EPS)
        xhat = xc * rstd
        ln = xhat * lng_ref[...] + lnb_ref[...]
        sg = _sigmoid(ln)
        dln = dca * (sg * (1.0 + ln * (1.0 - sg)))
        dlng_ref[...] += jnp.sum(dln * xhat, axis=0, keepdims=True)
        dlnb_ref[...] += jnp.sum(dln, axis=0, keepdims=True)
        dxh = dln * lng_ref[...]
        dconv = rstd * (dxh - jnp.mean(dxh, axis=-1, keepdims=True)
                        - xhat * jnp.mean(dxh * xhat, axis=-1, keepdims=True))
        dconv_ref[...] = dconv
        dbdw_ref[...] += jnp.sum(dconv, axis=0, keepdims=True)

        dm2 = _dot_nt(dypb, wpo_ref[...])
        dps_ref[...] += jnp.sum(dm2 * mw_ref[...], axis=0, keepdims=True)
        dmwb = (dm2 * ps_ref[...]).astype(BF16)
        dmwb_ref[...] = dmwb
        for gi in range(len(POOL_WINDOWS)):
            lanes = slice(gi * GD, (gi + 1) * GD)
            dm_ref[:, lanes] = _dot_nt(dmwb[:, lanes], wpool_ref[gi])

    in_specs = [_row_spec(), _row_spec(), _row_spec(),
                pl.BlockSpec((BR, D), lambda i: (i, 3)), pl.BlockSpec((BR, D), lambda i: (i, 4)),
                _const_spec((1, 2 * D)), _row_spec(), _const_spec((1, D)), _const_spec((1, D)), _row_spec(),
                _const_spec((1, D)), _const_spec((D, D)), _const_spec((D, D)), _const_spec((D, D)),
                _const_spec((4, GD, GD)), pl.BlockSpec(memory_space=pl.ANY)]
    return pl.pallas_call(
        body, name="mixers_bwd_rows",
        grid=(nb,),
        in_specs=in_specs,
        out_specs=[_row_spec(), _row_spec(), _row_spec(2 * D), _row_spec(), _row_spec(), _row_spec(),
                   _const_spec((1, 2 * D)), _const_spec((1, D)), _const_spec((1, D)), _const_spec((1, D)),
                   _const_spec((1, D))],
        out_shape=[jax.ShapeDtypeStruct((tp, D), BF16), jax.ShapeDtypeStruct((tp, D), BF16),
                   jax.ShapeDtypeStruct((tp, 2 * D), BF16), jax.ShapeDtypeStruct((tp, D), F32),
                   jax.ShapeDtypeStruct((tp, D), BF16), jax.ShapeDtypeStruct((tp, D), F32),
                   jax.ShapeDtypeStruct((1, 2 * D), F32), jax.ShapeDtypeStruct((1, D), F32),
                   jax.ShapeDtypeStruct((1, D), F32), jax.ShapeDtypeStruct((1, D), F32),
                   jax.ShapeDtypeStruct((1, D), F32)],
        compiler_params=_cparams(("arbitrary",)),
    )(dh1b, yc, yp, z, z, b_gate, cpre, ln_g, ln_b, mw, pool_scale, w_o, w_co, w_po, w_pool, dep)


def _mixers_bwd_halo(dconv, dm, z, dzg, w_dw, head, x, g_mix, dh1, w_in_b, dep, nb, t_total):
    tp = nb * BR
    ns = w_in_b.shape[0]
    wcols = w_in_b.shape[2]
    seq = x.shape[0]

    def body(dcp, dcc, dcn, dmp, dmc, dmn, avp, av, avn, agp, ag, agn, dzg_ref, wdw_ref, head_ref, x_ref, g_ref,
             dh1_ref, w_hbm, dep_ref,
             dzb_ref, gx_ref, dhead_ref, dwdw_ref, dgmix_ref,
             w_ref, sem, aext_ref, dext_ref, qext_ref, da_ref, rot_ref, dwp_ref):
        i = pl.program_id(0)

        @pl.when(i == 0)
        def _():
            cp = pltpu.make_async_copy(w_hbm, w_ref, sem.at[0])
            cp.start()
            dwp_ref[...] = jnp.zeros_like(dwp_ref)
            dgmix_ref[...] = jnp.zeros_like(dgmix_ref)
            cp.wait()

        sig_g = _sigmoid(ag[...])
        _fill_ext(aext_ref, avp[...] * _sigmoid(agp[...]), av[...] * sig_g, avn[...] * _sigmoid(agn[...]), i, nb)
        _fill_ext(dext_ref, dcp[...], dcc[...], dcn[...], i, nb)
        _fill_ext(qext_ref, dmp[...], dmc[...], dmn[...], i, nb)

        def conv_chunk(c, carry):
            lanes = pl.ds(pl.multiple_of(c * 128, 128), 128)
            _fill_rot(rot_ref, dext_ref, lanes)
            acc = jnp.zeros((BR, 128), F32)
            for k in range(KW):
                acc = acc + wdw_ref[k:k + 1, lanes] * _tap(rot_ref, dext_ref, lanes, KW - k)
            da_ref[:, lanes] = acc
            _fill_rot(rot_ref, aext_ref, lanes)
            dcv = dext_ref[HALO:HALO + BR, lanes]
            for k in range(KW):
                prod = _tap(rot_ref, aext_ref, lanes, 1 + k) * dcv
                dwp_ref[k, :, lanes] += jnp.sum(prod.reshape(BR // 8, 8, 128), axis=0)
            return carry
        lax.fori_loop(0, D // 128, conv_chunk, 0)

        @pl.when(i == nb - 1)
        def _():
            dwdw_ref[...] = jnp.sum(dwp_ref[...], axis=1)

        da = da_ref[...]
        a_val = av[...]
        dzb_ref[:, 0:D] = (da * sig_g).astype(BF16)
        dzb_ref[:, D:2 * D] = (da * a_val * (sig_g * (1.0 - sig_g))).astype(BF16)

        t_ext = _row_ids(i, EXT, -HALO)
        for gi, w in enumerate(POOL_WINDOWS):
            left = w // 2
            right = w - 1 - left
            lanes = slice(gi * GD, (gi + 1) * GD)
            qext_ref[:, lanes] = qext_ref[:, lanes] / _pool_cnt(t_ext, w, t_total)
            s = qext_ref[pl.ds(HALO - right, BR), lanes]
            for j in range(-right + 1, left + 1):
                s = s + qext_ref[pl.ds(HALO + j, BR), lanes]
            dzb_ref[:, 2 * D + gi * GD:2 * D + (gi + 1) * GD] = (s - dmc[:, lanes]).astype(BF16)
        dzb_ref[:, 3 * D:5 * D] = dzg_ref[...]

        du = _dot_nt(dzb_ref[:, 0:wcols], w_ref[0])
        for s_i in range(1, ns):
            du = du + _dot_nt(dzb_ref[:, s_i * wcols:(s_i + 1) * wcols], w_ref[s_i])
        h0 = jnp.where(i == 0, head_ref[...], x_ref[...])
        r0 = lax.rsqrt(jnp.mean(h0 * h0, axis=-1, keepdims=True) + RMS_EPS)
        un = h0 * r0
        dgmix_ref[...] += jnp.sum(du * un, axis=0, keepdims=True)
        gd = du * g_ref[...]
        dh0 = dh1_ref[...] + r0 * (gd - un * jnp.mean(un * gd, axis=-1, keepdims=True))
        gx_ref[...] = dh0

        @pl.when(i == 0)
        def _():
            dhead_ref[...] = dh0

    any_spec = pl.BlockSpec(memory_space=pl.ANY)
    in_specs = (_halo_specs(0, nb) + _halo_specs(0, nb) + _halo_specs(0, nb) + _halo_specs(1, nb)
                + [_row_spec(2 * D), _const_spec((32, D)), _const_spec((BR, D)), _x_spec(), _const_spec((1, D)),
                   _row_spec(), any_spec, any_spec])
    return pl.pallas_call(
        body, name="mixers_bwd_halo",
        grid=(nb,),
        in_specs=in_specs,
        out_specs=[_row_spec(D_IN), _x_spec(), _const_spec((BR, D)), _const_spec((32, D)), _const_spec((1, D))],
        out_shape=[jax.ShapeDtypeStruct((tp, D_IN), BF16), jax.ShapeDtypeStruct((seq, D), F32),
                   jax.ShapeDtypeStruct((BR, D), F32), jax.ShapeDtypeStruct((32, D), F32),
                   jax.ShapeDtypeStruct((1, D), F32)],
        scratch_shapes=[pltpu.VMEM((ns, D, wcols), BF16), pltpu.SemaphoreType.DMA((1,)),
                        pltpu.VMEM((EXT, D), F32), pltpu.VMEM((EXT, D), F32), pltpu.VMEM((EXT, D), F32),
                        pltpu.VMEM((BR, D), F32), pltpu.VMEM((8, ROT_ROWS, 128), F32), pltpu.VMEM((32, 8, D), F32)],
        compiler_params=_cparams(("arbitrary",)),
    )(dconv, dconv, dconv, dm, dm, dm, z, z, z, z, z, z, dzg, w_dw, head, x, g_mix, dh1, w_in_b, dep)


def _wgrad(a, c, tm, tn, tk, name, diag=False, col_major=False):
    tp, m = a.shape
    n = c.shape[1]
    nk = tp // tk
    gm, gn = m // tm, n // tn

    def body(a_ref, c_ref, o_ref, ob_ref):
        k = pl.program_id(2)

        @pl.when(k == 0)
        def _():
            o_ref[...] = jnp.zeros_like(o_ref)

        o_ref[...] += _dot_tn(a_ref[...], c_ref[...])

        @pl.when(k == nk - 1)
        def _():
            ob_ref[...] = o_ref[...].astype(BF16)

    c_map = lambda i, j, k: (k, j)
    grid = (gm, gn, nk)
    if diag:
        grid = (gm, 1, nk)
        c_map = lambda i, j, k: (k, i)
        o_spec = pl.BlockSpec((tm, tn), lambda i, j, k: (i, 0))
        o_shape = (m, tn)
    elif col_major:
        o_spec = pl.BlockSpec((None, tm, tn), lambda i, j, k: (j, i, 0))
        o_shape = (gn, m, tn)
    else:
        o_spec = pl.BlockSpec((tm, tn), lambda i, j, k: (i, j))
        o_shape = (m, n)
    return pl.pallas_call(
        body, name=name,
        grid=grid,
        in_specs=[pl.BlockSpec((tk, tm), lambda i, j, k: (k, i)), pl.BlockSpec((tk, tn), c_map)],
        out_specs=[o_spec, o_spec],
        out_shape=[jax.ShapeDtypeStruct(o_shape, F32), jax.ShapeDtypeStruct(o_shape, BF16)],
        compiler_params=_cparams(("arbitrary", "arbitrary", "arbitrary")),
    )(a, c)


def _place():
    x, y, c = lax.axis_index("x"), lax.axis_index("y"), lax.axis_index("c")
    others = [(1 - x, y), (x, 1 - y), (1 - x, 1 - y)]
    return x, y, c, others


def _split2(a, axis=0):
    return a.reshape(a.shape[:axis] + (2, a.shape[axis] // 2) + a.shape[axis + 1:])


def _merge2(a, axis=0):
    return a.reshape(a.shape[:axis] + (2 * a.shape[axis + 1],) + a.shape[axis + 2:])


def _cast_into_slot(w2d, chip, dep, name):
    r, c = w2d.shape
    r2 = r // 2

    def body(chip_ref, w_ref, dep_ref, o_ref):
        o_ref[...] = w_ref[...].astype(BF16)

    return pl.pallas_call(
        body, name=name,
        grid_spec=pltpu.PrefetchScalarGridSpec(
            num_scalar_prefetch=1, grid=(2,),
            in_specs=[pl.BlockSpec((r2, c), lambda h, chip_ref: (h, 0)), pl.BlockSpec(memory_space=pl.ANY)],
            out_specs=pl.BlockSpec((None, None, r2, c), lambda h, chip_ref: (chip_ref[0], h, 0, 0))),
        out_shape=jax.ShapeDtypeStruct((N_SHARD, 2, r2, c), BF16),
        compiler_params=_cparams(("arbitrary",)),
    )(chip, w2d, dep)


HBM_SPEC = pl.BlockSpec(memory_space=pltpu.HBM)
SEM_SPEC = pl.BlockSpec(memory_space=pltpu.SEMAPHORE)
DATAFLOW = pltpu.SideEffectType.DATAFLOW_SIDE_EFFECTING
TOKEN = jax.ShapeDtypeStruct((8, 128), F32)


def _in_hbm(a):
    return pltpu.with_memory_space_constraint(a, pltpu.HBM)


def _gather_tiny(v):
    vm = pl.BlockSpec(memory_space=pltpu.VMEM)

    def body(v_ref, out_ref, send_sems, recv_sems):
        x, y, c, others = _place()
        mine = 2 * x + y
        sends = [pltpu.make_async_remote_copy(
            src_ref=v_ref, dst_ref=out_ref.at[mine], send_sem=send_sems.at[j], recv_sem=recv_sems.at[j],
            device_id=(*chip, c), device_id_type=MESH) for j, chip in enumerate(others)]
        for cp in sends:
            cp.start()
        out_ref[mine] = v_ref[...]
        for j, chip in enumerate(others):
            landed = out_ref.at[2 * chip[0] + chip[1]]
            pltpu.make_async_remote_copy(
                src_ref=landed, dst_ref=landed, send_sem=send_sems.at[j], recv_sem=recv_sems.at[j],
                device_id=(x, y, c), device_id_type=MESH).wait_recv()
        for cp in sends:
            cp.wait_send()

    return pl.pallas_call(
        body, name="gather_tiny",
        in_specs=[vm], out_specs=vm,
        out_shape=jax.ShapeDtypeStruct((N_SHARD,) + v.shape, v.dtype),
        scratch_shapes=[pltpu.SemaphoreType.DMA((3,)), pltpu.SemaphoreType.DMA((3,))],
    )(v)


def _ici_copies(srcs, dsts, send_sems, recv_sems, started):
    x, y, c, others = _place()
    mine = 2 * x + y
    copies = []
    for a in range(len(srcs)):
        for j, chip in enumerate(others):
            there = 2 * chip[0] + chip[1]
            src, dst = srcs[a](mine, there, c), dsts[a](mine, there, c)
            if not started:
                dst = dsts[a](there, mine, c)
            copies.append(pltpu.make_async_remote_copy(
                src_ref=src, dst_ref=dst, send_sem=send_sems.at[a * 3 + j], recv_sem=recv_sems.at[a * 3 + j],
                device_id=(*chip, c), device_id_type=MESH))
    return copies


def _split_start(srcs_of, dsts_of, arrays, n_src, name):
    n = len(arrays)

    def body(*refs):
        ins = refs[:n]
        send_sems, recv_sems = refs[n], refs[n + 1]
        token = refs[2 * n + 2]
        for cp in _ici_copies(srcs_of(ins), dsts_of(ins), send_sems, recv_sems, True):
            cp.start()
        token[...] = jnp.zeros_like(token)

    out = pl.pallas_call(
        body, name=name,
        in_specs=[HBM_SPEC] * n,
        out_specs=(SEM_SPEC, SEM_SPEC, *([HBM_SPEC] * n), pl.BlockSpec(memory_space=pltpu.VMEM)),
        out_shape=(pltpu.SemaphoreType.DMA((3 * n_src,)), pltpu.SemaphoreType.DMA((3 * n_src,)),
                   *[pltpu.HBM(a.shape, a.dtype) for a in arrays], TOKEN),
        input_output_aliases={a: 2 + a for a in range(n)},
        compiler_params=pltpu.CompilerParams(has_side_effects=DATAFLOW),
    )(*[_in_hbm(a) for a in arrays])
    return out[0], out[1], list(out[2:2 + n]), out[2 + n]


def _split_wait(srcs_of, dsts_of, send_sems, recv_sems, arrays, after, name):
    n = len(arrays)

    def body(*refs):
        ins = refs[:n]
        send_sems, recv_sems = refs[n], refs[n + 1]
        for cp in _ici_copies(srcs_of(ins), dsts_of(ins), send_sems, recv_sems, False):
            cp.wait_send()
            cp.wait_recv()

    return pl.pallas_call(
        body, name=name,
        in_specs=[HBM_SPEC] * n + [SEM_SPEC, SEM_SPEC] + [pl.BlockSpec(memory_space=pl.ANY)] * len(after),
        out_specs=[HBM_SPEC] * n,
        out_shape=[pltpu.HBM(a.shape, a.dtype) for a in arrays],
        input_output_aliases={a: a for a in range(n)},
        compiler_params=pltpu.CompilerParams(has_side_effects=DATAFLOW),
    )(*arrays, send_sems, recv_sems, *after)


def _gather_views(ins):
    view = [lambda frm, to, c, r=r: r.at[frm, c] for r in ins]
    return view


def _gather_start(bufs, name):
    return _split_start(_gather_views, _gather_views, bufs, len(bufs), name)


def _gather_wait(send_sems, recv_sems, bufs, after, name):
    return _split_wait(_gather_views, _gather_views, send_sems, recv_sems, bufs, after, name)


def _forward_halves(bufs, name):
    n = len(bufs)
    any_spec = pl.BlockSpec(memory_space=pl.ANY)

    def body(*refs):
        outs = refs[n:2 * n]
        send_sems, recv_sems = refs[2 * n:]
        x, y, c, others = _place()
        copies = []
        for a in range(n):
            for j, chip in enumerate(others):
                landed = outs[a].at[2 * chip[0] + chip[1], c]
                copies.append(pltpu.make_async_remote_copy(
                    src_ref=landed, dst_ref=landed, send_sem=send_sems.at[a * 3 + j], recv_sem=recv_sems.at[a * 3 + j],
                    device_id=(x, y, 1 - c), device_id_type=MESH))
        for cp in copies:
            cp.start()
        for a in range(n):
            for j, chip in enumerate(others):
                landed = outs[a].at[2 * chip[0] + chip[1], 1 - c]
                pltpu.make_async_remote_copy(
                    src_ref=landed, dst_ref=landed, send_sem=send_sems.at[a * 3 + j], recv_sem=recv_sems.at[a * 3 + j],
                    device_id=(x, y, c), device_id_type=MESH).wait_recv()
        for cp in copies:
            cp.wait_send()

    out = pl.pallas_call(
        body, name=name,
        in_specs=[any_spec] * n, out_specs=[any_spec] * n,
        out_shape=[jax.ShapeDtypeStruct(b.shape, b.dtype) for b in bufs],
        input_output_aliases={a: a for a in range(n)},
        scratch_shapes=[pltpu.SemaphoreType.DMA((3 * n,)), pltpu.SemaphoreType.DMA((3 * n,))],
    )(*bufs)
    return [_merge2(o, 1) for o in out]


def _swap_halves_bf16(gbs, name):
    n = len(gbs)
    any_spec = pl.BlockSpec(memory_space=pl.ANY)

    def body(*refs):
        ins, outs = refs[:n], refs[n:2 * n]
        send_sems, recv_sems = refs[2 * n:]
        x, y, c, _ = _place()
        copies = []
        for a in range(n):
            copies.append(pltpu.make_async_remote_copy(
                src_ref=ins[a].at[:, 1 - c], dst_ref=outs[a], send_sem=send_sems.at[a], recv_sem=recv_sems.at[a],
                device_id=(x, y, 1 - c), device_id_type=MESH))
        for cp in copies:
            cp.start()
        for cp in copies:
            cp.wait()

    return pl.pallas_call(
        body, name=name,
        in_specs=[any_spec] * n, out_specs=[any_spec] * n,
        out_shape=[jax.ShapeDtypeStruct((g.shape[0], g.shape[1] // 2, g.shape[2]), g.dtype) for g in gbs],
        scratch_shapes=[pltpu.SemaphoreType.DMA((n,)), pltpu.SemaphoreType.DMA((n,))],
    )(*[_split2(g, 1) for g in gbs])


def _scatter_srcs(n):
    return lambda ins: [lambda frm, to, c, r=r: r.at[to] for r in ins[:n]]


def _scatter_dsts(n):
    return lambda ins: [lambda frm, to, c, r=r: r.at[frm] for r in ins[n:]]


def _scatter_start(hbs, name):
    n = len(hbs)
    lands = [lax.empty(h.shape, h.dtype) for h in hbs]
    return _split_start(_scatter_srcs(n), _scatter_dsts(n), list(hbs) + lands, n, name)


def _scatter_wait(send_sems, recv_sems, arrays, after, name):
    n = len(arrays) // 2
    return _split_wait(_scatter_srcs(n), _scatter_dsts(n), send_sems, recv_sems, arrays, after, name)[n:]


def _join_halves(rhs, name):
    n = len(rhs)
    any_spec = pl.BlockSpec(memory_space=pl.ANY)

    def body(*refs):
        outs = refs[n:2 * n]
        send_sems, recv_sems = refs[2 * n:]
        x, y, c, _ = _place()
        copies = []
        for a in range(n):
            copies.append(pltpu.make_async_remote_copy(
                src_ref=outs[a].at[c], dst_ref=outs[a].at[c], send_sem=send_sems.at[a],
                recv_sem=recv_sems.at[a], device_id=(x, y, 1 - c), device_id_type=MESH))
        for cp in copies:
            cp.start()
        for a in range(n):
            landed = outs[a].at[1 - c]
            pltpu.make_async_remote_copy(
                src_ref=landed, dst_ref=landed, send_sem=send_sems.at[a], recv_sem=recv_sems.at[a],
                device_id=(x, y, c), device_id_type=MESH).wait_recv()
        for cp in copies:
            cp.wait_send()

    out = pl.pallas_call(
        body, name=name,
        in_specs=[any_spec] * n, out_specs=[any_spec] * n,
        out_shape=[jax.ShapeDtypeStruct(r.shape, r.dtype) for r in rhs],
        input_output_aliases={a: a for a in range(n)},
        scratch_shapes=[pltpu.SemaphoreType.DMA((n,)), pltpu.SemaphoreType.DMA((n,))],
    )(*rhs)
    return [_merge2(o) for o in out]


def _allreduce_small(v):
    rows, cols = v.shape
    vm = pl.BlockSpec(memory_space=pltpu.VMEM)
    flips = [(dx, dy, dc) for dx in (0, 1) for dy in (0, 1) for dc in (0, 1)][1:]

    def body(v_ref, out_ref, buf_ref, send_sems, recv_sems):
        x, y, c, _ = _place()
        mine = 4 * x + 2 * y + c
        copies = []
        for k, (dx, dy, dc) in enumerate(flips):
            px, py, pc = jnp.bitwise_xor(x, dx), jnp.bitwise_xor(y, dy), jnp.bitwise_xor(c, dc)
            copies.append(pltpu.make_async_remote_copy(
                src_ref=v_ref, dst_ref=buf_ref.at[mine], send_sem=send_sems.at[k], recv_sem=recv_sems.at[k],
                device_id=(px, py, pc), device_id_type=MESH))
        for cp in copies:
            cp.start()
        buf_ref[mine] = v_ref[...]
        for k, (dx, dy, dc) in enumerate(flips):
            src = 4 * jnp.bitwise_xor(x, dx) + 2 * jnp.bitwise_xor(y, dy) + jnp.bitwise_xor(c, dc)
            pltpu.make_async_remote_copy(
                src_ref=v_ref, dst_ref=buf_ref.at[src], send_sem=send_sems.at[k], recv_sem=recv_sems.at[k],
                device_id=(x, y, c), device_id_type=MESH).wait_recv()
        for cp in copies:
            cp.wait_send()
        acc = buf_ref[0]
        for d in range(1, 8):
            acc = acc + buf_ref[d]
        out_ref[...] = acc

    return pl.pallas_call(
        body, name="allreduce_small",
        in_specs=[vm], out_specs=vm,
        out_shape=jax.ShapeDtypeStruct((rows, cols), F32),
        scratch_shapes=[pltpu.VMEM((8, rows, cols), F32), pltpu.SemaphoreType.DMA((7,)), pltpu.SemaphoreType.DMA((7,))],
    )(v)


def _row_block(r):
    for cand in (512, 352, 256, 128, 64, 48, 16):
        if r % cand == 0:
            return cand
    return r


def _add_sibling_half(g, sb, core, name):
    ns, r, c = g.shape
    r2 = r // 2

    def body(core_ref, g_ref, sb_ref, h_ref, hb_ref):
        h = g_ref[...] + sb_ref[...].astype(F32)
        h_ref[...] = h
        hb_ref[...] = h.astype(BF16)

    spec = pl.BlockSpec((None, r2, c), lambda s, core_ref: (s, 0, 0))
    return pl.pallas_call(
        body, name=name,
        grid_spec=pltpu.PrefetchScalarGridSpec(
            num_scalar_prefetch=1, grid=(ns,),
            in_specs=[pl.BlockSpec((None, r2, c), lambda s, core_ref: (s, core_ref[0], 0)), spec],
            out_specs=[spec, spec]),
        out_shape=[jax.ShapeDtypeStruct((ns, r2, c), F32), jax.ShapeDtypeStruct((ns, r2, c), BF16)],
        compiler_params=_cparams(("arbitrary",)),
    )(core, g, sb)


def _add_chip_slabs(h, rb, idx, name):
    ns, r2, c = h.shape

    def body(idx_ref, h_ref, r0_ref, r1_ref, r2_ref, o_ref):
        o_ref[...] = ((h_ref[...] + r0_ref[...].astype(F32)) + r1_ref[...].astype(F32)) + r2_ref[...].astype(F32)

    def pick(k):
        return pl.BlockSpec((None, r2, c), lambda i, idx_ref: (idx_ref[k], 0, 0))

    return pl.pallas_call(
        body, name=name,
        grid_spec=pltpu.PrefetchScalarGridSpec(
            num_scalar_prefetch=1, grid=(1,),
            in_specs=[pick(0), pick(1), pick(2), pick(3)],
            out_specs=pl.BlockSpec((None, r2, c), lambda i, idx_ref: (idx_ref[4], 0, 0))),
        out_shape=jax.ShapeDtypeStruct((2, r2, c), F32),
        compiler_params=_cparams(("arbitrary",)),
    )(idx, h, rb, rb, rb)


def _adamw(g, w, m, v, name):
    r, c = g.shape
    br = _row_block(r)

    def body(g_ref, w_ref, m_ref, v_ref, go_ref, d_ref, nm_ref, nv_ref):
        gg = g_ref[...]
        go_ref[...] = gg
        nm = B1 * m_ref[...] + (1.0 - B1) * gg
        nv = B2 * v_ref[...] + (1.0 - B2) * jnp.square(gg)
        m_hat = nm / (1.0 - B1 ** STEP)
        v_hat = nv / (1.0 - B2 ** STEP)
        d_ref[...] = -LR * (m_hat / (jnp.sqrt(v_hat) + ADAM_EPS) + WD * w_ref[...])
        nm_ref[...] = nm
        nv_ref[...] = nv

    spec = pl.BlockSpec((br, c), lambda i: (i, 0))
    return pl.pallas_call(
        body, name=name,
        grid=(r // br,),
        in_specs=[spec] * 4, out_specs=[spec] * 4,
        out_shape=[jax.ShapeDtypeStruct((r, c), F32)] * 4,
        compiler_params=_cparams(("arbitrary",)),
    )(g, w, m, v)


BIG = ("w_in", "w_conv_out", "w_pool", "w_pool_out", "w_o", "w_ffn_gate", "w_ffn_up", "w_ffn_down")
REPL = ("g_mix", "b_gate", "b_dw", "ln_g", "ln_b", "pool_scale", "g_ffn", "g_final")
GROUP_MIX = ("w_conv_out", "w_pool", "w_pool_out", "w_o")
GROUP_FFN = ("w_ffn_gate", "w_ffn_up", "w_ffn_down")
TRANSPOSED = ("w_ffn_gate", "w_ffn_up")
WEIGHT_ORDER = ("meta_tokens", "g_mix", "w_in", "b_gate", "w_dw", "b_dw", "ln_g", "ln_b", "w_conv_out", "w_pool",
                "pool_scale", "w_pool_out", "w_o", "g_ffn", "w_ffn_gate", "w_ffn_up", "w_ffn_down", "g_final")


def _shard2d(name, a):
    a = a[0]
    if name == "w_pool":
        return a.reshape(4 * 64, GD)
    if name in TRANSPOSED:
        return a.T
    return a


def _unshard2d(name, a, shape):
    return a.T.reshape(shape) if name in TRANSPOSED else a.reshape(shape)


def _cols_to_slabs(a):
    m, n = a.shape
    return a.reshape(m, N_SHARD, n // N_SHARD).transpose(1, 0, 2)


def _slabs_to_cols(a):
    ns, m, c = a.shape
    return a.transpose(1, 0, 2).reshape(m, ns * c)


def kernel(x, meta_tokens, g_mix, w_in, b_gate, w_dw, b_dw, ln_g, ln_b, w_conv_out, w_pool, pool_scale, w_pool_out, w_o, g_ffn, w_ffn_gate, w_ffn_up, w_ffn_down, g_final, loss_target, m_meta_tokens, m_g_mix, m_w_in, m_b_gate, m_w_dw, m_b_dw, m_ln_g, m_ln_b, m_w_conv_out, m_w_pool, m_pool_scale, m_w_pool_out, m_w_o, m_g_ffn, m_w_ffn_gate, m_w_ffn_up, m_w_ffn_down, m_g_final, v_meta_tokens, v_g_mix, v_w_in, v_b_gate, v_w_dw, v_b_dw, v_ln_g, v_ln_b, v_w_conv_out, v_w_pool, v_pool_scale, v_w_pool_out, v_w_o, v_g_ffn, v_w_ffn_gate, v_w_ffn_up, v_w_ffn_down, v_g_final):
    args = dict(locals())
    w = {n: args[n] for n in WEIGHT_ORDER}
    mom = {n: args["m_" + n] for n in WEIGHT_ORDER}
    var = {n: args["v_" + n] for n in WEIGHT_ORDER}
    seq = x.shape[1]
    nb = seq // BR + 1
    tp = nb * BR
    tk = tp // 2 if (tp // 2) % 16 == 0 else BR
    t_total = seq + N_META
    cx, cy, cc = lax.axis_index("x"), lax.axis_index("y"), lax.axis_index("c")
    chip = 2 * cx + cy
    chip1 = jnp.reshape(chip, (1,)).astype(jnp.int32)
    core = jnp.reshape(cc, (1,)).astype(jnp.int32)
    others = jnp.sort(jnp.stack([2 * (1 - cx) + cy, 2 * cx + (1 - cy), 2 * (1 - cx) + (1 - cy)]))
    idx = jnp.concatenate([chip1, others.astype(jnp.int32), core])
    xs, target = x[0], loss_target[0]

    tiny = _gather_tiny(jnp.concatenate([w["meta_tokens"], w["w_dw"][0], jnp.zeros((1, GD), F32)], axis=0))
    small = {n: w[n] for n in REPL if n != "g_final"}
    small["g_final"] = w["g_final"].reshape(1, D)
    small["w_dw"] = _slabs_to_cols(tiny[:, N_META:])
    head = jnp.concatenate([jnp.zeros((PAD, D), F32), _slabs_to_cols(tiny[:, :N_META])], axis=0)

    def gather_start(group, dep, name):
        bufs = [_cast_into_slot(_shard2d(n, w[n]), chip1, dep, "cast_" + n) for n in group]
        return _gather_start(bufs, "gather_start_" + name)

    def gather_finish(group, start, after, name):
        landed = _gather_wait(start[0], start[1], start[2], [after], "gather_wait_" + name)
        return dict(zip(group, _forward_halves(landed, "forward_" + name)))

    st_in = gather_start(("w_in",), tiny, "in")
    u = _rms_u(head, xs, small["g_mix"], nb)
    gw = gather_finish(("w_in",), st_in, u, "in")
    st_mix = gather_start(GROUP_MIX, gw["w_in"], "mix")
    z = _in_proj(u, gw["w_in"], st_mix[3], nb)
    gw.update(gather_finish(GROUP_MIX, st_mix, z, "mix"))
    st_ffn = gather_start(GROUP_FFN, gw["w_o"], "ffn")
    w_pool_b = gw["w_pool"].reshape(N_SHARD, 4, 64, GD).transpose(1, 0, 2, 3).reshape(4, GD, GD)
    w_co_b, w_po_b, w_o_b = (gw[n].reshape(D, D) for n in ("w_conv_out", "w_pool_out", "w_o"))
    h1, yc, yp, mg, ca, cpre, m, mw, m2b = _mixers_fwd(
        z, head, xs, small["b_gate"] + st_ffn[3][0, 0], small["w_dw"], small["b_dw"], small["ln_g"], small["ln_b"],
        small["pool_scale"], w_co_b, w_pool_b, w_po_b, w_o_b, nb, t_total)
    gw.update(gather_finish(GROUP_FFN, st_ffn, h1, "ffn"))

    dh1, dh1b, vb, fb, dgb, dub, dh2b, loss, dg_ffn, dg_final = _ffn_fwd_bwd(
        h1, target, small["g_ffn"], small["g_final"], gw["w_ffn_gate"].reshape(D_FF, D),
        gw["w_ffn_up"].reshape(D_FF, D), gw["w_ffn_down"].reshape(D_FF, D), nb)

    def slabs(name, g):
        if name == "w_in":
            return g
        if name == "w_pool":
            return g.reshape(4, N_SHARD, 64, GD).transpose(1, 0, 2, 3).reshape(N_SHARD, 4 * 64, GD)
        return g.reshape(N_SHARD, g.shape[0] // N_SHARD, g.shape[1])

    def reduce_start(group, grads, name):
        g32 = [slabs(n, grads[n][0]) for n in group]
        g16 = [slabs(n, grads[n][1]) for n in group]
        from_sibling = _swap_halves_bf16(g16, "swap_halves_" + name)
        halves = [_add_sibling_half(g, sb, core, "add_sibling_" + n) for n, g, sb in zip(group, g32, from_sibling)]
        return [h for h, _ in halves], _scatter_start([hb for _, hb in halves], "scatter_start_" + name)

    def reduce_finish(group, halves, start, after, name):
        from_chips = _scatter_wait(start[0], start[1], start[2], after, "scatter_wait_" + name)
        return [_add_chip_slabs(h, rb, idx, "add_chips_" + n) for n, h, rb in zip(group, halves, from_chips)]

    half_ff = D_FF // 2
    grads_ffn = {
        "w_ffn_gate": _wgrad(dgb, vb, half_ff, D, tk, "wgrad_ffn_gate"),
        "w_ffn_up": _wgrad(dub, vb, half_ff, D, tk, "wgrad_ffn_up"),
        "w_ffn_down": _wgrad(fb, dh2b, half_ff, D, tk, "wgrad_ffn_down"),
    }
    halves_ffn, sc_ffn = reduce_start(GROUP_FFN, grads_ffn, "ffn")

    dycb, dypb, dzg, dconv, dmwb, dm, db_gate, dln_g, dln_b, db_dw, dps = _mixers_bwd_rows(
        dh1b, yc, yp, z, small["b_gate"], cpre, small["ln_g"], small["ln_b"], mw, small["pool_scale"],
        w_o_b, w_co_b, w_po_b, w_pool_b, sc_ffn[3], nb)
    grads_mix = {
        "w_conv_out": _wgrad(ca, dycb, D, D, tk, "wgrad_conv_out"),
        "w_pool": _wgrad(m, dmwb, GD, GD, tk, "wgrad_pool", diag=True),
        "w_pool_out": _wgrad(m2b, dypb, D, D, tk, "wgrad_pool_out"),
        "w_o": _wgrad(mg, dh1b, D, D, tk, "wgrad_o"),
    }
    halves_mix, sc_mix = reduce_start(GROUP_MIX, grads_mix, "mix")
    dzb, grad_x, dhead, dw_dw, dg_mix = _mixers_bwd_halo(
        dconv, dm, z, dzg, small["w_dw"], head, xs, small["g_mix"], dh1, gw["w_in"], sc_mix[3], nb, t_total)
    grads_in = {"w_in": _wgrad(u, dzb, D, D_IN // N_SHARD, tk, "wgrad_in", col_major=True)}
    halves_in, sc_in = reduce_start(("w_in",), grads_in, "in")

    packed = jnp.concatenate(
        [dg_mix + sc_in[3][0:1, 0:1], db_gate.reshape(2, D), db_dw, dln_g, dln_b, dps, dg_ffn, dg_final,
         jnp.broadcast_to(loss, (1, D)), jnp.zeros((6, D), F32), dhead[PAD:], dw_dw], axis=0)
    summed = _allreduce_small(packed)
    loss = summed[9, 0]

    first = GROUP_FFN + GROUP_MIX
    reduced_half = reduce_finish(GROUP_FFN, halves_ffn, sc_ffn, [summed], "ffn")
    reduced_half += reduce_finish(GROUP_MIX, halves_mix, sc_mix, [summed], "mix")
    reduced = dict(zip(first, _join_halves(reduced_half, "join_halves_first")))
    updates = {n: _adamw(reduced[n], _shard2d(n, w[n]), _shard2d(n, mom[n]), _shard2d(n, var[n]), "adamw_" + n)
               for n in first}

    def repl_stack(d):
        return jnp.concatenate([d["g_mix"], d["b_gate"].reshape(2, D), d["b_dw"], d["ln_g"], d["ln_b"],
                                d["pool_scale"], d["g_ffn"], d["g_final"].reshape(1, D), jnp.ones((7, D), F32)], axis=0)

    def shard_stack(d):
        return jnp.concatenate([d["meta_tokens"], d["w_dw"][0], jnp.ones((1, GD), F32)], axis=0)

    g_repl = summed[0:16]
    g_shard = lax.dynamic_slice_in_dim(summed[16:64], chip * GD, GD, axis=1)
    g_repl, d_repl, m_repl, v_repl = _adamw(g_repl, repl_stack(w), repl_stack(mom), repl_stack(var), "adamw_repl")
    g_shard, d_shard, m_shard, v_shard = _adamw(g_shard, shard_stack(w), shard_stack(mom), shard_stack(var), "adamw_cols")

    done_first = [updates[n][1] for n in first] + [d_repl, d_shard]
    last_half = reduce_finish(("w_in",), halves_in, sc_in, done_first, "in")
    reduced["w_in"] = _join_halves(last_half, "join_halves_in")[0]
    updates["w_in"] = _adamw(reduced["w_in"], w["w_in"][0], mom["w_in"][0], var["w_in"][0], "adamw_w_in")

    def unpack(name, repl, shard):
        if name == "meta_tokens":
            return shard[0:N_META]
        if name == "w_dw":
            return shard[N_META:N_META + KW].reshape(1, KW, GD)
        row = {"g_mix": 0, "b_gate": 1, "b_dw": 3, "ln_g": 4, "ln_b": 5, "pool_scale": 6, "g_ffn": 7, "g_final": 8}[name]
        if name == "b_gate":
            return repl[1:3].reshape(1, 2 * D)
        if name == "g_final":
            return repl[8]
        return repl[row:row + 1]

    out_g, out_d, out_m, out_v = {}, {}, {}, {}
    for n in WEIGHT_ORDER:
        if n in BIG:
            g, d_, m_, v_ = updates[n]
            shape = w[n].shape
            out_g[n], out_d[n], out_m[n], out_v[n] = (_unshard2d(n, a, shape) for a in (g, d_, m_, v_))
        else:
            out_g[n] = unpack(n, g_repl, g_shard)
            out_d[n] = unpack(n, d_repl, d_shard)
            out_m[n] = unpack(n, m_repl, m_shard)
            out_v[n] = unpack(n, v_repl, v_shard)
    return (loss, grad_x[None], *[out_g[n] for n in WEIGHT_ORDER], *[out_d[n] for n in WEIGHT_ORDER],
            *[out_m[n] for n in WEIGHT_ORDER], *[out_v[n] for n in WEIGHT_ORDER])
```

```python
import functools

import jax
import jax.numpy as jnp
from jax import lax
from jax.experimental import pallas as pl
from jax.experimental.pallas import tpu as pltpu

F32 = jnp.float32
BF16 = jnp.bfloat16
MESH = pl.DeviceIdType.MESH

D = 1024
N_META = 16
KW = 31
CPAD = KW // 2
POOL_WINDOWS = (2, 4, 8, 16)
GD = 256
D_IN = 5 * D
D_FF = 2816
N_SHARD = 4
BR = 256
HALO = 16
PAD = BR - N_META
EXT = BR + 2 * HALO
RMS_EPS = 1e-6
LN_EPS = 1e-5
LR, B1, B2, ADAM_EPS, WD, STEP = 0.001, 0.9, 0.999, 1e-08, 0.01, 10
VMEM_LIMIT = 56 * 1024 * 1024


def _cparams(sem, vmem=VMEM_LIMIT):
    return pltpu.CompilerParams(dimension_semantics=sem, vmem_limit_bytes=vmem)


def _dot(a, b):
    return jnp.dot(a, b, preferred_element_type=F32)


def _dot_nt(a, b):
    return lax.dot_general(a, b, (((1,), (1,)), ((), ())), preferred_element_type=F32)


def _dot_tn(a, b):
    return lax.dot_general(a, b, (((0,), (0,)), ((), ())), preferred_element_type=F32)


def _sigmoid(x):
    return 1.0 / (1.0 + jnp.exp(-x))


def _row_ids(i, n, offset=0):
    return lax.broadcasted_iota(jnp.int32, (n, 1), 0) + (i * BR + offset - PAD)


def _pool_cnt(t, w, t_total):
    left = w // 2
    right = w - 1 - left
    lo = jnp.clip(t - left, 0, t_total)
    hi = jnp.clip(t + right + 1, 0, t_total)
    return jnp.maximum(hi - lo, 1).astype(F32)


def _halo_specs(col, nb):
    last = nb * (BR // HALO) - 1
    return [
        pl.BlockSpec((HALO, D), lambda i: (jnp.maximum(i * (BR // HALO) - 1, 0), col)),
        pl.BlockSpec((BR, D), lambda i: (i, col)),
        pl.BlockSpec((HALO, D), lambda i: (jnp.minimum((i + 1) * (BR // HALO), last), col)),
    ]


def _fill_ext(ext_ref, prev, cur, nxt, i, nb):
    ext_ref[0:HALO, :] = jnp.where(i > 0, prev, 0.0)
    ext_ref[HALO:HALO + BR, :] = cur
    ext_ref[HALO + BR:EXT, :] = jnp.where(i < nb - 1, nxt, 0.0)


ROT_ROWS = EXT - 8


def _fill_rot(rot_ref, ext_ref, lanes):
    for r in range(1, 8):
        rot_ref[r] = ext_ref[pl.ds(r, ROT_ROWS), lanes]


def _tap(rot_ref, ext_ref, lanes, offset):
    q, r = divmod(offset, 8)
    if r == 0:
        return ext_ref[pl.ds(8 * q, BR), lanes]
    return rot_ref[r, pl.ds(8 * q, BR), :]


def _row_spec(width=D):
    return pl.BlockSpec((BR, width), lambda i: (i, 0))


def _x_spec():
    return pl.BlockSpec((BR, D), lambda i: (jnp.maximum(i - 1, 0), 0))


def _const_spec(shape):
    nd = len(shape)
    return pl.BlockSpec(shape, lambda i: (0,) * nd)


def _rms_u(head, x, g_mix, nb):
    def body(head_ref, x_ref, g_ref, u_ref):
        i = pl.program_id(0)
        h = jnp.where(i == 0, head_ref[...], x_ref[...])
        r = lax.rsqrt(jnp.mean(h * h, axis=-1, keepdims=True) + RMS_EPS)
        u_ref[...] = ((h * r) * g_ref[...]).astype(BF16)

    return pl.pallas_call(
        body, name="rms_u",
        grid=(nb,),
        in_specs=[_const_spec((BR, D)), _x_spec(), _const_spec((1, D))],
        out_specs=_row_spec(),
        out_shape=jax.ShapeDtypeStruct((nb * BR, D), BF16),
        compiler_params=_cparams(("arbitrary",)),
    )(head, x, g_mix)


def _in_proj(u, w_in_b, dep, nb):
    tp = nb * BR
    ns = w_in_b.shape[0]
    wcols = w_in_b.shape[2]
    rows = tp // 4 if (tp // 4) % 16 == 0 else BR

    def body(u_ref, w_ref, dep_ref, z_ref):
        z_ref[...] = _dot(u_ref[...], w_ref[...])

    return pl.pallas_call(
        body, name="in_proj",
        grid=(ns, tp // rows),
        in_specs=[
            pl.BlockSpec((rows, D), lambda s, i: (i, 0)),
            pl.BlockSpec((None, D, wcols), lambda s, i: (s, 0, 0)),
            pl.BlockSpec(memory_space=pl.ANY),
        ],
        out_specs=pl.BlockSpec((rows, wcols), lambda s, i: (i, s)),
        out_shape=jax.ShapeDtypeStruct((tp, ns * wcols), F32),
        compiler_params=_cparams(("arbitrary", "arbitrary")),
    )(u, w_in_b, dep)


def _mixers_fwd(z, head, x, b_gate, w_dw, b_dw, ln_g, ln_b, pool_scale, w_co, w_pool, w_po, w_o, nb, t_total):
    tp = nb * BR

    def body(avp, av, avn, agp, ag, agn, pp, pc, pn, za, zb, head_ref, x_ref, bg_ref, wdw_ref, bdw_ref,
             lng_ref, lnb_ref, ps_ref, wco_ref, wpool_ref, wpo_ref, wo_ref,
             h1_ref, yc_ref, yp_ref, mg_ref, ca_ref, cpre_ref, m_ref, mw_ref, m2b_ref, ext_ref, pext_ref, rot_ref):
        i = pl.program_id(0)
        _fill_ext(ext_ref, avp[...] * _sigmoid(agp[...]), av[...] * _sigmoid(ag[...]),
                  avn[...] * _sigmoid(agn[...]), i, nb)
        _fill_ext(pext_ref, pp[...], pc[...], pn[...], i, nb)

        def conv_chunk(c, carry):
            lanes = pl.ds(pl.multiple_of(c * 128, 128), 128)
            _fill_rot(rot_ref, ext_ref, lanes)
            acc = jnp.broadcast_to(bdw_ref[:, lanes], (BR, 128))
            for k in range(KW):
                acc = acc + wdw_ref[k:k + 1, lanes] * _tap(rot_ref, ext_ref, lanes, 1 + k)
            cpre_ref[:, lanes] = acc
            return carry
        lax.fori_loop(0, D // 128, conv_chunk, 0)

        conv = cpre_ref[...]
        mu = jnp.mean(conv, axis=-1, keepdims=True)
        xc = conv - mu
        rstd = lax.rsqrt(jnp.mean(xc * xc, axis=-1, keepdims=True) + LN_EPS)
        ln = (xc * rstd) * lng_ref[...] + lnb_ref[...]
        cact = (ln * _sigmoid(ln)).astype(BF16)
        ca_ref[...] = cact
        y_conv = _dot(cact, wco_ref[...])
        yc_ref[...] = y_conv

        t = _row_ids(i, BR)
        for gi, w in enumerate(POOL_WINDOWS):
            left = w // 2
            right = w - 1 - left
            lanes = slice(gi * GD, (gi + 1) * GD)
            s = pext_ref[pl.ds(HALO - left, BR), lanes]
            for j in range(-left + 1, right + 1):
                s = s + pext_ref[pl.ds(HALO + j, BR), lanes]
            m = (s / _pool_cnt(t, w, t_total) - pext_ref[HALO:HALO + BR, lanes]).astype(BF16)
            m_ref[:, lanes] = m
            mw_ref[:, lanes] = _dot(m, wpool_ref[gi])
        mw = mw_ref[...]
        m2b = (mw * ps_ref[...]).astype(BF16)
        m2b_ref[...] = m2b
        y_pool = _dot(m2b, wpo_ref[...])
        yp_ref[...] = y_pool

        s_a = _sigmoid(za[...] + bg_ref[:, 0:D])
        s_b = _sigmoid(zb[...] + bg_ref[:, D:2 * D])
        merged = (s_a * y_conv + s_b * y_pool).astype(BF16)
        mg_ref[...] = merged
        h0 = jnp.where(i == 0, head_ref[...], x_ref[...])
        h1_ref[...] = h0 + _dot(merged, wo_ref[...])

    in_specs = (_halo_specs(0, nb) + _halo_specs(1, nb) + _halo_specs(2, nb)
                + [pl.BlockSpec((BR, D), lambda i: (i, 3)), pl.BlockSpec((BR, D), lambda i: (i, 4)),
                   _const_spec((BR, D)), _x_spec(), _const_spec((1, 2 * D)), _const_spec((32, D)),
                   _const_spec((1, D)), _const_spec((1, D)), _const_spec((1, D)), _const_spec((1, D)),
                   _const_spec((D, D)), _const_spec((4, GD, GD)), _const_spec((D, D)), _const_spec((D, D))])
    outs = [(F32, "h1"), (F32, "yc"), (F32, "yp"), (BF16, "mg"), (BF16, "ca"), (F32, "cpre"), (BF16, "m"), (F32, "mw"),
            (BF16, "m2b")]
    return pl.pallas_call(
        body, name="mixers_fwd",
        grid=(nb,),
        in_specs=in_specs,
        out_specs=[_row_spec() for _ in outs],
        out_shape=[jax.ShapeDtypeStruct((tp, D), dt) for dt, _ in outs],
        scratch_shapes=[pltpu.VMEM((EXT, D), F32), pltpu.VMEM((EXT, D), F32), pltpu.VMEM((8, ROT_ROWS, 128), F32)],
        compiler_params=_cparams(("arbitrary",)),
    )(z, z, z, z, z, z, z, z, z, z, z, head, x, b_gate, w_dw, b_dw, ln_g, ln_b, pool_scale, w_co, w_pool, w_po, w_o)


def _ffn_fwd_bwd(h1, target, g_ffn, g_final, w_g, w_u, w_d, nb):
    tp = nb * BR

    def body(h1_ref, tgt_ref, gf_ref, gfin_ref, wg_hbm, wu_hbm, wd_hbm,
             dh1_ref, dh1b_ref, vb_ref, fb_ref, dgb_ref, dub_ref, dh2b_ref, loss_ref, dgf_ref, dgfin_ref,
             wg_ref, wu_ref, wd_ref, sem):
        i = pl.program_id(0)

        @pl.when(i == 0)
        def _():
            copies = [pltpu.make_async_copy(wg_hbm, wg_ref, sem.at[0]),
                      pltpu.make_async_copy(wu_hbm, wu_ref, sem.at[1]),
                      pltpu.make_async_copy(wd_hbm, wd_ref, sem.at[2])]
            for cp in copies:
                cp.start()
            loss_ref[...] = jnp.zeros_like(loss_ref)
            dgf_ref[...] = jnp.zeros_like(dgf_ref)
            dgfin_ref[...] = jnp.zeros_like(dgfin_ref)
            for cp in copies:
                cp.wait()

        h1 = h1_ref[...]
        r1 = lax.rsqrt(jnp.mean(h1 * h1, axis=-1, keepdims=True) + RMS_EPS)
        vn = h1 * r1
        vb = (vn * gf_ref[...]).astype(BF16)
        vb_ref[...] = vb
        g = _dot_nt(vb, wg_ref[...])
        up = _dot_nt(vb, wu_ref[...])
        sg = _sigmoid(g)
        sl = g * sg
        fb = (sl * up).astype(BF16)
        fb_ref[...] = fb
        h2 = h1 + _dot(fb, wd_ref[...])
        r2 = lax.rsqrt(jnp.mean(h2 * h2, axis=-1, keepdims=True) + RMS_EPS)
        yn = h2 * r2
        valid = i > 0
        diff = jnp.where(valid, yn * gfin_ref[...] - tgt_ref[...], 0.0)
        loss_ref[...] += 0.5 * jnp.sum(jnp.mean(diff * diff, axis=-1, keepdims=True))
        dy = diff * (1.0 / D)
        dgfin_ref[...] += jnp.sum(dy * yn, axis=0, keepdims=True)
        gd = dy * gfin_ref[...]
        dh2 = r2 * (gd - yn * jnp.mean(yn * gd, axis=-1, keepdims=True))
        dh2b = dh2.astype(BF16)
        dh2b_ref[...] = dh2b
        df = _dot_nt(dh2b, wd_ref[...])
        dub = (df * sl).astype(BF16)
        dgb = (df * up * (sg * (1.0 + g * (1.0 - sg)))).astype(BF16)
        dub_ref[...] = dub
        dgb_ref[...] = dgb
        dv = _dot(dgb, wg_ref[...]) + _dot(dub, wu_ref[...])
        dgf_ref[...] += jnp.sum(dv * vn, axis=0, keepdims=True)
        gd1 = dv * gf_ref[...]
        dh1 = dh2 + r1 * (gd1 - vn * jnp.mean(vn * gd1, axis=-1, keepdims=True))
        dh1_ref[...] = dh1
        dh1b_ref[...] = dh1.astype(BF16)

    any_spec = pl.BlockSpec(memory_space=pl.ANY)
    return pl.pallas_call(
        body, name="ffn_fwd_bwd",
        grid=(nb,),
        in_specs=[_row_spec(), _x_spec(), _const_spec((1, D)), _const_spec((1, D)), any_spec, any_spec, any_spec],
        out_specs=[_row_spec(), _row_spec(), _row_spec(), _row_spec(D_FF), _row_spec(D_FF), _row_spec(D_FF), _row_spec(),
                   _const_spec((1, 1)), _const_spec((1, D)), _const_spec((1, D))],
        out_shape=[jax.ShapeDtypeStruct((tp, D), F32), jax.ShapeDtypeStruct((tp, D), BF16),
                   jax.ShapeDtypeStruct((tp, D), BF16), jax.ShapeDtypeStruct((tp, D_FF), BF16),
                   jax.ShapeDtypeStruct((tp, D_FF), BF16), jax.ShapeDtypeStruct((tp, D_FF), BF16),
                   jax.ShapeDtypeStruct((tp, D), BF16), jax.ShapeDtypeStruct((1, 1), F32),
                   jax.ShapeDtypeStruct((1, D), F32), jax.ShapeDtypeStruct((1, D), F32)],
        scratch_shapes=[pltpu.VMEM((D_FF, D), BF16), pltpu.VMEM((D_FF, D), BF16), pltpu.VMEM((D_FF, D), BF16),
                        pltpu.SemaphoreType.DMA((3,))],
        compiler_params=_cparams(("arbitrary",)),
    )(h1, target, g_ffn, g_final, w_g, w_u, w_d)


def _mixers_bwd_rows(dh1b, yc, yp, z, b_gate, cpre, ln_g, ln_b, mw, pool_scale, w_o, w_co, w_po, w_pool, dep, nb):
    tp = nb * BR

    def body(dh1b_ref, yc_ref, yp_ref, za, zb, bg_ref, cpre_ref, lng_ref, lnb_ref, mw_ref, ps_ref,
             wo_ref, wco_ref, wpo_ref, wpool_ref, dep_ref,
             dycb_ref, dypb_ref, dzg_ref, dconv_ref, dmwb_ref, dm_ref, dbg_ref, dlng_ref, dlnb_ref, dbdw_ref, dps_ref):
        i = pl.program_id(0)

        @pl.when(i == 0)
        def _():
            for r in (dbg_ref, dlng_ref, dlnb_ref, dbdw_ref, dps_ref):
                r[...] = jnp.zeros_like(r)

        dmg = _dot_nt(dh1b_ref[...], wo_ref[...])
        s_a = _sigmoid(za[...] + bg_ref[:, 0:D])
        s_b = _sigmoid(zb[...] + bg_ref[:, D:2 * D])
        dycb = (dmg * s_a).astype(BF16)
        dypb = (dmg * s_b).astype(BF16)
        dycb_ref[...] = dycb
        dypb_ref[...] = dypb
        dza = dmg * yc_ref[...] * (s_a * (1.0 - s_a))
        dzb = dmg * yp_ref[...] * (s_b * (1.0 - s_b))
        dzg_ref[:, 0:D] = dza.astype(BF16)
        dzg_ref[:, D:2 * D] = dzb.astype(BF16)
        dbg_ref[:, 0:D] += jnp.sum(dza, axis=0, keepdims=True)
        dbg_ref[:, D:2 * D] += jnp.sum(dzb, axis=0, keepdims=True)

        dca = _dot_nt(dycb, wco_ref[...])
        conv = cpre_ref[...]
        mu = jnp.mean(conv, axis=-1, keepdims=True)
        xc = conv - mu
        rstd = lax.rsqrt(jnp.mean(xc * xc, axis=-1, keepdims=True) + LN_EPS)
        xhat = xc * rstd
        ln = xhat * lng_ref[...] + lnb_ref[...]
        sg = _sigmoid(ln)
        dln = dca * (sg * (1.0 + ln * (1.0 - sg)))
        dlng_ref[...] += jnp.sum(dln * xhat, axis=0, keepdims=True)
        dlnb_ref[...] += jnp.sum(dln, axis=0, keepdims=True)
        dxh = dln * lng_ref[...]
        dconv = rstd * (dxh - jnp.mean(dxh, axis=-1, keepdims=True)
                        - xhat * jnp.mean(dxh * xhat, axis=-1, keepdims=True))
        dconv_ref[...] = dconv
        dbdw_ref[...] += jnp.sum(dconv, axis=0, keepdims=True)

        dm2 = _dot_nt(dypb, wpo_ref[...])
        dps_ref[...] += jnp.sum(dm2 * mw_ref[...], axis=0, keepdims=True)
        dmwb = (dm2 * ps_ref[...]).astype(BF16)
        dmwb_ref[...] = dmwb
        for gi in range(len(POOL_WINDOWS)):
            lanes = slice(gi * GD, (gi + 1) * GD)
            dm_ref[:, lanes] = _dot_nt(dmwb[:, lanes], wpool_ref[gi])

    in_specs = [_row_spec(), _row_spec(), _row_spec(),
                pl.BlockSpec((BR, D), lambda i: (i, 3)), pl.BlockSpec((BR, D), lambda i: (i, 4)),
                _const_spec((1, 2 * D)), _row_spec(), _const_spec((1, D)), _const_spec((1, D)), _row_spec(),
                _const_spec((1, D)), _const_spec((D, D)), _const_spec((D, D)), _const_spec((D, D)),
                _const_spec((4, GD, GD)), pl.BlockSpec(memory_space=pl.ANY)]
    return pl.pallas_call(
        body, name="mixers_bwd_rows",
        grid=(nb,),
        in_specs=in_specs,
        out_specs=[_row_spec(), _row_spec(), _row_spec(2 * D), _row_spec(), _row_spec(), _row_spec(),
                   _const_spec((1, 2 * D)), _const_spec((1, D)), _const_spec((1, D)), _const_spec((1, D)),
                   _const_spec((1, D))],
        out_shape=[jax.ShapeDtypeStruct((tp, D), BF16), jax.ShapeDtypeStruct((tp, D), BF16),
                   jax.ShapeDtypeStruct((tp, 2 * D), BF16), jax.ShapeDtypeStruct((tp, D), F32),
                   jax.ShapeDtypeStruct((tp, D), BF16), jax.ShapeDtypeStruct((tp, D), F32),
                   jax.ShapeDtypeStruct((1, 2 * D), F32), jax.ShapeDtypeStruct((1, D), F32),
                   jax.ShapeDtypeStruct((1, D), F32), jax.ShapeDtypeStruct((1, D), F32),
                   jax.ShapeDtypeStruct((1, D), F32)],
        compiler_params=_cparams(("arbitrary",)),
    )(dh1b, yc, yp, z, z, b_gate, cpre, ln_g, ln_b, mw, pool_scale, w_o, w_co, w_po, w_pool, dep)


def _mixers_bwd_halo(dconv, dm, z, dzg, w_dw, head, x, g_mix, dh1, w_in_b, dep, nb, t_total):
    tp = nb * BR
    ns = w_in_b.shape[0]
    wcols = w_in_b.shape[2]
    seq = x.shape[0]

    def body(dcp, dcc, dcn, dmp, dmc, dmn, avp, av, avn, agp, ag, agn, dzg_ref, wdw_ref, head_ref, x_ref, g_ref,
             dh1_ref, w_hbm, dep_ref,
             dzb_ref, gx_ref, dhead_ref, dwdw_ref, dgmix_ref,
             w_ref, sem, aext_ref, dext_ref, qext_ref, da_ref, rot_ref, dwp_ref):
        i = pl.program_id(0)

        @pl.when(i == 0)
        def _():
            cp = pltpu.make_async_copy(w_hbm, w_ref, sem.at[0])
            cp.start()
            dwp_ref[...] = jnp.zeros_like(dwp_ref)
            dgmix_ref[...] = jnp.zeros_like(dgmix_ref)
            cp.wait()

        sig_g = _sigmoid(ag[...])
        _fill_ext(aext_ref, avp[...] * _sigmoid(agp[...]), av[...] * sig_g, avn[...] * _sigmoid(agn[...]), i, nb)
        _fill_ext(dext_ref, dcp[...], dcc[...], dcn[...], i, nb)
        _fill_ext(qext_ref, dmp[...], dmc[...], dmn[...], i, nb)

        def conv_chunk(c, carry):
            lanes = pl.ds(pl.multiple_of(c * 128, 128), 128)
            _fill_rot(rot_ref, dext_ref, lanes)
            acc = jnp.zeros((BR, 128), F32)
            for k in range(KW):
                acc = acc + wdw_ref[k:k + 1, lanes] * _tap(rot_ref, dext_ref, lanes, KW - k)
            da_ref[:, lanes] = acc
            _fill_rot(rot_ref, aext_ref, lanes)
            dcv = dext_ref[HALO:HALO + BR, lanes]
            for k in range(KW):
                prod = _tap(rot_ref, aext_ref, lanes, 1 + k) * dcv
                dwp_ref[k, :, lanes] += jnp.sum(prod.reshape(BR // 8, 8, 128), axis=0)
            return carry
        lax.fori_loop(0, D // 128, conv_chunk, 0)

        @pl.when(i == nb - 1)
        def _():
            dwdw_ref[...] = jnp.sum(dwp_ref[...], axis=1)

        da = da_ref[...]
        a_val = av[...]
        dzb_ref[:, 0:D] = (da * sig_g).astype(BF16)
        dzb_ref[:, D:2 * D] = (da * a_val * (sig_g * (1.0 - sig_g))).astype(BF16)

        t_ext = _row_ids(i, EXT, -HALO)
        for gi, w in enumerate(POOL_WINDOWS):
            left = w // 2
            right = w - 1 - left
            lanes = slice(gi * GD, (gi + 1) * GD)
            qext_ref[:, lanes] = qext_ref[:, lanes] / _pool_cnt(t_ext, w, t_total)
            s = qext_ref[pl.ds(HALO - right, BR), lanes]
            for j in range(-right + 1, left + 1):
                s = s + qext_ref[pl.ds(HALO + j, BR), lanes]
            dzb_ref[:, 2 * D + gi * GD:2 * D + (gi + 1) * GD] = (s - dmc[:, lanes]).astype(BF16)
        dzb_ref[:, 3 * D:5 * D] = dzg_ref[...]

        du = _dot_nt(dzb_ref[:, 0:wcols], w_ref[0])
        for s_i in range(1, ns):
            du = du + _dot_nt(dzb_ref[:, s_i * wcols:(s_i + 1) * wcols], w_ref[s_i])
        h0 = jnp.where(i == 0, head_ref[...], x_ref[...])
        r0 = lax.rsqrt(jnp.mean(h0 * h0, axis=-1, keepdims=True) + RMS_EPS)
        un = h0 * r0
        dgmix_ref[...] += jnp.sum(du * un, axis=0, keepdims=True)
        gd = du * g_ref[...]
        dh0 = dh1_ref[...] + r0 * (gd - un * jnp.mean(un * gd, axis=-1, keepdims=True))
        gx_ref[...] = dh0

        @pl.when(i == 0)
        def _():
            dhead_ref[...] = dh0

    any_spec = pl.BlockSpec(memory_space=pl.ANY)
    in_specs = (_halo_specs(0, nb) + _halo_specs(0, nb) + _halo_specs(0, nb) + _halo_specs(1, nb)
                + [_row_spec(2 * D), _const_spec((32, D)), _const_spec((BR, D)), _x_spec(), _const_spec((1, D)),
                   _row_spec(), any_spec, any_spec])
    return pl.pallas_call(
        body, name="mixers_bwd_halo",
        grid=(nb,),
        in_specs=in_specs,
        out_specs=[_row_spec(D_IN), _x_spec(), _const_spec((BR, D)), _const_spec((32, D)), _const_spec((1, D))],
        out_shape=[jax.ShapeDtypeStruct((tp, D_IN), BF16), jax.ShapeDtypeStruct((seq, D), F32),
                   jax.ShapeDtypeStruct((BR, D), F32), jax.ShapeDtypeStruct((32, D), F32),
                   jax.ShapeDtypeStruct((1, D), F32)],
        scratch_shapes=[pltpu.VMEM((ns, D, wcols), BF16), pltpu.SemaphoreType.DMA((1,)),
                        pltpu.VMEM((EXT, D), F32), pltpu.VMEM((EXT, D), F32), pltpu.VMEM((EXT, D), F32),
                        pltpu.VMEM((BR, D), F32), pltpu.VMEM((8, ROT_ROWS, 128), F32), pltpu.VMEM((32, 8, D), F32)],
        compiler_params=_cparams(("arbitrary",)),
    )(dconv, dconv, dconv, dm, dm, dm, z, z, z, z, z, z, dzg, w_dw, head, x, g_mix, dh1, w_in_b, dep)


def _wgrad(a, c, tm, tn, tk, name, diag=False, col_major=False, dep=None):
    tp, m = a.shape
    n = c.shape[1]
    nk = tp // tk
    gm, gn = m // tm, n // tn

    def body(a_ref, c_ref, *rest):
        o_ref, ob_ref = rest[-2:]
        k = pl.program_id(2)

        @pl.when(k == 0)
        def _():
            o_ref[...] = jnp.zeros_like(o_ref)

        o_ref[...] += _dot_tn(a_ref[...], c_ref[...])

        @pl.when(k == nk - 1)
        def _():
            ob_ref[...] = o_ref[...].astype(BF16)

    c_map = lambda i, j, k: (k, j)
    grid = (gm, gn, nk)
    deps = [] if dep is None else [dep]
    if diag:
        grid = (gm, 1, nk)
        c_map = lambda i, j, k: (k, i)
        o_spec = pl.BlockSpec((tm, tn), lambda i, j, k: (i, 0))
        o_shape = (m, tn)
    elif col_major:
        o_spec = pl.BlockSpec((None, tm, tn), lambda i, j, k: (j, i, 0))
        o_shape = (gn, m, tn)
    else:
        o_spec = pl.BlockSpec((tm, tn), lambda i, j, k: (i, j))
        o_shape = (m, n)
    return pl.pallas_call(
        body, name=name,
        grid=grid,
        in_specs=[pl.BlockSpec((tk, tm), lambda i, j, k: (k, i)), pl.BlockSpec((tk, tn), c_map)]
        + [pl.BlockSpec(memory_space=pl.ANY)] * len(deps),
        out_specs=[o_spec, o_spec],
        out_shape=[jax.ShapeDtypeStruct(o_shape, F32), jax.ShapeDtypeStruct(o_shape, BF16)],
        compiler_params=_cparams(("arbitrary", "arbitrary", "arbitrary")),
    )(a, c, *deps)


def _place():
    x, y, c = lax.axis_index("x"), lax.axis_index("y"), lax.axis_index("c")
    others = [(1 - x, y), (x, 1 - y), (1 - x, 1 - y)]
    return x, y, c, others


def _split2(a, axis=0):
    return a.reshape(a.shape[:axis] + (2, a.shape[axis] // 2) + a.shape[axis + 1:])


def _merge2(a, axis=0):
    return a.reshape(a.shape[:axis] + (2 * a.shape[axis + 1],) + a.shape[axis + 2:])


def _cast_into_slot(w2d, chip, dep, name):
    r, c = w2d.shape
    r2 = r // 2

    def body(chip_ref, w_ref, dep_ref, o_ref):
        o_ref[...] = w_ref[...].astype(BF16)

    return pl.pallas_call(
        body, name=name,
        grid_spec=pltpu.PrefetchScalarGridSpec(
            num_scalar_prefetch=1, grid=(2,),
            in_specs=[pl.BlockSpec((r2, c), lambda h, chip_ref: (h, 0)), pl.BlockSpec(memory_space=pl.ANY)],
            out_specs=pl.BlockSpec((None, None, r2, c), lambda h, chip_ref: (chip_ref[0], h, 0, 0))),
        out_shape=jax.ShapeDtypeStruct((N_SHARD, 2, r2, c), BF16),
        compiler_params=_cparams(("arbitrary",)),
    )(chip, w2d, dep)


HBM_SPEC = pl.BlockSpec(memory_space=pltpu.HBM)
SEM_SPEC = pl.BlockSpec(memory_space=pltpu.SEMAPHORE)
DATAFLOW = pltpu.SideEffectType.DATAFLOW_SIDE_EFFECTING
TOKEN = jax.ShapeDtypeStruct((8, 128), F32)


def _in_hbm(a):
    return pltpu.with_memory_space_constraint(a, pltpu.HBM)


def _gather_tiny(v):
    vm = pl.BlockSpec(memory_space=pltpu.VMEM)

    def body(v_ref, out_ref, send_sems, recv_sems):
        x, y, c, others = _place()
        mine = 2 * x + y
        sends = [pltpu.make_async_remote_copy(
            src_ref=v_ref, dst_ref=out_ref.at[mine], send_sem=send_sems.at[j], recv_sem=recv_sems.at[j],
            device_id=(*chip, c), device_id_type=MESH) for j, chip in enumerate(others)]
        for cp in sends:
            cp.start()
        out_ref[mine] = v_ref[...]
        for j, chip in enumerate(others):
            landed = out_ref.at[2 * chip[0] + chip[1]]
            pltpu.make_async_remote_copy(
                src_ref=landed, dst_ref=landed, send_sem=send_sems.at[j], recv_sem=recv_sems.at[j],
                device_id=(x, y, c), device_id_type=MESH).wait_recv()
        for cp in sends:
            cp.wait_send()

    return pl.pallas_call(
        body, name="gather_tiny",
        in_specs=[vm], out_specs=vm,
        out_shape=jax.ShapeDtypeStruct((N_SHARD,) + v.shape, v.dtype),
        scratch_shapes=[pltpu.SemaphoreType.DMA((3,)), pltpu.SemaphoreType.DMA((3,))],
    )(v)


def _ici_copies(srcs, dsts, send_sems, recv_sems, started):
    x, y, c, others = _place()
    mine = 2 * x + y
    copies = []
    for a in range(len(srcs)):
        for j, chip in enumerate(others):
            there = 2 * chip[0] + chip[1]
            src, dst = srcs[a](mine, there, c), dsts[a](mine, there, c)
            if not started:
                dst = dsts[a](there, mine, c)
            copies.append(pltpu.make_async_remote_copy(
                src_ref=src, dst_ref=dst, send_sem=send_sems.at[a * 3 + j], recv_sem=recv_sems.at[a * 3 + j],
                device_id=(*chip, c), device_id_type=MESH))
    return copies


def _split_start(srcs_of, dsts_of, arrays, n_src, name, copies_of=None, n_sems=None):
    n = len(arrays)
    n_sems = n_sems or 3 * n_src
    copies_of = copies_of or (lambda ins, ss, rs, started: _ici_copies(srcs_of(ins), dsts_of(ins), ss, rs, started))

    def body(*refs):
        ins = refs[:n]
        send_sems, recv_sems = refs[n], refs[n + 1]
        token = refs[2 * n + 2]
        for cp in copies_of(ins, send_sems, recv_sems, True):
            cp.start()
        token[...] = jnp.zeros_like(token)

    out = pl.pallas_call(
        body, name=name,
        in_specs=[HBM_SPEC] * n,
        out_specs=(SEM_SPEC, SEM_SPEC, *([HBM_SPEC] * n), pl.BlockSpec(memory_space=pltpu.VMEM)),
        out_shape=(pltpu.SemaphoreType.DMA((n_sems,)), pltpu.SemaphoreType.DMA((n_sems,)),
                   *[pltpu.HBM(a.shape, a.dtype) for a in arrays], TOKEN),
        input_output_aliases={a: 2 + a for a in range(n)},
        compiler_params=pltpu.CompilerParams(has_side_effects=DATAFLOW),
    )(*[_in_hbm(a) for a in arrays])
    return out[0], out[1], list(out[2:2 + n]), out[2 + n]


def _split_wait(srcs_of, dsts_of, send_sems, recv_sems, arrays, after, name, copies_of=None):
    n = len(arrays)
    copies_of = copies_of or (lambda ins, ss, rs, started: _ici_copies(srcs_of(ins), dsts_of(ins), ss, rs, started))

    def body(*refs):
        ins = refs[:n]
        send_sems, recv_sems = refs[n], refs[n + 1]
        for cp in copies_of(ins, send_sems, recv_sems, False):
            cp.wait_send()
            cp.wait_recv()

    return pl.pallas_call(
        body, name=name,
        in_specs=[HBM_SPEC] * n + [SEM_SPEC, SEM_SPEC] + [pl.BlockSpec(memory_space=pl.ANY)] * len(after),
        out_specs=[HBM_SPEC] * n,
        out_shape=[pltpu.HBM(a.shape, a.dtype) for a in arrays],
        input_output_aliases={a: a for a in range(n)},
        compiler_params=pltpu.CompilerParams(has_side_effects=DATAFLOW),
    )(*arrays, send_sems, recv_sems, *after)


def _gather_views(ins):
    view = [lambda frm, to, c, r=r: r.at[frm, c] for r in ins]
    return view


def _gather_start(bufs, name):
    return _split_start(_gather_views, _gather_views, bufs, len(bufs), name)


def _gather_wait(send_sems, recv_sems, bufs, after, name):
    return _split_wait(_gather_views, _gather_views, send_sems, recv_sems, bufs, after, name)


def _forward_halves(bufs, name):
    n = len(bufs)
    any_spec = pl.BlockSpec(memory_space=pl.ANY)

    def body(*refs):
        outs = refs[n:2 * n]
        send_sems, recv_sems = refs[2 * n:]
        x, y, c, others = _place()
        copies = []
        for a in range(n):
            for j, chip in enumerate(others):
                landed = outs[a].at[2 * chip[0] + chip[1], c]
                copies.append(pltpu.make_async_remote_copy(
                    src_ref=landed, dst_ref=landed, send_sem=send_sems.at[a * 3 + j], recv_sem=recv_sems.at[a * 3 + j],
                    device_id=(x, y, 1 - c), device_id_type=MESH))
        for cp in copies:
            cp.start()
        for a in range(n):
            for j, chip in enumerate(others):
                landed = outs[a].at[2 * chip[0] + chip[1], 1 - c]
                pltpu.make_async_remote_copy(
                    src_ref=landed, dst_ref=landed, send_sem=send_sems.at[a * 3 + j], recv_sem=recv_sems.at[a * 3 + j],
                    device_id=(x, y, c), device_id_type=MESH).wait_recv()
        for cp in copies:
            cp.wait_send()

    out = pl.pallas_call(
        body, name=name,
        in_specs=[any_spec] * n, out_specs=[any_spec] * n,
        out_shape=[jax.ShapeDtypeStruct(b.shape, b.dtype) for b in bufs],
        input_output_aliases={a: a for a in range(n)},
        scratch_shapes=[pltpu.SemaphoreType.DMA((3 * n,)), pltpu.SemaphoreType.DMA((3 * n,))],
    )(*bufs)
    return [_merge2(o, 1) for o in out]


def _swap_halves_bf16(gbs, name):
    n = len(gbs)
    any_spec = pl.BlockSpec(memory_space=pl.ANY)

    def body(*refs):
        ins, outs = refs[:n], refs[n:2 * n]
        send_sems, recv_sems = refs[2 * n:]
        x, y, c, _ = _place()
        copies = []
        for a in range(n):
            copies.append(pltpu.make_async_remote_copy(
                src_ref=ins[a].at[:, 1 - c], dst_ref=outs[a], send_sem=send_sems.at[a], recv_sem=recv_sems.at[a],
                device_id=(x, y, 1 - c), device_id_type=MESH))
        for cp in copies:
            cp.start()
        for cp in copies:
            cp.wait()

    return pl.pallas_call(
        body, name=name,
        in_specs=[any_spec] * n, out_specs=[any_spec] * n,
        out_shape=[jax.ShapeDtypeStruct((g.shape[0], g.shape[1] // 2, g.shape[2]), g.dtype) for g in gbs],
        scratch_shapes=[pltpu.SemaphoreType.DMA((n,)), pltpu.SemaphoreType.DMA((n,))],
    )(*[_split2(g, 1) for g in gbs])


def _scatter_srcs(n):
    return lambda ins: [lambda frm, to, c, r=r: r.at[to] for r in ins[:n]]


def _scatter_dsts(n):
    return lambda ins: [lambda frm, to, c, r=r: r.at[frm] for r in ins[n:]]


def _scatter_start(hbs, name):
    n = len(hbs)
    lands = [lax.empty(h.shape, h.dtype) for h in hbs]
    return _split_start(_scatter_srcs(n), _scatter_dsts(n), list(hbs) + lands, n, name)


def _scatter_wait(send_sems, recv_sems, arrays, after, name):
    n = len(arrays) // 2
    return _split_wait(_scatter_srcs(n), _scatter_dsts(n), send_sems, recv_sems, arrays, after, name)[n:]


def _join_halves(rhs, name):
    n = len(rhs)
    any_spec = pl.BlockSpec(memory_space=pl.ANY)

    def body(*refs):
        outs = refs[n:2 * n]
        send_sems, recv_sems = refs[2 * n:]
        x, y, c, _ = _place()
        copies = []
        for a in range(n):
            copies.append(pltpu.make_async_remote_copy(
                src_ref=outs[a].at[c], dst_ref=outs[a].at[c], send_sem=send_sems.at[a],
                recv_sem=recv_sems.at[a], device_id=(x, y, 1 - c), device_id_type=MESH))
        for cp in copies:
            cp.start()
        for a in range(n):
            landed = outs[a].at[1 - c]
            pltpu.make_async_remote_copy(
                src_ref=landed, dst_ref=landed, send_sem=send_sems.at[a], recv_sem=recv_sems.at[a],
                device_id=(x, y, c), device_id_type=MESH).wait_recv()
        for cp in copies:
            cp.wait_send()

    out = pl.pallas_call(
        body, name=name,
        in_specs=[any_spec] * n, out_specs=[any_spec] * n,
        out_shape=[jax.ShapeDtypeStruct(r.shape, r.dtype) for r in rhs],
        input_output_aliases={a: a for a in range(n)},
        scratch_shapes=[pltpu.SemaphoreType.DMA((n,)), pltpu.SemaphoreType.DMA((n,))],
    )(*rhs)
    return [_merge2(o) for o in out]


FLIPS = [(dx, dy, dc) for dx in (0, 1) for dy in (0, 1) for dc in (0, 1)][1:]


def _peer_copies(ins, send_sems, recv_sems, started):
    x, y, c, _ = _place()
    copies = []
    for k, (dx, dy, dc) in enumerate(FLIPS):
        px, py, pc = jnp.bitwise_xor(x, dx), jnp.bitwise_xor(y, dy), jnp.bitwise_xor(c, dc)
        slot = 4 * x + 2 * y + c if started else 4 * px + 2 * py + pc
        copies.append(pltpu.make_async_remote_copy(
            src_ref=ins[0], dst_ref=ins[1].at[slot], send_sem=send_sems.at[k], recv_sem=recv_sems.at[k],
            device_id=(px, py, pc), device_id_type=MESH))
    return copies


def _small_start(v, name):
    land = lax.empty((8,) + v.shape, v.dtype)
    return _split_start(None, None, [v, land], 0, name, copies_of=_peer_copies, n_sems=len(FLIPS))


def _small_wait(send_sems, recv_sems, arrays, after, name):
    return _split_wait(None, None, send_sems, recv_sems, arrays, after, name, copies_of=_peer_copies)[1]


def _sum_slots(land, v, me):
    rows, cols = v.shape

    def body(me_ref, land_ref, v_ref, o_ref):
        o_ref[...] = jnp.zeros_like(o_ref)
        for d in range(8):
            @pl.when(me_ref[0] == d)
            def _():
                o_ref[...] += v_ref[...]

            @pl.when(me_ref[0] != d)
            def _():
                o_ref[...] += land_ref[d]

    return pl.pallas_call(
        body, name="sum_slots",
        grid_spec=pltpu.PrefetchScalarGridSpec(
            num_scalar_prefetch=1, grid=(1,),
            in_specs=[pl.BlockSpec((8, rows, cols), lambda i, me_ref: (0, 0, 0)),
                      pl.BlockSpec((rows, cols), lambda i, me_ref: (0, 0))],
            out_specs=pl.BlockSpec((rows, cols), lambda i, me_ref: (0, 0))),
        out_shape=jax.ShapeDtypeStruct((rows, cols), F32),
        compiler_params=_cparams(("arbitrary",)),
    )(me, land, v)


def _row_block(r):
    for cand in (512, 352, 256, 128, 64, 48, 16):
        if r % cand == 0:
            return cand
    return r


def _add_sibling_half(g, sb, core, name):
    ns, r, c = g.shape
    r2 = r // 2

    def body(core_ref, g_ref, sb_ref, h_ref, hb_ref):
        h = g_ref[...] + sb_ref[...].astype(F32)
        h_ref[...] = h
        hb_ref[...] = h.astype(BF16)

    spec = pl.BlockSpec((None, r2, c), lambda s, core_ref: (s, 0, 0))
    return pl.pallas_call(
        body, name=name,
        grid_spec=pltpu.PrefetchScalarGridSpec(
            num_scalar_prefetch=1, grid=(ns,),
            in_specs=[pl.BlockSpec((None, r2, c), lambda s, core_ref: (s, core_ref[0], 0)), spec],
            out_specs=[spec, spec]),
        out_shape=[jax.ShapeDtypeStruct((ns, r2, c), F32), jax.ShapeDtypeStruct((ns, r2, c), BF16)],
        compiler_params=_cparams(("arbitrary",)),
    )(core, g, sb)


def _add_chip_slabs(h, rb, idx, name):
    ns, r2, c = h.shape

    def body(idx_ref, h_ref, r0_ref, r1_ref, r2_ref, o_ref):
        o_ref[...] = ((h_ref[...] + r0_ref[...].astype(F32)) + r1_ref[...].astype(F32)) + r2_ref[...].astype(F32)

    def pick(k):
        return pl.BlockSpec((None, r2, c), lambda i, idx_ref: (idx_ref[k], 0, 0))

    return pl.pallas_call(
        body, name=name,
        grid_spec=pltpu.PrefetchScalarGridSpec(
            num_scalar_prefetch=1, grid=(1,),
            in_specs=[pick(0), pick(1), pick(2), pick(3)],
            out_specs=pl.BlockSpec((None, r2, c), lambda i, idx_ref: (idx_ref[4], 0, 0))),
        out_shape=jax.ShapeDtypeStruct((2, r2, c), F32),
        compiler_params=_cparams(("arbitrary",)),
    )(idx, h, rb, rb, rb)


def _adamw(g, w, m, v, name):
    r, c = g.shape
    br = _row_block(r)

    def body(g_ref, w_ref, m_ref, v_ref, go_ref, d_ref, nm_ref, nv_ref):
        gg = g_ref[...]
        go_ref[...] = gg
        nm = B1 * m_ref[...] + (1.0 - B1) * gg
        nv = B2 * v_ref[...] + (1.0 - B2) * jnp.square(gg)
        m_hat = nm / (1.0 - B1 ** STEP)
        v_hat = nv / (1.0 - B2 ** STEP)
        d_ref[...] = -LR * (m_hat / (jnp.sqrt(v_hat) + ADAM_EPS) + WD * w_ref[...])
        nm_ref[...] = nm
        nv_ref[...] = nv

    spec = pl.BlockSpec((br, c), lambda i: (i, 0))
    return pl.pallas_call(
        body, name=name,
        grid=(r // br,),
        in_specs=[spec] * 4, out_specs=[spec] * 4,
        out_shape=[jax.ShapeDtypeStruct((r, c), F32)] * 4,
        compiler_params=_cparams(("arbitrary",)),
    )(g, w, m, v)


BIG = ("w_in", "w_conv_out", "w_pool", "w_pool_out", "w_o", "w_ffn_gate", "w_ffn_up", "w_ffn_down")
REPL = ("g_mix", "b_gate", "b_dw", "ln_g", "ln_b", "pool_scale", "g_ffn", "g_final")
GROUP_MIX = ("w_conv_out", "w_pool", "w_pool_out", "w_o")
GROUP_FFN = ("w_ffn_gate", "w_ffn_up", "w_ffn_down")
TRANSPOSED = ("w_ffn_gate", "w_ffn_up")
WEIGHT_ORDER = ("meta_tokens", "g_mix", "w_in", "b_gate", "w_dw", "b_dw", "ln_g", "ln_b", "w_conv_out", "w_pool",
                "pool_scale", "w_pool_out", "w_o", "g_ffn", "w_ffn_gate", "w_ffn_up", "w_ffn_down", "g_final")


def _shard2d(name, a):
    a = a[0]
    if name == "w_pool":
        return a.reshape(4 * 64, GD)
    if name in TRANSPOSED:
        return a.T
    return a


def _unshard2d(name, a, shape):
    return a.T.reshape(shape) if name in TRANSPOSED else a.reshape(shape)


def _cols_to_slabs(a):
    m, n = a.shape
    return a.reshape(m, N_SHARD, n // N_SHARD).transpose(1, 0, 2)


def _slabs_to_cols(a):
    ns, m, c = a.shape
    return a.transpose(1, 0, 2).reshape(m, ns * c)


def kernel(x, meta_tokens, g_mix, w_in, b_gate, w_dw, b_dw, ln_g, ln_b, w_conv_out, w_pool, pool_scale, w_pool_out, w_o, g_ffn, w_ffn_gate, w_ffn_up, w_ffn_down, g_final, loss_target, m_meta_tokens, m_g_mix, m_w_in, m_b_gate, m_w_dw, m_b_dw, m_ln_g, m_ln_b, m_w_conv_out, m_w_pool, m_pool_scale, m_w_pool_out, m_w_o, m_g_ffn, m_w_ffn_gate, m_w_ffn_up, m_w_ffn_down, m_g_final, v_meta_tokens, v_g_mix, v_w_in, v_b_gate, v_w_dw, v_b_dw, v_ln_g, v_ln_b, v_w_conv_out, v_w_pool, v_pool_scale, v_w_pool_out, v_w_o, v_g_ffn, v_w_ffn_gate, v_w_ffn_up, v_w_ffn_down, v_g_final):
    args = dict(locals())
    w = {n: args[n] for n in WEIGHT_ORDER}
    mom = {n: args["m_" + n] for n in WEIGHT_ORDER}
    var = {n: args["v_" + n] for n in WEIGHT_ORDER}
    seq = x.shape[1]
    nb = seq // BR + 1
    tp = nb * BR
    tk = tp // 2 if (tp // 2) % 16 == 0 else BR
    t_total = seq + N_META
    cx, cy, cc = lax.axis_index("x"), lax.axis_index("y"), lax.axis_index("c")
    chip = 2 * cx + cy
    chip1 = jnp.reshape(chip, (1,)).astype(jnp.int32)
    core = jnp.reshape(cc, (1,)).astype(jnp.int32)
    others = jnp.sort(jnp.stack([2 * (1 - cx) + cy, 2 * cx + (1 - cy), 2 * (1 - cx) + (1 - cy)]))
    idx = jnp.concatenate([chip1, others.astype(jnp.int32), core])
    xs, target = x[0], loss_target[0]

    tiny = _gather_tiny(jnp.concatenate([w["meta_tokens"], w["w_dw"][0], jnp.zeros((1, GD), F32)], axis=0))
    small = {n: w[n] for n in REPL if n != "g_final"}
    small["g_final"] = w["g_final"].reshape(1, D)
    small["w_dw"] = _slabs_to_cols(tiny[:, N_META:])
    head = jnp.concatenate([jnp.zeros((PAD, D), F32), _slabs_to_cols(tiny[:, :N_META])], axis=0)

    def gather_start(group, dep, name):
        bufs = [_cast_into_slot(_shard2d(n, w[n]), chip1, dep, "cast_" + n) for n in group]
        return _gather_start(bufs, "gather_start_" + name)

    def gather_finish(group, start, after, name):
        landed = _gather_wait(start[0], start[1], start[2], [after], "gather_wait_" + name)
        return dict(zip(group, _forward_halves(landed, "forward_" + name)))

    st_in = gather_start(("w_in",), tiny, "in")
    u = _rms_u(head, xs, small["g_mix"], nb)
    gw = gather_finish(("w_in",), st_in, u, "in")
    st_mix = gather_start(GROUP_MIX, gw["w_in"], "mix")
    z = _in_proj(u, gw["w_in"], st_mix[3], nb)
    gw.update(gather_finish(GROUP_MIX, st_mix, z, "mix"))
    st_ffn = gather_start(GROUP_FFN, gw["w_o"], "ffn")
    w_pool_b = gw["w_pool"].reshape(N_SHARD, 4, 64, GD).transpose(1, 0, 2, 3).reshape(4, GD, GD)
    w_co_b, w_po_b, w_o_b = (gw[n].reshape(D, D) for n in ("w_conv_out", "w_pool_out", "w_o"))
    h1, yc, yp, mg, ca, cpre, m, mw, m2b = _mixers_fwd(
        z, head, xs, small["b_gate"] + st_ffn[3][0, 0], small["w_dw"], small["b_dw"], small["ln_g"], small["ln_b"],
        small["pool_scale"], w_co_b, w_pool_b, w_po_b, w_o_b, nb, t_total)
    gw.update(gather_finish(GROUP_FFN, st_ffn, h1, "ffn"))

    dh1, dh1b, vb, fb, dgb, dub, dh2b, loss, dg_ffn, dg_final = _ffn_fwd_bwd(
        h1, target, small["g_ffn"], small["g_final"], gw["w_ffn_gate"].reshape(D_FF, D),
        gw["w_ffn_up"].reshape(D_FF, D), gw["w_ffn_down"].reshape(D_FF, D), nb)

    def slabs(name, g):
        if name == "w_in":
            return g
        if name == "w_pool":
            return g.reshape(4, N_SHARD, 64, GD).transpose(1, 0, 2, 3).reshape(N_SHARD, 4 * 64, GD)
        return g.reshape(N_SHARD, g.shape[0] // N_SHARD, g.shape[1])

    def reduce_start(group, grads, name):
        g32 = [slabs(n, grads[n][0]) for n in group]
        g16 = [slabs(n, grads[n][1]) for n in group]
        from_sibling = _swap_halves_bf16(g16, "swap_halves_" + name)
        halves = [_add_sibling_half(g, sb, core, "add_sibling_" + n) for n, g, sb in zip(group, g32, from_sibling)]
        return [h for h, _ in halves], _scatter_start([hb for _, hb in halves], "scatter_start_" + name)

    def reduce_finish(group, halves, start, after, name):
        from_chips = _scatter_wait(start[0], start[1], start[2], after, "scatter_wait_" + name)
        return [_add_chip_slabs(h, rb, idx, "add_chips_" + n) for n, h, rb in zip(group, halves, from_chips)]

    half_ff = D_FF // 2
    grads_ffn = {
        "w_ffn_gate": _wgrad(dgb, vb, half_ff, D, tk, "wgrad_ffn_gate"),
        "w_ffn_up": _wgrad(dub, vb, half_ff, D, tk, "wgrad_ffn_up"),
        "w_ffn_down": _wgrad(fb, dh2b, half_ff, D, tk, "wgrad_ffn_down"),
    }
    halves_ffn, sc_ffn = reduce_start(GROUP_FFN, grads_ffn, "ffn")

    dycb, dypb, dzg, dconv, dmwb, dm, db_gate, dln_g, dln_b, db_dw, dps = _mixers_bwd_rows(
        dh1b, yc, yp, z, small["b_gate"], cpre, small["ln_g"], small["ln_b"], mw, small["pool_scale"],
        w_o_b, w_co_b, w_po_b, w_pool_b, sc_ffn[3], nb)
    grads_mix = {
        "w_conv_out": _wgrad(ca, dycb, D, D, tk, "wgrad_conv_out"),
        "w_pool": _wgrad(m, dmwb, GD, GD, tk, "wgrad_pool", diag=True),
        "w_pool_out": _wgrad(m2b, dypb, D, D, tk, "wgrad_pool_out"),
        "w_o": _wgrad(mg, dh1b, D, D, tk, "wgrad_o"),
    }
    halves_mix, sc_mix = reduce_start(GROUP_MIX, grads_mix, "mix")
    dzb, grad_x, dhead, dw_dw, dg_mix = _mixers_bwd_halo(
        dconv, dm, z, dzg, small["w_dw"], head, xs, small["g_mix"], dh1, gw["w_in"], sc_mix[3], nb, t_total)
    packed = jnp.concatenate(
        [dg_mix, db_gate.reshape(2, D), db_dw, dln_g, dln_b, dps, dg_ffn, dg_final,
         jnp.broadcast_to(loss, (1, D)), jnp.zeros((6, D), F32), dhead[PAD:], dw_dw], axis=0)
    sm = _small_start(packed, "small_start")
    grads_in = {"w_in": _wgrad(u, dzb, D, D_IN // N_SHARD, tk, "wgrad_in", col_major=True, dep=sm[3])}
    halves_in, sc_in = reduce_start(("w_in",), grads_in, "in")

    land = _small_wait(sm[0], sm[1], sm[2], [sc_in[3]], "small_wait")
    summed = _sum_slots(land, packed, jnp.reshape(4 * cx + 2 * cy + cc, (1,)).astype(jnp.int32))
    loss = summed[9, 0]

    first = GROUP_FFN + GROUP_MIX
    reduced_half = reduce_finish(GROUP_FFN, halves_ffn, sc_ffn, [summed], "ffn")
    reduced_half += reduce_finish(GROUP_MIX, halves_mix, sc_mix, [summed], "mix")
    reduced = dict(zip(first, _join_halves(reduced_half, "join_halves_first")))
    updates = {n: _adamw(reduced[n], _shard2d(n, w[n]), _shard2d(n, mom[n]), _shard2d(n, var[n]), "adamw_" + n)
               for n in first}

    def repl_stack(d):
        return jnp.concatenate([d["g_mix"], d["b_gate"].reshape(2, D), d["b_dw"], d["ln_g"], d["ln_b"],
                                d["pool_scale"], d["g_ffn"], d["g_final"].reshape(1, D), jnp.ones((7, D), F32)], axis=0)

    def shard_stack(d):
        return jnp.concatenate([d["meta_tokens"], d["w_dw"][0], jnp.ones((1, GD), F32)], axis=0)

    g_repl = summed[0:16]
    g_shard = lax.dynamic_slice_in_dim(summed[16:64], chip * GD, GD, axis=1)
    g_repl, d_repl, m_repl, v_repl = _adamw(g_repl, repl_stack(w), repl_stack(mom), repl_stack(var), "adamw_repl")
    g_shard, d_shard, m_shard, v_shard = _adamw(g_shard, shard_stack(w), shard_stack(mom), shard_stack(var), "adamw_cols")

    done_first = [updates[n][1] for n in first] + [d_repl, d_shard]
    last_half = reduce_finish(("w_in",), halves_in, sc_in, done_first, "in")
    reduced["w_in"] = _join_halves(last_half, "join_halves_in")[0]
    updates["w_in"] = _adamw(reduced["w_in"], w["w_in"][0], mom["w_in"][0], var["w_in"][0], "adamw_w_in")

    def unpack(name, repl, shard):
        if name == "meta_tokens":
            return shard[0:N_META]
        if name == "w_dw":
            return shard[N_META:N_META + KW].reshape(1, KW, GD)
        row = {"g_mix": 0, "b_gate": 1, "b_dw": 3, "ln_g": 4, "ln_b": 5, "pool_scale": 6, "g_ffn": 7, "g_final": 8}[name]
        if name == "b_gate":
            return repl[1:3].reshape(1, 2 * D)
        if name == "g_final":
            return repl[8]
        return repl[row:row + 1]

    out_g, out_d, out_m, out_v = {}, {}, {}, {}
    for n in WEIGHT_ORDER:
        if n in BIG:
            g, d_, m_, v_ = updates[n]
            shape = w[n].shape
            out_g[n], out_d[n], out_m[n], out_v[n] = (_unshard2d(n, a, shape) for a in (g, d_, m_, v_))
        else:
            out_g[n] = unpack(n, g_repl, g_shard)
            out_d[n] = unpack(n, d_repl, d_shard)
            out_m[n] = unpack(n, m_repl, m_shard)
            out_v[n] = unpack(n, v_repl, v_shard)
    return (loss, grad_x[None], *[out_g[n] for n in WEIGHT_ORDER], *[out_d[n] for n in WEIGHT_ORDER],
            *[out_m[n] for n in WEIGHT_ORDER], *[out_v[n] for n in WEIGHT_ORDER])
```

```python
import functools

import jax
import jax.numpy as jnp
from jax import lax
from jax.experimental import pallas as pl
from jax.experimental.pallas import tpu as pltpu

F32 = jnp.float32
BF16 = jnp.bfloat16
MESH = pl.DeviceIdType.MESH

D = 1024
N_META = 16
KW = 31
CPAD = KW // 2
POOL_WINDOWS = (2, 4, 8, 16)
GD = 256
D_IN = 5 * D
D_FF = 2816
N_SHARD = 4
BR = 256
HALO = 16
PAD = BR - N_META
EXT = BR + 2 * HALO
RMS_EPS = 1e-6
LN_EPS = 1e-5
LR, B1, B2, ADAM_EPS, WD, STEP = 0.001, 0.9, 0.999, 1e-08, 0.01, 10
VMEM_LIMIT = 56 * 1024 * 1024


def _cparams(sem, vmem=VMEM_LIMIT):
    return pltpu.CompilerParams(dimension_semantics=sem, vmem_limit_bytes=vmem)


def _dot(a, b):
    return jnp.dot(a, b, preferred_element_type=F32)


def _dot_nt(a, b):
    return lax.dot_general(a, b, (((1,), (1,)), ((), ())), preferred_element_type=F32)


def _dot_tn(a, b):
    return lax.dot_general(a, b, (((0,), (0,)), ((), ())), preferred_element_type=F32)


def _sigmoid(x):
    return 0.5 * jnp.tanh(0.5 * x) + 0.5


def _row_ids(i, n, offset=0):
    return lax.broadcasted_iota(jnp.int32, (n, 1), 0) + (i * BR + offset - PAD)


def _pool_cnt(t, w, t_total):
    left = w // 2
    right = w - 1 - left
    lo = jnp.clip(t - left, 0, t_total)
    hi = jnp.clip(t + right + 1, 0, t_total)
    return jnp.maximum(hi - lo, 1).astype(F32)


def _halo_specs(col, nb):
    last = nb * (BR // HALO) - 1
    return [
        pl.BlockSpec((HALO, D), lambda i: (jnp.maximum(i * (BR // HALO) - 1, 0), col)),
        pl.BlockSpec((BR, D), lambda i: (i, col)),
        pl.BlockSpec((HALO, D), lambda i: (jnp.minimum((i + 1) * (BR // HALO), last), col)),
    ]


def _fill_ext(ext_ref, prev, cur, nxt, i, nb):
    ext_ref[0:HALO, :] = jnp.where(i > 0, prev, 0.0)
    ext_ref[HALO:HALO + BR, :] = cur
    ext_ref[HALO + BR:EXT, :] = jnp.where(i < nb - 1, nxt, 0.0)


ROT_ROWS = EXT - 8


def _fill_rot(rot_ref, ext_ref, lanes):
    for r in range(1, 8):
        rot_ref[r] = ext_ref[pl.ds(r, ROT_ROWS), lanes]


def _tap(rot_ref, ext_ref, lanes, offset):
    q, r = divmod(offset, 8)
    if r == 0:
        return ext_ref[pl.ds(8 * q, BR), lanes]
    return rot_ref[r, pl.ds(8 * q, BR), :]


def _row_spec(width=D):
    return pl.BlockSpec((BR, width), lambda i: (i, 0))


def _x_spec():
    return pl.BlockSpec((BR, D), lambda i: (jnp.maximum(i - 1, 0), 0))


def _const_spec(shape):
    nd = len(shape)
    return pl.BlockSpec(shape, lambda i: (0,) * nd)


def _rms_u(head, x, g_mix, nb):
    def body(head_ref, x_ref, g_ref, u_ref):
        i = pl.program_id(0)
        h = jnp.where(i == 0, head_ref[...], x_ref[...])
        r = lax.rsqrt(jnp.mean(h * h, axis=-1, keepdims=True) + RMS_EPS)
        u_ref[...] = ((h * r) * g_ref[...]).astype(BF16)

    return pl.pallas_call(
        body, name="rms_u",
        grid=(nb,),
        in_specs=[_const_spec((BR, D)), _x_spec(), _const_spec((1, D))],
        out_specs=_row_spec(),
        out_shape=jax.ShapeDtypeStruct((nb * BR, D), BF16),
        compiler_params=_cparams(("arbitrary",)),
    )(head, x, g_mix)


def _in_proj(u, w_in_b, dep, nb):
    tp = nb * BR
    ns = w_in_b.shape[0]
    wcols = w_in_b.shape[2]
    rows = tp // 4 if (tp // 4) % 16 == 0 else BR

    def body(u_ref, w_ref, dep_ref, z_ref):
        z_ref[...] = _dot(u_ref[...], w_ref[...])

    return pl.pallas_call(
        body, name="in_proj",
        grid=(ns, tp // rows),
        in_specs=[
            pl.BlockSpec((rows, D), lambda s, i: (i, 0)),
            pl.BlockSpec((None, D, wcols), lambda s, i: (s, 0, 0)),
            pl.BlockSpec(memory_space=pl.ANY),
        ],
        out_specs=pl.BlockSpec((rows, wcols), lambda s, i: (i, s)),
        out_shape=jax.ShapeDtypeStruct((tp, ns * wcols), F32),
        compiler_params=_cparams(("arbitrary", "arbitrary")),
    )(u, w_in_b, dep)


def _mixers_fwd(z, head, x, b_gate, w_dw, b_dw, ln_g, ln_b, pool_scale, w_co, w_pool, w_po, w_o, nb, t_total):
    tp = nb * BR

    def body(avp, av, avn, agp, ag, agn, pp, pc, pn, za, zb, head_ref, x_ref, bg_ref, wdw_ref, bdw_ref,
             lng_ref, lnb_ref, ps_ref, wco_ref, wpool_ref, wpo_ref, wo_ref,
             h1_ref, yc_ref, yp_ref, mg_ref, ca_ref, cpre_ref, m_ref, mw_ref, m2b_ref, ext_ref, pext_ref, rot_ref):
        i = pl.program_id(0)
        _fill_ext(ext_ref, avp[...] * _sigmoid(agp[...]), av[...] * _sigmoid(ag[...]),
                  avn[...] * _sigmoid(agn[...]), i, nb)
        _fill_ext(pext_ref, pp[...], pc[...], pn[...], i, nb)

        def conv_chunk(c, carry):
            lanes = pl.ds(pl.multiple_of(c * 128, 128), 128)
            _fill_rot(rot_ref, ext_ref, lanes)
            acc = jnp.broadcast_to(bdw_ref[:, lanes], (BR, 128))
            for k in range(KW):
                acc = acc + wdw_ref[k:k + 1, lanes] * _tap(rot_ref, ext_ref, lanes, 1 + k)
            cpre_ref[:, lanes] = acc
            return carry
        lax.fori_loop(0, D // 128, conv_chunk, 0)

        conv = cpre_ref[...]
        mu = jnp.mean(conv, axis=-1, keepdims=True)
        xc = conv - mu
        rstd = lax.rsqrt(jnp.mean(xc * xc, axis=-1, keepdims=True) + LN_EPS)
        ln = (xc * rstd) * lng_ref[...] + lnb_ref[...]
        cact = (ln * _sigmoid(ln)).astype(BF16)
        ca_ref[...] = cact
        y_conv = _dot(cact, wco_ref[...])
        yc_ref[...] = y_conv

        t = _row_ids(i, BR)
        for gi, w in enumerate(POOL_WINDOWS):
            left = w // 2
            right = w - 1 - left
            lanes = slice(gi * GD, (gi + 1) * GD)
            s = pext_ref[pl.ds(HALO - left, BR), lanes]
            for j in range(-left + 1, right + 1):
                s = s + pext_ref[pl.ds(HALO + j, BR), lanes]
            m = (s / _pool_cnt(t, w, t_total) - pext_ref[HALO:HALO + BR, lanes]).astype(BF16)
            m_ref[:, lanes] = m
            mw_ref[:, lanes] = _dot(m, wpool_ref[gi])
        mw = mw_ref[...]
        m2b = (mw * ps_ref[...]).astype(BF16)
        m2b_ref[...] = m2b
        y_pool = _dot(m2b, wpo_ref[...])
        yp_ref[...] = y_pool

        s_a = _sigmoid(za[...] + bg_ref[:, 0:D])
        s_b = _sigmoid(zb[...] + bg_ref[:, D:2 * D])
        merged = (s_a * y_conv + s_b * y_pool).astype(BF16)
        mg_ref[...] = merged
        h0 = jnp.where(i == 0, head_ref[...], x_ref[...])
        h1_ref[...] = h0 + _dot(merged, wo_ref[...])

    in_specs = (_halo_specs(0, nb) + _halo_specs(1, nb) + _halo_specs(2, nb)
                + [pl.BlockSpec((BR, D), lambda i: (i, 3)), pl.BlockSpec((BR, D), lambda i: (i, 4)),
                   _const_spec((BR, D)), _x_spec(), _const_spec((1, 2 * D)), _const_spec((32, D)),
                   _const_spec((1, D)), _const_spec((1, D)), _const_spec((1, D)), _const_spec((1, D)),
                   _const_spec((D, D)), _const_spec((4, GD, GD)), _const_spec((D, D)), _const_spec((D, D))])
    outs = [(F32, "h1"), (F32, "yc"), (F32, "yp"), (BF16, "mg"), (BF16, "ca"), (F32, "cpre"), (BF16, "m"), (F32, "mw"),
            (BF16, "m2b")]
    return pl.pallas_call(
        body, name="mixers_fwd",
        grid=(nb,),
        in_specs=in_specs,
        out_specs=[_row_spec() for _ in outs],
        out_shape=[jax.ShapeDtypeStruct((tp, D), dt) for dt, _ in outs],
        scratch_shapes=[pltpu.VMEM((EXT, D), F32), pltpu.VMEM((EXT, D), F32), pltpu.VMEM((8, ROT_ROWS, 128), F32)],
        compiler_params=_cparams(("arbitrary",)),
    )(z, z, z, z, z, z, z, z, z, z, z, head, x, b_gate, w_dw, b_dw, ln_g, ln_b, pool_scale, w_co, w_pool, w_po, w_o)


def _ffn_fwd_bwd(h1, target, g_ffn, g_final, w_g, w_u, w_d, nb):
    tp = nb * BR

    def body(h1_ref, tgt_ref, gf_ref, gfin_ref, wg_hbm, wu_hbm, wd_hbm,
             dh1_ref, dh1b_ref, vb_ref, fb_ref, dgb_ref, dub_ref, dh2b_ref, loss_ref, dgf_ref, dgfin_ref,
             wg_ref, wu_ref, wd_ref, sem):
        i = pl.program_id(0)

        @pl.when(i == 0)
        def _():
            copies = [pltpu.make_async_copy(wg_hbm, wg_ref, sem.at[0]),
                      pltpu.make_async_copy(wu_hbm, wu_ref, sem.at[1]),
                      pltpu.make_async_copy(wd_hbm, wd_ref, sem.at[2])]
            for cp in copies:
                cp.start()
            loss_ref[...] = jnp.zeros_like(loss_ref)
            dgf_ref[...] = jnp.zeros_like(dgf_ref)
            dgfin_ref[...] = jnp.zeros_like(dgfin_ref)
            for cp in copies:
                cp.wait()

        h1 = h1_ref[...]
        r1 = lax.rsqrt(jnp.mean(h1 * h1, axis=-1, keepdims=True) + RMS_EPS)
        vn = h1 * r1
        vb = (vn * gf_ref[...]).astype(BF16)
        vb_ref[...] = vb
        g = _dot_nt(vb, wg_ref[...])
        up = _dot_nt(vb, wu_ref[...])
        sg = _sigmoid(g)
        sl = g * sg
        fb = (sl * up).astype(BF16)
        fb_ref[...] = fb
        h2 = h1 + _dot(fb, wd_ref[...])
        r2 = lax.rsqrt(jnp.mean(h2 * h2, axis=-1, keepdims=True) + RMS_EPS)
        yn = h2 * r2
        valid = i > 0
        diff = jnp.where(valid, yn * gfin_ref[...] - tgt_ref[...], 0.0)
        loss_ref[...] += 0.5 * jnp.sum(jnp.mean(diff * diff, axis=-1, keepdims=True))
        dy = diff * (1.0 / D)
        dgfin_ref[...] += jnp.sum(dy * yn, axis=0, keepdims=True)
        gd = dy * gfin_ref[...]
        dh2 = r2 * (gd - yn * jnp.mean(yn * gd, axis=-1, keepdims=True))
        dh2b = dh2.astype(BF16)
        dh2b_ref[...] = dh2b
        df = _dot_nt(dh2b, wd_ref[...])
        dub = (df * sl).astype(BF16)
        dgb = (df * up * (sg * (1.0 + g * (1.0 - sg)))).astype(BF16)
        dub_ref[...] = dub
        dgb_ref[...] = dgb
        dv = _dot(dgb, wg_ref[...]) + _dot(dub, wu_ref[...])
        dgf_ref[...] += jnp.sum(dv * vn, axis=0, keepdims=True)
        gd1 = dv * gf_ref[...]
        dh1 = dh2 + r1 * (gd1 - vn * jnp.mean(vn * gd1, axis=-1, keepdims=True))
        dh1_ref[...] = dh1
        dh1b_ref[...] = dh1.astype(BF16)

    any_spec = pl.BlockSpec(memory_space=pl.ANY)
    return pl.pallas_call(
        body, name="ffn_fwd_bwd",
        grid=(nb,),
        in_specs=[_row_spec(), _x_spec(), _const_spec((1, D)), _const_spec((1, D)), any_spec, any_spec, any_spec],
        out_specs=[_row_spec(), _row_spec(), _row_spec(), _row_spec(D_FF), _row_spec(D_FF), _row_spec(D_FF), _row_spec(),
                   _const_spec((1, 1)), _const_spec((1, D)), _const_spec((1, D))],
        out_shape=[jax.ShapeDtypeStruct((tp, D), F32), jax.ShapeDtypeStruct((tp, D), BF16),
                   jax.ShapeDtypeStruct((tp, D), BF16), jax.ShapeDtypeStruct((tp, D_FF), BF16),
                   jax.ShapeDtypeStruct((tp, D_FF), BF16), jax.ShapeDtypeStruct((tp, D_FF), BF16),
                   jax.ShapeDtypeStruct((tp, D), BF16), jax.ShapeDtypeStruct((1, 1), F32),
                   jax.ShapeDtypeStruct((1, D), F32), jax.ShapeDtypeStruct((1, D), F32)],
        scratch_shapes=[pltpu.VMEM((D_FF, D), BF16), pltpu.VMEM((D_FF, D), BF16), pltpu.VMEM((D_FF, D), BF16),
                        pltpu.SemaphoreType.DMA((3,))],
        compiler_params=_cparams(("arbitrary",)),
    )(h1, target, g_ffn, g_final, w_g, w_u, w_d)


def _mixers_bwd_rows(dh1b, yc, yp, z, b_gate, cpre, ln_g, ln_b, mw, pool_scale, w_o, w_co, w_po, w_pool, dep, nb):
    tp = nb * BR

    def body(dh1b_ref, yc_ref, yp_ref, za, zb, bg_ref, cpre_ref, lng_ref, lnb_ref, mw_ref, ps_ref,
             wo_ref, wco_ref, wpo_ref, wpool_ref, dep_ref,
             dycb_ref, dypb_ref, dzg_ref, dconv_ref, dmwb_ref, dm_ref, dbg_ref, dlng_ref, dlnb_ref, dbdw_ref, dps_ref):
        i = pl.program_id(0)

        @pl.when(i == 0)
        def _():
            for r in (dbg_ref, dlng_ref, dlnb_ref, dbdw_ref, dps_ref):
                r[...] = jnp.zeros_like(r)

        dmg = _dot_nt(dh1b_ref[...], wo_ref[...])
        s_a = _sigmoid(za[...] + bg_ref[:, 0:D])
        s_b = _sigmoid(zb[...] + bg_ref[:, D:2 * D])
        dycb = (dmg * s_a).astype(BF16)
        dypb = (dmg * s_b).astype(BF16)
        dycb_ref[...] = dycb
        dypb_ref[...] = dypb
        dza = dmg * yc_ref[...] * (s_a * (1.0 - s_a))
        dzb = dmg * yp_ref[...] * (s_b * (1.0 - s_b))
        dzg_ref[:, 0:D] = dza.astype(BF16)
        dzg_ref[:, D:2 * D] = dzb.astype(BF16)
        dbg_ref[:, 0:D] += jnp.sum(dza, axis=0, keepdims=True)
        dbg_ref[:, D:2 * D] += jnp.sum(dzb, axis=0, keepdims=True)

        dca = _dot_nt(dycb, wco_ref[...])
        conv = cpre_ref[...]
        mu = jnp.mean(conv, axis=-1, keepdims=True)
        xc = conv - mu
        rstd = lax.rsqrt(jnp.mean(xc * xc, axis=-1, keepdims=True) + LN_EPS)
        xhat = xc * rstd
        ln = xhat * lng_ref[...] + lnb_ref[...]
        sg = _sigmoid(ln)
        dln = dca * (sg * (1.0 + ln * (1.0 - sg)))
        dlng_ref[...] += jnp.sum(dln * xhat, axis=0, keepdims=True)
        dlnb_ref[...] += jnp.sum(dln, axis=0, keepdims=True)
        dxh = dln * lng_ref[...]
        dconv = rstd * (dxh - jnp.mean(dxh, axis=-1, keepdims=True)
                        - xhat * jnp.mean(dxh * xhat, axis=-1, keepdims=True))
        dconv_ref[...] = dconv
        dbdw_ref[...] += jnp.sum(dconv, axis=0, keepdims=True)

        dm2 = _dot_nt(dypb, wpo_ref[...])
        dps_ref[...] += jnp.sum(dm2 * mw_ref[...], axis=0, keepdims=True)
        dmwb = (dm2 * ps_ref[...]).astype(BF16)
        dmwb_ref[...] = dmwb
        for gi in range(len(POOL_WINDOWS)):
            lanes = slice(gi * GD, (gi + 1) * GD)
            dm_ref[:, lanes] = _dot_nt(dmwb[:, lanes], wpool_ref[gi])

    in_specs = [_row_spec(), _row_spec(), _row_spec(),
                pl.BlockSpec((BR, D), lambda i: (i, 3)), pl.BlockSpec((BR, D), lambda i: (i, 4)),
                _const_spec((1, 2 * D)), _row_spec(), _const_spec((1, D)), _const_spec((1, D)), _row_spec(),
                _const_spec((1, D)), _const_spec((D, D)), _const_spec((D, D)), _const_spec((D, D)),
                _const_spec((4, GD, GD)), pl.BlockSpec(memory_space=pl.ANY)]
    return pl.pallas_call(
        body, name="mixers_bwd_rows",
        grid=(nb,),
        in_specs=in_specs,
        out_specs=[_row_spec(), _row_spec(), _row_spec(2 * D), _row_spec(), _row_spec(), _row_spec(),
                   _const_spec((1, 2 * D)), _const_spec((1, D)), _const_spec((1, D)), _const_spec((1, D)),
                   _const_spec((1, D))],
        out_shape=[jax.ShapeDtypeStruct((tp, D), BF16), jax.ShapeDtypeStruct((tp, D), BF16),
                   jax.ShapeDtypeStruct((tp, 2 * D), BF16), jax.ShapeDtypeStruct((tp, D), F32),
                   jax.ShapeDtypeStruct((tp, D), BF16), jax.ShapeDtypeStruct((tp, D), F32),
                   jax.ShapeDtypeStruct((1, 2 * D), F32), jax.ShapeDtypeStruct((1, D), F32),
                   jax.ShapeDtypeStruct((1, D), F32), jax.ShapeDtypeStruct((1, D), F32),
                   jax.ShapeDtypeStruct((1, D), F32)],
        compiler_params=_cparams(("arbitrary",)),
    )(dh1b, yc, yp, z, z, b_gate, cpre, ln_g, ln_b, mw, pool_scale, w_o, w_co, w_po, w_pool, dep)


def _mixers_bwd_halo(dconv, dm, z, dzg, w_dw, head, x, g_mix, dh1, w_in_b, dep, nb, t_total):
    tp = nb * BR
    ns = w_in_b.shape[0]
    wcols = w_in_b.shape[2]
    seq = x.shape[0]

    def body(dcp, dcc, dcn, dmp, dmc, dmn, avp, av, avn, agp, ag, agn, dzg_ref, wdw_ref, head_ref, x_ref, g_ref,
             dh1_ref, w_hbm, dep_ref,
             dzb_ref, gx_ref, dhead_ref, dwdw_ref, dgmix_ref,
             w_ref, sem, aext_ref, dext_ref, qext_ref, da_ref, rot_ref, dwp_ref):
        i = pl.program_id(0)

        @pl.when(i == 0)
        def _():
            cp = pltpu.make_async_copy(w_hbm, w_ref, sem.at[0])
            cp.start()
            dwp_ref[...] = jnp.zeros_like(dwp_ref)
            dgmix_ref[...] = jnp.zeros_like(dgmix_ref)
            cp.wait()

        sig_g = _sigmoid(ag[...])
        _fill_ext(aext_ref, avp[...] * _sigmoid(agp[...]), av[...] * sig_g, avn[...] * _sigmoid(agn[...]), i, nb)
        _fill_ext(dext_ref, dcp[...], dcc[...], dcn[...], i, nb)
        _fill_ext(qext_ref, dmp[...], dmc[...], dmn[...], i, nb)

        def conv_chunk(c, carry):
            lanes = pl.ds(pl.multiple_of(c * 128, 128), 128)
            _fill_rot(rot_ref, dext_ref, lanes)
            acc = jnp.zeros((BR, 128), F32)
            for k in range(KW):
                acc = acc + wdw_ref[k:k + 1, lanes] * _tap(rot_ref, dext_ref, lanes, KW - k)
            da_ref[:, lanes] = acc
            _fill_rot(rot_ref, aext_ref, lanes)
            dcv = dext_ref[HALO:HALO + BR, lanes]
            for k in range(KW):
                prod = _tap(rot_ref, aext_ref, lanes, 1 + k) * dcv
                dwp_ref[k, :, lanes] += jnp.sum(prod.reshape(BR // 8, 8, 128), axis=0)
            return carry
        lax.fori_loop(0, D // 128, conv_chunk, 0)

        @pl.when(i == nb - 1)
        def _():
            dwdw_ref[...] = jnp.sum(dwp_ref[...], axis=1)

        da = da_ref[...]
        a_val = av[...]
        dzb_ref[:, 0:D] = (da * sig_g).astype(BF16)
        dzb_ref[:, D:2 * D] = (da * a_val * (sig_g * (1.0 - sig_g))).astype(BF16)

        t_ext = _row_ids(i, EXT, -HALO)
        for gi, w in enumerate(POOL_WINDOWS):
            left = w // 2
            right = w - 1 - left
            lanes = slice(gi * GD, (gi + 1) * GD)
            qext_ref[:, lanes] = qext_ref[:, lanes] / _pool_cnt(t_ext, w, t_total)
            s = qext_ref[pl.ds(HALO - right, BR), lanes]
            for j in range(-right + 1, left + 1):
                s = s + qext_ref[pl.ds(HALO + j, BR), lanes]
            dzb_ref[:, 2 * D + gi * GD:2 * D + (gi + 1) * GD] = (s - dmc[:, lanes]).astype(BF16)
        dzb_ref[:, 3 * D:5 * D] = dzg_ref[...]

        du = _dot_nt(dzb_ref[:, 0:wcols], w_ref[0])
        for s_i in range(1, ns):
            du = du + _dot_nt(dzb_ref[:, s_i * wcols:(s_i + 1) * wcols], w_ref[s_i])
        h0 = jnp.where(i == 0, head_ref[...], x_ref[...])
        r0 = lax.rsqrt(jnp.mean(h0 * h0, axis=-1, keepdims=True) + RMS_EPS)
        un = h0 * r0
        dgmix_ref[...] += jnp.sum(du * un, axis=0, keepdims=True)
        gd = du * g_ref[...]
        dh0 = dh1_ref[...] + r0 * (gd - un * jnp.mean(un * gd, axis=-1, keepdims=True))
        gx_ref[...] = dh0

        @pl.when(i == 0)
        def _():
            dhead_ref[...] = dh0

    any_spec = pl.BlockSpec(memory_space=pl.ANY)
    in_specs = (_halo_specs(0, nb) + _halo_specs(0, nb) + _halo_specs(0, nb) + _halo_specs(1, nb)
                + [_row_spec(2 * D), _const_spec((32, D)), _const_spec((BR, D)), _x_spec(), _const_spec((1, D)),
                   _row_spec(), any_spec, any_spec])
    return pl.pallas_call(
        body, name="mixers_bwd_halo",
        grid=(nb,),
        in_specs=in_specs,
        out_specs=[_row_spec(D_IN), _x_spec(), _const_spec((BR, D)), _const_spec((32, D)), _const_spec((1, D))],
        out_shape=[jax.ShapeDtypeStruct((tp, D_IN), BF16), jax.ShapeDtypeStruct((seq, D), F32),
                   jax.ShapeDtypeStruct((BR, D), F32), jax.ShapeDtypeStruct((32, D), F32),
                   jax.ShapeDtypeStruct((1, D), F32)],
        scratch_shapes=[pltpu.VMEM((ns, D, wcols), BF16), pltpu.SemaphoreType.DMA((1,)),
                        pltpu.VMEM((EXT, D), F32), pltpu.VMEM((EXT, D), F32), pltpu.VMEM((EXT, D), F32),
                        pltpu.VMEM((BR, D), F32), pltpu.VMEM((8, ROT_ROWS, 128), F32), pltpu.VMEM((32, 8, D), F32)],
        compiler_params=_cparams(("arbitrary",)),
    )(dconv, dconv, dconv, dm, dm, dm, z, z, z, z, z, z, dzg, w_dw, head, x, g_mix, dh1, w_in_b, dep)


def _wgrad(a, c, tm, tn, tk, name, diag=False, col_major=False, dep=None):
    tp, m = a.shape
    n = c.shape[1]
    nk = tp // tk
    gm, gn = m // tm, n // tn

    def body(a_ref, c_ref, *rest):
        o_ref, ob_ref = rest[-2:]
        k = pl.program_id(2)

        @pl.when(k == 0)
        def _():
            o_ref[...] = jnp.zeros_like(o_ref)

        o_ref[...] += _dot_tn(a_ref[...], c_ref[...])

        @pl.when(k == nk - 1)
        def _():
            ob_ref[...] = o_ref[...].astype(BF16)

    c_map = lambda i, j, k: (k, j)
    grid = (gm, gn, nk)
    deps = [] if dep is None else [dep]
    if diag:
        grid = (gm, 1, nk)
        c_map = lambda i, j, k: (k, i)
        o_spec = pl.BlockSpec((tm, tn), lambda i, j, k: (i, 0))
        o_shape = (m, tn)
    elif col_major:
        o_spec = pl.BlockSpec((None, tm, tn), lambda i, j, k: (j, i, 0))
        o_shape = (gn, m, tn)
    else:
        o_spec = pl.BlockSpec((tm, tn), lambda i, j, k: (i, j))
        o_shape = (m, n)
    return pl.pallas_call(
        body, name=name,
        grid=grid,
        in_specs=[pl.BlockSpec((tk, tm), lambda i, j, k: (k, i)), pl.BlockSpec((tk, tn), c_map)]
        + [pl.BlockSpec(memory_space=pl.ANY)] * len(deps),
        out_specs=[o_spec, o_spec],
        out_shape=[jax.ShapeDtypeStruct(o_shape, F32), jax.ShapeDtypeStruct(o_shape, BF16)],
        compiler_params=_cparams(("arbitrary", "arbitrary", "arbitrary")),
    )(a, c, *deps)


def _place():
    x, y, c = lax.axis_index("x"), lax.axis_index("y"), lax.axis_index("c")
    others = [(1 - x, y), (x, 1 - y), (1 - x, 1 - y)]
    return x, y, c, others


def _split2(a, axis=0):
    return a.reshape(a.shape[:axis] + (2, a.shape[axis] // 2) + a.shape[axis + 1:])


def _merge2(a, axis=0):
    return a.reshape(a.shape[:axis] + (2 * a.shape[axis + 1],) + a.shape[axis + 2:])


def _cast_into_slot(w2d, chip, dep, name):
    r, c = w2d.shape
    r2 = r // 2

    def body(chip_ref, w_ref, dep_ref, o_ref):
        o_ref[...] = w_ref[...].astype(BF16)

    return pl.pallas_call(
        body, name=name,
        grid_spec=pltpu.PrefetchScalarGridSpec(
            num_scalar_prefetch=1, grid=(2,),
            in_specs=[pl.BlockSpec((r2, c), lambda h, chip_ref: (h, 0)), pl.BlockSpec(memory_space=pl.ANY)],
            out_specs=pl.BlockSpec((None, None, r2, c), lambda h, chip_ref: (chip_ref[0], h, 0, 0))),
        out_shape=jax.ShapeDtypeStruct((N_SHARD, 2, r2, c), BF16),
        compiler_params=_cparams(("arbitrary",)),
    )(chip, w2d, dep)


HBM_SPEC = pl.BlockSpec(memory_space=pltpu.HBM)
SEM_SPEC = pl.BlockSpec(memory_space=pltpu.SEMAPHORE)
DATAFLOW = pltpu.SideEffectType.DATAFLOW_SIDE_EFFECTING
TOKEN = jax.ShapeDtypeStruct((8, 128), F32)


def _in_hbm(a):
    return pltpu.with_memory_space_constraint(a, pltpu.HBM)


def _gather_tiny(v):
    vm = pl.BlockSpec(memory_space=pltpu.VMEM)

    def body(v_ref, out_ref, send_sems, recv_sems):
        x, y, c, others = _place()
        mine = 2 * x + y
        sends = [pltpu.make_async_remote_copy(
            src_ref=v_ref, dst_ref=out_ref.at[mine], send_sem=send_sems.at[j], recv_sem=recv_sems.at[j],
            device_id=(*chip, c), device_id_type=MESH) for j, chip in enumerate(others)]
        for cp in sends:
            cp.start()
        out_ref[mine] = v_ref[...]
        for j, chip in enumerate(others):
            landed = out_ref.at[2 * chip[0] + chip[1]]
            pltpu.make_async_remote_copy(
                src_ref=landed, dst_ref=landed, send_sem=send_sems.at[j], recv_sem=recv_sems.at[j],
                device_id=(x, y, c), device_id_type=MESH).wait_recv()
        for cp in sends:
            cp.wait_send()

    return pl.pallas_call(
        body, name="gather_tiny",
        in_specs=[vm], out_specs=vm,
        out_shape=jax.ShapeDtypeStruct((N_SHARD,) + v.shape, v.dtype),
        scratch_shapes=[pltpu.SemaphoreType.DMA((3,)), pltpu.SemaphoreType.DMA((3,))],
    )(v)


def _ici_copies(srcs, dsts, send_sems, recv_sems, started):
    x, y, c, others = _place()
    mine = 2 * x + y
    copies = []
    for a in range(len(srcs)):
        for j, chip in enumerate(others):
            there = 2 * chip[0] + chip[1]
            src, dst = srcs[a](mine, there, c), dsts[a](mine, there, c)
            if not started:
                dst = dsts[a](there, mine, c)
            copies.append(pltpu.make_async_remote_copy(
                src_ref=src, dst_ref=dst, send_sem=send_sems.at[a * 3 + j], recv_sem=recv_sems.at[a * 3 + j],
                device_id=(*chip, c), device_id_type=MESH))
    return copies


def _split_start(srcs_of, dsts_of, arrays, n_src, name, copies_of=None, n_sems=None, dep=None):
    n = len(arrays)
    n_sems = n_sems or 3 * n_src
    copies_of = copies_of or (lambda ins, ss, rs, started: _ici_copies(srcs_of(ins), dsts_of(ins), ss, rs, started))
    deps = [] if dep is None else [dep]
    nd = len(deps)

    def body(*refs):
        ins = refs[:n]
        send_sems, recv_sems = refs[n + nd], refs[n + nd + 1]
        token = refs[2 * n + nd + 2]
        for cp in copies_of(ins, send_sems, recv_sems, True):
            cp.start()
        token[...] = jnp.zeros_like(token)

    out = pl.pallas_call(
        body, name=name,
        in_specs=[HBM_SPEC] * n + [pl.BlockSpec(memory_space=pl.ANY)] * nd,
        out_specs=(SEM_SPEC, SEM_SPEC, *([HBM_SPEC] * n), pl.BlockSpec(memory_space=pltpu.VMEM)),
        out_shape=(pltpu.SemaphoreType.DMA((n_sems,)), pltpu.SemaphoreType.DMA((n_sems,)),
                   *[pltpu.HBM(a.shape, a.dtype) for a in arrays], TOKEN),
        input_output_aliases={a: 2 + a for a in range(n)},
        compiler_params=pltpu.CompilerParams(has_side_effects=DATAFLOW),
    )(*[_in_hbm(a) for a in arrays], *deps)
    return out[0], out[1], list(out[2:2 + n]), out[2 + n]


def _split_wait(srcs_of, dsts_of, send_sems, recv_sems, arrays, after, name, copies_of=None):
    n = len(arrays)
    copies_of = copies_of or (lambda ins, ss, rs, started: _ici_copies(srcs_of(ins), dsts_of(ins), ss, rs, started))

    def body(*refs):
        ins = refs[:n]
        send_sems, recv_sems = refs[n], refs[n + 1]
        for cp in copies_of(ins, send_sems, recv_sems, False):
            cp.wait_send()
            cp.wait_recv()

    return pl.pallas_call(
        body, name=name,
        in_specs=[HBM_SPEC] * n + [SEM_SPEC, SEM_SPEC] + [pl.BlockSpec(memory_space=pl.ANY)] * len(after),
        out_specs=[HBM_SPEC] * n,
        out_shape=[pltpu.HBM(a.shape, a.dtype) for a in arrays],
        input_output_aliases={a: a for a in range(n)},
        compiler_params=pltpu.CompilerParams(has_side_effects=DATAFLOW),
    )(*arrays, send_sems, recv_sems, *after)


def _gather_views(ins):
    view = [lambda frm, to, c, r=r: r.at[frm, c] for r in ins]
    return view


def _gather_start(bufs, dep, name):
    return _split_start(_gather_views, _gather_views, bufs, len(bufs), name, dep=dep)


def _gather_wait(send_sems, recv_sems, bufs, after, name):
    return _split_wait(_gather_views, _gather_views, send_sems, recv_sems, bufs, after, name)


def _forward_halves(bufs, name):
    n = len(bufs)
    any_spec = pl.BlockSpec(memory_space=pl.ANY)

    def body(*refs):
        outs = refs[n:2 * n]
        send_sems, recv_sems = refs[2 * n:]
        x, y, c, others = _place()
        copies = []
        for a in range(n):
            for j, chip in enumerate(others):
                landed = outs[a].at[2 * chip[0] + chip[1], c]
                copies.append(pltpu.make_async_remote_copy(
                    src_ref=landed, dst_ref=landed, send_sem=send_sems.at[a * 3 + j], recv_sem=recv_sems.at[a * 3 + j],
                    device_id=(x, y, 1 - c), device_id_type=MESH))
        for cp in copies:
            cp.start()
        for a in range(n):
            for j, chip in enumerate(others):
                landed = outs[a].at[2 * chip[0] + chip[1], 1 - c]
                pltpu.make_async_remote_copy(
                    src_ref=landed, dst_ref=landed, send_sem=send_sems.at[a * 3 + j], recv_sem=recv_sems.at[a * 3 + j],
                    device_id=(x, y, c), device_id_type=MESH).wait_recv()
        for cp in copies:
            cp.wait_send()

    out = pl.pallas_call(
        body, name=name,
        in_specs=[any_spec] * n, out_specs=[any_spec] * n,
        out_shape=[jax.ShapeDtypeStruct(b.shape, b.dtype) for b in bufs],
        input_output_aliases={a: a for a in range(n)},
        scratch_shapes=[pltpu.SemaphoreType.DMA((3 * n,)), pltpu.SemaphoreType.DMA((3 * n,))],
    )(*bufs)
    return [_merge2(o, 1) for o in out]


def _swap_halves_bf16(gbs, name):
    n = len(gbs)
    any_spec = pl.BlockSpec(memory_space=pl.ANY)

    def body(*refs):
        ins, outs = refs[:n], refs[n:2 * n]
        send_sems, recv_sems = refs[2 * n:]
        x, y, c, _ = _place()
        copies = []
        for a in range(n):
            copies.append(pltpu.make_async_remote_copy(
                src_ref=ins[a].at[:, 1 - c], dst_ref=outs[a], send_sem=send_sems.at[a], recv_sem=recv_sems.at[a],
                device_id=(x, y, 1 - c), device_id_type=MESH))
        for cp in copies:
            cp.start()
        for cp in copies:
            cp.wait()

    return pl.pallas_call(
        body, name=name,
        in_specs=[any_spec] * n, out_specs=[any_spec] * n,
        out_shape=[jax.ShapeDtypeStruct((g.shape[0], g.shape[1] // 2, g.shape[2]), g.dtype) for g in gbs],
        scratch_shapes=[pltpu.SemaphoreType.DMA((n,)), pltpu.SemaphoreType.DMA((n,))],
    )(*[_split2(g, 1) for g in gbs])


def _scatter_srcs(n):
    return lambda ins: [lambda frm, to, c, r=r: r.at[to] for r in ins[:n]]


def _scatter_dsts(n):
    return lambda ins: [lambda frm, to, c, r=r: r.at[frm] for r in ins[n:]]


def _scatter_start(hbs, name):
    n = len(hbs)
    lands = [lax.empty(h.shape, h.dtype) for h in hbs]
    return _split_start(_scatter_srcs(n), _scatter_dsts(n), list(hbs) + lands, n, name)


def _scatter_wait(send_sems, recv_sems, arrays, after, name):
    n = len(arrays) // 2
    return _split_wait(_scatter_srcs(n), _scatter_dsts(n), send_sems, recv_sems, arrays, after, name)[n:]


def _join_halves(rhs, name):
    n = len(rhs)
    any_spec = pl.BlockSpec(memory_space=pl.ANY)

    def body(*refs):
        outs = refs[n:2 * n]
        send_sems, recv_sems = refs[2 * n:]
        x, y, c, _ = _place()
        copies = []
        for a in range(n):
            copies.append(pltpu.make_async_remote_copy(
                src_ref=outs[a].at[c], dst_ref=outs[a].at[c], send_sem=send_sems.at[a],
                recv_sem=recv_sems.at[a], device_id=(x, y, 1 - c), device_id_type=MESH))
        for cp in copies:
            cp.start()
        for a in range(n):
            landed = outs[a].at[1 - c]
            pltpu.make_async_remote_copy(
                src_ref=landed, dst_ref=landed, send_sem=send_sems.at[a], recv_sem=recv_sems.at[a],
                device_id=(x, y, c), device_id_type=MESH).wait_recv()
        for cp in copies:
            cp.wait_send()

    out = pl.pallas_call(
        body, name=name,
        in_specs=[any_spec] * n, out_specs=[any_spec] * n,
        out_shape=[jax.ShapeDtypeStruct(r.shape, r.dtype) for r in rhs],
        input_output_aliases={a: a for a in range(n)},
        scratch_shapes=[pltpu.SemaphoreType.DMA((n,)), pltpu.SemaphoreType.DMA((n,))],
    )(*rhs)
    return [_merge2(o) for o in out]


FLIPS = [(dx, dy, dc) for dx in (0, 1) for dy in (0, 1) for dc in (0, 1)][1:]


def _peer_copies(ins, send_sems, recv_sems, started):
    x, y, c, _ = _place()
    copies = []
    for k, (dx, dy, dc) in enumerate(FLIPS):
        px, py, pc = jnp.bitwise_xor(x, dx), jnp.bitwise_xor(y, dy), jnp.bitwise_xor(c, dc)
        slot = 4 * x + 2 * y + c if started else 4 * px + 2 * py + pc
        copies.append(pltpu.make_async_remote_copy(
            src_ref=ins[0], dst_ref=ins[1].at[slot], send_sem=send_sems.at[k], recv_sem=recv_sems.at[k],
            device_id=(px, py, pc), device_id_type=MESH))
    return copies


def _small_start(v, name):
    land = lax.empty((8,) + v.shape, v.dtype)
    return _split_start(None, None, [v, land], 0, name, copies_of=_peer_copies, n_sems=len(FLIPS))


def _small_wait(send_sems, recv_sems, arrays, after, name):
    return _split_wait(None, None, send_sems, recv_sems, arrays, after, name, copies_of=_peer_copies)[1]


def _sum_slots(land, v, me):
    rows, cols = v.shape

    def body(me_ref, land_ref, v_ref, o_ref):
        o_ref[...] = jnp.zeros_like(o_ref)
        for d in range(8):
            @pl.when(me_ref[0] == d)
            def _():
                o_ref[...] += v_ref[...]

            @pl.when(me_ref[0] != d)
            def _():
                o_ref[...] += land_ref[d]

    return pl.pallas_call(
        body, name="sum_slots",
        grid_spec=pltpu.PrefetchScalarGridSpec(
            num_scalar_prefetch=1, grid=(1,),
            in_specs=[pl.BlockSpec((8, rows, cols), lambda i, me_ref: (0, 0, 0)),
                      pl.BlockSpec((rows, cols), lambda i, me_ref: (0, 0))],
            out_specs=pl.BlockSpec((rows, cols), lambda i, me_ref: (0, 0))),
        out_shape=jax.ShapeDtypeStruct((rows, cols), F32),
        compiler_params=_cparams(("arbitrary",)),
    )(me, land, v)


def _row_block(r):
    for cand in (512, 352, 256, 128, 64, 48, 16):
        if r % cand == 0:
            return cand
    return r


def _add_sibling_half(g, sb, core, name):
    ns, r, c = g.shape
    r2 = r // 2

    def body(core_ref, g_ref, sb_ref, h_ref, hb_ref):
        h = g_ref[...] + sb_ref[...].astype(F32)
        h_ref[...] = h
        hb_ref[...] = h.astype(BF16)

    spec = pl.BlockSpec((None, r2, c), lambda s, core_ref: (s, 0, 0))
    return pl.pallas_call(
        body, name=name,
        grid_spec=pltpu.PrefetchScalarGridSpec(
            num_scalar_prefetch=1, grid=(ns,),
            in_specs=[pl.BlockSpec((None, r2, c), lambda s, core_ref: (s, core_ref[0], 0)), spec],
            out_specs=[spec, spec]),
        out_shape=[jax.ShapeDtypeStruct((ns, r2, c), F32), jax.ShapeDtypeStruct((ns, r2, c), BF16)],
        compiler_params=_cparams(("arbitrary",)),
    )(core, g, sb)


def _add_chip_slabs(h, rb, idx, name):
    ns, r2, c = h.shape

    def body(idx_ref, h_ref, r0_ref, r1_ref, r2_ref, o_ref):
        o_ref[...] = ((h_ref[...] + r0_ref[...].astype(F32)) + r1_ref[...].astype(F32)) + r2_ref[...].astype(F32)

    def pick(k):
        return pl.BlockSpec((None, r2, c), lambda i, idx_ref: (idx_ref[k], 0, 0))

    return pl.pallas_call(
        body, name=name,
        grid_spec=pltpu.PrefetchScalarGridSpec(
            num_scalar_prefetch=1, grid=(1,),
            in_specs=[pick(0), pick(1), pick(2), pick(3)],
            out_specs=pl.BlockSpec((None, r2, c), lambda i, idx_ref: (idx_ref[4], 0, 0))),
        out_shape=jax.ShapeDtypeStruct((2, r2, c), F32),
        compiler_params=_cparams(("arbitrary",)),
    )(idx, h, rb, rb, rb)


def _adamw(g, w, m, v, name):
    r, c = g.shape
    br = _row_block(r)

    def body(g_ref, w_ref, m_ref, v_ref, go_ref, d_ref, nm_ref, nv_ref):
        gg = g_ref[...]
        go_ref[...] = gg
        nm = B1 * m_ref[...] + (1.0 - B1) * gg
        nv = B2 * v_ref[...] + (1.0 - B2) * jnp.square(gg)
        m_hat = nm / (1.0 - B1 ** STEP)
        v_hat = nv / (1.0 - B2 ** STEP)
        d_ref[...] = -LR * (m_hat / (jnp.sqrt(v_hat) + ADAM_EPS) + WD * w_ref[...])
        nm_ref[...] = nm
        nv_ref[...] = nv

    spec = pl.BlockSpec((br, c), lambda i: (i, 0))
    return pl.pallas_call(
        body, name=name,
        grid=(r // br,),
        in_specs=[spec] * 4, out_specs=[spec] * 4,
        out_shape=[jax.ShapeDtypeStruct((r, c), F32)] * 4,
        compiler_params=_cparams(("arbitrary",)),
    )(g, w, m, v)


BIG = ("w_in", "w_conv_out", "w_pool", "w_pool_out", "w_o", "w_ffn_gate", "w_ffn_up", "w_ffn_down")
REPL = ("g_mix", "b_gate", "b_dw", "ln_g", "ln_b", "pool_scale", "g_ffn", "g_final")
GROUP_MIX = ("w_conv_out", "w_pool", "w_pool_out", "w_o")
GROUP_FFN = ("w_ffn_gate", "w_ffn_up", "w_ffn_down")
TRANSPOSED = ("w_ffn_gate", "w_ffn_up")
WEIGHT_ORDER = ("meta_tokens", "g_mix", "w_in", "b_gate", "w_dw", "b_dw", "ln_g", "ln_b", "w_conv_out", "w_pool",
                "pool_scale", "w_pool_out", "w_o", "g_ffn", "w_ffn_gate", "w_ffn_up", "w_ffn_down", "g_final")


def _shard2d(name, a):
    a = a[0]
    if name == "w_pool":
        return a.reshape(4 * 64, GD)
    if name in TRANSPOSED:
        return a.T
    return a


def _unshard2d(name, a, shape):
    return a.T.reshape(shape) if name in TRANSPOSED else a.reshape(shape)


def _cols_to_slabs(a):
    m, n = a.shape
    return a.reshape(m, N_SHARD, n // N_SHARD).transpose(1, 0, 2)


def _slabs_to_cols(a):
    ns, m, c = a.shape
    return a.transpose(1, 0, 2).reshape(m, ns * c)


def kernel(x, meta_tokens, g_mix, w_in, b_gate, w_dw, b_dw, ln_g, ln_b, w_conv_out, w_pool, pool_scale, w_pool_out, w_o, g_ffn, w_ffn_gate, w_ffn_up, w_ffn_down, g_final, loss_target, m_meta_tokens, m_g_mix, m_w_in, m_b_gate, m_w_dw, m_b_dw, m_ln_g, m_ln_b, m_w_conv_out, m_w_pool, m_pool_scale, m_w_pool_out, m_w_o, m_g_ffn, m_w_ffn_gate, m_w_ffn_up, m_w_ffn_down, m_g_final, v_meta_tokens, v_g_mix, v_w_in, v_b_gate, v_w_dw, v_b_dw, v_ln_g, v_ln_b, v_w_conv_out, v_w_pool, v_pool_scale, v_w_pool_out, v_w_o, v_g_ffn, v_w_ffn_gate, v_w_ffn_up, v_w_ffn_down, v_g_final):
    args = dict(locals())
    w = {n: args[n] for n in WEIGHT_ORDER}
    mom = {n: args["m_" + n] for n in WEIGHT_ORDER}
    var = {n: args["v_" + n] for n in WEIGHT_ORDER}
    seq = x.shape[1]
    nb = seq // BR + 1
    tp = nb * BR
    tk = tp // 2 if (tp // 2) % 16 == 0 else BR
    t_total = seq + N_META
    cx, cy, cc = lax.axis_index("x"), lax.axis_index("y"), lax.axis_index("c")
    chip = 2 * cx + cy
    chip1 = jnp.reshape(chip, (1,)).astype(jnp.int32)
    core = jnp.reshape(cc, (1,)).astype(jnp.int32)
    others = jnp.sort(jnp.stack([2 * (1 - cx) + cy, 2 * cx + (1 - cy), 2 * (1 - cx) + (1 - cy)]))
    idx = jnp.concatenate([chip1, others.astype(jnp.int32), core])
    xs, target = x[0], loss_target[0]

    tiny = _gather_tiny(jnp.concatenate([w["meta_tokens"], w["w_dw"][0], jnp.zeros((1, GD), F32)], axis=0))
    small = {n: w[n] for n in REPL if n != "g_final"}
    small["g_final"] = w["g_final"].reshape(1, D)
    small["w_dw"] = _slabs_to_cols(tiny[:, N_META:])
    head = jnp.concatenate([jnp.zeros((PAD, D), F32), _slabs_to_cols(tiny[:, :N_META])], axis=0)

    def cast(group, dep):
        return [_cast_into_slot(_shard2d(n, w[n]), chip1, dep, "cast_" + n) for n in group]

    def gather_finish(group, start, after, name):
        landed = _gather_wait(start[0], start[1], start[2], after, "gather_wait_" + name)
        return dict(zip(group, _forward_halves(landed, "forward_" + name)))

    st_in = _gather_start(cast(("w_in",), tiny), None, "gather_start_in")
    bufs_mix, bufs_ffn = cast(GROUP_MIX, st_in[3]), cast(GROUP_FFN, st_in[3])
    u = _rms_u(head, xs, small["g_mix"], nb)
    gw = gather_finish(("w_in",), st_in, [u] + bufs_mix + bufs_ffn, "in")
    st_mix = _gather_start(bufs_mix, gw["w_in"], "gather_start_mix")
    z = _in_proj(u, gw["w_in"], st_mix[3], nb)
    gw.update(gather_finish(GROUP_MIX, st_mix, [z], "mix"))
    st_ffn = _gather_start(bufs_ffn, gw["w_o"], "gather_start_ffn")
    w_pool_b = gw["w_pool"].reshape(N_SHARD, 4, 64, GD).transpose(1, 0, 2, 3).reshape(4, GD, GD)
    w_co_b, w_po_b, w_o_b = (gw[n].reshape(D, D) for n in ("w_conv_out", "w_pool_out", "w_o"))
    h1, yc, yp, mg, ca, cpre, m, mw, m2b = _mixers_fwd(
        z, head, xs, small["b_gate"] + st_ffn[3][0, 0], small["w_dw"], small["b_dw"], small["ln_g"], small["ln_b"],
        small["pool_scale"], w_co_b, w_pool_b, w_po_b, w_o_b, nb, t_total)
    gw.update(gather_finish(GROUP_FFN, st_ffn, [h1], "ffn"))

    dh1, dh1b, vb, fb, dgb, dub, dh2b, loss, dg_ffn, dg_final = _ffn_fwd_bwd(
        h1, target, small["g_ffn"], small["g_final"], gw["w_ffn_gate"].reshape(D_FF, D),
        gw["w_ffn_up"].reshape(D_FF, D), gw["w_ffn_down"].reshape(D_FF, D), nb)

    def slabs(name, g):
        if name == "w_in":
            return g
        if name == "w_pool":
            return g.reshape(4, N_SHARD, 64, GD).transpose(1, 0, 2, 3).reshape(N_SHARD, 4 * 64, GD)
        return g.reshape(N_SHARD, g.shape[0] // N_SHARD, g.shape[1])

    def reduce_start(group, grads, name):
        g32 = [slabs(n, grads[n][0]) for n in group]
        g16 = [slabs(n, grads[n][1]) for n in group]
        from_sibling = _swap_halves_bf16(g16, "swap_halves_" + name)
        halves = [_add_sibling_half(g, sb, core, "add_sibling_" + n) for n, g, sb in zip(group, g32, from_sibling)]
        return [h for h, _ in halves], _scatter_start([hb for _, hb in halves], "scatter_start_" + name)

    def reduce_finish(group, halves, start, after, name):
        from_chips = _scatter_wait(start[0], start[1], start[2], after, "scatter_wait_" + name)
        return [_add_chip_slabs(h, rb, idx, "add_chips_" + n) for n, h, rb in zip(group, halves, from_chips)]

    half_ff = D_FF // 2
    grads_ffn = {
        "w_ffn_gate": _wgrad(dgb, vb, half_ff, D, tk, "wgrad_ffn_gate"),
        "w_ffn_up": _wgrad(dub, vb, half_ff, D, tk, "wgrad_ffn_up"),
        "w_ffn_down": _wgrad(fb, dh2b, half_ff, D, tk, "wgrad_ffn_down"),
    }
    halves_ffn, sc_ffn = reduce_start(GROUP_FFN, grads_ffn, "ffn")

    dycb, dypb, dzg, dconv, dmwb, dm, db_gate, dln_g, dln_b, db_dw, dps = _mixers_bwd_rows(
        dh1b, yc, yp, z, small["b_gate"], cpre, small["ln_g"], small["ln_b"], mw, small["pool_scale"],
        w_o_b, w_co_b, w_po_b, w_pool_b, sc_ffn[3], nb)
    grads_mix = {
        "w_conv_out": _wgrad(ca, dycb, D, D, tk, "wgrad_conv_out"),
        "w_pool": _wgrad(m, dmwb, GD, GD, tk, "wgrad_pool", diag=True),
        "w_pool_out": _wgrad(m2b, dypb, D, D, tk, "wgrad_pool_out"),
        "w_o": _wgrad(mg, dh1b, D, D, tk, "wgrad_o"),
    }
    halves_mix, sc_mix = reduce_start(GROUP_MIX, grads_mix, "mix")
    dzb, grad_x, dhead, dw_dw, dg_mix = _mixers_bwd_halo(
        dconv, dm, z, dzg, small["w_dw"], head, xs, small["g_mix"], dh1, gw["w_in"], sc_mix[3], nb, t_total)
    packed = jnp.concatenate(
        [dg_mix, db_gate.reshape(2, D), db_dw, dln_g, dln_b, dps, dg_ffn, dg_final,
         jnp.broadcast_to(loss, (1, D)), jnp.zeros((6, D), F32), dhead[PAD:], dw_dw], axis=0)
    sm = _small_start(packed, "small_start")
    grads_in = {"w_in": _wgrad(u, dzb, D, D_IN // N_SHARD, tk, "wgrad_in", col_major=True, dep=sm[3])}
    halves_in, sc_in = reduce_start(("w_in",), grads_in, "in")

    land = _small_wait(sm[0], sm[1], sm[2], [sc_in[3]], "small_wait")
    summed = _sum_slots(land, packed, jnp.reshape(4 * cx + 2 * cy + cc, (1,)).astype(jnp.int32))
    loss = summed[9, 0]

    first = GROUP_FFN + GROUP_MIX
    reduced_half = reduce_finish(GROUP_FFN, halves_ffn, sc_ffn, [summed], "ffn")
    reduced_half += reduce_finish(GROUP_MIX, halves_mix, sc_mix, [summed], "mix")
    reduced = dict(zip(first, _join_halves(reduced_half, "join_halves_first")))
    updates = {n: _adamw(reduced[n], _shard2d(n, w[n]), _shard2d(n, mom[n]), _shard2d(n, var[n]), "adamw_" + n)
               for n in first}

    def repl_stack(d):
        return jnp.concatenate([d["g_mix"], d["b_gate"].reshape(2, D), d["b_dw"], d["ln_g"], d["ln_b"],
                                d["pool_scale"], d["g_ffn"], d["g_final"].reshape(1, D), jnp.ones((7, D), F32)], axis=0)

    def shard_stack(d):
        return jnp.concatenate([d["meta_tokens"], d["w_dw"][0], jnp.ones((1, GD), F32)], axis=0)

    g_repl = summed[0:16]
    g_shard = lax.dynamic_slice_in_dim(summed[16:64], chip * GD, GD, axis=1)
    g_repl, d_repl, m_repl, v_repl = _adamw(g_repl, repl_stack(w), repl_stack(mom), repl_stack(var), "adamw_repl")
    g_shard, d_shard, m_shard, v_shard = _adamw(g_shard, shard_stack(w), shard_stack(mom), shard_stack(var), "adamw_cols")

    done_first = [updates[n][1] for n in first] + [d_repl, d_shard]
    last_half = reduce_finish(("w_in",), halves_in, sc_in, done_first, "in")
    reduced["w_in"] = _join_halves(last_half, "join_halves_in")[0]
    updates["w_in"] = _adamw(reduced["w_in"], w["w_in"][0], mom["w_in"][0], var["w_in"][0], "adamw_w_in")

    def unpack(name, repl, shard):
        if name == "meta_tokens":
            return shard[0:N_META]
        if name == "w_dw":
            return shard[N_META:N_META + KW].reshape(1, KW, GD)
        row = {"g_mix": 0, "b_gate": 1, "b_dw": 3, "ln_g": 4, "ln_b": 5, "pool_scale": 6, "g_ffn": 7, "g_final": 8}[name]
        if name == "b_gate":
            return repl[1:3].reshape(1, 2 * D)
        if name == "g_final":
            return repl[8]
        return repl[row:row + 1]

    out_g, out_d, out_m, out_v = {}, {}, {}, {}
    for n in WEIGHT_ORDER:
        if n in BIG:
            g, d_, m_, v_ = updates[n]
            shape = w[n].shape
            out_g[n], out_d[n], out_m[n], out_v[n] = (_unshard2d(n, a, shape) for a in (g, d_, m_, v_))
        else:
            out_g[n] = unpack(n, g_repl, g_shard)
            out_d[n] = unpack(n, d_repl, d_shard)
            out_m[n] = unpack(n, m_repl, m_shard)
            out_v[n] = unpack(n, v_repl, v_shard)
    return (loss, grad_x[None], *[out_g[n] for n in WEIGHT_ORDER], *[out_d[n] for n in WEIGHT_ORDER],
            *[out_m[n] for n in WEIGHT_ORDER], *[out_v[n] for n in WEIGHT_ORDER])
```

```python
import functools

import jax
import jax.numpy as jnp
from jax import lax
from jax.experimental import pallas as pl
from jax.experimental.pallas import tpu as pltpu

F32 = jnp.float32
BF16 = jnp.bfloat16
MESH = pl.DeviceIdType.MESH

D = 1024
N_META = 16
KW = 31
CPAD = KW // 2
POOL_WINDOWS = (2, 4, 8, 16)
GD = 256
D_IN = 5 * D
D_FF = 2816
N_SHARD = 4
BR = 256
HALO = 16
PAD = BR - N_META
EXT = BR + 2 * HALO
RMS_EPS = 1e-6
LN_EPS = 1e-5
LR, B1, B2, ADAM_EPS, WD, STEP = 0.001, 0.9, 0.999, 1e-08, 0.01, 10
VMEM_LIMIT = 56 * 1024 * 1024


def _cparams(sem, vmem=VMEM_LIMIT):
    return pltpu.CompilerParams(dimension_semantics=sem, vmem_limit_bytes=vmem)


def _dot(a, b):
    return jnp.dot(a, b, preferred_element_type=F32)


def _dot_nt(a, b):
    return lax.dot_general(a, b, (((1,), (1,)), ((), ())), preferred_element_type=F32)


def _dot_tn(a, b):
    return lax.dot_general(a, b, (((0,), (0,)), ((), ())), preferred_element_type=F32)


def _sigmoid(x):
    return 0.5 * jnp.tanh(0.5 * x) + 0.5


def _row_ids(i, n, offset=0):
    return lax.broadcasted_iota(jnp.int32, (n, 1), 0) + (i * BR + offset - PAD)


def _pool_cnt(t, w, t_total):
    left = w // 2
    right = w - 1 - left
    lo = jnp.clip(t - left, 0, t_total)
    hi = jnp.clip(t + right + 1, 0, t_total)
    return jnp.maximum(hi - lo, 1).astype(F32)


def _halo_specs(nb, halo_width=D, width=D):
    last = nb * (BR // HALO) - 1
    return [
        pl.BlockSpec((HALO, halo_width), lambda i: (jnp.maximum(i * (BR // HALO) - 1, 0), 0)),
        pl.BlockSpec((BR, width), lambda i: (i, 0)),
        pl.BlockSpec((HALO, halo_width), lambda i: (jnp.minimum((i + 1) * (BR // HALO), last), 0)),
    ]


def _cols(ref, n):
    return [ref.at[:, k * D:(k + 1) * D] for k in range(n)]


def _fill_ext(ext_ref, prev, cur, nxt, i, nb):
    ext_ref[0:HALO, :] = jnp.where(i > 0, prev, 0.0)
    ext_ref[HALO:HALO + BR, :] = cur
    ext_ref[HALO + BR:EXT, :] = jnp.where(i < nb - 1, nxt, 0.0)


ROT_ROWS = EXT - 8


def _fill_rot(rot_ref, ext_ref, lanes):
    for r in range(1, 8):
        rot_ref[r] = ext_ref[pl.ds(r, ROT_ROWS), lanes]


def _tap(rot_ref, ext_ref, lanes, offset):
    q, r = divmod(offset, 8)
    if r == 0:
        return ext_ref[pl.ds(8 * q, BR), lanes]
    return rot_ref[r, pl.ds(8 * q, BR), :]


def _row_spec(width=D):
    return pl.BlockSpec((BR, width), lambda i: (i, 0))


def _x_spec():
    return pl.BlockSpec((BR, D), lambda i: (jnp.maximum(i - 1, 0), 0))


def _const_spec(shape):
    nd = len(shape)
    return pl.BlockSpec(shape, lambda i: (0,) * nd)


def _rms_u(head, x, g_mix, nb):
    def body(head_ref, x_ref, g_ref, u_ref):
        i = pl.program_id(0)
        h = jnp.where(i == 0, head_ref[...], x_ref[...])
        r = lax.rsqrt(jnp.mean(h * h, axis=-1, keepdims=True) + RMS_EPS)
        u_ref[...] = ((h * r) * g_ref[...]).astype(BF16)

    return pl.pallas_call(
        body, name="rms_u",
        grid=(nb,),
        in_specs=[_const_spec((BR, D)), _x_spec(), _const_spec((1, D))],
        out_specs=_row_spec(),
        out_shape=jax.ShapeDtypeStruct((nb * BR, D), BF16),
        compiler_params=_cparams(("arbitrary",)),
    )(head, x, g_mix)


def _in_proj(u, w_in_b, dep, nb):
    tp = nb * BR
    ns = w_in_b.shape[0]
    wcols = w_in_b.shape[2]
    rows = tp // 4 if (tp // 4) % 16 == 0 else BR

    def body(u_ref, w_ref, dep_ref, z_ref):
        z_ref[...] = _dot(u_ref[...], w_ref[...])

    return pl.pallas_call(
        body, name="in_proj",
        grid=(ns, tp // rows),
        in_specs=[
            pl.BlockSpec((rows, D), lambda s, i: (i, 0)),
            pl.BlockSpec((None, D, wcols), lambda s, i: (s, 0, 0)),
            pl.BlockSpec(memory_space=pl.ANY),
        ],
        out_specs=pl.BlockSpec((rows, wcols), lambda s, i: (i, s)),
        out_shape=jax.ShapeDtypeStruct((tp, ns * wcols), F32),
        compiler_params=_cparams(("arbitrary", "arbitrary")),
    )(u, w_in_b, dep)


def _mixers_fwd(z, head, x, b_gate, w_dw, b_dw, ln_g, ln_b, pool_scale, w_co, w_pool, w_po, w_o, nb, t_total):
    tp = nb * BR

    def body(z_prev, z_cur, z_next, head_ref, x_ref, bg_ref, wdw_ref, bdw_ref,
             lng_ref, lnb_ref, ps_ref, wco_ref, wpool_ref, wpo_ref, wo_ref,
             h1_ref, yc_ref, yp_ref, mg_ref, ca_ref, cpre_ref, m_ref, mw_ref, m2b_ref, ext_ref, pext_ref, rot_ref):
        i = pl.program_id(0)
        avp, agp, pp = _cols(z_prev, 3)
        av, ag, pc, za, zb = _cols(z_cur, 5)
        avn, agn, pn = _cols(z_next, 3)
        _fill_ext(ext_ref, avp[...] * _sigmoid(agp[...]), av[...] * _sigmoid(ag[...]),
                  avn[...] * _sigmoid(agn[...]), i, nb)
        _fill_ext(pext_ref, pp[...], pc[...], pn[...], i, nb)

        def conv_chunk(c, carry):
            lanes = pl.ds(pl.multiple_of(c * 128, 128), 128)
            _fill_rot(rot_ref, ext_ref, lanes)
            acc = jnp.broadcast_to(bdw_ref[:, lanes], (BR, 128))
            for k in range(KW):
                acc = acc + wdw_ref[k:k + 1, lanes] * _tap(rot_ref, ext_ref, lanes, 1 + k)
            cpre_ref[:, lanes] = acc
            return carry
        lax.fori_loop(0, D // 128, conv_chunk, 0)

        conv = cpre_ref[...]
        mu = jnp.mean(conv, axis=-1, keepdims=True)
        xc = conv - mu
        rstd = lax.rsqrt(jnp.mean(xc * xc, axis=-1, keepdims=True) + LN_EPS)
        ln = (xc * rstd) * lng_ref[...] + lnb_ref[...]
        cact = (ln * _sigmoid(ln)).astype(BF16)
        ca_ref[...] = cact
        y_conv = _dot(cact, wco_ref[...])
        yc_ref[...] = y_conv

        t = _row_ids(i, BR)
        for gi, w in enumerate(POOL_WINDOWS):
            left = w // 2
            right = w - 1 - left
            lanes = slice(gi * GD, (gi + 1) * GD)
            s = pext_ref[pl.ds(HALO - left, BR), lanes]
            for j in range(-left + 1, right + 1):
                s = s + pext_ref[pl.ds(HALO + j, BR), lanes]
            m = (s / _pool_cnt(t, w, t_total) - pext_ref[HALO:HALO + BR, lanes]).astype(BF16)
            m_ref[:, lanes] = m
            mw_ref[:, lanes] = _dot(m, wpool_ref[gi])
        mw = mw_ref[...]
        m2b = (mw * ps_ref[...]).astype(BF16)
        m2b_ref[...] = m2b
        y_pool = _dot(m2b, wpo_ref[...])
        yp_ref[...] = y_pool

        s_a = _sigmoid(za[...] + bg_ref[:, 0:D])
        s_b = _sigmoid(zb[...] + bg_ref[:, D:2 * D])
        merged = (s_a * y_conv + s_b * y_pool).astype(BF16)
        mg_ref[...] = merged
        h0 = jnp.where(i == 0, head_ref[...], x_ref[...])
        h1_ref[...] = h0 + _dot(merged, wo_ref[...])

    in_specs = (_halo_specs(nb, 3 * D, 5 * D)
                + [_const_spec((BR, D)), _x_spec(), _const_spec((1, 2 * D)), _const_spec((32, D)),
                   _const_spec((1, D)), _const_spec((1, D)), _const_spec((1, D)), _const_spec((1, D)),
                   _const_spec((D, D)), _const_spec((4, GD, GD)), _const_spec((D, D)), _const_spec((D, D))])
    outs = [(F32, "h1"), (F32, "yc"), (F32, "yp"), (BF16, "mg"), (BF16, "ca"), (F32, "cpre"), (BF16, "m"), (F32, "mw"),
            (BF16, "m2b")]
    return pl.pallas_call(
        body, name="mixers_fwd",
        grid=(nb,),
        in_specs=in_specs,
        out_specs=[_row_spec() for _ in outs],
        out_shape=[jax.ShapeDtypeStruct((tp, D), dt) for dt, _ in outs],
        scratch_shapes=[pltpu.VMEM((EXT, D), F32), pltpu.VMEM((EXT, D), F32), pltpu.VMEM((8, ROT_ROWS, 128), F32)],
        compiler_params=_cparams(("arbitrary",)),
    )(z, z, z, head, x, b_gate, w_dw, b_dw, ln_g, ln_b, pool_scale, w_co, w_pool, w_po, w_o)


def _ffn_fwd_bwd(h1, target, g_ffn, g_final, w_g, w_u, w_d, nb):
    tp = nb * BR

    def body(h1_ref, tgt_ref, gf_ref, gfin_ref, wg_hbm, wu_hbm, wd_hbm,
             dh1_ref, dh1b_ref, vb_ref, fb_ref, dgb_ref, dub_ref, dh2b_ref, loss_ref, dgf_ref, dgfin_ref,
             wg_ref, wu_ref, wd_ref, sem):
        i = pl.program_id(0)

        @pl.when(i == 0)
        def _():
            copies = [pltpu.make_async_copy(wg_hbm, wg_ref, sem.at[0]),
                      pltpu.make_async_copy(wu_hbm, wu_ref, sem.at[1]),
                      pltpu.make_async_copy(wd_hbm, wd_ref, sem.at[2])]
            for cp in copies:
                cp.start()
            loss_ref[...] = jnp.zeros_like(loss_ref)
            dgf_ref[...] = jnp.zeros_like(dgf_ref)
            dgfin_ref[...] = jnp.zeros_like(dgfin_ref)
            for cp in copies:
                cp.wait()

        h1 = h1_ref[...]
        r1 = lax.rsqrt(jnp.mean(h1 * h1, axis=-1, keepdims=True) + RMS_EPS)
        vn = h1 * r1
        vb = (vn * gf_ref[...]).astype(BF16)
        vb_ref[...] = vb
        g = _dot_nt(vb, wg_ref[...])
        up = _dot_nt(vb, wu_ref[...])
        sg = _sigmoid(g)
        sl = g * sg
        fb = (sl * up).astype(BF16)
        fb_ref[...] = fb
        h2 = h1 + _dot(fb, wd_ref[...])
        r2 = lax.rsqrt(jnp.mean(h2 * h2, axis=-1, keepdims=True) + RMS_EPS)
        yn = h2 * r2
        valid = i > 0
        diff = jnp.where(valid, yn * gfin_ref[...] - tgt_ref[...], 0.0)
        loss_ref[...] += 0.5 * jnp.sum(jnp.mean(diff * diff, axis=-1, keepdims=True))
        dy = diff * (1.0 / D)
        dgfin_ref[...] += jnp.sum(dy * yn, axis=0, keepdims=True)
        gd = dy * gfin_ref[...]
        dh2 = r2 * (gd - yn * jnp.mean(yn * gd, axis=-1, keepdims=True))
        dh2b = dh2.astype(BF16)
        dh2b_ref[...] = dh2b
        df = _dot_nt(dh2b, wd_ref[...])
        dub = (df * sl).astype(BF16)
        dgb = (df * up * (sg * (1.0 + g * (1.0 - sg)))).astype(BF16)
        dub_ref[...] = dub
        dgb_ref[...] = dgb
        dv = _dot(dgb, wg_ref[...]) + _dot(dub, wu_ref[...])
        dgf_ref[...] += jnp.sum(dv * vn, axis=0, keepdims=True)
        gd1 = dv * gf_ref[...]
        dh1 = dh2 + r1 * (gd1 - vn * jnp.mean(vn * gd1, axis=-1, keepdims=True))
        dh1_ref[...] = dh1
        dh1b_ref[...] = dh1.astype(BF16)

    any_spec = pl.BlockSpec(memory_space=pl.ANY)
    return pl.pallas_call(
        body, name="ffn_fwd_bwd",
        grid=(nb,),
        in_specs=[_row_spec(), _x_spec(), _const_spec((1, D)), _const_spec((1, D)), any_spec, any_spec, any_spec],
        out_specs=[_row_spec(), _row_spec(), _row_spec(), _row_spec(D_FF), _row_spec(D_FF), _row_spec(D_FF), _row_spec(),
                   _const_spec((1, 1)), _const_spec((1, D)), _const_spec((1, D))],
        out_shape=[jax.ShapeDtypeStruct((tp, D), F32), jax.ShapeDtypeStruct((tp, D), BF16),
                   jax.ShapeDtypeStruct((tp, D), BF16), jax.ShapeDtypeStruct((tp, D_FF), BF16),
                   jax.ShapeDtypeStruct((tp, D_FF), BF16), jax.ShapeDtypeStruct((tp, D_FF), BF16),
                   jax.ShapeDtypeStruct((tp, D), BF16), jax.ShapeDtypeStruct((1, 1), F32),
                   jax.ShapeDtypeStruct((1, D), F32), jax.ShapeDtypeStruct((1, D), F32)],
        scratch_shapes=[pltpu.VMEM((D_FF, D), BF16), pltpu.VMEM((D_FF, D), BF16), pltpu.VMEM((D_FF, D), BF16),
                        pltpu.SemaphoreType.DMA((3,))],
        compiler_params=_cparams(("arbitrary",)),
    )(h1, target, g_ffn, g_final, w_g, w_u, w_d)


def _mixers_bwd_rows(dh1b, yc, yp, z, b_gate, cpre, ln_g, ln_b, mw, pool_scale, w_o, w_co, w_po, w_pool, dep, nb):
    tp = nb * BR

    def body(dh1b_ref, yc_ref, yp_ref, za, zb, bg_ref, cpre_ref, lng_ref, lnb_ref, mw_ref, ps_ref,
             wo_ref, wco_ref, wpo_ref, wpool_ref, dep_ref,
             dycb_ref, dypb_ref, dzg_ref, dconv_ref, dmwb_ref, dm_ref, dbg_ref, dlng_ref, dlnb_ref, dbdw_ref, dps_ref):
        i = pl.program_id(0)

        @pl.when(i == 0)
        def _():
            for r in (dbg_ref, dlng_ref, dlnb_ref, dbdw_ref, dps_ref):
                r[...] = jnp.zeros_like(r)

        dmg = _dot_nt(dh1b_ref[...], wo_ref[...])
        s_a = _sigmoid(za[...] + bg_ref[:, 0:D])
        s_b = _sigmoid(zb[...] + bg_ref[:, D:2 * D])
        dycb = (dmg * s_a).astype(BF16)
        dypb = (dmg * s_b).astype(BF16)
        dycb_ref[...] = dycb
        dypb_ref[...] = dypb
        dza = dmg * yc_ref[...] * (s_a * (1.0 - s_a))
        dzb = dmg * yp_ref[...] * (s_b * (1.0 - s_b))
        dzg_ref[:, 0:D] = dza.astype(BF16)
        dzg_ref[:, D:2 * D] = dzb.astype(BF16)
        dbg_ref[:, 0:D] += jnp.sum(dza, axis=0, keepdims=True)
        dbg_ref[:, D:2 * D] += jnp.sum(dzb, axis=0, keepdims=True)

        dca = _dot_nt(dycb, wco_ref[...])
        conv = cpre_ref[...]
        mu = jnp.mean(conv, axis=-1, keepdims=True)
        xc = conv - mu
        rstd = lax.rsqrt(jnp.mean(xc * xc, axis=-1, keepdims=True) + LN_EPS)
        xhat = xc * rstd
        ln = xhat * lng_ref[...] + lnb_ref[...]
        sg = _sigmoid(ln)
        dln = dca * (sg * (1.0 + ln * (1.0 - sg)))
        dlng_ref[...] += jnp.sum(dln * xhat, axis=0, keepdims=True)
        dlnb_ref[...] += jnp.sum(dln, axis=0, keepdims=True)
        dxh = dln * lng_ref[...]
        dconv = rstd * (dxh - jnp.mean(dxh, axis=-1, keepdims=True)
                        - xhat * jnp.mean(dxh * xhat, axis=-1, keepdims=True))
        dconv_ref[...] = dconv
        dbdw_ref[...] += jnp.sum(dconv, axis=0, keepdims=True)

        dm2 = _dot_nt(dypb, wpo_ref[...])
        dps_ref[...] += jnp.sum(dm2 * mw_ref[...], axis=0, keepdims=True)
        dmwb = (dm2 * ps_ref[...]).astype(BF16)
        dmwb_ref[...] = dmwb
        for gi in range(len(POOL_WINDOWS)):
            lanes = slice(gi * GD, (gi + 1) * GD)
            dm_ref[:, lanes] = _dot_nt(dmwb[:, lanes], wpool_ref[gi])

    in_specs = [_row_spec(), _row_spec(), _row_spec(),
                pl.BlockSpec((BR, D), lambda i: (i, 3)), pl.BlockSpec((BR, D), lambda i: (i, 4)),
                _const_spec((1, 2 * D)), _row_spec(), _const_spec((1, D)), _const_spec((1, D)), _row_spec(),
                _const_spec((1, D)), _const_spec((D, D)), _const_spec((D, D)), _const_spec((D, D)),
                _const_spec((4, GD, GD)), pl.BlockSpec(memory_space=pl.ANY)]
    return pl.pallas_call(
        body, name="mixers_bwd_rows",
        grid=(nb,),
        in_specs=in_specs,
        out_specs=[_row_spec(), _row_spec(), _row_spec(2 * D), _row_spec(), _row_spec(), _row_spec(),
                   _const_spec((1, 2 * D)), _const_spec((1, D)), _const_spec((1, D)), _const_spec((1, D)),
                   _const_spec((1, D))],
        out_shape=[jax.ShapeDtypeStruct((tp, D), BF16), jax.ShapeDtypeStruct((tp, D), BF16),
                   jax.ShapeDtypeStruct((tp, 2 * D), BF16), jax.ShapeDtypeStruct((tp, D), F32),
                   jax.ShapeDtypeStruct((tp, D), BF16), jax.ShapeDtypeStruct((tp, D), F32),
                   jax.ShapeDtypeStruct((1, 2 * D), F32), jax.ShapeDtypeStruct((1, D), F32),
                   jax.ShapeDtypeStruct((1, D), F32), jax.ShapeDtypeStruct((1, D), F32),
                   jax.ShapeDtypeStruct((1, D), F32)],
        compiler_params=_cparams(("arbitrary",)),
    )(dh1b, yc, yp, z, z, b_gate, cpre, ln_g, ln_b, mw, pool_scale, w_o, w_co, w_po, w_pool, dep)


def _mixers_bwd_halo(dconv, dm, z, dzg, w_dw, head, x, g_mix, dh1, w_in_b, dep, nb, t_total):
    tp = nb * BR
    ns = w_in_b.shape[0]
    wcols = w_in_b.shape[2]
    seq = x.shape[0]

    def body(dcp, dcc, dcn, dmp, dmc, dmn, z_prev, z_cur, z_next, dzg_ref, wdw_ref, head_ref, x_ref, g_ref,
             dh1_ref, w_hbm, dep_ref,
             dzb_ref, gx_ref, dhead_ref, dwdw_ref, dgmix_ref,
             w_ref, sem, aext_ref, dext_ref, qext_ref, da_ref, rot_ref, dwp_ref):
        i = pl.program_id(0)
        (avp, agp), (av, ag), (avn, agn) = _cols(z_prev, 2), _cols(z_cur, 2), _cols(z_next, 2)

        @pl.when(i == 0)
        def _():
            cp = pltpu.make_async_copy(w_hbm, w_ref, sem.at[0])
            cp.start()
            dwp_ref[...] = jnp.zeros_like(dwp_ref)
            dgmix_ref[...] = jnp.zeros_like(dgmix_ref)
            cp.wait()

        sig_g = _sigmoid(ag[...])
        _fill_ext(aext_ref, avp[...] * _sigmoid(agp[...]), av[...] * sig_g, avn[...] * _sigmoid(agn[...]), i, nb)
        _fill_ext(dext_ref, dcp[...], dcc[...], dcn[...], i, nb)
        _fill_ext(qext_ref, dmp[...], dmc[...], dmn[...], i, nb)

        def conv_chunk(c, carry):
            lanes = pl.ds(pl.multiple_of(c * 128, 128), 128)
            _fill_rot(rot_ref, dext_ref, lanes)
            acc = jnp.zeros((BR, 128), F32)
            for k in range(KW):
                acc = acc + wdw_ref[k:k + 1, lanes] * _tap(rot_ref, dext_ref, lanes, KW - k)
            da_ref[:, lanes] = acc
            _fill_rot(rot_ref, aext_ref, lanes)
            dcv = dext_ref[HALO:HALO + BR, lanes]
            for k in range(KW):
                prod = _tap(rot_ref, aext_ref, lanes, 1 + k) * dcv
                dwp_ref[k, :, lanes] += jnp.sum(prod.reshape(BR // 8, 8, 128), axis=0)
            return carry
        lax.fori_loop(0, D // 128, conv_chunk, 0)

        @pl.when(i == nb - 1)
        def _():
            dwdw_ref[...] = jnp.sum(dwp_ref[...], axis=1)

        da = da_ref[...]
        a_val = av[...]
        dzb_ref[:, 0:D] = (da * sig_g).astype(BF16)
        dzb_ref[:, D:2 * D] = (da * a_val * (sig_g * (1.0 - sig_g))).astype(BF16)

        t_ext = _row_ids(i, EXT, -HALO)
        for gi, w in enumerate(POOL_WINDOWS):
            left = w // 2
            right = w - 1 - left
            lanes = slice(gi * GD, (gi + 1) * GD)
            qext_ref[:, lanes] = qext_ref[:, lanes] / _pool_cnt(t_ext, w, t_total)
            s = qext_ref[pl.ds(HALO - right, BR), lanes]
            for j in range(-right + 1, left + 1):
                s = s + qext_ref[pl.ds(HALO + j, BR), lanes]
            dzb_ref[:, 2 * D + gi * GD:2 * D + (gi + 1) * GD] = (s - dmc[:, lanes]).astype(BF16)
        dzb_ref[:, 3 * D:5 * D] = dzg_ref[...]

        du = _dot_nt(dzb_ref[:, 0:wcols], w_ref[0])
        for s_i in range(1, ns):
            du = du + _dot_nt(dzb_ref[:, s_i * wcols:(s_i + 1) * wcols], w_ref[s_i])
        h0 = jnp.where(i == 0, head_ref[...], x_ref[...])
        r0 = lax.rsqrt(jnp.mean(h0 * h0, axis=-1, keepdims=True) + RMS_EPS)
        un = h0 * r0
        dgmix_ref[...] += jnp.sum(du * un, axis=0, keepdims=True)
        gd = du * g_ref[...]
        dh0 = dh1_ref[...] + r0 * (gd - un * jnp.mean(un * gd, axis=-1, keepdims=True))
        gx_ref[...] = dh0

        @pl.when(i == 0)
        def _():
            dhead_ref[...] = dh0

    any_spec = pl.BlockSpec(memory_space=pl.ANY)
    in_specs = (_halo_specs(nb) + _halo_specs(nb) + _halo_specs(nb, 2 * D, 2 * D)
                + [_row_spec(2 * D), _const_spec((32, D)), _const_spec((BR, D)), _x_spec(), _const_spec((1, D)),
                   _row_spec(), any_spec, any_spec])
    return pl.pallas_call(
        body, name="mixers_bwd_halo",
        grid=(nb,),
        in_specs=in_specs,
        out_specs=[_row_spec(D_IN), _x_spec(), _const_spec((BR, D)), _const_spec((32, D)), _const_spec((1, D))],
        out_shape=[jax.ShapeDtypeStruct((tp, D_IN), BF16), jax.ShapeDtypeStruct((seq, D), F32),
                   jax.ShapeDtypeStruct((BR, D), F32), jax.ShapeDtypeStruct((32, D), F32),
                   jax.ShapeDtypeStruct((1, D), F32)],
        scratch_shapes=[pltpu.VMEM((ns, D, wcols), BF16), pltpu.SemaphoreType.DMA((1,)),
                        pltpu.VMEM((EXT, D), F32), pltpu.VMEM((EXT, D), F32), pltpu.VMEM((EXT, D), F32),
                        pltpu.VMEM((BR, D), F32), pltpu.VMEM((8, ROT_ROWS, 128), F32), pltpu.VMEM((32, 8, D), F32)],
        compiler_params=_cparams(("arbitrary",)),
    )(dconv, dconv, dconv, dm, dm, dm, z, z, z, dzg, w_dw, head, x, g_mix, dh1, w_in_b, dep)


def _wgrad(a, c, tm, tn, tk, name, diag=False, col_major=False, dep=None):
    tp, m = a.shape
    n = c.shape[1]
    nk = tp // tk
    gm, gn = m // tm, n // tn

    def body(a_ref, c_ref, *rest):
        o_ref, ob_ref = rest[-2:]
        k = pl.program_id(2)

        @pl.when(k == 0)
        def _():
            o_ref[...] = jnp.zeros_like(o_ref)

        o_ref[...] += _dot_tn(a_ref[...], c_ref[...])

        @pl.when(k == nk - 1)
        def _():
            ob_ref[...] = o_ref[...].astype(BF16)

    c_map = lambda i, j, k: (k, j)
    grid = (gm, gn, nk)
    deps = [] if dep is None else [dep]
    if diag:
        grid = (gm, 1, nk)
        c_map = lambda i, j, k: (k, i)
        o_spec = pl.BlockSpec((tm, tn), lambda i, j, k: (i, 0))
        o_shape = (m, tn)
    elif col_major:
        o_spec = pl.BlockSpec((None, tm, tn), lambda i, j, k: (j, i, 0))
        o_shape = (gn, m, tn)
    else:
        o_spec = pl.BlockSpec((tm, tn), lambda i, j, k: (i, j))
        o_shape = (m, n)
    return pl.pallas_call(
        body, name=name,
        grid=grid,
        in_specs=[pl.BlockSpec((tk, tm), lambda i, j, k: (k, i)), pl.BlockSpec((tk, tn), c_map)]
        + [pl.BlockSpec(memory_space=pl.ANY)] * len(deps),
        out_specs=[o_spec, o_spec],
        out_shape=[jax.ShapeDtypeStruct(o_shape, F32), jax.ShapeDtypeStruct(o_shape, BF16)],
        compiler_params=_cparams(("arbitrary", "arbitrary", "arbitrary")),
    )(a, c, *deps)


def _place():
    x, y, c = lax.axis_index("x"), lax.axis_index("y"), lax.axis_index("c")
    others = [(1 - x, y), (x, 1 - y), (1 - x, 1 - y)]
    return x, y, c, others


def _split2(a, axis=0):
    return a.reshape(a.shape[:axis] + (2, a.shape[axis] // 2) + a.shape[axis + 1:])


def _merge2(a, axis=0):
    return a.reshape(a.shape[:axis] + (2 * a.shape[axis + 1],) + a.shape[axis + 2:])


def _cast_into_slot(w2d, chip, dep, name):
    r, c = w2d.shape
    r2 = r // 2

    def body(chip_ref, w_ref, dep_ref, o_ref):
        o_ref[...] = w_ref[...].astype(BF16)

    return pl.pallas_call(
        body, name=name,
        grid_spec=pltpu.PrefetchScalarGridSpec(
            num_scalar_prefetch=1, grid=(2,),
            in_specs=[pl.BlockSpec((r2, c), lambda h, chip_ref: (h, 0)), pl.BlockSpec(memory_space=pl.ANY)],
            out_specs=pl.BlockSpec((None, None, r2, c), lambda h, chip_ref: (chip_ref[0], h, 0, 0))),
        out_shape=jax.ShapeDtypeStruct((N_SHARD, 2, r2, c), BF16),
        compiler_params=_cparams(("arbitrary",)),
    )(chip, w2d, dep)


HBM_SPEC = pl.BlockSpec(memory_space=pltpu.HBM)
SEM_SPEC = pl.BlockSpec(memory_space=pltpu.SEMAPHORE)
DATAFLOW = pltpu.SideEffectType.DATAFLOW_SIDE_EFFECTING
TOKEN = jax.ShapeDtypeStruct((8, 128), F32)


def _in_hbm(a):
    return pltpu.with_memory_space_constraint(a, pltpu.HBM)


def _gather_tiny(v):
    vm = pl.BlockSpec(memory_space=pltpu.VMEM)

    def body(v_ref, out_ref, send_sems, recv_sems):
        x, y, c, others = _place()
        mine = 2 * x + y
        sends = [pltpu.make_async_remote_copy(
            src_ref=v_ref, dst_ref=out_ref.at[mine], send_sem=send_sems.at[j], recv_sem=recv_sems.at[j],
            device_id=(*chip, c), device_id_type=MESH) for j, chip in enumerate(others)]
        for cp in sends:
            cp.start()
        out_ref[mine] = v_ref[...]
        for j, chip in enumerate(others):
            landed = out_ref.at[2 * chip[0] + chip[1]]
            pltpu.make_async_remote_copy(
                src_ref=landed, dst_ref=landed, send_sem=send_sems.at[j], recv_sem=recv_sems.at[j],
                device_id=(x, y, c), device_id_type=MESH).wait_recv()
        for cp in sends:
            cp.wait_send()

    return pl.pallas_call(
        body, name="gather_tiny",
        in_specs=[vm], out_specs=vm,
        out_shape=jax.ShapeDtypeStruct((N_SHARD,) + v.shape, v.dtype),
        scratch_shapes=[pltpu.SemaphoreType.DMA((3,)), pltpu.SemaphoreType.DMA((3,))],
    )(v)


def _ici_copies(srcs, dsts, send_sems, recv_sems, started):
    x, y, c, others = _place()
    mine = 2 * x + y
    copies = []
    for a in range(len(srcs)):
        for j, chip in enumerate(others):
            there = 2 * chip[0] + chip[1]
            src, dst = srcs[a](mine, there, c), dsts[a](mine, there, c)
            if not started:
                dst = dsts[a](there, mine, c)
            copies.append(pltpu.make_async_remote_copy(
                src_ref=src, dst_ref=dst, send_sem=send_sems.at[a * 3 + j], recv_sem=recv_sems.at[a * 3 + j],
                device_id=(*chip, c), device_id_type=MESH))
    return copies


def _split_start(srcs_of, dsts_of, arrays, n_src, name, copies_of=None, n_sems=None, dep=None):
    n = len(arrays)
    n_sems = n_sems or 3 * n_src
    copies_of = copies_of or (lambda ins, ss, rs, started: _ici_copies(srcs_of(ins), dsts_of(ins), ss, rs, started))
    deps = [] if dep is None else [dep]
    nd = len(deps)

    def body(*refs):
        ins = refs[:n]
        send_sems, recv_sems = refs[n + nd], refs[n + nd + 1]
        token = refs[2 * n + nd + 2]
        for cp in copies_of(ins, send_sems, recv_sems, True):
            cp.start()
        token[...] = jnp.zeros_like(token)

    out = pl.pallas_call(
        body, name=name,
        in_specs=[HBM_SPEC] * n + [pl.BlockSpec(memory_space=pl.ANY)] * nd,
        out_specs=(SEM_SPEC, SEM_SPEC, *([HBM_SPEC] * n), pl.BlockSpec(memory_space=pltpu.VMEM)),
        out_shape=(pltpu.SemaphoreType.DMA((n_sems,)), pltpu.SemaphoreType.DMA((n_sems,)),
                   *[pltpu.HBM(a.shape, a.dtype) for a in arrays], TOKEN),
        input_output_aliases={a: 2 + a for a in range(n)},
        compiler_params=pltpu.CompilerParams(has_side_effects=DATAFLOW),
    )(*[_in_hbm(a) for a in arrays], *deps)
    return out[0], out[1], list(out[2:2 + n]), out[2 + n]


def _split_wait(srcs_of, dsts_of, send_sems, recv_sems, arrays, after, name, copies_of=None):
    n = len(arrays)
    copies_of = copies_of or (lambda ins, ss, rs, started: _ici_copies(srcs_of(ins), dsts_of(ins), ss, rs, started))

    def body(*refs):
        ins = refs[:n]
        send_sems, recv_sems = refs[n], refs[n + 1]
        for cp in copies_of(ins, send_sems, recv_sems, False):
            cp.wait_send()
            cp.wait_recv()

    return pl.pallas_call(
        body, name=name,
        in_specs=[HBM_SPEC] * n + [SEM_SPEC, SEM_SPEC] + [pl.BlockSpec(memory_space=pl.ANY)] * len(after),
        out_specs=[HBM_SPEC] * n,
        out_shape=[pltpu.HBM(a.shape, a.dtype) for a in arrays],
        input_output_aliases={a: a for a in range(n)},
        compiler_params=pltpu.CompilerParams(has_side_effects=DATAFLOW),
    )(*arrays, send_sems, recv_sems, *after)


def _gather_views(ins):
    view = [lambda frm, to, c, r=r: r.at[frm, c] for r in ins]
    return view


def _gather_start(bufs, dep, name):
    return _split_start(_gather_views, _gather_views, bufs, len(bufs), name, dep=dep)


def _gather_wait(send_sems, recv_sems, bufs, after, name):
    return _split_wait(_gather_views, _gather_views, send_sems, recv_sems, bufs, after, name)


def _forward_halves(bufs, name):
    n = len(bufs)
    any_spec = pl.BlockSpec(memory_space=pl.ANY)

    def body(*refs):
        outs = refs[n:2 * n]
        send_sems, recv_sems = refs[2 * n:]
        x, y, c, others = _place()
        copies = []
        for a in range(n):
            for j, chip in enumerate(others):
                landed = outs[a].at[2 * chip[0] + chip[1], c]
                copies.append(pltpu.make_async_remote_copy(
                    src_ref=landed, dst_ref=landed, send_sem=send_sems.at[a * 3 + j], recv_sem=recv_sems.at[a * 3 + j],
                    device_id=(x, y, 1 - c), device_id_type=MESH))
        for cp in copies:
            cp.start()
        for a in range(n):
            for j, chip in enumerate(others):
                landed = outs[a].at[2 * chip[0] + chip[1], 1 - c]
                pltpu.make_async_remote_copy(
                    src_ref=landed, dst_ref=landed, send_sem=send_sems.at[a * 3 + j], recv_sem=recv_sems.at[a * 3 + j],
                    device_id=(x, y, c), device_id_type=MESH).wait_recv()
        for cp in copies:
            cp.wait_send()

    out = pl.pallas_call(
        body, name=name,
        in_specs=[any_spec] * n, out_specs=[any_spec] * n,
        out_shape=[jax.ShapeDtypeStruct(b.shape, b.dtype) for b in bufs],
        input_output_aliases={a: a for a in range(n)},
        scratch_shapes=[pltpu.SemaphoreType.DMA((3 * n,)), pltpu.SemaphoreType.DMA((3 * n,))],
    )(*bufs)
    return [_merge2(o, 1) for o in out]


def _swap_halves_bf16(gbs, name):
    n = len(gbs)
    any_spec = pl.BlockSpec(memory_space=pl.ANY)

    def body(*refs):
        ins, outs = refs[:n], refs[n:2 * n]
        send_sems, recv_sems = refs[2 * n:]
        x, y, c, _ = _place()
        copies = []
        for a in range(n):
            copies.append(pltpu.make_async_remote_copy(
                src_ref=ins[a].at[:, 1 - c], dst_ref=outs[a], send_sem=send_sems.at[a], recv_sem=recv_sems.at[a],
                device_id=(x, y, 1 - c), device_id_type=MESH))
        for cp in copies:
            cp.start()
        for cp in copies:
            cp.wait()

    return pl.pallas_call(
        body, name=name,
        in_specs=[any_spec] * n, out_specs=[any_spec] * n,
        out_shape=[jax.ShapeDtypeStruct((g.shape[0], g.shape[1] // 2, g.shape[2]), g.dtype) for g in gbs],
        scratch_shapes=[pltpu.SemaphoreType.DMA((n,)), pltpu.SemaphoreType.DMA((n,))],
    )(*[_split2(g, 1) for g in gbs])


def _scatter_srcs(n):
    return lambda ins: [lambda frm, to, c, r=r: r.at[to] for r in ins[:n]]


def _scatter_dsts(n):
    return lambda ins: [lambda frm, to, c, r=r: r.at[frm] for r in ins[n:]]


def _scatter_start(hbs, name):
    n = len(hbs)
    lands = [lax.empty(h.shape, h.dtype) for h in hbs]
    return _split_start(_scatter_srcs(n), _scatter_dsts(n), list(hbs) + lands, n, name)


def _scatter_wait(send_sems, recv_sems, arrays, after, name):
    n = len(arrays) // 2
    return _split_wait(_scatter_srcs(n), _scatter_dsts(n), send_sems, recv_sems, arrays, after, name)[n:]


def _join_halves(rhs, name):
    n = len(rhs)
    any_spec = pl.BlockSpec(memory_space=pl.ANY)

    def body(*refs):
        outs = refs[n:2 * n]
        send_sems, recv_sems = refs[2 * n:]
        x, y, c, _ = _place()
        copies = []
        for a in range(n):
            copies.append(pltpu.make_async_remote_copy(
                src_ref=outs[a].at[c], dst_ref=outs[a].at[c], send_sem=send_sems.at[a],
                recv_sem=recv_sems.at[a], device_id=(x, y, 1 - c), device_id_type=MESH))
        for cp in copies:
            cp.start()
        for a in range(n):
            landed = outs[a].at[1 - c]
            pltpu.make_async_remote_copy(
                src_ref=landed, dst_ref=landed, send_sem=send_sems.at[a], recv_sem=recv_sems.at[a],
                device_id=(x, y, c), device_id_type=MESH).wait_recv()
        for cp in copies:
            cp.wait_send()

    out = pl.pallas_call(
        body, name=name,
        in_specs=[any_spec] * n, out_specs=[any_spec] * n,
        out_shape=[jax.ShapeDtypeStruct(r.shape, r.dtype) for r in rhs],
        input_output_aliases={a: a for a in range(n)},
        scratch_shapes=[pltpu.SemaphoreType.DMA((n,)), pltpu.SemaphoreType.DMA((n,))],
    )(*rhs)
    return [_merge2(o) for o in out]


FLIPS = [(dx, dy, dc) for dx in (0, 1) for dy in (0, 1) for dc in (0, 1)][1:]


def _peer_copies(ins, send_sems, recv_sems, started):
    x, y, c, _ = _place()
    copies = []
    for k, (dx, dy, dc) in enumerate(FLIPS):
        px, py, pc = jnp.bitwise_xor(x, dx), jnp.bitwise_xor(y, dy), jnp.bitwise_xor(c, dc)
        slot = 4 * x + 2 * y + c if started else 4 * px + 2 * py + pc
        copies.append(pltpu.make_async_remote_copy(
            src_ref=ins[0], dst_ref=ins[1].at[slot], send_sem=send_sems.at[k], recv_sem=recv_sems.at[k],
            device_id=(px, py, pc), device_id_type=MESH))
    return copies


def _small_start(v, name):
    land = lax.empty((8,) + v.shape, v.dtype)
    return _split_start(None, None, [v, land], 0, name, copies_of=_peer_copies, n_sems=len(FLIPS))


def _small_wait(send_sems, recv_sems, arrays, after, name):
    return _split_wait(None, None, send_sems, recv_sems, arrays, after, name, copies_of=_peer_copies)[1]


def _sum_slots(land, v, me):
    rows, cols = v.shape

    def body(me_ref, land_ref, v_ref, o_ref):
        o_ref[...] = jnp.zeros_like(o_ref)
        for d in range(8):
            @pl.when(me_ref[0] == d)
            def _():
                o_ref[...] += v_ref[...]

            @pl.when(me_ref[0] != d)
            def _():
                o_ref[...] += land_ref[d]

    return pl.pallas_call(
        body, name="sum_slots",
        grid_spec=pltpu.PrefetchScalarGridSpec(
            num_scalar_prefetch=1, grid=(1,),
            in_specs=[pl.BlockSpec((8, rows, cols), lambda i, me_ref: (0, 0, 0)),
                      pl.BlockSpec((rows, cols), lambda i, me_ref: (0, 0))],
            out_specs=pl.BlockSpec((rows, cols), lambda i, me_ref: (0, 0))),
        out_shape=jax.ShapeDtypeStruct((rows, cols), F32),
        compiler_params=_cparams(("arbitrary",)),
    )(me, land, v)


def _row_block(r):
    for cand in (512, 352, 256, 128, 64, 48, 16):
        if r % cand == 0:
            return cand
    return r


def _add_sibling_half(g, sb, idx, name):
    ns, r, c = g.shape
    r2 = r // 2

    def body(idx_ref, g_ref, sb_ref, hown_ref, hb_ref):
        h = g_ref[...] + sb_ref[...].astype(F32)
        hb_ref[...] = h.astype(BF16)

        @pl.when(pl.program_id(0) == idx_ref[0])
        def _():
            hown_ref[...] = h

    spec = pl.BlockSpec((None, r2, c), lambda s, idx_ref: (s, 0, 0))
    return pl.pallas_call(
        body, name=name,
        grid_spec=pltpu.PrefetchScalarGridSpec(
            num_scalar_prefetch=1, grid=(ns,),
            in_specs=[pl.BlockSpec((None, r2, c), lambda s, idx_ref: (s, idx_ref[4], 0)), spec],
            out_specs=[pl.BlockSpec((r2, c), lambda s, idx_ref: (0, 0)), spec]),
        out_shape=[jax.ShapeDtypeStruct((r2, c), F32), jax.ShapeDtypeStruct((ns, r2, c), BF16)],
        compiler_params=_cparams(("arbitrary",)),
    )(idx, g, sb)


def _add_chip_slabs(h, rb, idx, name):
    r2, c = h.shape

    def body(idx_ref, h_ref, r0_ref, r1_ref, r2_ref, o_ref):
        o_ref[...] = ((h_ref[...] + r0_ref[...].astype(F32)) + r1_ref[...].astype(F32)) + r2_ref[...].astype(F32)

    def pick(k):
        return pl.BlockSpec((None, r2, c), lambda i, idx_ref: (idx_ref[k], 0, 0))

    return pl.pallas_call(
        body, name=name,
        grid_spec=pltpu.PrefetchScalarGridSpec(
            num_scalar_prefetch=1, grid=(1,),
            in_specs=[pl.BlockSpec((r2, c), lambda i, idx_ref: (0, 0)), pick(1), pick(2), pick(3)],
            out_specs=pl.BlockSpec((None, r2, c), lambda i, idx_ref: (idx_ref[4], 0, 0))),
        out_shape=jax.ShapeDtypeStruct((2, r2, c), F32),
        compiler_params=_cparams(("arbitrary",)),
    )(idx, h, rb, rb, rb)


def _adamw(g, w, m, v, name):
    r, c = g.shape
    br = _row_block(r)

    def body(g_ref, w_ref, m_ref, v_ref, go_ref, d_ref, nm_ref, nv_ref):
        gg = g_ref[...]
        go_ref[...] = gg
        nm = B1 * m_ref[...] + (1.0 - B1) * gg
        nv = B2 * v_ref[...] + (1.0 - B2) * jnp.square(gg)
        m_hat = nm / (1.0 - B1 ** STEP)
        v_hat = nv / (1.0 - B2 ** STEP)
        d_ref[...] = -LR * (m_hat / (jnp.sqrt(v_hat) + ADAM_EPS) + WD * w_ref[...])
        nm_ref[...] = nm
        nv_ref[...] = nv

    spec = pl.BlockSpec((br, c), lambda i: (i, 0))
    return pl.pallas_call(
        body, name=name,
        grid=(r // br,),
        in_specs=[spec] * 4, out_specs=[spec] * 4,
        out_shape=[jax.ShapeDtypeStruct((r, c), F32)] * 4,
        compiler_params=_cparams(("arbitrary",)),
    )(g, w, m, v)


BIG = ("w_in", "w_conv_out", "w_pool", "w_pool_out", "w_o", "w_ffn_gate", "w_ffn_up", "w_ffn_down")
REPL = ("g_mix", "b_gate", "b_dw", "ln_g", "ln_b", "pool_scale", "g_ffn", "g_final")
GROUP_MIX = ("w_conv_out", "w_pool", "w_pool_out", "w_o")
GROUP_FFN = ("w_ffn_gate", "w_ffn_up", "w_ffn_down")
TRANSPOSED = ("w_ffn_gate", "w_ffn_up")
WEIGHT_ORDER = ("meta_tokens", "g_mix", "w_in", "b_gate", "w_dw", "b_dw", "ln_g", "ln_b", "w_conv_out", "w_pool",
                "pool_scale", "w_pool_out", "w_o", "g_ffn", "w_ffn_gate", "w_ffn_up", "w_ffn_down", "g_final")


def _shard2d(name, a):
    a = a[0]
    if name == "w_pool":
        return a.reshape(4 * 64, GD)
    if name in TRANSPOSED:
        return a.T
    return a


def _unshard2d(name, a, shape):
    return a.T.reshape(shape) if name in TRANSPOSED else a.reshape(shape)


def _cols_to_slabs(a):
    m, n = a.shape
    return a.reshape(m, N_SHARD, n // N_SHARD).transpose(1, 0, 2)


def _slabs_to_cols(a):
    ns, m, c = a.shape
    return a.transpose(1, 0, 2).reshape(m, ns * c)


def kernel(x, meta_tokens, g_mix, w_in, b_gate, w_dw, b_dw, ln_g, ln_b, w_conv_out, w_pool, pool_scale, w_pool_out, w_o, g_ffn, w_ffn_gate, w_ffn_up, w_ffn_down, g_final, loss_target, m_meta_tokens, m_g_mix, m_w_in, m_b_gate, m_w_dw, m_b_dw, m_ln_g, m_ln_b, m_w_conv_out, m_w_pool, m_pool_scale, m_w_pool_out, m_w_o, m_g_ffn, m_w_ffn_gate, m_w_ffn_up, m_w_ffn_down, m_g_final, v_meta_tokens, v_g_mix, v_w_in, v_b_gate, v_w_dw, v_b_dw, v_ln_g, v_ln_b, v_w_conv_out, v_w_pool, v_pool_scale, v_w_pool_out, v_w_o, v_g_ffn, v_w_ffn_gate, v_w_ffn_up, v_w_ffn_down, v_g_final):
    args = dict(locals())
    w = {n: args[n] for n in WEIGHT_ORDER}
    mom = {n: args["m_" + n] for n in WEIGHT_ORDER}
    var = {n: args["v_" + n] for n in WEIGHT_ORDER}
    seq = x.shape[1]
    nb = seq // BR + 1
    tp = nb * BR
    tk = tp // 2 if (tp // 2) % 16 == 0 else BR
    t_total = seq + N_META
    cx, cy, cc = lax.axis_index("x"), lax.axis_index("y"), lax.axis_index("c")
    chip = 2 * cx + cy
    chip1 = jnp.reshape(chip, (1,)).astype(jnp.int32)
    core = jnp.reshape(cc, (1,)).astype(jnp.int32)
    others = jnp.sort(jnp.stack([2 * (1 - cx) + cy, 2 * cx + (1 - cy), 2 * (1 - cx) + (1 - cy)]))
    idx = jnp.concatenate([chip1, others.astype(jnp.int32), core])
    xs, target = x[0], loss_target[0]

    tiny = _gather_tiny(jnp.concatenate([w["meta_tokens"], w["w_dw"][0], jnp.zeros((1, GD), F32)], axis=0))
    small = {n: w[n] for n in REPL if n != "g_final"}
    small["g_final"] = w["g_final"].reshape(1, D)
    small["w_dw"] = _slabs_to_cols(tiny[:, N_META:])
    head = jnp.concatenate([jnp.zeros((PAD, D), F32), _slabs_to_cols(tiny[:, :N_META])], axis=0)

    def cast(group, dep):
        return [_cast_into_slot(_shard2d(n, w[n]), chip1, dep, "cast_" + n) for n in group]

    def gather_finish(group, start, after, name):
        landed = _gather_wait(start[0], start[1], start[2], after, "gather_wait_" + name)
        return dict(zip(group, _forward_halves(landed, "forward_" + name)))

    st_in = _gather_start(cast(("w_in",), tiny), None, "gather_start_in")
    bufs_mix, bufs_ffn = cast(GROUP_MIX, st_in[3]), cast(GROUP_FFN, st_in[3])
    u = _rms_u(head, xs, small["g_mix"], nb)
    gw = gather_finish(("w_in",), st_in, [u] + bufs_mix + bufs_ffn, "in")
    st_mix = _gather_start(bufs_mix, gw["w_in"], "gather_start_mix")
    z = _in_proj(u, gw["w_in"], st_mix[3], nb)
    gw.update(gather_finish(GROUP_MIX, st_mix, [z], "mix"))
    st_ffn = _gather_start(bufs_ffn, gw["w_o"], "gather_start_ffn")
    w_pool_b = gw["w_pool"].reshape(N_SHARD, 4, 64, GD).transpose(1, 0, 2, 3).reshape(4, GD, GD)
    w_co_b, w_po_b, w_o_b = (gw[n].reshape(D, D) for n in ("w_conv_out", "w_pool_out", "w_o"))
    h1, yc, yp, mg, ca, cpre, m, mw, m2b = _mixers_fwd(
        z, head, xs, small["b_gate"] + st_ffn[3][0, 0], small["w_dw"], small["b_dw"], small["ln_g"], small["ln_b"],
        small["pool_scale"], w_co_b, w_pool_b, w_po_b, w_o_b, nb, t_total)
    gw.update(gather_finish(GROUP_FFN, st_ffn, [h1], "ffn"))

    dh1, dh1b, vb, fb, dgb, dub, dh2b, loss, dg_ffn, dg_final = _ffn_fwd_bwd(
        h1, target, small["g_ffn"], small["g_final"], gw["w_ffn_gate"].reshape(D_FF, D),
        gw["w_ffn_up"].reshape(D_FF, D), gw["w_ffn_down"].reshape(D_FF, D), nb)

    def slabs(name, g):
        if name == "w_in":
            return g
        if name == "w_pool":
            return g.reshape(4, N_SHARD, 64, GD).transpose(1, 0, 2, 3).reshape(N_SHARD, 4 * 64, GD)
        return g.reshape(N_SHARD, g.shape[0] // N_SHARD, g.shape[1])

    def reduce_start(group, grads, name):
        g32 = [slabs(n, grads[n][0]) for n in group]
        g16 = [slabs(n, grads[n][1]) for n in group]
        from_sibling = _swap_halves_bf16(g16, "swap_halves_" + name)
        halves = [_add_sibling_half(g, sb, idx, "add_sibling_" + n) for n, g, sb in zip(group, g32, from_sibling)]
        return [h for h, _ in halves], _scatter_start([hb for _, hb in halves], "scatter_start_" + name)

    def reduce_finish(group, halves, start, after, name):
        from_chips = _scatter_wait(start[0], start[1], start[2], after, "scatter_wait_" + name)
        return [_add_chip_slabs(h, rb, idx, "add_chips_" + n) for n, h, rb in zip(group, halves, from_chips)]

    half_ff = D_FF // 2
    grads_ffn = {
        "w_ffn_gate": _wgrad(dgb, vb, half_ff, D, tk, "wgrad_ffn_gate"),
        "w_ffn_up": _wgrad(dub, vb, half_ff, D, tk, "wgrad_ffn_up"),
        "w_ffn_down": _wgrad(fb, dh2b, half_ff, D, tk, "wgrad_ffn_down"),
    }
    halves_ffn, sc_ffn = reduce_start(GROUP_FFN, grads_ffn, "ffn")

    dycb, dypb, dzg, dconv, dmwb, dm, db_gate, dln_g, dln_b, db_dw, dps = _mixers_bwd_rows(
        dh1b, yc, yp, z, small["b_gate"], cpre, small["ln_g"], small["ln_b"], mw, small["pool_scale"],
        w_o_b, w_co_b, w_po_b, w_pool_b, sc_ffn[3], nb)
    grads_mix = {
        "w_conv_out": _wgrad(ca, dycb, D, D, tk, "wgrad_conv_out"),
        "w_pool": _wgrad(m, dmwb, GD, GD, tk, "wgrad_pool", diag=True),
        "w_pool_out": _wgrad(m2b, dypb, D, D, tk, "wgrad_pool_out"),
        "w_o": _wgrad(mg, dh1b, D, D, tk, "wgrad_o"),
    }
    halves_mix, sc_mix = reduce_start(GROUP_MIX, grads_mix, "mix")
    dzb, grad_x, dhead, dw_dw, dg_mix = _mixers_bwd_halo(
        dconv, dm, z, dzg, small["w_dw"], head, xs, small["g_mix"], dh1, gw["w_in"], sc_mix[3], nb, t_total)
    packed = jnp.concatenate(
        [dg_mix, db_gate.reshape(2, D), db_dw, dln_g, dln_b, dps, dg_ffn, dg_final,
         jnp.broadcast_to(loss, (1, D)), jnp.zeros((6, D), F32), dhead[PAD:], dw_dw], axis=0)
    sm = _small_start(packed, "small_start")
    grads_in = {"w_in": _wgrad(u, dzb, D, D_IN // N_SHARD, tk, "wgrad_in", col_major=True, dep=sm[3])}
    halves_in, sc_in = reduce_start(("w_in",), grads_in, "in")

    land = _small_wait(sm[0], sm[1], sm[2], [sc_in[3]], "small_wait")
    summed = _sum_slots(land, packed, jnp.reshape(4 * cx + 2 * cy + cc, (1,)).astype(jnp.int32))
    loss = summed[9, 0]

    first = GROUP_FFN + GROUP_MIX
    reduced_half = reduce_finish(GROUP_FFN, halves_ffn, sc_ffn, [summed], "ffn")
    reduced_half += reduce_finish(GROUP_MIX, halves_mix, sc_mix, [summed], "mix")
    reduced = dict(zip(first, _join_halves(reduced_half, "join_halves_first")))
    updates = {n: _adamw(reduced[n], _shard2d(n, w[n]), _shard2d(n, mom[n]), _shard2d(n, var[n]), "adamw_" + n)
               for n in first}

    def repl_stack(d):
        return jnp.concatenate([d["g_mix"], d["b_gate"].reshape(2, D), d["b_dw"], d["ln_g"], d["ln_b"],
                                d["pool_scale"], d["g_ffn"], d["g_final"].reshape(1, D), jnp.ones((7, D), F32)], axis=0)

    def shard_stack(d):
        return jnp.concatenate([d["meta_tokens"], d["w_dw"][0], jnp.ones((1, GD), F32)], axis=0)

    g_repl = summed[0:16]
    g_shard = lax.dynamic_slice_in_dim(summed[16:64], chip * GD, GD, axis=1)
    g_repl, d_repl, m_repl, v_repl = _adamw(g_repl, repl_stack(w), repl_stack(mom), repl_stack(var), "adamw_repl")
    g_shard, d_shard, m_shard, v_shard = _adamw(g_shard, shard_stack(w), shard_stack(mom), shard_stack(var), "adamw_cols")

    done_first = [updates[n][1] for n in first] + [d_repl, d_shard]
    last_half = reduce_finish(("w_in",), halves_in, sc_in, done_first, "in")
    reduced["w_in"] = _join_halves(last_half, "join_halves_in")[0]
    updates["w_in"] = _adamw(reduced["w_in"], w["w_in"][0], mom["w_in"][0], var["w_in"][0], "adamw_w_in")

    def unpack(name, repl, shard):
        if name == "meta_tokens":
            return shard[0:N_META]
        if name == "w_dw":
            return shard[N_META:N_META + KW].reshape(1, KW, GD)
        row = {"g_mix": 0, "b_gate": 1, "b_dw": 3, "ln_g": 4, "ln_b": 5, "pool_scale": 6, "g_ffn": 7, "g_final": 8}[name]
        if name == "b_gate":
            return repl[1:3].reshape(1, 2 * D)
        if name == "g_final":
            return repl[8]
        return repl[row:row + 1]

    out_g, out_d, out_m, out_v = {}, {}, {}, {}
    for n in WEIGHT_ORDER:
        if n in BIG:
            g, d_, m_, v_ = updates[n]
            shape = w[n].shape
            out_g[n], out_d[n], out_m[n], out_v[n] = (_unshard2d(n, a, shape) for a in (g, d_, m_, v_))
        else:
            out_g[n] = unpack(n, g_repl, g_shard)
            out_d[n] = unpack(n, d_repl, d_shard)
            out_m[n] = unpack(n, m_repl, m_shard)
            out_v[n] = unpack(n, v_repl, v_shard)
    return (loss, grad_x[None], *[out_g[n] for n in WEIGHT_ORDER], *[out_d[n] for n in WEIGHT_ORDER],
            *[out_m[n] for n in WEIGHT_ORDER], *[out_v[n] for n in WEIGHT_ORDER])
```

```python
import functools

import jax
import jax.numpy as jnp
from jax import lax
from jax.experimental import pallas as pl
from jax.experimental.pallas import tpu as pltpu

F32 = jnp.float32
BF16 = jnp.bfloat16
MESH = pl.DeviceIdType.MESH

D = 1024
N_META = 16
KW = 31
CPAD = KW // 2
POOL_WINDOWS = (2, 4, 8, 16)
GD = 256
D_IN = 5 * D
D_FF = 2816
N_SHARD = 4
BR = 256
HALO = 16
PAD = BR - N_META
EXT = BR + 2 * HALO
RMS_EPS = 1e-6
LN_EPS = 1e-5
LR, B1, B2, ADAM_EPS, WD, STEP = 0.001, 0.9, 0.999, 1e-08, 0.01, 10
VMEM_LIMIT = 56 * 1024 * 1024


def _cparams(sem, vmem=VMEM_LIMIT):
    return pltpu.CompilerParams(dimension_semantics=sem, vmem_limit_bytes=vmem)


def _dot(a, b):
    return jnp.dot(a, b, preferred_element_type=F32)


def _dot_nt(a, b):
    return lax.dot_general(a, b, (((1,), (1,)), ((), ())), preferred_element_type=F32)


def _dot_tn(a, b):
    return lax.dot_general(a, b, (((0,), (0,)), ((), ())), preferred_element_type=F32)


def _sigmoid(x):
    return 0.5 * jnp.tanh(0.5 * x) + 0.5


def _row_ids(i, n, offset=0):
    return lax.broadcasted_iota(jnp.int32, (n, 1), 0) + (i * BR + offset - PAD)


def _pool_cnt(t, w, t_total):
    left = w // 2
    right = w - 1 - left
    lo = jnp.clip(t - left, 0, t_total)
    hi = jnp.clip(t + right + 1, 0, t_total)
    return jnp.maximum(hi - lo, 1).astype(F32)


def _halo_specs(nb, halo_width=D, width=D):
    last = nb * (BR // HALO) - 1
    return [
        pl.BlockSpec((HALO, halo_width), lambda i: (jnp.maximum(i * (BR // HALO) - 1, 0), 0)),
        pl.BlockSpec((BR, width), lambda i: (i, 0)),
        pl.BlockSpec((HALO, halo_width), lambda i: (jnp.minimum((i + 1) * (BR // HALO), last), 0)),
    ]


def _cols(ref, n):
    return [ref.at[:, k * D:(k + 1) * D] for k in range(n)]


def _fill_ext(ext_ref, prev, cur, nxt, i, nb):
    ext_ref[0:HALO, :] = jnp.where(i > 0, prev, 0.0)
    ext_ref[HALO:HALO + BR, :] = cur
    ext_ref[HALO + BR:EXT, :] = jnp.where(i < nb - 1, nxt, 0.0)


ROT_ROWS = EXT - 8


def _fill_rot(rot_ref, ext_ref, lanes):
    for r in range(1, 8):
        rot_ref[r] = ext_ref[pl.ds(r, ROT_ROWS), lanes]


def _tap(rot_ref, ext_ref, lanes, offset):
    q, r = divmod(offset, 8)
    if r == 0:
        return ext_ref[pl.ds(8 * q, BR), lanes]
    return rot_ref[r, pl.ds(8 * q, BR), :]


def _stack_spec(n, width=D):
    return pl.BlockSpec((n, BR, width), lambda i: (0, i, 0))


def _row_spec(width=D):
    return pl.BlockSpec((BR, width), lambda i: (i, 0))


def _x_spec():
    return pl.BlockSpec((BR, D), lambda i: (jnp.maximum(i - 1, 0), 0))


def _const_spec(shape):
    nd = len(shape)
    return pl.BlockSpec(shape, lambda i: (0,) * nd)


def _rms_u(head, x, g_mix, nb):
    def body(head_ref, x_ref, g_ref, u_ref):
        i = pl.program_id(0)
        h = jnp.where(i == 0, head_ref[...], x_ref[...])
        r = lax.rsqrt(jnp.mean(h * h, axis=-1, keepdims=True) + RMS_EPS)
        u_ref[...] = ((h * r) * g_ref[...]).astype(BF16)

    return pl.pallas_call(
        body, name="rms_u",
        grid=(nb,),
        in_specs=[_const_spec((BR, D)), _x_spec(), _const_spec((1, D))],
        out_specs=_row_spec(),
        out_shape=jax.ShapeDtypeStruct((nb * BR, D), BF16),
        compiler_params=_cparams(("arbitrary",)),
    )(head, x, g_mix)


def _in_proj(u, w_in_b, dep, nb):
    tp = nb * BR
    ns = w_in_b.shape[0]
    wcols = w_in_b.shape[2]
    rows = tp // 4 if (tp // 4) % 16 == 0 else BR

    def body(u_ref, w_ref, dep_ref, z_ref):
        z_ref[...] = _dot(u_ref[...], w_ref[...])

    return pl.pallas_call(
        body, name="in_proj",
        grid=(ns, tp // rows),
        in_specs=[
            pl.BlockSpec((rows, D), lambda s, i: (i, 0)),
            pl.BlockSpec((None, D, wcols), lambda s, i: (s, 0, 0)),
            pl.BlockSpec(memory_space=pl.ANY),
        ],
        out_specs=pl.BlockSpec((rows, wcols), lambda s, i: (i, s)),
        out_shape=jax.ShapeDtypeStruct((tp, ns * wcols), F32),
        compiler_params=_cparams(("arbitrary", "arbitrary")),
    )(u, w_in_b, dep)


def _mixers_fwd(z, head, x, b_gate, w_dw, b_dw, ln_g, ln_b, pool_scale, w_co, w_pool, w_po, w_o, nb, t_total):
    tp = nb * BR

    def body(z_prev, z_cur, z_next, head_ref, x_ref, bg_ref, wdw_ref, bdw_ref,
             lng_ref, lnb_ref, ps_ref, wco_ref, wpool_ref, wpo_ref, wo_ref,
             h1_ref, yc_ref, yp_ref, lhs_ref, cpre_ref, m_ref, mw_ref, ext_ref, pext_ref, rot_ref):
        i = pl.program_id(0)
        avp, agp, pp = _cols(z_prev, 3)
        av, ag, pc, za, zb = _cols(z_cur, 5)
        avn, agn, pn = _cols(z_next, 3)
        _fill_ext(ext_ref, avp[...] * _sigmoid(agp[...]), av[...] * _sigmoid(ag[...]),
                  avn[...] * _sigmoid(agn[...]), i, nb)
        _fill_ext(pext_ref, pp[...], pc[...], pn[...], i, nb)

        def conv_chunk(c, carry):
            lanes = pl.ds(pl.multiple_of(c * 128, 128), 128)
            _fill_rot(rot_ref, ext_ref, lanes)
            acc = jnp.broadcast_to(bdw_ref[:, lanes], (BR, 128))
            for k in range(KW):
                acc = acc + wdw_ref[k:k + 1, lanes] * _tap(rot_ref, ext_ref, lanes, 1 + k)
            cpre_ref[:, lanes] = acc
            return carry
        lax.fori_loop(0, D // 128, conv_chunk, 0)

        conv = cpre_ref[...]
        mu = jnp.mean(conv, axis=-1, keepdims=True)
        xc = conv - mu
        rstd = lax.rsqrt(jnp.mean(xc * xc, axis=-1, keepdims=True) + LN_EPS)
        ln = (xc * rstd) * lng_ref[...] + lnb_ref[...]
        cact = (ln * _sigmoid(ln)).astype(BF16)
        lhs_ref[0] = cact
        y_conv = _dot(cact, wco_ref[...])
        yc_ref[...] = y_conv

        t = _row_ids(i, BR)
        for gi, w in enumerate(POOL_WINDOWS):
            left = w // 2
            right = w - 1 - left
            lanes = slice(gi * GD, (gi + 1) * GD)
            s = pext_ref[pl.ds(HALO - left, BR), lanes]
            for j in range(-left + 1, right + 1):
                s = s + pext_ref[pl.ds(HALO + j, BR), lanes]
            m = (s / _pool_cnt(t, w, t_total) - pext_ref[HALO:HALO + BR, lanes]).astype(BF16)
            m_ref[:, lanes] = m
            mw_ref[:, lanes] = _dot(m, wpool_ref[gi])
        mw = mw_ref[...]
        m2b = (mw * ps_ref[...]).astype(BF16)
        lhs_ref[1] = m2b
        y_pool = _dot(m2b, wpo_ref[...])
        yp_ref[...] = y_pool

        s_a = _sigmoid(za[...] + bg_ref[:, 0:D])
        s_b = _sigmoid(zb[...] + bg_ref[:, D:2 * D])
        merged = (s_a * y_conv + s_b * y_pool).astype(BF16)
        lhs_ref[2] = merged
        h0 = jnp.where(i == 0, head_ref[...], x_ref[...])
        h1_ref[...] = h0 + _dot(merged, wo_ref[...])

    in_specs = (_halo_specs(nb, 3 * D, 5 * D)
                + [_const_spec((BR, D)), _x_spec(), _const_spec((1, 2 * D)), _const_spec((32, D)),
                   _const_spec((1, D)), _const_spec((1, D)), _const_spec((1, D)), _const_spec((1, D)),
                   _const_spec((D, D)), _const_spec((4, GD, GD)), _const_spec((D, D)), _const_spec((D, D))])
    outs = [F32, F32, F32, None, F32, BF16, F32]
    return pl.pallas_call(
        body, name="mixers_fwd",
        grid=(nb,),
        in_specs=in_specs,
        out_specs=[_row_spec() if dt else _stack_spec(3) for dt in outs],
        out_shape=[jax.ShapeDtypeStruct((tp, D), dt) if dt else jax.ShapeDtypeStruct((3, tp, D), BF16) for dt in outs],
        scratch_shapes=[pltpu.VMEM((EXT, D), F32), pltpu.VMEM((EXT, D), F32), pltpu.VMEM((8, ROT_ROWS, 128), F32)],
        compiler_params=_cparams(("arbitrary",)),
    )(z, z, z, head, x, b_gate, w_dw, b_dw, ln_g, ln_b, pool_scale, w_co, w_pool, w_po, w_o)


def _ffn_fwd_bwd(h1, target, g_ffn, g_final, w_g, w_u, w_d, nb):
    tp = nb * BR

    def body(h1_ref, tgt_ref, gf_ref, gfin_ref, wg_hbm, wu_hbm, wd_hbm,
             dh1_ref, dh1b_ref, rhs_ref, lhs_ref, loss_ref, dgf_ref, dgfin_ref,
             wg_ref, wu_ref, wd_ref, sem):
        i = pl.program_id(0)

        @pl.when(i == 0)
        def _():
            copies = [pltpu.make_async_copy(wg_hbm, wg_ref, sem.at[0]),
                      pltpu.make_async_copy(wu_hbm, wu_ref, sem.at[1]),
                      pltpu.make_async_copy(wd_hbm, wd_ref, sem.at[2])]
            for cp in copies:
                cp.start()
            loss_ref[...] = jnp.zeros_like(loss_ref)
            dgf_ref[...] = jnp.zeros_like(dgf_ref)
            dgfin_ref[...] = jnp.zeros_like(dgfin_ref)
            for cp in copies:
                cp.wait()

        h1 = h1_ref[...]
        r1 = lax.rsqrt(jnp.mean(h1 * h1, axis=-1, keepdims=True) + RMS_EPS)
        vn = h1 * r1
        vb = (vn * gf_ref[...]).astype(BF16)
        rhs_ref[0] = vb
        g = _dot_nt(vb, wg_ref[...])
        up = _dot_nt(vb, wu_ref[...])
        sg = _sigmoid(g)
        sl = g * sg
        fb = (sl * up).astype(BF16)
        lhs_ref[2] = fb
        h2 = h1 + _dot(fb, wd_ref[...])
        r2 = lax.rsqrt(jnp.mean(h2 * h2, axis=-1, keepdims=True) + RMS_EPS)
        yn = h2 * r2
        valid = i > 0
        diff = jnp.where(valid, yn * gfin_ref[...] - tgt_ref[...], 0.0)
        loss_ref[...] += 0.5 * jnp.sum(jnp.mean(diff * diff, axis=-1, keepdims=True))
        dy = diff * (1.0 / D)
        dgfin_ref[...] += jnp.sum(dy * yn, axis=0, keepdims=True)
        gd = dy * gfin_ref[...]
        dh2 = r2 * (gd - yn * jnp.mean(yn * gd, axis=-1, keepdims=True))
        dh2b = dh2.astype(BF16)
        rhs_ref[1] = dh2b
        df = _dot_nt(dh2b, wd_ref[...])
        dub = (df * sl).astype(BF16)
        dgb = (df * up * (sg * (1.0 + g * (1.0 - sg)))).astype(BF16)
        lhs_ref[1] = dub
        lhs_ref[0] = dgb
        dv = _dot(dgb, wg_ref[...]) + _dot(dub, wu_ref[...])
        dgf_ref[...] += jnp.sum(dv * vn, axis=0, keepdims=True)
        gd1 = dv * gf_ref[...]
        dh1 = dh2 + r1 * (gd1 - vn * jnp.mean(vn * gd1, axis=-1, keepdims=True))
        dh1_ref[...] = dh1
        dh1b_ref[...] = dh1.astype(BF16)

    any_spec = pl.BlockSpec(memory_space=pl.ANY)
    return pl.pallas_call(
        body, name="ffn_fwd_bwd",
        grid=(nb,),
        in_specs=[_row_spec(), _x_spec(), _const_spec((1, D)), _const_spec((1, D)), any_spec, any_spec, any_spec],
        out_specs=[_row_spec(), _row_spec(), _stack_spec(2), _stack_spec(3, D_FF),
                   _const_spec((1, 1)), _const_spec((1, D)), _const_spec((1, D))],
        out_shape=[jax.ShapeDtypeStruct((tp, D), F32), jax.ShapeDtypeStruct((tp, D), BF16),
                   jax.ShapeDtypeStruct((2, tp, D), BF16), jax.ShapeDtypeStruct((3, tp, D_FF), BF16),
                   jax.ShapeDtypeStruct((1, 1), F32),
                   jax.ShapeDtypeStruct((1, D), F32), jax.ShapeDtypeStruct((1, D), F32)],
        scratch_shapes=[pltpu.VMEM((D_FF, D), BF16), pltpu.VMEM((D_FF, D), BF16), pltpu.VMEM((D_FF, D), BF16),
                        pltpu.SemaphoreType.DMA((3,))],
        compiler_params=_cparams(("arbitrary",)),
    )(h1, target, g_ffn, g_final, w_g, w_u, w_d)


def _mixers_bwd_rows(dh1b, yc, yp, z, b_gate, cpre, ln_g, ln_b, mw, pool_scale, w_o, w_co, w_po, w_pool, dep, nb):
    tp = nb * BR

    def body(dh1b_ref, yc_ref, yp_ref, za, zb, bg_ref, cpre_ref, lng_ref, lnb_ref, mw_ref, ps_ref,
             wo_ref, wco_ref, wpo_ref, wpool_ref, dep_ref,
             rhs_ref, dzg_ref, dconv_ref, dmwb_ref, dm_ref, dbg_ref, dlng_ref, dlnb_ref, dbdw_ref, dps_ref):
        i = pl.program_id(0)

        @pl.when(i == 0)
        def _():
            for r in (dbg_ref, dlng_ref, dlnb_ref, dbdw_ref, dps_ref):
                r[...] = jnp.zeros_like(r)

        dh1b = dh1b_ref[...]
        rhs_ref[2] = dh1b
        dmg = _dot_nt(dh1b, wo_ref[...])
        s_a = _sigmoid(za[...] + bg_ref[:, 0:D])
        s_b = _sigmoid(zb[...] + bg_ref[:, D:2 * D])
        dycb = (dmg * s_a).astype(BF16)
        dypb = (dmg * s_b).astype(BF16)
        rhs_ref[0] = dycb
        rhs_ref[1] = dypb
        dza = dmg * yc_ref[...] * (s_a * (1.0 - s_a))
        dzb = dmg * yp_ref[...] * (s_b * (1.0 - s_b))
        dzg_ref[:, 0:D] = dza.astype(BF16)
        dzg_ref[:, D:2 * D] = dzb.astype(BF16)
        dbg_ref[:, 0:D] += jnp.sum(dza, axis=0, keepdims=True)
        dbg_ref[:, D:2 * D] += jnp.sum(dzb, axis=0, keepdims=True)

        dca = _dot_nt(dycb, wco_ref[...])
        conv = cpre_ref[...]
        mu = jnp.mean(conv, axis=-1, keepdims=True)
        xc = conv - mu
        rstd = lax.rsqrt(jnp.mean(xc * xc, axis=-1, keepdims=True) + LN_EPS)
        xhat = xc * rstd
        ln = xhat * lng_ref[...] + lnb_ref[...]
        sg = _sigmoid(ln)
        dln = dca * (sg * (1.0 + ln * (1.0 - sg)))
        dlng_ref[...] += jnp.sum(dln * xhat, axis=0, keepdims=True)
        dlnb_ref[...] += jnp.sum(dln, axis=0, keepdims=True)
        dxh = dln * lng_ref[...]
        dconv = rstd * (dxh - jnp.mean(dxh, axis=-1, keepdims=True)
                        - xhat * jnp.mean(dxh * xhat, axis=-1, keepdims=True))
        dconv_ref[...] = dconv
        dbdw_ref[...] += jnp.sum(dconv, axis=0, keepdims=True)

        dm2 = _dot_nt(dypb, wpo_ref[...])
        dps_ref[...] += jnp.sum(dm2 * mw_ref[...], axis=0, keepdims=True)
        dmwb = (dm2 * ps_ref[...]).astype(BF16)
        dmwb_ref[...] = dmwb
        for gi in range(len(POOL_WINDOWS)):
            lanes = slice(gi * GD, (gi + 1) * GD)
            dm_ref[:, lanes] = _dot_nt(dmwb[:, lanes], wpool_ref[gi])

    in_specs = [_row_spec(), _row_spec(), _row_spec(),
                pl.BlockSpec((BR, D), lambda i: (i, 3)), pl.BlockSpec((BR, D), lambda i: (i, 4)),
                _const_spec((1, 2 * D)), _row_spec(), _const_spec((1, D)), _const_spec((1, D)), _row_spec(),
                _const_spec((1, D)), _const_spec((D, D)), _const_spec((D, D)), _const_spec((D, D)),
                _const_spec((4, GD, GD)), pl.BlockSpec(memory_space=pl.ANY)]
    return pl.pallas_call(
        body, name="mixers_bwd_rows",
        grid=(nb,),
        in_specs=in_specs,
        out_specs=[_stack_spec(3), _row_spec(2 * D), _row_spec(), _row_spec(), _row_spec(),
                   _const_spec((1, 2 * D)), _const_spec((1, D)), _const_spec((1, D)), _const_spec((1, D)),
                   _const_spec((1, D))],
        out_shape=[jax.ShapeDtypeStruct((3, tp, D), BF16),
                   jax.ShapeDtypeStruct((tp, 2 * D), BF16), jax.ShapeDtypeStruct((tp, D), F32),
                   jax.ShapeDtypeStruct((tp, D), BF16), jax.ShapeDtypeStruct((tp, D), F32),
                   jax.ShapeDtypeStruct((1, 2 * D), F32), jax.ShapeDtypeStruct((1, D), F32),
                   jax.ShapeDtypeStruct((1, D), F32), jax.ShapeDtypeStruct((1, D), F32),
                   jax.ShapeDtypeStruct((1, D), F32)],
        compiler_params=_cparams(("arbitrary",)),
    )(dh1b, yc, yp, z, z, b_gate, cpre, ln_g, ln_b, mw, pool_scale, w_o, w_co, w_po, w_pool, dep)


def _mixers_bwd_halo(dconv, dm, z, dzg, w_dw, head, x, g_mix, dh1, w_in_b, dep, nb, t_total):
    tp = nb * BR
    ns = w_in_b.shape[0]
    wcols = w_in_b.shape[2]
    seq = x.shape[0]

    def body(dcp, dcc, dcn, dmp, dmc, dmn, z_prev, z_cur, z_next, dzg_ref, wdw_ref, head_ref, x_ref, g_ref,
             dh1_ref, w_hbm, dep_ref,
             dzb_ref, gx_ref, dhead_ref, dwdw_ref, dgmix_ref,
             w_ref, sem, aext_ref, dext_ref, qext_ref, da_ref, rot_ref, dwp_ref):
        i = pl.program_id(0)
        (avp, agp), (av, ag), (avn, agn) = _cols(z_prev, 2), _cols(z_cur, 2), _cols(z_next, 2)

        @pl.when(i == 0)
        def _():
            cp = pltpu.make_async_copy(w_hbm, w_ref, sem.at[0])
            cp.start()
            dwp_ref[...] = jnp.zeros_like(dwp_ref)
            dgmix_ref[...] = jnp.zeros_like(dgmix_ref)
            cp.wait()

        sig_g = _sigmoid(ag[...])
        _fill_ext(aext_ref, avp[...] * _sigmoid(agp[...]), av[...] * sig_g, avn[...] * _sigmoid(agn[...]), i, nb)
        _fill_ext(dext_ref, dcp[...], dcc[...], dcn[...], i, nb)
        _fill_ext(qext_ref, dmp[...], dmc[...], dmn[...], i, nb)

        def conv_chunk(c, carry):
            lanes = pl.ds(pl.multiple_of(c * 128, 128), 128)
            _fill_rot(rot_ref, dext_ref, lanes)
            acc = jnp.zeros((BR, 128), F32)
            for k in range(KW):
                acc = acc + wdw_ref[k:k + 1, lanes] * _tap(rot_ref, dext_ref, lanes, KW - k)
            da_ref[:, lanes] = acc
            _fill_rot(rot_ref, aext_ref, lanes)
            dcv = dext_ref[HALO:HALO + BR, lanes]
            for k in range(KW):
                prod = _tap(rot_ref, aext_ref, lanes, 1 + k) * dcv
                dwp_ref[k, :, lanes] += jnp.sum(prod.reshape(BR // 8, 8, 128), axis=0)
            return carry
        lax.fori_loop(0, D // 128, conv_chunk, 0)

        @pl.when(i == nb - 1)
        def _():
            dwdw_ref[...] = jnp.sum(dwp_ref[...], axis=1)

        da = da_ref[...]
        a_val = av[...]
        dzb_ref[:, 0:D] = (da * sig_g).astype(BF16)
        dzb_ref[:, D:2 * D] = (da * a_val * (sig_g * (1.0 - sig_g))).astype(BF16)

        t_ext = _row_ids(i, EXT, -HALO)
        for gi, w in enumerate(POOL_WINDOWS):
            left = w // 2
            right = w - 1 - left
            lanes = slice(gi * GD, (gi + 1) * GD)
            qext_ref[:, lanes] = qext_ref[:, lanes] / _pool_cnt(t_ext, w, t_total)
            s = qext_ref[pl.ds(HALO - right, BR), lanes]
            for j in range(-right + 1, left + 1):
                s = s + qext_ref[pl.ds(HALO + j, BR), lanes]
            dzb_ref[:, 2 * D + gi * GD:2 * D + (gi + 1) * GD] = (s - dmc[:, lanes]).astype(BF16)
        dzb_ref[:, 3 * D:5 * D] = dzg_ref[...]

        du = _dot_nt(dzb_ref[:, 0:wcols], w_ref[0])
        for s_i in range(1, ns):
            du = du + _dot_nt(dzb_ref[:, s_i * wcols:(s_i + 1) * wcols], w_ref[s_i])
        h0 = jnp.where(i == 0, head_ref[...], x_ref[...])
        r0 = lax.rsqrt(jnp.mean(h0 * h0, axis=-1, keepdims=True) + RMS_EPS)
        un = h0 * r0
        dgmix_ref[...] += jnp.sum(du * un, axis=0, keepdims=True)
        gd = du * g_ref[...]
        dh0 = dh1_ref[...] + r0 * (gd - un * jnp.mean(un * gd, axis=-1, keepdims=True))
        gx_ref[...] = dh0

        @pl.when(i == 0)
        def _():
            dhead_ref[...] = dh0

    any_spec = pl.BlockSpec(memory_space=pl.ANY)
    in_specs = (_halo_specs(nb) + _halo_specs(nb) + _halo_specs(nb, 2 * D, 2 * D)
                + [_row_spec(2 * D), _const_spec((32, D)), _const_spec((BR, D)), _x_spec(), _const_spec((1, D)),
                   _row_spec(), any_spec, any_spec])
    return pl.pallas_call(
        body, name="mixers_bwd_halo",
        grid=(nb,),
        in_specs=in_specs,
        out_specs=[_row_spec(D_IN), _x_spec(), _const_spec((BR, D)), _const_spec((32, D)), _const_spec((1, D))],
        out_shape=[jax.ShapeDtypeStruct((tp, D_IN), BF16), jax.ShapeDtypeStruct((seq, D), F32),
                   jax.ShapeDtypeStruct((BR, D), F32), jax.ShapeDtypeStruct((32, D), F32),
                   jax.ShapeDtypeStruct((1, D), F32)],
        scratch_shapes=[pltpu.VMEM((ns, D, wcols), BF16), pltpu.SemaphoreType.DMA((1,)),
                        pltpu.VMEM((EXT, D), F32), pltpu.VMEM((EXT, D), F32), pltpu.VMEM((EXT, D), F32),
                        pltpu.VMEM((BR, D), F32), pltpu.VMEM((8, ROT_ROWS, 128), F32), pltpu.VMEM((32, 8, D), F32)],
        compiler_params=_cparams(("arbitrary",)),
    )(dconv, dconv, dconv, dm, dm, dm, z, z, z, dzg, w_dw, head, x, g_mix, dh1, w_in_b, dep)


def _wgrad(a, c, tm, tn, tk, name, diag=False, col_major=False, dep=None):
    tp, m = a.shape
    n = c.shape[1]
    nk = tp // tk
    gm, gn = m // tm, n // tn

    def body(a_ref, c_ref, *rest):
        o_ref, ob_ref = rest[-2:]
        k = pl.program_id(2)

        @pl.when(k == 0)
        def _():
            o_ref[...] = jnp.zeros_like(o_ref)

        o_ref[...] += _dot_tn(a_ref[...], c_ref[...])

        @pl.when(k == nk - 1)
        def _():
            ob_ref[...] = o_ref[...].astype(BF16)

    c_map = lambda i, j, k: (k, j)
    grid = (gm, gn, nk)
    deps = [] if dep is None else [dep]
    if diag:
        grid = (gm, 1, nk)
        c_map = lambda i, j, k: (k, i)
        o_spec = pl.BlockSpec((tm, tn), lambda i, j, k: (i, 0))
        o_shape = (m, tn)
    elif col_major:
        o_spec = pl.BlockSpec((None, tm, tn), lambda i, j, k: (j, i, 0))
        o_shape = (gn, m, tn)
    else:
        o_spec = pl.BlockSpec((tm, tn), lambda i, j, k: (i, j))
        o_shape = (m, n)
    return pl.pallas_call(
        body, name=name,
        grid=grid,
        in_specs=[pl.BlockSpec((tk, tm), lambda i, j, k: (k, i)), pl.BlockSpec((tk, tn), c_map)]
        + [pl.BlockSpec(memory_space=pl.ANY)] * len(deps),
        out_specs=[o_spec, o_spec],
        out_shape=[jax.ShapeDtypeStruct(o_shape, F32), jax.ShapeDtypeStruct(o_shape, BF16)],
        compiler_params=_cparams(("arbitrary", "arbitrary", "arbitrary")),
    )(a, c, *deps)


def _wgrad_stack(a, c, c_div, tm, tn, tk, name):
    ns, tp, m = a.shape
    n = c.shape[2]
    nk = tp // tk
    gm, gn = m // tm, n // tn

    def body(a_ref, c_ref, *outs):
        s, k = pl.program_id(0), pl.program_id(3)
        for w in range(ns):
            @pl.when(s == w)
            def _():
                o_ref, ob_ref = outs[w], outs[ns + w]

                @pl.when(k == 0)
                def _():
                    o_ref[...] = jnp.zeros_like(o_ref)

                o_ref[...] += _dot_tn(a_ref[...], c_ref[...])

                @pl.when(k == nk - 1)
                def _():
                    ob_ref[...] = o_ref[...].astype(BF16)

    def o_spec(w):
        def index(s, i, j, k):
            return (jnp.where(s < w, 0, jnp.where(s > w, gm - 1, i)), jnp.where(s < w, 0, jnp.where(s > w, gn - 1, j)))
        return pl.BlockSpec((tm, tn), index)

    out = pl.pallas_call(
        body, name=name,
        grid=(ns, gm, gn, nk),
        in_specs=[pl.BlockSpec((None, tk, tm), lambda s, i, j, k: (s, k, i)),
                  pl.BlockSpec((None, tk, tn), lambda s, i, j, k: (s // c_div, k, j))],
        out_specs=[o_spec(w) for w in range(ns)] * 2,
        out_shape=[jax.ShapeDtypeStruct((m, n), F32)] * ns + [jax.ShapeDtypeStruct((m, n), BF16)] * ns,
        compiler_params=_cparams(("arbitrary",) * 4),
    )(a, c)
    return [(out[w], out[ns + w]) for w in range(ns)]


def _place():
    x, y, c = lax.axis_index("x"), lax.axis_index("y"), lax.axis_index("c")
    others = [(1 - x, y), (x, 1 - y), (1 - x, 1 - y)]
    return x, y, c, others


def _split2(a, axis=0):
    return a.reshape(a.shape[:axis] + (2, a.shape[axis] // 2) + a.shape[axis + 1:])


def _merge2(a, axis=0):
    return a.reshape(a.shape[:axis] + (2 * a.shape[axis + 1],) + a.shape[axis + 2:])


def _cast_into_slot(w2d, chip, dep, name):
    r, c = w2d.shape
    r2 = r // 2

    def body(chip_ref, w_ref, dep_ref, o_ref):
        o_ref[...] = w_ref[...].astype(BF16)

    return pl.pallas_call(
        body, name=name,
        grid_spec=pltpu.PrefetchScalarGridSpec(
            num_scalar_prefetch=1, grid=(2,),
            in_specs=[pl.BlockSpec((r2, c), lambda h, chip_ref: (h, 0)), pl.BlockSpec(memory_space=pl.ANY)],
            out_specs=pl.BlockSpec((None, None, r2, c), lambda h, chip_ref: (chip_ref[0], h, 0, 0))),
        out_shape=jax.ShapeDtypeStruct((N_SHARD, 2, r2, c), BF16),
        compiler_params=_cparams(("arbitrary",)),
    )(chip, w2d, dep)


HBM_SPEC = pl.BlockSpec(memory_space=pltpu.HBM)
SEM_SPEC = pl.BlockSpec(memory_space=pltpu.SEMAPHORE)
DATAFLOW = pltpu.SideEffectType.DATAFLOW_SIDE_EFFECTING
TOKEN = jax.ShapeDtypeStruct((8, 128), F32)


def _in_hbm(a):
    return pltpu.with_memory_space_constraint(a, pltpu.HBM)


def _gather_tiny(v):
    vm = pl.BlockSpec(memory_space=pltpu.VMEM)

    def body(v_ref, out_ref, send_sems, recv_sems):
        x, y, c, others = _place()
        mine = 2 * x + y
        sends = [pltpu.make_async_remote_copy(
            src_ref=v_ref, dst_ref=out_ref.at[mine], send_sem=send_sems.at[j], recv_sem=recv_sems.at[j],
            device_id=(*chip, c), device_id_type=MESH) for j, chip in enumerate(others)]
        for cp in sends:
            cp.start()
        out_ref[mine] = v_ref[...]
        for j, chip in enumerate(others):
            landed = out_ref.at[2 * chip[0] + chip[1]]
            pltpu.make_async_remote_copy(
                src_ref=landed, dst_ref=landed, send_sem=send_sems.at[j], recv_sem=recv_sems.at[j],
                device_id=(x, y, c), device_id_type=MESH).wait_recv()
        for cp in sends:
            cp.wait_send()

    return pl.pallas_call(
        body, name="gather_tiny",
        in_specs=[vm], out_specs=vm,
        out_shape=jax.ShapeDtypeStruct((N_SHARD,) + v.shape, v.dtype),
        scratch_shapes=[pltpu.SemaphoreType.DMA((3,)), pltpu.SemaphoreType.DMA((3,))],
    )(v)


def _ici_copies(srcs, dsts, send_sems, recv_sems, started):
    x, y, c, others = _place()
    mine = 2 * x + y
    copies = []
    for a in range(len(srcs)):
        for j, chip in enumerate(others):
            there = 2 * chip[0] + chip[1]
            src, dst = srcs[a](mine, there, c), dsts[a](mine, there, c)
            if not started:
                dst = dsts[a](there, mine, c)
            copies.append(pltpu.make_async_remote_copy(
                src_ref=src, dst_ref=dst, send_sem=send_sems.at[a * 3 + j], recv_sem=recv_sems.at[a * 3 + j],
                device_id=(*chip, c), device_id_type=MESH))
    return copies


def _split_start(srcs_of, dsts_of, arrays, n_src, name, copies_of=None, n_sems=None, dep=None):
    n = len(arrays)
    n_sems = n_sems or 3 * n_src
    copies_of = copies_of or (lambda ins, ss, rs, started: _ici_copies(srcs_of(ins), dsts_of(ins), ss, rs, started))
    deps = [] if dep is None else [dep]
    nd = len(deps)

    def body(*refs):
        ins = refs[:n]
        send_sems, recv_sems = refs[n + nd], refs[n + nd + 1]
        token = refs[2 * n + nd + 2]
        for cp in copies_of(ins, send_sems, recv_sems, True):
            cp.start()
        token[...] = jnp.zeros_like(token)

    out = pl.pallas_call(
        body, name=name,
        in_specs=[HBM_SPEC] * n + [pl.BlockSpec(memory_space=pl.ANY)] * nd,
        out_specs=(SEM_SPEC, SEM_SPEC, *([HBM_SPEC] * n), pl.BlockSpec(memory_space=pltpu.VMEM)),
        out_shape=(pltpu.SemaphoreType.DMA((n_sems,)), pltpu.SemaphoreType.DMA((n_sems,)),
                   *[pltpu.HBM(a.shape, a.dtype) for a in arrays], TOKEN),
        input_output_aliases={a: 2 + a for a in range(n)},
        compiler_params=pltpu.CompilerParams(has_side_effects=DATAFLOW),
    )(*[_in_hbm(a) for a in arrays], *deps)
    return out[0], out[1], list(out[2:2 + n]), out[2 + n]


def _split_wait(srcs_of, dsts_of, send_sems, recv_sems, arrays, after, name, copies_of=None):
    n = len(arrays)
    copies_of = copies_of or (lambda ins, ss, rs, started: _ici_copies(srcs_of(ins), dsts_of(ins), ss, rs, started))

    def body(*refs):
        ins = refs[:n]
        send_sems, recv_sems = refs[n], refs[n + 1]
        for cp in copies_of(ins, send_sems, recv_sems, False):
            cp.wait_send()
            cp.wait_recv()

    return pl.pallas_call(
        body, name=name,
        in_specs=[HBM_SPEC] * n + [SEM_SPEC, SEM_SPEC] + [pl.BlockSpec(memory_space=pl.ANY)] * len(after),
        out_specs=[HBM_SPEC] * n,
        out_shape=[pltpu.HBM(a.shape, a.dtype) for a in arrays],
        input_output_aliases={a: a for a in range(n)},
        compiler_params=pltpu.CompilerParams(has_side_effects=DATAFLOW),
    )(*arrays, send_sems, recv_sems, *after)


def _gather_views(ins):
    view = [lambda frm, to, c, r=r: r.at[frm, c] for r in ins]
    return view


def _gather_start(bufs, dep, name):
    return _split_start(_gather_views, _gather_views, bufs, len(bufs), name, dep=dep)


def _gather_wait(send_sems, recv_sems, bufs, after, name):
    return _split_wait(_gather_views, _gather_views, send_sems, recv_sems, bufs, after, name)


def _forward_halves(bufs, name):
    n = len(bufs)
    any_spec = pl.BlockSpec(memory_space=pl.ANY)

    def body(*refs):
        outs = refs[n:2 * n]
        send_sems, recv_sems = refs[2 * n:]
        x, y, c, others = _place()
        copies = []
        for a in range(n):
            for j, chip in enumerate(others):
                landed = outs[a].at[2 * chip[0] + chip[1], c]
                copies.append(pltpu.make_async_remote_copy(
                    src_ref=landed, dst_ref=landed, send_sem=send_sems.at[a * 3 + j], recv_sem=recv_sems.at[a * 3 + j],
                    device_id=(x, y, 1 - c), device_id_type=MESH))
        for cp in copies:
            cp.start()
        for a in range(n):
            for j, chip in enumerate(others):
                landed = outs[a].at[2 * chip[0] + chip[1], 1 - c]
                pltpu.make_async_remote_copy(
                    src_ref=landed, dst_ref=landed, send_sem=send_sems.at[a * 3 + j], recv_sem=recv_sems.at[a * 3 + j],
                    device_id=(x, y, c), device_id_type=MESH).wait_recv()
        for cp in copies:
            cp.wait_send()

    out = pl.pallas_call(
        body, name=name,
        in_specs=[any_spec] * n, out_specs=[any_spec] * n,
        out_shape=[jax.ShapeDtypeStruct(b.shape, b.dtype) for b in bufs],
        input_output_aliases={a: a for a in range(n)},
        scratch_shapes=[pltpu.SemaphoreType.DMA((3 * n,)), pltpu.SemaphoreType.DMA((3 * n,))],
    )(*bufs)
    return [_merge2(o, 1) for o in out]


def _swap_halves_bf16(gbs, name):
    n = len(gbs)
    any_spec = pl.BlockSpec(memory_space=pl.ANY)

    def body(*refs):
        ins, outs = refs[:n], refs[n:2 * n]
        send_sems, recv_sems = refs[2 * n:]
        x, y, c, _ = _place()
        copies = []
        for a in range(n):
            copies.append(pltpu.make_async_remote_copy(
                src_ref=ins[a].at[:, 1 - c], dst_ref=outs[a], send_sem=send_sems.at[a], recv_sem=recv_sems.at[a],
                device_id=(x, y, 1 - c), device_id_type=MESH))
        for cp in copies:
            cp.start()
        for cp in copies:
            cp.wait()

    return pl.pallas_call(
        body, name=name,
        in_specs=[any_spec] * n, out_specs=[any_spec] * n,
        out_shape=[jax.ShapeDtypeStruct((g.shape[0], g.shape[1] // 2, g.shape[2]), g.dtype) for g in gbs],
        scratch_shapes=[pltpu.SemaphoreType.DMA((n,)), pltpu.SemaphoreType.DMA((n,))],
    )(*[_split2(g, 1) for g in gbs])


def _scatter_srcs(n):
    return lambda ins: [lambda frm, to, c, r=r: r.at[to] for r in ins[:n]]


def _scatter_dsts(n):
    return lambda ins: [lambda frm, to, c, r=r: r.at[frm] for r in ins[n:]]


def _scatter_start(hbs, name):
    n = len(hbs)
    lands = [lax.empty(h.shape, h.dtype) for h in hbs]
    return _split_start(_scatter_srcs(n), _scatter_dsts(n), list(hbs) + lands, n, name)


def _scatter_wait(send_sems, recv_sems, arrays, after, name):
    n = len(arrays) // 2
    return _split_wait(_scatter_srcs(n), _scatter_dsts(n), send_sems, recv_sems, arrays, after, name)[n:]


def _join_halves(rhs, name):
    n = len(rhs)
    any_spec = pl.BlockSpec(memory_space=pl.ANY)

    def body(*refs):
        outs = refs[n:2 * n]
        send_sems, recv_sems = refs[2 * n:]
        x, y, c, _ = _place()
        copies = []
        for a in range(n):
            copies.append(pltpu.make_async_remote_copy(
                src_ref=outs[a].at[c], dst_ref=outs[a].at[c], send_sem=send_sems.at[a],
                recv_sem=recv_sems.at[a], device_id=(x, y, 1 - c), device_id_type=MESH))
        for cp in copies:
            cp.start()
        for a in range(n):
            landed = outs[a].at[1 - c]
            pltpu.make_async_remote_copy(
                src_ref=landed, dst_ref=landed, send_sem=send_sems.at[a], recv_sem=recv_sems.at[a],
                device_id=(x, y, c), device_id_type=MESH).wait_recv()
        for cp in copies:
            cp.wait_send()

    out = pl.pallas_call(
        body, name=name,
        in_specs=[any_spec] * n, out_specs=[any_spec] * n,
        out_shape=[jax.ShapeDtypeStruct(r.shape, r.dtype) for r in rhs],
        input_output_aliases={a: a for a in range(n)},
        scratch_shapes=[pltpu.SemaphoreType.DMA((n,)), pltpu.SemaphoreType.DMA((n,))],
    )(*rhs)
    return [_merge2(o) for o in out]


FLIPS = [(dx, dy, dc) for dx in (0, 1) for dy in (0, 1) for dc in (0, 1)][1:]


def _peer_copies(ins, send_sems, recv_sems, started):
    x, y, c, _ = _place()
    copies = []
    for k, (dx, dy, dc) in enumerate(FLIPS):
        px, py, pc = jnp.bitwise_xor(x, dx), jnp.bitwise_xor(y, dy), jnp.bitwise_xor(c, dc)
        slot = 4 * x + 2 * y + c if started else 4 * px + 2 * py + pc
        copies.append(pltpu.make_async_remote_copy(
            src_ref=ins[0], dst_ref=ins[1].at[slot], send_sem=send_sems.at[k], recv_sem=recv_sems.at[k],
            device_id=(px, py, pc), device_id_type=MESH))
    return copies


def _small_start(v, name):
    land = lax.empty((8,) + v.shape, v.dtype)
    return _split_start(None, None, [v, land], 0, name, copies_of=_peer_copies, n_sems=len(FLIPS))


def _small_wait(send_sems, recv_sems, arrays, after, name):
    return _split_wait(None, None, send_sems, recv_sems, arrays, after, name, copies_of=_peer_copies)[1]


def _sum_slots(land, v, me):
    rows, cols = v.shape

    def body(me_ref, land_ref, v_ref, o_ref):
        o_ref[...] = jnp.zeros_like(o_ref)
        for d in range(8):
            @pl.when(me_ref[0] == d)
            def _():
                o_ref[...] += v_ref[...]

            @pl.when(me_ref[0] != d)
            def _():
                o_ref[...] += land_ref[d]

    return pl.pallas_call(
        body, name="sum_slots",
        grid_spec=pltpu.PrefetchScalarGridSpec(
            num_scalar_prefetch=1, grid=(1,),
            in_specs=[pl.BlockSpec((8, rows, cols), lambda i, me_ref: (0, 0, 0)),
                      pl.BlockSpec((rows, cols), lambda i, me_ref: (0, 0))],
            out_specs=pl.BlockSpec((rows, cols), lambda i, me_ref: (0, 0))),
        out_shape=jax.ShapeDtypeStruct((rows, cols), F32),
        compiler_params=_cparams(("arbitrary",)),
    )(me, land, v)


def _row_block(r):
    for cand in (512, 352, 256, 128, 64, 48, 16):
        if r % cand == 0:
            return cand
    return r


def _add_sibling_half(g, sb, idx, name):
    ns, r, c = g.shape
    r2 = r // 2

    def body(idx_ref, g_ref, sb_ref, hown_ref, hb_ref):
        h = g_ref[...] + sb_ref[...].astype(F32)
        hb_ref[...] = h.astype(BF16)

        @pl.when(pl.program_id(0) == idx_ref[0])
        def _():
            hown_ref[...] = h

    spec = pl.BlockSpec((None, r2, c), lambda s, idx_ref: (s, 0, 0))
    return pl.pallas_call(
        body, name=name,
        grid_spec=pltpu.PrefetchScalarGridSpec(
            num_scalar_prefetch=1, grid=(ns,),
            in_specs=[pl.BlockSpec((None, r2, c), lambda s, idx_ref: (s, idx_ref[4], 0)), spec],
            out_specs=[pl.BlockSpec((r2, c), lambda s, idx_ref: (0, 0)), spec]),
        out_shape=[jax.ShapeDtypeStruct((r2, c), F32), jax.ShapeDtypeStruct((ns, r2, c), BF16)],
        compiler_params=_cparams(("arbitrary",)),
    )(idx, g, sb)


def _add_chip_slabs(h, rb, idx, name):
    r2, c = h.shape

    def body(idx_ref, h_ref, r0_ref, r1_ref, r2_ref, o_ref):
        o_ref[...] = ((h_ref[...] + r0_ref[...].astype(F32)) + r1_ref[...].astype(F32)) + r2_ref[...].astype(F32)

    def pick(k):
        return pl.BlockSpec((None, r2, c), lambda i, idx_ref: (idx_ref[k], 0, 0))

    return pl.pallas_call(
        body, name=name,
        grid_spec=pltpu.PrefetchScalarGridSpec(
            num_scalar_prefetch=1, grid=(1,),
            in_specs=[pl.BlockSpec((r2, c), lambda i, idx_ref: (0, 0)), pick(1), pick(2), pick(3)],
            out_specs=pl.BlockSpec((None, r2, c), lambda i, idx_ref: (idx_ref[4], 0, 0))),
        out_shape=jax.ShapeDtypeStruct((2, r2, c), F32),
        compiler_params=_cparams(("arbitrary",)),
    )(idx, h, rb, rb, rb)


def _adamw(g, w, m, v, name):
    r, c = g.shape
    br = _row_block(r)

    def body(g_ref, w_ref, m_ref, v_ref, go_ref, d_ref, nm_ref, nv_ref):
        gg = g_ref[...]
        go_ref[...] = gg
        nm = B1 * m_ref[...] + (1.0 - B1) * gg
        nv = B2 * v_ref[...] + (1.0 - B2) * jnp.square(gg)
        m_hat = nm / (1.0 - B1 ** STEP)
        v_hat = nv / (1.0 - B2 ** STEP)
        d_ref[...] = -LR * (m_hat / (jnp.sqrt(v_hat) + ADAM_EPS) + WD * w_ref[...])
        nm_ref[...] = nm
        nv_ref[...] = nv

    spec = pl.BlockSpec((br, c), lambda i: (i, 0))
    return pl.pallas_call(
        body, name=name,
        grid=(r // br,),
        in_specs=[spec] * 4, out_specs=[spec] * 4,
        out_shape=[jax.ShapeDtypeStruct((r, c), F32)] * 4,
        compiler_params=_cparams(("arbitrary",)),
    )(g, w, m, v)


BIG = ("w_in", "w_conv_out", "w_pool", "w_pool_out", "w_o", "w_ffn_gate", "w_ffn_up", "w_ffn_down")
REPL = ("g_mix", "b_gate", "b_dw", "ln_g", "ln_b", "pool_scale", "g_ffn", "g_final")
GROUP_MIX = ("w_conv_out", "w_pool", "w_pool_out", "w_o")
GROUP_FFN = ("w_ffn_gate", "w_ffn_up", "w_ffn_down")
TRANSPOSED = ("w_ffn_gate", "w_ffn_up")
WEIGHT_ORDER = ("meta_tokens", "g_mix", "w_in", "b_gate", "w_dw", "b_dw", "ln_g", "ln_b", "w_conv_out", "w_pool",
                "pool_scale", "w_pool_out", "w_o", "g_ffn", "w_ffn_gate", "w_ffn_up", "w_ffn_down", "g_final")


def _shard2d(name, a):
    a = a[0]
    if name == "w_pool":
        return a.reshape(4 * 64, GD)
    if name in TRANSPOSED:
        return a.T
    return a


def _unshard2d(name, a, shape):
    return a.T.reshape(shape) if name in TRANSPOSED else a.reshape(shape)


def _cols_to_slabs(a):
    m, n = a.shape
    return a.reshape(m, N_SHARD, n // N_SHARD).transpose(1, 0, 2)


def _slabs_to_cols(a):
    ns, m, c = a.shape
    return a.transpose(1, 0, 2).reshape(m, ns * c)


def kernel(x, meta_tokens, g_mix, w_in, b_gate, w_dw, b_dw, ln_g, ln_b, w_conv_out, w_pool, pool_scale, w_pool_out, w_o, g_ffn, w_ffn_gate, w_ffn_up, w_ffn_down, g_final, loss_target, m_meta_tokens, m_g_mix, m_w_in, m_b_gate, m_w_dw, m_b_dw, m_ln_g, m_ln_b, m_w_conv_out, m_w_pool, m_pool_scale, m_w_pool_out, m_w_o, m_g_ffn, m_w_ffn_gate, m_w_ffn_up, m_w_ffn_down, m_g_final, v_meta_tokens, v_g_mix, v_w_in, v_b_gate, v_w_dw, v_b_dw, v_ln_g, v_ln_b, v_w_conv_out, v_w_pool, v_pool_scale, v_w_pool_out, v_w_o, v_g_ffn, v_w_ffn_gate, v_w_ffn_up, v_w_ffn_down, v_g_final):
    args = dict(locals())
    w = {n: args[n] for n in WEIGHT_ORDER}
    mom = {n: args["m_" + n] for n in WEIGHT_ORDER}
    var = {n: args["v_" + n] for n in WEIGHT_ORDER}
    seq = x.shape[1]
    nb = seq // BR + 1
    tp = nb * BR
    tk = tp // 2 if (tp // 2) % 16 == 0 else BR
    t_total = seq + N_META
    cx, cy, cc = lax.axis_index("x"), lax.axis_index("y"), lax.axis_index("c")
    chip = 2 * cx + cy
    chip1 = jnp.reshape(chip, (1,)).astype(jnp.int32)
    core = jnp.reshape(cc, (1,)).astype(jnp.int32)
    others = jnp.sort(jnp.stack([2 * (1 - cx) + cy, 2 * cx + (1 - cy), 2 * (1 - cx) + (1 - cy)]))
    idx = jnp.concatenate([chip1, others.astype(jnp.int32), core])
    xs, target = x[0], loss_target[0]

    tiny = _gather_tiny(jnp.concatenate([w["meta_tokens"], w["w_dw"][0], jnp.zeros((1, GD), F32)], axis=0))
    small = {n: w[n] for n in REPL if n != "g_final"}
    small["g_final"] = w["g_final"].reshape(1, D)
    small["w_dw"] = _slabs_to_cols(tiny[:, N_META:])
    head = jnp.concatenate([jnp.zeros((PAD, D), F32), _slabs_to_cols(tiny[:, :N_META])], axis=0)

    def cast(group, dep):
        return [_cast_into_slot(_shard2d(n, w[n]), chip1, dep, "cast_" + n) for n in group]

    def gather_finish(group, start, after, name):
        landed = _gather_wait(start[0], start[1], start[2], after, "gather_wait_" + name)
        return dict(zip(group, _forward_halves(landed, "forward_" + name)))

    st_in = _gather_start(cast(("w_in",), tiny), None, "gather_start_in")
    bufs_mix, bufs_ffn = cast(GROUP_MIX, st_in[3]), cast(GROUP_FFN, st_in[3])
    u = _rms_u(head, xs, small["g_mix"], nb)
    gw = gather_finish(("w_in",), st_in, [u] + bufs_mix + bufs_ffn, "in")
    st_mix = _gather_start(bufs_mix, gw["w_in"], "gather_start_mix")
    z = _in_proj(u, gw["w_in"], st_mix[3], nb)
    gw.update(gather_finish(GROUP_MIX, st_mix, [z], "mix"))
    st_ffn = _gather_start(bufs_ffn, gw["w_o"], "gather_start_ffn")
    w_pool_b = gw["w_pool"].reshape(N_SHARD, 4, 64, GD).transpose(1, 0, 2, 3).reshape(4, GD, GD)
    w_co_b, w_po_b, w_o_b = (gw[n].reshape(D, D) for n in ("w_conv_out", "w_pool_out", "w_o"))
    h1, yc, yp, lhs_mix, cpre, m, mw = _mixers_fwd(
        z, head, xs, small["b_gate"] + st_ffn[3][0, 0], small["w_dw"], small["b_dw"], small["ln_g"], small["ln_b"],
        small["pool_scale"], w_co_b, w_pool_b, w_po_b, w_o_b, nb, t_total)
    gw.update(gather_finish(GROUP_FFN, st_ffn, [h1], "ffn"))

    dh1, dh1b, rhs_ffn, lhs_ffn, loss, dg_ffn, dg_final = _ffn_fwd_bwd(
        h1, target, small["g_ffn"], small["g_final"], gw["w_ffn_gate"].reshape(D_FF, D),
        gw["w_ffn_up"].reshape(D_FF, D), gw["w_ffn_down"].reshape(D_FF, D), nb)

    def slabs(name, g):
        if name == "w_in":
            return g
        if name == "w_pool":
            return g.reshape(4, N_SHARD, 64, GD).transpose(1, 0, 2, 3).reshape(N_SHARD, 4 * 64, GD)
        return g.reshape(N_SHARD, g.shape[0] // N_SHARD, g.shape[1])

    def reduce_start(group, grads, name):
        g32 = [slabs(n, grads[n][0]) for n in group]
        g16 = [slabs(n, grads[n][1]) for n in group]
        from_sibling = _swap_halves_bf16(g16, "swap_halves_" + name)
        halves = [_add_sibling_half(g, sb, idx, "add_sibling_" + n) for n, g, sb in zip(group, g32, from_sibling)]
        return [h for h, _ in halves], _scatter_start([hb for _, hb in halves], "scatter_start_" + name)

    def reduce_finish(group, halves, start, after, name):
        from_chips = _scatter_wait(start[0], start[1], start[2], after, "scatter_wait_" + name)
        return [_add_chip_slabs(h, rb, idx, "add_chips_" + n) for n, h, rb in zip(group, halves, from_chips)]

    grads_ffn = dict(zip(GROUP_FFN, _wgrad_stack(lhs_ffn, rhs_ffn, 2, D_FF // 2, D // 2, tk, "wgrad_ffn")))
    halves_ffn, sc_ffn = reduce_start(GROUP_FFN, grads_ffn, "ffn")

    rhs_mix, dzg, dconv, dmwb, dm, db_gate, dln_g, dln_b, db_dw, dps = _mixers_bwd_rows(
        dh1b, yc, yp, z, small["b_gate"], cpre, small["ln_g"], small["ln_b"], mw, small["pool_scale"],
        w_o_b, w_co_b, w_po_b, w_pool_b, sc_ffn[3], nb)
    grads_mix = dict(zip(("w_conv_out", "w_pool_out", "w_o"), _wgrad_stack(lhs_mix, rhs_mix, 1, D // 2, D, tk, "wgrad_mix")))
    grads_mix["w_pool"] = _wgrad(m, dmwb, GD, GD, tk, "wgrad_pool", diag=True)
    halves_mix, sc_mix = reduce_start(GROUP_MIX, grads_mix, "mix")
    dzb, grad_x, dhead, dw_dw, dg_mix = _mixers_bwd_halo(
        dconv, dm, z, dzg, small["w_dw"], head, xs, small["g_mix"], dh1, gw["w_in"], sc_mix[3], nb, t_total)
    packed = jnp.concatenate(
        [dg_mix, db_gate.reshape(2, D), db_dw, dln_g, dln_b, dps, dg_ffn, dg_final,
         jnp.broadcast_to(loss, (1, D)), jnp.zeros((6, D), F32), dhead[PAD:], dw_dw], axis=0)
    sm = _small_start(packed, "small_start")
    grads_in = {"w_in": _wgrad(u, dzb, D, D_IN // N_SHARD, tk, "wgrad_in", col_major=True, dep=sm[3])}
    halves_in, sc_in = reduce_start(("w_in",), grads_in, "in")

    land = _small_wait(sm[0], sm[1], sm[2], [sc_in[3]], "small_wait")
    summed = _sum_slots(land, packed, jnp.reshape(4 * cx + 2 * cy + cc, (1,)).astype(jnp.int32))
    loss = summed[9, 0]

    first = GROUP_FFN + GROUP_MIX
    reduced_half = reduce_finish(GROUP_FFN, halves_ffn, sc_ffn, [summed], "ffn")
    reduced_half += reduce_finish(GROUP_MIX, halves_mix, sc_mix, [summed], "mix")
    reduced = dict(zip(first, _join_halves(reduced_half, "join_halves_first")))
    updates = {n: _adamw(reduced[n], _shard2d(n, w[n]), _shard2d(n, mom[n]), _shard2d(n, var[n]), "adamw_" + n)
               for n in first}

    def repl_stack(d):
        return jnp.concatenate([d["g_mix"], d["b_gate"].reshape(2, D), d["b_dw"], d["ln_g"], d["ln_b"],
                                d["pool_scale"], d["g_ffn"], d["g_final"].reshape(1, D), jnp.ones((7, D), F32)], axis=0)

    def shard_stack(d):
        return jnp.concatenate([d["meta_tokens"], d["w_dw"][0], jnp.ones((1, GD), F32)], axis=0)

    g_repl = summed[0:16]
    g_shard = lax.dynamic_slice_in_dim(summed[16:64], chip * GD, GD, axis=1)
    g_repl, d_repl, m_repl, v_repl = _adamw(g_repl, repl_stack(w), repl_stack(mom), repl_stack(var), "adamw_repl")
    g_shard, d_shard, m_shard, v_shard = _adamw(g_shard, shard_stack(w), shard_stack(mom), shard_stack(var), "adamw_cols")

    done_first = [updates[n][1] for n in first] + [d_repl, d_shard]
    last_half = reduce_finish(("w_in",), halves_in, sc_in, done_first, "in")
    reduced["w_in"] = _join_halves(last_half, "join_halves_in")[0]
    updates["w_in"] = _adamw(reduced["w_in"], w["w_in"][0], mom["w_in"][0], var["w_in"][0], "adamw_w_in")

    def unpack(name, repl, shard):
        if name == "meta_tokens":
            return shard[0:N_META]
        if name == "w_dw":
            return shard[N_META:N_META + KW].reshape(1, KW, GD)
        row = {"g_mix": 0, "b_gate": 1, "b_dw": 3, "ln_g": 4, "ln_b": 5, "pool_scale": 6, "g_ffn": 7, "g_final": 8}[name]
        if name == "b_gate":
            return repl[1:3].reshape(1, 2 * D)
        if name == "g_final":
            return repl[8]
        return repl[row:row + 1]

    out_g, out_d, out_m, out_v = {}, {}, {}, {}
    for n in WEIGHT_ORDER:
        if n in BIG:
            g, d_, m_, v_ = updates[n]
            shape = w[n].shape
            out_g[n], out_d[n], out_m[n], out_v[n] = (_unshard2d(n, a, shape) for a in (g, d_, m_, v_))
        else:
            out_g[n] = unpack(n, g_repl, g_shard)
            out_d[n] = unpack(n, d_repl, d_shard)
            out_m[n] = unpack(n, m_repl, m_shard)
            out_v[n] = unpack(n, v_repl, v_shard)
    return (loss, grad_x[None], *[out_g[n] for n in WEIGHT_ORDER], *[out_d[n] for n in WEIGHT_ORDER],
            *[out_m[n] for n in WEIGHT_ORDER], *[out_v[n] for n in WEIGHT_ORDER])
```

```python
import functools

import jax
import jax.numpy as jnp
from jax import lax
from jax.experimental import pallas as pl
from jax.experimental.pallas import tpu as pltpu

F32 = jnp.float32
BF16 = jnp.bfloat16
MESH = pl.DeviceIdType.MESH

D = 1024
N_META = 16
KW = 31
CPAD = KW // 2
POOL_WINDOWS = (2, 4, 8, 16)
GD = 256
D_IN = 5 * D
D_FF = 2816
N_SHARD = 4
BR = 256
HALO = 16
PAD = BR - N_META
EXT = BR + 2 * HALO
RMS_EPS = 1e-6
LN_EPS = 1e-5
LR, B1, B2, ADAM_EPS, WD, STEP = 0.001, 0.9, 0.999, 1e-08, 0.01, 10
VMEM_LIMIT = 56 * 1024 * 1024


def _cparams(sem, vmem=VMEM_LIMIT):
    return pltpu.CompilerParams(dimension_semantics=sem, vmem_limit_bytes=vmem)


def _dot(a, b):
    return jnp.dot(a, b, preferred_element_type=F32)


def _dot_nt(a, b):
    return lax.dot_general(a, b, (((1,), (1,)), ((), ())), preferred_element_type=F32)


def _dot_tn(a, b):
    return lax.dot_general(a, b, (((0,), (0,)), ((), ())), preferred_element_type=F32)


def _sigmoid(x):
    return 0.5 * jnp.tanh(0.5 * x) + 0.5


def _row_ids(i, n, offset=0):
    return lax.broadcasted_iota(jnp.int32, (n, 1), 0) + (i * BR + offset - PAD)


def _pool_cnt(t, w, t_total):
    left = w // 2
    right = w - 1 - left
    lo = jnp.clip(t - left, 0, t_total)
    hi = jnp.clip(t + right + 1, 0, t_total)
    return jnp.maximum(hi - lo, 1).astype(F32)


def _halo_specs(nb, halo_width=D, width=D):
    last = nb * (BR // HALO) - 1
    return [
        pl.BlockSpec((HALO, halo_width), lambda i: (jnp.maximum(i * (BR // HALO) - 1, 0), 0)),
        pl.BlockSpec((BR, width), lambda i: (i, 0)),
        pl.BlockSpec((HALO, halo_width), lambda i: (jnp.minimum((i + 1) * (BR // HALO), last), 0)),
    ]


def _cols(ref, n):
    return [ref.at[:, k * D:(k + 1) * D] for k in range(n)]


def _fill_ext(ext_ref, prev, cur, nxt, i, nb):
    ext_ref[0:HALO, :] = jnp.where(i > 0, prev, 0.0)
    ext_ref[HALO:HALO + BR, :] = cur
    ext_ref[HALO + BR:EXT, :] = jnp.where(i < nb - 1, nxt, 0.0)


ROT_ROWS = EXT - 8


def _fill_rot(rot_ref, ext_ref, lanes):
    for r in range(1, 8):
        rot_ref[r] = ext_ref[pl.ds(r, ROT_ROWS), lanes]


def _tap(rot_ref, ext_ref, lanes, offset):
    q, r = divmod(offset, 8)
    if r == 0:
        return ext_ref[pl.ds(8 * q, BR), lanes]
    return rot_ref[r, pl.ds(8 * q, BR), :]


def _row_spec(width=D):
    return pl.BlockSpec((BR, width), lambda i: (i, 0))


def _x_spec():
    return pl.BlockSpec((BR, D), lambda i: (jnp.maximum(i - 1, 0), 0))


def _const_spec(shape):
    nd = len(shape)
    return pl.BlockSpec(shape, lambda i: (0,) * nd)


def _rms_u(head, x, g_mix, nb):
    def body(head_ref, x_ref, g_ref, u_ref):
        i = pl.program_id(0)
        h = jnp.where(i == 0, head_ref[...], x_ref[...])
        r = lax.rsqrt(jnp.mean(h * h, axis=-1, keepdims=True) + RMS_EPS)
        u_ref[...] = ((h * r) * g_ref[...]).astype(BF16)

    return pl.pallas_call(
        body, name="rms_u",
        grid=(nb,),
        in_specs=[_const_spec((BR, D)), _x_spec(), _const_spec((1, D))],
        out_specs=_row_spec(),
        out_shape=jax.ShapeDtypeStruct((nb * BR, D), BF16),
        compiler_params=_cparams(("arbitrary",)),
    )(head, x, g_mix)


def _in_proj(u, w_in_b, dep, nb):
    tp = nb * BR
    ns = w_in_b.shape[0]
    wcols = w_in_b.shape[2]
    rows = tp // 4 if (tp // 4) % 16 == 0 else BR

    def body(u_ref, w_ref, dep_ref, z_ref):
        z_ref[...] = _dot(u_ref[...], w_ref[...])

    return pl.pallas_call(
        body, name="in_proj",
        grid=(ns, tp // rows),
        in_specs=[
            pl.BlockSpec((rows, D), lambda s, i: (i, 0)),
            pl.BlockSpec((None, D, wcols), lambda s, i: (s, 0, 0)),
            pl.BlockSpec(memory_space=pl.ANY),
        ],
        out_specs=pl.BlockSpec((rows, wcols), lambda s, i: (i, s)),
        out_shape=jax.ShapeDtypeStruct((tp, ns * wcols), F32),
        compiler_params=_cparams(("arbitrary", "arbitrary")),
    )(u, w_in_b, dep)


def _mixers_fwd(z, head, x, b_gate, w_dw, b_dw, ln_g, ln_b, pool_scale, w_co, w_pool, w_po, w_o, nb, t_total):
    tp = nb * BR

    def body(z_prev, z_cur, z_next, head_ref, x_ref, bg_ref, wdw_ref, bdw_ref,
             lng_ref, lnb_ref, ps_ref, wco_ref, wpool_ref, wpo_ref, wo_ref,
             h1_ref, yc_ref, yp_ref, mg_ref, ca_ref, cpre_ref, m_ref, mw_ref, m2b_ref, ext_ref, pext_ref, rot_ref):
        i = pl.program_id(0)
        avp, agp, pp = _cols(z_prev, 3)
        av, ag, pc, za, zb = _cols(z_cur, 5)
        avn, agn, pn = _cols(z_next, 3)
        _fill_ext(ext_ref, avp[...] * _sigmoid(agp[...]), av[...] * _sigmoid(ag[...]),
                  avn[...] * _sigmoid(agn[...]), i, nb)
        _fill_ext(pext_ref, pp[...], pc[...], pn[...], i, nb)

        def conv_chunk(c, carry):
            lanes = pl.ds(pl.multiple_of(c * 128, 128), 128)
            _fill_rot(rot_ref, ext_ref, lanes)
            acc = jnp.broadcast_to(bdw_ref[:, lanes], (BR, 128))
            for k in range(KW):
                acc = acc + wdw_ref[k:k + 1, lanes] * _tap(rot_ref, ext_ref, lanes, 1 + k)
            cpre_ref[:, lanes] = acc
            return carry
        lax.fori_loop(0, D // 128, conv_chunk, 0)

        conv = cpre_ref[...]
        mu = jnp.mean(conv, axis=-1, keepdims=True)
        xc = conv - mu
        rstd = lax.rsqrt(jnp.mean(xc * xc, axis=-1, keepdims=True) + LN_EPS)
        ln = (xc * rstd) * lng_ref[...] + lnb_ref[...]
        cact = (ln * _sigmoid(ln)).astype(BF16)
        ca_ref[...] = cact
        y_conv = _dot(cact, wco_ref[...])
        yc_ref[...] = y_conv

        t = _row_ids(i, BR)
        for gi, w in enumerate(POOL_WINDOWS):
            left = w // 2
            right = w - 1 - left
            lanes = slice(gi * GD, (gi + 1) * GD)
            s = pext_ref[pl.ds(HALO - left, BR), lanes]
            for j in range(-left + 1, right + 1):
                s = s + pext_ref[pl.ds(HALO + j, BR), lanes]
            m = (s / _pool_cnt(t, w, t_total) - pext_ref[HALO:HALO + BR, lanes]).astype(BF16)
            m_ref[:, lanes] = m
            mw_ref[:, lanes] = _dot(m, wpool_ref[gi])
        mw = mw_ref[...]
        m2b = (mw * ps_ref[...]).astype(BF16)
        m2b_ref[...] = m2b
        y_pool = _dot(m2b, wpo_ref[...])
        yp_ref[...] = y_pool

        s_a = _sigmoid(za[...] + bg_ref[:, 0:D])
        s_b = _sigmoid(zb[...] + bg_ref[:, D:2 * D])
        merged = (s_a * y_conv + s_b * y_pool).astype(BF16)
        mg_ref[...] = merged
        h0 = jnp.where(i == 0, head_ref[...], x_ref[...])
        h1_ref[...] = h0 + _dot(merged, wo_ref[...])

    in_specs = (_halo_specs(nb, 3 * D, 5 * D)
                + [_const_spec((BR, D)), _x_spec(), _const_spec((1, 2 * D)), _const_spec((32, D)),
                   _const_spec((1, D)), _const_spec((1, D)), _const_spec((1, D)), _const_spec((1, D)),
                   _const_spec((D, D)), _const_spec((4, GD, GD)), _const_spec((D, D)), _const_spec((D, D))])
    outs = [(F32, "h1"), (F32, "yc"), (F32, "yp"), (BF16, "mg"), (BF16, "ca"), (F32, "cpre"), (BF16, "m"), (F32, "mw"),
            (BF16, "m2b")]
    return pl.pallas_call(
        body, name="mixers_fwd",
        grid=(nb,),
        in_specs=in_specs,
        out_specs=[_row_spec() for _ in outs],
        out_shape=[jax.ShapeDtypeStruct((tp, D), dt) for dt, _ in outs],
        scratch_shapes=[pltpu.VMEM((EXT, D), F32), pltpu.VMEM((EXT, D), F32), pltpu.VMEM((8, ROT_ROWS, 128), F32)],
        compiler_params=_cparams(("arbitrary",)),
    )(z, z, z, head, x, b_gate, w_dw, b_dw, ln_g, ln_b, pool_scale, w_co, w_pool, w_po, w_o)


def _ffn_fwd_bwd(h1, target, g_ffn, g_final, w_g, w_u, w_d, nb):
    tp = nb * BR

    def body(h1_ref, tgt_ref, gf_ref, gfin_ref, wg_hbm, wu_hbm, wd_hbm,
             dh1_ref, dh1b_ref, vb_ref, fb_ref, dgb_ref, dub_ref, dh2b_ref, loss_ref, dgf_ref, dgfin_ref,
             wg_ref, wu_ref, wd_ref, sem):
        i = pl.program_id(0)

        @pl.when(i == 0)
        def _():
            copies = [pltpu.make_async_copy(wg_hbm, wg_ref, sem.at[0]),
                      pltpu.make_async_copy(wu_hbm, wu_ref, sem.at[1]),
                      pltpu.make_async_copy(wd_hbm, wd_ref, sem.at[2])]
            for cp in copies:
                cp.start()
            loss_ref[...] = jnp.zeros_like(loss_ref)
            dgf_ref[...] = jnp.zeros_like(dgf_ref)
            dgfin_ref[...] = jnp.zeros_like(dgfin_ref)
            for cp in copies:
                cp.wait()

        h1 = h1_ref[...]
        r1 = lax.rsqrt(jnp.mean(h1 * h1, axis=-1, keepdims=True) + RMS_EPS)
        vn = h1 * r1
        vb = (vn * gf_ref[...]).astype(BF16)
        vb_ref[...] = vb
        g = _dot_nt(vb, wg_ref[...])
        up = _dot_nt(vb, wu_ref[...])
        sg = _sigmoid(g)
        sl = g * sg
        fb = (sl * up).astype(BF16)
        fb_ref[...] = fb
        h2 = h1 + _dot(fb, wd_ref[...])
        r2 = lax.rsqrt(jnp.mean(h2 * h2, axis=-1, keepdims=True) + RMS_EPS)
        yn = h2 * r2
        valid = i > 0
        diff = jnp.where(valid, yn * gfin_ref[...] - tgt_ref[...], 0.0)
        loss_ref[...] += 0.5 * jnp.sum(jnp.mean(diff * diff, axis=-1, keepdims=True))
        dy = diff * (1.0 / D)
        dgfin_ref[...] += jnp.sum(dy * yn, axis=0, keepdims=True)
        gd = dy * gfin_ref[...]
        dh2 = r2 * (gd - yn * jnp.mean(yn * gd, axis=-1, keepdims=True))
        dh2b = dh2.astype(BF16)
        dh2b_ref[...] = dh2b
        df = _dot_nt(dh2b, wd_ref[...])
        dub = (df * sl).astype(BF16)
        dgb = (df * up * (sg * (1.0 + g * (1.0 - sg)))).astype(BF16)
        dub_ref[...] = dub
        dgb_ref[...] = dgb
        dv = _dot(dgb, wg_ref[...]) + _dot(dub, wu_ref[...])
        dgf_ref[...] += jnp.sum(dv * vn, axis=0, keepdims=True)
        gd1 = dv * gf_ref[...]
        dh1 = dh2 + r1 * (gd1 - vn * jnp.mean(vn * gd1, axis=-1, keepdims=True))
        dh1_ref[...] = dh1
        dh1b_ref[...] = dh1.astype(BF16)

    any_spec = pl.BlockSpec(memory_space=pl.ANY)
    return pl.pallas_call(
        body, name="ffn_fwd_bwd",
        grid=(nb,),
        in_specs=[_row_spec(), _x_spec(), _const_spec((1, D)), _const_spec((1, D)), any_spec, any_spec, any_spec],
        out_specs=[_row_spec(), _row_spec(), _row_spec(), _row_spec(D_FF), _row_spec(D_FF), _row_spec(D_FF), _row_spec(),
                   _const_spec((1, 1)), _const_spec((1, D)), _const_spec((1, D))],
        out_shape=[jax.ShapeDtypeStruct((tp, D), F32), jax.ShapeDtypeStruct((tp, D), BF16),
                   jax.ShapeDtypeStruct((tp, D), BF16), jax.ShapeDtypeStruct((tp, D_FF), BF16),
                   jax.ShapeDtypeStruct((tp, D_FF), BF16), jax.ShapeDtypeStruct((tp, D_FF), BF16),
                   jax.ShapeDtypeStruct((tp, D), BF16), jax.ShapeDtypeStruct((1, 1), F32),
                   jax.ShapeDtypeStruct((1, D), F32), jax.ShapeDtypeStruct((1, D), F32)],
        scratch_shapes=[pltpu.VMEM((D_FF, D), BF16), pltpu.VMEM((D_FF, D), BF16), pltpu.VMEM((D_FF, D), BF16),
                        pltpu.SemaphoreType.DMA((3,))],
        compiler_params=_cparams(("arbitrary",)),
    )(h1, target, g_ffn, g_final, w_g, w_u, w_d)


def _mixers_bwd_rows(dh1b, yc, yp, z, b_gate, cpre, ln_g, ln_b, mw, pool_scale, w_o, w_co, w_po, w_pool, dep, nb):
    tp = nb * BR

    def body(dh1b_ref, yc_ref, yp_ref, za, zb, bg_ref, cpre_ref, lng_ref, lnb_ref, mw_ref, ps_ref,
             wo_ref, wco_ref, wpo_ref, wpool_ref, dep_ref,
             dycb_ref, dypb_ref, dzg_ref, dconv_ref, dmwb_ref, dm_ref, dbg_ref, dlng_ref, dlnb_ref, dbdw_ref, dps_ref):
        i = pl.program_id(0)

        @pl.when(i == 0)
        def _():
            for r in (dbg_ref, dlng_ref, dlnb_ref, dbdw_ref, dps_ref):
                r[...] = jnp.zeros_like(r)

        dmg = _dot_nt(dh1b_ref[...], wo_ref[...])
        s_a = _sigmoid(za[...] + bg_ref[:, 0:D])
        s_b = _sigmoid(zb[...] + bg_ref[:, D:2 * D])
        dycb = (dmg * s_a).astype(BF16)
        dypb = (dmg * s_b).astype(BF16)
        dycb_ref[...] = dycb
        dypb_ref[...] = dypb
        dza = dmg * yc_ref[...] * (s_a * (1.0 - s_a))
        dzb = dmg * yp_ref[...] * (s_b * (1.0 - s_b))
        dzg_ref[:, 0:D] = dza.astype(BF16)
        dzg_ref[:, D:2 * D] = dzb.astype(BF16)
        dbg_ref[:, 0:D] += jnp.sum(dza, axis=0, keepdims=True)
        dbg_ref[:, D:2 * D] += jnp.sum(dzb, axis=0, keepdims=True)

        dca = _dot_nt(dycb, wco_ref[...])
        conv = cpre_ref[...]
        mu = jnp.mean(conv, axis=-1, keepdims=True)
        xc = conv - mu
        rstd = lax.rsqrt(jnp.mean(xc * xc, axis=-1, keepdims=True) + LN_EPS)
        xhat = xc * rstd
        ln = xhat * lng_ref[...] + lnb_ref[...]
        sg = _sigmoid(ln)
        dln = dca * (sg * (1.0 + ln * (1.0 - sg)))
        dlng_ref[...] += jnp.sum(dln * xhat, axis=0, keepdims=True)
        dlnb_ref[...] += jnp.sum(dln, axis=0, keepdims=True)
        dxh = dln * lng_ref[...]
        dconv = rstd * (dxh - jnp.mean(dxh, axis=-1, keepdims=True)
                        - xhat * jnp.mean(dxh * xhat, axis=-1, keepdims=True))
        dconv_ref[...] = dconv
        dbdw_ref[...] += jnp.sum(dconv, axis=0, keepdims=True)

        dm2 = _dot_nt(dypb, wpo_ref[...])
        dps_ref[...] += jnp.sum(dm2 * mw_ref[...], axis=0, keepdims=True)
        dmwb = (dm2 * ps_ref[...]).astype(BF16)
        dmwb_ref[...] = dmwb
        for gi in range(len(POOL_WINDOWS)):
            lanes = slice(gi * GD, (gi + 1) * GD)
            dm_ref[:, lanes] = _dot_nt(dmwb[:, lanes], wpool_ref[gi])

    in_specs = [_row_spec(), _row_spec(), _row_spec(),
                pl.BlockSpec((BR, D), lambda i: (i, 3)), pl.BlockSpec((BR, D), lambda i: (i, 4)),
                _const_spec((1, 2 * D)), _row_spec(), _const_spec((1, D)), _const_spec((1, D)), _row_spec(),
                _const_spec((1, D)), _const_spec((D, D)), _const_spec((D, D)), _const_spec((D, D)),
                _const_spec((4, GD, GD)), pl.BlockSpec(memory_space=pl.ANY)]
    return pl.pallas_call(
        body, name="mixers_bwd_rows",
        grid=(nb,),
        in_specs=in_specs,
        out_specs=[_row_spec(), _row_spec(), _row_spec(2 * D), _row_spec(), _row_spec(), _row_spec(),
                   _const_spec((1, 2 * D)), _const_spec((1, D)), _const_spec((1, D)), _const_spec((1, D)),
                   _const_spec((1, D))],
        out_shape=[jax.ShapeDtypeStruct((tp, D), BF16), jax.ShapeDtypeStruct((tp, D), BF16),
                   jax.ShapeDtypeStruct((tp, 2 * D), BF16), jax.ShapeDtypeStruct((tp, D), F32),
                   jax.ShapeDtypeStruct((tp, D), BF16), jax.ShapeDtypeStruct((tp, D), F32),
                   jax.ShapeDtypeStruct((1, 2 * D), F32), jax.ShapeDtypeStruct((1, D), F32),
                   jax.ShapeDtypeStruct((1, D), F32), jax.ShapeDtypeStruct((1, D), F32),
                   jax.ShapeDtypeStruct((1, D), F32)],
        compiler_params=_cparams(("arbitrary",)),
    )(dh1b, yc, yp, z, z, b_gate, cpre, ln_g, ln_b, mw, pool_scale, w_o, w_co, w_po, w_pool, dep)


def _mixers_bwd_halo(dconv, dm, z, dzg, w_dw, head, x, g_mix, dh1, w_in_b, dep, nb, t_total):
    tp = nb * BR
    ns = w_in_b.shape[0]
    wcols = w_in_b.shape[2]
    seq = x.shape[0]

    def body(dcp, dcc, dcn, dmp, dmc, dmn, z_prev, z_cur, z_next, dzg_ref, wdw_ref, head_ref, x_ref, g_ref,
             dh1_ref, w_hbm, dep_ref,
             dzb_ref, gx_ref, dhead_ref, dwdw_ref, dgmix_ref,
             w_ref, sem, aext_ref, dext_ref, qext_ref, da_ref, rot_ref, dwp_ref):
        i = pl.program_id(0)
        (avp, agp), (av, ag), (avn, agn) = _cols(z_prev, 2), _cols(z_cur, 2), _cols(z_next, 2)

        @pl.when(i == 0)
        def _():
            cp = pltpu.make_async_copy(w_hbm, w_ref, sem.at[0])
            cp.start()
            dwp_ref[...] = jnp.zeros_like(dwp_ref)
            dgmix_ref[...] = jnp.zeros_like(dgmix_ref)
            cp.wait()

        sig_g = _sigmoid(ag[...])
        _fill_ext(aext_ref, avp[...] * _sigmoid(agp[...]), av[...] * sig_g, avn[...] * _sigmoid(agn[...]), i, nb)
        _fill_ext(dext_ref, dcp[...], dcc[...], dcn[...], i, nb)
        _fill_ext(qext_ref, dmp[...], dmc[...], dmn[...], i, nb)

        def conv_chunk(c, carry):
            lanes = pl.ds(pl.multiple_of(c * 128, 128), 128)
            _fill_rot(rot_ref, dext_ref, lanes)
            acc = jnp.zeros((BR, 128), F32)
            for k in range(KW):
                acc = acc + wdw_ref[k:k + 1, lanes] * _tap(rot_ref, dext_ref, lanes, KW - k)
            da_ref[:, lanes] = acc
            _fill_rot(rot_ref, aext_ref, lanes)
            dcv = dext_ref[HALO:HALO + BR, lanes]
            for k in range(KW):
                prod = _tap(rot_ref, aext_ref, lanes, 1 + k) * dcv
                dwp_ref[k, :, lanes] += jnp.sum(prod.reshape(BR // 8, 8, 128), axis=0)
            return carry
        lax.fori_loop(0, D // 128, conv_chunk, 0)

        @pl.when(i == nb - 1)
        def _():
            dwdw_ref[...] = jnp.sum(dwp_ref[...], axis=1)

        da = da_ref[...]
        a_val = av[...]
        dzb_ref[:, 0:D] = (da * sig_g).astype(BF16)
        dzb_ref[:, D:2 * D] = (da * a_val * (sig_g * (1.0 - sig_g))).astype(BF16)

        t_ext = _row_ids(i, EXT, -HALO)
        for gi, w in enumerate(POOL_WINDOWS):
            left = w // 2
            right = w - 1 - left
            lanes = slice(gi * GD, (gi + 1) * GD)
            qext_ref[:, lanes] = qext_ref[:, lanes] / _pool_cnt(t_ext, w, t_total)
            s = qext_ref[pl.ds(HALO - right, BR), lanes]
            for j in range(-right + 1, left + 1):
                s = s + qext_ref[pl.ds(HALO + j, BR), lanes]
            dzb_ref[:, 2 * D + gi * GD:2 * D + (gi + 1) * GD] = (s - dmc[:, lanes]).astype(BF16)
        dzb_ref[:, 3 * D:5 * D] = dzg_ref[...]

        du = _dot_nt(dzb_ref[:, 0:wcols], w_ref[0])
        for s_i in range(1, ns):
            du = du + _dot_nt(dzb_ref[:, s_i * wcols:(s_i + 1) * wcols], w_ref[s_i])
        h0 = jnp.where(i == 0, head_ref[...], x_ref[...])
        r0 = lax.rsqrt(jnp.mean(h0 * h0, axis=-1, keepdims=True) + RMS_EPS)
        un = h0 * r0
        dgmix_ref[...] += jnp.sum(du * un, axis=0, keepdims=True)
        gd = du * g_ref[...]
        dh0 = dh1_ref[...] + r0 * (gd - un * jnp.mean(un * gd, axis=-1, keepdims=True))
        gx_ref[...] = dh0

        @pl.when(i == 0)
        def _():
            dhead_ref[...] = dh0

    any_spec = pl.BlockSpec(memory_space=pl.ANY)
    in_specs = (_halo_specs(nb) + _halo_specs(nb) + _halo_specs(nb, 2 * D, 2 * D)
                + [_row_spec(2 * D), _const_spec((32, D)), _const_spec((BR, D)), _x_spec(), _const_spec((1, D)),
                   _row_spec(), any_spec, any_spec])
    return pl.pallas_call(
        body, name="mixers_bwd_halo",
        grid=(nb,),
        in_specs=in_specs,
        out_specs=[_row_spec(D_IN), _x_spec(), _const_spec((BR, D)), _const_spec((32, D)), _const_spec((1, D))],
        out_shape=[jax.ShapeDtypeStruct((tp, D_IN), BF16), jax.ShapeDtypeStruct((seq, D), F32),
                   jax.ShapeDtypeStruct((BR, D), F32), jax.ShapeDtypeStruct((32, D), F32),
                   jax.ShapeDtypeStruct((1, D), F32)],
        scratch_shapes=[pltpu.VMEM((ns, D, wcols), BF16), pltpu.SemaphoreType.DMA((1,)),
                        pltpu.VMEM((EXT, D), F32), pltpu.VMEM((EXT, D), F32), pltpu.VMEM((EXT, D), F32),
                        pltpu.VMEM((BR, D), F32), pltpu.VMEM((8, ROT_ROWS, 128), F32), pltpu.VMEM((32, 8, D), F32)],
        compiler_params=_cparams(("arbitrary",)),
    )(dconv, dconv, dconv, dm, dm, dm, z, z, z, dzg, w_dw, head, x, g_mix, dh1, w_in_b, dep)


def _wgrad(a, c, tm, tn, tk, name, diag=False, col_major=False, dep=None):
    tp, m = a.shape
    n = c.shape[1]
    nk = tp // tk
    gm, gn = m // tm, n // tn

    def body(a_ref, c_ref, *rest):
        o_ref, ob_ref = rest[-2:]
        k = pl.program_id(2)

        @pl.when(k == 0)
        def _():
            o_ref[...] = jnp.zeros_like(o_ref)

        o_ref[...] += _dot_tn(a_ref[...], c_ref[...])

        @pl.when(k == nk - 1)
        def _():
            ob_ref[...] = o_ref[...].astype(BF16)

    c_map = lambda i, j, k: (k, j)
    grid = (gm, gn, nk)
    deps = [] if dep is None else [dep]
    if diag:
        grid = (gm, 1, nk)
        c_map = lambda i, j, k: (k, i)
        o_spec = pl.BlockSpec((tm, tn), lambda i, j, k: (i, 0))
        o_shape = (m, tn)
    elif col_major:
        o_spec = pl.BlockSpec((None, tm, tn), lambda i, j, k: (j, i, 0))
        o_shape = (gn, m, tn)
    else:
        o_spec = pl.BlockSpec((tm, tn), lambda i, j, k: (i, j))
        o_shape = (m, n)
    return pl.pallas_call(
        body, name=name,
        grid=grid,
        in_specs=[pl.BlockSpec((tk, tm), lambda i, j, k: (k, i)), pl.BlockSpec((tk, tn), c_map)]
        + [pl.BlockSpec(memory_space=pl.ANY)] * len(deps),
        out_specs=[o_spec, o_spec],
        out_shape=[jax.ShapeDtypeStruct(o_shape, F32), jax.ShapeDtypeStruct(o_shape, BF16)],
        compiler_params=_cparams(("arbitrary", "arbitrary", "arbitrary")),
    )(a, c, *deps)


def _place():
    x, y, c = lax.axis_index("x"), lax.axis_index("y"), lax.axis_index("c")
    others = [(1 - x, y), (x, 1 - y), (1 - x, 1 - y)]
    return x, y, c, others


def _split2(a, axis=0):
    return a.reshape(a.shape[:axis] + (2, a.shape[axis] // 2) + a.shape[axis + 1:])


def _merge2(a, axis=0):
    return a.reshape(a.shape[:axis] + (2 * a.shape[axis + 1],) + a.shape[axis + 2:])


def _cast_into_slot(w2d, chip, dep, name):
    r, c = w2d.shape
    r2 = r // 2

    def body(chip_ref, w_ref, dep_ref, o_ref):
        o_ref[...] = w_ref[...].astype(BF16)

    return pl.pallas_call(
        body, name=name,
        grid_spec=pltpu.PrefetchScalarGridSpec(
            num_scalar_prefetch=1, grid=(2,),
            in_specs=[pl.BlockSpec((r2, c), lambda h, chip_ref: (h, 0)), pl.BlockSpec(memory_space=pl.ANY)],
            out_specs=pl.BlockSpec((None, None, r2, c), lambda h, chip_ref: (chip_ref[0], h, 0, 0))),
        out_shape=jax.ShapeDtypeStruct((N_SHARD, 2, r2, c), BF16),
        compiler_params=_cparams(("arbitrary",)),
    )(chip, w2d, dep)


HBM_SPEC = pl.BlockSpec(memory_space=pltpu.HBM)
SEM_SPEC = pl.BlockSpec(memory_space=pltpu.SEMAPHORE)
DATAFLOW = pltpu.SideEffectType.DATAFLOW_SIDE_EFFECTING
TOKEN = jax.ShapeDtypeStruct((8, 128), F32)


def _in_hbm(a):
    return pltpu.with_memory_space_constraint(a, pltpu.HBM)


def _gather_tiny(v):
    vm = pl.BlockSpec(memory_space=pltpu.VMEM)

    def body(v_ref, out_ref, send_sems, recv_sems):
        x, y, c, others = _place()
        mine = 2 * x + y
        sends = [pltpu.make_async_remote_copy(
            src_ref=v_ref, dst_ref=out_ref.at[mine], send_sem=send_sems.at[j], recv_sem=recv_sems.at[j],
            device_id=(*chip, c), device_id_type=MESH) for j, chip in enumerate(others)]
        for cp in sends:
            cp.start()
        out_ref[mine] = v_ref[...]
        for j, chip in enumerate(others):
            landed = out_ref.at[2 * chip[0] + chip[1]]
            pltpu.make_async_remote_copy(
                src_ref=landed, dst_ref=landed, send_sem=send_sems.at[j], recv_sem=recv_sems.at[j],
                device_id=(x, y, c), device_id_type=MESH).wait_recv()
        for cp in sends:
            cp.wait_send()

    return pl.pallas_call(
        body, name="gather_tiny",
        in_specs=[vm], out_specs=vm,
        out_shape=jax.ShapeDtypeStruct((N_SHARD,) + v.shape, v.dtype),
        scratch_shapes=[pltpu.SemaphoreType.DMA((3,)), pltpu.SemaphoreType.DMA((3,))],
    )(v)


def _ici_copies(srcs, dsts, send_sems, recv_sems, started):
    x, y, c, others = _place()
    mine = 2 * x + y
    copies = []
    for a in range(len(srcs)):
        for j, chip in enumerate(others):
            there = 2 * chip[0] + chip[1]
            src, dst = srcs[a](mine, there, c), dsts[a](mine, there, c)
            if not started:
                dst = dsts[a](there, mine, c)
            copies.append(pltpu.make_async_remote_copy(
                src_ref=src, dst_ref=dst, send_sem=send_sems.at[a * 3 + j], recv_sem=recv_sems.at[a * 3 + j],
                device_id=(*chip, c), device_id_type=MESH))
    return copies


def _split_start(srcs_of, dsts_of, arrays, n_src, name, copies_of=None, n_sems=None, dep=None):
    n = len(arrays)
    n_sems = n_sems or 3 * n_src
    copies_of = copies_of or (lambda ins, ss, rs, started: _ici_copies(srcs_of(ins), dsts_of(ins), ss, rs, started))
    deps = [] if dep is None else [dep]
    nd = len(deps)

    def body(*refs):
        ins = refs[:n]
        send_sems, recv_sems = refs[n + nd], refs[n + nd + 1]
        token = refs[2 * n + nd + 2]
        for cp in copies_of(ins, send_sems, recv_sems, True):
            cp.start()
        token[...] = jnp.zeros_like(token)

    out = pl.pallas_call(
        body, name=name,
        in_specs=[HBM_SPEC] * n + [pl.BlockSpec(memory_space=pl.ANY)] * nd,
        out_specs=(SEM_SPEC, SEM_SPEC, *([HBM_SPEC] * n), pl.BlockSpec(memory_space=pltpu.VMEM)),
        out_shape=(pltpu.SemaphoreType.DMA((n_sems,)), pltpu.SemaphoreType.DMA((n_sems,)),
                   *[pltpu.HBM(a.shape, a.dtype) for a in arrays], TOKEN),
        input_output_aliases={a: 2 + a for a in range(n)},
        compiler_params=pltpu.CompilerParams(has_side_effects=DATAFLOW),
    )(*[_in_hbm(a) for a in arrays], *deps)
    return out[0], out[1], list(out[2:2 + n]), out[2 + n]


def _split_wait(srcs_of, dsts_of, send_sems, recv_sems, arrays, after, name, copies_of=None):
    n = len(arrays)
    copies_of = copies_of or (lambda ins, ss, rs, started: _ici_copies(srcs_of(ins), dsts_of(ins), ss, rs, started))

    def body(*refs):
        ins = refs[:n]
        send_sems, recv_sems = refs[n], refs[n + 1]
        for cp in copies_of(ins, send_sems, recv_sems, False):
            cp.wait_send()
            cp.wait_recv()

    return pl.pallas_call(
        body, name=name,
        in_specs=[HBM_SPEC] * n + [SEM_SPEC, SEM_SPEC] + [pl.BlockSpec(memory_space=pl.ANY)] * len(after),
        out_specs=[HBM_SPEC] * n,
        out_shape=[pltpu.HBM(a.shape, a.dtype) for a in arrays],
        input_output_aliases={a: a for a in range(n)},
        compiler_params=pltpu.CompilerParams(has_side_effects=DATAFLOW),
    )(*arrays, send_sems, recv_sems, *after)


def _gather_views(ins):
    view = [lambda frm, to, c, r=r: r.at[frm, c] for r in ins]
    return view


def _gather_start(bufs, dep, name):
    return _split_start(_gather_views, _gather_views, bufs, len(bufs), name, dep=dep)


def _gather_wait(send_sems, recv_sems, bufs, after, name):
    return _split_wait(_gather_views, _gather_views, send_sems, recv_sems, bufs, after, name)


def _forward_halves(bufs, name):
    n = len(bufs)
    any_spec = pl.BlockSpec(memory_space=pl.ANY)

    def body(*refs):
        outs = refs[n:2 * n]
        send_sems, recv_sems = refs[2 * n:]
        x, y, c, others = _place()
        copies = []
        for a in range(n):
            for j, chip in enumerate(others):
                landed = outs[a].at[2 * chip[0] + chip[1], c]
                copies.append(pltpu.make_async_remote_copy(
                    src_ref=landed, dst_ref=landed, send_sem=send_sems.at[a * 3 + j], recv_sem=recv_sems.at[a * 3 + j],
                    device_id=(x, y, 1 - c), device_id_type=MESH))
        for cp in copies:
            cp.start()
        for a in range(n):
            for j, chip in enumerate(others):
                landed = outs[a].at[2 * chip[0] + chip[1], 1 - c]
                pltpu.make_async_remote_copy(
                    src_ref=landed, dst_ref=landed, send_sem=send_sems.at[a * 3 + j], recv_sem=recv_sems.at[a * 3 + j],
                    device_id=(x, y, c), device_id_type=MESH).wait_recv()
        for cp in copies:
            cp.wait_send()

    out = pl.pallas_call(
        body, name=name,
        in_specs=[any_spec] * n, out_specs=[any_spec] * n,
        out_shape=[jax.ShapeDtypeStruct(b.shape, b.dtype) for b in bufs],
        input_output_aliases={a: a for a in range(n)},
        scratch_shapes=[pltpu.SemaphoreType.DMA((3 * n,)), pltpu.SemaphoreType.DMA((3 * n,))],
    )(*bufs)
    return [_merge2(o, 1) for o in out]


def _swap_halves_bf16(gbs, name):
    n = len(gbs)
    any_spec = pl.BlockSpec(memory_space=pl.ANY)

    def body(*refs):
        ins, outs = refs[:n], refs[n:2 * n]
        send_sems, recv_sems = refs[2 * n:]
        x, y, c, _ = _place()
        copies = []
        for a in range(n):
            copies.append(pltpu.make_async_remote_copy(
                src_ref=ins[a].at[:, 1 - c], dst_ref=outs[a], send_sem=send_sems.at[a], recv_sem=recv_sems.at[a],
                device_id=(x, y, 1 - c), device_id_type=MESH))
        for cp in copies:
            cp.start()
        for cp in copies:
            cp.wait()

    return pl.pallas_call(
        body, name=name,
        in_specs=[any_spec] * n, out_specs=[any_spec] * n,
        out_shape=[jax.ShapeDtypeStruct((g.shape[0], g.shape[1] // 2, g.shape[2]), g.dtype) for g in gbs],
        scratch_shapes=[pltpu.SemaphoreType.DMA((n,)), pltpu.SemaphoreType.DMA((n,))],
    )(*[_split2(g, 1) for g in gbs])


def _scatter_srcs(n):
    return lambda ins: [lambda frm, to, c, r=r: r.at[to] for r in ins[:n]]


def _scatter_dsts(n):
    return lambda ins: [lambda frm, to, c, r=r: r.at[frm] for r in ins[n:]]


def _scatter_start(hbs, name):
    n = len(hbs)
    lands = [lax.empty(h.shape, h.dtype) for h in hbs]
    return _split_start(_scatter_srcs(n), _scatter_dsts(n), list(hbs) + lands, n, name)


def _scatter_wait(send_sems, recv_sems, arrays, after, name):
    n = len(arrays) // 2
    return _split_wait(_scatter_srcs(n), _scatter_dsts(n), send_sems, recv_sems, arrays, after, name)[n:]


def _join_halves(rhs, name):
    n = len(rhs)
    any_spec = pl.BlockSpec(memory_space=pl.ANY)

    def body(*refs):
        outs = refs[n:2 * n]
        send_sems, recv_sems = refs[2 * n:]
        x, y, c, _ = _place()
        copies = []
        for a in range(n):
            copies.append(pltpu.make_async_remote_copy(
                src_ref=outs[a].at[c], dst_ref=outs[a].at[c], send_sem=send_sems.at[a],
                recv_sem=recv_sems.at[a], device_id=(x, y, 1 - c), device_id_type=MESH))
        for cp in copies:
            cp.start()
        for a in range(n):
            landed = outs[a].at[1 - c]
            pltpu.make_async_remote_copy(
                src_ref=landed, dst_ref=landed, send_sem=send_sems.at[a], recv_sem=recv_sems.at[a],
                device_id=(x, y, c), device_id_type=MESH).wait_recv()
        for cp in copies:
            cp.wait_send()

    out = pl.pallas_call(
        body, name=name,
        in_specs=[any_spec] * n, out_specs=[any_spec] * n,
        out_shape=[jax.ShapeDtypeStruct(r.shape, r.dtype) for r in rhs],
        input_output_aliases={a: a for a in range(n)},
        scratch_shapes=[pltpu.SemaphoreType.DMA((n,)), pltpu.SemaphoreType.DMA((n,))],
    )(*rhs)
    return [_merge2(o) for o in out]


FLIPS = [(dx, dy, dc) for dx in (0, 1) for dy in (0, 1) for dc in (0, 1)][1:]


def _peer_copies(ins, send_sems, recv_sems, started):
    x, y, c, _ = _place()
    copies = []
    for k, (dx, dy, dc) in enumerate(FLIPS):
        px, py, pc = jnp.bitwise_xor(x, dx), jnp.bitwise_xor(y, dy), jnp.bitwise_xor(c, dc)
        slot = 4 * x + 2 * y + c if started else 4 * px + 2 * py + pc
        copies.append(pltpu.make_async_remote_copy(
            src_ref=ins[0], dst_ref=ins[1].at[slot], send_sem=send_sems.at[k], recv_sem=recv_sems.at[k],
            device_id=(px, py, pc), device_id_type=MESH))
    return copies


def _small_start(v, name):
    land = lax.empty((8,) + v.shape, v.dtype)
    return _split_start(None, None, [v, land], 0, name, copies_of=_peer_copies, n_sems=len(FLIPS))


def _small_wait(send_sems, recv_sems, arrays, after, name):
    return _split_wait(None, None, send_sems, recv_sems, arrays, after, name, copies_of=_peer_copies)[1]


def _sum_slots(land, v, me):
    rows, cols = v.shape

    def body(me_ref, land_ref, v_ref, o_ref):
        o_ref[...] = jnp.zeros_like(o_ref)
        for d in range(8):
            @pl.when(me_ref[0] == d)
            def _():
                o_ref[...] += v_ref[...]

            @pl.when(me_ref[0] != d)
            def _():
                o_ref[...] += land_ref[d]

    return pl.pallas_call(
        body, name="sum_slots",
        grid_spec=pltpu.PrefetchScalarGridSpec(
            num_scalar_prefetch=1, grid=(1,),
            in_specs=[pl.BlockSpec((8, rows, cols), lambda i, me_ref: (0, 0, 0)),
                      pl.BlockSpec((rows, cols), lambda i, me_ref: (0, 0))],
            out_specs=pl.BlockSpec((rows, cols), lambda i, me_ref: (0, 0))),
        out_shape=jax.ShapeDtypeStruct((rows, cols), F32),
        compiler_params=_cparams(("arbitrary",)),
    )(me, land, v)


def _by_shape(arrays):
    groups = {}
    for k, a in enumerate(arrays):
        groups.setdefault(a.shape, []).append(k)
    return list(groups.values())


def _add_sibling_half(gs, sbs, idx, name):
    n = len(gs)
    ns, r, c = gs[0].shape
    r2 = r // 2

    def body(idx_ref, *refs):
        for a in range(n):
            g_ref, sb_ref, hown_ref, hb_ref = refs[a], refs[n + a], refs[2 * n + a], refs[3 * n + a]
            h = g_ref[...] + sb_ref[...].astype(F32)
            hb_ref[...] = h.astype(BF16)

            @pl.when(pl.program_id(0) == idx_ref[0])
            def _():
                hown_ref[...] = h

    spec = pl.BlockSpec((None, r2, c), lambda s, idx_ref: (s, 0, 0))
    out = pl.pallas_call(
        body, name=name,
        grid_spec=pltpu.PrefetchScalarGridSpec(
            num_scalar_prefetch=1, grid=(ns,),
            in_specs=[pl.BlockSpec((None, r2, c), lambda s, idx_ref: (s, idx_ref[4], 0))] * n + [spec] * n,
            out_specs=[pl.BlockSpec((r2, c), lambda s, idx_ref: (0, 0))] * n + [spec] * n),
        out_shape=[jax.ShapeDtypeStruct((r2, c), F32)] * n + [jax.ShapeDtypeStruct((ns, r2, c), BF16)] * n,
        compiler_params=_cparams(("arbitrary",)),
    )(idx, *gs, *sbs)
    return [(out[a], out[n + a]) for a in range(n)]


def _add_chip_slabs(hs, rbs, idx, name):
    n = len(hs)
    r2, c = hs[0].shape

    def body(idx_ref, *refs):
        for a in range(n):
            h_ref, r0_ref, r1_ref, r2_ref = refs[4 * a:4 * a + 4]
            refs[4 * n + a][...] = ((h_ref[...] + r0_ref[...].astype(F32)) + r1_ref[...].astype(F32)) + r2_ref[...].astype(F32)

    def pick(k):
        return pl.BlockSpec((None, r2, c), lambda i, idx_ref: (idx_ref[k], 0, 0))

    operands = []
    for h, rb in zip(hs, rbs):
        operands += [h, rb, rb, rb]
    out = pl.pallas_call(
        body, name=name,
        grid_spec=pltpu.PrefetchScalarGridSpec(
            num_scalar_prefetch=1, grid=(1,),
            in_specs=[pl.BlockSpec((r2, c), lambda i, idx_ref: (0, 0)), pick(1), pick(2), pick(3)] * n,
            out_specs=[pl.BlockSpec((None, r2, c), lambda i, idx_ref: (idx_ref[4], 0, 0))] * n),
        out_shape=[jax.ShapeDtypeStruct((2, r2, c), F32)] * n,
        compiler_params=_cparams(("arbitrary",)),
    )(idx, *operands)
    return list(out)


ELEMENTWISE_VMEM = 16 * 1024 * 1024


def _adamw(items, name):
    n = len(items)
    r, c = items[0][0].shape
    br = max(b for b in range(8, r + 1, 8) if r % b == 0 and n * 16 * b * c * 4 <= ELEMENTWISE_VMEM) if r % 8 == 0 else r

    def body(*refs):
        for a in range(n):
            g_ref, w_ref, m_ref, v_ref = refs[4 * a:4 * a + 4]
            go_ref, d_ref, nm_ref, nv_ref = refs[4 * n + 4 * a:4 * n + 4 * a + 4]
            gg = g_ref[...]
            go_ref[...] = gg
            nm = B1 * m_ref[...] + (1.0 - B1) * gg
            nv = B2 * v_ref[...] + (1.0 - B2) * jnp.square(gg)
            m_hat = nm / (1.0 - B1 ** STEP)
            v_hat = nv / (1.0 - B2 ** STEP)
            d_ref[...] = -LR * (m_hat / (jnp.sqrt(v_hat) + ADAM_EPS) + WD * w_ref[...])
            nm_ref[...] = nm
            nv_ref[...] = nv

    spec = pl.BlockSpec((br, c), lambda i: (i, 0))
    out = pl.pallas_call(
        body, name=name,
        grid=(r // br,),
        in_specs=[spec] * (4 * n), out_specs=[spec] * (4 * n),
        out_shape=[jax.ShapeDtypeStruct((r, c), F32)] * (4 * n),
        compiler_params=_cparams(("arbitrary",)),
    )(*[a for item in items for a in item])
    return [tuple(out[4 * a:4 * a + 4]) for a in range(n)]


BIG = ("w_in", "w_conv_out", "w_pool", "w_pool_out", "w_o", "w_ffn_gate", "w_ffn_up", "w_ffn_down")
REPL = ("g_mix", "b_gate", "b_dw", "ln_g", "ln_b", "pool_scale", "g_ffn", "g_final")
GROUP_MIX = ("w_conv_out", "w_pool", "w_pool_out", "w_o")
GROUP_FFN = ("w_ffn_gate", "w_ffn_up", "w_ffn_down")
TRANSPOSED = ("w_ffn_gate", "w_ffn_up")
WEIGHT_ORDER = ("meta_tokens", "g_mix", "w_in", "b_gate", "w_dw", "b_dw", "ln_g", "ln_b", "w_conv_out", "w_pool",
                "pool_scale", "w_pool_out", "w_o", "g_ffn", "w_ffn_gate", "w_ffn_up", "w_ffn_down", "g_final")


def _shard2d(name, a):
    a = a[0]
    if name == "w_pool":
        return a.reshape(4 * 64, GD)
    if name in TRANSPOSED:
        return a.T
    return a


def _unshard2d(name, a, shape):
    return a.T.reshape(shape) if name in TRANSPOSED else a.reshape(shape)


def _cols_to_slabs(a):
    m, n = a.shape
    return a.reshape(m, N_SHARD, n // N_SHARD).transpose(1, 0, 2)


def _slabs_to_cols(a):
    ns, m, c = a.shape
    return a.transpose(1, 0, 2).reshape(m, ns * c)


def kernel(x, meta_tokens, g_mix, w_in, b_gate, w_dw, b_dw, ln_g, ln_b, w_conv_out, w_pool, pool_scale, w_pool_out, w_o, g_ffn, w_ffn_gate, w_ffn_up, w_ffn_down, g_final, loss_target, m_meta_tokens, m_g_mix, m_w_in, m_b_gate, m_w_dw, m_b_dw, m_ln_g, m_ln_b, m_w_conv_out, m_w_pool, m_pool_scale, m_w_pool_out, m_w_o, m_g_ffn, m_w_ffn_gate, m_w_ffn_up, m_w_ffn_down, m_g_final, v_meta_tokens, v_g_mix, v_w_in, v_b_gate, v_w_dw, v_b_dw, v_ln_g, v_ln_b, v_w_conv_out, v_w_pool, v_pool_scale, v_w_pool_out, v_w_o, v_g_ffn, v_w_ffn_gate, v_w_ffn_up, v_w_ffn_down, v_g_final):
    args = dict(locals())
    w = {n: args[n] for n in WEIGHT_ORDER}
    mom = {n: args["m_" + n] for n in WEIGHT_ORDER}
    var = {n: args["v_" + n] for n in WEIGHT_ORDER}
    seq = x.shape[1]
    nb = seq // BR + 1
    tp = nb * BR
    tk = tp // 2 if (tp // 2) % 16 == 0 else BR
    t_total = seq + N_META
    cx, cy, cc = lax.axis_index("x"), lax.axis_index("y"), lax.axis_index("c")
    chip = 2 * cx + cy
    chip1 = jnp.reshape(chip, (1,)).astype(jnp.int32)
    core = jnp.reshape(cc, (1,)).astype(jnp.int32)
    others = jnp.sort(jnp.stack([2 * (1 - cx) + cy, 2 * cx + (1 - cy), 2 * (1 - cx) + (1 - cy)]))
    idx = jnp.concatenate([chip1, others.astype(jnp.int32), core])
    xs, target = x[0], loss_target[0]

    tiny = _gather_tiny(jnp.concatenate([w["meta_tokens"], w["w_dw"][0], jnp.zeros((1, GD), F32)], axis=0))
    small = {n: w[n] for n in REPL if n != "g_final"}
    small["g_final"] = w["g_final"].reshape(1, D)
    small["w_dw"] = _slabs_to_cols(tiny[:, N_META:])
    head = jnp.concatenate([jnp.zeros((PAD, D), F32), _slabs_to_cols(tiny[:, :N_META])], axis=0)

    def cast(group, dep):
        return [_cast_into_slot(_shard2d(n, w[n]), chip1, dep, "cast_" + n) for n in group]

    def gather_finish(group, start, after, name):
        landed = _gather_wait(start[0], start[1], start[2], after, "gather_wait_" + name)
        return dict(zip(group, _forward_halves(landed, "forward_" + name)))

    st_in = _gather_start(cast(("w_in",), tiny), None, "gather_start_in")
    bufs_mix, bufs_ffn = cast(GROUP_MIX, st_in[3]), cast(GROUP_FFN, st_in[3])
    u = _rms_u(head, xs, small["g_mix"], nb)
    gw = gather_finish(("w_in",), st_in, [u] + bufs_mix + bufs_ffn, "in")
    st_mix = _gather_start(bufs_mix, gw["w_in"], "gather_start_mix")
    z = _in_proj(u, gw["w_in"], st_mix[3], nb)
    gw.update(gather_finish(GROUP_MIX, st_mix, [z], "mix"))
    st_ffn = _gather_start(bufs_ffn, gw["w_o"], "gather_start_ffn")
    w_pool_b = gw["w_pool"].reshape(N_SHARD, 4, 64, GD).transpose(1, 0, 2, 3).reshape(4, GD, GD)
    w_co_b, w_po_b, w_o_b = (gw[n].reshape(D, D) for n in ("w_conv_out", "w_pool_out", "w_o"))
    h1, yc, yp, mg, ca, cpre, m, mw, m2b = _mixers_fwd(
        z, head, xs, small["b_gate"] + st_ffn[3][0, 0], small["w_dw"], small["b_dw"], small["ln_g"], small["ln_b"],
        small["pool_scale"], w_co_b, w_pool_b, w_po_b, w_o_b, nb, t_total)
    gw.update(gather_finish(GROUP_FFN, st_ffn, [h1], "ffn"))

    dh1, dh1b, vb, fb, dgb, dub, dh2b, loss, dg_ffn, dg_final = _ffn_fwd_bwd(
        h1, target, small["g_ffn"], small["g_final"], gw["w_ffn_gate"].reshape(D_FF, D),
        gw["w_ffn_up"].reshape(D_FF, D), gw["w_ffn_down"].reshape(D_FF, D), nb)

    def slabs(name, g):
        if name == "w_in":
            return g
        if name == "w_pool":
            return g.reshape(4, N_SHARD, 64, GD).transpose(1, 0, 2, 3).reshape(N_SHARD, 4 * 64, GD)
        return g.reshape(N_SHARD, g.shape[0] // N_SHARD, g.shape[1])

    def reduce_start(group, grads, name):
        g32 = [slabs(n, grads[n][0]) for n in group]
        g16 = [slabs(n, grads[n][1]) for n in group]
        from_sibling = _swap_halves_bf16(g16, "swap_halves_" + name)
        halves = [None] * len(group)
        for ks in _by_shape(g32):
            done = _add_sibling_half([g32[k] for k in ks], [from_sibling[k] for k in ks], idx, "add_sibling_" + group[ks[0]])
            for k, pair in zip(ks, done):
                halves[k] = pair
        return [h for h, _ in halves], _scatter_start([hb for _, hb in halves], "scatter_start_" + name)

    def reduce_finish(group, halves, start, after, name):
        from_chips = _scatter_wait(start[0], start[1], start[2], after, "scatter_wait_" + name)
        reduced = [None] * len(group)
        for ks in _by_shape(halves):
            done = _add_chip_slabs([halves[k] for k in ks], [from_chips[k] for k in ks], idx, "add_chips_" + group[ks[0]])
            for k, r in zip(ks, done):
                reduced[k] = r
        return reduced

    half_ff = D_FF // 2
    grads_ffn = {
        "w_ffn_gate": _wgrad(dgb, vb, half_ff, D, tk, "wgrad_ffn_gate"),
        "w_ffn_up": _wgrad(dub, vb, half_ff, D, tk, "wgrad_ffn_up"),
        "w_ffn_down": _wgrad(fb, dh2b, half_ff, D, tk, "wgrad_ffn_down"),
    }
    halves_ffn, sc_ffn = reduce_start(GROUP_FFN, grads_ffn, "ffn")

    dycb, dypb, dzg, dconv, dmwb, dm, db_gate, dln_g, dln_b, db_dw, dps = _mixers_bwd_rows(
        dh1b, yc, yp, z, small["b_gate"], cpre, small["ln_g"], small["ln_b"], mw, small["pool_scale"],
        w_o_b, w_co_b, w_po_b, w_pool_b, sc_ffn[3], nb)
    grads_mix = {
        "w_conv_out": _wgrad(ca, dycb, D, D, tk, "wgrad_conv_out"),
        "w_pool": _wgrad(m, dmwb, GD, GD, tk, "wgrad_pool", diag=True),
        "w_pool_out": _wgrad(m2b, dypb, D, D, tk, "wgrad_pool_out"),
        "w_o": _wgrad(mg, dh1b, D, D, tk, "wgrad_o"),
    }
    halves_mix, sc_mix = reduce_start(GROUP_MIX, grads_mix, "mix")
    dzb, grad_x, dhead, dw_dw, dg_mix = _mixers_bwd_halo(
        dconv, dm, z, dzg, small["w_dw"], head, xs, small["g_mix"], dh1, gw["w_in"], sc_mix[3], nb, t_total)
    packed = jnp.concatenate(
        [dg_mix, db_gate.reshape(2, D), db_dw, dln_g, dln_b, dps, dg_ffn, dg_final,
         jnp.broadcast_to(loss, (1, D)), jnp.zeros((6, D), F32), dhead[PAD:], dw_dw], axis=0)
    sm = _small_start(packed, "small_start")
    grads_in = {"w_in": _wgrad(u, dzb, D, D_IN // N_SHARD, tk, "wgrad_in", col_major=True, dep=sm[3])}
    halves_in, sc_in = reduce_start(("w_in",), grads_in, "in")

    land = _small_wait(sm[0], sm[1], sm[2], [sc_in[3]], "small_wait")
    summed = _sum_slots(land, packed, jnp.reshape(4 * cx + 2 * cy + cc, (1,)).astype(jnp.int32))
    loss = summed[9, 0]

    first = GROUP_FFN + GROUP_MIX
    reduced_half = reduce_finish(GROUP_FFN, halves_ffn, sc_ffn, [summed], "ffn")
    reduced_half += reduce_finish(GROUP_MIX, halves_mix, sc_mix, [summed], "mix")
    reduced = dict(zip(first, _join_halves(reduced_half, "join_halves_first")))
    updates = {}
    for ks in _by_shape([reduced[n] for n in first]):
        names = [first[k] for k in ks]
        done = _adamw([(reduced[n], _shard2d(n, w[n]), _shard2d(n, mom[n]), _shard2d(n, var[n])) for n in names],
                      "adamw_" + names[0])
        updates.update(zip(names, done))

    def repl_stack(d):
        return jnp.concatenate([d["g_mix"], d["b_gate"].reshape(2, D), d["b_dw"], d["ln_g"], d["ln_b"],
                                d["pool_scale"], d["g_ffn"], d["g_final"].reshape(1, D), jnp.ones((7, D), F32)], axis=0)

    def shard_stack(d):
        return jnp.concatenate([d["meta_tokens"], d["w_dw"][0], jnp.ones((1, GD), F32)], axis=0)

    g_repl = summed[0:16]
    g_shard = lax.dynamic_slice_in_dim(summed[16:64], chip * GD, GD, axis=1)
    g_repl, d_repl, m_repl, v_repl = _adamw([(g_repl, repl_stack(w), repl_stack(mom), repl_stack(var))], "adamw_repl")[0]
    g_shard, d_shard, m_shard, v_shard = _adamw(
        [(g_shard, shard_stack(w), shard_stack(mom), shard_stack(var))], "adamw_cols")[0]

    done_first = [updates[n][1] for n in first] + [d_repl, d_shard]
    last_half = reduce_finish(("w_in",), halves_in, sc_in, done_first, "in")
    reduced["w_in"] = _join_halves(last_half, "join_halves_in")[0]
    updates["w_in"] = _adamw([(reduced["w_in"], w["w_in"][0], mom["w_in"][0], var["w_in"][0])], "adamw_w_in")[0]

    def unpack(name, repl, shard):
        if name == "meta_tokens":
            return shard[0:N_META]
        if name == "w_dw":
            return shard[N_META:N_META + KW].reshape(1, KW, GD)
        row = {"g_mix": 0, "b_gate": 1, "b_dw": 3, "ln_g": 4, "ln_b": 5, "pool_scale": 6, "g_ffn": 7, "g_final": 8}[name]
        if name == "b_gate":
            return repl[1:3].reshape(1, 2 * D)
        if name == "g_final":
            return repl[8]
        return repl[row:row + 1]

    out_g, out_d, out_m, out_v = {}, {}, {}, {}
    for n in WEIGHT_ORDER:
        if n in BIG:
            g, d_, m_, v_ = updates[n]
            shape = w[n].shape
            out_g[n], out_d[n], out_m[n], out_v[n] = (_unshard2d(n, a, shape) for a in (g, d_, m_, v_))
        else:
            out_g[n] = unpack(n, g_repl, g_shard)
            out_d[n] = unpack(n, d_repl, d_shard)
            out_m[n] = unpack(n, m_repl, m_shard)
            out_v[n] = unpack(n, v_repl, v_shard)
    return (loss, grad_x[None], *[out_g[n] for n in WEIGHT_ORDER], *[out_d[n] for n in WEIGHT_ORDER],
            *[out_m[n] for n in WEIGHT_ORDER], *[out_v[n] for n in WEIGHT_ORDER])
```

```python
import jax
import jax.numpy as jnp
from jax import lax
from jax.experimental import pallas as pl
from jax.experimental.pallas import tpu as pltpu

F32 = jnp.float32
BF16 = jnp.bfloat16
MESH = pl.DeviceIdType.MESH

D = 1024
N_META = 16
KW = 31
POOL_WINDOWS = (2, 4, 8, 16)
GD = 256
D_IN = 5 * D
D_FF = 2816
N_SHARD = 4
BR = 256
HALO = 16
PAD = BR - N_META
EXT = BR + 2 * HALO
RMS_EPS = 1e-6
LN_EPS = 1e-5
LR, B1, B2, ADAM_EPS, WD, STEP = 0.001, 0.9, 0.999, 1e-08, 0.01, 10
VMEM_LIMIT = 56 * 1024 * 1024


def _cparams(sem, vmem=VMEM_LIMIT):
    return pltpu.CompilerParams(dimension_semantics=sem, vmem_limit_bytes=vmem)


def _dot(a, b):
    return jnp.dot(a, b, preferred_element_type=F32)


def _dot_nt(a, b):
    return lax.dot_general(a, b, (((1,), (1,)), ((), ())), preferred_element_type=F32)


def _dot_tn(a, b):
    return lax.dot_general(a, b, (((0,), (0,)), ((), ())), preferred_element_type=F32)


def _sigmoid(x):
    return 0.5 * jnp.tanh(0.5 * x) + 0.5


def _row_ids(i, n, offset=0):
    return lax.broadcasted_iota(jnp.int32, (n, 1), 0) + (i * BR + offset - PAD)


def _pool_cnt(t, w, t_total):
    left = w // 2
    right = w - 1 - left
    lo = jnp.clip(t - left, 0, t_total)
    hi = jnp.clip(t + right + 1, 0, t_total)
    return jnp.maximum(hi - lo, 1).astype(F32)


def _halo_specs(nb, halo_width=D, width=D):
    last = nb * (BR // HALO) - 1
    return [
        pl.BlockSpec((HALO, halo_width), lambda i: (jnp.maximum(i * (BR // HALO) - 1, 0), 0)),
        pl.BlockSpec((BR, width), lambda i: (i, 0)),
        pl.BlockSpec((HALO, halo_width), lambda i: (jnp.minimum((i + 1) * (BR // HALO), last), 0)),
    ]


def _cols(ref, n):
    return [ref.at[:, k * D:(k + 1) * D] for k in range(n)]


def _fill_ext(ext_ref, prev, cur, nxt, i, nb):
    ext_ref[0:HALO, :] = jnp.where(i > 0, prev, 0.0)
    ext_ref[HALO:HALO + BR, :] = cur
    ext_ref[HALO + BR:EXT, :] = jnp.where(i < nb - 1, nxt, 0.0)


ROT_ROWS = EXT - 8


def _fill_rot(rot_ref, ext_ref, lanes):
    for r in range(1, 8):
        rot_ref[r] = ext_ref[pl.ds(r, ROT_ROWS), lanes]


def _tap(rot_ref, ext_ref, lanes, offset):
    q, r = divmod(offset, 8)
    if r == 0:
        return ext_ref[pl.ds(8 * q, BR), lanes]
    return rot_ref[r, pl.ds(8 * q, BR), :]


def _row_spec(width=D):
    return pl.BlockSpec((BR, width), lambda i: (i, 0))


def _x_spec():
    return pl.BlockSpec((BR, D), lambda i: (jnp.maximum(i - 1, 0), 0))


def _const_spec(shape):
    nd = len(shape)
    return pl.BlockSpec(shape, lambda i: (0,) * nd)


def _rms_u(head, x, g_mix, nb):
    def body(head_ref, x_ref, g_ref, u_ref):
        i = pl.program_id(0)
        h = jnp.where(i == 0, head_ref[...], x_ref[...])
        r = lax.rsqrt(jnp.mean(h * h, axis=-1, keepdims=True) + RMS_EPS)
        u_ref[...] = ((h * r) * g_ref[...]).astype(BF16)

    return pl.pallas_call(
        body, name="rms_u",
        grid=(nb,),
        in_specs=[_const_spec((BR, D)), _x_spec(), _const_spec((1, D))],
        out_specs=_row_spec(),
        out_shape=jax.ShapeDtypeStruct((nb * BR, D), BF16),
        compiler_params=_cparams(("arbitrary",)),
    )(head, x, g_mix)


def _in_proj_rows(tp):
    return tp // 4 if (tp // 4) % 16 == 0 else BR


def _in_proj_own(u, w_own, idx, nb):
    tp = nb * BR
    wcols = w_own.shape[1]
    rows = _in_proj_rows(tp)

    def body(idx_ref, u_ref, w_ref, z_ref, wb_ref):
        @pl.when(pl.program_id(0) == 0)
        def _():
            wb_ref[...] = w_ref[...].astype(BF16)

        z_ref[...] = _dot(u_ref[...], wb_ref[...])

    return pl.pallas_call(
        body, name="in_proj_own",
        grid_spec=pltpu.PrefetchScalarGridSpec(
            num_scalar_prefetch=1, grid=(tp // rows,),
            in_specs=[pl.BlockSpec((rows, D), lambda i, idx_ref: (i, 0)),
                      pl.BlockSpec((D, wcols), lambda i, idx_ref: (0, 0))],
            out_specs=pl.BlockSpec((rows, wcols), lambda i, idx_ref: (i, idx_ref[0])),
            scratch_shapes=[pltpu.VMEM((D, wcols), BF16)]),
        out_shape=jax.ShapeDtypeStruct((tp, N_SHARD * wcols), F32),
        compiler_params=_cparams(("arbitrary",)),
    )(idx, u, w_own)


def _in_proj_rest(u, w_in_b, z, idx, dep, nb):
    tp = nb * BR
    wcols = w_in_b.shape[2]
    rows = _in_proj_rows(tp)

    def body(idx_ref, u_ref, w_ref, z_in, dep_ref, z_ref):
        z_ref[...] = _dot(u_ref[...], w_ref[...])

    any_spec = pl.BlockSpec(memory_space=pl.ANY)
    return pl.pallas_call(
        body, name="in_proj_rest",
        grid_spec=pltpu.PrefetchScalarGridSpec(
            num_scalar_prefetch=1, grid=(N_SHARD - 1, tp // rows),
            in_specs=[pl.BlockSpec((rows, D), lambda s, i, idx_ref: (i, 0)),
                      pl.BlockSpec((None, D, wcols), lambda s, i, idx_ref: (idx_ref[1 + s], 0, 0)),
                      any_spec, any_spec],
            out_specs=pl.BlockSpec((rows, wcols), lambda s, i, idx_ref: (i, idx_ref[1 + s]))),
        out_shape=jax.ShapeDtypeStruct(z.shape, F32),
        input_output_aliases={3: 0},
        compiler_params=_cparams(("arbitrary", "arbitrary")),
    )(idx, u, w_in_b, z, dep)


def _mixers_fwd(z, head, x, b_gate, w_dw, b_dw, ln_g, ln_b, pool_scale, w_co, w_pool, w_po, w_o, nb, t_total):
    tp = nb * BR

    def body(z_prev, z_cur, z_next, head_ref, x_ref, bg_ref, wdw_ref, bdw_ref,
             lng_ref, lnb_ref, ps_ref, wco_ref, wpool_ref, wpo_ref, wo_ref,
             h1_ref, yc_ref, yp_ref, mg_ref, ca_ref, cpre_ref, m_ref, mw_ref, m2b_ref, ext_ref, pext_ref, rot_ref):
        i = pl.program_id(0)
        avp, agp, pp = _cols(z_prev, 3)
        av, ag, pc, za, zb = _cols(z_cur, 5)
        avn, agn, pn = _cols(z_next, 3)
        _fill_ext(ext_ref, avp[...] * _sigmoid(agp[...]), av[...] * _sigmoid(ag[...]),
                  avn[...] * _sigmoid(agn[...]), i, nb)
        _fill_ext(pext_ref, pp[...], pc[...], pn[...], i, nb)

        def conv_chunk(c, carry):
            lanes = pl.ds(pl.multiple_of(c * 128, 128), 128)
            _fill_rot(rot_ref, ext_ref, lanes)
            acc = jnp.broadcast_to(bdw_ref[:, lanes], (BR, 128))
            for k in range(KW):
                acc = acc + wdw_ref[k:k + 1, lanes] * _tap(rot_ref, ext_ref, lanes, 1 + k)
            cpre_ref[:, lanes] = acc
            return carry
        lax.fori_loop(0, D // 128, conv_chunk, 0)

        conv = cpre_ref[...]
        mu = jnp.mean(conv, axis=-1, keepdims=True)
        xc = conv - mu
        rstd = lax.rsqrt(jnp.mean(xc * xc, axis=-1, keepdims=True) + LN_EPS)
        ln = (xc * rstd) * lng_ref[...] + lnb_ref[...]
        cact = (ln * _sigmoid(ln)).astype(BF16)
        ca_ref[...] = cact
        y_conv = _dot(cact, wco_ref[...])
        yc_ref[...] = y_conv

        t = _row_ids(i, BR)
        for gi, w in enumerate(POOL_WINDOWS):
            left = w // 2
            right = w - 1 - left
            lanes = slice(gi * GD, (gi + 1) * GD)
            s = pext_ref[pl.ds(HALO - left, BR), lanes]
            for j in range(-left + 1, right + 1):
                s = s + pext_ref[pl.ds(HALO + j, BR), lanes]
            m = (s / _pool_cnt(t, w, t_total) - pext_ref[HALO:HALO + BR, lanes]).astype(BF16)
            m_ref[:, lanes] = m
            mw_ref[:, lanes] = _dot(m, wpool_ref[gi])
        mw = mw_ref[...]
        m2b = (mw * ps_ref[...]).astype(BF16)
        m2b_ref[...] = m2b
        y_pool = _dot(m2b, wpo_ref[...])
        yp_ref[...] = y_pool

        s_a = _sigmoid(za[...] + bg_ref[:, 0:D])
        s_b = _sigmoid(zb[...] + bg_ref[:, D:2 * D])
        merged = (s_a * y_conv + s_b * y_pool).astype(BF16)
        mg_ref[...] = merged
        h0 = jnp.where(i == 0, head_ref[...], x_ref[...])
        h1_ref[...] = h0 + _dot(merged, wo_ref[...])

    in_specs = (_halo_specs(nb, 3 * D, 5 * D)
                + [_const_spec((BR, D)), _x_spec(), _const_spec((1, 2 * D)), _const_spec((32, D)),
                   _const_spec((1, D)), _const_spec((1, D)), _const_spec((1, D)), _const_spec((1, D)),
                   _const_spec((D, D)), _const_spec((4, GD, GD)), _const_spec((D, D)), _const_spec((D, D))])
    outs = [(F32, "h1"), (F32, "yc"), (F32, "yp"), (BF16, "mg"), (BF16, "ca"), (F32, "cpre"), (BF16, "m"), (F32, "mw"),
            (BF16, "m2b")]
    return pl.pallas_call(
        body, name="mixers_fwd",
        grid=(nb,),
        in_specs=in_specs,
        out_specs=[_row_spec() for _ in outs],
        out_shape=[jax.ShapeDtypeStruct((tp, D), dt) for dt, _ in outs],
        scratch_shapes=[pltpu.VMEM((EXT, D), F32), pltpu.VMEM((EXT, D), F32), pltpu.VMEM((8, ROT_ROWS, 128), F32)],
        compiler_params=_cparams(("arbitrary",)),
    )(z, z, z, head, x, b_gate, w_dw, b_dw, ln_g, ln_b, pool_scale, w_co, w_pool, w_po, w_o)


def _ffn_fwd_bwd(h1, target, g_ffn, g_final, w_g, w_u, w_d, nb):
    tp = nb * BR

    def body(h1_ref, tgt_ref, gf_ref, gfin_ref, wg_hbm, wu_hbm, wd_hbm,
             dh1_ref, dh1b_ref, vb_ref, fb_ref, dgb_ref, dub_ref, dh2b_ref, loss_ref, dgf_ref, dgfin_ref,
             wg_ref, wu_ref, wd_ref, sem):
        i = pl.program_id(0)

        @pl.when(i == 0)
        def _():
            copies = [pltpu.make_async_copy(wg_hbm, wg_ref, sem.at[0]),
                      pltpu.make_async_copy(wu_hbm, wu_ref, sem.at[1]),
                      pltpu.make_async_copy(wd_hbm, wd_ref, sem.at[2])]
            for cp in copies:
                cp.start()
            loss_ref[...] = jnp.zeros_like(loss_ref)
            dgf_ref[...] = jnp.zeros_like(dgf_ref)
            dgfin_ref[...] = jnp.zeros_like(dgfin_ref)
            for cp in copies:
                cp.wait()

        h1 = h1_ref[...]
        r1 = lax.rsqrt(jnp.mean(h1 * h1, axis=-1, keepdims=True) + RMS_EPS)
        vn = h1 * r1
        vb = (vn * gf_ref[...]).astype(BF16)
        vb_ref[...] = vb
        g = _dot_nt(vb, wg_ref[...])
        up = _dot_nt(vb, wu_ref[...])
        sg = _sigmoid(g)
        sl = g * sg
        fb = (sl * up).astype(BF16)
        fb_ref[...] = fb
        h2 = h1 + _dot(fb, wd_ref[...])
        r2 = lax.rsqrt(jnp.mean(h2 * h2, axis=-1, keepdims=True) + RMS_EPS)
        yn = h2 * r2
        valid = i > 0
        diff = jnp.where(valid, yn * gfin_ref[...] - tgt_ref[...], 0.0)
        loss_ref[...] += 0.5 * jnp.sum(jnp.mean(diff * diff, axis=-1, keepdims=True))
        dy = diff * (1.0 / D)
        dgfin_ref[...] += jnp.sum(dy * yn, axis=0, keepdims=True)
        gd = dy * gfin_ref[...]
        dh2 = r2 * (gd - yn * jnp.mean(yn * gd, axis=-1, keepdims=True))
        dh2b = dh2.astype(BF16)
        dh2b_ref[...] = dh2b
        df = _dot_nt(dh2b, wd_ref[...])
        dub = (df * sl).astype(BF16)
        dgb = (df * up * (sg * (1.0 + g * (1.0 - sg)))).astype(BF16)
        dub_ref[...] = dub
        dgb_ref[...] = dgb
        dv = _dot(dgb, wg_ref[...]) + _dot(dub, wu_ref[...])
        dgf_ref[...] += jnp.sum(dv * vn, axis=0, keepdims=True)
        gd1 = dv * gf_ref[...]
        dh1 = dh2 + r1 * (gd1 - vn * jnp.mean(vn * gd1, axis=-1, keepdims=True))
        dh1_ref[...] = dh1
        dh1b_ref[...] = dh1.astype(BF16)

    any_spec = pl.BlockSpec(memory_space=pl.ANY)
    return pl.pallas_call(
        body, name="ffn_fwd_bwd",
        grid=(nb,),
        in_specs=[_row_spec(), _x_spec(), _const_spec((1, D)), _const_spec((1, D)), any_spec, any_spec, any_spec],
        out_specs=[_row_spec(), _row_spec(), _row_spec(), _row_spec(D_FF), _row_spec(D_FF), _row_spec(D_FF), _row_spec(),
                   _const_spec((1, 1)), _const_spec((1, D)), _const_spec((1, D))],
        out_shape=[jax.ShapeDtypeStruct((tp, D), F32), jax.ShapeDtypeStruct((tp, D), BF16),
                   jax.ShapeDtypeStruct((tp, D), BF16), jax.ShapeDtypeStruct((tp, D_FF), BF16),
                   jax.ShapeDtypeStruct((tp, D_FF), BF16), jax.ShapeDtypeStruct((tp, D_FF), BF16),
                   jax.ShapeDtypeStruct((tp, D), BF16), jax.ShapeDtypeStruct((1, 1), F32),
                   jax.ShapeDtypeStruct((1, D), F32), jax.ShapeDtypeStruct((1, D), F32)],
        scratch_shapes=[pltpu.VMEM((D_FF, D), BF16), pltpu.VMEM((D_FF, D), BF16), pltpu.VMEM((D_FF, D), BF16),
                        pltpu.SemaphoreType.DMA((3,))],
        compiler_params=_cparams(("arbitrary",)),
    )(h1, target, g_ffn, g_final, w_g, w_u, w_d)


def _mixers_bwd_rows(dh1b, yc, yp, z, b_gate, cpre, ln_g, ln_b, mw, pool_scale, w_o, w_co, w_po, w_pool, dep, nb):
    tp = nb * BR

    def body(dh1b_ref, yc_ref, yp_ref, za, zb, bg_ref, cpre_ref, lng_ref, lnb_ref, mw_ref, ps_ref,
             wo_ref, wco_ref, wpo_ref, wpool_ref, dep_ref,
             dycb_ref, dypb_ref, dzg_ref, dconv_ref, dmwb_ref, dm_ref, dbg_ref, dlng_ref, dlnb_ref, dbdw_ref, dps_ref):
        i = pl.program_id(0)

        @pl.when(i == 0)
        def _():
            for r in (dbg_ref, dlng_ref, dlnb_ref, dbdw_ref, dps_ref):
                r[...] = jnp.zeros_like(r)

        dmg = _dot_nt(dh1b_ref[...], wo_ref[...])
        s_a = _sigmoid(za[...] + bg_ref[:, 0:D])
        s_b = _sigmoid(zb[...] + bg_ref[:, D:2 * D])
        dycb = (dmg * s_a).astype(BF16)
        dypb = (dmg * s_b).astype(BF16)
        dycb_ref[...] = dycb
        dypb_ref[...] = dypb
        dza = dmg * yc_ref[...] * (s_a * (1.0 - s_a))
        dzb = dmg * yp_ref[...] * (s_b * (1.0 - s_b))
        dzg_ref[:, 0:D] = dza.astype(BF16)
        dzg_ref[:, D:2 * D] = dzb.astype(BF16)
        dbg_ref[:, 0:D] += jnp.sum(dza, axis=0, keepdims=True)
        dbg_ref[:, D:2 * D] += jnp.sum(dzb, axis=0, keepdims=True)

        dca = _dot_nt(dycb, wco_ref[...])
        conv = cpre_ref[...]
        mu = jnp.mean(conv, axis=-1, keepdims=True)
        xc = conv - mu
        rstd = lax.rsqrt(jnp.mean(xc * xc, axis=-1, keepdims=True) + LN_EPS)
        xhat = xc * rstd
        ln = xhat * lng_ref[...] + lnb_ref[...]
        sg = _sigmoid(ln)
        dln = dca * (sg * (1.0 + ln * (1.0 - sg)))
        dlng_ref[...] += jnp.sum(dln * xhat, axis=0, keepdims=True)
        dlnb_ref[...] += jnp.sum(dln, axis=0, keepdims=True)
        dxh = dln * lng_ref[...]
        dconv = rstd * (dxh - jnp.mean(dxh, axis=-1, keepdims=True)
                        - xhat * jnp.mean(dxh * xhat, axis=-1, keepdims=True))
        dconv_ref[...] = dconv
        dbdw_ref[...] += jnp.sum(dconv, axis=0, keepdims=True)

        dm2 = _dot_nt(dypb, wpo_ref[...])
        dps_ref[...] += jnp.sum(dm2 * mw_ref[...], axis=0, keepdims=True)
        dmwb = (dm2 * ps_ref[...]).astype(BF16)
        dmwb_ref[...] = dmwb
        for gi in range(len(POOL_WINDOWS)):
            lanes = slice(gi * GD, (gi + 1) * GD)
            dm_ref[:, lanes] = _dot_nt(dmwb[:, lanes], wpool_ref[gi])

    in_specs = [_row_spec(), _row_spec(), _row_spec(),
                pl.BlockSpec((BR, D), lambda i: (i, 3)), pl.BlockSpec((BR, D), lambda i: (i, 4)),
                _const_spec((1, 2 * D)), _row_spec(), _const_spec((1, D)), _const_spec((1, D)), _row_spec(),
                _const_spec((1, D)), _const_spec((D, D)), _const_spec((D, D)), _const_spec((D, D)),
                _const_spec((4, GD, GD)), pl.BlockSpec(memory_space=pl.ANY)]
    return pl.pallas_call(
        body, name="mixers_bwd_rows",
        grid=(nb,),
        in_specs=in_specs,
        out_specs=[_row_spec(), _row_spec(), _row_spec(2 * D), _row_spec(), _row_spec(), _row_spec(),
                   _const_spec((1, 2 * D)), _const_spec((1, D)), _const_spec((1, D)), _const_spec((1, D)),
                   _const_spec((1, D))],
        out_shape=[jax.ShapeDtypeStruct((tp, D), BF16), jax.ShapeDtypeStruct((tp, D), BF16),
                   jax.ShapeDtypeStruct((tp, 2 * D), BF16), jax.ShapeDtypeStruct((tp, D), F32),
                   jax.ShapeDtypeStruct((tp, D), BF16), jax.ShapeDtypeStruct((tp, D), F32),
                   jax.ShapeDtypeStruct((1, 2 * D), F32), jax.ShapeDtypeStruct((1, D), F32),
                   jax.ShapeDtypeStruct((1, D), F32), jax.ShapeDtypeStruct((1, D), F32),
                   jax.ShapeDtypeStruct((1, D), F32)],
        compiler_params=_cparams(("arbitrary",)),
    )(dh1b, yc, yp, z, z, b_gate, cpre, ln_g, ln_b, mw, pool_scale, w_o, w_co, w_po, w_pool, dep)


def _mixers_bwd_halo(dconv, dm, z, dzg, w_dw, head, x, g_mix, dh1, w_in_b, dep, nb, t_total):
    tp = nb * BR
    ns = w_in_b.shape[0]
    wcols = w_in_b.shape[2]
    seq = x.shape[0]

    def body(dcp, dcc, dcn, dmp, dmc, dmn, z_prev, z_cur, z_next, dzg_ref, wdw_ref, head_ref, x_ref, g_ref,
             dh1_ref, w_hbm, dep_ref,
             dzb_ref, gx_ref, dhead_ref, dwdw_ref, dgmix_ref,
             w_ref, sem, aext_ref, dext_ref, qext_ref, da_ref, rot_ref, dwp_ref):
        i = pl.program_id(0)
        (avp, agp), (av, ag), (avn, agn) = _cols(z_prev, 2), _cols(z_cur, 2), _cols(z_next, 2)

        @pl.when(i == 0)
        def _():
            cp = pltpu.make_async_copy(w_hbm, w_ref, sem.at[0])
            cp.start()
            dwp_ref[...] = jnp.zeros_like(dwp_ref)
            dgmix_ref[...] = jnp.zeros_like(dgmix_ref)
            cp.wait()

        sig_g = _sigmoid(ag[...])
        _fill_ext(aext_ref, avp[...] * _sigmoid(agp[...]), av[...] * sig_g, avn[...] * _sigmoid(agn[...]), i, nb)
        _fill_ext(dext_ref, dcp[...], dcc[...], dcn[...], i, nb)
        _fill_ext(qext_ref, dmp[...], dmc[...], dmn[...], i, nb)

        def conv_chunk(c, carry):
            lanes = pl.ds(pl.multiple_of(c * 128, 128), 128)
            _fill_rot(rot_ref, dext_ref, lanes)
            acc = jnp.zeros((BR, 128), F32)
            for k in range(KW):
                acc = acc + wdw_ref[k:k + 1, lanes] * _tap(rot_ref, dext_ref, lanes, KW - k)
            da_ref[:, lanes] = acc
            _fill_rot(rot_ref, aext_ref, lanes)
            dcv = dext_ref[HALO:HALO + BR, lanes]
            for k in range(KW):
                prod = _tap(rot_ref, aext_ref, lanes, 1 + k) * dcv
                dwp_ref[k, :, lanes] += jnp.sum(prod.reshape(BR // 8, 8, 128), axis=0)
            return carry
        lax.fori_loop(0, D // 128, conv_chunk, 0)

        @pl.when(i == nb - 1)
        def _():
            dwdw_ref[...] = jnp.sum(dwp_ref[...], axis=1)

        da = da_ref[...]
        a_val = av[...]
        dzb_ref[:, 0:D] = (da * sig_g).astype(BF16)
        dzb_ref[:, D:2 * D] = (da * a_val * (sig_g * (1.0 - sig_g))).astype(BF16)

        t_ext = _row_ids(i, EXT, -HALO)
        for gi, w in enumerate(POOL_WINDOWS):
            left = w // 2
            right = w - 1 - left
            lanes = slice(gi * GD, (gi + 1) * GD)
            qext_ref[:, lanes] = qext_ref[:, lanes] / _pool_cnt(t_ext, w, t_total)
            s = qext_ref[pl.ds(HALO - right, BR), lanes]
            for j in range(-right + 1, left + 1):
                s = s + qext_ref[pl.ds(HALO + j, BR), lanes]
            dzb_ref[:, 2 * D + gi * GD:2 * D + (gi + 1) * GD] = (s - dmc[:, lanes]).astype(BF16)
        dzb_ref[:, 3 * D:5 * D] = dzg_ref[...]

        du = _dot_nt(dzb_ref[:, 0:wcols], w_ref[0])
        for s_i in range(1, ns):
            du = du + _dot_nt(dzb_ref[:, s_i * wcols:(s_i + 1) * wcols], w_ref[s_i])
        h0 = jnp.where(i == 0, head_ref[...], x_ref[...])
        r0 = lax.rsqrt(jnp.mean(h0 * h0, axis=-1, keepdims=True) + RMS_EPS)
        un = h0 * r0
        dgmix_ref[...] += jnp.sum(du * un, axis=0, keepdims=True)
        gd = du * g_ref[...]
        dh0 = dh1_ref[...] + r0 * (gd - un * jnp.mean(un * gd, axis=-1, keepdims=True))
        gx_ref[...] = dh0

        @pl.when(i == 0)
        def _():
            dhead_ref[...] = dh0

    any_spec = pl.BlockSpec(memory_space=pl.ANY)
    in_specs = (_halo_specs(nb) + _halo_specs(nb) + _halo_specs(nb, 2 * D, 2 * D)
                + [_row_spec(2 * D), _const_spec((32, D)), _const_spec((BR, D)), _x_spec(), _const_spec((1, D)),
                   _row_spec(), any_spec, any_spec])
    return pl.pallas_call(
        body, name="mixers_bwd_halo",
        grid=(nb,),
        in_specs=in_specs,
        out_specs=[_row_spec(D_IN), _x_spec(), _const_spec((BR, D)), _const_spec((32, D)), _const_spec((1, D))],
        out_shape=[jax.ShapeDtypeStruct((tp, D_IN), BF16), jax.ShapeDtypeStruct((seq, D), F32),
                   jax.ShapeDtypeStruct((BR, D), F32), jax.ShapeDtypeStruct((32, D), F32),
                   jax.ShapeDtypeStruct((1, D), F32)],
        scratch_shapes=[pltpu.VMEM((ns, D, wcols), BF16), pltpu.SemaphoreType.DMA((1,)),
                        pltpu.VMEM((EXT, D), F32), pltpu.VMEM((EXT, D), F32), pltpu.VMEM((EXT, D), F32),
                        pltpu.VMEM((BR, D), F32), pltpu.VMEM((8, ROT_ROWS, 128), F32), pltpu.VMEM((32, 8, D), F32)],
        compiler_params=_cparams(("arbitrary",)),
    )(dconv, dconv, dconv, dm, dm, dm, z, z, z, dzg, w_dw, head, x, g_mix, dh1, w_in_b, dep)


def _wgrad(a, c, tm, tn, tk, name, diag=False, col_major=False, dep=None):
    tp, m = a.shape
    n = c.shape[1]
    nk = tp // tk
    gm, gn = m // tm, n // tn

    def body(a_ref, c_ref, *rest):
        o_ref, ob_ref = rest[-2:]
        k = pl.program_id(2)

        @pl.when(k == 0)
        def _():
            o_ref[...] = jnp.zeros_like(o_ref)

        o_ref[...] += _dot_tn(a_ref[...], c_ref[...])

        @pl.when(k == nk - 1)
        def _():
            ob_ref[...] = o_ref[...].astype(BF16)

    c_map = lambda i, j, k: (k, j)
    grid = (gm, gn, nk)
    deps = [] if dep is None else [dep]
    if diag:
        grid = (gm, 1, nk)
        c_map = lambda i, j, k: (k, i)
        o_spec = pl.BlockSpec((tm, tn), lambda i, j, k: (i, 0))
        o_shape = (m, tn)
    elif col_major:
        o_spec = pl.BlockSpec((None, tm, tn), lambda i, j, k: (j, i, 0))
        o_shape = (gn, m, tn)
    else:
        o_spec = pl.BlockSpec((tm, tn), lambda i, j, k: (i, j))
        o_shape = (m, n)
    return pl.pallas_call(
        body, name=name,
        grid=grid,
        in_specs=[pl.BlockSpec((tk, tm), lambda i, j, k: (k, i)), pl.BlockSpec((tk, tn), c_map)]
        + [pl.BlockSpec(memory_space=pl.ANY)] * len(deps),
        out_specs=[o_spec, o_spec],
        out_shape=[jax.ShapeDtypeStruct(o_shape, F32), jax.ShapeDtypeStruct(o_shape, BF16)],
        compiler_params=_cparams(("arbitrary", "arbitrary", "arbitrary")),
    )(a, c, *deps)


def _place():
    x, y, c = lax.axis_index("x"), lax.axis_index("y"), lax.axis_index("c")
    others = [(1 - x, y), (x, 1 - y), (1 - x, 1 - y)]
    return x, y, c, others


def _split2(a, axis=0):
    return a.reshape(a.shape[:axis] + (2, a.shape[axis] // 2) + a.shape[axis + 1:])


def _merge2(a, axis=0):
    return a.reshape(a.shape[:axis] + (2 * a.shape[axis + 1],) + a.shape[axis + 2:])


def _cast_into_slot(w2d, chip, dep, name):
    r, c = w2d.shape
    r2 = r // 2

    def body(chip_ref, w_ref, dep_ref, o_ref):
        o_ref[...] = w_ref[...].astype(BF16)

    return pl.pallas_call(
        body, name=name,
        grid_spec=pltpu.PrefetchScalarGridSpec(
            num_scalar_prefetch=1, grid=(2,),
            in_specs=[pl.BlockSpec((r2, c), lambda h, chip_ref: (h, 0)), pl.BlockSpec(memory_space=pl.ANY)],
            out_specs=pl.BlockSpec((None, None, r2, c), lambda h, chip_ref: (chip_ref[0], h, 0, 0))),
        out_shape=jax.ShapeDtypeStruct((N_SHARD, 2, r2, c), BF16),
        compiler_params=_cparams(("arbitrary",)),
    )(chip, w2d, dep)


HBM_SPEC = pl.BlockSpec(memory_space=pltpu.HBM)
SEM_SPEC = pl.BlockSpec(memory_space=pltpu.SEMAPHORE)
DATAFLOW = pltpu.SideEffectType.DATAFLOW_SIDE_EFFECTING
TOKEN = jax.ShapeDtypeStruct((8, 128), F32)


def _in_hbm(a):
    return pltpu.with_memory_space_constraint(a, pltpu.HBM)


def _gather_tiny(v):
    vm = pl.BlockSpec(memory_space=pltpu.VMEM)

    def body(v_ref, out_ref, send_sems, recv_sems):
        x, y, c, others = _place()
        mine = 2 * x + y
        sends = [pltpu.make_async_remote_copy(
            src_ref=v_ref, dst_ref=out_ref.at[mine], send_sem=send_sems.at[j], recv_sem=recv_sems.at[j],
            device_id=(*chip, c), device_id_type=MESH) for j, chip in enumerate(others)]
        for cp in sends:
            cp.start()
        out_ref[mine] = v_ref[...]
        for j, chip in enumerate(others):
            landed = out_ref.at[2 * chip[0] + chip[1]]
            pltpu.make_async_remote_copy(
                src_ref=landed, dst_ref=landed, send_sem=send_sems.at[j], recv_sem=recv_sems.at[j],
                device_id=(x, y, c), device_id_type=MESH).wait_recv()
        for cp in sends:
            cp.wait_send()

    return pl.pallas_call(
        body, name="gather_tiny",
        in_specs=[vm], out_specs=vm,
        out_shape=jax.ShapeDtypeStruct((N_SHARD,) + v.shape, v.dtype),
        scratch_shapes=[pltpu.SemaphoreType.DMA((3,)), pltpu.SemaphoreType.DMA((3,))],
    )(v)


def _ici_copies(srcs, dsts, send_sems, recv_sems, started):
    x, y, c, others = _place()
    mine = 2 * x + y
    copies = []
    for a in range(len(srcs)):
        for j, chip in enumerate(others):
            there = 2 * chip[0] + chip[1]
            src, dst = srcs[a](mine, there, c), dsts[a](mine, there, c)
            if not started:
                dst = dsts[a](there, mine, c)
            copies.append(pltpu.make_async_remote_copy(
                src_ref=src, dst_ref=dst, send_sem=send_sems.at[a * 3 + j], recv_sem=recv_sems.at[a * 3 + j],
                device_id=(*chip, c), device_id_type=MESH))
    return copies


def _split_start(srcs_of, dsts_of, arrays, n_src, name, copies_of=None, n_sems=None, dep=None):
    n = len(arrays)
    n_sems = n_sems or 3 * n_src
    copies_of = copies_of or (lambda ins, ss, rs, started: _ici_copies(srcs_of(ins), dsts_of(ins), ss, rs, started))
    deps = [] if dep is None else [dep]
    nd = len(deps)

    def body(*refs):
        ins = refs[:n]
        send_sems, recv_sems = refs[n + nd], refs[n + nd + 1]
        token = refs[2 * n + nd + 2]
        for cp in copies_of(ins, send_sems, recv_sems, True):
            cp.start()
        token[...] = jnp.zeros_like(token)

    out = pl.pallas_call(
        body, name=name,
        in_specs=[HBM_SPEC] * n + [pl.BlockSpec(memory_space=pl.ANY)] * nd,
        out_specs=(SEM_SPEC, SEM_SPEC, *([HBM_SPEC] * n), pl.BlockSpec(memory_space=pltpu.VMEM)),
        out_shape=(pltpu.SemaphoreType.DMA((n_sems,)), pltpu.SemaphoreType.DMA((n_sems,)),
                   *[pltpu.HBM(a.shape, a.dtype) for a in arrays], TOKEN),
        input_output_aliases={a: 2 + a for a in range(n)},
        compiler_params=pltpu.CompilerParams(has_side_effects=DATAFLOW),
    )(*[_in_hbm(a) for a in arrays], *deps)
    return out[0], out[1], list(out[2:2 + n]), out[2 + n]


def _split_wait(srcs_of, dsts_of, send_sems, recv_sems, arrays, after, name, copies_of=None):
    n = len(arrays)
    copies_of = copies_of or (lambda ins, ss, rs, started: _ici_copies(srcs_of(ins), dsts_of(ins), ss, rs, started))

    def body(*refs):
        ins = refs[:n]
        send_sems, recv_sems = refs[n], refs[n + 1]
        for cp in copies_of(ins, send_sems, recv_sems, False):
            cp.wait_send()
            cp.wait_recv()

    return pl.pallas_call(
        body, name=name,
        in_specs=[HBM_SPEC] * n + [SEM_SPEC, SEM_SPEC] + [pl.BlockSpec(memory_space=pl.ANY)] * len(after),
        out_specs=[HBM_SPEC] * n,
        out_shape=[pltpu.HBM(a.shape, a.dtype) for a in arrays],
        input_output_aliases={a: a for a in range(n)},
        compiler_params=pltpu.CompilerParams(has_side_effects=DATAFLOW),
    )(*arrays, send_sems, recv_sems, *after)


def _gather_views(ins):
    view = [lambda frm, to, c, r=r: r.at[frm, c] for r in ins]
    return view


def _gather_start(bufs, dep, name):
    return _split_start(_gather_views, _gather_views, bufs, len(bufs), name, dep=dep)


def _gather_wait(send_sems, recv_sems, bufs, after, name):
    return _split_wait(_gather_views, _gather_views, send_sems, recv_sems, bufs, after, name)


def _forward_halves(bufs, name):
    n = len(bufs)
    any_spec = pl.BlockSpec(memory_space=pl.ANY)

    def body(*refs):
        outs = refs[n:2 * n]
        send_sems, recv_sems = refs[2 * n:]
        x, y, c, others = _place()
        copies = []
        for a in range(n):
            for j, chip in enumerate(others):
                landed = outs[a].at[2 * chip[0] + chip[1], c]
                copies.append(pltpu.make_async_remote_copy(
                    src_ref=landed, dst_ref=landed, send_sem=send_sems.at[a * 3 + j], recv_sem=recv_sems.at[a * 3 + j],
                    device_id=(x, y, 1 - c), device_id_type=MESH))
        for cp in copies:
            cp.start()
        for a in range(n):
            for j, chip in enumerate(others):
                landed = outs[a].at[2 * chip[0] + chip[1], 1 - c]
                pltpu.make_async_remote_copy(
                    src_ref=landed, dst_ref=landed, send_sem=send_sems.at[a * 3 + j], recv_sem=recv_sems.at[a * 3 + j],
                    device_id=(x, y, c), device_id_type=MESH).wait_recv()
        for cp in copies:
            cp.wait_send()

    out = pl.pallas_call(
        body, name=name,
        in_specs=[any_spec] * n, out_specs=[any_spec] * n,
        out_shape=[jax.ShapeDtypeStruct(b.shape, b.dtype) for b in bufs],
        input_output_aliases={a: a for a in range(n)},
        scratch_shapes=[pltpu.SemaphoreType.DMA((3 * n,)), pltpu.SemaphoreType.DMA((3 * n,))],
    )(*bufs)
    return [_merge2(o, 1) for o in out]


def _swap_halves_bf16(gbs, name):
    n = len(gbs)
    any_spec = pl.BlockSpec(memory_space=pl.ANY)

    def body(*refs):
        ins, outs = refs[:n], refs[n:2 * n]
        send_sems, recv_sems = refs[2 * n:]
        x, y, c, _ = _place()
        copies = []
        for a in range(n):
            copies.append(pltpu.make_async_remote_copy(
                src_ref=ins[a].at[:, 1 - c], dst_ref=outs[a], send_sem=send_sems.at[a], recv_sem=recv_sems.at[a],
                device_id=(x, y, 1 - c), device_id_type=MESH))
        for cp in copies:
            cp.start()
        for cp in copies:
            cp.wait()

    return pl.pallas_call(
        body, name=name,
        in_specs=[any_spec] * n, out_specs=[any_spec] * n,
        out_shape=[jax.ShapeDtypeStruct((g.shape[0], g.shape[1] // 2, g.shape[2]), g.dtype) for g in gbs],
        scratch_shapes=[pltpu.SemaphoreType.DMA((n,)), pltpu.SemaphoreType.DMA((n,))],
    )(*[_split2(g, 1) for g in gbs])


def _scatter_srcs(n):
    return lambda ins: [lambda frm, to, c, r=r: r.at[to] for r in ins[:n]]


def _scatter_dsts(n):
    return lambda ins: [lambda frm, to, c, r=r: r.at[frm] for r in ins[n:]]


def _scatter_start(hbs, name):
    n = len(hbs)
    lands = [lax.empty(h.shape, h.dtype) for h in hbs]
    return _split_start(_scatter_srcs(n), _scatter_dsts(n), list(hbs) + lands, n, name)


def _scatter_wait(send_sems, recv_sems, arrays, after, name):
    n = len(arrays) // 2
    return _split_wait(_scatter_srcs(n), _scatter_dsts(n), send_sems, recv_sems, arrays, after, name)[n:]


def _join_halves(rhs, name):
    n = len(rhs)
    any_spec = pl.BlockSpec(memory_space=pl.ANY)

    def body(*refs):
        outs = refs[n:2 * n]
        send_sems, recv_sems = refs[2 * n:]
        x, y, c, _ = _place()
        copies = []
        for a in range(n):
            copies.append(pltpu.make_async_remote_copy(
                src_ref=outs[a].at[c], dst_ref=outs[a].at[c], send_sem=send_sems.at[a],
                recv_sem=recv_sems.at[a], device_id=(x, y, 1 - c), device_id_type=MESH))
        for cp in copies:
            cp.start()
        for a in range(n):
            landed = outs[a].at[1 - c]
            pltpu.make_async_remote_copy(
                src_ref=landed, dst_ref=landed, send_sem=send_sems.at[a], recv_sem=recv_sems.at[a],
                device_id=(x, y, c), device_id_type=MESH).wait_recv()
        for cp in copies:
            cp.wait_send()

    out = pl.pallas_call(
        body, name=name,
        in_specs=[any_spec] * n, out_specs=[any_spec] * n,
        out_shape=[jax.ShapeDtypeStruct(r.shape, r.dtype) for r in rhs],
        input_output_aliases={a: a for a in range(n)},
        scratch_shapes=[pltpu.SemaphoreType.DMA((n,)), pltpu.SemaphoreType.DMA((n,))],
    )(*rhs)
    return [_merge2(o) for o in out]


FLIPS = [(dx, dy, dc) for dx in (0, 1) for dy in (0, 1) for dc in (0, 1)][1:]


def _peer_copies(ins, send_sems, recv_sems, started):
    x, y, c, _ = _place()
    copies = []
    for k, (dx, dy, dc) in enumerate(FLIPS):
        px, py, pc = jnp.bitwise_xor(x, dx), jnp.bitwise_xor(y, dy), jnp.bitwise_xor(c, dc)
        slot = 4 * x + 2 * y + c if started else 4 * px + 2 * py + pc
        copies.append(pltpu.make_async_remote_copy(
            src_ref=ins[0], dst_ref=ins[1].at[slot], send_sem=send_sems.at[k], recv_sem=recv_sems.at[k],
            device_id=(px, py, pc), device_id_type=MESH))
    return copies


def _small_start(v, name):
    land = lax.empty((8,) + v.shape, v.dtype)
    return _split_start(None, None, [v, land], 0, name, copies_of=_peer_copies, n_sems=len(FLIPS))


def _small_wait(send_sems, recv_sems, arrays, after, name):
    return _split_wait(None, None, send_sems, recv_sems, arrays, after, name, copies_of=_peer_copies)[1]


def _sum_slots(land, v, me):
    rows, cols = v.shape

    def body(me_ref, land_ref, v_ref, o_ref):
        o_ref[...] = jnp.zeros_like(o_ref)
        for d in range(8):
            @pl.when(me_ref[0] == d)
            def _():
                o_ref[...] += v_ref[...]

            @pl.when(me_ref[0] != d)
            def _():
                o_ref[...] += land_ref[d]

    return pl.pallas_call(
        body, name="sum_slots",
        grid_spec=pltpu.PrefetchScalarGridSpec(
            num_scalar_prefetch=1, grid=(1,),
            in_specs=[pl.BlockSpec((8, rows, cols), lambda i, me_ref: (0, 0, 0)),
                      pl.BlockSpec((rows, cols), lambda i, me_ref: (0, 0))],
            out_specs=pl.BlockSpec((rows, cols), lambda i, me_ref: (0, 0))),
        out_shape=jax.ShapeDtypeStruct((rows, cols), F32),
        compiler_params=_cparams(("arbitrary",)),
    )(me, land, v)


def _by_shape(arrays):
    groups = {}
    for k, a in enumerate(arrays):
        groups.setdefault(a.shape, []).append(k)
    return list(groups.values())


def _add_sibling_half(gs, sbs, idx, name):
    n = len(gs)
    ns, r, c = gs[0].shape
    r2 = r // 2

    def body(idx_ref, *refs):
        for a in range(n):
            g_ref, sb_ref, hown_ref, hb_ref = refs[a], refs[n + a], refs[2 * n + a], refs[3 * n + a]
            h = g_ref[...] + sb_ref[...].astype(F32)
            hb_ref[...] = h.astype(BF16)

            @pl.when(pl.program_id(0) == idx_ref[0])
            def _():
                hown_ref[...] = h

    spec = pl.BlockSpec((None, r2, c), lambda s, idx_ref: (s, 0, 0))
    out = pl.pallas_call(
        body, name=name,
        grid_spec=pltpu.PrefetchScalarGridSpec(
            num_scalar_prefetch=1, grid=(ns,),
            in_specs=[pl.BlockSpec((None, r2, c), lambda s, idx_ref: (s, idx_ref[4], 0))] * n + [spec] * n,
            out_specs=[pl.BlockSpec((r2, c), lambda s, idx_ref: (0, 0))] * n + [spec] * n),
        out_shape=[jax.ShapeDtypeStruct((r2, c), F32)] * n + [jax.ShapeDtypeStruct((ns, r2, c), BF16)] * n,
        compiler_params=_cparams(("arbitrary",)),
    )(idx, *gs, *sbs)
    return [(out[a], out[n + a]) for a in range(n)]


def _add_chip_slabs(hs, rbs, idx, name):
    n = len(hs)
    r2, c = hs[0].shape

    def body(idx_ref, *refs):
        for a in range(n):
            h_ref, r0_ref, r1_ref, r2_ref = refs[4 * a:4 * a + 4]
            refs[4 * n + a][...] = ((h_ref[...] + r0_ref[...].astype(F32)) + r1_ref[...].astype(F32)) + r2_ref[...].astype(F32)

    def pick(k):
        return pl.BlockSpec((None, r2, c), lambda i, idx_ref: (idx_ref[k], 0, 0))

    operands = []
    for h, rb in zip(hs, rbs):
        operands += [h, rb, rb, rb]
    out = pl.pallas_call(
        body, name=name,
        grid_spec=pltpu.PrefetchScalarGridSpec(
            num_scalar_prefetch=1, grid=(1,),
            in_specs=[pl.BlockSpec((r2, c), lambda i, idx_ref: (0, 0)), pick(1), pick(2), pick(3)] * n,
            out_specs=[pl.BlockSpec((None, r2, c), lambda i, idx_ref: (idx_ref[4], 0, 0))] * n),
        out_shape=[jax.ShapeDtypeStruct((2, r2, c), F32)] * n,
        compiler_params=_cparams(("arbitrary",)),
    )(idx, *operands)
    return list(out)


ELEMENTWISE_VMEM = 16 * 1024 * 1024


def _adamw(items, name):
    n = len(items)
    r, c = items[0][0].shape
    br = max(b for b in range(8, r + 1, 8) if r % b == 0 and n * 16 * b * c * 4 <= ELEMENTWISE_VMEM) if r % 8 == 0 else r

    def body(*refs):
        for a in range(n):
            g_ref, w_ref, m_ref, v_ref = refs[4 * a:4 * a + 4]
            go_ref, d_ref, nm_ref, nv_ref = refs[4 * n + 4 * a:4 * n + 4 * a + 4]
            gg = g_ref[...]
            go_ref[...] = gg
            nm = B1 * m_ref[...] + (1.0 - B1) * gg
            nv = B2 * v_ref[...] + (1.0 - B2) * jnp.square(gg)
            m_hat = nm / (1.0 - B1 ** STEP)
            v_hat = nv / (1.0 - B2 ** STEP)
            d_ref[...] = -LR * (m_hat / (jnp.sqrt(v_hat) + ADAM_EPS) + WD * w_ref[...])
            nm_ref[...] = nm
            nv_ref[...] = nv

    spec = pl.BlockSpec((br, c), lambda i: (i, 0))
    out = pl.pallas_call(
        body, name=name,
        grid=(r // br,),
        in_specs=[spec] * (4 * n), out_specs=[spec] * (4 * n),
        out_shape=[jax.ShapeDtypeStruct((r, c), F32)] * (4 * n),
        compiler_params=_cparams(("arbitrary",)),
    )(*[a for item in items for a in item])
    return [tuple(out[4 * a:4 * a + 4]) for a in range(n)]


BIG = ("w_in", "w_conv_out", "w_pool", "w_pool_out", "w_o", "w_ffn_gate", "w_ffn_up", "w_ffn_down")
REPL = ("g_mix", "b_gate", "b_dw", "ln_g", "ln_b", "pool_scale", "g_ffn", "g_final")
GROUP_MIX = ("w_conv_out", "w_pool", "w_pool_out", "w_o")
GROUP_FFN = ("w_ffn_gate", "w_ffn_up", "w_ffn_down")
TRANSPOSED = ("w_ffn_gate", "w_ffn_up")
WEIGHT_ORDER = ("meta_tokens", "g_mix", "w_in", "b_gate", "w_dw", "b_dw", "ln_g", "ln_b", "w_conv_out", "w_pool",
                "pool_scale", "w_pool_out", "w_o", "g_ffn", "w_ffn_gate", "w_ffn_up", "w_ffn_down", "g_final")


def _shard2d(name, a):
    a = a[0]
    if name == "w_pool":
        return a.reshape(4 * 64, GD)
    if name in TRANSPOSED:
        return a.T
    return a


def _unshard2d(name, a, shape):
    return a.T.reshape(shape) if name in TRANSPOSED else a.reshape(shape)


def _cols_to_slabs(a):
    m, n = a.shape
    return a.reshape(m, N_SHARD, n // N_SHARD).transpose(1, 0, 2)


def _slabs_to_cols(a):
    ns, m, c = a.shape
    return a.transpose(1, 0, 2).reshape(m, ns * c)


def kernel(x, meta_tokens, g_mix, w_in, b_gate, w_dw, b_dw, ln_g, ln_b, w_conv_out, w_pool, pool_scale, w_pool_out, w_o, g_ffn, w_ffn_gate, w_ffn_up, w_ffn_down, g_final, loss_target, m_meta_tokens, m_g_mix, m_w_in, m_b_gate, m_w_dw, m_b_dw, m_ln_g, m_ln_b, m_w_conv_out, m_w_pool, m_pool_scale, m_w_pool_out, m_w_o, m_g_ffn, m_w_ffn_gate, m_w_ffn_up, m_w_ffn_down, m_g_final, v_meta_tokens, v_g_mix, v_w_in, v_b_gate, v_w_dw, v_b_dw, v_ln_g, v_ln_b, v_w_conv_out, v_w_pool, v_pool_scale, v_w_pool_out, v_w_o, v_g_ffn, v_w_ffn_gate, v_w_ffn_up, v_w_ffn_down, v_g_final):
    args = dict(locals())
    w = {n: args[n] for n in WEIGHT_ORDER}
    mom = {n: args["m_" + n] for n in WEIGHT_ORDER}
    var = {n: args["v_" + n] for n in WEIGHT_ORDER}
    seq = x.shape[1]
    nb = seq // BR + 1
    tp = nb * BR
    tk = tp // 2 if (tp // 2) % 16 == 0 else BR
    t_total = seq + N_META
    cx, cy, cc = lax.axis_index("x"), lax.axis_index("y"), lax.axis_index("c")
    chip = 2 * cx + cy
    chip1 = jnp.reshape(chip, (1,)).astype(jnp.int32)
    core = jnp.reshape(cc, (1,)).astype(jnp.int32)
    others = jnp.sort(jnp.stack([2 * (1 - cx) + cy, 2 * cx + (1 - cy), 2 * (1 - cx) + (1 - cy)]))
    idx = jnp.concatenate([chip1, others.astype(jnp.int32), core])
    xs, target = x[0], loss_target[0]

    tiny = _gather_tiny(jnp.concatenate([w["meta_tokens"], w["w_dw"][0], jnp.zeros((1, GD), F32)], axis=0))
    small = {n: w[n] for n in REPL if n != "g_final"}
    small["g_final"] = w["g_final"].reshape(1, D)
    small["w_dw"] = _slabs_to_cols(tiny[:, N_META:])
    head = jnp.concatenate([jnp.zeros((PAD, D), F32), _slabs_to_cols(tiny[:, :N_META])], axis=0)

    def cast(group, dep):
        return [_cast_into_slot(_shard2d(n, w[n]), chip1, dep, "cast_" + n) for n in group]

    def gather_finish(group, start, after, name):
        landed = _gather_wait(start[0], start[1], start[2], after, "gather_wait_" + name)
        return dict(zip(group, _forward_halves(landed, "forward_" + name)))

    st_in = _gather_start(cast(("w_in",), tiny), None, "gather_start_in")
    bufs_mix, bufs_ffn = cast(GROUP_MIX, st_in[3]), cast(GROUP_FFN, st_in[3])
    u = _rms_u(head, xs, small["g_mix"], nb)
    z_own = _in_proj_own(u, w["w_in"][0], idx, nb)
    gw = gather_finish(("w_in",), st_in, [z_own] + bufs_mix + bufs_ffn, "in")
    st_mix = _gather_start(bufs_mix, gw["w_in"], "gather_start_mix")
    z = _in_proj_rest(u, gw["w_in"], z_own, idx, st_mix[3], nb)
    gw.update(gather_finish(GROUP_MIX, st_mix, [z], "mix"))
    st_ffn = _gather_start(bufs_ffn, gw["w_o"], "gather_start_ffn")
    w_pool_b = gw["w_pool"].reshape(N_SHARD, 4, 64, GD).transpose(1, 0, 2, 3).reshape(4, GD, GD)
    w_co_b, w_po_b, w_o_b = (gw[n].reshape(D, D) for n in ("w_conv_out", "w_pool_out", "w_o"))
    h1, yc, yp, mg, ca, cpre, m, mw, m2b = _mixers_fwd(
        z, head, xs, small["b_gate"] + st_ffn[3][0, 0], small["w_dw"], small["b_dw"], small["ln_g"], small["ln_b"],
        small["pool_scale"], w_co_b, w_pool_b, w_po_b, w_o_b, nb, t_total)
    gw.update(gather_finish(GROUP_FFN, st_ffn, [h1], "ffn"))

    dh1, dh1b, vb, fb, dgb, dub, dh2b, loss, dg_ffn, dg_final = _ffn_fwd_bwd(
        h1, target, small["g_ffn"], small["g_final"], gw["w_ffn_gate"].reshape(D_FF, D),
        gw["w_ffn_up"].reshape(D_FF, D), gw["w_ffn_down"].reshape(D_FF, D), nb)

    def slabs(name, g):
        if name == "w_in":
            return g
        if name == "w_pool":
            return g.reshape(4, N_SHARD, 64, GD).transpose(1, 0, 2, 3).reshape(N_SHARD, 4 * 64, GD)
        return g.reshape(N_SHARD, g.shape[0] // N_SHARD, g.shape[1])

    def reduce_start(group, grads, name):
        g32 = [slabs(n, grads[n][0]) for n in group]
        g16 = [slabs(n, grads[n][1]) for n in group]
        from_sibling = _swap_halves_bf16(g16, "swap_halves_" + name)
        halves = [None] * len(group)
        for ks in _by_shape(g32):
            done = _add_sibling_half([g32[k] for k in ks], [from_sibling[k] for k in ks], idx, "add_sibling_" + group[ks[0]])
            for k, pair in zip(ks, done):
                halves[k] = pair
        return [h for h, _ in halves], _scatter_start([hb for _, hb in halves], "scatter_start_" + name)

    def reduce_finish(group, halves, start, after, name):
        from_chips = _scatter_wait(start[0], start[1], start[2], after, "scatter_wait_" + name)
        reduced = [None] * len(group)
        for ks in _by_shape(halves):
            done = _add_chip_slabs([halves[k] for k in ks], [from_chips[k] for k in ks], idx, "add_chips_" + group[ks[0]])
            for k, r in zip(ks, done):
                reduced[k] = r
        return reduced

    half_ff = D_FF // 2
    grads_ffn = {
        "w_ffn_gate": _wgrad(dgb, vb, half_ff, D, tk, "wgrad_ffn_gate"),
        "w_ffn_up": _wgrad(dub, vb, half_ff, D, tk, "wgrad_ffn_up"),
        "w_ffn_down": _wgrad(fb, dh2b, half_ff, D, tk, "wgrad_ffn_down"),
    }
    halves_ffn, sc_ffn = reduce_start(GROUP_FFN, grads_ffn, "ffn")

    dycb, dypb, dzg, dconv, dmwb, dm, db_gate, dln_g, dln_b, db_dw, dps = _mixers_bwd_rows(
        dh1b, yc, yp, z, small["b_gate"], cpre, small["ln_g"], small["ln_b"], mw, small["pool_scale"],
        w_o_b, w_co_b, w_po_b, w_pool_b, sc_ffn[3], nb)
    grads_mix = {
        "w_conv_out": _wgrad(ca, dycb, D, D, tk, "wgrad_conv_out"),
        "w_pool": _wgrad(m, dmwb, GD, GD, tk, "wgrad_pool", diag=True),
        "w_pool_out": _wgrad(m2b, dypb, D, D, tk, "wgrad_pool_out"),
        "w_o": _wgrad(mg, dh1b, D, D, tk, "wgrad_o"),
    }
    halves_mix, sc_mix = reduce_start(GROUP_MIX, grads_mix, "mix")
    dzb, grad_x, dhead, dw_dw, dg_mix = _mixers_bwd_halo(
        dconv, dm, z, dzg, small["w_dw"], head, xs, small["g_mix"], dh1, gw["w_in"], sc_mix[3], nb, t_total)
    packed = jnp.concatenate(
        [dg_mix, db_gate.reshape(2, D), db_dw, dln_g, dln_b, dps, dg_ffn, dg_final,
         jnp.broadcast_to(loss, (1, D)), jnp.zeros((6, D), F32), dhead[PAD:], dw_dw], axis=0)
    sm = _small_start(packed, "small_start")
    grads_in = {"w_in": _wgrad(u, dzb, D, D_IN // N_SHARD, tk, "wgrad_in", col_major=True, dep=sm[3])}
    halves_in, sc_in = reduce_start(("w_in",), grads_in, "in")

    land = _small_wait(sm[0], sm[1], sm[2], [sc_in[3]], "small_wait")
    summed = _sum_slots(land, packed, jnp.reshape(4 * cx + 2 * cy + cc, (1,)).astype(jnp.int32))
    loss = summed[9, 0]

    first = GROUP_FFN + GROUP_MIX
    reduced_half = reduce_finish(GROUP_FFN, halves_ffn, sc_ffn, [summed], "ffn")
    reduced_half += reduce_finish(GROUP_MIX, halves_mix, sc_mix, [summed], "mix")
    reduced = dict(zip(first, _join_halves(reduced_half, "join_halves_first")))
    updates = {}
    for ks in _by_shape([reduced[n] for n in first]):
        names = [first[k] for k in ks]
        done = _adamw([(reduced[n], _shard2d(n, w[n]), _shard2d(n, mom[n]), _shard2d(n, var[n])) for n in names],
                      "adamw_" + names[0])
        updates.update(zip(names, done))

    def repl_stack(d):
        return jnp.concatenate([d["g_mix"], d["b_gate"].reshape(2, D), d["b_dw"], d["ln_g"], d["ln_b"],
                                d["pool_scale"], d["g_ffn"], d["g_final"].reshape(1, D), jnp.ones((7, D), F32)], axis=0)

    def shard_stack(d):
        return jnp.concatenate([d["meta_tokens"], d["w_dw"][0], jnp.ones((1, GD), F32)], axis=0)

    g_repl = summed[0:16]
    g_shard = lax.dynamic_slice_in_dim(summed[16:64], chip * GD, GD, axis=1)
    g_repl, d_repl, m_repl, v_repl = _adamw([(g_repl, repl_stack(w), repl_stack(mom), repl_stack(var))], "adamw_repl")[0]
    g_shard, d_shard, m_shard, v_shard = _adamw(
        [(g_shard, shard_stack(w), shard_stack(mom), shard_stack(var))], "adamw_cols")[0]

    done_first = [updates[n][1] for n in first] + [d_repl, d_shard]
    last_half = reduce_finish(("w_in",), halves_in, sc_in, done_first, "in")
    reduced["w_in"] = _join_halves(last_half, "join_halves_in")[0]
    updates["w_in"] = _adamw([(reduced["w_in"], w["w_in"][0], mom["w_in"][0], var["w_in"][0])], "adamw_w_in")[0]

    def unpack(name, repl, shard):
        if name == "meta_tokens":
            return shard[0:N_META]
        if name == "w_dw":
            return shard[N_META:N_META + KW].reshape(1, KW, GD)
        row = {"g_mix": 0, "b_gate": 1, "b_dw": 3, "ln_g": 4, "ln_b": 5, "pool_scale": 6, "g_ffn": 7, "g_final": 8}[name]
        if name == "b_gate":
            return repl[1:3].reshape(1, 2 * D)
        if name == "g_final":
            return repl[8]
        return repl[row:row + 1]

    out_g, out_d, out_m, out_v = {}, {}, {}, {}
    for n in WEIGHT_ORDER:
        if n in BIG:
            g, d_, m_, v_ = updates[n]
            shape = w[n].shape
            out_g[n], out_d[n], out_m[n], out_v[n] = (_unshard2d(n, a, shape) for a in (g, d_, m_, v_))
        else:
            out_g[n] = unpack(n, g_repl, g_shard)
            out_d[n] = unpack(n, d_repl, d_shard)
            out_m[n] = unpack(n, m_repl, m_shard)
            out_v[n] = unpack(n, v_repl, v_shard)
    return (loss, grad_x[None], *[out_g[n] for n in WEIGHT_ORDER], *[out_d[n] for n in WEIGHT_ORDER],
            *[out_m[n] for n in WEIGHT_ORDER], *[out_v[n] for n in WEIGHT_ORDER])
```

```python
import jax
import jax.numpy as jnp
from jax import lax
from jax.experimental import pallas as pl
from jax.experimental.pallas import tpu as pltpu

F32 = jnp.float32
BF16 = jnp.bfloat16
MESH = pl.DeviceIdType.MESH

D = 1024
N_META = 16
KW = 31
POOL_WINDOWS = (2, 4, 8, 16)
GD = 256
D_IN = 5 * D
D_FF = 2816
N_SHARD = 4
BR = 256
HALO = 16
PAD = BR - N_META
EXT = BR + 2 * HALO
RMS_EPS = 1e-6
LN_EPS = 1e-5
LR, B1, B2, ADAM_EPS, WD, STEP = 0.001, 0.9, 0.999, 1e-08, 0.01, 10
VMEM_LIMIT = 56 * 1024 * 1024


def _cparams(sem, vmem=VMEM_LIMIT):
    return pltpu.CompilerParams(dimension_semantics=sem, vmem_limit_bytes=vmem)


def _dot(a, b):
    return jnp.dot(a, b, preferred_element_type=F32)


def _dot_nt(a, b):
    return lax.dot_general(a, b, (((1,), (1,)), ((), ())), preferred_element_type=F32)


def _dot_tn(a, b):
    return lax.dot_general(a, b, (((0,), (0,)), ((), ())), preferred_element_type=F32)


def _sigmoid(x):
    return 0.5 * jnp.tanh(0.5 * x) + 0.5


def _row_ids(i, n, offset=0):
    return lax.broadcasted_iota(jnp.int32, (n, 1), 0) + (i * BR + offset - PAD)


def _pool_cnt(t, w, t_total):
    left = w // 2
    right = w - 1 - left
    lo = jnp.clip(t - left, 0, t_total)
    hi = jnp.clip(t + right + 1, 0, t_total)
    return jnp.maximum(hi - lo, 1).astype(F32)


def _halo_specs(nb, halo_width=D, width=D):
    last = nb * (BR // HALO) - 1
    return [
        pl.BlockSpec((HALO, halo_width), lambda i: (jnp.maximum(i * (BR // HALO) - 1, 0), 0)),
        pl.BlockSpec((BR, width), lambda i: (i, 0)),
        pl.BlockSpec((HALO, halo_width), lambda i: (jnp.minimum((i + 1) * (BR // HALO), last), 0)),
    ]


def _cols(ref, n):
    return [ref.at[:, k * D:(k + 1) * D] for k in range(n)]


def _fill_ext(ext_ref, prev, cur, nxt, i, nb):
    ext_ref[0:HALO, :] = jnp.where(i > 0, prev, 0.0)
    ext_ref[HALO:HALO + BR, :] = cur
    ext_ref[HALO + BR:EXT, :] = jnp.where(i < nb - 1, nxt, 0.0)


ROT_ROWS = EXT - 8


def _fill_rot(rot_ref, ext_ref, lanes):
    for r in range(1, 8):
        rot_ref[r] = ext_ref[pl.ds(r, ROT_ROWS), lanes]


def _tap(rot_ref, ext_ref, lanes, offset):
    q, r = divmod(offset, 8)
    if r == 0:
        return ext_ref[pl.ds(8 * q, BR), lanes]
    return rot_ref[r, pl.ds(8 * q, BR), :]


def _row_spec(width=D):
    return pl.BlockSpec((BR, width), lambda i: (i, 0))


def _x_spec():
    return pl.BlockSpec((BR, D), lambda i: (jnp.maximum(i - 1, 0), 0))


def _const_spec(shape):
    nd = len(shape)
    return pl.BlockSpec(shape, lambda i: (0,) * nd)


def _rms_u(head, x, g_mix, nb):
    def body(head_ref, x_ref, g_ref, u_ref):
        i = pl.program_id(0)
        h = jnp.where(i == 0, head_ref[...], x_ref[...])
        r = lax.rsqrt(jnp.mean(h * h, axis=-1, keepdims=True) + RMS_EPS)
        u_ref[...] = ((h * r) * g_ref[...]).astype(BF16)

    return pl.pallas_call(
        body, name="rms_u",
        grid=(nb,),
        in_specs=[_const_spec((BR, D)), _x_spec(), _const_spec((1, D))],
        out_specs=_row_spec(),
        out_shape=jax.ShapeDtypeStruct((nb * BR, D), BF16),
        compiler_params=_cparams(("arbitrary",)),
    )(head, x, g_mix)


def _in_proj_rows(tp):
    return tp // 4 if (tp // 4) % 16 == 0 else BR


def _in_proj_own(u, w_own, idx, nb):
    tp = nb * BR
    wcols = w_own.shape[1]
    rows = _in_proj_rows(tp)

    def body(idx_ref, u_ref, w_ref, z_ref, wb_ref):
        @pl.when(pl.program_id(0) == 0)
        def _():
            wb_ref[...] = w_ref[...].astype(BF16)

        z_ref[...] = _dot(u_ref[...], wb_ref[...])

    return pl.pallas_call(
        body, name="in_proj_own",
        grid_spec=pltpu.PrefetchScalarGridSpec(
            num_scalar_prefetch=1, grid=(tp // rows,),
            in_specs=[pl.BlockSpec((rows, D), lambda i, idx_ref: (i, 0)),
                      pl.BlockSpec((D, wcols), lambda i, idx_ref: (0, 0))],
            out_specs=pl.BlockSpec((rows, wcols), lambda i, idx_ref: (i, idx_ref[0])),
            scratch_shapes=[pltpu.VMEM((D, wcols), BF16)]),
        out_shape=jax.ShapeDtypeStruct((tp, N_SHARD * wcols), F32),
        compiler_params=_cparams(("arbitrary",)),
    )(idx, u, w_own)


def _in_proj_rest(u, w_in_b, z, idx, dep, nb):
    tp = nb * BR
    wcols = w_in_b.shape[2]
    rows = _in_proj_rows(tp)

    def body(idx_ref, u_ref, w_ref, z_in, dep_ref, z_ref):
        z_ref[...] = _dot(u_ref[...], w_ref[...])

    any_spec = pl.BlockSpec(memory_space=pl.ANY)
    return pl.pallas_call(
        body, name="in_proj_rest",
        grid_spec=pltpu.PrefetchScalarGridSpec(
            num_scalar_prefetch=1, grid=(N_SHARD - 1, tp // rows),
            in_specs=[pl.BlockSpec((rows, D), lambda s, i, idx_ref: (i, 0)),
                      pl.BlockSpec((None, D, wcols), lambda s, i, idx_ref: (idx_ref[1 + s], 0, 0)),
                      any_spec, any_spec],
            out_specs=pl.BlockSpec((rows, wcols), lambda s, i, idx_ref: (i, idx_ref[1 + s]))),
        out_shape=jax.ShapeDtypeStruct(z.shape, F32),
        input_output_aliases={3: 0},
        compiler_params=_cparams(("arbitrary", "arbitrary")),
    )(idx, u, w_in_b, z, dep)


def _mixers_fwd(z, head, x, b_gate, w_dw, b_dw, ln_g, ln_b, pool_scale, w_co, w_pool, w_po, w_o, nb, t_total):
    tp = nb * BR

    def body(z_prev, z_cur, z_next, head_ref, x_ref, bg_ref, wdw_ref, bdw_ref,
             lng_ref, lnb_ref, ps_ref, wco_ref, wpool_ref, wpo_ref, wo_ref,
             h1_ref, yc_ref, yp_ref, mg_ref, ca_ref, cpre_ref, m_ref, mw_ref, m2b_ref, ext_ref, pext_ref, rot_ref):
        i = pl.program_id(0)
        avp, agp, pp = _cols(z_prev, 3)
        av, ag, pc, za, zb = _cols(z_cur, 5)
        avn, agn, pn = _cols(z_next, 3)
        _fill_ext(ext_ref, avp[...] * _sigmoid(agp[...]), av[...] * _sigmoid(ag[...]),
                  avn[...] * _sigmoid(agn[...]), i, nb)
        _fill_ext(pext_ref, pp[...], pc[...], pn[...], i, nb)

        def conv_chunk(c, carry):
            lanes = pl.ds(pl.multiple_of(c * 128, 128), 128)
            _fill_rot(rot_ref, ext_ref, lanes)
            acc = jnp.broadcast_to(bdw_ref[:, lanes], (BR, 128))
            for k in range(KW):
                acc = acc + wdw_ref[k:k + 1, lanes] * _tap(rot_ref, ext_ref, lanes, 1 + k)
            cpre_ref[:, lanes] = acc
            return carry
        lax.fori_loop(0, D // 128, conv_chunk, 0)

        conv = cpre_ref[...]
        mu = jnp.mean(conv, axis=-1, keepdims=True)
        xc = conv - mu
        rstd = lax.rsqrt(jnp.mean(xc * xc, axis=-1, keepdims=True) + LN_EPS)
        ln = (xc * rstd) * lng_ref[...] + lnb_ref[...]
        cact = (ln * _sigmoid(ln)).astype(BF16)
        ca_ref[...] = cact
        y_conv = _dot(cact, wco_ref[...])
        yc_ref[...] = y_conv

        t = _row_ids(i, BR)
        for gi, w in enumerate(POOL_WINDOWS):
            left = w // 2
            right = w - 1 - left
            lanes = slice(gi * GD, (gi + 1) * GD)
            s = pext_ref[pl.ds(HALO - left, BR), lanes]
            for j in range(-left + 1, right + 1):
                s = s + pext_ref[pl.ds(HALO + j, BR), lanes]
            m = (s / _pool_cnt(t, w, t_total) - pext_ref[HALO:HALO + BR, lanes]).astype(BF16)
            m_ref[:, lanes] = m
            mw_ref[:, lanes] = _dot(m, wpool_ref[gi])
        mw = mw_ref[...]
        m2b = (mw * ps_ref[...]).astype(BF16)
        m2b_ref[...] = m2b
        y_pool = _dot(m2b, wpo_ref[...])
        yp_ref[...] = y_pool

        s_a = _sigmoid(za[...] + bg_ref[:, 0:D])
        s_b = _sigmoid(zb[...] + bg_ref[:, D:2 * D])
        merged = (s_a * y_conv + s_b * y_pool).astype(BF16)
        mg_ref[...] = merged
        h0 = jnp.where(i == 0, head_ref[...], x_ref[...])
        h1_ref[...] = h0 + _dot(merged, wo_ref[...])

    in_specs = (_halo_specs(nb, 3 * D, 5 * D)
                + [_const_spec((BR, D)), _x_spec(), _const_spec((1, 2 * D)), _const_spec((32, D)),
                   _const_spec((1, D)), _const_spec((1, D)), _const_spec((1, D)), _const_spec((1, D)),
                   _const_spec((D, D)), _const_spec((4, GD, GD)), _const_spec((D, D)), _const_spec((D, D))])
    outs = [(F32, "h1"), (F32, "yc"), (F32, "yp"), (BF16, "mg"), (BF16, "ca"), (F32, "cpre"), (BF16, "m"), (F32, "mw"),
            (BF16, "m2b")]
    return pl.pallas_call(
        body, name="mixers_fwd",
        grid=(nb,),
        in_specs=in_specs,
        out_specs=[_row_spec() for _ in outs],
        out_shape=[jax.ShapeDtypeStruct((tp, D), dt) for dt, _ in outs],
        scratch_shapes=[pltpu.VMEM((EXT, D), F32), pltpu.VMEM((EXT, D), F32), pltpu.VMEM((8, ROT_ROWS, 128), F32)],
        compiler_params=_cparams(("arbitrary",)),
    )(z, z, z, head, x, b_gate, w_dw, b_dw, ln_g, ln_b, pool_scale, w_co, w_pool, w_po, w_o)


def _ffn_fwd_bwd(h1, target, g_ffn, g_final, w_g, w_u, w_d, nb):
    tp = nb * BR

    def body(h1_ref, tgt_ref, gf_ref, gfin_ref, wg_hbm, wu_hbm, wd_hbm,
             dh1_ref, dh1b_ref, vb_ref, fb_ref, dgb_ref, dub_ref, dh2b_ref, loss_ref, dgf_ref, dgfin_ref,
             wg_ref, wu_ref, wd_ref, sem):
        i = pl.program_id(0)

        @pl.when(i == 0)
        def _():
            copies = [pltpu.make_async_copy(wg_hbm, wg_ref, sem.at[0]),
                      pltpu.make_async_copy(wu_hbm, wu_ref, sem.at[1]),
                      pltpu.make_async_copy(wd_hbm, wd_ref, sem.at[2])]
            for cp in copies:
                cp.start()
            loss_ref[...] = jnp.zeros_like(loss_ref)
            dgf_ref[...] = jnp.zeros_like(dgf_ref)
            dgfin_ref[...] = jnp.zeros_like(dgfin_ref)
            for cp in copies:
                cp.wait()

        h1 = h1_ref[...]
        r1 = lax.rsqrt(jnp.mean(h1 * h1, axis=-1, keepdims=True) + RMS_EPS)
        vn = h1 * r1
        vb = (vn * gf_ref[...]).astype(BF16)
        vb_ref[...] = vb
        g = _dot_nt(vb, wg_ref[...])
        up = _dot_nt(vb, wu_ref[...])
        sg = _sigmoid(g)
        sl = g * sg
        fb = (sl * up).astype(BF16)
        fb_ref[...] = fb
        h2 = h1 + _dot(fb, wd_ref[...])
        r2 = lax.rsqrt(jnp.mean(h2 * h2, axis=-1, keepdims=True) + RMS_EPS)
        yn = h2 * r2
        valid = i > 0
        diff = jnp.where(valid, yn * gfin_ref[...] - tgt_ref[...], 0.0)
        loss_ref[...] += 0.5 * jnp.sum(jnp.mean(diff * diff, axis=-1, keepdims=True))
        dy = diff * (1.0 / D)
        dgfin_ref[...] += jnp.sum(dy * yn, axis=0, keepdims=True)
        gd = dy * gfin_ref[...]
        dh2 = r2 * (gd - yn * jnp.mean(yn * gd, axis=-1, keepdims=True))
        dh2b = dh2.astype(BF16)
        dh2b_ref[...] = dh2b
        df = _dot_nt(dh2b, wd_ref[...])
        dub = (df * sl).astype(BF16)
        dgb = (df * up * (sg * (1.0 + g * (1.0 - sg)))).astype(BF16)
        dub_ref[...] = dub
        dgb_ref[...] = dgb
        dv = _dot(dgb, wg_ref[...]) + _dot(dub, wu_ref[...])
        dgf_ref[...] += jnp.sum(dv * vn, axis=0, keepdims=True)
        gd1 = dv * gf_ref[...]
        dh1 = dh2 + r1 * (gd1 - vn * jnp.mean(vn * gd1, axis=-1, keepdims=True))
        dh1_ref[...] = dh1
        dh1b_ref[...] = dh1.astype(BF16)

    any_spec = pl.BlockSpec(memory_space=pl.ANY)
    return pl.pallas_call(
        body, name="ffn_fwd_bwd",
        grid=(nb,),
        in_specs=[_row_spec(), _x_spec(), _const_spec((1, D)), _const_spec((1, D)), any_spec, any_spec, any_spec],
        out_specs=[_row_spec(), _row_spec(), _row_spec(), _row_spec(D_FF), _row_spec(D_FF), _row_spec(D_FF), _row_spec(),
                   _const_spec((1, 1)), _const_spec((1, D)), _const_spec((1, D))],
        out_shape=[jax.ShapeDtypeStruct((tp, D), F32), jax.ShapeDtypeStruct((tp, D), BF16),
                   jax.ShapeDtypeStruct((tp, D), BF16), jax.ShapeDtypeStruct((tp, D_FF), BF16),
                   jax.ShapeDtypeStruct((tp, D_FF), BF16), jax.ShapeDtypeStruct((tp, D_FF), BF16),
                   jax.ShapeDtypeStruct((tp, D), BF16), jax.ShapeDtypeStruct((1, 1), F32),
                   jax.ShapeDtypeStruct((1, D), F32), jax.ShapeDtypeStruct((1, D), F32)],
        scratch_shapes=[pltpu.VMEM((D_FF, D), BF16), pltpu.VMEM((D_FF, D), BF16), pltpu.VMEM((D_FF, D), BF16),
                        pltpu.SemaphoreType.DMA((3,))],
        compiler_params=_cparams(("arbitrary",)),
    )(h1, target, g_ffn, g_final, w_g, w_u, w_d)


def _mixers_bwd_rows(dh1b, yc, yp, z, b_gate, cpre, ln_g, ln_b, mw, pool_scale, w_o, w_co, w_po, w_pool, dep, nb):
    tp = nb * BR

    def body(dh1b_ref, yc_ref, yp_ref, za, zb, bg_ref, cpre_ref, lng_ref, lnb_ref, mw_ref, ps_ref,
             wo_ref, wco_ref, wpo_ref, wpool_ref, dep_ref,
             dycb_ref, dypb_ref, dzg_ref, dconv_ref, dmwb_ref, dm_ref, dbg_ref, dlng_ref, dlnb_ref, dbdw_ref, dps_ref):
        i = pl.program_id(0)

        @pl.when(i == 0)
        def _():
            for r in (dbg_ref, dlng_ref, dlnb_ref, dbdw_ref, dps_ref):
                r[...] = jnp.zeros_like(r)

        dmg = _dot_nt(dh1b_ref[...], wo_ref[...])
        s_a = _sigmoid(za[...] + bg_ref[:, 0:D])
        s_b = _sigmoid(zb[...] + bg_ref[:, D:2 * D])
        dycb = (dmg * s_a).astype(BF16)
        dypb = (dmg * s_b).astype(BF16)
        dycb_ref[...] = dycb
        dypb_ref[...] = dypb
        dza = dmg * yc_ref[...] * (s_a * (1.0 - s_a))
        dzb = dmg * yp_ref[...] * (s_b * (1.0 - s_b))
        dzg_ref[:, 0:D] = dza.astype(BF16)
        dzg_ref[:, D:2 * D] = dzb.astype(BF16)
        dbg_ref[:, 0:D] += jnp.sum(dza, axis=0, keepdims=True)
        dbg_ref[:, D:2 * D] += jnp.sum(dzb, axis=0, keepdims=True)

        dca = _dot_nt(dycb, wco_ref[...])
        conv = cpre_ref[...]
        mu = jnp.mean(conv, axis=-1, keepdims=True)
        xc = conv - mu
        rstd = lax.rsqrt(jnp.mean(xc * xc, axis=-1, keepdims=True) + LN_EPS)
        xhat = xc * rstd
        ln = xhat * lng_ref[...] + lnb_ref[...]
        sg = _sigmoid(ln)
        dln = dca * (sg * (1.0 + ln * (1.0 - sg)))
        dlng_ref[...] += jnp.sum(dln * xhat, axis=0, keepdims=True)
        dlnb_ref[...] += jnp.sum(dln, axis=0, keepdims=True)
        dxh = dln * lng_ref[...]
        dconv = rstd * (dxh - jnp.mean(dxh, axis=-1, keepdims=True)
                        - xhat * jnp.mean(dxh * xhat, axis=-1, keepdims=True))
        dconv_ref[...] = dconv
        dbdw_ref[...] += jnp.sum(dconv, axis=0, keepdims=True)

        dm2 = _dot_nt(dypb, wpo_ref[...])
        dps_ref[...] += jnp.sum(dm2 * mw_ref[...], axis=0, keepdims=True)
        dmwb = (dm2 * ps_ref[...]).astype(BF16)
        dmwb_ref[...] = dmwb
        for gi in range(len(POOL_WINDOWS)):
            lanes = slice(gi * GD, (gi + 1) * GD)
            dm_ref[:, lanes] = _dot_nt(dmwb[:, lanes], wpool_ref[gi])

    in_specs = [_row_spec(), _row_spec(), _row_spec(),
                pl.BlockSpec((BR, D), lambda i: (i, 3)), pl.BlockSpec((BR, D), lambda i: (i, 4)),
                _const_spec((1, 2 * D)), _row_spec(), _const_spec((1, D)), _const_spec((1, D)), _row_spec(),
                _const_spec((1, D)), _const_spec((D, D)), _const_spec((D, D)), _const_spec((D, D)),
                _const_spec((4, GD, GD)), pl.BlockSpec(memory_space=pl.ANY)]
    return pl.pallas_call(
        body, name="mixers_bwd_rows",
        grid=(nb,),
        in_specs=in_specs,
        out_specs=[_row_spec(), _row_spec(), _row_spec(2 * D), _row_spec(), _row_spec(), _row_spec(),
                   _const_spec((1, 2 * D)), _const_spec((1, D)), _const_spec((1, D)), _const_spec((1, D)),
                   _const_spec((1, D))],
        out_shape=[jax.ShapeDtypeStruct((tp, D), BF16), jax.ShapeDtypeStruct((tp, D), BF16),
                   jax.ShapeDtypeStruct((tp, 2 * D), BF16), jax.ShapeDtypeStruct((tp, D), F32),
                   jax.ShapeDtypeStruct((tp, D), BF16), jax.ShapeDtypeStruct((tp, D), F32),
                   jax.ShapeDtypeStruct((1, 2 * D), F32), jax.ShapeDtypeStruct((1, D), F32),
                   jax.ShapeDtypeStruct((1, D), F32), jax.ShapeDtypeStruct((1, D), F32),
                   jax.ShapeDtypeStruct((1, D), F32)],
        compiler_params=_cparams(("arbitrary",)),
    )(dh1b, yc, yp, z, z, b_gate, cpre, ln_g, ln_b, mw, pool_scale, w_o, w_co, w_po, w_pool, dep)


def _mixers_bwd_halo(dconv, dm, z, dzg, w_dw, head, x, g_mix, dh1, w_in_b, dep, nb, t_total):
    tp = nb * BR
    ns = w_in_b.shape[0]
    wcols = w_in_b.shape[2]
    seq = x.shape[0]

    def body(dcp, dcc, dcn, dmp, dmc, dmn, z_prev, z_cur, z_next, dzg_ref, wdw_ref, head_ref, x_ref, g_ref,
             dh1_ref, w_hbm, dep_ref,
             dzb_ref, gx_ref, dhead_ref, dwdw_ref, dgmix_ref,
             w_ref, sem, aext_ref, dext_ref, qext_ref, da_ref, rot_ref, dwp_ref):
        i = pl.program_id(0)
        (avp, agp), (av, ag), (avn, agn) = _cols(z_prev, 2), _cols(z_cur, 2), _cols(z_next, 2)

        @pl.when(i == 0)
        def _():
            cp = pltpu.make_async_copy(w_hbm, w_ref, sem.at[0])
            cp.start()
            dwp_ref[...] = jnp.zeros_like(dwp_ref)
            dgmix_ref[...] = jnp.zeros_like(dgmix_ref)
            cp.wait()

        sig_g = _sigmoid(ag[...])
        _fill_ext(aext_ref, avp[...] * _sigmoid(agp[...]), av[...] * sig_g, avn[...] * _sigmoid(agn[...]), i, nb)
        _fill_ext(dext_ref, dcp[...], dcc[...], dcn[...], i, nb)
        _fill_ext(qext_ref, dmp[...], dmc[...], dmn[...], i, nb)

        def conv_chunk(c, carry):
            lanes = pl.ds(pl.multiple_of(c * 128, 128), 128)
            _fill_rot(rot_ref, dext_ref, lanes)
            acc = jnp.zeros((BR, 128), F32)
            for k in range(KW):
                acc = acc + wdw_ref[k:k + 1, lanes] * _tap(rot_ref, dext_ref, lanes, KW - k)
            da_ref[:, lanes] = acc
            _fill_rot(rot_ref, aext_ref, lanes)
            dcv = dext_ref[HALO:HALO + BR, lanes]
            for k in range(KW):
                prod = _tap(rot_ref, aext_ref, lanes, 1 + k) * dcv
                dwp_ref[k, :, lanes] += jnp.sum(prod.reshape(BR // 8, 8, 128), axis=0)
            return carry
        lax.fori_loop(0, D // 128, conv_chunk, 0)

        @pl.when(i == nb - 1)
        def _():
            dwdw_ref[...] = jnp.sum(dwp_ref[...], axis=1)

        da = da_ref[...]
        a_val = av[...]
        dzb_ref[:, 0:D] = (da * sig_g).astype(BF16)
        dzb_ref[:, D:2 * D] = (da * a_val * (sig_g * (1.0 - sig_g))).astype(BF16)

        t_ext = _row_ids(i, EXT, -HALO)
        for gi, w in enumerate(POOL_WINDOWS):
            left = w // 2
            right = w - 1 - left
            lanes = slice(gi * GD, (gi + 1) * GD)
            qext_ref[:, lanes] = qext_ref[:, lanes] / _pool_cnt(t_ext, w, t_total)
            s = qext_ref[pl.ds(HALO - right, BR), lanes]
            for j in range(-right + 1, left + 1):
                s = s + qext_ref[pl.ds(HALO + j, BR), lanes]
            dzb_ref[:, 2 * D + gi * GD:2 * D + (gi + 1) * GD] = (s - dmc[:, lanes]).astype(BF16)
        dzb_ref[:, 3 * D:5 * D] = dzg_ref[...]

        du = _dot_nt(dzb_ref[:, 0:wcols], w_ref[0])
        for s_i in range(1, ns):
            du = du + _dot_nt(dzb_ref[:, s_i * wcols:(s_i + 1) * wcols], w_ref[s_i])
        h0 = jnp.where(i == 0, head_ref[...], x_ref[...])
        r0 = lax.rsqrt(jnp.mean(h0 * h0, axis=-1, keepdims=True) + RMS_EPS)
        un = h0 * r0
        dgmix_ref[...] += jnp.sum(du * un, axis=0, keepdims=True)
        gd = du * g_ref[...]
        dh0 = dh1_ref[...] + r0 * (gd - un * jnp.mean(un * gd, axis=-1, keepdims=True))
        gx_ref[...] = dh0

        @pl.when(i == 0)
        def _():
            dhead_ref[...] = dh0

    any_spec = pl.BlockSpec(memory_space=pl.ANY)
    in_specs = (_halo_specs(nb) + _halo_specs(nb) + _halo_specs(nb, 2 * D, 2 * D)
                + [_row_spec(2 * D), _const_spec((32, D)), _const_spec((BR, D)), _x_spec(), _const_spec((1, D)),
                   _row_spec(), any_spec, any_spec])
    return pl.pallas_call(
        body, name="mixers_bwd_halo",
        grid=(nb,),
        in_specs=in_specs,
        out_specs=[_row_spec(D_IN), _x_spec(), _const_spec((BR, D)), _const_spec((32, D)), _const_spec((1, D))],
        out_shape=[jax.ShapeDtypeStruct((tp, D_IN), BF16), jax.ShapeDtypeStruct((seq, D), F32),
                   jax.ShapeDtypeStruct((BR, D), F32), jax.ShapeDtypeStruct((32, D), F32),
                   jax.ShapeDtypeStruct((1, D), F32)],
        scratch_shapes=[pltpu.VMEM((ns, D, wcols), BF16), pltpu.SemaphoreType.DMA((1,)),
                        pltpu.VMEM((EXT, D), F32), pltpu.VMEM((EXT, D), F32), pltpu.VMEM((EXT, D), F32),
                        pltpu.VMEM((BR, D), F32), pltpu.VMEM((8, ROT_ROWS, 128), F32), pltpu.VMEM((32, 8, D), F32)],
        compiler_params=_cparams(("arbitrary",)),
    )(dconv, dconv, dconv, dm, dm, dm, z, z, z, dzg, w_dw, head, x, g_mix, dh1, w_in_b, dep)


def _wgrad(a, c, tm, tn, tk, name, diag=False, col_major=False, dep=None):
    tp, m = a.shape
    n = c.shape[1]
    nk = tp // tk
    gm, gn = m // tm, n // tn

    def body(a_ref, c_ref, *rest):
        o_ref, ob_ref = rest[-2:]
        k = pl.program_id(2)

        @pl.when(k == 0)
        def _():
            o_ref[...] = jnp.zeros_like(o_ref)

        o_ref[...] += _dot_tn(a_ref[...], c_ref[...])

        @pl.when(k == nk - 1)
        def _():
            ob_ref[...] = o_ref[...].astype(BF16)

    c_map = lambda i, j, k: (k, j)
    grid = (gm, gn, nk)
    deps = [] if dep is None else [dep]
    if diag:
        grid = (gm, 1, nk)
        c_map = lambda i, j, k: (k, i)
        o_spec = pl.BlockSpec((tm, tn), lambda i, j, k: (i, 0))
        o_shape = (m, tn)
    elif col_major:
        o_spec = pl.BlockSpec((None, tm, tn), lambda i, j, k: (j, i, 0))
        o_shape = (gn, m, tn)
    else:
        o_spec = pl.BlockSpec((tm, tn), lambda i, j, k: (i, j))
        o_shape = (m, n)
    return pl.pallas_call(
        body, name=name,
        grid=grid,
        in_specs=[pl.BlockSpec((tk, tm), lambda i, j, k: (k, i)), pl.BlockSpec((tk, tn), c_map)]
        + [pl.BlockSpec(memory_space=pl.ANY)] * len(deps),
        out_specs=[o_spec, o_spec],
        out_shape=[jax.ShapeDtypeStruct(o_shape, F32), jax.ShapeDtypeStruct(o_shape, BF16)],
        compiler_params=_cparams(("arbitrary", "arbitrary", "arbitrary")),
    )(a, c, *deps)


def _place():
    x, y, c = lax.axis_index("x"), lax.axis_index("y"), lax.axis_index("c")
    others = [(1 - x, y), (x, 1 - y), (1 - x, 1 - y)]
    return x, y, c, others


def _split2(a, axis=0):
    return a.reshape(a.shape[:axis] + (2, a.shape[axis] // 2) + a.shape[axis + 1:])


def _merge2(a, axis=0):
    return a.reshape(a.shape[:axis] + (2 * a.shape[axis + 1],) + a.shape[axis + 2:])


def _cast_into_slot(w2d, chip, dep, name):
    r, c = w2d.shape
    r2 = r // 2

    def body(chip_ref, w_ref, dep_ref, o_ref):
        o_ref[...] = w_ref[...].astype(BF16)

    return pl.pallas_call(
        body, name=name,
        grid_spec=pltpu.PrefetchScalarGridSpec(
            num_scalar_prefetch=1, grid=(2,),
            in_specs=[pl.BlockSpec((r2, c), lambda h, chip_ref: (h, 0)), pl.BlockSpec(memory_space=pl.ANY)],
            out_specs=pl.BlockSpec((None, None, r2, c), lambda h, chip_ref: (chip_ref[0], h, 0, 0))),
        out_shape=jax.ShapeDtypeStruct((N_SHARD, 2, r2, c), BF16),
        compiler_params=_cparams(("arbitrary",)),
    )(chip, w2d, dep)


HBM_SPEC = pl.BlockSpec(memory_space=pltpu.HBM)
SEM_SPEC = pl.BlockSpec(memory_space=pltpu.SEMAPHORE)
DATAFLOW = pltpu.SideEffectType.DATAFLOW_SIDE_EFFECTING
TOKEN = jax.ShapeDtypeStruct((8, 128), F32)


def _in_hbm(a):
    return pltpu.with_memory_space_constraint(a, pltpu.HBM)


def _gather_tiny(v):
    vm = pl.BlockSpec(memory_space=pltpu.VMEM)

    def body(v_ref, out_ref, send_sems, recv_sems):
        x, y, c, others = _place()
        mine = 2 * x + y
        sends = [pltpu.make_async_remote_copy(
            src_ref=v_ref, dst_ref=out_ref.at[mine], send_sem=send_sems.at[j], recv_sem=recv_sems.at[j],
            device_id=(*chip, c), device_id_type=MESH) for j, chip in enumerate(others)]
        for cp in sends:
            cp.start()
        out_ref[mine] = v_ref[...]
        for j, chip in enumerate(others):
            landed = out_ref.at[2 * chip[0] + chip[1]]
            pltpu.make_async_remote_copy(
                src_ref=landed, dst_ref=landed, send_sem=send_sems.at[j], recv_sem=recv_sems.at[j],
                device_id=(x, y, c), device_id_type=MESH).wait_recv()
        for cp in sends:
            cp.wait_send()

    return pl.pallas_call(
        body, name="gather_tiny",
        in_specs=[vm], out_specs=vm,
        out_shape=jax.ShapeDtypeStruct((N_SHARD,) + v.shape, v.dtype),
        scratch_shapes=[pltpu.SemaphoreType.DMA((3,)), pltpu.SemaphoreType.DMA((3,))],
    )(v)


def _ici_copies(srcs, dsts, send_sems, recv_sems, started):
    x, y, c, others = _place()
    mine = 2 * x + y
    copies = []
    for a in range(len(srcs)):
        for j, chip in enumerate(others):
            there = 2 * chip[0] + chip[1]
            src, dst = srcs[a](mine, there, c), dsts[a](mine, there, c)
            if not started:
                dst = dsts[a](there, mine, c)
            copies.append(pltpu.make_async_remote_copy(
                src_ref=src, dst_ref=dst, send_sem=send_sems.at[a * 3 + j], recv_sem=recv_sems.at[a * 3 + j],
                device_id=(*chip, c), device_id_type=MESH))
    return copies


def _split_start(srcs_of, dsts_of, arrays, n_src, name, copies_of=None, n_sems=None, dep=None):
    n = len(arrays)
    n_sems = n_sems or 3 * n_src
    copies_of = copies_of or (lambda ins, ss, rs, started: _ici_copies(srcs_of(ins), dsts_of(ins), ss, rs, started))
    deps = [] if dep is None else [dep]
    nd = len(deps)

    def body(*refs):
        ins = refs[:n]
        send_sems, recv_sems = refs[n + nd], refs[n + nd + 1]
        token = refs[2 * n + nd + 2]
        for cp in copies_of(ins, send_sems, recv_sems, True):
            cp.start()
        token[...] = jnp.zeros_like(token)

    out = pl.pallas_call(
        body, name=name,
        in_specs=[HBM_SPEC] * n + [pl.BlockSpec(memory_space=pl.ANY)] * nd,
        out_specs=(SEM_SPEC, SEM_SPEC, *([HBM_SPEC] * n), pl.BlockSpec(memory_space=pltpu.VMEM)),
        out_shape=(pltpu.SemaphoreType.DMA((n_sems,)), pltpu.SemaphoreType.DMA((n_sems,)),
                   *[pltpu.HBM(a.shape, a.dtype) for a in arrays], TOKEN),
        input_output_aliases={a: 2 + a for a in range(n)},
        compiler_params=pltpu.CompilerParams(has_side_effects=DATAFLOW),
    )(*[_in_hbm(a) for a in arrays], *deps)
    return out[0], out[1], list(out[2:2 + n]), out[2 + n]


def _split_wait(srcs_of, dsts_of, send_sems, recv_sems, arrays, after, name, copies_of=None):
    n = len(arrays)
    copies_of = copies_of or (lambda ins, ss, rs, started: _ici_copies(srcs_of(ins), dsts_of(ins), ss, rs, started))

    def body(*refs):
        ins = refs[:n]
        send_sems, recv_sems = refs[n], refs[n + 1]
        for cp in copies_of(ins, send_sems, recv_sems, False):
            cp.wait_send()
            cp.wait_recv()

    return pl.pallas_call(
        body, name=name,
        in_specs=[HBM_SPEC] * n + [SEM_SPEC, SEM_SPEC] + [pl.BlockSpec(memory_space=pl.ANY)] * len(after),
        out_specs=[HBM_SPEC] * n,
        out_shape=[pltpu.HBM(a.shape, a.dtype) for a in arrays],
        input_output_aliases={a: a for a in range(n)},
        compiler_params=pltpu.CompilerParams(has_side_effects=DATAFLOW),
    )(*arrays, send_sems, recv_sems, *after)


def _gather_views(ins):
    view = [lambda frm, to, c, r=r: r.at[frm, c] for r in ins]
    return view


def _gather_start(bufs, dep, name):
    return _split_start(_gather_views, _gather_views, bufs, len(bufs), name, dep=dep)


def _gather_wait(send_sems, recv_sems, bufs, after, name):
    return _split_wait(_gather_views, _gather_views, send_sems, recv_sems, bufs, after, name)


def _forward_halves(bufs, name):
    n = len(bufs)
    any_spec = pl.BlockSpec(memory_space=pl.ANY)

    def body(*refs):
        outs = refs[n:2 * n]
        send_sems, recv_sems = refs[2 * n:]
        x, y, c, others = _place()
        copies = []
        for a in range(n):
            for j, chip in enumerate(others):
                landed = outs[a].at[2 * chip[0] + chip[1], c]
                copies.append(pltpu.make_async_remote_copy(
                    src_ref=landed, dst_ref=landed, send_sem=send_sems.at[a * 3 + j], recv_sem=recv_sems.at[a * 3 + j],
                    device_id=(x, y, 1 - c), device_id_type=MESH))
        for cp in copies:
            cp.start()
        for a in range(n):
            for j, chip in enumerate(others):
                landed = outs[a].at[2 * chip[0] + chip[1], 1 - c]
                pltpu.make_async_remote_copy(
                    src_ref=landed, dst_ref=landed, send_sem=send_sems.at[a * 3 + j], recv_sem=recv_sems.at[a * 3 + j],
                    device_id=(x, y, c), device_id_type=MESH).wait_recv()
        for cp in copies:
            cp.wait_send()

    out = pl.pallas_call(
        body, name=name,
        in_specs=[any_spec] * n, out_specs=[any_spec] * n,
        out_shape=[jax.ShapeDtypeStruct(b.shape, b.dtype) for b in bufs],
        input_output_aliases={a: a for a in range(n)},
        scratch_shapes=[pltpu.SemaphoreType.DMA((3 * n,)), pltpu.SemaphoreType.DMA((3 * n,))],
    )(*bufs)
    return [_merge2(o, 1) for o in out]


def _swap_halves_bf16(gbs, name):
    n = len(gbs)
    any_spec = pl.BlockSpec(memory_space=pl.ANY)

    def body(*refs):
        ins, outs = refs[:n], refs[n:2 * n]
        send_sems, recv_sems = refs[2 * n:]
        x, y, c, _ = _place()
        copies = []
        for a in range(n):
            copies.append(pltpu.make_async_remote_copy(
                src_ref=ins[a].at[:, 1 - c], dst_ref=outs[a], send_sem=send_sems.at[a], recv_sem=recv_sems.at[a],
                device_id=(x, y, 1 - c), device_id_type=MESH))
        for cp in copies:
            cp.start()
        for cp in copies:
            cp.wait()

    return pl.pallas_call(
        body, name=name,
        in_specs=[any_spec] * n, out_specs=[any_spec] * n,
        out_shape=[jax.ShapeDtypeStruct((g.shape[0], g.shape[1] // 2, g.shape[2]), g.dtype) for g in gbs],
        scratch_shapes=[pltpu.SemaphoreType.DMA((n,)), pltpu.SemaphoreType.DMA((n,))],
    )(*[_split2(g, 1) for g in gbs])


def _scatter_srcs(n):
    return lambda ins: [lambda frm, to, c, r=r: r.at[to] for r in ins[:n]]


def _scatter_dsts(n):
    return lambda ins: [lambda frm, to, c, r=r: r.at[frm] for r in ins[n:]]


def _scatter_start(hbs, name):
    n = len(hbs)
    lands = [lax.empty(h.shape, h.dtype) for h in hbs]
    return _split_start(_scatter_srcs(n), _scatter_dsts(n), list(hbs) + lands, n, name)


def _scatter_wait(send_sems, recv_sems, arrays, after, name):
    n = len(arrays) // 2
    return _split_wait(_scatter_srcs(n), _scatter_dsts(n), send_sems, recv_sems, arrays, after, name)[n:]


def _join_halves(rhs, name):
    n = len(rhs)
    any_spec = pl.BlockSpec(memory_space=pl.ANY)

    def body(*refs):
        outs = refs[n:2 * n]
        send_sems, recv_sems = refs[2 * n:]
        x, y, c, _ = _place()
        copies = []
        for a in range(n):
            copies.append(pltpu.make_async_remote_copy(
                src_ref=outs[a].at[c], dst_ref=outs[a].at[c], send_sem=send_sems.at[a],
                recv_sem=recv_sems.at[a], device_id=(x, y, 1 - c), device_id_type=MESH))
        for cp in copies:
            cp.start()
        for a in range(n):
            landed = outs[a].at[1 - c]
            pltpu.make_async_remote_copy(
                src_ref=landed, dst_ref=landed, send_sem=send_sems.at[a], recv_sem=recv_sems.at[a],
                device_id=(x, y, c), device_id_type=MESH).wait_recv()
        for cp in copies:
            cp.wait_send()

    out = pl.pallas_call(
        body, name=name,
        in_specs=[any_spec] * n, out_specs=[any_spec] * n,
        out_shape=[jax.ShapeDtypeStruct(r.shape, r.dtype) for r in rhs],
        input_output_aliases={a: a for a in range(n)},
        scratch_shapes=[pltpu.SemaphoreType.DMA((n,)), pltpu.SemaphoreType.DMA((n,))],
    )(*rhs)
    return [_merge2(o) for o in out]


FLIPS = [(dx, dy, dc) for dx in (0, 1) for dy in (0, 1) for dc in (0, 1)][1:]


def _peer_copies(ins, send_sems, recv_sems, started):
    x, y, c, _ = _place()
    copies = []
    for k, (dx, dy, dc) in enumerate(FLIPS):
        px, py, pc = jnp.bitwise_xor(x, dx), jnp.bitwise_xor(y, dy), jnp.bitwise_xor(c, dc)
        slot = 4 * x + 2 * y + c if started else 4 * px + 2 * py + pc
        copies.append(pltpu.make_async_remote_copy(
            src_ref=ins[0], dst_ref=ins[1].at[slot], send_sem=send_sems.at[k], recv_sem=recv_sems.at[k],
            device_id=(px, py, pc), device_id_type=MESH))
    return copies


def _small_start(v, name):
    land = lax.empty((8,) + v.shape, v.dtype)
    return _split_start(None, None, [v, land], 0, name, copies_of=_peer_copies, n_sems=len(FLIPS))


def _small_wait(send_sems, recv_sems, arrays, after, name):
    return _split_wait(None, None, send_sems, recv_sems, arrays, after, name, copies_of=_peer_copies)[1]


def _sum_slots(land, v, me):
    rows, cols = v.shape

    def body(me_ref, land_ref, v_ref, o_ref):
        o_ref[...] = jnp.zeros_like(o_ref)
        for d in range(8):
            @pl.when(me_ref[0] == d)
            def _():
                o_ref[...] += v_ref[...]

            @pl.when(me_ref[0] != d)
            def _():
                o_ref[...] += land_ref[d]

    return pl.pallas_call(
        body, name="sum_slots",
        grid_spec=pltpu.PrefetchScalarGridSpec(
            num_scalar_prefetch=1, grid=(1,),
            in_specs=[pl.BlockSpec((8, rows, cols), lambda i, me_ref: (0, 0, 0)),
                      pl.BlockSpec((rows, cols), lambda i, me_ref: (0, 0))],
            out_specs=pl.BlockSpec((rows, cols), lambda i, me_ref: (0, 0))),
        out_shape=jax.ShapeDtypeStruct((rows, cols), F32),
        compiler_params=_cparams(("arbitrary",)),
    )(me, land, v)


def _by_shape(arrays):
    groups = {}
    for k, a in enumerate(arrays):
        groups.setdefault(a.shape, []).append(k)
    return list(groups.values())


def _add_sibling_half(gs, sbs, idx, name):
    n = len(gs)
    ns, r, c = gs[0].shape
    r2 = r // 2

    def body(idx_ref, *refs):
        for a in range(n):
            g_ref, sb_ref, hown_ref, hb_ref = refs[a], refs[n + a], refs[2 * n + a], refs[3 * n + a]
            h = g_ref[...] + sb_ref[...].astype(F32)
            hb_ref[...] = h.astype(BF16)

            @pl.when(pl.program_id(0) == idx_ref[0])
            def _():
                hown_ref[...] = h

    spec = pl.BlockSpec((None, r2, c), lambda s, idx_ref: (s, 0, 0))
    out = pl.pallas_call(
        body, name=name,
        grid_spec=pltpu.PrefetchScalarGridSpec(
            num_scalar_prefetch=1, grid=(ns,),
            in_specs=[pl.BlockSpec((None, r2, c), lambda s, idx_ref: (s, idx_ref[4], 0))] * n + [spec] * n,
            out_specs=[pl.BlockSpec((r2, c), lambda s, idx_ref: (0, 0))] * n + [spec] * n),
        out_shape=[jax.ShapeDtypeStruct((r2, c), F32)] * n + [jax.ShapeDtypeStruct((ns, r2, c), BF16)] * n,
        compiler_params=_cparams(("arbitrary",)),
    )(idx, *gs, *sbs)
    return [(out[a], out[n + a]) for a in range(n)]


def _add_chip_slabs(hs, rbs, idx, name):
    n = len(hs)
    r2, c = hs[0].shape

    def body(idx_ref, *refs):
        for a in range(n):
            h_ref, r0_ref, r1_ref, r2_ref = refs[4 * a:4 * a + 4]
            refs[4 * n + a][...] = ((h_ref[...] + r0_ref[...].astype(F32)) + r1_ref[...].astype(F32)) + r2_ref[...].astype(F32)

    def pick(k):
        return pl.BlockSpec((None, r2, c), lambda i, idx_ref: (idx_ref[k], 0, 0))

    operands = []
    for h, rb in zip(hs, rbs):
        operands += [h, rb, rb, rb]
    out = pl.pallas_call(
        body, name=name,
        grid_spec=pltpu.PrefetchScalarGridSpec(
            num_scalar_prefetch=1, grid=(1,),
            in_specs=[pl.BlockSpec((r2, c), lambda i, idx_ref: (0, 0)), pick(1), pick(2), pick(3)] * n,
            out_specs=[pl.BlockSpec((None, r2, c), lambda i, idx_ref: (idx_ref[4], 0, 0))] * n),
        out_shape=[jax.ShapeDtypeStruct((2, r2, c), F32)] * n,
        compiler_params=_cparams(("arbitrary",)),
    )(idx, *operands)
    return list(out)


ELEMENTWISE_VMEM = 16 * 1024 * 1024


def _adamw(items, name):
    n = len(items)
    r, c = items[0][0].shape
    br = max(b for b in range(8, r + 1, 8) if r % b == 0 and n * 16 * b * c * 4 <= ELEMENTWISE_VMEM) if r % 8 == 0 else r

    def body(*refs):
        for a in range(n):
            g_ref, w_ref, m_ref, v_ref = refs[4 * a:4 * a + 4]
            go_ref, d_ref, nm_ref, nv_ref = refs[4 * n + 4 * a:4 * n + 4 * a + 4]
            gg = g_ref[...]
            go_ref[...] = gg
            nm = B1 * m_ref[...] + (1.0 - B1) * gg
            nv = B2 * v_ref[...] + (1.0 - B2) * jnp.square(gg)
            m_hat = nm / (1.0 - B1 ** STEP)
            v_hat = nv / (1.0 - B2 ** STEP)
            d_ref[...] = -LR * (m_hat / (jnp.sqrt(v_hat) + ADAM_EPS) + WD * w_ref[...])
            nm_ref[...] = nm
            nv_ref[...] = nv

    spec = pl.BlockSpec((br, c), lambda i: (i, 0))
    out = pl.pallas_call(
        body, name=name,
        grid=(r // br,),
        in_specs=[spec] * (4 * n), out_specs=[spec] * (4 * n),
        out_shape=[jax.ShapeDtypeStruct((r, c), F32)] * (4 * n),
        compiler_params=_cparams(("arbitrary",)),
    )(*[a for item in items for a in item])
    return [tuple(out[4 * a:4 * a + 4]) for a in range(n)]


BIG = ("w_in", "w_conv_out", "w_pool", "w_pool_out", "w_o", "w_ffn_gate", "w_ffn_up", "w_ffn_down")
REPL = ("g_mix", "b_gate", "b_dw", "ln_g", "ln_b", "pool_scale", "g_ffn", "g_final")
GROUP_MIX = ("w_conv_out", "w_pool", "w_pool_out", "w_o")
GROUP_FFN = ("w_ffn_gate", "w_ffn_up", "w_ffn_down")
TRANSPOSED = ("w_ffn_gate", "w_ffn_up")
WEIGHT_ORDER = ("meta_tokens", "g_mix", "w_in", "b_gate", "w_dw", "b_dw", "ln_g", "ln_b", "w_conv_out", "w_pool",
                "pool_scale", "w_pool_out", "w_o", "g_ffn", "w_ffn_gate", "w_ffn_up", "w_ffn_down", "g_final")


def _shard2d(name, a):
    a = a[0]
    if name == "w_pool":
        return a.reshape(4 * 64, GD)
    if name in TRANSPOSED:
        return a.T
    return a


def _unshard2d(name, a, shape):
    return a.T.reshape(shape) if name in TRANSPOSED else a.reshape(shape)


def _cols_to_slabs(a):
    m, n = a.shape
    return a.reshape(m, N_SHARD, n // N_SHARD).transpose(1, 0, 2)


def _slabs_to_cols(a):
    ns, m, c = a.shape
    return a.transpose(1, 0, 2).reshape(m, ns * c)


def kernel(x, meta_tokens, g_mix, w_in, b_gate, w_dw, b_dw, ln_g, ln_b, w_conv_out, w_pool, pool_scale, w_pool_out, w_o, g_ffn, w_ffn_gate, w_ffn_up, w_ffn_down, g_final, loss_target, m_meta_tokens, m_g_mix, m_w_in, m_b_gate, m_w_dw, m_b_dw, m_ln_g, m_ln_b, m_w_conv_out, m_w_pool, m_pool_scale, m_w_pool_out, m_w_o, m_g_ffn, m_w_ffn_gate, m_w_ffn_up, m_w_ffn_down, m_g_final, v_meta_tokens, v_g_mix, v_w_in, v_b_gate, v_w_dw, v_b_dw, v_ln_g, v_ln_b, v_w_conv_out, v_w_pool, v_pool_scale, v_w_pool_out, v_w_o, v_g_ffn, v_w_ffn_gate, v_w_ffn_up, v_w_ffn_down, v_g_final):
    args = dict(locals())
    w = {n: args[n] for n in WEIGHT_ORDER}
    mom = {n: args["m_" + n] for n in WEIGHT_ORDER}
    var = {n: args["v_" + n] for n in WEIGHT_ORDER}
    seq = x.shape[1]
    nb = seq // BR + 1
    tp = nb * BR
    tk = tp // 2 if (tp // 2) % 16 == 0 else BR
    t_total = seq + N_META
    cx, cy, cc = lax.axis_index("x"), lax.axis_index("y"), lax.axis_index("c")
    chip = 2 * cx + cy
    chip1 = jnp.reshape(chip, (1,)).astype(jnp.int32)
    core = jnp.reshape(cc, (1,)).astype(jnp.int32)
    others = jnp.sort(jnp.stack([2 * (1 - cx) + cy, 2 * cx + (1 - cy), 2 * (1 - cx) + (1 - cy)]))
    idx = jnp.concatenate([chip1, others.astype(jnp.int32), core])
    xs, target = x[0], loss_target[0]

    tiny = _gather_tiny(jnp.concatenate([w["meta_tokens"], w["w_dw"][0], jnp.zeros((1, GD), F32)], axis=0))
    small = {n: w[n] for n in REPL if n != "g_final"}
    small["g_final"] = w["g_final"].reshape(1, D)
    small["w_dw"] = _slabs_to_cols(tiny[:, N_META:])
    head = jnp.concatenate([jnp.zeros((PAD, D), F32), _slabs_to_cols(tiny[:, :N_META])], axis=0)

    def cast(group, dep):
        return [_cast_into_slot(_shard2d(n, w[n]), chip1, dep, "cast_" + n) for n in group]

    def gather_finish(group, start, after, name):
        landed = _gather_wait(start[0], start[1], start[2], after, "gather_wait_" + name)
        return dict(zip(group, _forward_halves(landed, "forward_" + name)))

    st_in = _gather_start(cast(("w_in",), tiny), None, "gather_start_in")
    bufs_mix, bufs_ffn = cast(GROUP_MIX, st_in[3]), cast(GROUP_FFN, st_in[3])
    u = _rms_u(head, xs, small["g_mix"] + st_in[3][0:1, 0:1], nb)
    z_own = _in_proj_own(u, w["w_in"][0], idx, nb)
    gw = gather_finish(("w_in",), st_in, [z_own] + bufs_mix + bufs_ffn, "in")
    st_mix = _gather_start(bufs_mix, gw["w_in"], "gather_start_mix")
    z = _in_proj_rest(u, gw["w_in"], z_own, idx, st_mix[3], nb)
    gw.update(gather_finish(GROUP_MIX, st_mix, [z], "mix"))
    st_ffn = _gather_start(bufs_ffn, gw["w_o"], "gather_start_ffn")
    w_pool_b = gw["w_pool"].reshape(N_SHARD, 4, 64, GD).transpose(1, 0, 2, 3).reshape(4, GD, GD)
    w_co_b, w_po_b, w_o_b = (gw[n].reshape(D, D) for n in ("w_conv_out", "w_pool_out", "w_o"))
    h1, yc, yp, mg, ca, cpre, m, mw, m2b = _mixers_fwd(
        z, head, xs, small["b_gate"] + st_ffn[3][0, 0], small["w_dw"], small["b_dw"], small["ln_g"], small["ln_b"],
        small["pool_scale"], w_co_b, w_pool_b, w_po_b, w_o_b, nb, t_total)
    gw.update(gather_finish(GROUP_FFN, st_ffn, [h1], "ffn"))

    dh1, dh1b, vb, fb, dgb, dub, dh2b, loss, dg_ffn, dg_final = _ffn_fwd_bwd(
        h1, target, small["g_ffn"], small["g_final"], gw["w_ffn_gate"].reshape(D_FF, D),
        gw["w_ffn_up"].reshape(D_FF, D), gw["w_ffn_down"].reshape(D_FF, D), nb)

    def slabs(name, g):
        if name == "w_in":
            return g
        if name == "w_pool":
            return g.reshape(4, N_SHARD, 64, GD).transpose(1, 0, 2, 3).reshape(N_SHARD, 4 * 64, GD)
        return g.reshape(N_SHARD, g.shape[0] // N_SHARD, g.shape[1])

    def reduce_start(group, grads, name):
        g32 = [slabs(n, grads[n][0]) for n in group]
        g16 = [slabs(n, grads[n][1]) for n in group]
        from_sibling = _swap_halves_bf16(g16, "swap_halves_" + name)
        halves = [None] * len(group)
        for ks in _by_shape(g32):
            done = _add_sibling_half([g32[k] for k in ks], [from_sibling[k] for k in ks], idx, "add_sibling_" + group[ks[0]])
            for k, pair in zip(ks, done):
                halves[k] = pair
        return [h for h, _ in halves], _scatter_start([hb for _, hb in halves], "scatter_start_" + name)

    def reduce_finish(group, halves, start, after, name):
        from_chips = _scatter_wait(start[0], start[1], start[2], after, "scatter_wait_" + name)
        reduced = [None] * len(group)
        for ks in _by_shape(halves):
            done = _add_chip_slabs([halves[k] for k in ks], [from_chips[k] for k in ks], idx, "add_chips_" + group[ks[0]])
            for k, r in zip(ks, done):
                reduced[k] = r
        return reduced

    half_ff = D_FF // 2
    grads_ffn = {
        "w_ffn_gate": _wgrad(dgb, vb, half_ff, D, tk, "wgrad_ffn_gate"),
        "w_ffn_up": _wgrad(dub, vb, half_ff, D, tk, "wgrad_ffn_up"),
        "w_ffn_down": _wgrad(fb, dh2b, half_ff, D, tk, "wgrad_ffn_down"),
    }
    halves_ffn, sc_ffn = reduce_start(GROUP_FFN, grads_ffn, "ffn")

    dycb, dypb, dzg, dconv, dmwb, dm, db_gate, dln_g, dln_b, db_dw, dps = _mixers_bwd_rows(
        dh1b, yc, yp, z, small["b_gate"], cpre, small["ln_g"], small["ln_b"], mw, small["pool_scale"],
        w_o_b, w_co_b, w_po_b, w_pool_b, sc_ffn[3], nb)
    grads_mix = {
        "w_conv_out": _wgrad(ca, dycb, D, D, tk, "wgrad_conv_out"),
        "w_pool": _wgrad(m, dmwb, GD, GD, tk, "wgrad_pool", diag=True),
        "w_pool_out": _wgrad(m2b, dypb, D, D, tk, "wgrad_pool_out"),
        "w_o": _wgrad(mg, dh1b, D, D, tk, "wgrad_o"),
    }
    halves_mix, sc_mix = reduce_start(GROUP_MIX, grads_mix, "mix")
    dzb, grad_x, dhead, dw_dw, dg_mix = _mixers_bwd_halo(
        dconv, dm, z, dzg, small["w_dw"], head, xs, small["g_mix"], dh1, gw["w_in"], sc_mix[3], nb, t_total)
    packed = jnp.concatenate(
        [dg_mix, db_gate.reshape(2, D), db_dw, dln_g, dln_b, dps, dg_ffn, dg_final,
         jnp.broadcast_to(loss, (1, D)), jnp.zeros((6, D), F32), dhead[PAD:], dw_dw], axis=0)
    sm = _small_start(packed, "small_start")
    grads_in = {"w_in": _wgrad(u, dzb, D, D_IN // N_SHARD, tk, "wgrad_in", col_major=True, dep=sm[3])}
    halves_in, sc_in = reduce_start(("w_in",), grads_in, "in")

    land = _small_wait(sm[0], sm[1], sm[2], [sc_in[3]], "small_wait")
    summed = _sum_slots(land, packed, jnp.reshape(4 * cx + 2 * cy + cc, (1,)).astype(jnp.int32))
    loss = summed[9, 0]

    first = GROUP_FFN + GROUP_MIX
    reduced_half = reduce_finish(GROUP_FFN, halves_ffn, sc_ffn, [summed], "ffn")
    reduced_half += reduce_finish(GROUP_MIX, halves_mix, sc_mix, [summed], "mix")
    reduced = dict(zip(first, _join_halves(reduced_half, "join_halves_first")))
    updates = {}
    for ks in _by_shape([reduced[n] for n in first]):
        names = [first[k] for k in ks]
        done = _adamw([(reduced[n], _shard2d(n, w[n]), _shard2d(n, mom[n]), _shard2d(n, var[n])) for n in names],
                      "adamw_" + names[0])
        updates.update(zip(names, done))

    def repl_stack(d):
        return jnp.concatenate([d["g_mix"], d["b_gate"].reshape(2, D), d["b_dw"], d["ln_g"], d["ln_b"],
                                d["pool_scale"], d["g_ffn"], d["g_final"].reshape(1, D), jnp.ones((7, D), F32)], axis=0)

    def shard_stack(d):
        return jnp.concatenate([d["meta_tokens"], d["w_dw"][0], jnp.ones((1, GD), F32)], axis=0)

    g_repl = summed[0:16]
    g_shard = lax.dynamic_slice_in_dim(summed[16:64], chip * GD, GD, axis=1)
    g_repl, d_repl, m_repl, v_repl = _adamw([(g_repl, repl_stack(w), repl_stack(mom), repl_stack(var))], "adamw_repl")[0]
    g_shard, d_shard, m_shard, v_shard = _adamw(
        [(g_shard, shard_stack(w), shard_stack(mom), shard_stack(var))], "adamw_cols")[0]

    done_first = [updates[n][1] for n in first] + [d_repl, d_shard]
    last_half = reduce_finish(("w_in",), halves_in, sc_in, done_first, "in")
    reduced["w_in"] = _join_halves(last_half, "join_halves_in")[0]
    updates["w_in"] = _adamw([(reduced["w_in"], w["w_in"][0], mom["w_in"][0], var["w_in"][0])], "adamw_w_in")[0]

    def unpack(name, repl, shard):
        if name == "meta_tokens":
            return shard[0:N_META]
        if name == "w_dw":
            return shard[N_META:N_META + KW].reshape(1, KW, GD)
        row = {"g_mix": 0, "b_gate": 1, "b_dw": 3, "ln_g": 4, "ln_b": 5, "pool_scale": 6, "g_ffn": 7, "g_final": 8}[name]
        if name == "b_gate":
            return repl[1:3].reshape(1, 2 * D)
        if name == "g_final":
            return repl[8]
        return repl[row:row + 1]

    out_g, out_d, out_m, out_v = {}, {}, {}, {}
    for n in WEIGHT_ORDER:
        if n in BIG:
            g, d_, m_, v_ = updates[n]
            shape = w[n].shape
            out_g[n], out_d[n], out_m[n], out_v[n] = (_unshard2d(n, a, shape) for a in (g, d_, m_, v_))
        else:
            out_g[n] = unpack(n, g_repl, g_shard)
            out_d[n] = unpack(n, d_repl, d_shard)
            out_m[n] = unpack(n, m_repl, m_shard)
            out_v[n] = unpack(n, v_repl, v_shard)
    return (loss, grad_x[None], *[out_g[n] for n in WEIGHT_ORDER], *[out_d[n] for n in WEIGHT_ORDER],
            *[out_m[n] for n in WEIGHT_ORDER], *[out_v[n] for n in WEIGHT_ORDER])
```

```python
import jax
import jax.numpy as jnp
from jax import lax
from jax.experimental import pallas as pl
from jax.experimental.pallas import tpu as pltpu

F32 = jnp.float32
BF16 = jnp.bfloat16
MESH = pl.DeviceIdType.MESH

D = 1024
N_META = 16
KW = 31
POOL_WINDOWS = (2, 4, 8, 16)
GD = 256
D_IN = 5 * D
D_FF = 2816
N_SHARD = 4
BR = 256
HALO = 16
PAD = BR - N_META
EXT = BR + 2 * HALO
RMS_EPS = 1e-6
LN_EPS = 1e-5
LR, B1, B2, ADAM_EPS, WD, STEP = 0.001, 0.9, 0.999, 1e-08, 0.01, 10
VMEM_LIMIT = 56 * 1024 * 1024


def _cparams(sem, vmem=VMEM_LIMIT):
    return pltpu.CompilerParams(dimension_semantics=sem, vmem_limit_bytes=vmem)


def _dot(a, b):
    return jnp.dot(a, b, preferred_element_type=F32)


def _dot_nt(a, b):
    return lax.dot_general(a, b, (((1,), (1,)), ((), ())), preferred_element_type=F32)


def _dot_tn(a, b):
    return lax.dot_general(a, b, (((0,), (0,)), ((), ())), preferred_element_type=F32)


def _sigmoid(x):
    return 0.5 * jnp.tanh(0.5 * x) + 0.5


def _row_ids(i, n, offset=0):
    return lax.broadcasted_iota(jnp.int32, (n, 1), 0) + (i * BR + offset - PAD)


def _pool_cnt(t, w, t_total):
    left = w // 2
    right = w - 1 - left
    lo = jnp.clip(t - left, 0, t_total)
    hi = jnp.clip(t + right + 1, 0, t_total)
    return jnp.maximum(hi - lo, 1).astype(F32)


def _halo_specs(nb, halo_width=D, width=D):
    last = nb * (BR // HALO) - 1
    return [
        pl.BlockSpec((HALO, halo_width), lambda i: (jnp.maximum(i * (BR // HALO) - 1, 0), 0)),
        pl.BlockSpec((BR, width), lambda i: (i, 0)),
        pl.BlockSpec((HALO, halo_width), lambda i: (jnp.minimum((i + 1) * (BR // HALO), last), 0)),
    ]


def _cols(ref, n):
    return [ref.at[:, k * D:(k + 1) * D] for k in range(n)]


def _fill_ext(ext_ref, prev, cur, nxt, i, nb):
    ext_ref[0:HALO, :] = jnp.where(i > 0, prev, 0.0)
    ext_ref[HALO:HALO + BR, :] = cur
    ext_ref[HALO + BR:EXT, :] = jnp.where(i < nb - 1, nxt, 0.0)


ROT_ROWS = EXT - 8


def _fill_rot(rot_ref, ext_ref, lanes):
    for r in range(1, 8):
        rot_ref[r] = ext_ref[pl.ds(r, ROT_ROWS), lanes]


def _tap(rot_ref, ext_ref, lanes, offset):
    q, r = divmod(offset, 8)
    if r == 0:
        return ext_ref[pl.ds(8 * q, BR), lanes]
    return rot_ref[r, pl.ds(8 * q, BR), :]


def _row_spec(width=D):
    return pl.BlockSpec((BR, width), lambda i: (i, 0))


def _x_spec():
    return pl.BlockSpec((BR, D), lambda i: (jnp.maximum(i - 1, 0), 0))


def _const_spec(shape):
    nd = len(shape)
    return pl.BlockSpec(shape, lambda i: (0,) * nd)


def _rms_u(head, x, g_mix, nb):
    def body(head_ref, x_ref, g_ref, u_ref):
        i = pl.program_id(0)
        h = jnp.where(i == 0, head_ref[...], x_ref[...])
        r = lax.rsqrt(jnp.mean(h * h, axis=-1, keepdims=True) + RMS_EPS)
        u_ref[...] = ((h * r) * g_ref[...]).astype(BF16)

    return pl.pallas_call(
        body, name="rms_u",
        grid=(nb,),
        in_specs=[_const_spec((BR, D)), _x_spec(), _const_spec((1, D))],
        out_specs=_row_spec(),
        out_shape=jax.ShapeDtypeStruct((nb * BR, D), BF16),
        compiler_params=_cparams(("arbitrary",)),
    )(head, x, g_mix)


def _in_proj_rows(tp):
    return tp // 4 if (tp // 4) % 16 == 0 else BR


def _in_proj_own(u, w_own, idx, nb):
    tp = nb * BR
    wcols = w_own.shape[1]
    rows = _in_proj_rows(tp)

    def body(idx_ref, u_ref, w_ref, z_ref, wb_ref):
        @pl.when(pl.program_id(0) == 0)
        def _():
            wb_ref[...] = w_ref[...].astype(BF16)

        z_ref[...] = _dot(u_ref[...], wb_ref[...])

    return pl.pallas_call(
        body, name="in_proj_own",
        grid_spec=pltpu.PrefetchScalarGridSpec(
            num_scalar_prefetch=1, grid=(tp // rows,),
            in_specs=[pl.BlockSpec((rows, D), lambda i, idx_ref: (i, 0)),
                      pl.BlockSpec((D, wcols), lambda i, idx_ref: (0, 0))],
            out_specs=pl.BlockSpec((rows, wcols), lambda i, idx_ref: (i, idx_ref[0])),
            scratch_shapes=[pltpu.VMEM((D, wcols), BF16)]),
        out_shape=jax.ShapeDtypeStruct((tp, N_SHARD * wcols), F32),
        compiler_params=_cparams(("arbitrary",)),
    )(idx, u, w_own)


def _in_proj_rest(u, w_in_b, z, idx, dep, nb):
    tp = nb * BR
    wcols = w_in_b.shape[2]
    rows = _in_proj_rows(tp)

    def body(idx_ref, u_ref, w_ref, z_in, dep_ref, z_ref):
        z_ref[...] = _dot(u_ref[...], w_ref[...])

    any_spec = pl.BlockSpec(memory_space=pl.ANY)
    return pl.pallas_call(
        body, name="in_proj_rest",
        grid_spec=pltpu.PrefetchScalarGridSpec(
            num_scalar_prefetch=1, grid=(N_SHARD - 1, tp // rows),
            in_specs=[pl.BlockSpec((rows, D), lambda s, i, idx_ref: (i, 0)),
                      pl.BlockSpec((None, D, wcols), lambda s, i, idx_ref: (idx_ref[1 + s], 0, 0)),
                      any_spec, any_spec],
            out_specs=pl.BlockSpec((rows, wcols), lambda s, i, idx_ref: (i, idx_ref[1 + s]))),
        out_shape=jax.ShapeDtypeStruct(z.shape, F32),
        input_output_aliases={3: 0},
        compiler_params=_cparams(("arbitrary", "arbitrary")),
    )(idx, u, w_in_b, z, dep)


def _mixers_fwd(z, head, x, b_gate, w_dw, b_dw, ln_g, ln_b, pool_scale, w_co, w_pool, w_po, w_o, nb, t_total):
    tp = nb * BR

    def body(z_prev, z_cur, z_next, head_ref, x_ref, bg_ref, wdw_ref, bdw_ref,
             lng_ref, lnb_ref, ps_ref, wco_ref, wpool_ref, wpo_ref, wo_ref,
             h1_ref, yc_ref, yp_ref, mg_ref, ca_ref, cpre_ref, m_ref, mw_ref, m2b_ref, ext_ref, pext_ref, rot_ref):
        i = pl.program_id(0)
        avp, agp, pp = _cols(z_prev, 3)
        av, ag, pc, za, zb = _cols(z_cur, 5)
        avn, agn, pn = _cols(z_next, 3)
        _fill_ext(ext_ref, avp[...] * _sigmoid(agp[...]), av[...] * _sigmoid(ag[...]),
                  avn[...] * _sigmoid(agn[...]), i, nb)
        _fill_ext(pext_ref, pp[...], pc[...], pn[...], i, nb)

        def conv_chunk(c, carry):
            lanes = pl.ds(pl.multiple_of(c * 128, 128), 128)
            _fill_rot(rot_ref, ext_ref, lanes)
            acc = jnp.broadcast_to(bdw_ref[:, lanes], (BR, 128))
            for k in range(KW):
                acc = acc + wdw_ref[k:k + 1, lanes] * _tap(rot_ref, ext_ref, lanes, 1 + k)
            cpre_ref[:, lanes] = acc
            return carry
        lax.fori_loop(0, D // 128, conv_chunk, 0)

        conv = cpre_ref[...]
        mu = jnp.mean(conv, axis=-1, keepdims=True)
        xc = conv - mu
        rstd = lax.rsqrt(jnp.mean(xc * xc, axis=-1, keepdims=True) + LN_EPS)
        ln = (xc * rstd) * lng_ref[...] + lnb_ref[...]
        cact = (ln * _sigmoid(ln)).astype(BF16)
        ca_ref[...] = cact
        y_conv = _dot(cact, wco_ref[...])
        yc_ref[...] = y_conv

        t = _row_ids(i, BR)
        for gi, w in enumerate(POOL_WINDOWS):
            left = w // 2
            right = w - 1 - left
            lanes = slice(gi * GD, (gi + 1) * GD)
            s = pext_ref[pl.ds(HALO - left, BR), lanes]
            for j in range(-left + 1, right + 1):
                s = s + pext_ref[pl.ds(HALO + j, BR), lanes]
            m = (s / _pool_cnt(t, w, t_total) - pext_ref[HALO:HALO + BR, lanes]).astype(BF16)
            m_ref[:, lanes] = m
            mw_ref[:, lanes] = _dot(m, wpool_ref[gi])
        mw = mw_ref[...]
        m2b = (mw * ps_ref[...]).astype(BF16)
        m2b_ref[...] = m2b
        y_pool = _dot(m2b, wpo_ref[...])
        yp_ref[...] = y_pool

        s_a = _sigmoid(za[...] + bg_ref[:, 0:D])
        s_b = _sigmoid(zb[...] + bg_ref[:, D:2 * D])
        merged = (s_a * y_conv + s_b * y_pool).astype(BF16)
        mg_ref[...] = merged
        h0 = jnp.where(i == 0, head_ref[...], x_ref[...])
        h1_ref[...] = h0 + _dot(merged, wo_ref[...])

    in_specs = (_halo_specs(nb, 3 * D, 5 * D)
                + [_const_spec((BR, D)), _x_spec(), _const_spec((1, 2 * D)), _const_spec((32, D)),
                   _const_spec((1, D)), _const_spec((1, D)), _const_spec((1, D)), _const_spec((1, D)),
                   _const_spec((D, D)), _const_spec((4, GD, GD)), _const_spec((D, D)), _const_spec((D, D))])
    outs = [(F32, "h1"), (F32, "yc"), (F32, "yp"), (BF16, "mg"), (BF16, "ca"), (F32, "cpre"), (BF16, "m"), (F32, "mw"),
            (BF16, "m2b")]
    return pl.pallas_call(
        body, name="mixers_fwd",
        grid=(nb,),
        in_specs=in_specs,
        out_specs=[_row_spec() for _ in outs],
        out_shape=[jax.ShapeDtypeStruct((tp, D), dt) for dt, _ in outs],
        scratch_shapes=[pltpu.VMEM((EXT, D), F32), pltpu.VMEM((EXT, D), F32), pltpu.VMEM((8, ROT_ROWS, 128), F32)],
        compiler_params=_cparams(("arbitrary",)),
    )(z, z, z, head, x, b_gate, w_dw, b_dw, ln_g, ln_b, pool_scale, w_co, w_pool, w_po, w_o)


def _ffn_fwd_bwd(h1, target, g_ffn, g_final, w_g, w_u, w_d, nb):
    tp = nb * BR

    def body(h1_ref, tgt_ref, gf_ref, gfin_ref, wg_hbm, wu_hbm, wd_hbm,
             dh1_ref, dh1b_ref, vb_ref, fb_ref, dgb_ref, dub_ref, dh2b_ref, loss_ref, dgf_ref, dgfin_ref,
             wg_ref, wu_ref, wd_ref, sem):
        i = pl.program_id(0)

        @pl.when(i == 0)
        def _():
            copies = [pltpu.make_async_copy(wg_hbm, wg_ref, sem.at[0]),
                      pltpu.make_async_copy(wu_hbm, wu_ref, sem.at[1]),
                      pltpu.make_async_copy(wd_hbm, wd_ref, sem.at[2])]
            for cp in copies:
                cp.start()
            loss_ref[...] = jnp.zeros_like(loss_ref)
            dgf_ref[...] = jnp.zeros_like(dgf_ref)
            dgfin_ref[...] = jnp.zeros_like(dgfin_ref)
            for cp in copies:
                cp.wait()

        h1 = h1_ref[...]
        r1 = lax.rsqrt(jnp.mean(h1 * h1, axis=-1, keepdims=True) + RMS_EPS)
        vn = h1 * r1
        vb = (vn * gf_ref[...]).astype(BF16)
        vb_ref[...] = vb
        g = _dot_nt(vb, wg_ref[...])
        up = _dot_nt(vb, wu_ref[...])
        sg = _sigmoid(g)
        sl = g * sg
        fb = (sl * up).astype(BF16)
        fb_ref[...] = fb
        h2 = h1 + _dot(fb, wd_ref[...])
        r2 = lax.rsqrt(jnp.mean(h2 * h2, axis=-1, keepdims=True) + RMS_EPS)
        yn = h2 * r2
        valid = i > 0
        diff = jnp.where(valid, yn * gfin_ref[...] - tgt_ref[...], 0.0)
        loss_ref[...] += 0.5 * jnp.sum(jnp.mean(diff * diff, axis=-1, keepdims=True))
        dy = diff * (1.0 / D)
        dgfin_ref[...] += jnp.sum(dy * yn, axis=0, keepdims=True)
        gd = dy * gfin_ref[...]
        dh2 = r2 * (gd - yn * jnp.mean(yn * gd, axis=-1, keepdims=True))
        dh2b = dh2.astype(BF16)
        dh2b_ref[...] = dh2b
        df = _dot_nt(dh2b, wd_ref[...])
        dub = (df * sl).astype(BF16)
        dgb = (df * up * (sg * (1.0 + g * (1.0 - sg)))).astype(BF16)
        dub_ref[...] = dub
        dgb_ref[...] = dgb
        dv = _dot(dgb, wg_ref[...]) + _dot(dub, wu_ref[...])
        dgf_ref[...] += jnp.sum(dv * vn, axis=0, keepdims=True)
        gd1 = dv * gf_ref[...]
        dh1 = dh2 + r1 * (gd1 - vn * jnp.mean(vn * gd1, axis=-1, keepdims=True))
        dh1_ref[...] = dh1
        dh1b_ref[...] = dh1.astype(BF16)

    any_spec = pl.BlockSpec(memory_space=pl.ANY)
    return pl.pallas_call(
        body, name="ffn_fwd_bwd",
        grid=(nb,),
        in_specs=[_row_spec(), _x_spec(), _const_spec((1, D)), _const_spec((1, D)), any_spec, any_spec, any_spec],
        out_specs=[_row_spec(), _row_spec(), _row_spec(), _row_spec(D_FF), _row_spec(D_FF), _row_spec(D_FF), _row_spec(),
                   _const_spec((1, 1)), _const_spec((1, D)), _const_spec((1, D))],
        out_shape=[jax.ShapeDtypeStruct((tp, D), F32), jax.ShapeDtypeStruct((tp, D), BF16),
                   jax.ShapeDtypeStruct((tp, D), BF16), jax.ShapeDtypeStruct((tp, D_FF), BF16),
                   jax.ShapeDtypeStruct((tp, D_FF), BF16), jax.ShapeDtypeStruct((tp, D_FF), BF16),
                   jax.ShapeDtypeStruct((tp, D), BF16), jax.ShapeDtypeStruct((1, 1), F32),
                   jax.ShapeDtypeStruct((1, D), F32), jax.ShapeDtypeStruct((1, D), F32)],
        scratch_shapes=[pltpu.VMEM((D_FF, D), BF16), pltpu.VMEM((D_FF, D), BF16), pltpu.VMEM((D_FF, D), BF16),
                        pltpu.SemaphoreType.DMA((3,))],
        compiler_params=_cparams(("arbitrary",)),
    )(h1, target, g_ffn, g_final, w_g, w_u, w_d)


def _mixers_bwd_rows(dh1b, yc, yp, z, b_gate, cpre, ln_g, ln_b, mw, pool_scale, w_o, w_co, w_po, w_pool, dep, nb):
    tp = nb * BR

    def body(dh1b_ref, yc_ref, yp_ref, za, zb, bg_ref, cpre_ref, lng_ref, lnb_ref, mw_ref, ps_ref,
             wo_ref, wco_ref, wpo_ref, wpool_ref, dep_ref,
             dycb_ref, dypb_ref, dzg_ref, dconv_ref, dmwb_ref, dm_ref, dbg_ref, dlng_ref, dlnb_ref, dbdw_ref, dps_ref):
        i = pl.program_id(0)

        @pl.when(i == 0)
        def _():
            for r in (dbg_ref, dlng_ref, dlnb_ref, dbdw_ref, dps_ref):
                r[...] = jnp.zeros_like(r)

        dmg = _dot_nt(dh1b_ref[...], wo_ref[...])
        s_a = _sigmoid(za[...] + bg_ref[:, 0:D])
        s_b = _sigmoid(zb[...] + bg_ref[:, D:2 * D])
        dycb = (dmg * s_a).astype(BF16)
        dypb = (dmg * s_b).astype(BF16)
        dycb_ref[...] = dycb
        dypb_ref[...] = dypb
        dza = dmg * yc_ref[...] * (s_a * (1.0 - s_a))
        dzb = dmg * yp_ref[...] * (s_b * (1.0 - s_b))
        dzg_ref[:, 0:D] = dza.astype(BF16)
        dzg_ref[:, D:2 * D] = dzb.astype(BF16)
        dbg_ref[:, 0:D] += jnp.sum(dza, axis=0, keepdims=True)
        dbg_ref[:, D:2 * D] += jnp.sum(dzb, axis=0, keepdims=True)

        dca = _dot_nt(dycb, wco_ref[...])
        conv = cpre_ref[...]
        mu = jnp.mean(conv, axis=-1, keepdims=True)
        xc = conv - mu
        rstd = lax.rsqrt(jnp.mean(xc * xc, axis=-1, keepdims=True) + LN_EPS)
        xhat = xc * rstd
        ln = xhat * lng_ref[...] + lnb_ref[...]
        sg = _sigmoid(ln)
        dln = dca * (sg * (1.0 + ln * (1.0 - sg)))
        dlng_ref[...] += jnp.sum(dln * xhat, axis=0, keepdims=True)
        dlnb_ref[...] += jnp.sum(dln, axis=0, keepdims=True)
        dxh = dln * lng_ref[...]
        dconv = rstd * (dxh - jnp.mean(dxh, axis=-1, keepdims=True)
                        - xhat * jnp.mean(dxh * xhat, axis=-1, keepdims=True))
        dconv_ref[...] = dconv
        dbdw_ref[...] += jnp.sum(dconv, axis=0, keepdims=True)

        dm2 = _dot_nt(dypb, wpo_ref[...])
        dps_ref[...] += jnp.sum(dm2 * mw_ref[...], axis=0, keepdims=True)
        dmwb = (dm2 * ps_ref[...]).astype(BF16)
        dmwb_ref[...] = dmwb
        for gi in range(len(POOL_WINDOWS)):
            lanes = slice(gi * GD, (gi + 1) * GD)
            dm_ref[:, lanes] = _dot_nt(dmwb[:, lanes], wpool_ref[gi])

    in_specs = [_row_spec(), _row_spec(), _row_spec(),
                pl.BlockSpec((BR, D), lambda i: (i, 3)), pl.BlockSpec((BR, D), lambda i: (i, 4)),
                _const_spec((1, 2 * D)), _row_spec(), _const_spec((1, D)), _const_spec((1, D)), _row_spec(),
                _const_spec((1, D)), _const_spec((D, D)), _const_spec((D, D)), _const_spec((D, D)),
                _const_spec((4, GD, GD)), pl.BlockSpec(memory_space=pl.ANY)]
    return pl.pallas_call(
        body, name="mixers_bwd_rows",
        grid=(nb,),
        in_specs=in_specs,
        out_specs=[_row_spec(), _row_spec(), _row_spec(2 * D), _row_spec(), _row_spec(), _row_spec(),
                   _const_spec((1, 2 * D)), _const_spec((1, D)), _const_spec((1, D)), _const_spec((1, D)),
                   _const_spec((1, D))],
        out_shape=[jax.ShapeDtypeStruct((tp, D), BF16), jax.ShapeDtypeStruct((tp, D), BF16),
                   jax.ShapeDtypeStruct((tp, 2 * D), BF16), jax.ShapeDtypeStruct((tp, D), F32),
                   jax.ShapeDtypeStruct((tp, D), BF16), jax.ShapeDtypeStruct((tp, D), F32),
                   jax.ShapeDtypeStruct((1, 2 * D), F32), jax.ShapeDtypeStruct((1, D), F32),
                   jax.ShapeDtypeStruct((1, D), F32), jax.ShapeDtypeStruct((1, D), F32),
                   jax.ShapeDtypeStruct((1, D), F32)],
        compiler_params=_cparams(("arbitrary",)),
    )(dh1b, yc, yp, z, z, b_gate, cpre, ln_g, ln_b, mw, pool_scale, w_o, w_co, w_po, w_pool, dep)


def _mixers_bwd_halo(dconv, dm, z, dzg, w_dw, head, x, g_mix, dh1, w_in_b, dep, nb, t_total):
    tp = nb * BR
    ns = w_in_b.shape[0]
    wcols = w_in_b.shape[2]
    seq = x.shape[0]

    def body(dcp, dcc, dcn, dmp, dmc, dmn, z_prev, z_cur, z_next, dzg_ref, wdw_ref, head_ref, x_ref, g_ref,
             dh1_ref, w_hbm, dep_ref,
             dzb_ref, gx_ref, dhead_ref, dwdw_ref, dgmix_ref,
             w_ref, sem, aext_ref, dext_ref, qext_ref, da_ref, rot_ref, dwp_ref):
        i = pl.program_id(0)
        (avp, agp), (av, ag), (avn, agn) = _cols(z_prev, 2), _cols(z_cur, 2), _cols(z_next, 2)

        @pl.when(i == 0)
        def _():
            cp = pltpu.make_async_copy(w_hbm, w_ref, sem.at[0])
            cp.start()
            dwp_ref[...] = jnp.zeros_like(dwp_ref)
            dgmix_ref[...] = jnp.zeros_like(dgmix_ref)
            cp.wait()

        sig_g = _sigmoid(ag[...])
        _fill_ext(aext_ref, avp[...] * _sigmoid(agp[...]), av[...] * sig_g, avn[...] * _sigmoid(agn[...]), i, nb)
        _fill_ext(dext_ref, dcp[...], dcc[...], dcn[...], i, nb)
        _fill_ext(qext_ref, dmp[...], dmc[...], dmn[...], i, nb)

        def conv_chunk(c, carry):
            lanes = pl.ds(pl.multiple_of(c * 128, 128), 128)
            _fill_rot(rot_ref, dext_ref, lanes)
            acc = jnp.zeros((BR, 128), F32)
            for k in range(KW):
                acc = acc + wdw_ref[k:k + 1, lanes] * _tap(rot_ref, dext_ref, lanes, KW - k)
            da_ref[:, lanes] = acc
            _fill_rot(rot_ref, aext_ref, lanes)
            dcv = dext_ref[HALO:HALO + BR, lanes]
            for k in range(KW):
                prod = _tap(rot_ref, aext_ref, lanes, 1 + k) * dcv
                dwp_ref[k, :, lanes] += jnp.sum(prod.reshape(BR // 8, 8, 128), axis=0)
            return carry
        lax.fori_loop(0, D // 128, conv_chunk, 0)

        @pl.when(i == nb - 1)
        def _():
            dwdw_ref[...] = jnp.sum(dwp_ref[...], axis=1)

        da = da_ref[...]
        a_val = av[...]
        dzb_ref[:, 0:D] = (da * sig_g).astype(BF16)
        dzb_ref[:, D:2 * D] = (da * a_val * (sig_g * (1.0 - sig_g))).astype(BF16)

        t_ext = _row_ids(i, EXT, -HALO)
        for gi, w in enumerate(POOL_WINDOWS):
            left = w // 2
            right = w - 1 - left
            lanes = slice(gi * GD, (gi + 1) * GD)
            qext_ref[:, lanes] = qext_ref[:, lanes] / _pool_cnt(t_ext, w, t_total)
            s = qext_ref[pl.ds(HALO - right, BR), lanes]
            for j in range(-right + 1, left + 1):
                s = s + qext_ref[pl.ds(HALO + j, BR), lanes]
            dzb_ref[:, 2 * D + gi * GD:2 * D + (gi + 1) * GD] = (s - dmc[:, lanes]).astype(BF16)
        dzb_ref[:, 3 * D:5 * D] = dzg_ref[...]

        du = _dot_nt(dzb_ref[:, 0:wcols], w_ref[0])
        for s_i in range(1, ns):
            du = du + _dot_nt(dzb_ref[:, s_i * wcols:(s_i + 1) * wcols], w_ref[s_i])
        h0 = jnp.where(i == 0, head_ref[...], x_ref[...])
        r0 = lax.rsqrt(jnp.mean(h0 * h0, axis=-1, keepdims=True) + RMS_EPS)
        un = h0 * r0
        dgmix_ref[...] += jnp.sum(du * un, axis=0, keepdims=True)
        gd = du * g_ref[...]
        dh0 = dh1_ref[...] + r0 * (gd - un * jnp.mean(un * gd, axis=-1, keepdims=True))
        gx_ref[...] = dh0

        @pl.when(i == 0)
        def _():
            dhead_ref[...] = dh0

    any_spec = pl.BlockSpec(memory_space=pl.ANY)
    in_specs = (_halo_specs(nb) + _halo_specs(nb) + _halo_specs(nb, 2 * D, 2 * D)
                + [_row_spec(2 * D), _const_spec((32, D)), _const_spec((BR, D)), _x_spec(), _const_spec((1, D)),
                   _row_spec(), any_spec, any_spec])
    return pl.pallas_call(
        body, name="mixers_bwd_halo",
        grid=(nb,),
        in_specs=in_specs,
        out_specs=[_row_spec(D_IN), _x_spec(), _const_spec((BR, D)), _const_spec((32, D)), _const_spec((1, D))],
        out_shape=[jax.ShapeDtypeStruct((tp, D_IN), BF16), jax.ShapeDtypeStruct((seq, D), F32),
                   jax.ShapeDtypeStruct((BR, D), F32), jax.ShapeDtypeStruct((32, D), F32),
                   jax.ShapeDtypeStruct((1, D), F32)],
        scratch_shapes=[pltpu.VMEM((ns, D, wcols), BF16), pltpu.SemaphoreType.DMA((1,)),
                        pltpu.VMEM((EXT, D), F32), pltpu.VMEM((EXT, D), F32), pltpu.VMEM((EXT, D), F32),
                        pltpu.VMEM((BR, D), F32), pltpu.VMEM((8, ROT_ROWS, 128), F32), pltpu.VMEM((32, 8, D), F32)],
        compiler_params=_cparams(("arbitrary",)),
    )(dconv, dconv, dconv, dm, dm, dm, z, z, z, dzg, w_dw, head, x, g_mix, dh1, w_in_b, dep)


def _wgrad(a, c, tm, tn, tk, name, diag=False, col_major=False, dep=None):
    tp, m = a.shape
    n = c.shape[1]
    nk = tp // tk
    gm, gn = m // tm, n // tn

    def body(a_ref, c_ref, *rest):
        o_ref, ob_ref = rest[-2:]
        k = pl.program_id(2)

        @pl.when(k == 0)
        def _():
            o_ref[...] = jnp.zeros_like(o_ref)

        o_ref[...] += _dot_tn(a_ref[...], c_ref[...])

        @pl.when(k == nk - 1)
        def _():
            ob_ref[...] = o_ref[...].astype(BF16)

    c_map = lambda i, j, k: (k, j)
    grid = (gm, gn, nk)
    deps = [] if dep is None else [dep]
    if diag:
        grid = (gm, 1, nk)
        c_map = lambda i, j, k: (k, i)
        o_spec = pl.BlockSpec((tm, tn), lambda i, j, k: (i, 0))
        o_shape = (m, tn)
    elif col_major:
        o_spec = pl.BlockSpec((None, tm, tn), lambda i, j, k: (j, i, 0))
        o_shape = (gn, m, tn)
    else:
        o_spec = pl.BlockSpec((tm, tn), lambda i, j, k: (i, j))
        o_shape = (m, n)
    return pl.pallas_call(
        body, name=name,
        grid=grid,
        in_specs=[pl.BlockSpec((tk, tm), lambda i, j, k: (k, i)), pl.BlockSpec((tk, tn), c_map)]
        + [pl.BlockSpec(memory_space=pl.ANY)] * len(deps),
        out_specs=[o_spec, o_spec],
        out_shape=[jax.ShapeDtypeStruct(o_shape, F32), jax.ShapeDtypeStruct(o_shape, BF16)],
        compiler_params=_cparams(("arbitrary", "arbitrary", "arbitrary")),
    )(a, c, *deps)


def _place():
    x, y, c = lax.axis_index("x"), lax.axis_index("y"), lax.axis_index("c")
    others = [(1 - x, y), (x, 1 - y), (1 - x, 1 - y)]
    return x, y, c, others


def _split2(a, axis=0):
    return a.reshape(a.shape[:axis] + (2, a.shape[axis] // 2) + a.shape[axis + 1:])


def _merge2(a, axis=0):
    return a.reshape(a.shape[:axis] + (2 * a.shape[axis + 1],) + a.shape[axis + 2:])


def _cast_into_slot(shards, chip, dep, name):
    n = len(shards)
    r, c = shards[0].shape
    r2 = r // 2

    def body(chip_ref, *refs):
        for a in range(n):
            refs[n + 1 + a][...] = refs[a][...].astype(BF16)

    out = pl.pallas_call(
        body, name=name,
        grid_spec=pltpu.PrefetchScalarGridSpec(
            num_scalar_prefetch=1, grid=(2,),
            in_specs=[pl.BlockSpec((r2, c), lambda h, chip_ref: (h, 0))] * n + [pl.BlockSpec(memory_space=pl.ANY)],
            out_specs=[pl.BlockSpec((None, None, r2, c), lambda h, chip_ref: (chip_ref[0], h, 0, 0))] * n),
        out_shape=[jax.ShapeDtypeStruct((N_SHARD, 2, r2, c), BF16)] * n,
        compiler_params=_cparams(("arbitrary",)),
    )(chip, *shards, dep)
    return list(out)


HBM_SPEC = pl.BlockSpec(memory_space=pltpu.HBM)
SEM_SPEC = pl.BlockSpec(memory_space=pltpu.SEMAPHORE)
DATAFLOW = pltpu.SideEffectType.DATAFLOW_SIDE_EFFECTING
TOKEN = jax.ShapeDtypeStruct((8, 128), F32)


def _in_hbm(a):
    return pltpu.with_memory_space_constraint(a, pltpu.HBM)


def _gather_tiny(v):
    vm = pl.BlockSpec(memory_space=pltpu.VMEM)

    def body(v_ref, out_ref, send_sems, recv_sems):
        x, y, c, others = _place()
        mine = 2 * x + y
        sends = [pltpu.make_async_remote_copy(
            src_ref=v_ref, dst_ref=out_ref.at[mine], send_sem=send_sems.at[j], recv_sem=recv_sems.at[j],
            device_id=(*chip, c), device_id_type=MESH) for j, chip in enumerate(others)]
        for cp in sends:
            cp.start()
        out_ref[mine] = v_ref[...]
        for j, chip in enumerate(others):
            landed = out_ref.at[2 * chip[0] + chip[1]]
            pltpu.make_async_remote_copy(
                src_ref=landed, dst_ref=landed, send_sem=send_sems.at[j], recv_sem=recv_sems.at[j],
                device_id=(x, y, c), device_id_type=MESH).wait_recv()
        for cp in sends:
            cp.wait_send()

    return pl.pallas_call(
        body, name="gather_tiny",
        in_specs=[vm], out_specs=vm,
        out_shape=jax.ShapeDtypeStruct((N_SHARD,) + v.shape, v.dtype),
        scratch_shapes=[pltpu.SemaphoreType.DMA((3,)), pltpu.SemaphoreType.DMA((3,))],
    )(v)


def _ici_copies(srcs, dsts, send_sems, recv_sems, started):
    x, y, c, others = _place()
    mine = 2 * x + y
    copies = []
    for a in range(len(srcs)):
        for j, chip in enumerate(others):
            there = 2 * chip[0] + chip[1]
            src, dst = srcs[a](mine, there, c), dsts[a](mine, there, c)
            if not started:
                dst = dsts[a](there, mine, c)
            copies.append(pltpu.make_async_remote_copy(
                src_ref=src, dst_ref=dst, send_sem=send_sems.at[a * 3 + j], recv_sem=recv_sems.at[a * 3 + j],
                device_id=(*chip, c), device_id_type=MESH))
    return copies


def _split_start(srcs_of, dsts_of, arrays, n_src, name, copies_of=None, n_sems=None, dep=None):
    n = len(arrays)
    n_sems = n_sems or 3 * n_src
    copies_of = copies_of or (lambda ins, ss, rs, started: _ici_copies(srcs_of(ins), dsts_of(ins), ss, rs, started))
    deps = [] if dep is None else [dep]
    nd = len(deps)

    def body(*refs):
        ins = refs[:n]
        send_sems, recv_sems = refs[n + nd], refs[n + nd + 1]
        token = refs[2 * n + nd + 2]
        for cp in copies_of(ins, send_sems, recv_sems, True):
            cp.start()
        token[...] = jnp.zeros_like(token)

    out = pl.pallas_call(
        body, name=name,
        in_specs=[HBM_SPEC] * n + [pl.BlockSpec(memory_space=pl.ANY)] * nd,
        out_specs=(SEM_SPEC, SEM_SPEC, *([HBM_SPEC] * n), pl.BlockSpec(memory_space=pltpu.VMEM)),
        out_shape=(pltpu.SemaphoreType.DMA((n_sems,)), pltpu.SemaphoreType.DMA((n_sems,)),
                   *[pltpu.HBM(a.shape, a.dtype) for a in arrays], TOKEN),
        input_output_aliases={a: 2 + a for a in range(n)},
        compiler_params=pltpu.CompilerParams(has_side_effects=DATAFLOW),
    )(*[_in_hbm(a) for a in arrays], *deps)
    return out[0], out[1], list(out[2:2 + n]), out[2 + n]


def _split_wait(srcs_of, dsts_of, send_sems, recv_sems, arrays, after, name, copies_of=None):
    n = len(arrays)
    copies_of = copies_of or (lambda ins, ss, rs, started: _ici_copies(srcs_of(ins), dsts_of(ins), ss, rs, started))

    def body(*refs):
        ins = refs[:n]
        send_sems, recv_sems = refs[n], refs[n + 1]
        for cp in copies_of(ins, send_sems, recv_sems, False):
            cp.wait_send()
            cp.wait_recv()

    return pl.pallas_call(
        body, name=name,
        in_specs=[HBM_SPEC] * n + [SEM_SPEC, SEM_SPEC] + [pl.BlockSpec(memory_space=pl.ANY)] * len(after),
        out_specs=[HBM_SPEC] * n,
        out_shape=[pltpu.HBM(a.shape, a.dtype) for a in arrays],
        input_output_aliases={a: a for a in range(n)},
        compiler_params=pltpu.CompilerParams(has_side_effects=DATAFLOW),
    )(*arrays, send_sems, recv_sems, *after)


def _gather_views(ins):
    view = [lambda frm, to, c, r=r: r.at[frm, c] for r in ins]
    return view


def _gather_start(bufs, dep, name):
    return _split_start(_gather_views, _gather_views, bufs, len(bufs), name, dep=dep)


def _gather_wait(send_sems, recv_sems, bufs, after, name):
    return _split_wait(_gather_views, _gather_views, send_sems, recv_sems, bufs, after, name)


def _forward_halves(bufs, name):
    n = len(bufs)
    any_spec = pl.BlockSpec(memory_space=pl.ANY)

    def body(*refs):
        outs = refs[n:2 * n]
        send_sems, recv_sems = refs[2 * n:]
        x, y, c, others = _place()
        copies = []
        for a in range(n):
            for j, chip in enumerate(others):
                landed = outs[a].at[2 * chip[0] + chip[1], c]
                copies.append(pltpu.make_async_remote_copy(
                    src_ref=landed, dst_ref=landed, send_sem=send_sems.at[a * 3 + j], recv_sem=recv_sems.at[a * 3 + j],
                    device_id=(x, y, 1 - c), device_id_type=MESH))
        for cp in copies:
            cp.start()
        for a in range(n):
            for j, chip in enumerate(others):
                landed = outs[a].at[2 * chip[0] + chip[1], 1 - c]
                pltpu.make_async_remote_copy(
                    src_ref=landed, dst_ref=landed, send_sem=send_sems.at[a * 3 + j], recv_sem=recv_sems.at[a * 3 + j],
                    device_id=(x, y, c), device_id_type=MESH).wait_recv()
        for cp in copies:
            cp.wait_send()

    out = pl.pallas_call(
        body, name=name,
        in_specs=[any_spec] * n, out_specs=[any_spec] * n,
        out_shape=[jax.ShapeDtypeStruct(b.shape, b.dtype) for b in bufs],
        input_output_aliases={a: a for a in range(n)},
        scratch_shapes=[pltpu.SemaphoreType.DMA((3 * n,)), pltpu.SemaphoreType.DMA((3 * n,))],
    )(*bufs)
    return [_merge2(o, 1) for o in out]


def _swap_halves_bf16(gbs, name):
    n = len(gbs)
    any_spec = pl.BlockSpec(memory_space=pl.ANY)

    def body(*refs):
        ins, outs = refs[:n], refs[n:2 * n]
        send_sems, recv_sems = refs[2 * n:]
        x, y, c, _ = _place()
        copies = []
        for a in range(n):
            copies.append(pltpu.make_async_remote_copy(
                src_ref=ins[a].at[:, 1 - c], dst_ref=outs[a], send_sem=send_sems.at[a], recv_sem=recv_sems.at[a],
                device_id=(x, y, 1 - c), device_id_type=MESH))
        for cp in copies:
            cp.start()
        for cp in copies:
            cp.wait()

    return pl.pallas_call(
        body, name=name,
        in_specs=[any_spec] * n, out_specs=[any_spec] * n,
        out_shape=[jax.ShapeDtypeStruct((g.shape[0], g.shape[1] // 2, g.shape[2]), g.dtype) for g in gbs],
        scratch_shapes=[pltpu.SemaphoreType.DMA((n,)), pltpu.SemaphoreType.DMA((n,))],
    )(*[_split2(g, 1) for g in gbs])


def _scatter_srcs(n):
    return lambda ins: [lambda frm, to, c, r=r: r.at[to] for r in ins[:n]]


def _scatter_dsts(n):
    return lambda ins: [lambda frm, to, c, r=r: r.at[frm] for r in ins[n:]]


def _scatter_start(hbs, name):
    n = len(hbs)
    lands = [lax.empty(h.shape, h.dtype) for h in hbs]
    return _split_start(_scatter_srcs(n), _scatter_dsts(n), list(hbs) + lands, n, name)


def _scatter_wait(send_sems, recv_sems, arrays, after, name):
    n = len(arrays) // 2
    return _split_wait(_scatter_srcs(n), _scatter_dsts(n), send_sems, recv_sems, arrays, after, name)[n:]


def _join_halves(rhs, name):
    n = len(rhs)
    any_spec = pl.BlockSpec(memory_space=pl.ANY)

    def body(*refs):
        outs = refs[n:2 * n]
        send_sems, recv_sems = refs[2 * n:]
        x, y, c, _ = _place()
        copies = []
        for a in range(n):
            copies.append(pltpu.make_async_remote_copy(
                src_ref=outs[a].at[c], dst_ref=outs[a].at[c], send_sem=send_sems.at[a],
                recv_sem=recv_sems.at[a], device_id=(x, y, 1 - c), device_id_type=MESH))
        for cp in copies:
            cp.start()
        for a in range(n):
            landed = outs[a].at[1 - c]
            pltpu.make_async_remote_copy(
                src_ref=landed, dst_ref=landed, send_sem=send_sems.at[a], recv_sem=recv_sems.at[a],
                device_id=(x, y, c), device_id_type=MESH).wait_recv()
        for cp in copies:
            cp.wait_send()

    out = pl.pallas_call(
        body, name=name,
        in_specs=[any_spec] * n, out_specs=[any_spec] * n,
        out_shape=[jax.ShapeDtypeStruct(r.shape, r.dtype) for r in rhs],
        input_output_aliases={a: a for a in range(n)},
        scratch_shapes=[pltpu.SemaphoreType.DMA((n,)), pltpu.SemaphoreType.DMA((n,))],
    )(*rhs)
    return [_merge2(o) for o in out]


FLIPS = [(dx, dy, dc) for dx in (0, 1) for dy in (0, 1) for dc in (0, 1)][1:]


def _peer_copies(ins, send_sems, recv_sems, started):
    x, y, c, _ = _place()
    copies = []
    for k, (dx, dy, dc) in enumerate(FLIPS):
        px, py, pc = jnp.bitwise_xor(x, dx), jnp.bitwise_xor(y, dy), jnp.bitwise_xor(c, dc)
        slot = 4 * x + 2 * y + c if started else 4 * px + 2 * py + pc
        copies.append(pltpu.make_async_remote_copy(
            src_ref=ins[0], dst_ref=ins[1].at[slot], send_sem=send_sems.at[k], recv_sem=recv_sems.at[k],
            device_id=(px, py, pc), device_id_type=MESH))
    return copies


def _small_start(v, name):
    land = lax.empty((8,) + v.shape, v.dtype)
    return _split_start(None, None, [v, land], 0, name, copies_of=_peer_copies, n_sems=len(FLIPS))


def _small_wait(send_sems, recv_sems, arrays, after, name):
    return _split_wait(None, None, send_sems, recv_sems, arrays, after, name, copies_of=_peer_copies)[1]


def _sum_slots(land, v, me):
    rows, cols = v.shape

    def body(me_ref, land_ref, v_ref, o_ref):
        o_ref[...] = jnp.zeros_like(o_ref)
        for d in range(8):
            @pl.when(me_ref[0] == d)
            def _():
                o_ref[...] += v_ref[...]

            @pl.when(me_ref[0] != d)
            def _():
                o_ref[...] += land_ref[d]

    return pl.pallas_call(
        body, name="sum_slots",
        grid_spec=pltpu.PrefetchScalarGridSpec(
            num_scalar_prefetch=1, grid=(1,),
            in_specs=[pl.BlockSpec((8, rows, cols), lambda i, me_ref: (0, 0, 0)),
                      pl.BlockSpec((rows, cols), lambda i, me_ref: (0, 0))],
            out_specs=pl.BlockSpec((rows, cols), lambda i, me_ref: (0, 0))),
        out_shape=jax.ShapeDtypeStruct((rows, cols), F32),
        compiler_params=_cparams(("arbitrary",)),
    )(me, land, v)


def _by_shape(arrays):
    groups = {}
    for k, a in enumerate(arrays):
        groups.setdefault(a.shape, []).append(k)
    return list(groups.values())


def _add_sibling_half(gs, sbs, idx, name):
    n = len(gs)
    ns, r, c = gs[0].shape
    r2 = r // 2

    def body(idx_ref, *refs):
        for a in range(n):
            g_ref, sb_ref, hown_ref, hb_ref = refs[a], refs[n + a], refs[2 * n + a], refs[3 * n + a]
            h = g_ref[...] + sb_ref[...].astype(F32)
            hb_ref[...] = h.astype(BF16)

            @pl.when(pl.program_id(0) == idx_ref[0])
            def _():
                hown_ref[...] = h

    spec = pl.BlockSpec((None, r2, c), lambda s, idx_ref: (s, 0, 0))
    out = pl.pallas_call(
        body, name=name,
        grid_spec=pltpu.PrefetchScalarGridSpec(
            num_scalar_prefetch=1, grid=(ns,),
            in_specs=[pl.BlockSpec((None, r2, c), lambda s, idx_ref: (s, idx_ref[4], 0))] * n + [spec] * n,
            out_specs=[pl.BlockSpec((r2, c), lambda s, idx_ref: (0, 0))] * n + [spec] * n),
        out_shape=[jax.ShapeDtypeStruct((r2, c), F32)] * n + [jax.ShapeDtypeStruct((ns, r2, c), BF16)] * n,
        compiler_params=_cparams(("arbitrary",)),
    )(idx, *gs, *sbs)
    return [(out[a], out[n + a]) for a in range(n)]


def _add_chip_slabs(hs, rbs, idx, name):
    n = len(hs)
    r2, c = hs[0].shape

    def body(idx_ref, *refs):
        for a in range(n):
            h_ref, r0_ref, r1_ref, r2_ref = refs[4 * a:4 * a + 4]
            refs[4 * n + a][...] = ((h_ref[...] + r0_ref[...].astype(F32)) + r1_ref[...].astype(F32)) + r2_ref[...].astype(F32)

    def pick(k):
        return pl.BlockSpec((None, r2, c), lambda i, idx_ref: (idx_ref[k], 0, 0))

    operands = []
    for h, rb in zip(hs, rbs):
        operands += [h, rb, rb, rb]
    out = pl.pallas_call(
        body, name=name,
        grid_spec=pltpu.PrefetchScalarGridSpec(
            num_scalar_prefetch=1, grid=(1,),
            in_specs=[pl.BlockSpec((r2, c), lambda i, idx_ref: (0, 0)), pick(1), pick(2), pick(3)] * n,
            out_specs=[pl.BlockSpec((None, r2, c), lambda i, idx_ref: (idx_ref[4], 0, 0))] * n),
        out_shape=[jax.ShapeDtypeStruct((2, r2, c), F32)] * n,
        compiler_params=_cparams(("arbitrary",)),
    )(idx, *operands)
    return list(out)


ELEMENTWISE_VMEM = 16 * 1024 * 1024


def _adamw(items, name):
    n = len(items)
    r, c = items[0][0].shape
    br = max(b for b in range(8, r + 1, 8) if r % b == 0 and n * 16 * b * c * 4 <= ELEMENTWISE_VMEM) if r % 8 == 0 else r

    def body(*refs):
        for a in range(n):
            g_ref, w_ref, m_ref, v_ref = refs[4 * a:4 * a + 4]
            go_ref, d_ref, nm_ref, nv_ref = refs[4 * n + 4 * a:4 * n + 4 * a + 4]
            gg = g_ref[...]
            go_ref[...] = gg
            nm = B1 * m_ref[...] + (1.0 - B1) * gg
            nv = B2 * v_ref[...] + (1.0 - B2) * jnp.square(gg)
            m_hat = nm / (1.0 - B1 ** STEP)
            v_hat = nv / (1.0 - B2 ** STEP)
            d_ref[...] = -LR * (m_hat / (jnp.sqrt(v_hat) + ADAM_EPS) + WD * w_ref[...])
            nm_ref[...] = nm
            nv_ref[...] = nv

    spec = pl.BlockSpec((br, c), lambda i: (i, 0))
    out = pl.pallas_call(
        body, name=name,
        grid=(r // br,),
        in_specs=[spec] * (4 * n), out_specs=[spec] * (4 * n),
        out_shape=[jax.ShapeDtypeStruct((r, c), F32)] * (4 * n),
        compiler_params=_cparams(("arbitrary",)),
    )(*[a for item in items for a in item])
    return [tuple(out[4 * a:4 * a + 4]) for a in range(n)]


BIG = ("w_in", "w_conv_out", "w_pool", "w_pool_out", "w_o", "w_ffn_gate", "w_ffn_up", "w_ffn_down")
REPL = ("g_mix", "b_gate", "b_dw", "ln_g", "ln_b", "pool_scale", "g_ffn", "g_final")
GROUP_MIX = ("w_conv_out", "w_pool", "w_pool_out", "w_o")
GROUP_FFN = ("w_ffn_gate", "w_ffn_up", "w_ffn_down")
TRANSPOSED = ("w_ffn_gate", "w_ffn_up")
WEIGHT_ORDER = ("meta_tokens", "g_mix", "w_in", "b_gate", "w_dw", "b_dw", "ln_g", "ln_b", "w_conv_out", "w_pool",
                "pool_scale", "w_pool_out", "w_o", "g_ffn", "w_ffn_gate", "w_ffn_up", "w_ffn_down", "g_final")


def _shard2d(name, a):
    a = a[0]
    if name == "w_pool":
        return a.reshape(4 * 64, GD)
    if name in TRANSPOSED:
        return a.T
    return a


def _unshard2d(name, a, shape):
    return a.T.reshape(shape) if name in TRANSPOSED else a.reshape(shape)


def _cols_to_slabs(a):
    m, n = a.shape
    return a.reshape(m, N_SHARD, n // N_SHARD).transpose(1, 0, 2)


def _slabs_to_cols(a):
    ns, m, c = a.shape
    return a.transpose(1, 0, 2).reshape(m, ns * c)


def kernel(x, meta_tokens, g_mix, w_in, b_gate, w_dw, b_dw, ln_g, ln_b, w_conv_out, w_pool, pool_scale, w_pool_out, w_o, g_ffn, w_ffn_gate, w_ffn_up, w_ffn_down, g_final, loss_target, m_meta_tokens, m_g_mix, m_w_in, m_b_gate, m_w_dw, m_b_dw, m_ln_g, m_ln_b, m_w_conv_out, m_w_pool, m_pool_scale, m_w_pool_out, m_w_o, m_g_ffn, m_w_ffn_gate, m_w_ffn_up, m_w_ffn_down, m_g_final, v_meta_tokens, v_g_mix, v_w_in, v_b_gate, v_w_dw, v_b_dw, v_ln_g, v_ln_b, v_w_conv_out, v_w_pool, v_pool_scale, v_w_pool_out, v_w_o, v_g_ffn, v_w_ffn_gate, v_w_ffn_up, v_w_ffn_down, v_g_final):
    args = dict(locals())
    w = {n: args[n] for n in WEIGHT_ORDER}
    mom = {n: args["m_" + n] for n in WEIGHT_ORDER}
    var = {n: args["v_" + n] for n in WEIGHT_ORDER}
    seq = x.shape[1]
    nb = seq // BR + 1
    tp = nb * BR
    tk = tp // 2 if (tp // 2) % 16 == 0 else BR
    t_total = seq + N_META
    cx, cy, cc = lax.axis_index("x"), lax.axis_index("y"), lax.axis_index("c")
    chip = 2 * cx + cy
    chip1 = jnp.reshape(chip, (1,)).astype(jnp.int32)
    core = jnp.reshape(cc, (1,)).astype(jnp.int32)
    others = jnp.sort(jnp.stack([2 * (1 - cx) + cy, 2 * cx + (1 - cy), 2 * (1 - cx) + (1 - cy)]))
    idx = jnp.concatenate([chip1, others.astype(jnp.int32), core])
    xs, target = x[0], loss_target[0]

    tiny = _gather_tiny(jnp.concatenate([w["meta_tokens"], w["w_dw"][0], jnp.zeros((1, GD), F32)], axis=0))
    small = {n: w[n] for n in REPL if n != "g_final"}
    small["g_final"] = w["g_final"].reshape(1, D)
    small["w_dw"] = _slabs_to_cols(tiny[:, N_META:])
    head = jnp.concatenate([jnp.zeros((PAD, D), F32), _slabs_to_cols(tiny[:, :N_META])], axis=0)

    def cast(group, dep):
        shards = [_shard2d(n, w[n]) for n in group]
        bufs = [None] * len(group)
        for ks in _by_shape(shards):
            done = _cast_into_slot([shards[k] for k in ks], chip1, dep, "cast_" + group[ks[0]])
            for k, b in zip(ks, done):
                bufs[k] = b
        return bufs

    def gather_finish(group, start, after, name):
        landed = _gather_wait(start[0], start[1], start[2], after, "gather_wait_" + name)
        return dict(zip(group, _forward_halves(landed, "forward_" + name)))

    st_in = _gather_start(cast(("w_in",), tiny), None, "gather_start_in")
    bufs_mix, bufs_ffn = cast(GROUP_MIX, st_in[3]), cast(GROUP_FFN, st_in[3])
    u = _rms_u(head, xs, small["g_mix"] + st_in[3][0:1, 0:1], nb)
    z_own = _in_proj_own(u, w["w_in"][0], idx, nb)
    gw = gather_finish(("w_in",), st_in, [z_own] + bufs_mix + bufs_ffn, "in")
    st_mix = _gather_start(bufs_mix, gw["w_in"], "gather_start_mix")
    z = _in_proj_rest(u, gw["w_in"], z_own, idx, st_mix[3], nb)
    gw.update(gather_finish(GROUP_MIX, st_mix, [z], "mix"))
    st_ffn = _gather_start(bufs_ffn, gw["w_o"], "gather_start_ffn")
    w_pool_b = gw["w_pool"].reshape(N_SHARD, 4, 64, GD).transpose(1, 0, 2, 3).reshape(4, GD, GD)
    w_co_b, w_po_b, w_o_b = (gw[n].reshape(D, D) for n in ("w_conv_out", "w_pool_out", "w_o"))
    h1, yc, yp, mg, ca, cpre, m, mw, m2b = _mixers_fwd(
        z, head, xs, small["b_gate"] + st_ffn[3][0, 0], small["w_dw"], small["b_dw"], small["ln_g"], small["ln_b"],
        small["pool_scale"], w_co_b, w_pool_b, w_po_b, w_o_b, nb, t_total)
    gw.update(gather_finish(GROUP_FFN, st_ffn, [h1], "ffn"))

    dh1, dh1b, vb, fb, dgb, dub, dh2b, loss, dg_ffn, dg_final = _ffn_fwd_bwd(
        h1, target, small["g_ffn"], small["g_final"], gw["w_ffn_gate"].reshape(D_FF, D),
        gw["w_ffn_up"].reshape(D_FF, D), gw["w_ffn_down"].reshape(D_FF, D), nb)

    def slabs(name, g):
        if name == "w_in":
            return g
        if name == "w_pool":
            return g.reshape(4, N_SHARD, 64, GD).transpose(1, 0, 2, 3).reshape(N_SHARD, 4 * 64, GD)
        return g.reshape(N_SHARD, g.shape[0] // N_SHARD, g.shape[1])

    def reduce_start(group, grads, name):
        g32 = [slabs(n, grads[n][0]) for n in group]
        g16 = [slabs(n, grads[n][1]) for n in group]
        from_sibling = _swap_halves_bf16(g16, "swap_halves_" + name)
        halves = [None] * len(group)
        for ks in _by_shape(g32):
            done = _add_sibling_half([g32[k] for k in ks], [from_sibling[k] for k in ks], idx, "add_sibling_" + group[ks[0]])
            for k, pair in zip(ks, done):
                halves[k] = pair
        return [h for h, _ in halves], _scatter_start([hb for _, hb in halves], "scatter_start_" + name)

    def reduce_finish(group, halves, start, after, name):
        from_chips = _scatter_wait(start[0], start[1], start[2], after, "scatter_wait_" + name)
        reduced = [None] * len(group)
        for ks in _by_shape(halves):
            done = _add_chip_slabs([halves[k] for k in ks], [from_chips[k] for k in ks], idx, "add_chips_" + group[ks[0]])
            for k, r in zip(ks, done):
                reduced[k] = r
        return reduced

    half_ff = D_FF // 2
    grads_ffn = {
        "w_ffn_gate": _wgrad(dgb, vb, half_ff, D, tk, "wgrad_ffn_gate"),
        "w_ffn_up": _wgrad(dub, vb, half_ff, D, tk, "wgrad_ffn_up"),
        "w_ffn_down": _wgrad(fb, dh2b, half_ff, D, tk, "wgrad_ffn_down"),
    }
    halves_ffn, sc_ffn = reduce_start(GROUP_FFN, grads_ffn, "ffn")

    dycb, dypb, dzg, dconv, dmwb, dm, db_gate, dln_g, dln_b, db_dw, dps = _mixers_bwd_rows(
        dh1b, yc, yp, z, small["b_gate"], cpre, small["ln_g"], small["ln_b"], mw, small["pool_scale"],
        w_o_b, w_co_b, w_po_b, w_pool_b, sc_ffn[3], nb)
    grads_mix = {
        "w_conv_out": _wgrad(ca, dycb, D, D, tk, "wgrad_conv_out"),
        "w_pool": _wgrad(m, dmwb, GD, GD, tk, "wgrad_pool", diag=True),
        "w_pool_out": _wgrad(m2b, dypb, D, D, tk, "wgrad_pool_out"),
        "w_o": _wgrad(mg, dh1b, D, D, tk, "wgrad_o"),
    }
    halves_mix, sc_mix = reduce_start(GROUP_MIX, grads_mix, "mix")
    dzb, grad_x, dhead, dw_dw, dg_mix = _mixers_bwd_halo(
        dconv, dm, z, dzg, small["w_dw"], head, xs, small["g_mix"], dh1, gw["w_in"], sc_mix[3], nb, t_total)
    packed = jnp.concatenate(
        [dg_mix, db_gate.reshape(2, D), db_dw, dln_g, dln_b, dps, dg_ffn, dg_final,
         jnp.broadcast_to(loss, (1, D)), jnp.zeros((6, D), F32), dhead[PAD:], dw_dw], axis=0)
    sm = _small_start(packed, "small_start")
    grads_in = {"w_in": _wgrad(u, dzb, D, D_IN // N_SHARD, tk, "wgrad_in", col_major=True, dep=sm[3])}
    halves_in, sc_in = reduce_start(("w_in",), grads_in, "in")

    land = _small_wait(sm[0], sm[1], sm[2], [sc_in[3]], "small_wait")
    summed = _sum_slots(land, packed, jnp.reshape(4 * cx + 2 * cy + cc, (1,)).astype(jnp.int32))
    loss = summed[9, 0]

    first = GROUP_FFN + GROUP_MIX
    reduced_half = reduce_finish(GROUP_FFN, halves_ffn, sc_ffn, [summed], "ffn")
    reduced_half += reduce_finish(GROUP_MIX, halves_mix, sc_mix, [summed], "mix")
    reduced = dict(zip(first, _join_halves(reduced_half, "join_halves_first")))
    updates = {}
    for ks in _by_shape([reduced[n] for n in first]):
        names = [first[k] for k in ks]
        done = _adamw([(reduced[n], _shard2d(n, w[n]), _shard2d(n, mom[n]), _shard2d(n, var[n])) for n in names],
                      "adamw_" + names[0])
        updates.update(zip(names, done))

    def repl_stack(d):
        return jnp.concatenate([d["g_mix"], d["b_gate"].reshape(2, D), d["b_dw"], d["ln_g"], d["ln_b"],
                                d["pool_scale"], d["g_ffn"], d["g_final"].reshape(1, D), jnp.ones((7, D), F32)], axis=0)

    def shard_stack(d):
        return jnp.concatenate([d["meta_tokens"], d["w_dw"][0], jnp.ones((1, GD), F32)], axis=0)

    g_repl = summed[0:16]
    g_shard = lax.dynamic_slice_in_dim(summed[16:64], chip * GD, GD, axis=1)
    g_repl, d_repl, m_repl, v_repl = _adamw([(g_repl, repl_stack(w), repl_stack(mom), repl_stack(var))], "adamw_repl")[0]
    g_shard, d_shard, m_shard, v_shard = _adamw(
        [(g_shard, shard_stack(w), shard_stack(mom), shard_stack(var))], "adamw_cols")[0]

    done_first = [updates[n][1] for n in first] + [d_repl, d_shard]
    last_half = reduce_finish(("w_in",), halves_in, sc_in, done_first, "in")
    reduced["w_in"] = _join_halves(last_half, "join_halves_in")[0]
    updates["w_in"] = _adamw([(reduced["w_in"], w["w_in"][0], mom["w_in"][0], var["w_in"][0])], "adamw_w_in")[0]

    def unpack(name, repl, shard):
        if name == "meta_tokens":
            return shard[0:N_META]
        if name == "w_dw":
            return shard[N_META:N_META + KW].reshape(1, KW, GD)
        row = {"g_mix": 0, "b_gate": 1, "b_dw": 3, "ln_g": 4, "ln_b": 5, "pool_scale": 6, "g_ffn": 7, "g_final": 8}[name]
        if name == "b_gate":
            return repl[1:3].reshape(1, 2 * D)
        if name == "g_final":
            return repl[8]
        return repl[row:row + 1]

    out_g, out_d, out_m, out_v = {}, {}, {}, {}
    for n in WEIGHT_ORDER:
        if n in BIG:
            g, d_, m_, v_ = updates[n]
            shape = w[n].shape
            out_g[n], out_d[n], out_m[n], out_v[n] = (_unshard2d(n, a, shape) for a in (g, d_, m_, v_))
        else:
            out_g[n] = unpack(n, g_repl, g_shard)
            out_d[n] = unpack(n, d_repl, d_shard)
            out_m[n] = unpack(n, m_repl, m_shard)
            out_v[n] = unpack(n, v_repl, v_shard)
    return (loss, grad_x[None], *[out_g[n] for n in WEIGHT_ORDER], *[out_d[n] for n in WEIGHT_ORDER],
            *[out_m[n] for n in WEIGHT_ORDER], *[out_v[n] for n in WEIGHT_ORDER])
```

```python
import jax
import jax.numpy as jnp
from jax import lax
from jax.experimental import pallas as pl
from jax.experimental.pallas import tpu as pltpu

F32 = jnp.float32
BF16 = jnp.bfloat16
MESH = pl.DeviceIdType.MESH

D = 1024
N_META = 16
KW = 31
POOL_WINDOWS = (2, 4, 8, 16)
GD = 256
D_IN = 5 * D
D_FF = 2816
N_SHARD = 4
BR = 256
HALO = 16
PAD = BR - N_META
EXT = BR + 2 * HALO
RMS_EPS = 1e-6
LN_EPS = 1e-5
LR, B1, B2, ADAM_EPS, WD, STEP = 0.001, 0.9, 0.999, 1e-08, 0.01, 10
VMEM_LIMIT = 56 * 1024 * 1024


def _cparams(sem, vmem=VMEM_LIMIT):
    return pltpu.CompilerParams(dimension_semantics=sem, vmem_limit_bytes=vmem)


def _dot(a, b):
    return jnp.dot(a, b, preferred_element_type=F32)


def _dot_nt(a, b):
    return lax.dot_general(a, b, (((1,), (1,)), ((), ())), preferred_element_type=F32)


def _dot_tn(a, b):
    return lax.dot_general(a, b, (((0,), (0,)), ((), ())), preferred_element_type=F32)


def _sigmoid(x):
    return 0.5 * jnp.tanh(0.5 * x) + 0.5


def _row_ids(i, n, offset=0):
    return lax.broadcasted_iota(jnp.int32, (n, 1), 0) + (i * BR + offset - PAD)


def _pool_cnt(t, w, t_total):
    left = w // 2
    right = w - 1 - left
    lo = jnp.clip(t - left, 0, t_total)
    hi = jnp.clip(t + right + 1, 0, t_total)
    return jnp.maximum(hi - lo, 1).astype(F32)


def _halo_specs(nb, halo_width=D, width=D):
    last = nb * (BR // HALO) - 1
    return [
        pl.BlockSpec((HALO, halo_width), lambda i: (jnp.maximum(i * (BR // HALO) - 1, 0), 0)),
        pl.BlockSpec((BR, width), lambda i: (i, 0)),
        pl.BlockSpec((HALO, halo_width), lambda i: (jnp.minimum((i + 1) * (BR // HALO), last), 0)),
    ]


def _cols(ref, n):
    return [ref.at[:, k * D:(k + 1) * D] for k in range(n)]


def _fill_ext(ext_ref, prev, cur, nxt, i, nb):
    ext_ref[0:HALO, :] = jnp.where(i > 0, prev, 0.0)
    ext_ref[HALO:HALO + BR, :] = cur
    ext_ref[HALO + BR:EXT, :] = jnp.where(i < nb - 1, nxt, 0.0)


ROT_ROWS = EXT - 8


def _fill_rot(rot_ref, ext_ref, lanes):
    for r in range(1, 8):
        rot_ref[r] = ext_ref[pl.ds(r, ROT_ROWS), lanes]


def _tap(rot_ref, ext_ref, lanes, offset):
    q, r = divmod(offset, 8)
    if r == 0:
        return ext_ref[pl.ds(8 * q, BR), lanes]
    return rot_ref[r, pl.ds(8 * q, BR), :]


def _row_spec(width=D):
    return pl.BlockSpec((BR, width), lambda i: (i, 0))


def _x_spec():
    return pl.BlockSpec((BR, D), lambda i: (jnp.maximum(i - 1, 0), 0))


def _const_spec(shape):
    nd = len(shape)
    return pl.BlockSpec(shape, lambda i: (0,) * nd)


def _rms_u(head, x, g_mix, nb):
    def body(head_ref, x_ref, g_ref, u_ref):
        i = pl.program_id(0)
        h = jnp.where(i == 0, head_ref[...], x_ref[...])
        r = lax.rsqrt(jnp.mean(h * h, axis=-1, keepdims=True) + RMS_EPS)
        u_ref[...] = ((h * r) * g_ref[...]).astype(BF16)

    return pl.pallas_call(
        body, name="rms_u",
        grid=(nb,),
        in_specs=[_const_spec((BR, D)), _x_spec(), _const_spec((1, D))],
        out_specs=_row_spec(),
        out_shape=jax.ShapeDtypeStruct((nb * BR, D), BF16),
        compiler_params=_cparams(("arbitrary",)),
    )(head, x, g_mix)


def _in_proj_rows(tp):
    return tp // 4 if (tp // 4) % 16 == 0 else BR


def _in_proj_own(u, w_own, idx, nb):
    tp = nb * BR
    wcols = w_own.shape[1]
    rows = _in_proj_rows(tp)

    def body(idx_ref, u_ref, w_ref, z_ref, wb_ref):
        @pl.when(pl.program_id(0) == 0)
        def _():
            wb_ref[...] = w_ref[...].astype(BF16)

        z_ref[...] = _dot(u_ref[...], wb_ref[...])

    return pl.pallas_call(
        body, name="in_proj_own",
        grid_spec=pltpu.PrefetchScalarGridSpec(
            num_scalar_prefetch=1, grid=(tp // rows,),
            in_specs=[pl.BlockSpec((rows, D), lambda i, idx_ref: (i, 0)),
                      pl.BlockSpec((D, wcols), lambda i, idx_ref: (0, 0))],
            out_specs=pl.BlockSpec((rows, wcols), lambda i, idx_ref: (i, idx_ref[0])),
            scratch_shapes=[pltpu.VMEM((D, wcols), BF16)]),
        out_shape=jax.ShapeDtypeStruct((tp, N_SHARD * wcols), F32),
        compiler_params=_cparams(("arbitrary",)),
    )(idx, u, w_own)


def _in_proj_rest(u, w_in_b, z, idx, dep, nb):
    tp = nb * BR
    wcols = w_in_b.shape[2]
    rows = _in_proj_rows(tp)

    def body(idx_ref, u_ref, w_ref, z_in, dep_ref, z_ref):
        z_ref[...] = _dot(u_ref[...], w_ref[...])

    any_spec = pl.BlockSpec(memory_space=pl.ANY)
    return pl.pallas_call(
        body, name="in_proj_rest",
        grid_spec=pltpu.PrefetchScalarGridSpec(
            num_scalar_prefetch=1, grid=(N_SHARD - 1, tp // rows),
            in_specs=[pl.BlockSpec((rows, D), lambda s, i, idx_ref: (i, 0)),
                      pl.BlockSpec((None, D, wcols), lambda s, i, idx_ref: (idx_ref[1 + s], 0, 0)),
                      any_spec, any_spec],
            out_specs=pl.BlockSpec((rows, wcols), lambda s, i, idx_ref: (i, idx_ref[1 + s]))),
        out_shape=jax.ShapeDtypeStruct(z.shape, F32),
        input_output_aliases={3: 0},
        compiler_params=_cparams(("arbitrary", "arbitrary")),
    )(idx, u, w_in_b, z, dep)


def _mixers_fwd(z, head, x, b_gate, w_dw, b_dw, ln_g, ln_b, pool_scale, w_co, w_pool, w_po, w_o, nb, t_total):
    tp = nb * BR

    def body(z_prev, z_cur, z_next, head_ref, x_ref, bg_ref, wdw_ref, bdw_ref,
             lng_ref, lnb_ref, ps_ref, wco_ref, wpool_ref, wpo_ref, wo_ref,
             h1_ref, yc_ref, yp_ref, mg_ref, ca_ref, cpre_ref, m_ref, mw_ref, m2b_ref, ext_ref, pext_ref, rot_ref):
        i = pl.program_id(0)
        avp, agp, pp = _cols(z_prev, 3)
        av, ag, pc, za, zb = _cols(z_cur, 5)
        avn, agn, pn = _cols(z_next, 3)
        _fill_ext(ext_ref, avp[...] * _sigmoid(agp[...]), av[...] * _sigmoid(ag[...]),
                  avn[...] * _sigmoid(agn[...]), i, nb)
        _fill_ext(pext_ref, pp[...], pc[...], pn[...], i, nb)

        def conv_chunk(c, carry):
            lanes = pl.ds(pl.multiple_of(c * 128, 128), 128)
            _fill_rot(rot_ref, ext_ref, lanes)
            acc = jnp.broadcast_to(bdw_ref[:, lanes], (BR, 128))
            for k in range(KW):
                acc = acc + wdw_ref[k:k + 1, lanes] * _tap(rot_ref, ext_ref, lanes, 1 + k)
            cpre_ref[:, lanes] = acc
            return carry
        lax.fori_loop(0, D // 128, conv_chunk, 0)

        conv = cpre_ref[...]
        mu = jnp.mean(conv, axis=-1, keepdims=True)
        xc = conv - mu
        rstd = lax.rsqrt(jnp.mean(xc * xc, axis=-1, keepdims=True) + LN_EPS)
        ln = (xc * rstd) * lng_ref[...] + lnb_ref[...]
        cact = (ln * _sigmoid(ln)).astype(BF16)
        ca_ref[...] = cact
        y_conv = _dot(cact, wco_ref[...])
        yc_ref[...] = y_conv

        t = _row_ids(i, BR)
        for gi, w in enumerate(POOL_WINDOWS):
            left = w // 2
            right = w - 1 - left
            lanes = slice(gi * GD, (gi + 1) * GD)
            s = pext_ref[pl.ds(HALO - left, BR), lanes]
            for j in range(-left + 1, right + 1):
                s = s + pext_ref[pl.ds(HALO + j, BR), lanes]
            m = (s / _pool_cnt(t, w, t_total) - pext_ref[HALO:HALO + BR, lanes]).astype(BF16)
            m_ref[:, lanes] = m
            mw_ref[:, lanes] = _dot(m, wpool_ref[gi])
        mw = mw_ref[...]
        m2b = (mw * ps_ref[...]).astype(BF16)
        m2b_ref[...] = m2b
        y_pool = _dot(m2b, wpo_ref[...])
        yp_ref[...] = y_pool

        s_a = _sigmoid(za[...] + bg_ref[:, 0:D])
        s_b = _sigmoid(zb[...] + bg_ref[:, D:2 * D])
        merged = (s_a * y_conv + s_b * y_pool).astype(BF16)
        mg_ref[...] = merged
        h0 = jnp.where(i == 0, head_ref[...], x_ref[...])
        h1_ref[...] = h0 + _dot(merged, wo_ref[...])

    in_specs = (_halo_specs(nb, 3 * D, 5 * D)
                + [_const_spec((BR, D)), _x_spec(), _const_spec((1, 2 * D)), _const_spec((32, D)),
                   _const_spec((1, D)), _const_spec((1, D)), _const_spec((1, D)), _const_spec((1, D)),
                   _const_spec((D, D)), _const_spec((4, GD, GD)), _const_spec((D, D)), _const_spec((D, D))])
    outs = [(F32, "h1"), (F32, "yc"), (F32, "yp"), (BF16, "mg"), (BF16, "ca"), (F32, "cpre"), (BF16, "m"), (F32, "mw"),
            (BF16, "m2b")]
    return pl.pallas_call(
        body, name="mixers_fwd",
        grid=(nb,),
        in_specs=in_specs,
        out_specs=[_row_spec() for _ in outs],
        out_shape=[jax.ShapeDtypeStruct((tp, D), dt) for dt, _ in outs],
        scratch_shapes=[pltpu.VMEM((EXT, D), F32), pltpu.VMEM((EXT, D), F32), pltpu.VMEM((8, ROT_ROWS, 128), F32)],
        compiler_params=_cparams(("arbitrary",)),
    )(z, z, z, head, x, b_gate, w_dw, b_dw, ln_g, ln_b, pool_scale, w_co, w_pool, w_po, w_o)


def _ffn_fwd_bwd(h1, target, g_ffn, g_final, w_g, w_u, w_d, nb):
    tp = nb * BR

    def body(h1_ref, tgt_ref, gf_ref, gfin_ref, wg_hbm, wu_hbm, wd_hbm,
             dh1_ref, dh1b_ref, vb_ref, fb_ref, dgb_ref, dub_ref, dh2b_ref, loss_ref, dgf_ref, dgfin_ref,
             wg_ref, wu_ref, wd_ref, sem):
        i = pl.program_id(0)

        @pl.when(i == 0)
        def _():
            copies = [pltpu.make_async_copy(wg_hbm, wg_ref, sem.at[0]),
                      pltpu.make_async_copy(wu_hbm, wu_ref, sem.at[1]),
                      pltpu.make_async_copy(wd_hbm, wd_ref, sem.at[2])]
            for cp in copies:
                cp.start()
            loss_ref[...] = jnp.zeros_like(loss_ref)
            dgf_ref[...] = jnp.zeros_like(dgf_ref)
            dgfin_ref[...] = jnp.zeros_like(dgfin_ref)
            for cp in copies:
                cp.wait()

        h1 = h1_ref[...]
        r1 = lax.rsqrt(jnp.mean(h1 * h1, axis=-1, keepdims=True) + RMS_EPS)
        vn = h1 * r1
        vb = (vn * gf_ref[...]).astype(BF16)
        vb_ref[...] = vb
        g = _dot_nt(vb, wg_ref[...])
        up = _dot_nt(vb, wu_ref[...])
        sg = _sigmoid(g)
        sl = g * sg
        fb = (sl * up).astype(BF16)
        fb_ref[...] = fb
        h2 = h1 + _dot(fb, wd_ref[...])
        r2 = lax.rsqrt(jnp.mean(h2 * h2, axis=-1, keepdims=True) + RMS_EPS)
        yn = h2 * r2
        valid = i > 0
        diff = jnp.where(valid, yn * gfin_ref[...] - tgt_ref[...], 0.0)
        loss_ref[...] += 0.5 * jnp.sum(jnp.mean(diff * diff, axis=-1, keepdims=True))
        dy = diff * (1.0 / D)
        dgfin_ref[...] += jnp.sum(dy * yn, axis=0, keepdims=True)
        gd = dy * gfin_ref[...]
        dh2 = r2 * (gd - yn * jnp.mean(yn * gd, axis=-1, keepdims=True))
        dh2b = dh2.astype(BF16)
        dh2b_ref[...] = dh2b
        df = _dot_nt(dh2b, wd_ref[...])
        dub = (df * sl).astype(BF16)
        dgb = (df * up * (sg * (1.0 + g * (1.0 - sg)))).astype(BF16)
        dub_ref[...] = dub
        dgb_ref[...] = dgb
        dv = _dot(dgb, wg_ref[...]) + _dot(dub, wu_ref[...])
        dgf_ref[...] += jnp.sum(dv * vn, axis=0, keepdims=True)
        gd1 = dv * gf_ref[...]
        dh1 = dh2 + r1 * (gd1 - vn * jnp.mean(vn * gd1, axis=-1, keepdims=True))
        dh1_ref[...] = dh1
        dh1b_ref[...] = dh1.astype(BF16)

    any_spec = pl.BlockSpec(memory_space=pl.ANY)
    return pl.pallas_call(
        body, name="ffn_fwd_bwd",
        grid=(nb,),
        in_specs=[_row_spec(), _x_spec(), _const_spec((1, D)), _const_spec((1, D)), any_spec, any_spec, any_spec],
        out_specs=[_row_spec(), _row_spec(), _row_spec(), _row_spec(D_FF), _row_spec(D_FF), _row_spec(D_FF), _row_spec(),
                   _const_spec((1, 1)), _const_spec((1, D)), _const_spec((1, D))],
        out_shape=[jax.ShapeDtypeStruct((tp, D), F32), jax.ShapeDtypeStruct((tp, D), BF16),
                   jax.ShapeDtypeStruct((tp, D), BF16), jax.ShapeDtypeStruct((tp, D_FF), BF16),
                   jax.ShapeDtypeStruct((tp, D_FF), BF16), jax.ShapeDtypeStruct((tp, D_FF), BF16),
                   jax.ShapeDtypeStruct((tp, D), BF16), jax.ShapeDtypeStruct((1, 1), F32),
                   jax.ShapeDtypeStruct((1, D), F32), jax.ShapeDtypeStruct((1, D), F32)],
        scratch_shapes=[pltpu.VMEM((D_FF, D), BF16), pltpu.VMEM((D_FF, D), BF16), pltpu.VMEM((D_FF, D), BF16),
                        pltpu.SemaphoreType.DMA((3,))],
        compiler_params=_cparams(("arbitrary",)),
    )(h1, target, g_ffn, g_final, w_g, w_u, w_d)


def _mixers_bwd_rows(dh1b, yc, yp, z, b_gate, cpre, ln_g, ln_b, mw, pool_scale, w_o, w_co, w_po, w_pool, dep, nb):
    tp = nb * BR

    def body(dh1b_ref, yc_ref, yp_ref, za, zb, bg_ref, cpre_ref, lng_ref, lnb_ref, mw_ref, ps_ref,
             wo_ref, wco_ref, wpo_ref, wpool_ref, dep_ref,
             dycb_ref, dypb_ref, dzg_ref, dconv_ref, dmwb_ref, dm_ref, dbg_ref, dlng_ref, dlnb_ref, dbdw_ref, dps_ref):
        i = pl.program_id(0)

        @pl.when(i == 0)
        def _():
            for r in (dbg_ref, dlng_ref, dlnb_ref, dbdw_ref, dps_ref):
                r[...] = jnp.zeros_like(r)

        dmg = _dot_nt(dh1b_ref[...], wo_ref[...])
        s_a = _sigmoid(za[...] + bg_ref[:, 0:D])
        s_b = _sigmoid(zb[...] + bg_ref[:, D:2 * D])
        dycb = (dmg * s_a).astype(BF16)
        dypb = (dmg * s_b).astype(BF16)
        dycb_ref[...] = dycb
        dypb_ref[...] = dypb
        dza = dmg * yc_ref[...] * (s_a * (1.0 - s_a))
        dzb = dmg * yp_ref[...] * (s_b * (1.0 - s_b))
        dzg_ref[:, 0:D] = dza.astype(BF16)
        dzg_ref[:, D:2 * D] = dzb.astype(BF16)
        dbg_ref[:, 0:D] += jnp.sum(dza, axis=0, keepdims=True)
        dbg_ref[:, D:2 * D] += jnp.sum(dzb, axis=0, keepdims=True)

        dca = _dot_nt(dycb, wco_ref[...])
        conv = cpre_ref[...]
        mu = jnp.mean(conv, axis=-1, keepdims=True)
        xc = conv - mu
        rstd = lax.rsqrt(jnp.mean(xc * xc, axis=-1, keepdims=True) + LN_EPS)
        xhat = xc * rstd
        ln = xhat * lng_ref[...] + lnb_ref[...]
        sg = _sigmoid(ln)
        dln = dca * (sg * (1.0 + ln * (1.0 - sg)))
        dlng_ref[...] += jnp.sum(dln * xhat, axis=0, keepdims=True)
        dlnb_ref[...] += jnp.sum(dln, axis=0, keepdims=True)
        dxh = dln * lng_ref[...]
        dconv = rstd * (dxh - jnp.mean(dxh, axis=-1, keepdims=True)
                        - xhat * jnp.mean(dxh * xhat, axis=-1, keepdims=True))
        dconv_ref[...] = dconv
        dbdw_ref[...] += jnp.sum(dconv, axis=0, keepdims=True)

        dm2 = _dot_nt(dypb, wpo_ref[...])
        dps_ref[...] += jnp.sum(dm2 * mw_ref[...], axis=0, keepdims=True)
        dmwb = (dm2 * ps_ref[...]).astype(BF16)
        dmwb_ref[...] = dmwb
        for gi in range(len(POOL_WINDOWS)):
            lanes = slice(gi * GD, (gi + 1) * GD)
            dm_ref[:, lanes] = _dot_nt(dmwb[:, lanes], wpool_ref[gi])

    in_specs = [_row_spec(), _row_spec(), _row_spec(),
                pl.BlockSpec((BR, D), lambda i: (i, 3)), pl.BlockSpec((BR, D), lambda i: (i, 4)),
                _const_spec((1, 2 * D)), _row_spec(), _const_spec((1, D)), _const_spec((1, D)), _row_spec(),
                _const_spec((1, D)), _const_spec((D, D)), _const_spec((D, D)), _const_spec((D, D)),
                _const_spec((4, GD, GD)), pl.BlockSpec(memory_space=pl.ANY)]
    return pl.pallas_call(
        body, name="mixers_bwd_rows",
        grid=(nb,),
        in_specs=in_specs,
        out_specs=[_row_spec(), _row_spec(), _row_spec(2 * D), _row_spec(), _row_spec(), _row_spec(),
                   _const_spec((1, 2 * D)), _const_spec((1, D)), _const_spec((1, D)), _const_spec((1, D)),
                   _const_spec((1, D))],
        out_shape=[jax.ShapeDtypeStruct((tp, D), BF16), jax.ShapeDtypeStruct((tp, D), BF16),
                   jax.ShapeDtypeStruct((tp, 2 * D), BF16), jax.ShapeDtypeStruct((tp, D), F32),
                   jax.ShapeDtypeStruct((tp, D), BF16), jax.ShapeDtypeStruct((tp, D), F32),
                   jax.ShapeDtypeStruct((1, 2 * D), F32), jax.ShapeDtypeStruct((1, D), F32),
                   jax.ShapeDtypeStruct((1, D), F32), jax.ShapeDtypeStruct((1, D), F32),
                   jax.ShapeDtypeStruct((1, D), F32)],
        compiler_params=_cparams(("arbitrary",)),
    )(dh1b, yc, yp, z, z, b_gate, cpre, ln_g, ln_b, mw, pool_scale, w_o, w_co, w_po, w_pool, dep)


def _mixers_bwd_halo(dconv, dm, z, dzg, w_dw, head, x, g_mix, dh1, w_in_b, dep, nb, t_total):
    tp = nb * BR
    ns = w_in_b.shape[0]
    wcols = w_in_b.shape[2]
    seq = x.shape[0]

    def body(dcp, dcc, dcn, dmp, dmc, dmn, z_prev, z_cur, z_next, dzg_ref, wdw_ref, head_ref, x_ref, g_ref,
             dh1_ref, w_hbm, dep_ref,
             dzb_ref, gx_ref, dhead_ref, dwdw_ref, dgmix_ref,
             w_ref, sem, aext_ref, dext_ref, qext_ref, da_ref, rot_ref, dwp_ref):
        i = pl.program_id(0)
        (avp, agp), (av, ag), (avn, agn) = _cols(z_prev, 2), _cols(z_cur, 2), _cols(z_next, 2)

        @pl.when(i == 0)
        def _():
            cp = pltpu.make_async_copy(w_hbm, w_ref, sem.at[0])
            cp.start()
            dwp_ref[...] = jnp.zeros_like(dwp_ref)
            dgmix_ref[...] = jnp.zeros_like(dgmix_ref)
            cp.wait()

        sig_g = _sigmoid(ag[...])
        _fill_ext(aext_ref, avp[...] * _sigmoid(agp[...]), av[...] * sig_g, avn[...] * _sigmoid(agn[...]), i, nb)
        _fill_ext(dext_ref, dcp[...], dcc[...], dcn[...], i, nb)
        _fill_ext(qext_ref, dmp[...], dmc[...], dmn[...], i, nb)

        def conv_chunk(c, carry):
            lanes = pl.ds(pl.multiple_of(c * 128, 128), 128)
            _fill_rot(rot_ref, dext_ref, lanes)
            acc = jnp.zeros((BR, 128), F32)
            for k in range(KW):
                acc = acc + wdw_ref[k:k + 1, lanes] * _tap(rot_ref, dext_ref, lanes, KW - k)
            da_ref[:, lanes] = acc
            _fill_rot(rot_ref, aext_ref, lanes)
            dcv = dext_ref[HALO:HALO + BR, lanes]
            for k in range(KW):
                prod = _tap(rot_ref, aext_ref, lanes, 1 + k) * dcv
                dwp_ref[k, :, lanes] += jnp.sum(prod.reshape(BR // 8, 8, 128), axis=0)
            return carry
        lax.fori_loop(0, D // 128, conv_chunk, 0)

        @pl.when(i == nb - 1)
        def _():
            dwdw_ref[...] = jnp.sum(dwp_ref[...], axis=1)

        da = da_ref[...]
        a_val = av[...]
        dzb_ref[:, 0:D] = (da * sig_g).astype(BF16)
        dzb_ref[:, D:2 * D] = (da * a_val * (sig_g * (1.0 - sig_g))).astype(BF16)

        t_ext = _row_ids(i, EXT, -HALO)
        for gi, w in enumerate(POOL_WINDOWS):
            left = w // 2
            right = w - 1 - left
            lanes = slice(gi * GD, (gi + 1) * GD)
            qext_ref[:, lanes] = qext_ref[:, lanes] / _pool_cnt(t_ext, w, t_total)
            s = qext_ref[pl.ds(HALO - right, BR), lanes]
            for j in range(-right + 1, left + 1):
                s = s + qext_ref[pl.ds(HALO + j, BR), lanes]
            dzb_ref[:, 2 * D + gi * GD:2 * D + (gi + 1) * GD] = (s - dmc[:, lanes]).astype(BF16)
        dzb_ref[:, 3 * D:5 * D] = dzg_ref[...]

        du = _dot_nt(dzb_ref[:, 0:wcols], w_ref[0])
        for s_i in range(1, ns):
            du = du + _dot_nt(dzb_ref[:, s_i * wcols:(s_i + 1) * wcols], w_ref[s_i])
        h0 = jnp.where(i == 0, head_ref[...], x_ref[...])
        r0 = lax.rsqrt(jnp.mean(h0 * h0, axis=-1, keepdims=True) + RMS_EPS)
        un = h0 * r0
        dgmix_ref[...] += jnp.sum(du * un, axis=0, keepdims=True)
        gd = du * g_ref[...]
        dh0 = dh1_ref[...] + r0 * (gd - un * jnp.mean(un * gd, axis=-1, keepdims=True))
        gx_ref[...] = dh0

        @pl.when(i == 0)
        def _():
            dhead_ref[...] = dh0

    any_spec = pl.BlockSpec(memory_space=pl.ANY)
    in_specs = (_halo_specs(nb) + _halo_specs(nb) + _halo_specs(nb, 2 * D, 2 * D)
                + [_row_spec(2 * D), _const_spec((32, D)), _const_spec((BR, D)), _x_spec(), _const_spec((1, D)),
                   _row_spec(), any_spec, any_spec])
    return pl.pallas_call(
        body, name="mixers_bwd_halo",
        grid=(nb,),
        in_specs=in_specs,
        out_specs=[_row_spec(D_IN), _x_spec(), _const_spec((BR, D)), _const_spec((32, D)), _const_spec((1, D))],
        out_shape=[jax.ShapeDtypeStruct((tp, D_IN), BF16), jax.ShapeDtypeStruct((seq, D), F32),
                   jax.ShapeDtypeStruct((BR, D), F32), jax.ShapeDtypeStruct((32, D), F32),
                   jax.ShapeDtypeStruct((1, D), F32)],
        scratch_shapes=[pltpu.VMEM((ns, D, wcols), BF16), pltpu.SemaphoreType.DMA((1,)),
                        pltpu.VMEM((EXT, D), F32), pltpu.VMEM((EXT, D), F32), pltpu.VMEM((EXT, D), F32),
                        pltpu.VMEM((BR, D), F32), pltpu.VMEM((8, ROT_ROWS, 128), F32), pltpu.VMEM((32, 8, D), F32)],
        compiler_params=_cparams(("arbitrary",)),
    )(dconv, dconv, dconv, dm, dm, dm, z, z, z, dzg, w_dw, head, x, g_mix, dh1, w_in_b, dep)


def _wgrad(a, c, tm, tn, tk, name, diag=False, col_major=False, dep=None):
    tp, m = a.shape
    n = c.shape[1]
    nk = tp // tk
    gm, gn = m // tm, n // tn

    def body(a_ref, c_ref, *rest):
        o_ref, ob_ref = rest[-2:]
        k = pl.program_id(2)

        @pl.when(k == 0)
        def _():
            o_ref[...] = jnp.zeros_like(o_ref)

        o_ref[...] += _dot_tn(a_ref[...], c_ref[...])

        @pl.when(k == nk - 1)
        def _():
            ob_ref[...] = o_ref[...].astype(BF16)

    c_map = lambda i, j, k: (k, j)
    grid = (gm, gn, nk)
    deps = [] if dep is None else [dep]
    if diag:
        grid = (gm, 1, nk)
        c_map = lambda i, j, k: (k, i)
        o_spec = pl.BlockSpec((tm, tn), lambda i, j, k: (i, 0))
        o_shape = (m, tn)
    elif col_major:
        o_spec = pl.BlockSpec((None, tm, tn), lambda i, j, k: (j, i, 0))
        o_shape = (gn, m, tn)
    else:
        o_spec = pl.BlockSpec((tm, tn), lambda i, j, k: (i, j))
        o_shape = (m, n)
    return pl.pallas_call(
        body, name=name,
        grid=grid,
        in_specs=[pl.BlockSpec((tk, tm), lambda i, j, k: (k, i)), pl.BlockSpec((tk, tn), c_map)]
        + [pl.BlockSpec(memory_space=pl.ANY)] * len(deps),
        out_specs=[o_spec, o_spec],
        out_shape=[jax.ShapeDtypeStruct(o_shape, F32), jax.ShapeDtypeStruct(o_shape, BF16)],
        compiler_params=_cparams(("arbitrary", "arbitrary", "arbitrary")),
    )(a, c, *deps)


def _place():
    x, y, c = lax.axis_index("x"), lax.axis_index("y"), lax.axis_index("c")
    others = [(1 - x, y), (x, 1 - y), (1 - x, 1 - y)]
    return x, y, c, others


def _split2(a, axis=0):
    return a.reshape(a.shape[:axis] + (2, a.shape[axis] // 2) + a.shape[axis + 1:])


def _merge2(a, axis=0):
    return a.reshape(a.shape[:axis] + (2 * a.shape[axis + 1],) + a.shape[axis + 2:])


def _cast_into_slot(shards, chip, dep, name):
    n = len(shards)
    r, c = shards[0].shape
    r2 = r // 2

    def body(chip_ref, *refs):
        for a in range(n):
            refs[n + 1 + a][...] = refs[a][...].astype(BF16)

    out = pl.pallas_call(
        body, name=name,
        grid_spec=pltpu.PrefetchScalarGridSpec(
            num_scalar_prefetch=1, grid=(2,),
            in_specs=[pl.BlockSpec((r2, c), lambda h, chip_ref: (h, 0))] * n + [pl.BlockSpec(memory_space=pl.ANY)],
            out_specs=[pl.BlockSpec((None, None, r2, c), lambda h, chip_ref: (chip_ref[0], h, 0, 0))] * n),
        out_shape=[jax.ShapeDtypeStruct((N_SHARD, 2, r2, c), BF16)] * n,
        compiler_params=_cparams(("arbitrary",)),
    )(chip, *shards, dep)
    return list(out)


HBM_SPEC = pl.BlockSpec(memory_space=pltpu.HBM)
SEM_SPEC = pl.BlockSpec(memory_space=pltpu.SEMAPHORE)
DATAFLOW = pltpu.SideEffectType.DATAFLOW_SIDE_EFFECTING
TOKEN = jax.ShapeDtypeStruct((8, 128), F32)


def _in_hbm(a):
    return pltpu.with_memory_space_constraint(a, pltpu.HBM)


def _gather_tiny(v):
    vm = pl.BlockSpec(memory_space=pltpu.VMEM)

    def body(v_ref, out_ref, send_sems, recv_sems):
        x, y, c, others = _place()
        mine = 2 * x + y
        sends = [pltpu.make_async_remote_copy(
            src_ref=v_ref, dst_ref=out_ref.at[mine], send_sem=send_sems.at[j], recv_sem=recv_sems.at[j],
            device_id=(*chip, c), device_id_type=MESH) for j, chip in enumerate(others)]
        for cp in sends:
            cp.start()
        out_ref[mine] = v_ref[...]
        for j, chip in enumerate(others):
            landed = out_ref.at[2 * chip[0] + chip[1]]
            pltpu.make_async_remote_copy(
                src_ref=landed, dst_ref=landed, send_sem=send_sems.at[j], recv_sem=recv_sems.at[j],
                device_id=(x, y, c), device_id_type=MESH).wait_recv()
        for cp in sends:
            cp.wait_send()

    return pl.pallas_call(
        body, name="gather_tiny",
        in_specs=[vm], out_specs=vm,
        out_shape=jax.ShapeDtypeStruct((N_SHARD,) + v.shape, v.dtype),
        scratch_shapes=[pltpu.SemaphoreType.DMA((3,)), pltpu.SemaphoreType.DMA((3,))],
    )(v)


def _ici_copies(srcs, dsts, send_sems, recv_sems, started):
    x, y, c, others = _place()
    mine = 2 * x + y
    copies = []
    for a in range(len(srcs)):
        for j, chip in enumerate(others):
            there = 2 * chip[0] + chip[1]
            src, dst = srcs[a](mine, there, c), dsts[a](mine, there, c)
            if not started:
                dst = dsts[a](there, mine, c)
            copies.append(pltpu.make_async_remote_copy(
                src_ref=src, dst_ref=dst, send_sem=send_sems.at[a * 3 + j], recv_sem=recv_sems.at[a * 3 + j],
                device_id=(*chip, c), device_id_type=MESH))
    return copies


def _split_start(srcs_of, dsts_of, arrays, n_src, name, copies_of=None, n_sems=None, dep=None):
    n = len(arrays)
    n_sems = n_sems or 3 * n_src
    copies_of = copies_of or (lambda ins, ss, rs, started: _ici_copies(srcs_of(ins), dsts_of(ins), ss, rs, started))
    deps = [] if dep is None else [dep]
    nd = len(deps)

    def body(*refs):
        ins = refs[:n]
        send_sems, recv_sems = refs[n + nd], refs[n + nd + 1]
        token = refs[2 * n + nd + 2]
        for cp in copies_of(ins, send_sems, recv_sems, True):
            cp.start()
        token[...] = jnp.zeros_like(token)

    out = pl.pallas_call(
        body, name=name,
        in_specs=[HBM_SPEC] * n + [pl.BlockSpec(memory_space=pl.ANY)] * nd,
        out_specs=(SEM_SPEC, SEM_SPEC, *([HBM_SPEC] * n), pl.BlockSpec(memory_space=pltpu.VMEM)),
        out_shape=(pltpu.SemaphoreType.DMA((n_sems,)), pltpu.SemaphoreType.DMA((n_sems,)),
                   *[pltpu.HBM(a.shape, a.dtype) for a in arrays], TOKEN),
        input_output_aliases={a: 2 + a for a in range(n)},
        compiler_params=pltpu.CompilerParams(has_side_effects=DATAFLOW),
    )(*[_in_hbm(a) for a in arrays], *deps)
    return out[0], out[1], list(out[2:2 + n]), out[2 + n]


def _split_wait(srcs_of, dsts_of, send_sems, recv_sems, arrays, after, name, copies_of=None):
    n = len(arrays)
    copies_of = copies_of or (lambda ins, ss, rs, started: _ici_copies(srcs_of(ins), dsts_of(ins), ss, rs, started))

    def body(*refs):
        ins = refs[:n]
        send_sems, recv_sems = refs[n], refs[n + 1]
        for cp in copies_of(ins, send_sems, recv_sems, False):
            cp.wait_send()
            cp.wait_recv()

    return pl.pallas_call(
        body, name=name,
        in_specs=[HBM_SPEC] * n + [SEM_SPEC, SEM_SPEC] + [pl.BlockSpec(memory_space=pl.ANY)] * len(after),
        out_specs=[HBM_SPEC] * n,
        out_shape=[pltpu.HBM(a.shape, a.dtype) for a in arrays],
        input_output_aliases={a: a for a in range(n)},
        compiler_params=pltpu.CompilerParams(has_side_effects=DATAFLOW),
    )(*arrays, send_sems, recv_sems, *after)


def _gather_views(ins):
    view = [lambda frm, to, c, r=r: r.at[frm, c] for r in ins]
    return view


def _gather_start(bufs, dep, name):
    return _split_start(_gather_views, _gather_views, bufs, len(bufs), name, dep=dep)


def _gather_wait(send_sems, recv_sems, bufs, after, name):
    return _split_wait(_gather_views, _gather_views, send_sems, recv_sems, bufs, after, name)


def _forward_halves(bufs, name):
    n = len(bufs)
    any_spec = pl.BlockSpec(memory_space=pl.ANY)

    def body(*refs):
        outs = refs[n:2 * n]
        send_sems, recv_sems = refs[2 * n:]
        x, y, c, others = _place()
        copies = []
        for a in range(n):
            for j, chip in enumerate(others):
                landed = outs[a].at[2 * chip[0] + chip[1], c]
                copies.append(pltpu.make_async_remote_copy(
                    src_ref=landed, dst_ref=landed, send_sem=send_sems.at[a * 3 + j], recv_sem=recv_sems.at[a * 3 + j],
                    device_id=(x, y, 1 - c), device_id_type=MESH))
        for cp in copies:
            cp.start()
        for a in range(n):
            for j, chip in enumerate(others):
                landed = outs[a].at[2 * chip[0] + chip[1], 1 - c]
                pltpu.make_async_remote_copy(
                    src_ref=landed, dst_ref=landed, send_sem=send_sems.at[a * 3 + j], recv_sem=recv_sems.at[a * 3 + j],
                    device_id=(x, y, c), device_id_type=MESH).wait_recv()
        for cp in copies:
            cp.wait_send()

    out = pl.pallas_call(
        body, name=name,
        in_specs=[any_spec] * n, out_specs=[any_spec] * n,
        out_shape=[jax.ShapeDtypeStruct(b.shape, b.dtype) for b in bufs],
        input_output_aliases={a: a for a in range(n)},
        scratch_shapes=[pltpu.SemaphoreType.DMA((3 * n,)), pltpu.SemaphoreType.DMA((3 * n,))],
    )(*bufs)
    return [_merge2(o, 1) for o in out]


def _swap_halves_bf16(gbs, name):
    n = len(gbs)
    any_spec = pl.BlockSpec(memory_space=pl.ANY)

    def body(*refs):
        ins, outs = refs[:n], refs[n:2 * n]
        send_sems, recv_sems = refs[2 * n:]
        x, y, c, _ = _place()
        copies = []
        for a in range(n):
            copies.append(pltpu.make_async_remote_copy(
                src_ref=ins[a].at[:, 1 - c], dst_ref=outs[a], send_sem=send_sems.at[a], recv_sem=recv_sems.at[a],
                device_id=(x, y, 1 - c), device_id_type=MESH))
        for cp in copies:
            cp.start()
        for cp in copies:
            cp.wait()

    return pl.pallas_call(
        body, name=name,
        in_specs=[any_spec] * n, out_specs=[any_spec] * n,
        out_shape=[jax.ShapeDtypeStruct((g.shape[0], g.shape[1] // 2, g.shape[2]), g.dtype) for g in gbs],
        scratch_shapes=[pltpu.SemaphoreType.DMA((n,)), pltpu.SemaphoreType.DMA((n,))],
    )(*[_split2(g, 1) for g in gbs])


def _scatter_srcs(n):
    return lambda ins: [lambda frm, to, c, r=r: r.at[to] for r in ins[:n]]


def _scatter_dsts(n):
    return lambda ins: [lambda frm, to, c, r=r: r.at[frm] for r in ins[n:]]


def _scatter_start(hbs, name):
    n = len(hbs)
    lands = [lax.empty(h.shape, h.dtype) for h in hbs]
    return _split_start(_scatter_srcs(n), _scatter_dsts(n), list(hbs) + lands, n, name)


def _scatter_wait(send_sems, recv_sems, arrays, after, name):
    n = len(arrays) // 2
    return _split_wait(_scatter_srcs(n), _scatter_dsts(n), send_sems, recv_sems, arrays, after, name)[n:]


def _join_halves(rhs, name):
    n = len(rhs)
    any_spec = pl.BlockSpec(memory_space=pl.ANY)

    def body(*refs):
        outs = refs[n:2 * n]
        send_sems, recv_sems = refs[2 * n:]
        x, y, c, _ = _place()
        copies = []
        for a in range(n):
            copies.append(pltpu.make_async_remote_copy(
                src_ref=outs[a].at[c], dst_ref=outs[a].at[c], send_sem=send_sems.at[a],
                recv_sem=recv_sems.at[a], device_id=(x, y, 1 - c), device_id_type=MESH))
        for cp in copies:
            cp.start()
        for a in range(n):
            landed = outs[a].at[1 - c]
            pltpu.make_async_remote_copy(
                src_ref=landed, dst_ref=landed, send_sem=send_sems.at[a], recv_sem=recv_sems.at[a],
                device_id=(x, y, c), device_id_type=MESH).wait_recv()
        for cp in copies:
            cp.wait_send()

    out = pl.pallas_call(
        body, name=name,
        in_specs=[any_spec] * n, out_specs=[any_spec] * n,
        out_shape=[jax.ShapeDtypeStruct(r.shape, r.dtype) for r in rhs],
        input_output_aliases={a: a for a in range(n)},
        scratch_shapes=[pltpu.SemaphoreType.DMA((n,)), pltpu.SemaphoreType.DMA((n,))],
    )(*rhs)
    return [_merge2(o) for o in out]


FLIPS = [(dx, dy, dc) for dx in (0, 1) for dy in (0, 1) for dc in (0, 1)][1:]


def _peer_copies(ins, send_sems, recv_sems, started):
    x, y, c, _ = _place()
    copies = []
    for k, (dx, dy, dc) in enumerate(FLIPS):
        px, py, pc = jnp.bitwise_xor(x, dx), jnp.bitwise_xor(y, dy), jnp.bitwise_xor(c, dc)
        slot = 4 * x + 2 * y + c if started else 4 * px + 2 * py + pc
        copies.append(pltpu.make_async_remote_copy(
            src_ref=ins[0], dst_ref=ins[1].at[slot], send_sem=send_sems.at[k], recv_sem=recv_sems.at[k],
            device_id=(px, py, pc), device_id_type=MESH))
    return copies


def _small_start(v, name):
    land = lax.empty((8,) + v.shape, v.dtype)
    return _split_start(None, None, [v, land], 0, name, copies_of=_peer_copies, n_sems=len(FLIPS))


def _small_wait(send_sems, recv_sems, arrays, after, name):
    return _split_wait(None, None, send_sems, recv_sems, arrays, after, name, copies_of=_peer_copies)[1]


def _sum_slots(land, v, me):
    rows, cols = v.shape

    def body(me_ref, land_ref, v_ref, o_ref):
        o_ref[...] = jnp.zeros_like(o_ref)
        for d in range(8):
            @pl.when(me_ref[0] == d)
            def _():
                o_ref[...] += v_ref[...]

            @pl.when(me_ref[0] != d)
            def _():
                o_ref[...] += land_ref[d]

    return pl.pallas_call(
        body, name="sum_slots",
        grid_spec=pltpu.PrefetchScalarGridSpec(
            num_scalar_prefetch=1, grid=(1,),
            in_specs=[pl.BlockSpec((8, rows, cols), lambda i, me_ref: (0, 0, 0)),
                      pl.BlockSpec((rows, cols), lambda i, me_ref: (0, 0))],
            out_specs=pl.BlockSpec((rows, cols), lambda i, me_ref: (0, 0))),
        out_shape=jax.ShapeDtypeStruct((rows, cols), F32),
        compiler_params=_cparams(("arbitrary",)),
    )(me, land, v)


def _by_shape(arrays):
    groups = {}
    for k, a in enumerate(arrays):
        groups.setdefault(a.shape, []).append(k)
    return list(groups.values())


def _add_sibling_half(gs, sbs, idx, name):
    n = len(gs)
    ns, r, c = gs[0].shape
    r2 = r // 2

    def body(idx_ref, *refs):
        for a in range(n):
            g_ref, sb_ref, hown_ref, hb_ref = refs[a], refs[n + a], refs[2 * n + a], refs[3 * n + a]
            h = g_ref[...] + sb_ref[...].astype(F32)
            hb_ref[...] = h.astype(BF16)

            @pl.when(pl.program_id(0) == idx_ref[0])
            def _():
                hown_ref[...] = h

    spec = pl.BlockSpec((None, r2, c), lambda s, idx_ref: (s, 0, 0))
    out = pl.pallas_call(
        body, name=name,
        grid_spec=pltpu.PrefetchScalarGridSpec(
            num_scalar_prefetch=1, grid=(ns,),
            in_specs=[pl.BlockSpec((None, r2, c), lambda s, idx_ref: (s, idx_ref[4], 0))] * n + [spec] * n,
            out_specs=[pl.BlockSpec((r2, c), lambda s, idx_ref: (0, 0))] * n + [spec] * n),
        out_shape=[jax.ShapeDtypeStruct((r2, c), F32)] * n + [jax.ShapeDtypeStruct((ns, r2, c), BF16)] * n,
        compiler_params=_cparams(("arbitrary",)),
    )(idx, *gs, *sbs)
    return [(out[a], out[n + a]) for a in range(n)]


def _add_chip_slabs(hs, rbs, idx, name):
    n = len(hs)
    r2, c = hs[0].shape

    def body(idx_ref, *refs):
        for a in range(n):
            h_ref, r0_ref, r1_ref, r2_ref = refs[4 * a:4 * a + 4]
            refs[4 * n + a][...] = ((h_ref[...] + r0_ref[...].astype(F32)) + r1_ref[...].astype(F32)) + r2_ref[...].astype(F32)

    def pick(k):
        return pl.BlockSpec((None, r2, c), lambda i, idx_ref: (idx_ref[k], 0, 0))

    operands = []
    for h, rb in zip(hs, rbs):
        operands += [h, rb, rb, rb]
    out = pl.pallas_call(
        body, name=name,
        grid_spec=pltpu.PrefetchScalarGridSpec(
            num_scalar_prefetch=1, grid=(1,),
            in_specs=[pl.BlockSpec((r2, c), lambda i, idx_ref: (0, 0)), pick(1), pick(2), pick(3)] * n,
            out_specs=[pl.BlockSpec((None, r2, c), lambda i, idx_ref: (idx_ref[4], 0, 0))] * n),
        out_shape=[jax.ShapeDtypeStruct((2, r2, c), F32)] * n,
        compiler_params=_cparams(("arbitrary",)),
    )(idx, *operands)
    return list(out)


ELEMENTWISE_VMEM = 16 * 1024 * 1024


def _adamw(items, name):
    n = len(items)
    r, c = items[0][0].shape
    br = max(b for b in range(8, r + 1, 8) if r % b == 0 and n * 16 * b * c * 4 <= ELEMENTWISE_VMEM) if r % 8 == 0 else r

    def body(*refs):
        for a in range(n):
            g_ref, w_ref, m_ref, v_ref = refs[4 * a:4 * a + 4]
            go_ref, d_ref, nm_ref, nv_ref = refs[4 * n + 4 * a:4 * n + 4 * a + 4]
            gg = g_ref[...]
            go_ref[...] = gg
            nm = B1 * m_ref[...] + (1.0 - B1) * gg
            nv = B2 * v_ref[...] + (1.0 - B2) * jnp.square(gg)
            m_hat = nm / (1.0 - B1 ** STEP)
            v_hat = nv / (1.0 - B2 ** STEP)
            d_ref[...] = -LR * (m_hat / (jnp.sqrt(v_hat) + ADAM_EPS) + WD * w_ref[...])
            nm_ref[...] = nm
            nv_ref[...] = nv

    spec = pl.BlockSpec((br, c), lambda i: (i, 0))
    out = pl.pallas_call(
        body, name=name,
        grid=(r // br,),
        in_specs=[spec] * (4 * n), out_specs=[spec] * (4 * n),
        out_shape=[jax.ShapeDtypeStruct((r, c), F32)] * (4 * n),
        compiler_params=_cparams(("arbitrary",)),
    )(*[a for item in items for a in item])
    return [tuple(out[4 * a:4 * a + 4]) for a in range(n)]


BIG = ("w_in", "w_conv_out", "w_pool", "w_pool_out", "w_o", "w_ffn_gate", "w_ffn_up", "w_ffn_down")
REPL = ("g_mix", "b_gate", "b_dw", "ln_g", "ln_b", "pool_scale", "g_ffn", "g_final")
GROUP_MIX = ("w_conv_out", "w_pool", "w_pool_out", "w_o")
GROUP_FFN = ("w_ffn_gate", "w_ffn_up", "w_ffn_down")
TRANSPOSED = ("w_ffn_gate", "w_ffn_up")
WEIGHT_ORDER = ("meta_tokens", "g_mix", "w_in", "b_gate", "w_dw", "b_dw", "ln_g", "ln_b", "w_conv_out", "w_pool",
                "pool_scale", "w_pool_out", "w_o", "g_ffn", "w_ffn_gate", "w_ffn_up", "w_ffn_down", "g_final")


def _shard2d(name, a):
    a = a[0]
    if name == "w_pool":
        return a.reshape(4 * 64, GD)
    if name in TRANSPOSED:
        return a.T
    return a


def _unshard2d(name, a, shape):
    return a.T.reshape(shape) if name in TRANSPOSED else a.reshape(shape)


def _slabs_to_cols(a):
    ns, m, c = a.shape
    return a.transpose(1, 0, 2).reshape(m, ns * c)


def kernel(x, meta_tokens, g_mix, w_in, b_gate, w_dw, b_dw, ln_g, ln_b, w_conv_out, w_pool, pool_scale, w_pool_out, w_o, g_ffn, w_ffn_gate, w_ffn_up, w_ffn_down, g_final, loss_target, m_meta_tokens, m_g_mix, m_w_in, m_b_gate, m_w_dw, m_b_dw, m_ln_g, m_ln_b, m_w_conv_out, m_w_pool, m_pool_scale, m_w_pool_out, m_w_o, m_g_ffn, m_w_ffn_gate, m_w_ffn_up, m_w_ffn_down, m_g_final, v_meta_tokens, v_g_mix, v_w_in, v_b_gate, v_w_dw, v_b_dw, v_ln_g, v_ln_b, v_w_conv_out, v_w_pool, v_pool_scale, v_w_pool_out, v_w_o, v_g_ffn, v_w_ffn_gate, v_w_ffn_up, v_w_ffn_down, v_g_final):
    args = dict(locals())
    w = {n: args[n] for n in WEIGHT_ORDER}
    mom = {n: args["m_" + n] for n in WEIGHT_ORDER}
    var = {n: args["v_" + n] for n in WEIGHT_ORDER}
    seq = x.shape[1]
    nb = seq // BR + 1
    tp = nb * BR
    tk = tp // 2 if (tp // 2) % 16 == 0 else BR
    t_total = seq + N_META
    cx, cy, cc = lax.axis_index("x"), lax.axis_index("y"), lax.axis_index("c")
    chip = 2 * cx + cy
    chip1 = jnp.reshape(chip, (1,)).astype(jnp.int32)
    core = jnp.reshape(cc, (1,)).astype(jnp.int32)
    others = jnp.sort(jnp.stack([2 * (1 - cx) + cy, 2 * cx + (1 - cy), 2 * (1 - cx) + (1 - cy)]))
    idx = jnp.concatenate([chip1, others.astype(jnp.int32), core])
    xs, target = x[0], loss_target[0]

    tiny = _gather_tiny(jnp.concatenate([w["meta_tokens"], w["w_dw"][0], jnp.zeros((1, GD), F32)], axis=0))
    small = {n: w[n] for n in REPL if n != "g_final"}
    small["g_final"] = w["g_final"].reshape(1, D)
    small["w_dw"] = _slabs_to_cols(tiny[:, N_META:])
    head = jnp.concatenate([jnp.zeros((PAD, D), F32), _slabs_to_cols(tiny[:, :N_META])], axis=0)

    def cast(group, dep):
        shards = [_shard2d(n, w[n]) for n in group]
        bufs = [None] * len(group)
        for ks in _by_shape(shards):
            done = _cast_into_slot([shards[k] for k in ks], chip1, dep, "cast_" + group[ks[0]])
            for k, b in zip(ks, done):
                bufs[k] = b
        return bufs

    def gather_finish(group, start, after, name):
        landed = _gather_wait(start[0], start[1], start[2], after, "gather_wait_" + name)
        return dict(zip(group, _forward_halves(landed, "forward_" + name)))

    st_in = _gather_start(cast(("w_in",), tiny), None, "gather_start_in")
    bufs_mix, bufs_ffn = cast(GROUP_MIX, st_in[3]), cast(GROUP_FFN, st_in[3])
    u = _rms_u(head, xs, small["g_mix"] + st_in[3][0:1, 0:1], nb)
    z_own = _in_proj_own(u, w["w_in"][0], idx, nb)
    gw = gather_finish(("w_in",), st_in, [z_own] + bufs_mix + bufs_ffn, "in")
    st_mix = _gather_start(bufs_mix, gw["w_in"], "gather_start_mix")
    z = _in_proj_rest(u, gw["w_in"], z_own, idx, st_mix[3], nb)
    gw.update(gather_finish(GROUP_MIX, st_mix, [z], "mix"))
    st_ffn = _gather_start(bufs_ffn, gw["w_o"], "gather_start_ffn")
    w_pool_b = gw["w_pool"].reshape(N_SHARD, 4, 64, GD).transpose(1, 0, 2, 3).reshape(4, GD, GD)
    w_co_b, w_po_b, w_o_b = (gw[n].reshape(D, D) for n in ("w_conv_out", "w_pool_out", "w_o"))
    h1, yc, yp, mg, ca, cpre, m, mw, m2b = _mixers_fwd(
        z, head, xs, small["b_gate"] + st_ffn[3][0, 0], small["w_dw"], small["b_dw"], small["ln_g"], small["ln_b"],
        small["pool_scale"], w_co_b, w_pool_b, w_po_b, w_o_b, nb, t_total)
    gw.update(gather_finish(GROUP_FFN, st_ffn, [h1], "ffn"))

    dh1, dh1b, vb, fb, dgb, dub, dh2b, loss, dg_ffn, dg_final = _ffn_fwd_bwd(
        h1, target, small["g_ffn"], small["g_final"], gw["w_ffn_gate"].reshape(D_FF, D),
        gw["w_ffn_up"].reshape(D_FF, D), gw["w_ffn_down"].reshape(D_FF, D), nb)

    def slabs(name, g):
        if name == "w_in":
            return g
        if name == "w_pool":
            return g.reshape(4, N_SHARD, 64, GD).transpose(1, 0, 2, 3).reshape(N_SHARD, 4 * 64, GD)
        return g.reshape(N_SHARD, g.shape[0] // N_SHARD, g.shape[1])

    def reduce_start(group, grads, name):
        g32 = [slabs(n, grads[n][0]) for n in group]
        g16 = [slabs(n, grads[n][1]) for n in group]
        from_sibling = _swap_halves_bf16(g16, "swap_halves_" + name)
        halves = [None] * len(group)
        for ks in _by_shape(g32):
            done = _add_sibling_half([g32[k] for k in ks], [from_sibling[k] for k in ks], idx, "add_sibling_" + group[ks[0]])
            for k, pair in zip(ks, done):
                halves[k] = pair
        return [h for h, _ in halves], _scatter_start([hb for _, hb in halves], "scatter_start_" + name)

    def reduce_finish(group, halves, start, after, name):
        from_chips = _scatter_wait(start[0], start[1], start[2], after, "scatter_wait_" + name)
        reduced = [None] * len(group)
        for ks in _by_shape(halves):
            done = _add_chip_slabs([halves[k] for k in ks], [from_chips[k] for k in ks], idx, "add_chips_" + group[ks[0]])
            for k, r in zip(ks, done):
                reduced[k] = r
        return reduced

    half_ff = D_FF // 2
    grads_ffn = {
        "w_ffn_gate": _wgrad(dgb, vb, half_ff, D, tk, "wgrad_ffn_gate"),
        "w_ffn_up": _wgrad(dub, vb, half_ff, D, tk, "wgrad_ffn_up"),
        "w_ffn_down": _wgrad(fb, dh2b, half_ff, D, tk, "wgrad_ffn_down"),
    }
    halves_ffn, sc_ffn = reduce_start(GROUP_FFN, grads_ffn, "ffn")

    dycb, dypb, dzg, dconv, dmwb, dm, db_gate, dln_g, dln_b, db_dw, dps = _mixers_bwd_rows(
        dh1b, yc, yp, z, small["b_gate"], cpre, small["ln_g"], small["ln_b"], mw, small["pool_scale"],
        w_o_b, w_co_b, w_po_b, w_pool_b, sc_ffn[3], nb)
    grads_mix = {
        "w_conv_out": _wgrad(ca, dycb, D, D, tp, "wgrad_conv_out"),
        "w_pool": _wgrad(m, dmwb, GD, GD, tp, "wgrad_pool", diag=True),
        "w_pool_out": _wgrad(m2b, dypb, D, D, tp, "wgrad_pool_out"),
        "w_o": _wgrad(mg, dh1b, D, D, tp, "wgrad_o"),
    }
    halves_mix, sc_mix = reduce_start(GROUP_MIX, grads_mix, "mix")
    dzb, grad_x, dhead, dw_dw, dg_mix = _mixers_bwd_halo(
        dconv, dm, z, dzg, small["w_dw"], head, xs, small["g_mix"], dh1, gw["w_in"], sc_mix[3], nb, t_total)
    packed = jnp.concatenate(
        [dg_mix, db_gate.reshape(2, D), db_dw, dln_g, dln_b, dps, dg_ffn, dg_final,
         jnp.broadcast_to(loss, (1, D)), jnp.zeros((6, D), F32), dhead[PAD:], dw_dw], axis=0)
    sm = _small_start(packed, "small_start")
    grads_in = {"w_in": _wgrad(u, dzb, D, D_IN // N_SHARD, tk, "wgrad_in", col_major=True, dep=sm[3])}
    halves_in, sc_in = reduce_start(("w_in",), grads_in, "in")

    land = _small_wait(sm[0], sm[1], sm[2], [sc_in[3]], "small_wait")
    summed = _sum_slots(land, packed, jnp.reshape(4 * cx + 2 * cy + cc, (1,)).astype(jnp.int32))
    loss = summed[9, 0]

    first = GROUP_FFN + GROUP_MIX
    reduced_half = reduce_finish(GROUP_FFN, halves_ffn, sc_ffn, [summed], "ffn")
    reduced_half += reduce_finish(GROUP_MIX, halves_mix, sc_mix, [summed], "mix")
    reduced = dict(zip(first, _join_halves(reduced_half, "join_halves_first")))
    updates = {}
    for ks in _by_shape([reduced[n] for n in first]):
        names = [first[k] for k in ks]
        done = _adamw([(reduced[n], _shard2d(n, w[n]), _shard2d(n, mom[n]), _shard2d(n, var[n])) for n in names],
                      "adamw_" + names[0])
        updates.update(zip(names, done))

    def repl_stack(d):
        return jnp.concatenate([d["g_mix"], d["b_gate"].reshape(2, D), d["b_dw"], d["ln_g"], d["ln_b"],
                                d["pool_scale"], d["g_ffn"], d["g_final"].reshape(1, D), jnp.ones((7, D), F32)], axis=0)

    def shard_stack(d):
        return jnp.concatenate([d["meta_tokens"], d["w_dw"][0], jnp.ones((1, GD), F32)], axis=0)

    g_repl = summed[0:16]
    g_shard = lax.dynamic_slice_in_dim(summed[16:64], chip * GD, GD, axis=1)
    g_repl, d_repl, m_repl, v_repl = _adamw([(g_repl, repl_stack(w), repl_stack(mom), repl_stack(var))], "adamw_repl")[0]
    g_shard, d_shard, m_shard, v_shard = _adamw(
        [(g_shard, shard_stack(w), shard_stack(mom), shard_stack(var))], "adamw_cols")[0]

    done_first = [updates[n][1] for n in first] + [d_repl, d_shard]
    last_half = reduce_finish(("w_in",), halves_in, sc_in, done_first, "in")
    reduced["w_in"] = _join_halves(last_half, "join_halves_in")[0]
    updates["w_in"] = _adamw([(reduced["w_in"], w["w_in"][0], mom["w_in"][0], var["w_in"][0])], "adamw_w_in")[0]

    def unpack(name, repl, shard):
        if name == "meta_tokens":
            return shard[0:N_META]
        if name == "w_dw":
            return shard[N_META:N_META + KW].reshape(1, KW, GD)
        row = {"g_mix": 0, "b_gate": 1, "b_dw": 3, "ln_g": 4, "ln_b": 5, "pool_scale": 6, "g_ffn": 7, "g_final": 8}[name]
        if name == "b_gate":
            return repl[1:3].reshape(1, 2 * D)
        if name == "g_final":
            return repl[8]
        return repl[row:row + 1]

    out_g, out_d, out_m, out_v = {}, {}, {}, {}
    for n in WEIGHT_ORDER:
        if n in BIG:
            g, d_, m_, v_ = updates[n]
            shape = w[n].shape
            out_g[n], out_d[n], out_m[n], out_v[n] = (_unshard2d(n, a, shape) for a in (g, d_, m_, v_))
        else:
            out_g[n] = unpack(n, g_repl, g_shard)
            out_d[n] = unpack(n, d_repl, d_shard)
            out_m[n] = unpack(n, m_repl, m_shard)
            out_v[n] = unpack(n, v_repl, v_shard)
    return (loss, grad_x[None], *[out_g[n] for n in WEIGHT_ORDER], *[out_d[n] for n in WEIGHT_ORDER],
            *[out_m[n] for n in WEIGHT_ORDER], *[out_v[n] for n in WEIGHT_ORDER])
```

```python
import jax
import jax.numpy as jnp
from jax import lax
from jax.experimental import pallas as pl
from jax.experimental.pallas import tpu as pltpu

F32 = jnp.float32
BF16 = jnp.bfloat16
MESH = pl.DeviceIdType.MESH

D = 1024
N_META = 16
KW = 31
POOL_WINDOWS = (2, 4, 8, 16)
GD = 256
D_IN = 5 * D
D_FF = 2816
N_SHARD = 4
BR = 256
HALO = 16
PAD = BR - N_META
EXT = BR + 2 * HALO
RMS_EPS = 1e-6
LN_EPS = 1e-5
LR, B1, B2, ADAM_EPS, WD, STEP = 0.001, 0.9, 0.999, 1e-08, 0.01, 10
VMEM_LIMIT = 56 * 1024 * 1024


def _cparams(sem, vmem=VMEM_LIMIT):
    return pltpu.CompilerParams(dimension_semantics=sem, vmem_limit_bytes=vmem)


def _dot(a, b):
    return jnp.dot(a, b, preferred_element_type=F32)


def _dot_nt(a, b):
    return lax.dot_general(a, b, (((1,), (1,)), ((), ())), preferred_element_type=F32)


def _dot_tn(a, b):
    return lax.dot_general(a, b, (((0,), (0,)), ((), ())), preferred_element_type=F32)


def _sigmoid(x):
    return 0.5 * jnp.tanh(0.5 * x) + 0.5


def _row_ids(i, n, offset=0):
    return lax.broadcasted_iota(jnp.int32, (n, 1), 0) + (i * BR + offset - PAD)


def _pool_cnt(t, w, t_total):
    left = w // 2
    right = w - 1 - left
    lo = jnp.clip(t - left, 0, t_total)
    hi = jnp.clip(t + right + 1, 0, t_total)
    return jnp.maximum(hi - lo, 1).astype(F32)


def _halo_specs(nb, halo_width=D, width=D):
    last = nb * (BR // HALO) - 1
    return [
        pl.BlockSpec((HALO, halo_width), lambda i: (jnp.maximum(i * (BR // HALO) - 1, 0), 0)),
        pl.BlockSpec((BR, width), lambda i: (i, 0)),
        pl.BlockSpec((HALO, halo_width), lambda i: (jnp.minimum((i + 1) * (BR // HALO), last), 0)),
    ]


def _cols(ref, n):
    return [ref.at[:, k * D:(k + 1) * D] for k in range(n)]


def _fill_ext(ext_ref, prev, cur, nxt, i, nb):
    ext_ref[0:HALO, :] = jnp.where(i > 0, prev, 0.0)
    ext_ref[HALO:HALO + BR, :] = cur
    ext_ref[HALO + BR:EXT, :] = jnp.where(i < nb - 1, nxt, 0.0)


ROT_ROWS = EXT - 8


def _fill_rot(rot_ref, ext_ref, lanes):
    for r in range(1, 8):
        rot_ref[r] = ext_ref[pl.ds(r, ROT_ROWS), lanes]


def _tap(rot_ref, ext_ref, lanes, offset):
    q, r = divmod(offset, 8)
    if r == 0:
        return ext_ref[pl.ds(8 * q, BR), lanes]
    return rot_ref[r, pl.ds(8 * q, BR), :]


def _row_spec(width=D):
    return pl.BlockSpec((BR, width), lambda i: (i, 0))


def _x_spec():
    return pl.BlockSpec((BR, D), lambda i: (jnp.maximum(i - 1, 0), 0))


def _const_spec(shape):
    nd = len(shape)
    return pl.BlockSpec(shape, lambda i: (0,) * nd)


def _rms_u(head, x, g_mix, nb):
    def body(head_ref, x_ref, g_ref, u_ref):
        i = pl.program_id(0)
        h = jnp.where(i == 0, head_ref[...], x_ref[...])
        r = lax.rsqrt(jnp.mean(h * h, axis=-1, keepdims=True) + RMS_EPS)
        u_ref[...] = ((h * r) * g_ref[...]).astype(BF16)

    return pl.pallas_call(
        body, name="rms_u",
        grid=(nb,),
        in_specs=[_const_spec((BR, D)), _x_spec(), _const_spec((1, D))],
        out_specs=_row_spec(),
        out_shape=jax.ShapeDtypeStruct((nb * BR, D), BF16),
        compiler_params=_cparams(("arbitrary",)),
    )(head, x, g_mix)


def _in_proj_rows(tp):
    return tp // 4 if (tp // 4) % 16 == 0 else BR


def _in_proj_own(u, w_own, idx, nb):
    tp = nb * BR
    wcols = w_own.shape[1]
    rows = _in_proj_rows(tp)

    def body(idx_ref, u_ref, w_ref, z_ref, wb_ref):
        @pl.when(pl.program_id(0) == 0)
        def _():
            wb_ref[...] = w_ref[...].astype(BF16)

        z_ref[...] = _dot(u_ref[...], wb_ref[...])

    return pl.pallas_call(
        body, name="in_proj_own",
        grid_spec=pltpu.PrefetchScalarGridSpec(
            num_scalar_prefetch=1, grid=(tp // rows,),
            in_specs=[pl.BlockSpec((rows, D), lambda i, idx_ref: (i, 0)),
                      pl.BlockSpec((D, wcols), lambda i, idx_ref: (0, 0))],
            out_specs=pl.BlockSpec((rows, wcols), lambda i, idx_ref: (i, idx_ref[0])),
            scratch_shapes=[pltpu.VMEM((D, wcols), BF16)]),
        out_shape=jax.ShapeDtypeStruct((tp, N_SHARD * wcols), F32),
        compiler_params=_cparams(("arbitrary",)),
    )(idx, u, w_own)


def _in_proj_rest(u, w_in_b, z, idx, dep, nb):
    tp = nb * BR
    wcols = w_in_b.shape[2]
    rows = _in_proj_rows(tp)

    def body(idx_ref, u_ref, w_ref, z_in, dep_ref, z_ref):
        z_ref[...] = _dot(u_ref[...], w_ref[...])

    any_spec = pl.BlockSpec(memory_space=pl.ANY)
    return pl.pallas_call(
        body, name="in_proj_rest",
        grid_spec=pltpu.PrefetchScalarGridSpec(
            num_scalar_prefetch=1, grid=(N_SHARD - 1, tp // rows),
            in_specs=[pl.BlockSpec((rows, D), lambda s, i, idx_ref: (i, 0)),
                      pl.BlockSpec((None, D, wcols), lambda s, i, idx_ref: (idx_ref[1 + s], 0, 0)),
                      any_spec, any_spec],
            out_specs=pl.BlockSpec((rows, wcols), lambda s, i, idx_ref: (i, idx_ref[1 + s]))),
        out_shape=jax.ShapeDtypeStruct(z.shape, F32),
        input_output_aliases={3: 0},
        compiler_params=_cparams(("arbitrary", "arbitrary")),
    )(idx, u, w_in_b, z, dep)


def _mixers_fwd(z, head, x, b_gate, w_dw, b_dw, ln_g, ln_b, pool_scale, w_co, w_pool, w_po, w_o, nb, t_total):
    tp = nb * BR

    def body(z_prev, z_cur, z_next, head_ref, x_ref, bg_ref, wdw_ref, bdw_ref,
             lng_ref, lnb_ref, ps_ref, wco_ref, wpool_ref, wpo_ref, wo_ref,
             h1_ref, yc_ref, yp_ref, mg_ref, ca_ref, cpre_ref, m_ref, mw_ref, m2b_ref, ext_ref, pext_ref, rot_ref):
        i = pl.program_id(0)
        avp, agp, pp = _cols(z_prev, 3)
        av, ag, pc, za, zb = _cols(z_cur, 5)
        avn, agn, pn = _cols(z_next, 3)
        _fill_ext(ext_ref, avp[...] * _sigmoid(agp[...]), av[...] * _sigmoid(ag[...]),
                  avn[...] * _sigmoid(agn[...]), i, nb)
        _fill_ext(pext_ref, pp[...], pc[...], pn[...], i, nb)

        def conv_chunk(c, carry):
            lanes = pl.ds(pl.multiple_of(c * 128, 128), 128)
            _fill_rot(rot_ref, ext_ref, lanes)
            acc = jnp.broadcast_to(bdw_ref[:, lanes], (BR, 128))
            for k in range(KW):
                acc = acc + wdw_ref[k:k + 1, lanes] * _tap(rot_ref, ext_ref, lanes, 1 + k)
            cpre_ref[:, lanes] = acc
            return carry
        lax.fori_loop(0, D // 128, conv_chunk, 0)

        conv = cpre_ref[...]
        mu = jnp.mean(conv, axis=-1, keepdims=True)
        xc = conv - mu
        rstd = lax.rsqrt(jnp.mean(xc * xc, axis=-1, keepdims=True) + LN_EPS)
        ln = (xc * rstd) * lng_ref[...] + lnb_ref[...]
        cact = (ln * _sigmoid(ln)).astype(BF16)
        ca_ref[...] = cact
        y_conv = _dot(cact, wco_ref[...])
        yc_ref[...] = y_conv

        t = _row_ids(i, BR)
        for gi, w in enumerate(POOL_WINDOWS):
            left = w // 2
            right = w - 1 - left
            lanes = slice(gi * GD, (gi + 1) * GD)
            s = pext_ref[pl.ds(HALO - left, BR), lanes]
            for j in range(-left + 1, right + 1):
                s = s + pext_ref[pl.ds(HALO + j, BR), lanes]
            m = (s / _pool_cnt(t, w, t_total) - pext_ref[HALO:HALO + BR, lanes]).astype(BF16)
            m_ref[:, lanes] = m
            mw_ref[:, lanes] = _dot(m, wpool_ref[gi])
        mw = mw_ref[...]
        m2b = (mw * ps_ref[...]).astype(BF16)
        m2b_ref[...] = m2b
        y_pool = _dot(m2b, wpo_ref[...])
        yp_ref[...] = y_pool

        s_a = _sigmoid(za[...] + bg_ref[:, 0:D])
        s_b = _sigmoid(zb[...] + bg_ref[:, D:2 * D])
        merged = (s_a * y_conv + s_b * y_pool).astype(BF16)
        mg_ref[...] = merged
        h0 = jnp.where(i == 0, head_ref[...], x_ref[...])
        h1_ref[...] = h0 + _dot(merged, wo_ref[...])

    in_specs = (_halo_specs(nb, 3 * D, 5 * D)
                + [_const_spec((BR, D)), _x_spec(), _const_spec((1, 2 * D)), _const_spec((32, D)),
                   _const_spec((1, D)), _const_spec((1, D)), _const_spec((1, D)), _const_spec((1, D)),
                   _const_spec((D, D)), _const_spec((4, GD, GD)), _const_spec((D, D)), _const_spec((D, D))])
    outs = [(F32, "h1"), (F32, "yc"), (F32, "yp"), (BF16, "mg"), (BF16, "ca"), (F32, "cpre"), (BF16, "m"), (F32, "mw"),
            (BF16, "m2b")]
    return pl.pallas_call(
        body, name="mixers_fwd",
        grid=(nb,),
        in_specs=in_specs,
        out_specs=[_row_spec() for _ in outs],
        out_shape=[jax.ShapeDtypeStruct((tp, D), dt) for dt, _ in outs],
        scratch_shapes=[pltpu.VMEM((EXT, D), F32), pltpu.VMEM((EXT, D), F32), pltpu.VMEM((8, ROT_ROWS, 128), F32)],
        compiler_params=_cparams(("arbitrary",)),
    )(z, z, z, head, x, b_gate, w_dw, b_dw, ln_g, ln_b, pool_scale, w_co, w_pool, w_po, w_o)


def _ffn_fwd_bwd(h1, target, g_ffn, g_final, w_g, w_u, w_d, nb):
    tp = nb * BR

    def body(h1_ref, tgt_ref, gf_ref, gfin_ref, wg_hbm, wu_hbm, wd_hbm,
             dh1_ref, dh1b_ref, vb_ref, fb_ref, dgb_ref, dub_ref, dh2b_ref, loss_ref, dgf_ref, dgfin_ref,
             wg_ref, wu_ref, wd_ref, sem):
        i = pl.program_id(0)

        @pl.when(i == 0)
        def _():
            copies = [pltpu.make_async_copy(wg_hbm, wg_ref, sem.at[0]),
                      pltpu.make_async_copy(wu_hbm, wu_ref, sem.at[1]),
                      pltpu.make_async_copy(wd_hbm, wd_ref, sem.at[2])]
            for cp in copies:
                cp.start()
            loss_ref[...] = jnp.zeros_like(loss_ref)
            dgf_ref[...] = jnp.zeros_like(dgf_ref)
            dgfin_ref[...] = jnp.zeros_like(dgfin_ref)
            for cp in copies:
                cp.wait()

        h1 = h1_ref[...]
        r1 = lax.rsqrt(jnp.mean(h1 * h1, axis=-1, keepdims=True) + RMS_EPS)
        vn = h1 * r1
        vb = (vn * gf_ref[...]).astype(BF16)
        vb_ref[...] = vb
        g = _dot_nt(vb, wg_ref[...])
        up = _dot_nt(vb, wu_ref[...])
        sg = _sigmoid(g)
        sl = g * sg
        fb = (sl * up).astype(BF16)
        fb_ref[...] = fb
        h2 = h1 + _dot(fb, wd_ref[...])
        r2 = lax.rsqrt(jnp.mean(h2 * h2, axis=-1, keepdims=True) + RMS_EPS)
        yn = h2 * r2
        valid = i > 0
        diff = jnp.where(valid, yn * gfin_ref[...] - tgt_ref[...], 0.0)
        loss_ref[...] += 0.5 * jnp.sum(jnp.mean(diff * diff, axis=-1, keepdims=True))
        dy = diff * (1.0 / D)
        dgfin_ref[...] += jnp.sum(dy * yn, axis=0, keepdims=True)
        gd = dy * gfin_ref[...]
        dh2 = r2 * (gd - yn * jnp.mean(yn * gd, axis=-1, keepdims=True))
        dh2b = dh2.astype(BF16)
        dh2b_ref[...] = dh2b
        df = _dot_nt(dh2b, wd_ref[...])
        dub = (df * sl).astype(BF16)
        dgb = (df * up * (sg * (1.0 + g * (1.0 - sg)))).astype(BF16)
        dub_ref[...] = dub
        dgb_ref[...] = dgb
        dv = _dot(dgb, wg_ref[...]) + _dot(dub, wu_ref[...])
        dgf_ref[...] += jnp.sum(dv * vn, axis=0, keepdims=True)
        gd1 = dv * gf_ref[...]
        dh1 = dh2 + r1 * (gd1 - vn * jnp.mean(vn * gd1, axis=-1, keepdims=True))
        dh1_ref[...] = dh1
        dh1b_ref[...] = dh1.astype(BF16)

    any_spec = pl.BlockSpec(memory_space=pl.ANY)
    return pl.pallas_call(
        body, name="ffn_fwd_bwd",
        grid=(nb,),
        in_specs=[_row_spec(), _x_spec(), _const_spec((1, D)), _const_spec((1, D)), any_spec, any_spec, any_spec],
        out_specs=[_row_spec(), _row_spec(), _row_spec(), _row_spec(D_FF), _row_spec(D_FF), _row_spec(D_FF), _row_spec(),
                   _const_spec((1, 1)), _const_spec((1, D)), _const_spec((1, D))],
        out_shape=[jax.ShapeDtypeStruct((tp, D), F32), jax.ShapeDtypeStruct((tp, D), BF16),
                   jax.ShapeDtypeStruct((tp, D), BF16), jax.ShapeDtypeStruct((tp, D_FF), BF16),
                   jax.ShapeDtypeStruct((tp, D_FF), BF16), jax.ShapeDtypeStruct((tp, D_FF), BF16),
                   jax.ShapeDtypeStruct((tp, D), BF16), jax.ShapeDtypeStruct((1, 1), F32),
                   jax.ShapeDtypeStruct((1, D), F32), jax.ShapeDtypeStruct((1, D), F32)],
        scratch_shapes=[pltpu.VMEM((D_FF, D), BF16), pltpu.VMEM((D_FF, D), BF16), pltpu.VMEM((D_FF, D), BF16),
                        pltpu.SemaphoreType.DMA((3,))],
        compiler_params=_cparams(("arbitrary",)),
    )(h1, target, g_ffn, g_final, w_g, w_u, w_d)


def _mixers_bwd_rows(dh1b, yc, yp, z, b_gate, cpre, ln_g, ln_b, mw, pool_scale, w_o, w_co, w_po, w_pool, dep, nb):
    tp = nb * BR

    def body(dh1b_ref, yc_ref, yp_ref, za, zb, bg_ref, cpre_ref, lng_ref, lnb_ref, mw_ref, ps_ref,
             wo_ref, wco_ref, wpo_ref, wpool_ref, dep_ref,
             dycb_ref, dypb_ref, dzg_ref, dconv_ref, dmwb_ref, dm_ref, dbg_ref, dlng_ref, dlnb_ref, dbdw_ref, dps_ref):
        i = pl.program_id(0)

        @pl.when(i == 0)
        def _():
            for r in (dbg_ref, dlng_ref, dlnb_ref, dbdw_ref, dps_ref):
                r[...] = jnp.zeros_like(r)

        dmg = _dot_nt(dh1b_ref[...], wo_ref[...])
        s_a = _sigmoid(za[...] + bg_ref[:, 0:D])
        s_b = _sigmoid(zb[...] + bg_ref[:, D:2 * D])
        dycb = (dmg * s_a).astype(BF16)
        dypb = (dmg * s_b).astype(BF16)
        dycb_ref[...] = dycb
        dypb_ref[...] = dypb
        dza = dmg * yc_ref[...] * (s_a * (1.0 - s_a))
        dzb = dmg * yp_ref[...] * (s_b * (1.0 - s_b))
        dzg_ref[:, 0:D] = dza.astype(BF16)
        dzg_ref[:, D:2 * D] = dzb.astype(BF16)
        dbg_ref[:, 0:D] += jnp.sum(dza, axis=0, keepdims=True)
        dbg_ref[:, D:2 * D] += jnp.sum(dzb, axis=0, keepdims=True)

        dca = _dot_nt(dycb, wco_ref[...])
        conv = cpre_ref[...]
        mu = jnp.mean(conv, axis=-1, keepdims=True)
        xc = conv - mu
        rstd = lax.rsqrt(jnp.mean(xc * xc, axis=-1, keepdims=True) + LN_EPS)
        xhat = xc * rstd
        ln = xhat * lng_ref[...] + lnb_ref[...]
        sg = _sigmoid(ln)
        dln = dca * (sg * (1.0 + ln * (1.0 - sg)))
        dlng_ref[...] += jnp.sum(dln * xhat, axis=0, keepdims=True)
        dlnb_ref[...] += jnp.sum(dln, axis=0, keepdims=True)
        dxh = dln * lng_ref[...]
        dconv = rstd * (dxh - jnp.mean(dxh, axis=-1, keepdims=True)
                        - xhat * jnp.mean(dxh * xhat, axis=-1, keepdims=True))
        dconv_ref[...] = dconv
        dbdw_ref[...] += jnp.sum(dconv, axis=0, keepdims=True)

        dm2 = _dot_nt(dypb, wpo_ref[...])
        dps_ref[...] += jnp.sum(dm2 * mw_ref[...], axis=0, keepdims=True)
        dmwb = (dm2 * ps_ref[...]).astype(BF16)
        dmwb_ref[...] = dmwb
        for gi in range(len(POOL_WINDOWS)):
            lanes = slice(gi * GD, (gi + 1) * GD)
            dm_ref[:, lanes] = _dot_nt(dmwb[:, lanes], wpool_ref[gi])

    in_specs = [_row_spec(), _row_spec(), _row_spec(),
                pl.BlockSpec((BR, D), lambda i: (i, 3)), pl.BlockSpec((BR, D), lambda i: (i, 4)),
                _const_spec((1, 2 * D)), _row_spec(), _const_spec((1, D)), _const_spec((1, D)), _row_spec(),
                _const_spec((1, D)), _const_spec((D, D)), _const_spec((D, D)), _const_spec((D, D)),
                _const_spec((4, GD, GD)), pl.BlockSpec(memory_space=pl.ANY)]
    return pl.pallas_call(
        body, name="mixers_bwd_rows",
        grid=(nb,),
        in_specs=in_specs,
        out_specs=[_row_spec(), _row_spec(), _row_spec(2 * D), _row_spec(), _row_spec(), _row_spec(),
                   _const_spec((1, 2 * D)), _const_spec((1, D)), _const_spec((1, D)), _const_spec((1, D)),
                   _const_spec((1, D))],
        out_shape=[jax.ShapeDtypeStruct((tp, D), BF16), jax.ShapeDtypeStruct((tp, D), BF16),
                   jax.ShapeDtypeStruct((tp, 2 * D), BF16), jax.ShapeDtypeStruct((tp, D), F32),
                   jax.ShapeDtypeStruct((tp, D), BF16), jax.ShapeDtypeStruct((tp, D), F32),
                   jax.ShapeDtypeStruct((1, 2 * D), F32), jax.ShapeDtypeStruct((1, D), F32),
                   jax.ShapeDtypeStruct((1, D), F32), jax.ShapeDtypeStruct((1, D), F32),
                   jax.ShapeDtypeStruct((1, D), F32)],
        compiler_params=_cparams(("arbitrary",)),
    )(dh1b, yc, yp, z, z, b_gate, cpre, ln_g, ln_b, mw, pool_scale, w_o, w_co, w_po, w_pool, dep)


def _mixers_bwd_halo(dconv, dm, z, dzg, w_dw, head, x, g_mix, dh1, w_in_b, dep, nb, t_total):
    tp = nb * BR
    ns = w_in_b.shape[0]
    wcols = w_in_b.shape[2]
    seq = x.shape[0]

    def body(dcp, dcc, dcn, dmp, dmc, dmn, z_prev, z_cur, z_next, dzg_ref, wdw_ref, head_ref, x_ref, g_ref,
             dh1_ref, w_hbm, dep_ref,
             dzb_ref, gx_ref, dhead_ref, dwdw_ref, dgmix_ref,
             w_ref, sem, aext_ref, dext_ref, qext_ref, da_ref, rot_ref, dwp_ref):
        i = pl.program_id(0)
        (avp, agp), (av, ag), (avn, agn) = _cols(z_prev, 2), _cols(z_cur, 2), _cols(z_next, 2)

        @pl.when(i == 0)
        def _():
            cp = pltpu.make_async_copy(w_hbm, w_ref, sem.at[0])
            cp.start()
            dwp_ref[...] = jnp.zeros_like(dwp_ref)
            dgmix_ref[...] = jnp.zeros_like(dgmix_ref)
            cp.wait()

        sig_g = _sigmoid(ag[...])
        _fill_ext(aext_ref, avp[...] * _sigmoid(agp[...]), av[...] * sig_g, avn[...] * _sigmoid(agn[...]), i, nb)
        _fill_ext(dext_ref, dcp[...], dcc[...], dcn[...], i, nb)
        _fill_ext(qext_ref, dmp[...], dmc[...], dmn[...], i, nb)

        def conv_chunk(c, carry):
            lanes = pl.ds(pl.multiple_of(c * 128, 128), 128)
            _fill_rot(rot_ref, dext_ref, lanes)
            acc = jnp.zeros((BR, 128), F32)
            for k in range(KW):
                acc = acc + wdw_ref[k:k + 1, lanes] * _tap(rot_ref, dext_ref, lanes, KW - k)
            da_ref[:, lanes] = acc
            _fill_rot(rot_ref, aext_ref, lanes)
            dcv = dext_ref[HALO:HALO + BR, lanes]
            for k in range(KW):
                prod = _tap(rot_ref, aext_ref, lanes, 1 + k) * dcv
                dwp_ref[k, :, lanes] += jnp.sum(prod.reshape(BR // 8, 8, 128), axis=0)
            return carry
        lax.fori_loop(0, D // 128, conv_chunk, 0)

        @pl.when(i == nb - 1)
        def _():
            dwdw_ref[...] = jnp.sum(dwp_ref[...], axis=1)

        da = da_ref[...]
        a_val = av[...]
        dzb_ref[:, 0:D] = (da * sig_g).astype(BF16)
        dzb_ref[:, D:2 * D] = (da * a_val * (sig_g * (1.0 - sig_g))).astype(BF16)

        t_ext = _row_ids(i, EXT, -HALO)
        for gi, w in enumerate(POOL_WINDOWS):
            left = w // 2
            right = w - 1 - left
            lanes = slice(gi * GD, (gi + 1) * GD)
            qext_ref[:, lanes] = qext_ref[:, lanes] / _pool_cnt(t_ext, w, t_total)
            s = qext_ref[pl.ds(HALO - right, BR), lanes]
            for j in range(-right + 1, left + 1):
                s = s + qext_ref[pl.ds(HALO + j, BR), lanes]
            dzb_ref[:, 2 * D + gi * GD:2 * D + (gi + 1) * GD] = (s - dmc[:, lanes]).astype(BF16)
        dzb_ref[:, 3 * D:5 * D] = dzg_ref[...]

        du = _dot_nt(dzb_ref[:, 0:wcols], w_ref[0])
        for s_i in range(1, ns):
            du = du + _dot_nt(dzb_ref[:, s_i * wcols:(s_i + 1) * wcols], w_ref[s_i])
        h0 = jnp.where(i == 0, head_ref[...], x_ref[...])
        r0 = lax.rsqrt(jnp.mean(h0 * h0, axis=-1, keepdims=True) + RMS_EPS)
        un = h0 * r0
        dgmix_ref[...] += jnp.sum(du * un, axis=0, keepdims=True)
        gd = du * g_ref[...]
        dh0 = dh1_ref[...] + r0 * (gd - un * jnp.mean(un * gd, axis=-1, keepdims=True))
        gx_ref[...] = dh0

        @pl.when(i == 0)
        def _():
            dhead_ref[...] = dh0

    any_spec = pl.BlockSpec(memory_space=pl.ANY)
    in_specs = (_halo_specs(nb) + _halo_specs(nb) + _halo_specs(nb, 2 * D, 2 * D)
                + [_row_spec(2 * D), _const_spec((32, D)), _const_spec((BR, D)), _x_spec(), _const_spec((1, D)),
                   _row_spec(), any_spec, any_spec])
    return pl.pallas_call(
        body, name="mixers_bwd_halo",
        grid=(nb,),
        in_specs=in_specs,
        out_specs=[_row_spec(D_IN), _x_spec(), _const_spec((BR, D)), _const_spec((32, D)), _const_spec((1, D))],
        out_shape=[jax.ShapeDtypeStruct((tp, D_IN), BF16), jax.ShapeDtypeStruct((seq, D), F32),
                   jax.ShapeDtypeStruct((BR, D), F32), jax.ShapeDtypeStruct((32, D), F32),
                   jax.ShapeDtypeStruct((1, D), F32)],
        scratch_shapes=[pltpu.VMEM((ns, D, wcols), BF16), pltpu.SemaphoreType.DMA((1,)),
                        pltpu.VMEM((EXT, D), F32), pltpu.VMEM((EXT, D), F32), pltpu.VMEM((EXT, D), F32),
                        pltpu.VMEM((BR, D), F32), pltpu.VMEM((8, ROT_ROWS, 128), F32), pltpu.VMEM((32, 8, D), F32)],
        compiler_params=_cparams(("arbitrary",)),
    )(dconv, dconv, dconv, dm, dm, dm, z, z, z, dzg, w_dw, head, x, g_mix, dh1, w_in_b, dep)


def _wgrad(a, c, tm, tn, tk, name, diag=False, col_major=False, dep=None):
    tp, m = a.shape
    n = c.shape[1]
    nk = tp // tk
    gm, gn = m // tm, n // tn

    def body(a_ref, c_ref, *rest):
        o_ref, ob_ref = rest[-2:]
        k = pl.program_id(2)

        @pl.when(k == 0)
        def _():
            o_ref[...] = jnp.zeros_like(o_ref)

        o_ref[...] += _dot_tn(a_ref[...], c_ref[...])

        @pl.when(k == nk - 1)
        def _():
            ob_ref[...] = o_ref[...].astype(BF16)

    c_map = lambda i, j, k: (k, j)
    grid = (gm, gn, nk)
    deps = [] if dep is None else [dep]
    if diag:
        grid = (gm, 1, nk)
        c_map = lambda i, j, k: (k, i)
        o_spec = pl.BlockSpec((tm, tn), lambda i, j, k: (i, 0))
        o_shape = (m, tn)
    elif col_major:
        o_spec = pl.BlockSpec((None, tm, tn), lambda i, j, k: (j, i, 0))
        o_shape = (gn, m, tn)
    else:
        o_spec = pl.BlockSpec((tm, tn), lambda i, j, k: (i, j))
        o_shape = (m, n)
    return pl.pallas_call(
        body, name=name,
        grid=grid,
        in_specs=[pl.BlockSpec((tk, tm), lambda i, j, k: (k, i)), pl.BlockSpec((tk, tn), c_map)]
        + [pl.BlockSpec(memory_space=pl.ANY)] * len(deps),
        out_specs=[o_spec, o_spec],
        out_shape=[jax.ShapeDtypeStruct(o_shape, F32), jax.ShapeDtypeStruct(o_shape, BF16)],
        compiler_params=_cparams(("arbitrary", "arbitrary", "arbitrary")),
    )(a, c, *deps)


def _place():
    x, y, c = lax.axis_index("x"), lax.axis_index("y"), lax.axis_index("c")
    others = [(1 - x, y), (x, 1 - y), (1 - x, 1 - y)]
    return x, y, c, others


def _split2(a, axis=0):
    return a.reshape(a.shape[:axis] + (2, a.shape[axis] // 2) + a.shape[axis + 1:])


def _merge2(a, axis=0):
    return a.reshape(a.shape[:axis] + (2 * a.shape[axis + 1],) + a.shape[axis + 2:])


def _cast_into_slot(shards, chip, dep, name):
    n = len(shards)
    r, c = shards[0].shape
    r2 = r // 2

    def body(chip_ref, *refs):
        for a in range(n):
            refs[n + 1 + a][...] = refs[a][...].astype(BF16)

    out = pl.pallas_call(
        body, name=name,
        grid_spec=pltpu.PrefetchScalarGridSpec(
            num_scalar_prefetch=1, grid=(2,),
            in_specs=[pl.BlockSpec((r2, c), lambda h, chip_ref: (h, 0))] * n + [pl.BlockSpec(memory_space=pl.ANY)],
            out_specs=[pl.BlockSpec((None, None, r2, c), lambda h, chip_ref: (chip_ref[0], h, 0, 0))] * n),
        out_shape=[jax.ShapeDtypeStruct((N_SHARD, 2, r2, c), BF16)] * n,
        compiler_params=_cparams(("arbitrary",)),
    )(chip, *shards, dep)
    return list(out)


HBM_SPEC = pl.BlockSpec(memory_space=pltpu.HBM)
SEM_SPEC = pl.BlockSpec(memory_space=pltpu.SEMAPHORE)
DATAFLOW = pltpu.SideEffectType.DATAFLOW_SIDE_EFFECTING
TOKEN = jax.ShapeDtypeStruct((8, 128), F32)


def _in_hbm(a):
    return pltpu.with_memory_space_constraint(a, pltpu.HBM)


def _gather_tiny(v):
    vm = pl.BlockSpec(memory_space=pltpu.VMEM)

    def body(v_ref, out_ref, send_sems, recv_sems):
        x, y, c, others = _place()
        mine = 2 * x + y
        sends = [pltpu.make_async_remote_copy(
            src_ref=v_ref, dst_ref=out_ref.at[mine], send_sem=send_sems.at[j], recv_sem=recv_sems.at[j],
            device_id=(*chip, c), device_id_type=MESH) for j, chip in enumerate(others)]
        for cp in sends:
            cp.start()
        out_ref[mine] = v_ref[...]
        for j, chip in enumerate(others):
            landed = out_ref.at[2 * chip[0] + chip[1]]
            pltpu.make_async_remote_copy(
                src_ref=landed, dst_ref=landed, send_sem=send_sems.at[j], recv_sem=recv_sems.at[j],
                device_id=(x, y, c), device_id_type=MESH).wait_recv()
        for cp in sends:
            cp.wait_send()

    return pl.pallas_call(
        body, name="gather_tiny",
        in_specs=[vm], out_specs=vm,
        out_shape=jax.ShapeDtypeStruct((N_SHARD,) + v.shape, v.dtype),
        scratch_shapes=[pltpu.SemaphoreType.DMA((3,)), pltpu.SemaphoreType.DMA((3,))],
    )(v)


def _ici_copies(srcs, dsts, send_sems, recv_sems, started):
    x, y, c, others = _place()
    mine = 2 * x + y
    copies = []
    for a in range(len(srcs)):
        for j, chip in enumerate(others):
            there = 2 * chip[0] + chip[1]
            src, dst = srcs[a](mine, there, c), dsts[a](mine, there, c)
            if not started:
                dst = dsts[a](there, mine, c)
            copies.append(pltpu.make_async_remote_copy(
                src_ref=src, dst_ref=dst, send_sem=send_sems.at[a * 3 + j], recv_sem=recv_sems.at[a * 3 + j],
                device_id=(*chip, c), device_id_type=MESH))
    return copies


def _split_start(srcs_of, dsts_of, arrays, n_src, name, copies_of=None, n_sems=None, dep=None):
    n = len(arrays)
    n_sems = n_sems or 3 * n_src
    copies_of = copies_of or (lambda ins, ss, rs, started: _ici_copies(srcs_of(ins), dsts_of(ins), ss, rs, started))
    deps = [] if dep is None else [dep]
    nd = len(deps)

    def body(*refs):
        ins = refs[:n]
        send_sems, recv_sems = refs[n + nd], refs[n + nd + 1]
        token = refs[2 * n + nd + 2]
        for cp in copies_of(ins, send_sems, recv_sems, True):
            cp.start()
        token[...] = jnp.zeros_like(token)

    out = pl.pallas_call(
        body, name=name,
        in_specs=[HBM_SPEC] * n + [pl.BlockSpec(memory_space=pl.ANY)] * nd,
        out_specs=(SEM_SPEC, SEM_SPEC, *([HBM_SPEC] * n), pl.BlockSpec(memory_space=pltpu.VMEM)),
        out_shape=(pltpu.SemaphoreType.DMA((n_sems,)), pltpu.SemaphoreType.DMA((n_sems,)),
                   *[pltpu.HBM(a.shape, a.dtype) for a in arrays], TOKEN),
        input_output_aliases={a: 2 + a for a in range(n)},
        compiler_params=pltpu.CompilerParams(has_side_effects=DATAFLOW),
    )(*[_in_hbm(a) for a in arrays], *deps)
    return out[0], out[1], list(out[2:2 + n]), out[2 + n]


def _split_wait(srcs_of, dsts_of, send_sems, recv_sems, arrays, after, name, copies_of=None):
    n = len(arrays)
    copies_of = copies_of or (lambda ins, ss, rs, started: _ici_copies(srcs_of(ins), dsts_of(ins), ss, rs, started))

    def body(*refs):
        ins = refs[:n]
        send_sems, recv_sems = refs[n], refs[n + 1]
        for cp in copies_of(ins, send_sems, recv_sems, False):
            cp.wait_send()
            cp.wait_recv()

    return pl.pallas_call(
        body, name=name,
        in_specs=[HBM_SPEC] * n + [SEM_SPEC, SEM_SPEC] + [pl.BlockSpec(memory_space=pl.ANY)] * len(after),
        out_specs=[HBM_SPEC] * n,
        out_shape=[pltpu.HBM(a.shape, a.dtype) for a in arrays],
        input_output_aliases={a: a for a in range(n)},
        compiler_params=pltpu.CompilerParams(has_side_effects=DATAFLOW),
    )(*arrays, send_sems, recv_sems, *after)


def _gather_views(ins):
    view = [lambda frm, to, c, r=r: r.at[frm, c] for r in ins]
    return view


def _gather_start(bufs, dep, name):
    return _split_start(_gather_views, _gather_views, bufs, len(bufs), name, dep=dep)


def _gather_wait(send_sems, recv_sems, bufs, after, name):
    return _split_wait(_gather_views, _gather_views, send_sems, recv_sems, bufs, after, name)


def _forward_halves(bufs, name):
    n = len(bufs)
    any_spec = pl.BlockSpec(memory_space=pl.ANY)

    def body(*refs):
        outs = refs[n:2 * n]
        send_sems, recv_sems = refs[2 * n:]
        x, y, c, others = _place()
        copies = []
        for a in range(n):
            for j, chip in enumerate(others):
                landed = outs[a].at[2 * chip[0] + chip[1], c]
                copies.append(pltpu.make_async_remote_copy(
                    src_ref=landed, dst_ref=landed, send_sem=send_sems.at[a * 3 + j], recv_sem=recv_sems.at[a * 3 + j],
                    device_id=(x, y, 1 - c), device_id_type=MESH))
        for cp in copies:
            cp.start()
        for a in range(n):
            for j, chip in enumerate(others):
                landed = outs[a].at[2 * chip[0] + chip[1], 1 - c]
                pltpu.make_async_remote_copy(
                    src_ref=landed, dst_ref=landed, send_sem=send_sems.at[a * 3 + j], recv_sem=recv_sems.at[a * 3 + j],
                    device_id=(x, y, c), device_id_type=MESH).wait_recv()
        for cp in copies:
            cp.wait_send()

    out = pl.pallas_call(
        body, name=name,
        in_specs=[any_spec] * n, out_specs=[any_spec] * n,
        out_shape=[jax.ShapeDtypeStruct(b.shape, b.dtype) for b in bufs],
        input_output_aliases={a: a for a in range(n)},
        scratch_shapes=[pltpu.SemaphoreType.DMA((3 * n,)), pltpu.SemaphoreType.DMA((3 * n,))],
    )(*bufs)
    return [_merge2(o, 1) for o in out]


def _swap_halves_bf16(gbs, name):
    n = len(gbs)
    any_spec = pl.BlockSpec(memory_space=pl.ANY)

    def body(*refs):
        ins, outs = refs[:n], refs[n:2 * n]
        send_sems, recv_sems = refs[2 * n:]
        x, y, c, _ = _place()
        copies = []
        for a in range(n):
            copies.append(pltpu.make_async_remote_copy(
                src_ref=ins[a].at[:, 1 - c], dst_ref=outs[a], send_sem=send_sems.at[a], recv_sem=recv_sems.at[a],
                device_id=(x, y, 1 - c), device_id_type=MESH))
        for cp in copies:
            cp.start()
        for cp in copies:
            cp.wait()

    return pl.pallas_call(
        body, name=name,
        in_specs=[any_spec] * n, out_specs=[any_spec] * n,
        out_shape=[jax.ShapeDtypeStruct((g.shape[0], g.shape[1] // 2, g.shape[2]), g.dtype) for g in gbs],
        scratch_shapes=[pltpu.SemaphoreType.DMA((n,)), pltpu.SemaphoreType.DMA((n,))],
    )(*[_split2(g, 1) for g in gbs])


def _scatter_srcs(n):
    return lambda ins: [lambda frm, to, c, r=r: r.at[to] for r in ins[:n]]


def _scatter_dsts(n):
    return lambda ins: [lambda frm, to, c, r=r: r.at[frm] for r in ins[n:]]


def _scatter_start(hbs, name):
    n = len(hbs)
    lands = [lax.empty(h.shape, h.dtype) for h in hbs]
    return _split_start(_scatter_srcs(n), _scatter_dsts(n), list(hbs) + lands, n, name)


def _scatter_wait(send_sems, recv_sems, arrays, after, name):
    n = len(arrays) // 2
    return _split_wait(_scatter_srcs(n), _scatter_dsts(n), send_sems, recv_sems, arrays, after, name)[n:]


def _join_halves(rhs, name):
    n = len(rhs)
    any_spec = pl.BlockSpec(memory_space=pl.ANY)

    def body(*refs):
        outs = refs[n:2 * n]
        send_sems, recv_sems = refs[2 * n:]
        x, y, c, _ = _place()
        copies = []
        for a in range(n):
            copies.append(pltpu.make_async_remote_copy(
                src_ref=outs[a].at[c], dst_ref=outs[a].at[c], send_sem=send_sems.at[a],
                recv_sem=recv_sems.at[a], device_id=(x, y, 1 - c), device_id_type=MESH))
        for cp in copies:
            cp.start()
        for a in range(n):
            landed = outs[a].at[1 - c]
            pltpu.make_async_remote_copy(
                src_ref=landed, dst_ref=landed, send_sem=send_sems.at[a], recv_sem=recv_sems.at[a],
                device_id=(x, y, c), device_id_type=MESH).wait_recv()
        for cp in copies:
            cp.wait_send()

    out = pl.pallas_call(
        body, name=name,
        in_specs=[any_spec] * n, out_specs=[any_spec] * n,
        out_shape=[jax.ShapeDtypeStruct(r.shape, r.dtype) for r in rhs],
        input_output_aliases={a: a for a in range(n)},
        scratch_shapes=[pltpu.SemaphoreType.DMA((n,)), pltpu.SemaphoreType.DMA((n,))],
    )(*rhs)
    return [_merge2(o) for o in out]


FLIPS = [(dx, dy, dc) for dx in (0, 1) for dy in (0, 1) for dc in (0, 1)][1:]


def _peer_copies(ins, send_sems, recv_sems, started):
    x, y, c, _ = _place()
    copies = []
    for k, (dx, dy, dc) in enumerate(FLIPS):
        px, py, pc = jnp.bitwise_xor(x, dx), jnp.bitwise_xor(y, dy), jnp.bitwise_xor(c, dc)
        slot = 4 * x + 2 * y + c if started else 4 * px + 2 * py + pc
        copies.append(pltpu.make_async_remote_copy(
            src_ref=ins[0], dst_ref=ins[1].at[slot], send_sem=send_sems.at[k], recv_sem=recv_sems.at[k],
            device_id=(px, py, pc), device_id_type=MESH))
    return copies


def _small_start(v, name):
    land = lax.empty((8,) + v.shape, v.dtype)
    return _split_start(None, None, [v, land], 0, name, copies_of=_peer_copies, n_sems=len(FLIPS))


def _small_wait(send_sems, recv_sems, arrays, after, name):
    return _split_wait(None, None, send_sems, recv_sems, arrays, after, name, copies_of=_peer_copies)[1]


def _sum_slots(land, v, me):
    rows, cols = v.shape

    def body(me_ref, land_ref, v_ref, o_ref):
        o_ref[...] = jnp.zeros_like(o_ref)
        for d in range(8):
            @pl.when(me_ref[0] == d)
            def _():
                o_ref[...] += v_ref[...]

            @pl.when(me_ref[0] != d)
            def _():
                o_ref[...] += land_ref[d]

    return pl.pallas_call(
        body, name="sum_slots",
        grid_spec=pltpu.PrefetchScalarGridSpec(
            num_scalar_prefetch=1, grid=(1,),
            in_specs=[pl.BlockSpec((8, rows, cols), lambda i, me_ref: (0, 0, 0)),
                      pl.BlockSpec((rows, cols), lambda i, me_ref: (0, 0))],
            out_specs=pl.BlockSpec((rows, cols), lambda i, me_ref: (0, 0))),
        out_shape=jax.ShapeDtypeStruct((rows, cols), F32),
        compiler_params=_cparams(("arbitrary",)),
    )(me, land, v)


def _by_shape(arrays):
    groups = {}
    for k, a in enumerate(arrays):
        groups.setdefault(a.shape, []).append(k)
    return list(groups.values())


def _add_sibling_half(gs, sbs, idx, name):
    n = len(gs)
    ns, r, c = gs[0].shape
    r2 = r // 2

    def body(idx_ref, *refs):
        for a in range(n):
            g_ref, sb_ref, hown_ref, hb_ref = refs[a], refs[n + a], refs[2 * n + a], refs[3 * n + a]
            h = g_ref[...] + sb_ref[...].astype(F32)
            hb_ref[...] = h.astype(BF16)

            @pl.when(pl.program_id(0) == idx_ref[0])
            def _():
                hown_ref[...] = h

    spec = pl.BlockSpec((None, r2, c), lambda s, idx_ref: (s, 0, 0))
    out = pl.pallas_call(
        body, name=name,
        grid_spec=pltpu.PrefetchScalarGridSpec(
            num_scalar_prefetch=1, grid=(ns,),
            in_specs=[pl.BlockSpec((None, r2, c), lambda s, idx_ref: (s, idx_ref[4], 0))] * n + [spec] * n,
            out_specs=[pl.BlockSpec((r2, c), lambda s, idx_ref: (0, 0))] * n + [spec] * n),
        out_shape=[jax.ShapeDtypeStruct((r2, c), F32)] * n + [jax.ShapeDtypeStruct((ns, r2, c), BF16)] * n,
        compiler_params=_cparams(("arbitrary",)),
    )(idx, *gs, *sbs)
    return [(out[a], out[n + a]) for a in range(n)]


def _add_chip_slabs(hs, rbs, idx, name):
    n = len(hs)
    r2, c = hs[0].shape

    def body(idx_ref, *refs):
        for a in range(n):
            h_ref, r0_ref, r1_ref, r2_ref = refs[4 * a:4 * a + 4]
            refs[4 * n + a][...] = ((h_ref[...] + r0_ref[...].astype(F32)) + r1_ref[...].astype(F32)) + r2_ref[...].astype(F32)

    def pick(k):
        return pl.BlockSpec((None, r2, c), lambda i, idx_ref: (idx_ref[k], 0, 0))

    operands = []
    for h, rb in zip(hs, rbs):
        operands += [h, rb, rb, rb]
    out = pl.pallas_call(
        body, name=name,
        grid_spec=pltpu.PrefetchScalarGridSpec(
            num_scalar_prefetch=1, grid=(1,),
            in_specs=[pl.BlockSpec((r2, c), lambda i, idx_ref: (0, 0)), pick(1), pick(2), pick(3)] * n,
            out_specs=[pl.BlockSpec((None, r2, c), lambda i, idx_ref: (idx_ref[4], 0, 0))] * n),
        out_shape=[jax.ShapeDtypeStruct((2, r2, c), F32)] * n,
        compiler_params=_cparams(("arbitrary",)),
    )(idx, *operands)
    return list(out)


ELEMENTWISE_VMEM = 16 * 1024 * 1024


def _adamw(items, name):
    n = len(items)
    r, c = items[0][0].shape
    br = max(b for b in range(8, r + 1, 8) if r % b == 0 and n * 16 * b * c * 4 <= ELEMENTWISE_VMEM) if r % 8 == 0 else r

    def body(*refs):
        for a in range(n):
            g_ref, w_ref, m_ref, v_ref = refs[4 * a:4 * a + 4]
            go_ref, d_ref, nm_ref, nv_ref = refs[4 * n + 4 * a:4 * n + 4 * a + 4]
            gg = g_ref[...]
            go_ref[...] = gg
            nm = B1 * m_ref[...] + (1.0 - B1) * gg
            nv = B2 * v_ref[...] + (1.0 - B2) * jnp.square(gg)
            m_hat = nm / (1.0 - B1 ** STEP)
            v_hat = nv / (1.0 - B2 ** STEP)
            d_ref[...] = -LR * (m_hat / (jnp.sqrt(v_hat) + ADAM_EPS) + WD * w_ref[...])
            nm_ref[...] = nm
            nv_ref[...] = nv

    spec = pl.BlockSpec((br, c), lambda i: (i, 0))
    out = pl.pallas_call(
        body, name=name,
        grid=(r // br,),
        in_specs=[spec] * (4 * n), out_specs=[spec] * (4 * n),
        out_shape=[jax.ShapeDtypeStruct((r, c), F32)] * (4 * n),
        compiler_params=_cparams(("arbitrary",)),
    )(*[a for item in items for a in item])
    return [tuple(out[4 * a:4 * a + 4]) for a in range(n)]


BIG = ("w_in", "w_conv_out", "w_pool", "w_pool_out", "w_o", "w_ffn_gate", "w_ffn_up", "w_ffn_down")
REPL = ("g_mix", "b_gate", "b_dw", "ln_g", "ln_b", "pool_scale", "g_ffn", "g_final")
GROUP_MIX = ("w_conv_out", "w_pool", "w_pool_out", "w_o")
GROUP_FFN = ("w_ffn_gate", "w_ffn_up", "w_ffn_down")
TRANSPOSED = ("w_ffn_gate", "w_ffn_up")
WEIGHT_ORDER = ("meta_tokens", "g_mix", "w_in", "b_gate", "w_dw", "b_dw", "ln_g", "ln_b", "w_conv_out", "w_pool",
                "pool_scale", "w_pool_out", "w_o", "g_ffn", "w_ffn_gate", "w_ffn_up", "w_ffn_down", "g_final")


def _shard2d(name, a):
    a = a[0]
    if name == "w_pool":
        return a.reshape(4 * 64, GD)
    if name in TRANSPOSED:
        return a.T
    return a


def _unshard2d(name, a, shape):
    return a.T.reshape(shape) if name in TRANSPOSED else a.reshape(shape)


def _slabs_to_cols(a):
    ns, m, c = a.shape
    return a.transpose(1, 0, 2).reshape(m, ns * c)


def kernel(x, meta_tokens, g_mix, w_in, b_gate, w_dw, b_dw, ln_g, ln_b, w_conv_out, w_pool, pool_scale, w_pool_out, w_o, g_ffn, w_ffn_gate, w_ffn_up, w_ffn_down, g_final, loss_target, m_meta_tokens, m_g_mix, m_w_in, m_b_gate, m_w_dw, m_b_dw, m_ln_g, m_ln_b, m_w_conv_out, m_w_pool, m_pool_scale, m_w_pool_out, m_w_o, m_g_ffn, m_w_ffn_gate, m_w_ffn_up, m_w_ffn_down, m_g_final, v_meta_tokens, v_g_mix, v_w_in, v_b_gate, v_w_dw, v_b_dw, v_ln_g, v_ln_b, v_w_conv_out, v_w_pool, v_pool_scale, v_w_pool_out, v_w_o, v_g_ffn, v_w_ffn_gate, v_w_ffn_up, v_w_ffn_down, v_g_final):
    args = dict(locals())
    w = {n: args[n] for n in WEIGHT_ORDER}
    mom = {n: args["m_" + n] for n in WEIGHT_ORDER}
    var = {n: args["v_" + n] for n in WEIGHT_ORDER}
    seq = x.shape[1]
    nb = seq // BR + 1
    tp = nb * BR
    tk = tp // 2 if (tp // 2) % 16 == 0 else BR
    t_total = seq + N_META
    cx, cy, cc = lax.axis_index("x"), lax.axis_index("y"), lax.axis_index("c")
    chip = 2 * cx + cy
    chip1 = jnp.reshape(chip, (1,)).astype(jnp.int32)
    core = jnp.reshape(cc, (1,)).astype(jnp.int32)
    others = jnp.sort(jnp.stack([2 * (1 - cx) + cy, 2 * cx + (1 - cy), 2 * (1 - cx) + (1 - cy)]))
    idx = jnp.concatenate([chip1, others.astype(jnp.int32), core])
    xs, target = x[0], loss_target[0]

    tiny = _gather_tiny(jnp.concatenate([w["meta_tokens"], w["w_dw"][0], jnp.zeros((1, GD), F32)], axis=0))
    small = {n: w[n] for n in REPL if n != "g_final"}
    small["g_final"] = w["g_final"].reshape(1, D)
    small["w_dw"] = _slabs_to_cols(tiny[:, N_META:])
    head = jnp.concatenate([jnp.zeros((PAD, D), F32), _slabs_to_cols(tiny[:, :N_META])], axis=0)

    def cast(group, dep):
        shards = [_shard2d(n, w[n]) for n in group]
        bufs = [None] * len(group)
        for ks in _by_shape(shards):
            done = _cast_into_slot([shards[k] for k in ks], chip1, dep, "cast_" + group[ks[0]])
            for k, b in zip(ks, done):
                bufs[k] = b
        return bufs

    def gather_finish(group, start, after, name):
        landed = _gather_wait(start[0], start[1], start[2], after, "gather_wait_" + name)
        return dict(zip(group, _forward_halves(landed, "forward_" + name)))

    st_in = _gather_start(cast(("w_in",), tiny), None, "gather_start_in")
    bufs_mix, bufs_ffn = cast(GROUP_MIX, st_in[3]), cast(GROUP_FFN, st_in[3])
    u = _rms_u(head, xs, small["g_mix"] + st_in[3][0:1, 0:1], nb)
    z_own = _in_proj_own(u, w["w_in"][0], idx, nb)
    gw = gather_finish(("w_in",), st_in, [z_own] + bufs_mix + bufs_ffn, "in")
    st_mix = _gather_start(bufs_mix, gw["w_in"], "gather_start_mix")
    z = _in_proj_rest(u, gw["w_in"], z_own, idx, st_mix[3], nb)
    gw.update(gather_finish(GROUP_MIX, st_mix, [z], "mix"))
    st_ffn = _gather_start(bufs_ffn, gw["w_o"], "gather_start_ffn")
    w_pool_b = gw["w_pool"].reshape(N_SHARD, 4, 64, GD).transpose(1, 0, 2, 3).reshape(4, GD, GD)
    w_co_b, w_po_b, w_o_b = (gw[n].reshape(D, D) for n in ("w_conv_out", "w_pool_out", "w_o"))
    h1, yc, yp, mg, ca, cpre, m, mw, m2b = _mixers_fwd(
        z, head, xs, small["b_gate"] + st_ffn[3][0, 0], small["w_dw"], small["b_dw"], small["ln_g"], small["ln_b"],
        small["pool_scale"], w_co_b, w_pool_b, w_po_b, w_o_b, nb, t_total)
    gw.update(gather_finish(GROUP_FFN, st_ffn, [h1], "ffn"))

    dh1, dh1b, vb, fb, dgb, dub, dh2b, loss, dg_ffn, dg_final = _ffn_fwd_bwd(
        h1, target, small["g_ffn"], small["g_final"], gw["w_ffn_gate"].reshape(D_FF, D),
        gw["w_ffn_up"].reshape(D_FF, D), gw["w_ffn_down"].reshape(D_FF, D), nb)

    def slabs(name, g):
        if name == "w_in":
            return g
        if name == "w_pool":
            return g.reshape(4, N_SHARD, 64, GD).transpose(1, 0, 2, 3).reshape(N_SHARD, 4 * 64, GD)
        return g.reshape(N_SHARD, g.shape[0] // N_SHARD, g.shape[1])

    def reduce_start(group, grads, name):
        g32 = [slabs(n, grads[n][0]) for n in group]
        g16 = [slabs(n, grads[n][1]) for n in group]
        from_sibling = _swap_halves_bf16(g16, "swap_halves_" + name)
        halves = [None] * len(group)
        for ks in _by_shape(g32):
            done = _add_sibling_half([g32[k] for k in ks], [from_sibling[k] for k in ks], idx, "add_sibling_" + group[ks[0]])
            for k, pair in zip(ks, done):
                halves[k] = pair
        return [h for h, _ in halves], _scatter_start([hb for _, hb in halves], "scatter_start_" + name)

    def reduce_finish(group, halves, start, after, name):
        from_chips = _scatter_wait(start[0], start[1], start[2], after, "scatter_wait_" + name)
        reduced = [None] * len(group)
        for ks in _by_shape(halves):
            done = _add_chip_slabs([halves[k] for k in ks], [from_chips[k] for k in ks], idx, "add_chips_" + group[ks[0]])
            for k, r in zip(ks, done):
                reduced[k] = r
        return reduced

    half_ff = D_FF // 2
    grads_ffn = {
        "w_ffn_gate": _wgrad(dgb, vb, half_ff, D, tk, "wgrad_ffn_gate"),
        "w_ffn_up": _wgrad(dub, vb, half_ff, D, tk, "wgrad_ffn_up"),
        "w_ffn_down": _wgrad(fb, dh2b, half_ff, D, tk, "wgrad_ffn_down"),
    }
    halves_ffn, sc_ffn = reduce_start(GROUP_FFN, grads_ffn, "ffn")

    dycb, dypb, dzg, dconv, dmwb, dm, db_gate, dln_g, dln_b, db_dw, dps = _mixers_bwd_rows(
        dh1b, yc, yp, z, small["b_gate"], cpre, small["ln_g"], small["ln_b"], mw, small["pool_scale"],
        w_o_b, w_co_b, w_po_b, w_pool_b, sc_ffn[3], nb)
    grads_mix = {
        "w_conv_out": _wgrad(ca, dycb, D, D, tk, "wgrad_conv_out"),
        "w_pool": _wgrad(m, dmwb, GD, GD, tp, "wgrad_pool", diag=True),
        "w_pool_out": _wgrad(m2b, dypb, D, D, tk, "wgrad_pool_out"),
        "w_o": _wgrad(mg, dh1b, D, D, tk, "wgrad_o"),
    }
    halves_mix, sc_mix = reduce_start(GROUP_MIX, grads_mix, "mix")
    dzb, grad_x, dhead, dw_dw, dg_mix = _mixers_bwd_halo(
        dconv, dm, z, dzg, small["w_dw"], head, xs, small["g_mix"], dh1, gw["w_in"], sc_mix[3], nb, t_total)
    packed = jnp.concatenate(
        [dg_mix, db_gate.reshape(2, D), db_dw, dln_g, dln_b, dps, dg_ffn, dg_final,
         jnp.broadcast_to(loss, (1, D)), jnp.zeros((6, D), F32), dhead[PAD:], dw_dw], axis=0)
    sm = _small_start(packed, "small_start")
    grads_in = {"w_in": _wgrad(u, dzb, D, D_IN // N_SHARD, tk, "wgrad_in", col_major=True, dep=sm[3])}
    halves_in, sc_in = reduce_start(("w_in",), grads_in, "in")

    land = _small_wait(sm[0], sm[1], sm[2], [sc_in[3]], "small_wait")
    summed = _sum_slots(land, packed, jnp.reshape(4 * cx + 2 * cy + cc, (1,)).astype(jnp.int32))
    loss = summed[9, 0]

    first = GROUP_FFN + GROUP_MIX
    reduced_half = reduce_finish(GROUP_FFN, halves_ffn, sc_ffn, [summed], "ffn")
    reduced_half += reduce_finish(GROUP_MIX, halves_mix, sc_mix, [summed], "mix")
    reduced = dict(zip(first, _join_halves(reduced_half, "join_halves_first")))
    updates = {}
    for ks in _by_shape([reduced[n] for n in first]):
        names = [first[k] for k in ks]
        done = _adamw([(reduced[n], _shard2d(n, w[n]), _shard2d(n, mom[n]), _shard2d(n, var[n])) for n in names],
                      "adamw_" + names[0])
        updates.update(zip(names, done))

    def repl_stack(d):
        return jnp.concatenate([d["g_mix"], d["b_gate"].reshape(2, D), d["b_dw"], d["ln_g"], d["ln_b"],
                                d["pool_scale"], d["g_ffn"], d["g_final"].reshape(1, D), jnp.ones((7, D), F32)], axis=0)

    def shard_stack(d):
        return jnp.concatenate([d["meta_tokens"], d["w_dw"][0], jnp.ones((1, GD), F32)], axis=0)

    g_repl = summed[0:16]
    g_shard = lax.dynamic_slice_in_dim(summed[16:64], chip * GD, GD, axis=1)
    g_repl, d_repl, m_repl, v_repl = _adamw([(g_repl, repl_stack(w), repl_stack(mom), repl_stack(var))], "adamw_repl")[0]
    g_shard, d_shard, m_shard, v_shard = _adamw(
        [(g_shard, shard_stack(w), shard_stack(mom), shard_stack(var))], "adamw_cols")[0]

    done_first = [updates[n][1] for n in first] + [d_repl, d_shard]
    last_half = reduce_finish(("w_in",), halves_in, sc_in, done_first, "in")
    reduced["w_in"] = _join_halves(last_half, "join_halves_in")[0]
    updates["w_in"] = _adamw([(reduced["w_in"], w["w_in"][0], mom["w_in"][0], var["w_in"][0])], "adamw_w_in")[0]

    def unpack(name, repl, shard):
        if name == "meta_tokens":
            return shard[0:N_META]
        if name == "w_dw":
            return shard[N_META:N_META + KW].reshape(1, KW, GD)
        row = {"g_mix": 0, "b_gate": 1, "b_dw": 3, "ln_g": 4, "ln_b": 5, "pool_scale": 6, "g_ffn": 7, "g_final": 8}[name]
        if name == "b_gate":
            return repl[1:3].reshape(1, 2 * D)
        if name == "g_final":
            return repl[8]
        return repl[row:row + 1]

    out_g, out_d, out_m, out_v = {}, {}, {}, {}
    for n in WEIGHT_ORDER:
        if n in BIG:
            g, d_, m_, v_ = updates[n]
            shape = w[n].shape
            out_g[n], out_d[n], out_m[n], out_v[n] = (_unshard2d(n, a, shape) for a in (g, d_, m_, v_))
        else:
            out_g[n] = unpack(n, g_repl, g_shard)
            out_d[n] = unpack(n, d_repl, d_shard)
            out_m[n] = unpack(n, m_repl, m_shard)
            out_v[n] = unpack(n, v_repl, v_shard)
    return (loss, grad_x[None], *[out_g[n] for n in WEIGHT_ORDER], *[out_d[n] for n in WEIGHT_ORDER],
            *[out_m[n] for n in WEIGHT_ORDER], *[out_v[n] for n in WEIGHT_ORDER])
```

```python
import jax
import jax.numpy as jnp
from jax import lax
from jax.experimental import pallas as pl
from jax.experimental.pallas import tpu as pltpu

F32 = jnp.float32
BF16 = jnp.bfloat16
MESH = pl.DeviceIdType.MESH

D = 1024
N_META = 16
KW = 31
POOL_WINDOWS = (2, 4, 8, 16)
GD = 256
D_IN = 5 * D
D_FF = 2816
N_SHARD = 4
BR = 256
HALO = 16
PAD = BR - N_META
EXT = BR + 2 * HALO
RMS_EPS = 1e-6
LN_EPS = 1e-5
LR, B1, B2, ADAM_EPS, WD, STEP = 0.001, 0.9, 0.999, 1e-08, 0.01, 10
VMEM_LIMIT = 56 * 1024 * 1024


def _cparams(sem, vmem=VMEM_LIMIT):
    return pltpu.CompilerParams(dimension_semantics=sem, vmem_limit_bytes=vmem)


def _dot(a, b):
    return jnp.dot(a, b, preferred_element_type=F32)


def _dot_nt(a, b):
    return lax.dot_general(a, b, (((1,), (1,)), ((), ())), preferred_element_type=F32)


def _dot_tn(a, b):
    return lax.dot_general(a, b, (((0,), (0,)), ((), ())), preferred_element_type=F32)


def _sigmoid(x):
    return 0.5 * jnp.tanh(0.5 * x) + 0.5


def _row_ids(i, n, offset=0):
    return lax.broadcasted_iota(jnp.int32, (n, 1), 0) + (i * BR + offset - PAD)


def _pool_cnt(t, w, t_total):
    left = w // 2
    right = w - 1 - left
    lo = jnp.clip(t - left, 0, t_total)
    hi = jnp.clip(t + right + 1, 0, t_total)
    return jnp.maximum(hi - lo, 1).astype(F32)


def _halo_specs(nb, halo_width=D, width=D):
    last = nb * (BR // HALO) - 1
    return [
        pl.BlockSpec((HALO, halo_width), lambda i: (jnp.maximum(i * (BR // HALO) - 1, 0), 0)),
        pl.BlockSpec((BR, width), lambda i: (i, 0)),
        pl.BlockSpec((HALO, halo_width), lambda i: (jnp.minimum((i + 1) * (BR // HALO), last), 0)),
    ]


def _cols(ref, n):
    return [ref.at[:, k * D:(k + 1) * D] for k in range(n)]


def _fill_ext(ext_ref, prev, cur, nxt, i, nb):
    ext_ref[0:HALO, :] = jnp.where(i > 0, prev, 0.0)
    ext_ref[HALO:HALO + BR, :] = cur
    ext_ref[HALO + BR:EXT, :] = jnp.where(i < nb - 1, nxt, 0.0)


ROT_ROWS = EXT - 8


def _fill_rot(rot_ref, ext_ref, lanes):
    for r in range(1, 8):
        rot_ref[r] = ext_ref[pl.ds(r, ROT_ROWS), lanes]


def _tap(rot_ref, ext_ref, lanes, offset):
    q, r = divmod(offset, 8)
    if r == 0:
        return ext_ref[pl.ds(8 * q, BR), lanes]
    return rot_ref[r, pl.ds(8 * q, BR), :]


def _row_spec(width=D):
    return pl.BlockSpec((BR, width), lambda i: (i, 0))


def _x_spec():
    return pl.BlockSpec((BR, D), lambda i: (jnp.maximum(i - 1, 0), 0))


def _const_spec(shape):
    nd = len(shape)
    return pl.BlockSpec(shape, lambda i: (0,) * nd)


def _rms_u(head, x, g_mix, nb):
    def body(head_ref, x_ref, g_ref, u_ref):
        i = pl.program_id(0)
        h = jnp.where(i == 0, head_ref[...], x_ref[...])
        r = lax.rsqrt(jnp.mean(h * h, axis=-1, keepdims=True) + RMS_EPS)
        u_ref[...] = ((h * r) * g_ref[...]).astype(BF16)

    return pl.pallas_call(
        body, name="rms_u",
        grid=(nb,),
        in_specs=[_const_spec((BR, D)), _x_spec(), _const_spec((1, D))],
        out_specs=_row_spec(),
        out_shape=jax.ShapeDtypeStruct((nb * BR, D), BF16),
        compiler_params=_cparams(("arbitrary",)),
    )(head, x, g_mix)


def _in_proj_rows(tp):
    return tp // 4 if (tp // 4) % 16 == 0 else BR


def _in_proj_own(u, w_own, idx, nb):
    tp = nb * BR
    wcols = w_own.shape[1]
    rows = _in_proj_rows(tp)

    def body(idx_ref, u_ref, w_ref, z_ref, wb_ref):
        @pl.when(pl.program_id(0) == 0)
        def _():
            wb_ref[...] = w_ref[...].astype(BF16)

        z_ref[...] = _dot(u_ref[...], wb_ref[...])

    return pl.pallas_call(
        body, name="in_proj_own",
        grid_spec=pltpu.PrefetchScalarGridSpec(
            num_scalar_prefetch=1, grid=(tp // rows,),
            in_specs=[pl.BlockSpec((rows, D), lambda i, idx_ref: (i, 0)),
                      pl.BlockSpec((D, wcols), lambda i, idx_ref: (0, 0))],
            out_specs=pl.BlockSpec((rows, wcols), lambda i, idx_ref: (i, idx_ref[0])),
            scratch_shapes=[pltpu.VMEM((D, wcols), BF16)]),
        out_shape=jax.ShapeDtypeStruct((tp, N_SHARD * wcols), F32),
        compiler_params=_cparams(("arbitrary",)),
    )(idx, u, w_own)


def _in_proj_rest(u, w_in_b, z, idx, dep, nb):
    tp = nb * BR
    wcols = w_in_b.shape[2]
    rows = _in_proj_rows(tp)

    def body(idx_ref, u_ref, w_ref, z_in, dep_ref, z_ref):
        z_ref[...] = _dot(u_ref[...], w_ref[...])

    any_spec = pl.BlockSpec(memory_space=pl.ANY)
    return pl.pallas_call(
        body, name="in_proj_rest",
        grid_spec=pltpu.PrefetchScalarGridSpec(
            num_scalar_prefetch=1, grid=(N_SHARD - 1, tp // rows),
            in_specs=[pl.BlockSpec((rows, D), lambda s, i, idx_ref: (i, 0)),
                      pl.BlockSpec((None, D, wcols), lambda s, i, idx_ref: (idx_ref[1 + s], 0, 0)),
                      any_spec, any_spec],
            out_specs=pl.BlockSpec((rows, wcols), lambda s, i, idx_ref: (i, idx_ref[1 + s]))),
        out_shape=jax.ShapeDtypeStruct(z.shape, F32),
        input_output_aliases={3: 0},
        compiler_params=_cparams(("arbitrary", "arbitrary")),
    )(idx, u, w_in_b, z, dep)


def _mixers_fwd(z, head, x, b_gate, w_dw, b_dw, ln_g, ln_b, pool_scale, w_co, w_pool, w_po, w_o, nb, t_total):
    tp = nb * BR

    def body(z_prev, z_cur, z_next, head_ref, x_ref, bg_ref, wdw_ref, bdw_ref,
             lng_ref, lnb_ref, ps_ref, wco_ref, wpool_ref, wpo_ref, wo_ref,
             h1_ref, mg_ref, ca_ref, cpre_ref, m_ref, m2b_ref, ext_ref, pext_ref, rot_ref, mw_ref):
        i = pl.program_id(0)
        avp, agp, pp = _cols(z_prev, 3)
        av, ag, pc, za, zb = _cols(z_cur, 5)
        avn, agn, pn = _cols(z_next, 3)
        _fill_ext(ext_ref, avp[...] * _sigmoid(agp[...]), av[...] * _sigmoid(ag[...]),
                  avn[...] * _sigmoid(agn[...]), i, nb)
        _fill_ext(pext_ref, pp[...], pc[...], pn[...], i, nb)

        def conv_chunk(c, carry):
            lanes = pl.ds(pl.multiple_of(c * 128, 128), 128)
            _fill_rot(rot_ref, ext_ref, lanes)
            acc = jnp.broadcast_to(bdw_ref[:, lanes], (BR, 128))
            for k in range(KW):
                acc = acc + wdw_ref[k:k + 1, lanes] * _tap(rot_ref, ext_ref, lanes, 1 + k)
            cpre_ref[:, lanes] = acc
            return carry
        lax.fori_loop(0, D // 128, conv_chunk, 0)

        conv = cpre_ref[...]
        mu = jnp.mean(conv, axis=-1, keepdims=True)
        xc = conv - mu
        rstd = lax.rsqrt(jnp.mean(xc * xc, axis=-1, keepdims=True) + LN_EPS)
        ln = (xc * rstd) * lng_ref[...] + lnb_ref[...]
        cact = (ln * _sigmoid(ln)).astype(BF16)
        ca_ref[...] = cact
        y_conv = _dot(cact, wco_ref[...])

        t = _row_ids(i, BR)
        for gi, w in enumerate(POOL_WINDOWS):
            left = w // 2
            right = w - 1 - left
            lanes = slice(gi * GD, (gi + 1) * GD)
            s = pext_ref[pl.ds(HALO - left, BR), lanes]
            for j in range(-left + 1, right + 1):
                s = s + pext_ref[pl.ds(HALO + j, BR), lanes]
            m = (s / _pool_cnt(t, w, t_total) - pext_ref[HALO:HALO + BR, lanes]).astype(BF16)
            m_ref[:, lanes] = m
            mw_ref[:, lanes] = _dot(m, wpool_ref[gi])
        mw = mw_ref[...]
        m2b = (mw * ps_ref[...]).astype(BF16)
        m2b_ref[...] = m2b
        y_pool = _dot(m2b, wpo_ref[...])

        s_a = _sigmoid(za[...] + bg_ref[:, 0:D])
        s_b = _sigmoid(zb[...] + bg_ref[:, D:2 * D])
        merged = (s_a * y_conv + s_b * y_pool).astype(BF16)
        mg_ref[...] = merged
        h0 = jnp.where(i == 0, head_ref[...], x_ref[...])
        h1_ref[...] = h0 + _dot(merged, wo_ref[...])

    in_specs = (_halo_specs(nb, 3 * D, 5 * D)
                + [_const_spec((BR, D)), _x_spec(), _const_spec((1, 2 * D)), _const_spec((32, D)),
                   _const_spec((1, D)), _const_spec((1, D)), _const_spec((1, D)), _const_spec((1, D)),
                   _const_spec((D, D)), _const_spec((4, GD, GD)), _const_spec((D, D)), _const_spec((D, D))])
    outs = [(F32, "h1"), (BF16, "mg"), (BF16, "ca"), (F32, "cpre"), (BF16, "m"), (BF16, "m2b")]
    return pl.pallas_call(
        body, name="mixers_fwd",
        grid=(nb,),
        in_specs=in_specs,
        out_specs=[_row_spec() for _ in outs],
        out_shape=[jax.ShapeDtypeStruct((tp, D), dt) for dt, _ in outs],
        scratch_shapes=[pltpu.VMEM((EXT, D), F32), pltpu.VMEM((EXT, D), F32), pltpu.VMEM((8, ROT_ROWS, 128), F32),
                        pltpu.VMEM((BR, D), F32)],
        compiler_params=_cparams(("arbitrary",)),
    )(z, z, z, head, x, b_gate, w_dw, b_dw, ln_g, ln_b, pool_scale, w_co, w_pool, w_po, w_o)


def _ffn_fwd_bwd(h1, target, g_ffn, g_final, w_g, w_u, w_d, nb):
    tp = nb * BR

    def body(h1_ref, tgt_ref, gf_ref, gfin_ref, wg_hbm, wu_hbm, wd_hbm,
             dh1_ref, dh1b_ref, vb_ref, fb_ref, dgb_ref, dub_ref, dh2b_ref, loss_ref, dgf_ref, dgfin_ref,
             wg_ref, wu_ref, wd_ref, sem):
        i = pl.program_id(0)

        @pl.when(i == 0)
        def _():
            copies = [pltpu.make_async_copy(wg_hbm, wg_ref, sem.at[0]),
                      pltpu.make_async_copy(wu_hbm, wu_ref, sem.at[1]),
                      pltpu.make_async_copy(wd_hbm, wd_ref, sem.at[2])]
            for cp in copies:
                cp.start()
            loss_ref[...] = jnp.zeros_like(loss_ref)
            dgf_ref[...] = jnp.zeros_like(dgf_ref)
            dgfin_ref[...] = jnp.zeros_like(dgfin_ref)
            for cp in copies:
                cp.wait()

        h1 = h1_ref[...]
        r1 = lax.rsqrt(jnp.mean(h1 * h1, axis=-1, keepdims=True) + RMS_EPS)
        vn = h1 * r1
        vb = (vn * gf_ref[...]).astype(BF16)
        vb_ref[...] = vb
        g = _dot_nt(vb, wg_ref[...])
        up = _dot_nt(vb, wu_ref[...])
        sg = _sigmoid(g)
        sl = g * sg
        fb = (sl * up).astype(BF16)
        fb_ref[...] = fb
        h2 = h1 + _dot(fb, wd_ref[...])
        r2 = lax.rsqrt(jnp.mean(h2 * h2, axis=-1, keepdims=True) + RMS_EPS)
        yn = h2 * r2
        valid = i > 0
        diff = jnp.where(valid, yn * gfin_ref[...] - tgt_ref[...], 0.0)
        loss_ref[...] += 0.5 * jnp.sum(jnp.mean(diff * diff, axis=-1, keepdims=True))
        dy = diff * (1.0 / D)
        dgfin_ref[...] += jnp.sum(dy * yn, axis=0, keepdims=True)
        gd = dy * gfin_ref[...]
        dh2 = r2 * (gd - yn * jnp.mean(yn * gd, axis=-1, keepdims=True))
        dh2b = dh2.astype(BF16)
        dh2b_ref[...] = dh2b
        df = _dot_nt(dh2b, wd_ref[...])
        dub = (df * sl).astype(BF16)
        dgb = (df * up * (sg * (1.0 + g * (1.0 - sg)))).astype(BF16)
        dub_ref[...] = dub
        dgb_ref[...] = dgb
        dv = _dot(dgb, wg_ref[...]) + _dot(dub, wu_ref[...])
        dgf_ref[...] += jnp.sum(dv * vn, axis=0, keepdims=True)
        gd1 = dv * gf_ref[...]
        dh1 = dh2 + r1 * (gd1 - vn * jnp.mean(vn * gd1, axis=-1, keepdims=True))
        dh1_ref[...] = dh1
        dh1b_ref[...] = dh1.astype(BF16)

    any_spec = pl.BlockSpec(memory_space=pl.ANY)
    return pl.pallas_call(
        body, name="ffn_fwd_bwd",
        grid=(nb,),
        in_specs=[_row_spec(), _x_spec(), _const_spec((1, D)), _const_spec((1, D)), any_spec, any_spec, any_spec],
        out_specs=[_row_spec(), _row_spec(), _row_spec(), _row_spec(D_FF), _row_spec(D_FF), _row_spec(D_FF), _row_spec(),
                   _const_spec((1, 1)), _const_spec((1, D)), _const_spec((1, D))],
        out_shape=[jax.ShapeDtypeStruct((tp, D), F32), jax.ShapeDtypeStruct((tp, D), BF16),
                   jax.ShapeDtypeStruct((tp, D), BF16), jax.ShapeDtypeStruct((tp, D_FF), BF16),
                   jax.ShapeDtypeStruct((tp, D_FF), BF16), jax.ShapeDtypeStruct((tp, D_FF), BF16),
                   jax.ShapeDtypeStruct((tp, D), BF16), jax.ShapeDtypeStruct((1, 1), F32),
                   jax.ShapeDtypeStruct((1, D), F32), jax.ShapeDtypeStruct((1, D), F32)],
        scratch_shapes=[pltpu.VMEM((D_FF, D), BF16), pltpu.VMEM((D_FF, D), BF16), pltpu.VMEM((D_FF, D), BF16),
                        pltpu.SemaphoreType.DMA((3,))],
        compiler_params=_cparams(("arbitrary",)),
    )(h1, target, g_ffn, g_final, w_g, w_u, w_d)


def _mixers_bwd_rows(dh1b, ca, m2b, z, b_gate, cpre, ln_g, ln_b, m, pool_scale, w_o, w_co, w_po, w_pool, dep, nb):
    tp = nb * BR

    def body(dh1b_ref, ca_ref, m2b_ref, za, zb, bg_ref, cpre_ref, lng_ref, lnb_ref, m_ref, ps_ref,
             wo_ref, wco_ref, wpo_ref, wpool_ref, dep_ref,
             dycb_ref, dypb_ref, dzg_ref, dconv_ref, dmwb_ref, dm_ref, dbg_ref, dlng_ref, dlnb_ref, dbdw_ref, dps_ref):
        i = pl.program_id(0)

        @pl.when(i == 0)
        def _():
            for r in (dbg_ref, dlng_ref, dlnb_ref, dbdw_ref, dps_ref):
                r[...] = jnp.zeros_like(r)

        dmg = _dot_nt(dh1b_ref[...], wo_ref[...])
        s_a = _sigmoid(za[...] + bg_ref[:, 0:D])
        s_b = _sigmoid(zb[...] + bg_ref[:, D:2 * D])
        dycb = (dmg * s_a).astype(BF16)
        dypb = (dmg * s_b).astype(BF16)
        dycb_ref[...] = dycb
        dypb_ref[...] = dypb
        dza = dmg * _dot(ca_ref[...], wco_ref[...]) * (s_a * (1.0 - s_a))
        dzb = dmg * _dot(m2b_ref[...], wpo_ref[...]) * (s_b * (1.0 - s_b))
        dzg_ref[:, 0:D] = dza.astype(BF16)
        dzg_ref[:, D:2 * D] = dzb.astype(BF16)
        dbg_ref[:, 0:D] += jnp.sum(dza, axis=0, keepdims=True)
        dbg_ref[:, D:2 * D] += jnp.sum(dzb, axis=0, keepdims=True)

        dca = _dot_nt(dycb, wco_ref[...])
        conv = cpre_ref[...]
        mu = jnp.mean(conv, axis=-1, keepdims=True)
        xc = conv - mu
        rstd = lax.rsqrt(jnp.mean(xc * xc, axis=-1, keepdims=True) + LN_EPS)
        xhat = xc * rstd
        ln = xhat * lng_ref[...] + lnb_ref[...]
        sg = _sigmoid(ln)
        dln = dca * (sg * (1.0 + ln * (1.0 - sg)))
        dlng_ref[...] += jnp.sum(dln * xhat, axis=0, keepdims=True)
        dlnb_ref[...] += jnp.sum(dln, axis=0, keepdims=True)
        dxh = dln * lng_ref[...]
        dconv = rstd * (dxh - jnp.mean(dxh, axis=-1, keepdims=True)
                        - xhat * jnp.mean(dxh * xhat, axis=-1, keepdims=True))
        dconv_ref[...] = dconv
        dbdw_ref[...] += jnp.sum(dconv, axis=0, keepdims=True)

        dm2 = _dot_nt(dypb, wpo_ref[...])
        dmwb = (dm2 * ps_ref[...]).astype(BF16)
        dmwb_ref[...] = dmwb
        for gi in range(len(POOL_WINDOWS)):
            lanes = slice(gi * GD, (gi + 1) * GD)
            mw = _dot(m_ref[:, lanes], wpool_ref[gi])
            dps_ref[:, lanes] += jnp.sum(dm2[:, lanes] * mw, axis=0, keepdims=True)
            dm_ref[:, lanes] = _dot_nt(dmwb[:, lanes], wpool_ref[gi])

    in_specs = [_row_spec(), _row_spec(), _row_spec(),
                pl.BlockSpec((BR, D), lambda i: (i, 3)), pl.BlockSpec((BR, D), lambda i: (i, 4)),
                _const_spec((1, 2 * D)), _row_spec(), _const_spec((1, D)), _const_spec((1, D)), _row_spec(),
                _const_spec((1, D)), _const_spec((D, D)), _const_spec((D, D)), _const_spec((D, D)),
                _const_spec((4, GD, GD)), pl.BlockSpec(memory_space=pl.ANY)]
    return pl.pallas_call(
        body, name="mixers_bwd_rows",
        grid=(nb,),
        in_specs=in_specs,
        out_specs=[_row_spec(), _row_spec(), _row_spec(2 * D), _row_spec(), _row_spec(), _row_spec(),
                   _const_spec((1, 2 * D)), _const_spec((1, D)), _const_spec((1, D)), _const_spec((1, D)),
                   _const_spec((1, D))],
        out_shape=[jax.ShapeDtypeStruct((tp, D), BF16), jax.ShapeDtypeStruct((tp, D), BF16),
                   jax.ShapeDtypeStruct((tp, 2 * D), BF16), jax.ShapeDtypeStruct((tp, D), F32),
                   jax.ShapeDtypeStruct((tp, D), BF16), jax.ShapeDtypeStruct((tp, D), F32),
                   jax.ShapeDtypeStruct((1, 2 * D), F32), jax.ShapeDtypeStruct((1, D), F32),
                   jax.ShapeDtypeStruct((1, D), F32), jax.ShapeDtypeStruct((1, D), F32),
                   jax.ShapeDtypeStruct((1, D), F32)],
        compiler_params=_cparams(("arbitrary",)),
    )(dh1b, ca, m2b, z, z, b_gate, cpre, ln_g, ln_b, m, pool_scale, w_o, w_co, w_po, w_pool, dep)


def _mixers_bwd_halo(dconv, dm, z, dzg, w_dw, head, x, g_mix, dh1, w_in_b, dep, nb, t_total):
    tp = nb * BR
    ns = w_in_b.shape[0]
    wcols = w_in_b.shape[2]
    seq = x.shape[0]

    def body(dcp, dcc, dcn, dmp, dmc, dmn, z_prev, z_cur, z_next, dzg_ref, wdw_ref, head_ref, x_ref, g_ref,
             dh1_ref, w_hbm, dep_ref,
             dzb_ref, gx_ref, dhead_ref, dwdw_ref, dgmix_ref,
             w_ref, sem, aext_ref, dext_ref, qext_ref, da_ref, rot_ref, dwp_ref):
        i = pl.program_id(0)
        (avp, agp), (av, ag), (avn, agn) = _cols(z_prev, 2), _cols(z_cur, 2), _cols(z_next, 2)

        @pl.when(i == 0)
        def _():
            cp = pltpu.make_async_copy(w_hbm, w_ref, sem.at[0])
            cp.start()
            dwp_ref[...] = jnp.zeros_like(dwp_ref)
            dgmix_ref[...] = jnp.zeros_like(dgmix_ref)
            cp.wait()

        sig_g = _sigmoid(ag[...])
        _fill_ext(aext_ref, avp[...] * _sigmoid(agp[...]), av[...] * sig_g, avn[...] * _sigmoid(agn[...]), i, nb)
        _fill_ext(dext_ref, dcp[...], dcc[...], dcn[...], i, nb)
        _fill_ext(qext_ref, dmp[...], dmc[...], dmn[...], i, nb)

        def conv_chunk(c, carry):
            lanes = pl.ds(pl.multiple_of(c * 128, 128), 128)
            _fill_rot(rot_ref, dext_ref, lanes)
            acc = jnp.zeros((BR, 128), F32)
            for k in range(KW):
                acc = acc + wdw_ref[k:k + 1, lanes] * _tap(rot_ref, dext_ref, lanes, KW - k)
            da_ref[:, lanes] = acc
            _fill_rot(rot_ref, aext_ref, lanes)
            dcv = dext_ref[HALO:HALO + BR, lanes]
            for k in range(KW):
                prod = _tap(rot_ref, aext_ref, lanes, 1 + k) * dcv
                dwp_ref[k, :, lanes] += jnp.sum(prod.reshape(BR // 8, 8, 128), axis=0)
            return carry
        lax.fori_loop(0, D // 128, conv_chunk, 0)

        @pl.when(i == nb - 1)
        def _():
            dwdw_ref[...] = jnp.sum(dwp_ref[...], axis=1)

        da = da_ref[...]
        a_val = av[...]
        dzb_ref[:, 0:D] = (da * sig_g).astype(BF16)
        dzb_ref[:, D:2 * D] = (da * a_val * (sig_g * (1.0 - sig_g))).astype(BF16)

        t_ext = _row_ids(i, EXT, -HALO)
        for gi, w in enumerate(POOL_WINDOWS):
            left = w // 2
            right = w - 1 - left
            lanes = slice(gi * GD, (gi + 1) * GD)
            qext_ref[:, lanes] = qext_ref[:, lanes] / _pool_cnt(t_ext, w, t_total)
            s = qext_ref[pl.ds(HALO - right, BR), lanes]
            for j in range(-right + 1, left + 1):
                s = s + qext_ref[pl.ds(HALO + j, BR), lanes]
            dzb_ref[:, 2 * D + gi * GD:2 * D + (gi + 1) * GD] = (s - dmc[:, lanes]).astype(BF16)
        dzb_ref[:, 3 * D:5 * D] = dzg_ref[...]

        du = _dot_nt(dzb_ref[:, 0:wcols], w_ref[0])
        for s_i in range(1, ns):
            du = du + _dot_nt(dzb_ref[:, s_i * wcols:(s_i + 1) * wcols], w_ref[s_i])
        h0 = jnp.where(i == 0, head_ref[...], x_ref[...])
        r0 = lax.rsqrt(jnp.mean(h0 * h0, axis=-1, keepdims=True) + RMS_EPS)
        un = h0 * r0
        dgmix_ref[...] += jnp.sum(du * un, axis=0, keepdims=True)
        gd = du * g_ref[...]
        dh0 = dh1_ref[...] + r0 * (gd - un * jnp.mean(un * gd, axis=-1, keepdims=True))
        gx_ref[...] = dh0

        @pl.when(i == 0)
        def _():
            dhead_ref[...] = dh0

    any_spec = pl.BlockSpec(memory_space=pl.ANY)
    in_specs = (_halo_specs(nb) + _halo_specs(nb) + _halo_specs(nb, 2 * D, 2 * D)
                + [_row_spec(2 * D), _const_spec((32, D)), _const_spec((BR, D)), _x_spec(), _const_spec((1, D)),
                   _row_spec(), any_spec, any_spec])
    return pl.pallas_call(
        body, name="mixers_bwd_halo",
        grid=(nb,),
        in_specs=in_specs,
        out_specs=[_row_spec(D_IN), _x_spec(), _const_spec((BR, D)), _const_spec((32, D)), _const_spec((1, D))],
        out_shape=[jax.ShapeDtypeStruct((tp, D_IN), BF16), jax.ShapeDtypeStruct((seq, D), F32),
                   jax.ShapeDtypeStruct((BR, D), F32), jax.ShapeDtypeStruct((32, D), F32),
                   jax.ShapeDtypeStruct((1, D), F32)],
        scratch_shapes=[pltpu.VMEM((ns, D, wcols), BF16), pltpu.SemaphoreType.DMA((1,)),
                        pltpu.VMEM((EXT, D), F32), pltpu.VMEM((EXT, D), F32), pltpu.VMEM((EXT, D), F32),
                        pltpu.VMEM((BR, D), F32), pltpu.VMEM((8, ROT_ROWS, 128), F32), pltpu.VMEM((32, 8, D), F32)],
        compiler_params=_cparams(("arbitrary",)),
    )(dconv, dconv, dconv, dm, dm, dm, z, z, z, dzg, w_dw, head, x, g_mix, dh1, w_in_b, dep)


def _wgrad(a, c, tm, tn, tk, name, diag=False, col_major=False, dep=None):
    tp, m = a.shape
    n = c.shape[1]
    nk = tp // tk
    gm, gn = m // tm, n // tn

    def body(a_ref, c_ref, *rest):
        o_ref, ob_ref = rest[-2:]
        k = pl.program_id(2)

        @pl.when(k == 0)
        def _():
            o_ref[...] = jnp.zeros_like(o_ref)

        o_ref[...] += _dot_tn(a_ref[...], c_ref[...])

        @pl.when(k == nk - 1)
        def _():
            ob_ref[...] = o_ref[...].astype(BF16)

    c_map = lambda i, j, k: (k, j)
    grid = (gm, gn, nk)
    deps = [] if dep is None else [dep]
    if diag:
        grid = (gm, 1, nk)
        c_map = lambda i, j, k: (k, i)
        o_spec = pl.BlockSpec((tm, tn), lambda i, j, k: (i, 0))
        o_shape = (m, tn)
    elif col_major:
        o_spec = pl.BlockSpec((None, tm, tn), lambda i, j, k: (j, i, 0))
        o_shape = (gn, m, tn)
    else:
        o_spec = pl.BlockSpec((tm, tn), lambda i, j, k: (i, j))
        o_shape = (m, n)
    return pl.pallas_call(
        body, name=name,
        grid=grid,
        in_specs=[pl.BlockSpec((tk, tm), lambda i, j, k: (k, i)), pl.BlockSpec((tk, tn), c_map)]
        + [pl.BlockSpec(memory_space=pl.ANY)] * len(deps),
        out_specs=[o_spec, o_spec],
        out_shape=[jax.ShapeDtypeStruct(o_shape, F32), jax.ShapeDtypeStruct(o_shape, BF16)],
        compiler_params=_cparams(("arbitrary", "arbitrary", "arbitrary")),
    )(a, c, *deps)


def _place():
    x, y, c = lax.axis_index("x"), lax.axis_index("y"), lax.axis_index("c")
    others = [(1 - x, y), (x, 1 - y), (1 - x, 1 - y)]
    return x, y, c, others


def _split2(a, axis=0):
    return a.reshape(a.shape[:axis] + (2, a.shape[axis] // 2) + a.shape[axis + 1:])


def _merge2(a, axis=0):
    return a.reshape(a.shape[:axis] + (2 * a.shape[axis + 1],) + a.shape[axis + 2:])


def _cast_into_slot(shards, chip, dep, name):
    n = len(shards)
    r, c = shards[0].shape
    r2 = r // 2

    def body(chip_ref, *refs):
        for a in range(n):
            refs[n + 1 + a][...] = refs[a][...].astype(BF16)

    out = pl.pallas_call(
        body, name=name,
        grid_spec=pltpu.PrefetchScalarGridSpec(
            num_scalar_prefetch=1, grid=(2,),
            in_specs=[pl.BlockSpec((r2, c), lambda h, chip_ref: (h, 0))] * n + [pl.BlockSpec(memory_space=pl.ANY)],
            out_specs=[pl.BlockSpec((None, None, r2, c), lambda h, chip_ref: (chip_ref[0], h, 0, 0))] * n),
        out_shape=[jax.ShapeDtypeStruct((N_SHARD, 2, r2, c), BF16)] * n,
        compiler_params=_cparams(("arbitrary",)),
    )(chip, *shards, dep)
    return list(out)


HBM_SPEC = pl.BlockSpec(memory_space=pltpu.HBM)
SEM_SPEC = pl.BlockSpec(memory_space=pltpu.SEMAPHORE)
DATAFLOW = pltpu.SideEffectType.DATAFLOW_SIDE_EFFECTING
TOKEN = jax.ShapeDtypeStruct((8, 128), F32)


def _in_hbm(a):
    return pltpu.with_memory_space_constraint(a, pltpu.HBM)


def _gather_tiny(v):
    vm = pl.BlockSpec(memory_space=pltpu.VMEM)

    def body(v_ref, out_ref, send_sems, recv_sems):
        x, y, c, others = _place()
        mine = 2 * x + y
        sends = [pltpu.make_async_remote_copy(
            src_ref=v_ref, dst_ref=out_ref.at[mine], send_sem=send_sems.at[j], recv_sem=recv_sems.at[j],
            device_id=(*chip, c), device_id_type=MESH) for j, chip in enumerate(others)]
        for cp in sends:
            cp.start()
        out_ref[mine] = v_ref[...]
        for j, chip in enumerate(others):
            landed = out_ref.at[2 * chip[0] + chip[1]]
            pltpu.make_async_remote_copy(
                src_ref=landed, dst_ref=landed, send_sem=send_sems.at[j], recv_sem=recv_sems.at[j],
                device_id=(x, y, c), device_id_type=MESH).wait_recv()
        for cp in sends:
            cp.wait_send()

    return pl.pallas_call(
        body, name="gather_tiny",
        in_specs=[vm], out_specs=vm,
        out_shape=jax.ShapeDtypeStruct((N_SHARD,) + v.shape, v.dtype),
        scratch_shapes=[pltpu.SemaphoreType.DMA((3,)), pltpu.SemaphoreType.DMA((3,))],
    )(v)


def _ici_copies(srcs, dsts, send_sems, recv_sems, started):
    x, y, c, others = _place()
    mine = 2 * x + y
    copies = []
    for a in range(len(srcs)):
        for j, chip in enumerate(others):
            there = 2 * chip[0] + chip[1]
            src, dst = srcs[a](mine, there, c), dsts[a](mine, there, c)
            if not started:
                dst = dsts[a](there, mine, c)
            copies.append(pltpu.make_async_remote_copy(
                src_ref=src, dst_ref=dst, send_sem=send_sems.at[a * 3 + j], recv_sem=recv_sems.at[a * 3 + j],
                device_id=(*chip, c), device_id_type=MESH))
    return copies


def _split_start(srcs_of, dsts_of, arrays, n_src, name, copies_of=None, n_sems=None, dep=None):
    n = len(arrays)
    n_sems = n_sems or 3 * n_src
    copies_of = copies_of or (lambda ins, ss, rs, started: _ici_copies(srcs_of(ins), dsts_of(ins), ss, rs, started))
    deps = [] if dep is None else [dep]
    nd = len(deps)

    def body(*refs):
        ins = refs[:n]
        send_sems, recv_sems = refs[n + nd], refs[n + nd + 1]
        token = refs[2 * n + nd + 2]
        for cp in copies_of(ins, send_sems, recv_sems, True):
            cp.start()
        token[...] = jnp.zeros_like(token)

    out = pl.pallas_call(
        body, name=name,
        in_specs=[HBM_SPEC] * n + [pl.BlockSpec(memory_space=pl.ANY)] * nd,
        out_specs=(SEM_SPEC, SEM_SPEC, *([HBM_SPEC] * n), pl.BlockSpec(memory_space=pltpu.VMEM)),
        out_shape=(pltpu.SemaphoreType.DMA((n_sems,)), pltpu.SemaphoreType.DMA((n_sems,)),
                   *[pltpu.HBM(a.shape, a.dtype) for a in arrays], TOKEN),
        input_output_aliases={a: 2 + a for a in range(n)},
        compiler_params=pltpu.CompilerParams(has_side_effects=DATAFLOW),
    )(*[_in_hbm(a) for a in arrays], *deps)
    return out[0], out[1], list(out[2:2 + n]), out[2 + n]


def _split_wait(srcs_of, dsts_of, send_sems, recv_sems, arrays, after, name, copies_of=None):
    n = len(arrays)
    copies_of = copies_of or (lambda ins, ss, rs, started: _ici_copies(srcs_of(ins), dsts_of(ins), ss, rs, started))

    def body(*refs):
        ins = refs[:n]
        send_sems, recv_sems = refs[n], refs[n + 1]
        for cp in copies_of(ins, send_sems, recv_sems, False):
            cp.wait_send()
            cp.wait_recv()

    return pl.pallas_call(
        body, name=name,
        in_specs=[HBM_SPEC] * n + [SEM_SPEC, SEM_SPEC] + [pl.BlockSpec(memory_space=pl.ANY)] * len(after),
        out_specs=[HBM_SPEC] * n,
        out_shape=[pltpu.HBM(a.shape, a.dtype) for a in arrays],
        input_output_aliases={a: a for a in range(n)},
        compiler_params=pltpu.CompilerParams(has_side_effects=DATAFLOW),
    )(*arrays, send_sems, recv_sems, *after)


def _gather_views(ins):
    view = [lambda frm, to, c, r=r: r.at[frm, c] for r in ins]
    return view


def _gather_start(bufs, dep, name):
    return _split_start(_gather_views, _gather_views, bufs, len(bufs), name, dep=dep)


def _gather_wait(send_sems, recv_sems, bufs, after, name):
    return _split_wait(_gather_views, _gather_views, send_sems, recv_sems, bufs, after, name)


def _forward_halves(bufs, name):
    n = len(bufs)
    any_spec = pl.BlockSpec(memory_space=pl.ANY)

    def body(*refs):
        outs = refs[n:2 * n]
        send_sems, recv_sems = refs[2 * n:]
        x, y, c, others = _place()
        copies = []
        for a in range(n):
            for j, chip in enumerate(others):
                landed = outs[a].at[2 * chip[0] + chip[1], c]
                copies.append(pltpu.make_async_remote_copy(
                    src_ref=landed, dst_ref=landed, send_sem=send_sems.at[a * 3 + j], recv_sem=recv_sems.at[a * 3 + j],
                    device_id=(x, y, 1 - c), device_id_type=MESH))
        for cp in copies:
            cp.start()
        for a in range(n):
            for j, chip in enumerate(others):
                landed = outs[a].at[2 * chip[0] + chip[1], 1 - c]
                pltpu.make_async_remote_copy(
                    src_ref=landed, dst_ref=landed, send_sem=send_sems.at[a * 3 + j], recv_sem=recv_sems.at[a * 3 + j],
                    device_id=(x, y, c), device_id_type=MESH).wait_recv()
        for cp in copies:
            cp.wait_send()

    out = pl.pallas_call(
        body, name=name,
        in_specs=[any_spec] * n, out_specs=[any_spec] * n,
        out_shape=[jax.ShapeDtypeStruct(b.shape, b.dtype) for b in bufs],
        input_output_aliases={a: a for a in range(n)},
        scratch_shapes=[pltpu.SemaphoreType.DMA((3 * n,)), pltpu.SemaphoreType.DMA((3 * n,))],
    )(*bufs)
    return [_merge2(o, 1) for o in out]


def _swap_halves_bf16(gbs, name):
    n = len(gbs)
    any_spec = pl.BlockSpec(memory_space=pl.ANY)

    def body(*refs):
        ins, outs = refs[:n], refs[n:2 * n]
        send_sems, recv_sems = refs[2 * n:]
        x, y, c, _ = _place()
        copies = []
        for a in range(n):
            copies.append(pltpu.make_async_remote_copy(
                src_ref=ins[a].at[:, 1 - c], dst_ref=outs[a], send_sem=send_sems.at[a], recv_sem=recv_sems.at[a],
                device_id=(x, y, 1 - c), device_id_type=MESH))
        for cp in copies:
            cp.start()
        for cp in copies:
            cp.wait()

    return pl.pallas_call(
        body, name=name,
        in_specs=[any_spec] * n, out_specs=[any_spec] * n,
        out_shape=[jax.ShapeDtypeStruct((g.shape[0], g.shape[1] // 2, g.shape[2]), g.dtype) for g in gbs],
        scratch_shapes=[pltpu.SemaphoreType.DMA((n,)), pltpu.SemaphoreType.DMA((n,))],
    )(*[_split2(g, 1) for g in gbs])


def _scatter_srcs(n):
    return lambda ins: [lambda frm, to, c, r=r: r.at[to] for r in ins[:n]]


def _scatter_dsts(n):
    return lambda ins: [lambda frm, to, c, r=r: r.at[frm] for r in ins[n:]]


def _scatter_start(hbs, name):
    n = len(hbs)
    lands = [lax.empty(h.shape, h.dtype) for h in hbs]
    return _split_start(_scatter_srcs(n), _scatter_dsts(n), list(hbs) + lands, n, name)


def _scatter_wait(send_sems, recv_sems, arrays, after, name):
    n = len(arrays) // 2
    return _split_wait(_scatter_srcs(n), _scatter_dsts(n), send_sems, recv_sems, arrays, after, name)[n:]


def _join_halves(rhs, name):
    n = len(rhs)
    any_spec = pl.BlockSpec(memory_space=pl.ANY)

    def body(*refs):
        outs = refs[n:2 * n]
        send_sems, recv_sems = refs[2 * n:]
        x, y, c, _ = _place()
        copies = []
        for a in range(n):
            copies.append(pltpu.make_async_remote_copy(
                src_ref=outs[a].at[c], dst_ref=outs[a].at[c], send_sem=send_sems.at[a],
                recv_sem=recv_sems.at[a], device_id=(x, y, 1 - c), device_id_type=MESH))
        for cp in copies:
            cp.start()
        for a in range(n):
            landed = outs[a].at[1 - c]
            pltpu.make_async_remote_copy(
                src_ref=landed, dst_ref=landed, send_sem=send_sems.at[a], recv_sem=recv_sems.at[a],
                device_id=(x, y, c), device_id_type=MESH).wait_recv()
        for cp in copies:
            cp.wait_send()

    out = pl.pallas_call(
        body, name=name,
        in_specs=[any_spec] * n, out_specs=[any_spec] * n,
        out_shape=[jax.ShapeDtypeStruct(r.shape, r.dtype) for r in rhs],
        input_output_aliases={a: a for a in range(n)},
        scratch_shapes=[pltpu.SemaphoreType.DMA((n,)), pltpu.SemaphoreType.DMA((n,))],
    )(*rhs)
    return [_merge2(o) for o in out]


FLIPS = [(dx, dy, dc) for dx in (0, 1) for dy in (0, 1) for dc in (0, 1)][1:]


def _peer_copies(ins, send_sems, recv_sems, started):
    x, y, c, _ = _place()
    copies = []
    for k, (dx, dy, dc) in enumerate(FLIPS):
        px, py, pc = jnp.bitwise_xor(x, dx), jnp.bitwise_xor(y, dy), jnp.bitwise_xor(c, dc)
        slot = 4 * x + 2 * y + c if started else 4 * px + 2 * py + pc
        copies.append(pltpu.make_async_remote_copy(
            src_ref=ins[0], dst_ref=ins[1].at[slot], send_sem=send_sems.at[k], recv_sem=recv_sems.at[k],
            device_id=(px, py, pc), device_id_type=MESH))
    return copies


def _small_start(v, name):
    land = lax.empty((8,) + v.shape, v.dtype)
    return _split_start(None, None, [v, land], 0, name, copies_of=_peer_copies, n_sems=len(FLIPS))


def _small_wait(send_sems, recv_sems, arrays, after, name):
    return _split_wait(None, None, send_sems, recv_sems, arrays, after, name, copies_of=_peer_copies)[1]


def _sum_slots(land, v, me):
    rows, cols = v.shape

    def body(me_ref, land_ref, v_ref, o_ref):
        o_ref[...] = jnp.zeros_like(o_ref)
        for d in range(8):
            @pl.when(me_ref[0] == d)
            def _():
                o_ref[...] += v_ref[...]

            @pl.when(me_ref[0] != d)
            def _():
                o_ref[...] += land_ref[d]

    return pl.pallas_call(
        body, name="sum_slots",
        grid_spec=pltpu.PrefetchScalarGridSpec(
            num_scalar_prefetch=1, grid=(1,),
            in_specs=[pl.BlockSpec((8, rows, cols), lambda i, me_ref: (0, 0, 0)),
                      pl.BlockSpec((rows, cols), lambda i, me_ref: (0, 0))],
            out_specs=pl.BlockSpec((rows, cols), lambda i, me_ref: (0, 0))),
        out_shape=jax.ShapeDtypeStruct((rows, cols), F32),
        compiler_params=_cparams(("arbitrary",)),
    )(me, land, v)


def _by_shape(arrays):
    groups = {}
    for k, a in enumerate(arrays):
        groups.setdefault(a.shape, []).append(k)
    return list(groups.values())


def _add_sibling_half(gs, sbs, idx, name):
    n = len(gs)
    ns, r, c = gs[0].shape
    r2 = r // 2

    def body(idx_ref, *refs):
        for a in range(n):
            g_ref, sb_ref, hown_ref, hb_ref = refs[a], refs[n + a], refs[2 * n + a], refs[3 * n + a]
            h = g_ref[...] + sb_ref[...].astype(F32)
            hb_ref[...] = h.astype(BF16)

            @pl.when(pl.program_id(0) == idx_ref[0])
            def _():
                hown_ref[...] = h

    spec = pl.BlockSpec((None, r2, c), lambda s, idx_ref: (s, 0, 0))
    out = pl.pallas_call(
        body, name=name,
        grid_spec=pltpu.PrefetchScalarGridSpec(
            num_scalar_prefetch=1, grid=(ns,),
            in_specs=[pl.BlockSpec((None, r2, c), lambda s, idx_ref: (s, idx_ref[4], 0))] * n + [spec] * n,
            out_specs=[pl.BlockSpec((r2, c), lambda s, idx_ref: (0, 0))] * n + [spec] * n),
        out_shape=[jax.ShapeDtypeStruct((r2, c), F32)] * n + [jax.ShapeDtypeStruct((ns, r2, c), BF16)] * n,
        compiler_params=_cparams(("arbitrary",)),
    )(idx, *gs, *sbs)
    return [(out[a], out[n + a]) for a in range(n)]


def _add_chip_slabs(hs, rbs, idx, name):
    n = len(hs)
    r2, c = hs[0].shape

    def body(idx_ref, *refs):
        for a in range(n):
            h_ref, r0_ref, r1_ref, r2_ref = refs[4 * a:4 * a + 4]
            refs[4 * n + a][...] = ((h_ref[...] + r0_ref[...].astype(F32)) + r1_ref[...].astype(F32)) + r2_ref[...].astype(F32)

    def pick(k):
        return pl.BlockSpec((None, r2, c), lambda i, idx_ref: (idx_ref[k], 0, 0))

    operands = []
    for h, rb in zip(hs, rbs):
        operands += [h, rb, rb, rb]
    out = pl.pallas_call(
        body, name=name,
        grid_spec=pltpu.PrefetchScalarGridSpec(
            num_scalar_prefetch=1, grid=(1,),
            in_specs=[pl.BlockSpec((r2, c), lambda i, idx_ref: (0, 0)), pick(1), pick(2), pick(3)] * n,
            out_specs=[pl.BlockSpec((None, r2, c), lambda i, idx_ref: (idx_ref[4], 0, 0))] * n),
        out_shape=[jax.ShapeDtypeStruct((2, r2, c), F32)] * n,
        compiler_params=_cparams(("arbitrary",)),
    )(idx, *operands)
    return list(out)


ELEMENTWISE_VMEM = 16 * 1024 * 1024


def _adamw(items, name):
    n = len(items)
    r, c = items[0][0].shape
    br = max(b for b in range(8, r + 1, 8) if r % b == 0 and n * 16 * b * c * 4 <= ELEMENTWISE_VMEM) if r % 8 == 0 else r

    def body(*refs):
        for a in range(n):
            g_ref, w_ref, m_ref, v_ref = refs[4 * a:4 * a + 4]
            go_ref, d_ref, nm_ref, nv_ref = refs[4 * n + 4 * a:4 * n + 4 * a + 4]
            gg = g_ref[...]
            go_ref[...] = gg
            nm = B1 * m_ref[...] + (1.0 - B1) * gg
            nv = B2 * v_ref[...] + (1.0 - B2) * jnp.square(gg)
            m_hat = nm / (1.0 - B1 ** STEP)
            v_hat = nv / (1.0 - B2 ** STEP)
            d_ref[...] = -LR * (m_hat / (jnp.sqrt(v_hat) + ADAM_EPS) + WD * w_ref[...])
            nm_ref[...] = nm
            nv_ref[...] = nv

    spec = pl.BlockSpec((br, c), lambda i: (i, 0))
    out = pl.pallas_call(
        body, name=name,
        grid=(r // br,),
        in_specs=[spec] * (4 * n), out_specs=[spec] * (4 * n),
        out_shape=[jax.ShapeDtypeStruct((r, c), F32)] * (4 * n),
        compiler_params=_cparams(("arbitrary",)),
    )(*[a for item in items for a in item])
    return [tuple(out[4 * a:4 * a + 4]) for a in range(n)]


BIG = ("w_in", "w_conv_out", "w_pool", "w_pool_out", "w_o", "w_ffn_gate", "w_ffn_up", "w_ffn_down")
REPL = ("g_mix", "b_gate", "b_dw", "ln_g", "ln_b", "pool_scale", "g_ffn", "g_final")
GROUP_MIX = ("w_conv_out", "w_pool", "w_pool_out", "w_o")
GROUP_FFN = ("w_ffn_gate", "w_ffn_up", "w_ffn_down")
TRANSPOSED = ("w_ffn_gate", "w_ffn_up")
WEIGHT_ORDER = ("meta_tokens", "g_mix", "w_in", "b_gate", "w_dw", "b_dw", "ln_g", "ln_b", "w_conv_out", "w_pool",
                "pool_scale", "w_pool_out", "w_o", "g_ffn", "w_ffn_gate", "w_ffn_up", "w_ffn_down", "g_final")


def _shard2d(name, a):
    a = a[0]
    if name == "w_pool":
        return a.reshape(4 * 64, GD)
    if name in TRANSPOSED:
        return a.T
    return a


def _unshard2d(name, a, shape):
    return a.T.reshape(shape) if name in TRANSPOSED else a.reshape(shape)


def _slabs_to_cols(a):
    ns, m, c = a.shape
    return a.transpose(1, 0, 2).reshape(m, ns * c)


def kernel(x, meta_tokens, g_mix, w_in, b_gate, w_dw, b_dw, ln_g, ln_b, w_conv_out, w_pool, pool_scale, w_pool_out, w_o, g_ffn, w_ffn_gate, w_ffn_up, w_ffn_down, g_final, loss_target, m_meta_tokens, m_g_mix, m_w_in, m_b_gate, m_w_dw, m_b_dw, m_ln_g, m_ln_b, m_w_conv_out, m_w_pool, m_pool_scale, m_w_pool_out, m_w_o, m_g_ffn, m_w_ffn_gate, m_w_ffn_up, m_w_ffn_down, m_g_final, v_meta_tokens, v_g_mix, v_w_in, v_b_gate, v_w_dw, v_b_dw, v_ln_g, v_ln_b, v_w_conv_out, v_w_pool, v_pool_scale, v_w_pool_out, v_w_o, v_g_ffn, v_w_ffn_gate, v_w_ffn_up, v_w_ffn_down, v_g_final):
    args = dict(locals())
    w = {n: args[n] for n in WEIGHT_ORDER}
    mom = {n: args["m_" + n] for n in WEIGHT_ORDER}
    var = {n: args["v_" + n] for n in WEIGHT_ORDER}
    seq = x.shape[1]
    nb = seq // BR + 1
    tp = nb * BR
    tk = tp // 2 if (tp // 2) % 16 == 0 else BR
    t_total = seq + N_META
    cx, cy, cc = lax.axis_index("x"), lax.axis_index("y"), lax.axis_index("c")
    chip = 2 * cx + cy
    chip1 = jnp.reshape(chip, (1,)).astype(jnp.int32)
    core = jnp.reshape(cc, (1,)).astype(jnp.int32)
    others = jnp.sort(jnp.stack([2 * (1 - cx) + cy, 2 * cx + (1 - cy), 2 * (1 - cx) + (1 - cy)]))
    idx = jnp.concatenate([chip1, others.astype(jnp.int32), core])
    xs, target = x[0], loss_target[0]

    tiny = _gather_tiny(jnp.concatenate([w["meta_tokens"], w["w_dw"][0], jnp.zeros((1, GD), F32)], axis=0))
    small = {n: w[n] for n in REPL if n != "g_final"}
    small["g_final"] = w["g_final"].reshape(1, D)
    small["w_dw"] = _slabs_to_cols(tiny[:, N_META:])
    head = jnp.concatenate([jnp.zeros((PAD, D), F32), _slabs_to_cols(tiny[:, :N_META])], axis=0)

    def cast(group, dep):
        shards = [_shard2d(n, w[n]) for n in group]
        bufs = [None] * len(group)
        for ks in _by_shape(shards):
            done = _cast_into_slot([shards[k] for k in ks], chip1, dep, "cast_" + group[ks[0]])
            for k, b in zip(ks, done):
                bufs[k] = b
        return bufs

    def gather_finish(group, start, after, name):
        landed = _gather_wait(start[0], start[1], start[2], after, "gather_wait_" + name)
        return dict(zip(group, _forward_halves(landed, "forward_" + name)))

    st_in = _gather_start(cast(("w_in",), tiny), None, "gather_start_in")
    bufs_mix, bufs_ffn = cast(GROUP_MIX, st_in[3]), cast(GROUP_FFN, st_in[3])
    u = _rms_u(head, xs, small["g_mix"] + st_in[3][0:1, 0:1], nb)
    z_own = _in_proj_own(u, w["w_in"][0], idx, nb)
    gw = gather_finish(("w_in",), st_in, [z_own] + bufs_mix + bufs_ffn, "in")
    st_mix = _gather_start(bufs_mix, gw["w_in"], "gather_start_mix")
    z = _in_proj_rest(u, gw["w_in"], z_own, idx, st_mix[3], nb)
    gw.update(gather_finish(GROUP_MIX, st_mix, [z], "mix"))
    st_ffn = _gather_start(bufs_ffn, gw["w_o"], "gather_start_ffn")
    w_pool_b = gw["w_pool"].reshape(N_SHARD, 4, 64, GD).transpose(1, 0, 2, 3).reshape(4, GD, GD)
    w_co_b, w_po_b, w_o_b = (gw[n].reshape(D, D) for n in ("w_conv_out", "w_pool_out", "w_o"))
    h1, mg, ca, cpre, m, m2b = _mixers_fwd(
        z, head, xs, small["b_gate"] + st_ffn[3][0, 0], small["w_dw"], small["b_dw"], small["ln_g"], small["ln_b"],
        small["pool_scale"], w_co_b, w_pool_b, w_po_b, w_o_b, nb, t_total)
    gw.update(gather_finish(GROUP_FFN, st_ffn, [h1], "ffn"))

    dh1, dh1b, vb, fb, dgb, dub, dh2b, loss, dg_ffn, dg_final = _ffn_fwd_bwd(
        h1, target, small["g_ffn"], small["g_final"], gw["w_ffn_gate"].reshape(D_FF, D),
        gw["w_ffn_up"].reshape(D_FF, D), gw["w_ffn_down"].reshape(D_FF, D), nb)

    def slabs(name, g):
        if name == "w_in":
            return g
        if name == "w_pool":
            return g.reshape(4, N_SHARD, 64, GD).transpose(1, 0, 2, 3).reshape(N_SHARD, 4 * 64, GD)
        return g.reshape(N_SHARD, g.shape[0] // N_SHARD, g.shape[1])

    def reduce_start(group, grads, name):
        g32 = [slabs(n, grads[n][0]) for n in group]
        g16 = [slabs(n, grads[n][1]) for n in group]
        from_sibling = _swap_halves_bf16(g16, "swap_halves_" + name)
        halves = [None] * len(group)
        for ks in _by_shape(g32):
            done = _add_sibling_half([g32[k] for k in ks], [from_sibling[k] for k in ks], idx, "add_sibling_" + group[ks[0]])
            for k, pair in zip(ks, done):
                halves[k] = pair
        return [h for h, _ in halves], _scatter_start([hb for _, hb in halves], "scatter_start_" + name)

    def reduce_finish(group, halves, start, after, name):
        from_chips = _scatter_wait(start[0], start[1], start[2], after, "scatter_wait_" + name)
        reduced = [None] * len(group)
        for ks in _by_shape(halves):
            done = _add_chip_slabs([halves[k] for k in ks], [from_chips[k] for k in ks], idx, "add_chips_" + group[ks[0]])
            for k, r in zip(ks, done):
                reduced[k] = r
        return reduced

    half_ff = D_FF // 2
    grads_ffn = {
        "w_ffn_gate": _wgrad(dgb, vb, half_ff, D, tk, "wgrad_ffn_gate"),
        "w_ffn_up": _wgrad(dub, vb, half_ff, D, tk, "wgrad_ffn_up"),
        "w_ffn_down": _wgrad(fb, dh2b, half_ff, D, tk, "wgrad_ffn_down"),
    }
    halves_ffn, sc_ffn = reduce_start(GROUP_FFN, grads_ffn, "ffn")

    dycb, dypb, dzg, dconv, dmwb, dm, db_gate, dln_g, dln_b, db_dw, dps = _mixers_bwd_rows(
        dh1b, ca, m2b, z, small["b_gate"], cpre, small["ln_g"], small["ln_b"], m, small["pool_scale"],
        w_o_b, w_co_b, w_po_b, w_pool_b, sc_ffn[3], nb)
    grads_mix = {
        "w_conv_out": _wgrad(ca, dycb, D, D, tk, "wgrad_conv_out"),
        "w_pool": _wgrad(m, dmwb, GD, GD, tp, "wgrad_pool", diag=True),
        "w_pool_out": _wgrad(m2b, dypb, D, D, tk, "wgrad_pool_out"),
        "w_o": _wgrad(mg, dh1b, D, D, tk, "wgrad_o"),
    }
    halves_mix, sc_mix = reduce_start(GROUP_MIX, grads_mix, "mix")
    dzb, grad_x, dhead, dw_dw, dg_mix = _mixers_bwd_halo(
        dconv, dm, z, dzg, small["w_dw"], head, xs, small["g_mix"], dh1, gw["w_in"], sc_mix[3], nb, t_total)
    packed = jnp.concatenate(
        [dg_mix, db_gate.reshape(2, D), db_dw, dln_g, dln_b, dps, dg_ffn, dg_final,
         jnp.broadcast_to(loss, (1, D)), jnp.zeros((6, D), F32), dhead[PAD:], dw_dw], axis=0)
    sm = _small_start(packed, "small_start")
    grads_in = {"w_in": _wgrad(u, dzb, D, D_IN // N_SHARD, tk, "wgrad_in", col_major=True, dep=sm[3])}
    halves_in, sc_in = reduce_start(("w_in",), grads_in, "in")

    land = _small_wait(sm[0], sm[1], sm[2], [sc_in[3]], "small_wait")
    summed = _sum_slots(land, packed, jnp.reshape(4 * cx + 2 * cy + cc, (1,)).astype(jnp.int32))
    loss = summed[9, 0]

    first = GROUP_FFN + GROUP_MIX
    reduced_half = reduce_finish(GROUP_FFN, halves_ffn, sc_ffn, [summed], "ffn")
    reduced_half += reduce_finish(GROUP_MIX, halves_mix, sc_mix, [summed], "mix")
    reduced = dict(zip(first, _join_halves(reduced_half, "join_halves_first")))
    updates = {}
    for ks in _by_shape([reduced[n] for n in first]):
        names = [first[k] for k in ks]
        done = _adamw([(reduced[n], _shard2d(n, w[n]), _shard2d(n, mom[n]), _shard2d(n, var[n])) for n in names],
                      "adamw_" + names[0])
        updates.update(zip(names, done))

    def repl_stack(d):
        return jnp.concatenate([d["g_mix"], d["b_gate"].reshape(2, D), d["b_dw"], d["ln_g"], d["ln_b"],
                                d["pool_scale"], d["g_ffn"], d["g_final"].reshape(1, D), jnp.ones((7, D), F32)], axis=0)

    def shard_stack(d):
        return jnp.concatenate([d["meta_tokens"], d["w_dw"][0], jnp.ones((1, GD), F32)], axis=0)

    g_repl = summed[0:16]
    g_shard = lax.dynamic_slice_in_dim(summed[16:64], chip * GD, GD, axis=1)
    g_repl, d_repl, m_repl, v_repl = _adamw([(g_repl, repl_stack(w), repl_stack(mom), repl_stack(var))], "adamw_repl")[0]
    g_shard, d_shard, m_shard, v_shard = _adamw(
        [(g_shard, shard_stack(w), shard_stack(mom), shard_stack(var))], "adamw_cols")[0]

    done_first = [updates[n][1] for n in first] + [d_repl, d_shard]
    last_half = reduce_finish(("w_in",), halves_in, sc_in, done_first, "in")
    reduced["w_in"] = _join_halves(last_half, "join_halves_in")[0]
    updates["w_in"] = _adamw([(reduced["w_in"], w["w_in"][0], mom["w_in"][0], var["w_in"][0])], "adamw_w_in")[0]

    def unpack(name, repl, shard):
        if name == "meta_tokens":
            return shard[0:N_META]
        if name == "w_dw":
            return shard[N_META:N_META + KW].reshape(1, KW, GD)
        row = {"g_mix": 0, "b_gate": 1, "b_dw": 3, "ln_g": 4, "ln_b": 5, "pool_scale": 6, "g_ffn": 7, "g_final": 8}[name]
        if name == "b_gate":
            return repl[1:3].reshape(1, 2 * D)
        if name == "g_final":
            return repl[8]
        return repl[row:row + 1]

    out_g, out_d, out_m, out_v = {}, {}, {}, {}
    for n in WEIGHT_ORDER:
        if n in BIG:
            g, d_, m_, v_ = updates[n]
            shape = w[n].shape
            out_g[n], out_d[n], out_m[n], out_v[n] = (_unshard2d(n, a, shape) for a in (g, d_, m_, v_))
        else:
            out_g[n] = unpack(n, g_repl, g_shard)
            out_d[n] = unpack(n, d_repl, d_shard)
            out_m[n] = unpack(n, m_repl, m_shard)
            out_v[n] = unpack(n, v_repl, v_shard)
    return (loss, grad_x[None], *[out_g[n] for n in WEIGHT_ORDER], *[out_d[n] for n in WEIGHT_ORDER],
            *[out_m[n] for n in WEIGHT_ORDER], *[out_v[n] for n in WEIGHT_ORDER])
```

```python
import jax
import jax.numpy as jnp
from jax import lax
from jax.experimental import pallas as pl
from jax.experimental.pallas import tpu as pltpu

F32 = jnp.float32
BF16 = jnp.bfloat16
MESH = pl.DeviceIdType.MESH

D = 1024
N_META = 16
KW = 31
POOL_WINDOWS = (2, 4, 8, 16)
GD = 256
D_IN = 5 * D
D_FF = 2816
N_SHARD = 4
BR = 256
HALO = 16
PAD = BR - N_META
EXT = BR + 2 * HALO
RMS_EPS = 1e-6
LN_EPS = 1e-5
LR, B1, B2, ADAM_EPS, WD, STEP = 0.001, 0.9, 0.999, 1e-08, 0.01, 10
VMEM_LIMIT = 56 * 1024 * 1024


def _cparams(sem, vmem=VMEM_LIMIT):
    return pltpu.CompilerParams(dimension_semantics=sem, vmem_limit_bytes=vmem)


def _dot(a, b):
    return jnp.dot(a, b, preferred_element_type=F32)


def _dot_nt(a, b):
    return lax.dot_general(a, b, (((1,), (1,)), ((), ())), preferred_element_type=F32)


def _dot_tn(a, b):
    return lax.dot_general(a, b, (((0,), (0,)), ((), ())), preferred_element_type=F32)


def _sigmoid(x):
    return 0.5 * jnp.tanh(0.5 * x) + 0.5


def _row_ids(i, n, offset=0):
    return lax.broadcasted_iota(jnp.int32, (n, 1), 0) + (i * BR + offset - PAD)


def _pool_cnt(t, w, t_total):
    left = w // 2
    right = w - 1 - left
    lo = jnp.clip(t - left, 0, t_total)
    hi = jnp.clip(t + right + 1, 0, t_total)
    return jnp.maximum(hi - lo, 1).astype(F32)


def _halo_specs(nb, halo_width=D, width=D):
    last = nb * (BR // HALO) - 1
    return [
        pl.BlockSpec((HALO, halo_width), lambda i: (jnp.maximum(i * (BR // HALO) - 1, 0), 0)),
        pl.BlockSpec((BR, width), lambda i: (i, 0)),
        pl.BlockSpec((HALO, halo_width), lambda i: (jnp.minimum((i + 1) * (BR // HALO), last), 0)),
    ]


def _cols(ref, n):
    return [ref.at[:, k * D:(k + 1) * D] for k in range(n)]


def _fill_ext(ext_ref, prev, cur, nxt, i, nb):
    ext_ref[0:HALO, :] = jnp.where(i > 0, prev, 0.0)
    ext_ref[HALO:HALO + BR, :] = cur
    ext_ref[HALO + BR:EXT, :] = jnp.where(i < nb - 1, nxt, 0.0)


ROT_ROWS = EXT - 8


def _fill_rot(rot_ref, ext_ref, lanes):
    for r in range(1, 8):
        rot_ref[r] = ext_ref[pl.ds(r, ROT_ROWS), lanes]


def _tap(rot_ref, ext_ref, lanes, offset):
    q, r = divmod(offset, 8)
    if r == 0:
        return ext_ref[pl.ds(8 * q, BR), lanes]
    return rot_ref[r, pl.ds(8 * q, BR), :]


def _row_spec(width=D):
    return pl.BlockSpec((BR, width), lambda i: (i, 0))


def _x_spec():
    return pl.BlockSpec((BR, D), lambda i: (jnp.maximum(i - 1, 0), 0))


def _const_spec(shape):
    nd = len(shape)
    return pl.BlockSpec(shape, lambda i: (0,) * nd)


def _rms_u(head, x, g_mix, nb):
    def body(head_ref, x_ref, g_ref, u_ref):
        i = pl.program_id(0)
        h = jnp.where(i == 0, head_ref[...], x_ref[...])
        r = lax.rsqrt(jnp.mean(h * h, axis=-1, keepdims=True) + RMS_EPS)
        u_ref[...] = ((h * r) * g_ref[...]).astype(BF16)

    return pl.pallas_call(
        body, name="rms_u",
        grid=(nb,),
        in_specs=[_const_spec((BR, D)), _x_spec(), _const_spec((1, D))],
        out_specs=_row_spec(),
        out_shape=jax.ShapeDtypeStruct((nb * BR, D), BF16),
        compiler_params=_cparams(("arbitrary",)),
    )(head, x, g_mix)


def _in_proj_rows(tp):
    return tp // 4 if (tp // 4) % 16 == 0 else BR


def _in_proj_own(u, w_own, idx, nb):
    tp = nb * BR
    wcols = w_own.shape[1]
    rows = _in_proj_rows(tp)

    def body(idx_ref, u_ref, w_ref, z_ref, wb_ref):
        @pl.when(pl.program_id(0) == 0)
        def _():
            wb_ref[...] = w_ref[...].astype(BF16)

        z_ref[...] = _dot(u_ref[...], wb_ref[...])

    return pl.pallas_call(
        body, name="in_proj_own",
        grid_spec=pltpu.PrefetchScalarGridSpec(
            num_scalar_prefetch=1, grid=(tp // rows,),
            in_specs=[pl.BlockSpec((rows, D), lambda i, idx_ref: (i, 0)),
                      pl.BlockSpec((D, wcols), lambda i, idx_ref: (0, 0))],
            out_specs=pl.BlockSpec((rows, wcols), lambda i, idx_ref: (i, idx_ref[0])),
            scratch_shapes=[pltpu.VMEM((D, wcols), BF16)]),
        out_shape=jax.ShapeDtypeStruct((tp, N_SHARD * wcols), F32),
        compiler_params=_cparams(("arbitrary",)),
    )(idx, u, w_own)


def _in_proj_rest(u, w_in_b, z, idx, dep, nb):
    tp = nb * BR
    wcols = w_in_b.shape[2]
    rows = _in_proj_rows(tp)

    def body(idx_ref, u_ref, w_ref, z_in, dep_ref, z_ref):
        z_ref[...] = _dot(u_ref[...], w_ref[...])

    any_spec = pl.BlockSpec(memory_space=pl.ANY)
    return pl.pallas_call(
        body, name="in_proj_rest",
        grid_spec=pltpu.PrefetchScalarGridSpec(
            num_scalar_prefetch=1, grid=(N_SHARD - 1, tp // rows),
            in_specs=[pl.BlockSpec((rows, D), lambda s, i, idx_ref: (i, 0)),
                      pl.BlockSpec((None, D, wcols), lambda s, i, idx_ref: (idx_ref[1 + s], 0, 0)),
                      any_spec, any_spec],
            out_specs=pl.BlockSpec((rows, wcols), lambda s, i, idx_ref: (i, idx_ref[1 + s]))),
        out_shape=jax.ShapeDtypeStruct(z.shape, F32),
        input_output_aliases={3: 0},
        compiler_params=_cparams(("arbitrary", "arbitrary")),
    )(idx, u, w_in_b, z, dep)


def _mixers_fwd(z, head, x, b_gate, w_dw, b_dw, ln_g, ln_b, pool_scale, w_co, w_pool, w_po, w_o, nb, t_total):
    tp = nb * BR

    def body(z_prev, z_cur, z_next, head_ref, x_ref, bg_ref, wdw_ref, bdw_ref,
             lng_ref, lnb_ref, ps_ref, wco_ref, wpool_ref, wpo_ref, wo_ref,
             h1_ref, yc_ref, yp_ref, mg_ref, ca_ref, cpre_ref, m_ref, mw_ref, m2b_ref, ext_ref, pext_ref, rot_ref):
        i = pl.program_id(0)
        avp, agp, pp = _cols(z_prev, 3)
        av, ag, pc, za, zb = _cols(z_cur, 5)
        avn, agn, pn = _cols(z_next, 3)
        _fill_ext(ext_ref, avp[...] * _sigmoid(agp[...]), av[...] * _sigmoid(ag[...]),
                  avn[...] * _sigmoid(agn[...]), i, nb)
        _fill_ext(pext_ref, pp[...], pc[...], pn[...], i, nb)

        def conv_chunk(c, carry):
            lanes = pl.ds(pl.multiple_of(c * 128, 128), 128)
            _fill_rot(rot_ref, ext_ref, lanes)
            acc = jnp.broadcast_to(bdw_ref[:, lanes], (BR, 128))
            for k in range(KW):
                acc = acc + wdw_ref[k:k + 1, lanes] * _tap(rot_ref, ext_ref, lanes, 1 + k)
            cpre_ref[:, lanes] = acc
            return carry
        lax.fori_loop(0, D // 128, conv_chunk, 0)

        conv = cpre_ref[...]
        mu = jnp.mean(conv, axis=-1, keepdims=True)
        xc = conv - mu
        rstd = lax.rsqrt(jnp.mean(xc * xc, axis=-1, keepdims=True) + LN_EPS)
        ln = (xc * rstd) * lng_ref[...] + lnb_ref[...]
        cact = (ln * _sigmoid(ln)).astype(BF16)
        ca_ref[...] = cact
        y_conv = _dot(cact, wco_ref[...])
        yc_ref[...] = y_conv

        t = _row_ids(i, BR)
        for gi, w in enumerate(POOL_WINDOWS):
            left = w // 2
            right = w - 1 - left
            lanes = slice(gi * GD, (gi + 1) * GD)
            s = pext_ref[pl.ds(HALO - left, BR), lanes]
            for j in range(-left + 1, right + 1):
                s = s + pext_ref[pl.ds(HALO + j, BR), lanes]
            m = (s / _pool_cnt(t, w, t_total) - pext_ref[HALO:HALO + BR, lanes]).astype(BF16)
            m_ref[:, lanes] = m
            mw_ref[:, lanes] = _dot(m, wpool_ref[gi])
        mw = mw_ref[...]
        m2b = (mw * ps_ref[...]).astype(BF16)
        m2b_ref[...] = m2b
        y_pool = _dot(m2b, wpo_ref[...])
        yp_ref[...] = y_pool

        s_a = _sigmoid(za[...] + bg_ref[:, 0:D])
        s_b = _sigmoid(zb[...] + bg_ref[:, D:2 * D])
        merged = (s_a * y_conv + s_b * y_pool).astype(BF16)
        mg_ref[...] = merged
        h0 = jnp.where(i == 0, head_ref[...], x_ref[...])
        h1_ref[...] = h0 + _dot(merged, wo_ref[...])

    in_specs = (_halo_specs(nb, 3 * D, 5 * D)
                + [_const_spec((BR, D)), _x_spec(), _const_spec((1, 2 * D)), _const_spec((32, D)),
                   _const_spec((1, D)), _const_spec((1, D)), _const_spec((1, D)), _const_spec((1, D)),
                   _const_spec((D, D)), _const_spec((4, GD, GD)), _const_spec((D, D)), _const_spec((D, D))])
    outs = [(F32, "h1"), (F32, "yc"), (F32, "yp"), (BF16, "mg"), (BF16, "ca"), (F32, "cpre"), (BF16, "m"), (F32, "mw"),
            (BF16, "m2b")]
    return pl.pallas_call(
        body, name="mixers_fwd",
        grid=(nb,),
        in_specs=in_specs,
        out_specs=[_row_spec() for _ in outs],
        out_shape=[jax.ShapeDtypeStruct((tp, D), dt) for dt, _ in outs],
        scratch_shapes=[pltpu.VMEM((EXT, D), F32), pltpu.VMEM((EXT, D), F32), pltpu.VMEM((8, ROT_ROWS, 128), F32)],
        compiler_params=_cparams(("arbitrary",)),
    )(z, z, z, head, x, b_gate, w_dw, b_dw, ln_g, ln_b, pool_scale, w_co, w_pool, w_po, w_o)


def _ffn_fwd_bwd(h1, target, g_ffn, g_final, w_g, w_u, w_d, nb):
    tp = nb * BR

    def body(h1_ref, tgt_ref, gf_ref, gfin_ref, wg_hbm, wu_hbm, wd_hbm,
             dh1_ref, dh1b_ref, vb_ref, fb_ref, dgb_ref, dub_ref, dh2b_ref, loss_ref, dgf_ref, dgfin_ref,
             wg_ref, wu_ref, wd_ref, sem):
        i = pl.program_id(0)

        @pl.when(i == 0)
        def _():
            copies = [pltpu.make_async_copy(wg_hbm, wg_ref, sem.at[0]),
                      pltpu.make_async_copy(wu_hbm, wu_ref, sem.at[1]),
                      pltpu.make_async_copy(wd_hbm, wd_ref, sem.at[2])]
            for cp in copies:
                cp.start()
            loss_ref[...] = jnp.zeros_like(loss_ref)
            dgf_ref[...] = jnp.zeros_like(dgf_ref)
            dgfin_ref[...] = jnp.zeros_like(dgfin_ref)
            for cp in copies:
                cp.wait()

        h1 = h1_ref[...]
        r1 = lax.rsqrt(jnp.mean(h1 * h1, axis=-1, keepdims=True) + RMS_EPS)
        vn = h1 * r1
        vb = (vn * gf_ref[...]).astype(BF16)
        vb_ref[...] = vb
        g = _dot_nt(vb, wg_ref[...])
        up = _dot_nt(vb, wu_ref[...])
        sg = _sigmoid(g)
        sl = g * sg
        fb = (sl * up).astype(BF16)
        fb_ref[...] = fb
        h2 = h1 + _dot(fb, wd_ref[...])
        r2 = lax.rsqrt(jnp.mean(h2 * h2, axis=-1, keepdims=True) + RMS_EPS)
        yn = h2 * r2
        valid = i > 0
        diff = jnp.where(valid, yn * gfin_ref[...] - tgt_ref[...], 0.0)
        loss_ref[...] += 0.5 * jnp.sum(jnp.mean(diff * diff, axis=-1, keepdims=True))
        dy = diff * (1.0 / D)
        dgfin_ref[...] += jnp.sum(dy * yn, axis=0, keepdims=True)
        gd = dy * gfin_ref[...]
        dh2 = r2 * (gd - yn * jnp.mean(yn * gd, axis=-1, keepdims=True))
        dh2b = dh2.astype(BF16)
        dh2b_ref[...] = dh2b
        df = _dot_nt(dh2b, wd_ref[...])
        dub = (df * sl).astype(BF16)
        dgb = (df * up * (sg * (1.0 + g * (1.0 - sg)))).astype(BF16)
        dub_ref[...] = dub
        dgb_ref[...] = dgb
        dv = _dot(dgb, wg_ref[...]) + _dot(dub, wu_ref[...])
        dgf_ref[...] += jnp.sum(dv * vn, axis=0, keepdims=True)
        gd1 = dv * gf_ref[...]
        dh1 = dh2 + r1 * (gd1 - vn * jnp.mean(vn * gd1, axis=-1, keepdims=True))
        dh1_ref[...] = dh1
        dh1b_ref[...] = dh1.astype(BF16)

    any_spec = pl.BlockSpec(memory_space=pl.ANY)
    return pl.pallas_call(
        body, name="ffn_fwd_bwd",
        grid=(nb,),
        in_specs=[_row_spec(), _x_spec(), _const_spec((1, D)), _const_spec((1, D)), any_spec, any_spec, any_spec],
        out_specs=[_row_spec(), _row_spec(), _row_spec(), _row_spec(D_FF), _row_spec(D_FF), _row_spec(D_FF), _row_spec(),
                   _const_spec((1, 1)), _const_spec((1, D)), _const_spec((1, D))],
        out_shape=[jax.ShapeDtypeStruct((tp, D), F32), jax.ShapeDtypeStruct((tp, D), BF16),
                   jax.ShapeDtypeStruct((tp, D), BF16), jax.ShapeDtypeStruct((tp, D_FF), BF16),
                   jax.ShapeDtypeStruct((tp, D_FF), BF16), jax.ShapeDtypeStruct((tp, D_FF), BF16),
                   jax.ShapeDtypeStruct((tp, D), BF16), jax.ShapeDtypeStruct((1, 1), F32),
                   jax.ShapeDtypeStruct((1, D), F32), jax.ShapeDtypeStruct((1, D), F32)],
        scratch_shapes=[pltpu.VMEM((D_FF, D), BF16), pltpu.VMEM((D_FF, D), BF16), pltpu.VMEM((D_FF, D), BF16),
                        pltpu.SemaphoreType.DMA((3,))],
        compiler_params=_cparams(("arbitrary",)),
    )(h1, target, g_ffn, g_final, w_g, w_u, w_d)


def _mixers_bwd_rows(dh1b, yc, yp, z, b_gate, cpre, ln_g, ln_b, mw, pool_scale, w_o, w_co, w_po, w_pool, dep, nb):
    tp = nb * BR

    def body(dh1b_ref, yc_ref, yp_ref, za, zb, bg_ref, cpre_ref, lng_ref, lnb_ref, mw_ref, ps_ref,
             wo_ref, wco_ref, wpo_ref, wpool_ref, dep_ref,
             dycb_ref, dypb_ref, dzg_ref, dconv_ref, dmwb_ref, dm_ref, dbg_ref, dlng_ref, dlnb_ref, dbdw_ref, dps_ref):
        i = pl.program_id(0)

        @pl.when(i == 0)
        def _():
            for r in (dbg_ref, dlng_ref, dlnb_ref, dbdw_ref, dps_ref):
                r[...] = jnp.zeros_like(r)

        dmg = _dot_nt(dh1b_ref[...], wo_ref[...])
        s_a = _sigmoid(za[...] + bg_ref[:, 0:D])
        s_b = _sigmoid(zb[...] + bg_ref[:, D:2 * D])
        dycb = (dmg * s_a).astype(BF16)
        dypb = (dmg * s_b).astype(BF16)
        dycb_ref[...] = dycb
        dypb_ref[...] = dypb
        dza = dmg * yc_ref[...] * (s_a * (1.0 - s_a))
        dzb = dmg * yp_ref[...] * (s_b * (1.0 - s_b))
        dzg_ref[:, 0:D] = dza.astype(BF16)
        dzg_ref[:, D:2 * D] = dzb.astype(BF16)
        dbg_ref[:, 0:D] += jnp.sum(dza, axis=0, keepdims=True)
        dbg_ref[:, D:2 * D] += jnp.sum(dzb, axis=0, keepdims=True)

        dca = _dot_nt(dycb, wco_ref[...])
        conv = cpre_ref[...]
        mu = jnp.mean(conv, axis=-1, keepdims=True)
        xc = conv - mu
        rstd = lax.rsqrt(jnp.mean(xc * xc, axis=-1, keepdims=True) + LN_EPS)
        xhat = xc * rstd
        ln = xhat * lng_ref[...] + lnb_ref[...]
        sg = _sigmoid(ln)
        dln = dca * (sg * (1.0 + ln * (1.0 - sg)))
        dlng_ref[...] += jnp.sum(dln * xhat, axis=0, keepdims=True)
        dlnb_ref[...] += jnp.sum(dln, axis=0, keepdims=True)
        dxh = dln * lng_ref[...]
        dconv = rstd * (dxh - jnp.mean(dxh, axis=-1, keepdims=True)
                        - xhat * jnp.mean(dxh * xhat, axis=-1, keepdims=True))
        dconv_ref[...] = dconv
        dbdw_ref[...] += jnp.sum(dconv, axis=0, keepdims=True)

        dm2 = _dot_nt(dypb, wpo_ref[...])
        dps_ref[...] += jnp.sum(dm2 * mw_ref[...], axis=0, keepdims=True)
        dmwb = (dm2 * ps_ref[...]).astype(BF16)
        dmwb_ref[...] = dmwb
        for gi in range(len(POOL_WINDOWS)):
            lanes = slice(gi * GD, (gi + 1) * GD)
            dm_ref[:, lanes] = _dot_nt(dmwb[:, lanes], wpool_ref[gi])

    in_specs = [_row_spec(), _row_spec(), _row_spec(),
                pl.BlockSpec((BR, D), lambda i: (i, 3)), pl.BlockSpec((BR, D), lambda i: (i, 4)),
                _const_spec((1, 2 * D)), _row_spec(), _const_spec((1, D)), _const_spec((1, D)), _row_spec(),
                _const_spec((1, D)), _const_spec((D, D)), _const_spec((D, D)), _const_spec((D, D)),
                _const_spec((4, GD, GD)), pl.BlockSpec(memory_space=pl.ANY)]
    return pl.pallas_call(
        body, name="mixers_bwd_rows",
        grid=(nb,),
        in_specs=in_specs,
        out_specs=[_row_spec(), _row_spec(), _row_spec(2 * D), _row_spec(), _row_spec(), _row_spec(),
                   _const_spec((1, 2 * D)), _const_spec((1, D)), _const_spec((1, D)), _const_spec((1, D)),
                   _const_spec((1, D))],
        out_shape=[jax.ShapeDtypeStruct((tp, D), BF16), jax.ShapeDtypeStruct((tp, D), BF16),
                   jax.ShapeDtypeStruct((tp, 2 * D), BF16), jax.ShapeDtypeStruct((tp, D), F32),
                   jax.ShapeDtypeStruct((tp, D), BF16), jax.ShapeDtypeStruct((tp, D), F32),
                   jax.ShapeDtypeStruct((1, 2 * D), F32), jax.ShapeDtypeStruct((1, D), F32),
                   jax.ShapeDtypeStruct((1, D), F32), jax.ShapeDtypeStruct((1, D), F32),
                   jax.ShapeDtypeStruct((1, D), F32)],
        compiler_params=_cparams(("arbitrary",)),
    )(dh1b, yc, yp, z, z, b_gate, cpre, ln_g, ln_b, mw, pool_scale, w_o, w_co, w_po, w_pool, dep)


def _mixers_bwd_halo(dconv, dm, z, dzg, w_dw, head, x, g_mix, dh1, w_in_b, dep, nb, t_total):
    tp = nb * BR
    ns = w_in_b.shape[0]
    wcols = w_in_b.shape[2]
    seq = x.shape[0]

    def body(dcp, dcc, dcn, dmp, dmc, dmn, z_prev, z_cur, z_next, dzg_ref, wdw_ref, head_ref, x_ref, g_ref,
             dh1_ref, w_hbm, dep_ref,
             dzb_ref, gx_ref, dhead_ref, dwdw_ref, dgmix_ref,
             w_ref, sem, aext_ref, dext_ref, qext_ref, da_ref, rot_ref, dwp_ref):
        i = pl.program_id(0)
        (avp, agp), (av, ag), (avn, agn) = _cols(z_prev, 2), _cols(z_cur, 2), _cols(z_next, 2)

        @pl.when(i == 0)
        def _():
            cp = pltpu.make_async_copy(w_hbm, w_ref, sem.at[0])
            cp.start()
            dwp_ref[...] = jnp.zeros_like(dwp_ref)
            dgmix_ref[...] = jnp.zeros_like(dgmix_ref)
            cp.wait()

        sig_g = _sigmoid(ag[...])
        _fill_ext(aext_ref, avp[...] * _sigmoid(agp[...]), av[...] * sig_g, avn[...] * _sigmoid(agn[...]), i, nb)
        _fill_ext(dext_ref, dcp[...], dcc[...], dcn[...], i, nb)
        _fill_ext(qext_ref, dmp[...], dmc[...], dmn[...], i, nb)

        def conv_chunk(c, carry):
            lanes = pl.ds(pl.multiple_of(c * 128, 128), 128)
            _fill_rot(rot_ref, dext_ref, lanes)
            acc = jnp.zeros((BR, 128), F32)
            for k in range(KW):
                acc = acc + wdw_ref[k:k + 1, lanes] * _tap(rot_ref, dext_ref, lanes, KW - k)
            da_ref[:, lanes] = acc
            _fill_rot(rot_ref, aext_ref, lanes)
            dcv = dext_ref[HALO:HALO + BR, lanes]
            for k in range(KW):
                prod = _tap(rot_ref, aext_ref, lanes, 1 + k) * dcv
                dwp_ref[k, :, lanes] += jnp.sum(prod.reshape(BR // 8, 8, 128), axis=0)
            return carry
        lax.fori_loop(0, D // 128, conv_chunk, 0)

        @pl.when(i == nb - 1)
        def _():
            dwdw_ref[...] = jnp.sum(dwp_ref[...], axis=1)

        da = da_ref[...]
        a_val = av[...]
        dzb_ref[:, 0:D] = (da * sig_g).astype(BF16)
        dzb_ref[:, D:2 * D] = (da * a_val * (sig_g * (1.0 - sig_g))).astype(BF16)

        t_ext = _row_ids(i, EXT, -HALO)
        for gi, w in enumerate(POOL_WINDOWS):
            left = w // 2
            right = w - 1 - left
            lanes = slice(gi * GD, (gi + 1) * GD)
            qext_ref[:, lanes] = qext_ref[:, lanes] / _pool_cnt(t_ext, w, t_total)
            s = qext_ref[pl.ds(HALO - right, BR), lanes]
            for j in range(-right + 1, left + 1):
                s = s + qext_ref[pl.ds(HALO + j, BR), lanes]
            dzb_ref[:, 2 * D + gi * GD:2 * D + (gi + 1) * GD] = (s - dmc[:, lanes]).astype(BF16)
        dzb_ref[:, 3 * D:5 * D] = dzg_ref[...]

        du = _dot_nt(dzb_ref[:, 0:wcols], w_ref[0])
        for s_i in range(1, ns):
            du = du + _dot_nt(dzb_ref[:, s_i * wcols:(s_i + 1) * wcols], w_ref[s_i])
        h0 = jnp.where(i == 0, head_ref[...], x_ref[...])
        r0 = lax.rsqrt(jnp.mean(h0 * h0, axis=-1, keepdims=True) + RMS_EPS)
        un = h0 * r0
        dgmix_ref[...] += jnp.sum(du * un, axis=0, keepdims=True)
        gd = du * g_ref[...]
        dh0 = dh1_ref[...] + r0 * (gd - un * jnp.mean(un * gd, axis=-1, keepdims=True))
        gx_ref[...] = dh0

        @pl.when(i == 0)
        def _():
            dhead_ref[...] = dh0

    any_spec = pl.BlockSpec(memory_space=pl.ANY)
    in_specs = (_halo_specs(nb) + _halo_specs(nb) + _halo_specs(nb, 2 * D, 2 * D)
                + [_row_spec(2 * D), _const_spec((32, D)), _const_spec((BR, D)), _x_spec(), _const_spec((1, D)),
                   _row_spec(), any_spec, any_spec])
    return pl.pallas_call(
        body, name="mixers_bwd_halo",
        grid=(nb,),
        in_specs=in_specs,
        out_specs=[_row_spec(D_IN), _x_spec(), _const_spec((BR, D)), _const_spec((32, D)), _const_spec((1, D))],
        out_shape=[jax.ShapeDtypeStruct((tp, D_IN), BF16), jax.ShapeDtypeStruct((seq, D), F32),
                   jax.ShapeDtypeStruct((BR, D), F32), jax.ShapeDtypeStruct((32, D), F32),
                   jax.ShapeDtypeStruct((1, D), F32)],
        scratch_shapes=[pltpu.VMEM((ns, D, wcols), BF16), pltpu.SemaphoreType.DMA((1,)),
                        pltpu.VMEM((EXT, D), F32), pltpu.VMEM((EXT, D), F32), pltpu.VMEM((EXT, D), F32),
                        pltpu.VMEM((BR, D), F32), pltpu.VMEM((8, ROT_ROWS, 128), F32), pltpu.VMEM((32, 8, D), F32)],
        compiler_params=_cparams(("arbitrary",)),
    )(dconv, dconv, dconv, dm, dm, dm, z, z, z, dzg, w_dw, head, x, g_mix, dh1, w_in_b, dep)


def _wgrad(a, c, tm, tn, tk, name, diag=False, col_major=False, dep=None):
    tp, m = a.shape
    n = c.shape[1]
    nk = tp // tk
    gm, gn = m // tm, n // tn

    def body(a_ref, c_ref, *rest):
        o_ref, ob_ref = rest[-2:]
        k = pl.program_id(2)

        @pl.when(k == 0)
        def _():
            o_ref[...] = jnp.zeros_like(o_ref)

        o_ref[...] += _dot_tn(a_ref[...], c_ref[...])

        @pl.when(k == nk - 1)
        def _():
            ob_ref[...] = o_ref[...].astype(BF16)

    c_map = lambda i, j, k: (k, j)
    grid = (gm, gn, nk)
    deps = [] if dep is None else [dep]
    if diag:
        grid = (gm, 1, nk)
        c_map = lambda i, j, k: (k, i)
        o_spec = pl.BlockSpec((tm, tn), lambda i, j, k: (i, 0))
        o_shape = (m, tn)
    elif col_major:
        o_spec = pl.BlockSpec((None, tm, tn), lambda i, j, k: (j, i, 0))
        o_shape = (gn, m, tn)
    else:
        o_spec = pl.BlockSpec((tm, tn), lambda i, j, k: (i, j))
        o_shape = (m, n)
    return pl.pallas_call(
        body, name=name,
        grid=grid,
        in_specs=[pl.BlockSpec((tk, tm), lambda i, j, k: (k, i)), pl.BlockSpec((tk, tn), c_map)]
        + [pl.BlockSpec(memory_space=pl.ANY)] * len(deps),
        out_specs=[o_spec, o_spec],
        out_shape=[jax.ShapeDtypeStruct(o_shape, F32), jax.ShapeDtypeStruct(o_shape, BF16)],
        compiler_params=_cparams(("arbitrary", "arbitrary", "arbitrary")),
    )(a, c, *deps)


def _place():
    x, y, c = lax.axis_index("x"), lax.axis_index("y"), lax.axis_index("c")
    others = [(1 - x, y), (x, 1 - y), (1 - x, 1 - y)]
    return x, y, c, others


def _split2(a, axis=0):
    return a.reshape(a.shape[:axis] + (2, a.shape[axis] // 2) + a.shape[axis + 1:])


def _merge2(a, axis=0):
    return a.reshape(a.shape[:axis] + (2 * a.shape[axis + 1],) + a.shape[axis + 2:])


def _cast_into_slot(shards, chip, dep, name):
    n = len(shards)
    r, c = shards[0].shape
    r2 = r // 2

    def body(chip_ref, *refs):
        for a in range(n):
            refs[n + 1 + a][...] = refs[a][...].astype(BF16)

    out = pl.pallas_call(
        body, name=name,
        grid_spec=pltpu.PrefetchScalarGridSpec(
            num_scalar_prefetch=1, grid=(2,),
            in_specs=[pl.BlockSpec((r2, c), lambda h, chip_ref: (h, 0))] * n + [pl.BlockSpec(memory_space=pl.ANY)],
            out_specs=[pl.BlockSpec((None, None, r2, c), lambda h, chip_ref: (chip_ref[0], h, 0, 0))] * n),
        out_shape=[jax.ShapeDtypeStruct((N_SHARD, 2, r2, c), BF16)] * n,
        compiler_params=_cparams(("arbitrary",)),
    )(chip, *shards, dep)
    return list(out)


HBM_SPEC = pl.BlockSpec(memory_space=pltpu.HBM)
SEM_SPEC = pl.BlockSpec(memory_space=pltpu.SEMAPHORE)
DATAFLOW = pltpu.SideEffectType.DATAFLOW_SIDE_EFFECTING
TOKEN = jax.ShapeDtypeStruct((8, 128), F32)


def _in_hbm(a):
    return pltpu.with_memory_space_constraint(a, pltpu.HBM)


def _gather_tiny(v):
    vm = pl.BlockSpec(memory_space=pltpu.VMEM)

    def body(v_ref, out_ref, send_sems, recv_sems):
        x, y, c, others = _place()
        mine = 2 * x + y
        sends = [pltpu.make_async_remote_copy(
            src_ref=v_ref, dst_ref=out_ref.at[mine], send_sem=send_sems.at[j], recv_sem=recv_sems.at[j],
            device_id=(*chip, c), device_id_type=MESH) for j, chip in enumerate(others)]
        for cp in sends:
            cp.start()
        out_ref[mine] = v_ref[...]
        for j, chip in enumerate(others):
            landed = out_ref.at[2 * chip[0] + chip[1]]
            pltpu.make_async_remote_copy(
                src_ref=landed, dst_ref=landed, send_sem=send_sems.at[j], recv_sem=recv_sems.at[j],
                device_id=(x, y, c), device_id_type=MESH).wait_recv()
        for cp in sends:
            cp.wait_send()

    return pl.pallas_call(
        body, name="gather_tiny",
        in_specs=[vm], out_specs=vm,
        out_shape=jax.ShapeDtypeStruct((N_SHARD,) + v.shape, v.dtype),
        scratch_shapes=[pltpu.SemaphoreType.DMA((3,)), pltpu.SemaphoreType.DMA((3,))],
    )(v)


def _ici_copies(srcs, dsts, send_sems, recv_sems, started):
    x, y, c, others = _place()
    mine = 2 * x + y
    copies = []
    for a in range(len(srcs)):
        for j, chip in enumerate(others):
            there = 2 * chip[0] + chip[1]
            src, dst = srcs[a](mine, there, c), dsts[a](mine, there, c)
            if not started:
                dst = dsts[a](there, mine, c)
            copies.append(pltpu.make_async_remote_copy(
                src_ref=src, dst_ref=dst, send_sem=send_sems.at[a * 3 + j], recv_sem=recv_sems.at[a * 3 + j],
                device_id=(*chip, c), device_id_type=MESH))
    return copies


def _split_start(srcs_of, dsts_of, arrays, n_src, name, copies_of=None, n_sems=None, dep=None):
    n = len(arrays)
    n_sems = n_sems or 3 * n_src
    copies_of = copies_of or (lambda ins, ss, rs, started: _ici_copies(srcs_of(ins), dsts_of(ins), ss, rs, started))
    deps = [] if dep is None else [dep]
    nd = len(deps)

    def body(*refs):
        ins = refs[:n]
        send_sems, recv_sems = refs[n + nd], refs[n + nd + 1]
        token = refs[2 * n + nd + 2]
        for cp in copies_of(ins, send_sems, recv_sems, True):
            cp.start()
        token[...] = jnp.zeros_like(token)

    out = pl.pallas_call(
        body, name=name,
        in_specs=[HBM_SPEC] * n + [pl.BlockSpec(memory_space=pl.ANY)] * nd,
        out_specs=(SEM_SPEC, SEM_SPEC, *([HBM_SPEC] * n), pl.BlockSpec(memory_space=pltpu.VMEM)),
        out_shape=(pltpu.SemaphoreType.DMA((n_sems,)), pltpu.SemaphoreType.DMA((n_sems,)),
                   *[pltpu.HBM(a.shape, a.dtype) for a in arrays], TOKEN),
        input_output_aliases={a: 2 + a for a in range(n)},
        compiler_params=pltpu.CompilerParams(has_side_effects=DATAFLOW),
    )(*[_in_hbm(a) for a in arrays], *deps)
    return out[0], out[1], list(out[2:2 + n]), out[2 + n]


def _split_wait(srcs_of, dsts_of, send_sems, recv_sems, arrays, after, name, copies_of=None):
    n = len(arrays)
    copies_of = copies_of or (lambda ins, ss, rs, started: _ici_copies(srcs_of(ins), dsts_of(ins), ss, rs, started))

    def body(*refs):
        ins = refs[:n]
        send_sems, recv_sems = refs[n], refs[n + 1]
        for cp in copies_of(ins, send_sems, recv_sems, False):
            cp.wait_send()
            cp.wait_recv()

    return pl.pallas_call(
        body, name=name,
        in_specs=[HBM_SPEC] * n + [SEM_SPEC, SEM_SPEC] + [pl.BlockSpec(memory_space=pl.ANY)] * len(after),
        out_specs=[HBM_SPEC] * n,
        out_shape=[pltpu.HBM(a.shape, a.dtype) for a in arrays],
        input_output_aliases={a: a for a in range(n)},
        compiler_params=pltpu.CompilerParams(has_side_effects=DATAFLOW),
    )(*arrays, send_sems, recv_sems, *after)


def _gather_views(ins):
    view = [lambda frm, to, c, r=r: r.at[frm, c] for r in ins]
    return view


def _gather_start(bufs, dep, name):
    return _split_start(_gather_views, _gather_views, bufs, len(bufs), name, dep=dep)


def _gather_wait(send_sems, recv_sems, bufs, after, name):
    return _split_wait(_gather_views, _gather_views, send_sems, recv_sems, bufs, after, name)


SIBLING_ONLY = pltpu.CompilerParams(collective_id=0)


def _sibling_handshake(x, y, c):
    barrier = pltpu.get_barrier_semaphore()
    pl.semaphore_signal(barrier, inc=1, device_id=(x, y, 1 - c), device_id_type=MESH)
    pl.semaphore_wait(barrier, 1)


def _forward_halves(bufs, name):
    n = len(bufs)
    any_spec = pl.BlockSpec(memory_space=pl.ANY)

    def body(*refs):
        outs = refs[n:2 * n]
        send_sems, recv_sems = refs[2 * n:]
        x, y, c, others = _place()
        _sibling_handshake(x, y, c)
        copies = []
        for a in range(n):
            for j, chip in enumerate(others):
                landed = outs[a].at[2 * chip[0] + chip[1], c]
                copies.append(pltpu.make_async_remote_copy(
                    src_ref=landed, dst_ref=landed, send_sem=send_sems.at[a * 3 + j], recv_sem=recv_sems.at[a * 3 + j],
                    device_id=(x, y, 1 - c), device_id_type=MESH))
        for cp in copies:
            cp.start()
        for a in range(n):
            for j, chip in enumerate(others):
                landed = outs[a].at[2 * chip[0] + chip[1], 1 - c]
                pltpu.make_async_remote_copy(
                    src_ref=landed, dst_ref=landed, send_sem=send_sems.at[a * 3 + j], recv_sem=recv_sems.at[a * 3 + j],
                    device_id=(x, y, c), device_id_type=MESH).wait_recv()
        for cp in copies:
            cp.wait_send()

    out = pl.pallas_call(
        body, name=name,
        in_specs=[any_spec] * n, out_specs=[any_spec] * n,
        out_shape=[jax.ShapeDtypeStruct(b.shape, b.dtype) for b in bufs],
        input_output_aliases={a: a for a in range(n)},
        scratch_shapes=[pltpu.SemaphoreType.DMA((3 * n,)), pltpu.SemaphoreType.DMA((3 * n,))],
        compiler_params=SIBLING_ONLY,
    )(*bufs)
    return [_merge2(o, 1) for o in out]


def _swap_halves_bf16(gbs, name):
    n = len(gbs)
    any_spec = pl.BlockSpec(memory_space=pl.ANY)

    def body(*refs):
        ins, outs = refs[:n], refs[n:2 * n]
        send_sems, recv_sems = refs[2 * n:]
        x, y, c, _ = _place()
        _sibling_handshake(x, y, c)
        copies = []
        for a in range(n):
            copies.append(pltpu.make_async_remote_copy(
                src_ref=ins[a].at[:, 1 - c], dst_ref=outs[a], send_sem=send_sems.at[a], recv_sem=recv_sems.at[a],
                device_id=(x, y, 1 - c), device_id_type=MESH))
        for cp in copies:
            cp.start()
        for cp in copies:
            cp.wait()

    return pl.pallas_call(
        body, name=name,
        in_specs=[any_spec] * n, out_specs=[any_spec] * n,
        out_shape=[jax.ShapeDtypeStruct((g.shape[0], g.shape[1] // 2, g.shape[2]), g.dtype) for g in gbs],
        scratch_shapes=[pltpu.SemaphoreType.DMA((n,)), pltpu.SemaphoreType.DMA((n,))],
        compiler_params=SIBLING_ONLY,
    )(*[_split2(g, 1) for g in gbs])


def _scatter_srcs(n):
    return lambda ins: [lambda frm, to, c, r=r: r.at[to] for r in ins[:n]]


def _scatter_dsts(n):
    return lambda ins: [lambda frm, to, c, r=r: r.at[frm] for r in ins[n:]]


def _scatter_start(hbs, name):
    n = len(hbs)
    lands = [lax.empty(h.shape, h.dtype) for h in hbs]
    return _split_start(_scatter_srcs(n), _scatter_dsts(n), list(hbs) + lands, n, name)


def _scatter_wait(send_sems, recv_sems, arrays, after, name):
    n = len(arrays) // 2
    return _split_wait(_scatter_srcs(n), _scatter_dsts(n), send_sems, recv_sems, arrays, after, name)[n:]


def _join_halves(rhs, name):
    n = len(rhs)
    any_spec = pl.BlockSpec(memory_space=pl.ANY)

    def body(*refs):
        outs = refs[n:2 * n]
        send_sems, recv_sems = refs[2 * n:]
        x, y, c, _ = _place()
        _sibling_handshake(x, y, c)
        copies = []
        for a in range(n):
            copies.append(pltpu.make_async_remote_copy(
                src_ref=outs[a].at[c], dst_ref=outs[a].at[c], send_sem=send_sems.at[a],
                recv_sem=recv_sems.at[a], device_id=(x, y, 1 - c), device_id_type=MESH))
        for cp in copies:
            cp.start()
        for a in range(n):
            landed = outs[a].at[1 - c]
            pltpu.make_async_remote_copy(
                src_ref=landed, dst_ref=landed, send_sem=send_sems.at[a], recv_sem=recv_sems.at[a],
                device_id=(x, y, c), device_id_type=MESH).wait_recv()
        for cp in copies:
            cp.wait_send()

    out = pl.pallas_call(
        body, name=name,
        in_specs=[any_spec] * n, out_specs=[any_spec] * n,
        out_shape=[jax.ShapeDtypeStruct(r.shape, r.dtype) for r in rhs],
        input_output_aliases={a: a for a in range(n)},
        scratch_shapes=[pltpu.SemaphoreType.DMA((n,)), pltpu.SemaphoreType.DMA((n,))],
        compiler_params=SIBLING_ONLY,
    )(*rhs)
    return [_merge2(o) for o in out]


FLIPS = [(dx, dy, dc) for dx in (0, 1) for dy in (0, 1) for dc in (0, 1)][1:]


def _peer_copies(ins, send_sems, recv_sems, started):
    x, y, c, _ = _place()
    copies = []
    for k, (dx, dy, dc) in enumerate(FLIPS):
        px, py, pc = jnp.bitwise_xor(x, dx), jnp.bitwise_xor(y, dy), jnp.bitwise_xor(c, dc)
        slot = 4 * x + 2 * y + c if started else 4 * px + 2 * py + pc
        copies.append(pltpu.make_async_remote_copy(
            src_ref=ins[0], dst_ref=ins[1].at[slot], send_sem=send_sems.at[k], recv_sem=recv_sems.at[k],
            device_id=(px, py, pc), device_id_type=MESH))
    return copies


def _small_start(v, name):
    land = lax.empty((8,) + v.shape, v.dtype)
    return _split_start(None, None, [v, land], 0, name, copies_of=_peer_copies, n_sems=len(FLIPS))


def _small_wait(send_sems, recv_sems, arrays, after, name):
    return _split_wait(None, None, send_sems, recv_sems, arrays, after, name, copies_of=_peer_copies)[1]


def _sum_slots(land, v, me):
    rows, cols = v.shape

    def body(me_ref, land_ref, v_ref, o_ref):
        o_ref[...] = jnp.zeros_like(o_ref)
        for d in range(8):
            @pl.when(me_ref[0] == d)
            def _():
                o_ref[...] += v_ref[...]

            @pl.when(me_ref[0] != d)
            def _():
                o_ref[...] += land_ref[d]

    return pl.pallas_call(
        body, name="sum_slots",
        grid_spec=pltpu.PrefetchScalarGridSpec(
            num_scalar_prefetch=1, grid=(1,),
            in_specs=[pl.BlockSpec((8, rows, cols), lambda i, me_ref: (0, 0, 0)),
                      pl.BlockSpec((rows, cols), lambda i, me_ref: (0, 0))],
            out_specs=pl.BlockSpec((rows, cols), lambda i, me_ref: (0, 0))),
        out_shape=jax.ShapeDtypeStruct((rows, cols), F32),
        compiler_params=_cparams(("arbitrary",)),
    )(me, land, v)


def _by_shape(arrays):
    groups = {}
    for k, a in enumerate(arrays):
        groups.setdefault(a.shape, []).append(k)
    return list(groups.values())


def _add_sibling_half(gs, sbs, idx, name):
    n = len(gs)
    ns, r, c = gs[0].shape
    r2 = r // 2

    def body(idx_ref, *refs):
        for a in range(n):
            g_ref, sb_ref, hown_ref, hb_ref = refs[a], refs[n + a], refs[2 * n + a], refs[3 * n + a]
            h = g_ref[...] + sb_ref[...].astype(F32)
            hb_ref[...] = h.astype(BF16)

            @pl.when(pl.program_id(0) == idx_ref[0])
            def _():
                hown_ref[...] = h

    spec = pl.BlockSpec((None, r2, c), lambda s, idx_ref: (s, 0, 0))
    out = pl.pallas_call(
        body, name=name,
        grid_spec=pltpu.PrefetchScalarGridSpec(
            num_scalar_prefetch=1, grid=(ns,),
            in_specs=[pl.BlockSpec((None, r2, c), lambda s, idx_ref: (s, idx_ref[4], 0))] * n + [spec] * n,
            out_specs=[pl.BlockSpec((r2, c), lambda s, idx_ref: (0, 0))] * n + [spec] * n),
        out_shape=[jax.ShapeDtypeStruct((r2, c), F32)] * n + [jax.ShapeDtypeStruct((ns, r2, c), BF16)] * n,
        compiler_params=_cparams(("arbitrary",)),
    )(idx, *gs, *sbs)
    return [(out[a], out[n + a]) for a in range(n)]


def _add_chip_slabs(hs, rbs, idx, name):
    n = len(hs)
    r2, c = hs[0].shape

    def body(idx_ref, *refs):
        for a in range(n):
            h_ref, r0_ref, r1_ref, r2_ref = refs[4 * a:4 * a + 4]
            refs[4 * n + a][...] = ((h_ref[...] + r0_ref[...].astype(F32)) + r1_ref[...].astype(F32)) + r2_ref[...].astype(F32)

    def pick(k):
        return pl.BlockSpec((None, r2, c), lambda i, idx_ref: (idx_ref[k], 0, 0))

    operands = []
    for h, rb in zip(hs, rbs):
        operands += [h, rb, rb, rb]
    out = pl.pallas_call(
        body, name=name,
        grid_spec=pltpu.PrefetchScalarGridSpec(
            num_scalar_prefetch=1, grid=(1,),
            in_specs=[pl.BlockSpec((r2, c), lambda i, idx_ref: (0, 0)), pick(1), pick(2), pick(3)] * n,
            out_specs=[pl.BlockSpec((None, r2, c), lambda i, idx_ref: (idx_ref[4], 0, 0))] * n),
        out_shape=[jax.ShapeDtypeStruct((2, r2, c), F32)] * n,
        compiler_params=_cparams(("arbitrary",)),
    )(idx, *operands)
    return list(out)


ELEMENTWISE_VMEM = 16 * 1024 * 1024


def _adamw(items, name):
    n = len(items)
    r, c = items[0][0].shape
    br = max(b for b in range(8, r + 1, 8) if r % b == 0 and n * 16 * b * c * 4 <= ELEMENTWISE_VMEM) if r % 8 == 0 else r

    def body(*refs):
        for a in range(n):
            g_ref, w_ref, m_ref, v_ref = refs[4 * a:4 * a + 4]
            go_ref, d_ref, nm_ref, nv_ref = refs[4 * n + 4 * a:4 * n + 4 * a + 4]
            gg = g_ref[...]
            go_ref[...] = gg
            nm = B1 * m_ref[...] + (1.0 - B1) * gg
            nv = B2 * v_ref[...] + (1.0 - B2) * jnp.square(gg)
            m_hat = nm / (1.0 - B1 ** STEP)
            v_hat = nv / (1.0 - B2 ** STEP)
            d_ref[...] = -LR * (m_hat / (jnp.sqrt(v_hat) + ADAM_EPS) + WD * w_ref[...])
            nm_ref[...] = nm
            nv_ref[...] = nv

    spec = pl.BlockSpec((br, c), lambda i: (i, 0))
    out = pl.pallas_call(
        body, name=name,
        grid=(r // br,),
        in_specs=[spec] * (4 * n), out_specs=[spec] * (4 * n),
        out_shape=[jax.ShapeDtypeStruct((r, c), F32)] * (4 * n),
        compiler_params=_cparams(("arbitrary",)),
    )(*[a for item in items for a in item])
    return [tuple(out[4 * a:4 * a + 4]) for a in range(n)]


BIG = ("w_in", "w_conv_out", "w_pool", "w_pool_out", "w_o", "w_ffn_gate", "w_ffn_up", "w_ffn_down")
REPL = ("g_mix", "b_gate", "b_dw", "ln_g", "ln_b", "pool_scale", "g_ffn", "g_final")
GROUP_MIX = ("w_conv_out", "w_pool", "w_pool_out", "w_o")
GROUP_FFN = ("w_ffn_gate", "w_ffn_up", "w_ffn_down")
TRANSPOSED = ("w_ffn_gate", "w_ffn_up")
WEIGHT_ORDER = ("meta_tokens", "g_mix", "w_in", "b_gate", "w_dw", "b_dw", "ln_g", "ln_b", "w_conv_out", "w_pool",
                "pool_scale", "w_pool_out", "w_o", "g_ffn", "w_ffn_gate", "w_ffn_up", "w_ffn_down", "g_final")


def _shard2d(name, a):
    a = a[0]
    if name == "w_pool":
        return a.reshape(4 * 64, GD)
    if name in TRANSPOSED:
        return a.T
    return a


def _unshard2d(name, a, shape):
    return a.T.reshape(shape) if name in TRANSPOSED else a.reshape(shape)


def _slabs_to_cols(a):
    ns, m, c = a.shape
    return a.transpose(1, 0, 2).reshape(m, ns * c)


def kernel(x, meta_tokens, g_mix, w_in, b_gate, w_dw, b_dw, ln_g, ln_b, w_conv_out, w_pool, pool_scale, w_pool_out, w_o, g_ffn, w_ffn_gate, w_ffn_up, w_ffn_down, g_final, loss_target, m_meta_tokens, m_g_mix, m_w_in, m_b_gate, m_w_dw, m_b_dw, m_ln_g, m_ln_b, m_w_conv_out, m_w_pool, m_pool_scale, m_w_pool_out, m_w_o, m_g_ffn, m_w_ffn_gate, m_w_ffn_up, m_w_ffn_down, m_g_final, v_meta_tokens, v_g_mix, v_w_in, v_b_gate, v_w_dw, v_b_dw, v_ln_g, v_ln_b, v_w_conv_out, v_w_pool, v_pool_scale, v_w_pool_out, v_w_o, v_g_ffn, v_w_ffn_gate, v_w_ffn_up, v_w_ffn_down, v_g_final):
    args = dict(locals())
    w = {n: args[n] for n in WEIGHT_ORDER}
    mom = {n: args["m_" + n] for n in WEIGHT_ORDER}
    var = {n: args["v_" + n] for n in WEIGHT_ORDER}
    seq = x.shape[1]
    nb = seq // BR + 1
    tp = nb * BR
    tk = tp // 2 if (tp // 2) % 16 == 0 else BR
    t_total = seq + N_META
    cx, cy, cc = lax.axis_index("x"), lax.axis_index("y"), lax.axis_index("c")
    chip = 2 * cx + cy
    chip1 = jnp.reshape(chip, (1,)).astype(jnp.int32)
    core = jnp.reshape(cc, (1,)).astype(jnp.int32)
    others = jnp.sort(jnp.stack([2 * (1 - cx) + cy, 2 * cx + (1 - cy), 2 * (1 - cx) + (1 - cy)]))
    idx = jnp.concatenate([chip1, others.astype(jnp.int32), core])
    xs, target = x[0], loss_target[0]

    tiny = _gather_tiny(jnp.concatenate([w["meta_tokens"], w["w_dw"][0], jnp.zeros((1, GD), F32)], axis=0))
    small = {n: w[n] for n in REPL if n != "g_final"}
    small["g_final"] = w["g_final"].reshape(1, D)
    small["w_dw"] = _slabs_to_cols(tiny[:, N_META:])
    head = jnp.concatenate([jnp.zeros((PAD, D), F32), _slabs_to_cols(tiny[:, :N_META])], axis=0)

    def cast(group, dep):
        shards = [_shard2d(n, w[n]) for n in group]
        bufs = [None] * len(group)
        for ks in _by_shape(shards):
            done = _cast_into_slot([shards[k] for k in ks], chip1, dep, "cast_" + group[ks[0]])
            for k, b in zip(ks, done):
                bufs[k] = b
        return bufs

    def gather_finish(group, start, after, name):
        landed = _gather_wait(start[0], start[1], start[2], after, "gather_wait_" + name)
        return dict(zip(group, _forward_halves(landed, "forward_" + name)))

    st_in = _gather_start(cast(("w_in",), tiny), None, "gather_start_in")
    bufs_mix, bufs_ffn = cast(GROUP_MIX, st_in[3]), cast(GROUP_FFN, st_in[3])
    u = _rms_u(head, xs, small["g_mix"] + st_in[3][0:1, 0:1], nb)
    z_own = _in_proj_own(u, w["w_in"][0], idx, nb)
    gw = gather_finish(("w_in",), st_in, [z_own] + bufs_mix + bufs_ffn, "in")
    st_mix = _gather_start(bufs_mix, gw["w_in"], "gather_start_mix")
    z = _in_proj_rest(u, gw["w_in"], z_own, idx, st_mix[3], nb)
    gw.update(gather_finish(GROUP_MIX, st_mix, [z], "mix"))
    st_ffn = _gather_start(bufs_ffn, gw["w_o"], "gather_start_ffn")
    w_pool_b = gw["w_pool"].reshape(N_SHARD, 4, 64, GD).transpose(1, 0, 2, 3).reshape(4, GD, GD)
    w_co_b, w_po_b, w_o_b = (gw[n].reshape(D, D) for n in ("w_conv_out", "w_pool_out", "w_o"))
    h1, yc, yp, mg, ca, cpre, m, mw, m2b = _mixers_fwd(
        z, head, xs, small["b_gate"] + st_ffn[3][0, 0], small["w_dw"], small["b_dw"], small["ln_g"], small["ln_b"],
        small["pool_scale"], w_co_b, w_pool_b, w_po_b, w_o_b, nb, t_total)
    gw.update(gather_finish(GROUP_FFN, st_ffn, [h1], "ffn"))

    dh1, dh1b, vb, fb, dgb, dub, dh2b, loss, dg_ffn, dg_final = _ffn_fwd_bwd(
        h1, target, small["g_ffn"], small["g_final"], gw["w_ffn_gate"].reshape(D_FF, D),
        gw["w_ffn_up"].reshape(D_FF, D), gw["w_ffn_down"].reshape(D_FF, D), nb)

    def slabs(name, g):
        if name == "w_in":
            return g
        if name == "w_pool":
            return g.reshape(4, N_SHARD, 64, GD).transpose(1, 0, 2, 3).reshape(N_SHARD, 4 * 64, GD)
        return g.reshape(N_SHARD, g.shape[0] // N_SHARD, g.shape[1])

    def reduce_start(group, grads, name):
        g32 = [slabs(n, grads[n][0]) for n in group]
        g16 = [slabs(n, grads[n][1]) for n in group]
        from_sibling = _swap_halves_bf16(g16, "swap_halves_" + name)
        halves = [None] * len(group)
        for ks in _by_shape(g32):
            done = _add_sibling_half([g32[k] for k in ks], [from_sibling[k] for k in ks], idx, "add_sibling_" + group[ks[0]])
            for k, pair in zip(ks, done):
                halves[k] = pair
        return [h for h, _ in halves], _scatter_start([hb for _, hb in halves], "scatter_start_" + name)

    def reduce_finish(group, halves, start, after, name):
        from_chips = _scatter_wait(start[0], start[1], start[2], after, "scatter_wait_" + name)
        reduced = [None] * len(group)
        for ks in _by_shape(halves):
            done = _add_chip_slabs([halves[k] for k in ks], [from_chips[k] for k in ks], idx, "add_chips_" + group[ks[0]])
            for k, r in zip(ks, done):
                reduced[k] = r
        return reduced

    half_ff = D_FF // 2
    grads_ffn = {
        "w_ffn_gate": _wgrad(dgb, vb, half_ff, D, tk, "wgrad_ffn_gate"),
        "w_ffn_up": _wgrad(dub, vb, half_ff, D, tk, "wgrad_ffn_up"),
        "w_ffn_down": _wgrad(fb, dh2b, half_ff, D, tk, "wgrad_ffn_down"),
    }
    halves_ffn, sc_ffn = reduce_start(GROUP_FFN, grads_ffn, "ffn")

    dycb, dypb, dzg, dconv, dmwb, dm, db_gate, dln_g, dln_b, db_dw, dps = _mixers_bwd_rows(
        dh1b, yc, yp, z, small["b_gate"], cpre, small["ln_g"], small["ln_b"], mw, small["pool_scale"],
        w_o_b, w_co_b, w_po_b, w_pool_b, sc_ffn[3], nb)
    grads_mix = {
        "w_conv_out": _wgrad(ca, dycb, D, D, tk, "wgrad_conv_out"),
        "w_pool": _wgrad(m, dmwb, GD, GD, tp, "wgrad_pool", diag=True),
        "w_pool_out": _wgrad(m2b, dypb, D, D, tk, "wgrad_pool_out"),
        "w_o": _wgrad(mg, dh1b, D, D, tk, "wgrad_o"),
    }
    halves_mix, sc_mix = reduce_start(GROUP_MIX, grads_mix, "mix")
    dzb, grad_x, dhead, dw_dw, dg_mix = _mixers_bwd_halo(
        dconv, dm, z, dzg, small["w_dw"], head, xs, small["g_mix"], dh1, gw["w_in"], sc_mix[3], nb, t_total)
    packed = jnp.concatenate(
        [dg_mix, db_gate.reshape(2, D), db_dw, dln_g, dln_b, dps, dg_ffn, dg_final,
         jnp.broadcast_to(loss, (1, D)), jnp.zeros((6, D), F32), dhead[PAD:], dw_dw], axis=0)
    sm = _small_start(packed, "small_start")
    grads_in = {"w_in": _wgrad(u, dzb, D, D_IN // N_SHARD, tk, "wgrad_in", col_major=True, dep=sm[3])}
    halves_in, sc_in = reduce_start(("w_in",), grads_in, "in")

    land = _small_wait(sm[0], sm[1], sm[2], [sc_in[3]], "small_wait")
    summed = _sum_slots(land, packed, jnp.reshape(4 * cx + 2 * cy + cc, (1,)).astype(jnp.int32))
    loss = summed[9, 0]

    first = GROUP_FFN + GROUP_MIX
    reduced_half = reduce_finish(GROUP_FFN, halves_ffn, sc_ffn, [summed], "ffn")
    reduced_half += reduce_finish(GROUP_MIX, halves_mix, sc_mix, [summed], "mix")
    reduced = dict(zip(first, _join_halves(reduced_half, "join_halves_first")))
    updates = {}
    for ks in _by_shape([reduced[n] for n in first]):
        names = [first[k] for k in ks]
        done = _adamw([(reduced[n], _shard2d(n, w[n]), _shard2d(n, mom[n]), _shard2d(n, var[n])) for n in names],
                      "adamw_" + names[0])
        updates.update(zip(names, done))

    def repl_stack(d):
        return jnp.concatenate([d["g_mix"], d["b_gate"].reshape(2, D), d["b_dw"], d["ln_g"], d["ln_b"],
                                d["pool_scale"], d["g_ffn"], d["g_final"].reshape(1, D), jnp.ones((7, D), F32)], axis=0)

    def shard_stack(d):
        return jnp.concatenate([d["meta_tokens"], d["w_dw"][0], jnp.ones((1, GD), F32)], axis=0)

    g_repl = summed[0:16]
    g_shard = lax.dynamic_slice_in_dim(summed[16:64], chip * GD, GD, axis=1)
    g_repl, d_repl, m_repl, v_repl = _adamw([(g_repl, repl_stack(w), repl_stack(mom), repl_stack(var))], "adamw_repl")[0]
    g_shard, d_shard, m_shard, v_shard = _adamw(
        [(g_shard, shard_stack(w), shard_stack(mom), shard_stack(var))], "adamw_cols")[0]

    done_first = [updates[n][1] for n in first] + [d_repl, d_shard]
    last_half = reduce_finish(("w_in",), halves_in, sc_in, done_first, "in")
    reduced["w_in"] = _join_halves(last_half, "join_halves_in")[0]
    updates["w_in"] = _adamw([(reduced["w_in"], w["w_in"][0], mom["w_in"][0], var["w_in"][0])], "adamw_w_in")[0]

    def unpack(name, repl, shard):
        if name == "meta_tokens":
            return shard[0:N_META]
        if name == "w_dw":
            return shard[N_META:N_META + KW].reshape(1, KW, GD)
        row = {"g_mix": 0, "b_gate": 1, "b_dw": 3, "ln_g": 4, "ln_b": 5, "pool_scale": 6, "g_ffn": 7, "g_final": 8}[name]
        if name == "b_gate":
            return repl[1:3].reshape(1, 2 * D)
        if name == "g_final":
            return repl[8]
        return repl[row:row + 1]

    out_g, out_d, out_m, out_v = {}, {}, {}, {}
    for n in WEIGHT_ORDER:
        if n in BIG:
            g, d_, m_, v_ = updates[n]
            shape = w[n].shape
            out_g[n], out_d[n], out_m[n], out_v[n] = (_unshard2d(n, a, shape) for a in (g, d_, m_, v_))
        else:
            out_g[n] = unpack(n, g_repl, g_shard)
            out_d[n] = unpack(n, d_repl, d_shard)
            out_m[n] = unpack(n, m_repl, m_shard)
            out_v[n] = unpack(n, v_repl, v_shard)
    return (loss, grad_x[None], *[out_g[n] for n in WEIGHT_ORDER], *[out_d[n] for n in WEIGHT_ORDER],
            *[out_m[n] for n in WEIGHT_ORDER], *[out_v[n] for n in WEIGHT_ORDER])
```

```python
import jax
import jax.numpy as jnp
from jax import lax
from jax.experimental import pallas as pl
from jax.experimental.pallas import tpu as pltpu

F32 = jnp.float32
BF16 = jnp.bfloat16
MESH = pl.DeviceIdType.MESH

D = 1024
N_META = 16
KW = 31
POOL_WINDOWS = (2, 4, 8, 16)
GD = 256
D_IN = 5 * D
D_FF = 2816
N_SHARD = 4
BR = 256
HALO = 16
PAD = BR - N_META
EXT = BR + 2 * HALO
RMS_EPS = 1e-6
LN_EPS = 1e-5
LR, B1, B2, ADAM_EPS, WD, STEP = 0.001, 0.9, 0.999, 1e-08, 0.01, 10
VMEM_LIMIT = 56 * 1024 * 1024


def _cparams(sem, vmem=VMEM_LIMIT):
    return pltpu.CompilerParams(dimension_semantics=sem, vmem_limit_bytes=vmem)


def _dot(a, b):
    return jnp.dot(a, b, preferred_element_type=F32)


def _dot_nt(a, b):
    return lax.dot_general(a, b, (((1,), (1,)), ((), ())), preferred_element_type=F32)


def _dot_tn(a, b):
    return lax.dot_general(a, b, (((0,), (0,)), ((), ())), preferred_element_type=F32)


def _sigmoid(x):
    return 0.5 * jnp.tanh(0.5 * x) + 0.5


def _row_ids(i, n, offset=0):
    return lax.broadcasted_iota(jnp.int32, (n, 1), 0) + (i * BR + offset - PAD)


def _pool_cnt(t, w, t_total):
    left = w // 2
    right = w - 1 - left
    lo = jnp.clip(t - left, 0, t_total)
    hi = jnp.clip(t + right + 1, 0, t_total)
    return jnp.maximum(hi - lo, 1).astype(F32)


def _halo_specs(nb, halo_width=D, width=D):
    last = nb * (BR // HALO) - 1
    return [
        pl.BlockSpec((HALO, halo_width), lambda i: (jnp.maximum(i * (BR // HALO) - 1, 0), 0)),
        pl.BlockSpec((BR, width), lambda i: (i, 0)),
        pl.BlockSpec((HALO, halo_width), lambda i: (jnp.minimum((i + 1) * (BR // HALO), last), 0)),
    ]


def _cols(ref, n):
    return [ref.at[:, k * D:(k + 1) * D] for k in range(n)]


def _fill_ext(ext_ref, prev, cur, nxt, i, nb):
    ext_ref[0:HALO, :] = jnp.where(i > 0, prev, 0.0)
    ext_ref[HALO:HALO + BR, :] = cur
    ext_ref[HALO + BR:EXT, :] = jnp.where(i < nb - 1, nxt, 0.0)


ROT_ROWS = EXT - 8


def _fill_rot(rot_ref, ext_ref, lanes):
    for r in range(1, 8):
        rot_ref[r] = ext_ref[pl.ds(r, ROT_ROWS), lanes]


def _tap(rot_ref, ext_ref, lanes, offset):
    q, r = divmod(offset, 8)
    if r == 0:
        return ext_ref[pl.ds(8 * q, BR), lanes]
    return rot_ref[r, pl.ds(8 * q, BR), :]


def _row_spec(width=D):
    return pl.BlockSpec((BR, width), lambda i: (i, 0))


def _x_spec():
    return pl.BlockSpec((BR, D), lambda i: (jnp.maximum(i - 1, 0), 0))


def _const_spec(shape):
    nd = len(shape)
    return pl.BlockSpec(shape, lambda i: (0,) * nd)


def _rms_u(head, x, g_mix, nb):
    def body(head_ref, x_ref, g_ref, u_ref):
        i = pl.program_id(0)
        h = jnp.where(i == 0, head_ref[...], x_ref[...])
        r = lax.rsqrt(jnp.mean(h * h, axis=-1, keepdims=True) + RMS_EPS)
        u_ref[...] = ((h * r) * g_ref[...]).astype(BF16)

    return pl.pallas_call(
        body, name="rms_u",
        grid=(nb,),
        in_specs=[_const_spec((BR, D)), _x_spec(), _const_spec((1, D))],
        out_specs=_row_spec(),
        out_shape=jax.ShapeDtypeStruct((nb * BR, D), BF16),
        compiler_params=_cparams(("arbitrary",)),
    )(head, x, g_mix)


def _in_proj_rows(tp):
    return tp // 4 if (tp // 4) % 16 == 0 else BR


def _in_proj_own(u, w_own, idx, nb):
    tp = nb * BR
    wcols = w_own.shape[1]
    rows = _in_proj_rows(tp)

    def body(idx_ref, u_ref, w_ref, z_ref, wb_ref):
        @pl.when(pl.program_id(0) == 0)
        def _():
            wb_ref[...] = w_ref[...].astype(BF16)

        z_ref[...] = _dot(u_ref[...], wb_ref[...])

    return pl.pallas_call(
        body, name="in_proj_own",
        grid_spec=pltpu.PrefetchScalarGridSpec(
            num_scalar_prefetch=1, grid=(tp // rows,),
            in_specs=[pl.BlockSpec((rows, D), lambda i, idx_ref: (i, 0)),
                      pl.BlockSpec((D, wcols), lambda i, idx_ref: (0, 0))],
            out_specs=pl.BlockSpec((rows, wcols), lambda i, idx_ref: (i, idx_ref[0])),
            scratch_shapes=[pltpu.VMEM((D, wcols), BF16)]),
        out_shape=jax.ShapeDtypeStruct((tp, N_SHARD * wcols), F32),
        compiler_params=_cparams(("arbitrary",)),
    )(idx, u, w_own)


def _in_proj_rest(u, w_in_b, z, idx, dep, nb):
    tp = nb * BR
    wcols = w_in_b.shape[2]
    rows = _in_proj_rows(tp)

    def body(idx_ref, u_ref, w_ref, z_in, dep_ref, z_ref):
        z_ref[...] = _dot(u_ref[...], w_ref[...])

    any_spec = pl.BlockSpec(memory_space=pl.ANY)
    return pl.pallas_call(
        body, name="in_proj_rest",
        grid_spec=pltpu.PrefetchScalarGridSpec(
            num_scalar_prefetch=1, grid=(N_SHARD - 1, tp // rows),
            in_specs=[pl.BlockSpec((rows, D), lambda s, i, idx_ref: (i, 0)),
                      pl.BlockSpec((None, D, wcols), lambda s, i, idx_ref: (idx_ref[1 + s], 0, 0)),
                      any_spec, any_spec],
            out_specs=pl.BlockSpec((rows, wcols), lambda s, i, idx_ref: (i, idx_ref[1 + s]))),
        out_shape=jax.ShapeDtypeStruct(z.shape, F32),
        input_output_aliases={3: 0},
        compiler_params=_cparams(("arbitrary", "arbitrary")),
    )(idx, u, w_in_b, z, dep)


def _mixers_fwd(z, head, x, b_gate, w_dw, b_dw, ln_g, ln_b, pool_scale, w_co, w_pool, w_po, w_o, nb, t_total):
    tp = nb * BR

    def body(z_prev, z_cur, z_next, head_ref, x_ref, bg_ref, wdw_ref, bdw_ref,
             lng_ref, lnb_ref, ps_ref, wco_ref, wpool_ref, wpo_ref, wo_ref,
             h1_ref, yc_ref, yp_ref, mg_ref, ca_ref, cpre_ref, m_ref, mw_ref, m2b_ref, ext_ref, pext_ref, rot_ref):
        i = pl.program_id(0)
        avp, agp, pp = _cols(z_prev, 3)
        av, ag, pc, za, zb = _cols(z_cur, 5)
        avn, agn, pn = _cols(z_next, 3)
        _fill_ext(ext_ref, avp[...] * _sigmoid(agp[...]), av[...] * _sigmoid(ag[...]),
                  avn[...] * _sigmoid(agn[...]), i, nb)
        _fill_ext(pext_ref, pp[...], pc[...], pn[...], i, nb)

        def conv_chunk(c, carry):
            lanes = pl.ds(pl.multiple_of(c * 128, 128), 128)
            _fill_rot(rot_ref, ext_ref, lanes)
            acc = jnp.broadcast_to(bdw_ref[:, lanes], (BR, 128))
            for k in range(KW):
                acc = acc + wdw_ref[k:k + 1, lanes] * _tap(rot_ref, ext_ref, lanes, 1 + k)
            cpre_ref[:, lanes] = acc
            return carry
        lax.fori_loop(0, D // 128, conv_chunk, 0)

        conv = cpre_ref[...]
        mu = jnp.mean(conv, axis=-1, keepdims=True)
        xc = conv - mu
        rstd = lax.rsqrt(jnp.mean(xc * xc, axis=-1, keepdims=True) + LN_EPS)
        ln = (xc * rstd) * lng_ref[...] + lnb_ref[...]
        cact = (ln * _sigmoid(ln)).astype(BF16)
        ca_ref[...] = cact
        y_conv = _dot(cact, wco_ref[...])
        yc_ref[...] = y_conv

        t = _row_ids(i, BR)
        for gi, w in enumerate(POOL_WINDOWS):
            left = w // 2
            right = w - 1 - left
            lanes = slice(gi * GD, (gi + 1) * GD)
            s = pext_ref[pl.ds(HALO - left, BR), lanes]
            for j in range(-left + 1, right + 1):
                s = s + pext_ref[pl.ds(HALO + j, BR), lanes]
            m = (s / _pool_cnt(t, w, t_total) - pext_ref[HALO:HALO + BR, lanes]).astype(BF16)
            m_ref[:, lanes] = m
            mw_ref[:, lanes] = _dot(m, wpool_ref[gi])
        mw = mw_ref[...]
        m2b = (mw * ps_ref[...]).astype(BF16)
        m2b_ref[...] = m2b
        y_pool = _dot(m2b, wpo_ref[...])
        yp_ref[...] = y_pool

        s_a = _sigmoid(za[...] + bg_ref[:, 0:D])
        s_b = _sigmoid(zb[...] + bg_ref[:, D:2 * D])
        merged = (s_a * y_conv + s_b * y_pool).astype(BF16)
        mg_ref[...] = merged
        h0 = jnp.where(i == 0, head_ref[...], x_ref[...])
        h1_ref[...] = h0 + _dot(merged, wo_ref[...])

    in_specs = (_halo_specs(nb, 3 * D, 5 * D)
                + [_const_spec((BR, D)), _x_spec(), _const_spec((1, 2 * D)), _const_spec((32, D)),
                   _const_spec((1, D)), _const_spec((1, D)), _const_spec((1, D)), _const_spec((1, D)),
                   _const_spec((D, D)), _const_spec((4, GD, GD)), _const_spec((D, D)), _const_spec((D, D))])
    outs = [(F32, "h1"), (F32, "yc"), (F32, "yp"), (BF16, "mg"), (BF16, "ca"), (F32, "cpre"), (BF16, "m"), (F32, "mw"),
            (BF16, "m2b")]
    return pl.pallas_call(
        body, name="mixers_fwd",
        grid=(nb,),
        in_specs=in_specs,
        out_specs=[_row_spec() for _ in outs],
        out_shape=[jax.ShapeDtypeStruct((tp, D), dt) for dt, _ in outs],
        scratch_shapes=[pltpu.VMEM((EXT, D), F32), pltpu.VMEM((EXT, D), F32), pltpu.VMEM((8, ROT_ROWS, 128), F32)],
        compiler_params=_cparams(("arbitrary",)),
    )(z, z, z, head, x, b_gate, w_dw, b_dw, ln_g, ln_b, pool_scale, w_co, w_pool, w_po, w_o)


def _ffn_fwd_bwd(h1, target, g_ffn, g_final, w_g, w_u, w_d, nb):
    tp = nb * BR

    def body(h1_ref, tgt_ref, gf_ref, gfin_ref, wg_hbm, wu_hbm, wd_hbm,
             dh1_ref, dh1b_ref, vb_ref, fb_ref, dgb_ref, dub_ref, dh2b_ref, loss_ref, dgf_ref, dgfin_ref,
             wg_ref, wu_ref, wd_ref, sem):
        i = pl.program_id(0)

        @pl.when(i == 0)
        def _():
            copies = [pltpu.make_async_copy(wg_hbm, wg_ref, sem.at[0]),
                      pltpu.make_async_copy(wu_hbm, wu_ref, sem.at[1]),
                      pltpu.make_async_copy(wd_hbm, wd_ref, sem.at[2])]
            for cp in copies:
                cp.start()
            loss_ref[...] = jnp.zeros_like(loss_ref)
            dgf_ref[...] = jnp.zeros_like(dgf_ref)
            dgfin_ref[...] = jnp.zeros_like(dgfin_ref)
            for cp in copies:
                cp.wait()

        h1 = h1_ref[...]
        r1 = lax.rsqrt(jnp.mean(h1 * h1, axis=-1, keepdims=True) + RMS_EPS)
        vn = h1 * r1
        vb = (vn * gf_ref[...]).astype(BF16)
        vb_ref[...] = vb
        g = _dot_nt(vb, wg_ref[...])
        up = _dot_nt(vb, wu_ref[...])
        sg = _sigmoid(g)
        sl = g * sg
        fb = (sl * up).astype(BF16)
        fb_ref[...] = fb
        h2 = h1 + _dot(fb, wd_ref[...])
        r2 = lax.rsqrt(jnp.mean(h2 * h2, axis=-1, keepdims=True) + RMS_EPS)
        yn = h2 * r2
        valid = i > 0
        diff = jnp.where(valid, yn * gfin_ref[...] - tgt_ref[...], 0.0)
        loss_ref[...] += 0.5 * jnp.sum(jnp.mean(diff * diff, axis=-1, keepdims=True))
        dy = diff * (1.0 / D)
        dgfin_ref[...] += jnp.sum(dy * yn, axis=0, keepdims=True)
        gd = dy * gfin_ref[...]
        dh2 = r2 * (gd - yn * jnp.mean(yn * gd, axis=-1, keepdims=True))
        dh2b = dh2.astype(BF16)
        dh2b_ref[...] = dh2b
        df = _dot_nt(dh2b, wd_ref[...])
        dub = (df * sl).astype(BF16)
        dgb = (df * up * (sg * (1.0 + g * (1.0 - sg)))).astype(BF16)
        dub_ref[...] = dub
        dgb_ref[...] = dgb
        dv = _dot(dgb, wg_ref[...]) + _dot(dub, wu_ref[...])
        dgf_ref[...] += jnp.sum(dv * vn, axis=0, keepdims=True)
        gd1 = dv * gf_ref[...]
        dh1 = dh2 + r1 * (gd1 - vn * jnp.mean(vn * gd1, axis=-1, keepdims=True))
        dh1_ref[...] = dh1
        dh1b_ref[...] = dh1.astype(BF16)

    any_spec = pl.BlockSpec(memory_space=pl.ANY)
    return pl.pallas_call(
        body, name="ffn_fwd_bwd",
        grid=(nb,),
        in_specs=[_row_spec(), _x_spec(), _const_spec((1, D)), _const_spec((1, D)), any_spec, any_spec, any_spec],
        out_specs=[_row_spec(), _row_spec(), _row_spec(), _row_spec(D_FF), _row_spec(D_FF), _row_spec(D_FF), _row_spec(),
                   _const_spec((1, 1)), _const_spec((1, D)), _const_spec((1, D))],
        out_shape=[jax.ShapeDtypeStruct((tp, D), F32), jax.ShapeDtypeStruct((tp, D), BF16),
                   jax.ShapeDtypeStruct((tp, D), BF16), jax.ShapeDtypeStruct((tp, D_FF), BF16),
                   jax.ShapeDtypeStruct((tp, D_FF), BF16), jax.ShapeDtypeStruct((tp, D_FF), BF16),
                   jax.ShapeDtypeStruct((tp, D), BF16), jax.ShapeDtypeStruct((1, 1), F32),
                   jax.ShapeDtypeStruct((1, D), F32), jax.ShapeDtypeStruct((1, D), F32)],
        scratch_shapes=[pltpu.VMEM((D_FF, D), BF16), pltpu.VMEM((D_FF, D), BF16), pltpu.VMEM((D_FF, D), BF16),
                        pltpu.SemaphoreType.DMA((3,))],
        compiler_params=_cparams(("arbitrary",)),
    )(h1, target, g_ffn, g_final, w_g, w_u, w_d)


def _mixers_bwd_rows(dh1b, yc, yp, z, b_gate, cpre, ln_g, ln_b, mw, pool_scale, w_o, w_co, w_po, w_pool, dep, nb):
    tp = nb * BR

    def body(dh1b_ref, yc_ref, yp_ref, za, zb, bg_ref, cpre_ref, lng_ref, lnb_ref, mw_ref, ps_ref,
             wo_ref, wco_ref, wpo_ref, wpool_ref, dep_ref,
             dycb_ref, dypb_ref, dzg_ref, dconv_ref, dmwb_ref, dm_ref, dbg_ref, dlng_ref, dlnb_ref, dbdw_ref, dps_ref):
        i = pl.program_id(0)

        @pl.when(i == 0)
        def _():
            for r in (dbg_ref, dlng_ref, dlnb_ref, dbdw_ref, dps_ref):
                r[...] = jnp.zeros_like(r)

        dmg = _dot_nt(dh1b_ref[...], wo_ref[...])
        s_a = _sigmoid(za[...] + bg_ref[:, 0:D])
        s_b = _sigmoid(zb[...] + bg_ref[:, D:2 * D])
        dycb = (dmg * s_a).astype(BF16)
        dypb = (dmg * s_b).astype(BF16)
        dycb_ref[...] = dycb
        dypb_ref[...] = dypb
        dza = dmg * yc_ref[...] * (s_a * (1.0 - s_a))
        dzb = dmg * yp_ref[...] * (s_b * (1.0 - s_b))
        dzg_ref[:, 0:D] = dza.astype(BF16)
        dzg_ref[:, D:2 * D] = dzb.astype(BF16)
        dbg_ref[:, 0:D] += jnp.sum(dza, axis=0, keepdims=True)
        dbg_ref[:, D:2 * D] += jnp.sum(dzb, axis=0, keepdims=True)

        dca = _dot_nt(dycb, wco_ref[...])
        conv = cpre_ref[...]
        mu = jnp.mean(conv, axis=-1, keepdims=True)
        xc = conv - mu
        rstd = lax.rsqrt(jnp.mean(xc * xc, axis=-1, keepdims=True) + LN_EPS)
        xhat = xc * rstd
        ln = xhat * lng_ref[...] + lnb_ref[...]
        sg = _sigmoid(ln)
        dln = dca * (sg * (1.0 + ln * (1.0 - sg)))
        dlng_ref[...] += jnp.sum(dln * xhat, axis=0, keepdims=True)
        dlnb_ref[...] += jnp.sum(dln, axis=0, keepdims=True)
        dxh = dln * lng_ref[...]
        dconv = rstd * (dxh - jnp.mean(dxh, axis=-1, keepdims=True)
                        - xhat * jnp.mean(dxh * xhat, axis=-1, keepdims=True))
        dconv_ref[...] = dconv
        dbdw_ref[...] += jnp.sum(dconv, axis=0, keepdims=True)

        dm2 = _dot_nt(dypb, wpo_ref[...])
        dps_ref[...] += jnp.sum(dm2 * mw_ref[...], axis=0, keepdims=True)
        dmwb = (dm2 * ps_ref[...]).astype(BF16)
        dmwb_ref[...] = dmwb
        for gi in range(len(POOL_WINDOWS)):
            lanes = slice(gi * GD, (gi + 1) * GD)
            dm_ref[:, lanes] = _dot_nt(dmwb[:, lanes], wpool_ref[gi])

    in_specs = [_row_spec(), _row_spec(), _row_spec(),
                pl.BlockSpec((BR, D), lambda i: (i, 3)), pl.BlockSpec((BR, D), lambda i: (i, 4)),
                _const_spec((1, 2 * D)), _row_spec(), _const_spec((1, D)), _const_spec((1, D)), _row_spec(),
                _const_spec((1, D)), _const_spec((D, D)), _const_spec((D, D)), _const_spec((D, D)),
                _const_spec((4, GD, GD)), pl.BlockSpec(memory_space=pl.ANY)]
    return pl.pallas_call(
        body, name="mixers_bwd_rows",
        grid=(nb,),
        in_specs=in_specs,
        out_specs=[_row_spec(), _row_spec(), _row_spec(2 * D), _row_spec(), _row_spec(), _row_spec(),
                   _const_spec((1, 2 * D)), _const_spec((1, D)), _const_spec((1, D)), _const_spec((1, D)),
                   _const_spec((1, D))],
        out_shape=[jax.ShapeDtypeStruct((tp, D), BF16), jax.ShapeDtypeStruct((tp, D), BF16),
                   jax.ShapeDtypeStruct((tp, 2 * D), BF16), jax.ShapeDtypeStruct((tp, D), F32),
                   jax.ShapeDtypeStruct((tp, D), BF16), jax.ShapeDtypeStruct((tp, D), F32),
                   jax.ShapeDtypeStruct((1, 2 * D), F32), jax.ShapeDtypeStruct((1, D), F32),
                   jax.ShapeDtypeStruct((1, D), F32), jax.ShapeDtypeStruct((1, D), F32),
                   jax.ShapeDtypeStruct((1, D), F32)],
        compiler_params=_cparams(("arbitrary",)),
    )(dh1b, yc, yp, z, z, b_gate, cpre, ln_g, ln_b, mw, pool_scale, w_o, w_co, w_po, w_pool, dep)


def _mixers_bwd_halo(dconv, dm, z, dzg, w_dw, head, x, g_mix, dh1, w_in_b, dep, nb, t_total):
    tp = nb * BR
    ns = w_in_b.shape[0]
    wcols = w_in_b.shape[2]
    seq = x.shape[0]

    def body(dcp, dcc, dcn, dmp, dmc, dmn, z_prev, z_cur, z_next, dzg_ref, wdw_ref, head_ref, x_ref, g_ref,
             dh1_ref, w_hbm, dep_ref,
             dzb_ref, gx_ref, dhead_ref, dwdw_ref, dgmix_ref,
             w_ref, sem, aext_ref, dext_ref, qext_ref, da_ref, rot_ref, dwp_ref):
        i = pl.program_id(0)
        (avp, agp), (av, ag), (avn, agn) = _cols(z_prev, 2), _cols(z_cur, 2), _cols(z_next, 2)

        @pl.when(i == 0)
        def _():
            cp = pltpu.make_async_copy(w_hbm, w_ref, sem.at[0])
            cp.start()
            dwp_ref[...] = jnp.zeros_like(dwp_ref)
            dgmix_ref[...] = jnp.zeros_like(dgmix_ref)
            cp.wait()

        sig_g = _sigmoid(ag[...])
        _fill_ext(aext_ref, avp[...] * _sigmoid(agp[...]), av[...] * sig_g, avn[...] * _sigmoid(agn[...]), i, nb)
        _fill_ext(dext_ref, dcp[...], dcc[...], dcn[...], i, nb)
        _fill_ext(qext_ref, dmp[...], dmc[...], dmn[...], i, nb)

        def conv_chunk(c, carry):
            lanes = pl.ds(pl.multiple_of(c * 128, 128), 128)
            _fill_rot(rot_ref, dext_ref, lanes)
            acc = jnp.zeros((BR, 128), F32)
            for k in range(KW):
                acc = acc + wdw_ref[k:k + 1, lanes] * _tap(rot_ref, dext_ref, lanes, KW - k)
            da_ref[:, lanes] = acc
            _fill_rot(rot_ref, aext_ref, lanes)
            dcv = dext_ref[HALO:HALO + BR, lanes]
            for k in range(KW):
                prod = _tap(rot_ref, aext_ref, lanes, 1 + k) * dcv
                dwp_ref[k, :, lanes] += jnp.sum(prod.reshape(BR // 8, 8, 128), axis=0)
            return carry
        lax.fori_loop(0, D // 128, conv_chunk, 0)

        @pl.when(i == nb - 1)
        def _():
            dwdw_ref[...] = jnp.sum(dwp_ref[...], axis=1)

        da = da_ref[...]
        a_val = av[...]
        dzb_ref[:, 0:D] = (da * sig_g).astype(BF16)
        dzb_ref[:, D:2 * D] = (da * a_val * (sig_g * (1.0 - sig_g))).astype(BF16)

        t_ext = _row_ids(i, EXT, -HALO)
        for gi, w in enumerate(POOL_WINDOWS):
            left = w // 2
            right = w - 1 - left
            lanes = slice(gi * GD, (gi + 1) * GD)
            qext_ref[:, lanes] = qext_ref[:, lanes] / _pool_cnt(t_ext, w, t_total)
            s = qext_ref[pl.ds(HALO - right, BR), lanes]
            for j in range(-right + 1, left + 1):
                s = s + qext_ref[pl.ds(HALO + j, BR), lanes]
            dzb_ref[:, 2 * D + gi * GD:2 * D + (gi + 1) * GD] = (s - dmc[:, lanes]).astype(BF16)
        dzb_ref[:, 3 * D:5 * D] = dzg_ref[...]

        du = _dot_nt(dzb_ref[:, 0:wcols], w_ref[0])
        for s_i in range(1, ns):
            du = du + _dot_nt(dzb_ref[:, s_i * wcols:(s_i + 1) * wcols], w_ref[s_i])
        h0 = jnp.where(i == 0, head_ref[...], x_ref[...])
        r0 = lax.rsqrt(jnp.mean(h0 * h0, axis=-1, keepdims=True) + RMS_EPS)
        un = h0 * r0
        dgmix_ref[...] += jnp.sum(du * un, axis=0, keepdims=True)
        gd = du * g_ref[...]
        dh0 = dh1_ref[...] + r0 * (gd - un * jnp.mean(un * gd, axis=-1, keepdims=True))
        gx_ref[...] = dh0

        @pl.when(i == 0)
        def _():
            dhead_ref[...] = dh0

    any_spec = pl.BlockSpec(memory_space=pl.ANY)
    in_specs = (_halo_specs(nb) + _halo_specs(nb) + _halo_specs(nb, 2 * D, 2 * D)
                + [_row_spec(2 * D), _const_spec((32, D)), _const_spec((BR, D)), _x_spec(), _const_spec((1, D)),
                   _row_spec(), any_spec, any_spec])
    return pl.pallas_call(
        body, name="mixers_bwd_halo",
        grid=(nb,),
        in_specs=in_specs,
        out_specs=[_row_spec(D_IN), _x_spec(), _const_spec((BR, D)), _const_spec((32, D)), _const_spec((1, D))],
        out_shape=[jax.ShapeDtypeStruct((tp, D_IN), BF16), jax.ShapeDtypeStruct((seq, D), F32),
                   jax.ShapeDtypeStruct((BR, D), F32), jax.ShapeDtypeStruct((32, D), F32),
                   jax.ShapeDtypeStruct((1, D), F32)],
        scratch_shapes=[pltpu.VMEM((ns, D, wcols), BF16), pltpu.SemaphoreType.DMA((1,)),
                        pltpu.VMEM((EXT, D), F32), pltpu.VMEM((EXT, D), F32), pltpu.VMEM((EXT, D), F32),
                        pltpu.VMEM((BR, D), F32), pltpu.VMEM((8, ROT_ROWS, 128), F32), pltpu.VMEM((32, 8, D), F32)],
        compiler_params=_cparams(("arbitrary",)),
    )(dconv, dconv, dconv, dm, dm, dm, z, z, z, dzg, w_dw, head, x, g_mix, dh1, w_in_b, dep)


def _wgrad(a, c, tm, tn, tk, name, diag=False, col_major=False, dep=None):
    tp, m = a.shape
    n = c.shape[1]
    nk = tp // tk
    gm, gn = m // tm, n // tn

    def body(a_ref, c_ref, *rest):
        o_ref, ob_ref = rest[-2:]
        k = pl.program_id(2)

        @pl.when(k == 0)
        def _():
            o_ref[...] = jnp.zeros_like(o_ref)

        o_ref[...] += _dot_tn(a_ref[...], c_ref[...])

        @pl.when(k == nk - 1)
        def _():
            ob_ref[...] = o_ref[...].astype(BF16)

    c_map = lambda i, j, k: (k, j)
    grid = (gm, gn, nk)
    deps = [] if dep is None else [dep]
    if diag:
        grid = (gm, 1, nk)
        c_map = lambda i, j, k: (k, i)
        o_spec = pl.BlockSpec((tm, tn), lambda i, j, k: (i, 0))
        o_shape = (m, tn)
    elif col_major:
        o_spec = pl.BlockSpec((None, tm, tn), lambda i, j, k: (j, i, 0))
        o_shape = (gn, m, tn)
    else:
        o_spec = pl.BlockSpec((tm, tn), lambda i, j, k: (i, j))
        o_shape = (m, n)
    return pl.pallas_call(
        body, name=name,
        grid=grid,
        in_specs=[pl.BlockSpec((tk, tm), lambda i, j, k: (k, i)), pl.BlockSpec((tk, tn), c_map)]
        + [pl.BlockSpec(memory_space=pl.ANY)] * len(deps),
        out_specs=[o_spec, o_spec],
        out_shape=[jax.ShapeDtypeStruct(o_shape, F32), jax.ShapeDtypeStruct(o_shape, BF16)],
        compiler_params=_cparams(("arbitrary", "arbitrary", "arbitrary")),
    )(a, c, *deps)


def _place():
    x, y, c = lax.axis_index("x"), lax.axis_index("y"), lax.axis_index("c")
    others = [(1 - x, y), (x, 1 - y), (1 - x, 1 - y)]
    return x, y, c, others


def _split2(a, axis=0):
    return a.reshape(a.shape[:axis] + (2, a.shape[axis] // 2) + a.shape[axis + 1:])


def _merge2(a, axis=0):
    return a.reshape(a.shape[:axis] + (2 * a.shape[axis + 1],) + a.shape[axis + 2:])


def _cast_into_slot(shards, chip, dep, name):
    n = len(shards)
    r, c = shards[0].shape
    r2 = r // 2

    def body(chip_ref, *refs):
        for a in range(n):
            refs[n + 1 + a][...] = refs[a][...].astype(BF16)

    out = pl.pallas_call(
        body, name=name,
        grid_spec=pltpu.PrefetchScalarGridSpec(
            num_scalar_prefetch=1, grid=(2,),
            in_specs=[pl.BlockSpec((r2, c), lambda h, chip_ref: (h, 0))] * n + [pl.BlockSpec(memory_space=pl.ANY)],
            out_specs=[pl.BlockSpec((None, None, r2, c), lambda h, chip_ref: (chip_ref[0], h, 0, 0))] * n),
        out_shape=[jax.ShapeDtypeStruct((N_SHARD, 2, r2, c), BF16)] * n,
        compiler_params=_cparams(("arbitrary",)),
    )(chip, *shards, dep)
    return list(out)


GATHER_ID, SMALL_ID = 1, 2
SCATTER_IDS = {"ffn": 3, "mix": 4, "in": 5}
FLIPS = [(dx, dy, dc) for dx in (0, 1) for dy in (0, 1) for dc in (0, 1)][1:]
HBM_SPEC = pl.BlockSpec(memory_space=pltpu.HBM)
SEM_SPEC = pl.BlockSpec(memory_space=pltpu.SEMAPHORE)
DATAFLOW = pltpu.SideEffectType.DATAFLOW_SIDE_EFFECTING
TOKEN = jax.ShapeDtypeStruct((8, 128), F32)


def _in_hbm(a):
    return pltpu.with_memory_space_constraint(a, pltpu.HBM)


def _gather_tiny(v):
    vm = pl.BlockSpec(memory_space=pltpu.VMEM)

    def body(v_ref, out_ref, send_sems, recv_sems):
        x, y, c, others = _place()
        mine = 2 * x + y
        sends = [pltpu.make_async_remote_copy(
            src_ref=v_ref, dst_ref=out_ref.at[mine], send_sem=send_sems.at[j], recv_sem=recv_sems.at[j],
            device_id=(*chip, c), device_id_type=MESH) for j, chip in enumerate(others)]
        for cp in sends:
            cp.start()
        out_ref[mine] = v_ref[...]
        for j, chip in enumerate(others):
            landed = out_ref.at[2 * chip[0] + chip[1]]
            pltpu.make_async_remote_copy(
                src_ref=landed, dst_ref=landed, send_sem=send_sems.at[j], recv_sem=recv_sems.at[j],
                device_id=(x, y, c), device_id_type=MESH).wait_recv()
        for cp in sends:
            cp.wait_send()

    return pl.pallas_call(
        body, name="gather_tiny",
        in_specs=[vm], out_specs=vm,
        out_shape=jax.ShapeDtypeStruct((N_SHARD,) + v.shape, v.dtype),
        scratch_shapes=[pltpu.SemaphoreType.DMA((3,)), pltpu.SemaphoreType.DMA((3,))],
    )(v)


def _ici_copies(srcs, dsts, send_sems, recv_sems, started):
    x, y, c, others = _place()
    mine = 2 * x + y
    copies = []
    for a in range(len(srcs)):
        for j, chip in enumerate(others):
            there = 2 * chip[0] + chip[1]
            src, dst = srcs[a](mine, there, c), dsts[a](mine, there, c)
            if not started:
                dst = dsts[a](there, mine, c)
            copies.append(pltpu.make_async_remote_copy(
                src_ref=src, dst_ref=dst, send_sem=send_sems.at[a * 3 + j], recv_sem=recv_sems.at[a * 3 + j],
                device_id=(*chip, c), device_id_type=MESH))
    return copies


def _same_core_peers():
    x, y, c, others = _place()
    return [(*chip, c) for chip in others]


def _all_peers():
    x, y, c, _ = _place()
    return [(jnp.bitwise_xor(x, dx), jnp.bitwise_xor(y, dy), jnp.bitwise_xor(c, dc)) for dx, dy, dc in FLIPS]


def _split_start(srcs_of, dsts_of, arrays, n_src, name, collective_id, peers_of=_same_core_peers,
                 copies_of=None, n_sems=None, dep=None):
    n = len(arrays)
    n_sems = n_sems or 3 * n_src
    copies_of = copies_of or (lambda ins, ss, rs, started: _ici_copies(srcs_of(ins), dsts_of(ins), ss, rs, started))
    deps = [] if dep is None else [dep]
    nd = len(deps)

    def body(*refs):
        ins = refs[:n]
        send_sems, recv_sems = refs[n + nd], refs[n + nd + 1]
        token = refs[2 * n + nd + 2]
        peers = peers_of()
        barrier = pltpu.get_barrier_semaphore()
        for peer in peers:
            pl.semaphore_signal(barrier, inc=1, device_id=peer, device_id_type=MESH)
        pl.semaphore_wait(barrier, len(peers))
        for cp in copies_of(ins, send_sems, recv_sems, True):
            cp.start()
        token[...] = jnp.zeros_like(token)

    out = pl.pallas_call(
        body, name=name,
        in_specs=[HBM_SPEC] * n + [pl.BlockSpec(memory_space=pl.ANY)] * nd,
        out_specs=(SEM_SPEC, SEM_SPEC, *([HBM_SPEC] * n), pl.BlockSpec(memory_space=pltpu.VMEM)),
        out_shape=(pltpu.SemaphoreType.DMA((n_sems,)), pltpu.SemaphoreType.DMA((n_sems,)),
                   *[pltpu.HBM(a.shape, a.dtype) for a in arrays], TOKEN),
        input_output_aliases={a: 2 + a for a in range(n)},
        compiler_params=pltpu.CompilerParams(has_side_effects=DATAFLOW, collective_id=collective_id),
    )(*[_in_hbm(a) for a in arrays], *deps)
    return out[0], out[1], list(out[2:2 + n]), out[2 + n]


def _split_wait(srcs_of, dsts_of, send_sems, recv_sems, arrays, after, name, copies_of=None):
    n = len(arrays)
    copies_of = copies_of or (lambda ins, ss, rs, started: _ici_copies(srcs_of(ins), dsts_of(ins), ss, rs, started))

    def body(*refs):
        ins = refs[:n]
        send_sems, recv_sems = refs[n], refs[n + 1]
        for cp in copies_of(ins, send_sems, recv_sems, False):
            cp.wait_send()
            cp.wait_recv()

    return pl.pallas_call(
        body, name=name,
        in_specs=[HBM_SPEC] * n + [SEM_SPEC, SEM_SPEC] + [pl.BlockSpec(memory_space=pl.ANY)] * len(after),
        out_specs=[HBM_SPEC] * n,
        out_shape=[pltpu.HBM(a.shape, a.dtype) for a in arrays],
        input_output_aliases={a: a for a in range(n)},
        compiler_params=pltpu.CompilerParams(has_side_effects=DATAFLOW),
    )(*arrays, send_sems, recv_sems, *after)


def _gather_views(ins):
    view = [lambda frm, to, c, r=r: r.at[frm, c] for r in ins]
    return view


def _gather_start(bufs, dep, name):
    return _split_start(_gather_views, _gather_views, bufs, len(bufs), name, GATHER_ID, dep=dep)


def _gather_wait(send_sems, recv_sems, bufs, after, name):
    return _split_wait(_gather_views, _gather_views, send_sems, recv_sems, bufs, after, name)


SIBLING_ONLY = pltpu.CompilerParams(collective_id=0)


def _sibling_handshake(x, y, c):
    barrier = pltpu.get_barrier_semaphore()
    pl.semaphore_signal(barrier, inc=1, device_id=(x, y, 1 - c), device_id_type=MESH)
    pl.semaphore_wait(barrier, 1)


def _forward_halves(bufs, name):
    n = len(bufs)
    any_spec = pl.BlockSpec(memory_space=pl.ANY)

    def body(*refs):
        outs = refs[n:2 * n]
        send_sems, recv_sems = refs[2 * n:]
        x, y, c, others = _place()
        _sibling_handshake(x, y, c)
        copies = []
        for a in range(n):
            for j, chip in enumerate(others):
                landed = outs[a].at[2 * chip[0] + chip[1], c]
                copies.append(pltpu.make_async_remote_copy(
                    src_ref=landed, dst_ref=landed, send_sem=send_sems.at[a * 3 + j], recv_sem=recv_sems.at[a * 3 + j],
                    device_id=(x, y, 1 - c), device_id_type=MESH))
        for cp in copies:
            cp.start()
        for a in range(n):
            for j, chip in enumerate(others):
                landed = outs[a].at[2 * chip[0] + chip[1], 1 - c]
                pltpu.make_async_remote_copy(
                    src_ref=landed, dst_ref=landed, send_sem=send_sems.at[a * 3 + j], recv_sem=recv_sems.at[a * 3 + j],
                    device_id=(x, y, c), device_id_type=MESH).wait_recv()
        for cp in copies:
            cp.wait_send()

    out = pl.pallas_call(
        body, name=name,
        in_specs=[any_spec] * n, out_specs=[any_spec] * n,
        out_shape=[jax.ShapeDtypeStruct(b.shape, b.dtype) for b in bufs],
        input_output_aliases={a: a for a in range(n)},
        scratch_shapes=[pltpu.SemaphoreType.DMA((3 * n,)), pltpu.SemaphoreType.DMA((3 * n,))],
        compiler_params=SIBLING_ONLY,
    )(*bufs)
    return [_merge2(o, 1) for o in out]


def _swap_halves_bf16(gbs, name):
    n = len(gbs)
    any_spec = pl.BlockSpec(memory_space=pl.ANY)

    def body(*refs):
        ins, outs = refs[:n], refs[n:2 * n]
        send_sems, recv_sems = refs[2 * n:]
        x, y, c, _ = _place()
        _sibling_handshake(x, y, c)
        copies = []
        for a in range(n):
            copies.append(pltpu.make_async_remote_copy(
                src_ref=ins[a].at[:, 1 - c], dst_ref=outs[a], send_sem=send_sems.at[a], recv_sem=recv_sems.at[a],
                device_id=(x, y, 1 - c), device_id_type=MESH))
        for cp in copies:
            cp.start()
        for cp in copies:
            cp.wait()

    return pl.pallas_call(
        body, name=name,
        in_specs=[any_spec] * n, out_specs=[any_spec] * n,
        out_shape=[jax.ShapeDtypeStruct((g.shape[0], g.shape[1] // 2, g.shape[2]), g.dtype) for g in gbs],
        scratch_shapes=[pltpu.SemaphoreType.DMA((n,)), pltpu.SemaphoreType.DMA((n,))],
        compiler_params=SIBLING_ONLY,
    )(*[_split2(g, 1) for g in gbs])


def _scatter_srcs(n):
    return lambda ins: [lambda frm, to, c, r=r: r.at[to] for r in ins[:n]]


def _scatter_dsts(n):
    return lambda ins: [lambda frm, to, c, r=r: r.at[frm] for r in ins[n:]]


def _scatter_start(hbs, name, collective_id):
    n = len(hbs)
    lands = [lax.empty(h.shape, h.dtype) for h in hbs]
    return _split_start(_scatter_srcs(n), _scatter_dsts(n), list(hbs) + lands, n, name, collective_id)


def _scatter_wait(send_sems, recv_sems, arrays, after, name):
    n = len(arrays) // 2
    return _split_wait(_scatter_srcs(n), _scatter_dsts(n), send_sems, recv_sems, arrays, after, name)[n:]


def _join_halves(rhs, name):
    n = len(rhs)
    any_spec = pl.BlockSpec(memory_space=pl.ANY)

    def body(*refs):
        outs = refs[n:2 * n]
        send_sems, recv_sems = refs[2 * n:]
        x, y, c, _ = _place()
        _sibling_handshake(x, y, c)
        copies = []
        for a in range(n):
            copies.append(pltpu.make_async_remote_copy(
                src_ref=outs[a].at[c], dst_ref=outs[a].at[c], send_sem=send_sems.at[a],
                recv_sem=recv_sems.at[a], device_id=(x, y, 1 - c), device_id_type=MESH))
        for cp in copies:
            cp.start()
        for a in range(n):
            landed = outs[a].at[1 - c]
            pltpu.make_async_remote_copy(
                src_ref=landed, dst_ref=landed, send_sem=send_sems.at[a], recv_sem=recv_sems.at[a],
                device_id=(x, y, c), device_id_type=MESH).wait_recv()
        for cp in copies:
            cp.wait_send()

    out = pl.pallas_call(
        body, name=name,
        in_specs=[any_spec] * n, out_specs=[any_spec] * n,
        out_shape=[jax.ShapeDtypeStruct(r.shape, r.dtype) for r in rhs],
        input_output_aliases={a: a for a in range(n)},
        scratch_shapes=[pltpu.SemaphoreType.DMA((n,)), pltpu.SemaphoreType.DMA((n,))],
        compiler_params=SIBLING_ONLY,
    )(*rhs)
    return [_merge2(o) for o in out]


def _peer_copies(ins, send_sems, recv_sems, started):
    x, y, c, _ = _place()
    copies = []
    for k, (dx, dy, dc) in enumerate(FLIPS):
        px, py, pc = jnp.bitwise_xor(x, dx), jnp.bitwise_xor(y, dy), jnp.bitwise_xor(c, dc)
        slot = 4 * x + 2 * y + c if started else 4 * px + 2 * py + pc
        copies.append(pltpu.make_async_remote_copy(
            src_ref=ins[0], dst_ref=ins[1].at[slot], send_sem=send_sems.at[k], recv_sem=recv_sems.at[k],
            device_id=(px, py, pc), device_id_type=MESH))
    return copies


def _small_start(v, name):
    land = lax.empty((8,) + v.shape, v.dtype)
    return _split_start(None, None, [v, land], 0, name, SMALL_ID, peers_of=_all_peers, copies_of=_peer_copies,
                        n_sems=len(FLIPS))


def _small_wait(send_sems, recv_sems, arrays, after, name):
    return _split_wait(None, None, send_sems, recv_sems, arrays, after, name, copies_of=_peer_copies)[1]


def _sum_slots(land, v, me):
    rows, cols = v.shape

    def body(me_ref, land_ref, v_ref, o_ref):
        o_ref[...] = jnp.zeros_like(o_ref)
        for d in range(8):
            @pl.when(me_ref[0] == d)
            def _():
                o_ref[...] += v_ref[...]

            @pl.when(me_ref[0] != d)
            def _():
                o_ref[...] += land_ref[d]

    return pl.pallas_call(
        body, name="sum_slots",
        grid_spec=pltpu.PrefetchScalarGridSpec(
            num_scalar_prefetch=1, grid=(1,),
            in_specs=[pl.BlockSpec((8, rows, cols), lambda i, me_ref: (0, 0, 0)),
                      pl.BlockSpec((rows, cols), lambda i, me_ref: (0, 0))],
            out_specs=pl.BlockSpec((rows, cols), lambda i, me_ref: (0, 0))),
        out_shape=jax.ShapeDtypeStruct((rows, cols), F32),
        compiler_params=_cparams(("arbitrary",)),
    )(me, land, v)


def _by_shape(arrays):
    groups = {}
    for k, a in enumerate(arrays):
        groups.setdefault(a.shape, []).append(k)
    return list(groups.values())


def _add_sibling_half(gs, sbs, idx, name):
    n = len(gs)
    ns, r, c = gs[0].shape
    r2 = r // 2

    def body(idx_ref, *refs):
        for a in range(n):
            g_ref, sb_ref, hown_ref, hb_ref = refs[a], refs[n + a], refs[2 * n + a], refs[3 * n + a]
            h = g_ref[...] + sb_ref[...].astype(F32)
            hb_ref[...] = h.astype(BF16)

            @pl.when(pl.program_id(0) == idx_ref[0])
            def _():
                hown_ref[...] = h

    spec = pl.BlockSpec((None, r2, c), lambda s, idx_ref: (s, 0, 0))
    out = pl.pallas_call(
        body, name=name,
        grid_spec=pltpu.PrefetchScalarGridSpec(
            num_scalar_prefetch=1, grid=(ns,),
            in_specs=[pl.BlockSpec((None, r2, c), lambda s, idx_ref: (s, idx_ref[4], 0))] * n + [spec] * n,
            out_specs=[pl.BlockSpec((r2, c), lambda s, idx_ref: (0, 0))] * n + [spec] * n),
        out_shape=[jax.ShapeDtypeStruct((r2, c), F32)] * n + [jax.ShapeDtypeStruct((ns, r2, c), BF16)] * n,
        compiler_params=_cparams(("arbitrary",)),
    )(idx, *gs, *sbs)
    return [(out[a], out[n + a]) for a in range(n)]


def _add_chip_slabs(hs, rbs, idx, name):
    n = len(hs)
    r2, c = hs[0].shape

    def body(idx_ref, *refs):
        for a in range(n):
            h_ref, r0_ref, r1_ref, r2_ref = refs[4 * a:4 * a + 4]
            refs[4 * n + a][...] = ((h_ref[...] + r0_ref[...].astype(F32)) + r1_ref[...].astype(F32)) + r2_ref[...].astype(F32)

    def pick(k):
        return pl.BlockSpec((None, r2, c), lambda i, idx_ref: (idx_ref[k], 0, 0))

    operands = []
    for h, rb in zip(hs, rbs):
        operands += [h, rb, rb, rb]
    out = pl.pallas_call(
        body, name=name,
        grid_spec=pltpu.PrefetchScalarGridSpec(
            num_scalar_prefetch=1, grid=(1,),
            in_specs=[pl.BlockSpec((r2, c), lambda i, idx_ref: (0, 0)), pick(1), pick(2), pick(3)] * n,
            out_specs=[pl.BlockSpec((None, r2, c), lambda i, idx_ref: (idx_ref[4], 0, 0))] * n),
        out_shape=[jax.ShapeDtypeStruct((2, r2, c), F32)] * n,
        compiler_params=_cparams(("arbitrary",)),
    )(idx, *operands)
    return list(out)


ELEMENTWISE_VMEM = 16 * 1024 * 1024


def _adamw(items, name):
    n = len(items)
    r, c = items[0][0].shape
    br = max(b for b in range(8, r + 1, 8) if r % b == 0 and n * 16 * b * c * 4 <= ELEMENTWISE_VMEM) if r % 8 == 0 else r

    def body(*refs):
        for a in range(n):
            g_ref, w_ref, m_ref, v_ref = refs[4 * a:4 * a + 4]
            go_ref, d_ref, nm_ref, nv_ref = refs[4 * n + 4 * a:4 * n + 4 * a + 4]
            gg = g_ref[...]
            go_ref[...] = gg
            nm = B1 * m_ref[...] + (1.0 - B1) * gg
            nv = B2 * v_ref[...] + (1.0 - B2) * jnp.square(gg)
            m_hat = nm / (1.0 - B1 ** STEP)
            v_hat = nv / (1.0 - B2 ** STEP)
            d_ref[...] = -LR * (m_hat / (jnp.sqrt(v_hat) + ADAM_EPS) + WD * w_ref[...])
            nm_ref[...] = nm
            nv_ref[...] = nv

    spec = pl.BlockSpec((br, c), lambda i: (i, 0))
    out = pl.pallas_call(
        body, name=name,
        grid=(r // br,),
        in_specs=[spec] * (4 * n), out_specs=[spec] * (4 * n),
        out_shape=[jax.ShapeDtypeStruct((r, c), F32)] * (4 * n),
        compiler_params=_cparams(("arbitrary",)),
    )(*[a for item in items for a in item])
    return [tuple(out[4 * a:4 * a + 4]) for a in range(n)]


BIG = ("w_in", "w_conv_out", "w_pool", "w_pool_out", "w_o", "w_ffn_gate", "w_ffn_up", "w_ffn_down")
REPL = ("g_mix", "b_gate", "b_dw", "ln_g", "ln_b", "pool_scale", "g_ffn", "g_final")
GROUP_MIX = ("w_conv_out", "w_pool", "w_pool_out", "w_o")
GROUP_FFN = ("w_ffn_gate", "w_ffn_up", "w_ffn_down")
TRANSPOSED = ("w_ffn_gate", "w_ffn_up")
WEIGHT_ORDER = ("meta_tokens", "g_mix", "w_in", "b_gate", "w_dw", "b_dw", "ln_g", "ln_b", "w_conv_out", "w_pool",
                "pool_scale", "w_pool_out", "w_o", "g_ffn", "w_ffn_gate", "w_ffn_up", "w_ffn_down", "g_final")


def _shard2d(name, a):
    a = a[0]
    if name == "w_pool":
        return a.reshape(4 * 64, GD)
    if name in TRANSPOSED:
        return a.T
    return a


def _unshard2d(name, a, shape):
    return a.T.reshape(shape) if name in TRANSPOSED else a.reshape(shape)


def _slabs_to_cols(a):
    ns, m, c = a.shape
    return a.transpose(1, 0, 2).reshape(m, ns * c)


def kernel(x, meta_tokens, g_mix, w_in, b_gate, w_dw, b_dw, ln_g, ln_b, w_conv_out, w_pool, pool_scale, w_pool_out, w_o, g_ffn, w_ffn_gate, w_ffn_up, w_ffn_down, g_final, loss_target, m_meta_tokens, m_g_mix, m_w_in, m_b_gate, m_w_dw, m_b_dw, m_ln_g, m_ln_b, m_w_conv_out, m_w_pool, m_pool_scale, m_w_pool_out, m_w_o, m_g_ffn, m_w_ffn_gate, m_w_ffn_up, m_w_ffn_down, m_g_final, v_meta_tokens, v_g_mix, v_w_in, v_b_gate, v_w_dw, v_b_dw, v_ln_g, v_ln_b, v_w_conv_out, v_w_pool, v_pool_scale, v_w_pool_out, v_w_o, v_g_ffn, v_w_ffn_gate, v_w_ffn_up, v_w_ffn_down, v_g_final):
    args = dict(locals())
    w = {n: args[n] for n in WEIGHT_ORDER}
    mom = {n: args["m_" + n] for n in WEIGHT_ORDER}
    var = {n: args["v_" + n] for n in WEIGHT_ORDER}
    seq = x.shape[1]
    nb = seq // BR + 1
    tp = nb * BR
    tk = tp // 2 if (tp // 2) % 16 == 0 else BR
    t_total = seq + N_META
    cx, cy, cc = lax.axis_index("x"), lax.axis_index("y"), lax.axis_index("c")
    chip = 2 * cx + cy
    chip1 = jnp.reshape(chip, (1,)).astype(jnp.int32)
    core = jnp.reshape(cc, (1,)).astype(jnp.int32)
    others = jnp.sort(jnp.stack([2 * (1 - cx) + cy, 2 * cx + (1 - cy), 2 * (1 - cx) + (1 - cy)]))
    idx = jnp.concatenate([chip1, others.astype(jnp.int32), core])
    xs, target = x[0], loss_target[0]

    tiny = _gather_tiny(jnp.concatenate([w["meta_tokens"], w["w_dw"][0], jnp.zeros((1, GD), F32)], axis=0))
    small = {n: w[n] for n in REPL if n != "g_final"}
    small["g_final"] = w["g_final"].reshape(1, D)
    small["w_dw"] = _slabs_to_cols(tiny[:, N_META:])
    head = jnp.concatenate([jnp.zeros((PAD, D), F32), _slabs_to_cols(tiny[:, :N_META])], axis=0)

    def cast(group, dep):
        shards = [_shard2d(n, w[n]) for n in group]
        bufs = [None] * len(group)
        for ks in _by_shape(shards):
            done = _cast_into_slot([shards[k] for k in ks], chip1, dep, "cast_" + group[ks[0]])
            for k, b in zip(ks, done):
                bufs[k] = b
        return bufs

    def gather_finish(group, start, after, name):
        landed = _gather_wait(start[0], start[1], start[2], after, "gather_wait_" + name)
        return dict(zip(group, _forward_halves(landed, "forward_" + name)))

    st_in = _gather_start(cast(("w_in",), tiny), None, "gather_start_in")
    bufs_mix, bufs_ffn = cast(GROUP_MIX, st_in[3]), cast(GROUP_FFN, st_in[3])
    u = _rms_u(head, xs, small["g_mix"] + st_in[3][0:1, 0:1], nb)
    z_own = _in_proj_own(u, w["w_in"][0], idx, nb)
    gw = gather_finish(("w_in",), st_in, [z_own] + bufs_mix + bufs_ffn, "in")
    st_mix = _gather_start(bufs_mix, gw["w_in"], "gather_start_mix")
    z = _in_proj_rest(u, gw["w_in"], z_own, idx, st_mix[3], nb)
    gw.update(gather_finish(GROUP_MIX, st_mix, [z], "mix"))
    st_ffn = _gather_start(bufs_ffn, gw["w_o"], "gather_start_ffn")
    w_pool_b = gw["w_pool"].reshape(N_SHARD, 4, 64, GD).transpose(1, 0, 2, 3).reshape(4, GD, GD)
    w_co_b, w_po_b, w_o_b = (gw[n].reshape(D, D) for n in ("w_conv_out", "w_pool_out", "w_o"))
    h1, yc, yp, mg, ca, cpre, m, mw, m2b = _mixers_fwd(
        z, head, xs, small["b_gate"] + st_ffn[3][0, 0], small["w_dw"], small["b_dw"], small["ln_g"], small["ln_b"],
        small["pool_scale"], w_co_b, w_pool_b, w_po_b, w_o_b, nb, t_total)
    gw.update(gather_finish(GROUP_FFN, st_ffn, [h1], "ffn"))

    dh1, dh1b, vb, fb, dgb, dub, dh2b, loss, dg_ffn, dg_final = _ffn_fwd_bwd(
        h1, target, small["g_ffn"], small["g_final"], gw["w_ffn_gate"].reshape(D_FF, D),
        gw["w_ffn_up"].reshape(D_FF, D), gw["w_ffn_down"].reshape(D_FF, D), nb)

    def slabs(name, g):
        if name == "w_in":
            return g
        if name == "w_pool":
            return g.reshape(4, N_SHARD, 64, GD).transpose(1, 0, 2, 3).reshape(N_SHARD, 4 * 64, GD)
        return g.reshape(N_SHARD, g.shape[0] // N_SHARD, g.shape[1])

    def reduce_start(group, grads, name):
        g32 = [slabs(n, grads[n][0]) for n in group]
        g16 = [slabs(n, grads[n][1]) for n in group]
        from_sibling = _swap_halves_bf16(g16, "swap_halves_" + name)
        halves = [None] * len(group)
        for ks in _by_shape(g32):
            done = _add_sibling_half([g32[k] for k in ks], [from_sibling[k] for k in ks], idx, "add_sibling_" + group[ks[0]])
            for k, pair in zip(ks, done):
                halves[k] = pair
        return [h for h, _ in halves], _scatter_start([hb for _, hb in halves], "scatter_start_" + name, SCATTER_IDS[name])

    def reduce_finish(group, halves, start, after, name):
        from_chips = _scatter_wait(start[0], start[1], start[2], after, "scatter_wait_" + name)
        reduced = [None] * len(group)
        for ks in _by_shape(halves):
            done = _add_chip_slabs([halves[k] for k in ks], [from_chips[k] for k in ks], idx, "add_chips_" + group[ks[0]])
            for k, r in zip(ks, done):
                reduced[k] = r
        return reduced

    half_ff = D_FF // 2
    grads_ffn = {
        "w_ffn_gate": _wgrad(dgb, vb, half_ff, D, tk, "wgrad_ffn_gate"),
        "w_ffn_up": _wgrad(dub, vb, half_ff, D, tk, "wgrad_ffn_up"),
        "w_ffn_down": _wgrad(fb, dh2b, half_ff, D, tk, "wgrad_ffn_down"),
    }
    halves_ffn, sc_ffn = reduce_start(GROUP_FFN, grads_ffn, "ffn")

    dycb, dypb, dzg, dconv, dmwb, dm, db_gate, dln_g, dln_b, db_dw, dps = _mixers_bwd_rows(
        dh1b, yc, yp, z, small["b_gate"], cpre, small["ln_g"], small["ln_b"], mw, small["pool_scale"],
        w_o_b, w_co_b, w_po_b, w_pool_b, sc_ffn[3], nb)
    grads_mix = {
        "w_conv_out": _wgrad(ca, dycb, D, D, tk, "wgrad_conv_out"),
        "w_pool": _wgrad(m, dmwb, GD, GD, tp, "wgrad_pool", diag=True),
        "w_pool_out": _wgrad(m2b, dypb, D, D, tk, "wgrad_pool_out"),
        "w_o": _wgrad(mg, dh1b, D, D, tk, "wgrad_o"),
    }
    halves_mix, sc_mix = reduce_start(GROUP_MIX, grads_mix, "mix")
    dzb, grad_x, dhead, dw_dw, dg_mix = _mixers_bwd_halo(
        dconv, dm, z, dzg, small["w_dw"], head, xs, small["g_mix"], dh1, gw["w_in"], sc_mix[3], nb, t_total)
    packed = jnp.concatenate(
        [dg_mix, db_gate.reshape(2, D), db_dw, dln_g, dln_b, dps, dg_ffn, dg_final,
         jnp.broadcast_to(loss, (1, D)), jnp.zeros((6, D), F32), dhead[PAD:], dw_dw], axis=0)
    sm = _small_start(packed, "small_start")
    grads_in = {"w_in": _wgrad(u, dzb, D, D_IN // N_SHARD, tk, "wgrad_in", col_major=True, dep=sm[3])}
    halves_in, sc_in = reduce_start(("w_in",), grads_in, "in")

    land = _small_wait(sm[0], sm[1], sm[2], [sc_in[3]], "small_wait")
    summed = _sum_slots(land, packed, jnp.reshape(4 * cx + 2 * cy + cc, (1,)).astype(jnp.int32))
    loss = summed[9, 0]

    first = GROUP_FFN + GROUP_MIX
    reduced_half = reduce_finish(GROUP_FFN, halves_ffn, sc_ffn, [summed], "ffn")
    reduced_half += reduce_finish(GROUP_MIX, halves_mix, sc_mix, [summed], "mix")
    reduced = dict(zip(first, _join_halves(reduced_half, "join_halves_first")))
    updates = {}
    for ks in _by_shape([reduced[n] for n in first]):
        names = [first[k] for k in ks]
        done = _adamw([(reduced[n], _shard2d(n, w[n]), _shard2d(n, mom[n]), _shard2d(n, var[n])) for n in names],
                      "adamw_" + names[0])
        updates.update(zip(names, done))

    def repl_stack(d):
        return jnp.concatenate([d["g_mix"], d["b_gate"].reshape(2, D), d["b_dw"], d["ln_g"], d["ln_b"],
                                d["pool_scale"], d["g_ffn"], d["g_final"].reshape(1, D), jnp.ones((7, D), F32)], axis=0)

    def shard_stack(d):
        return jnp.concatenate([d["meta_tokens"], d["w_dw"][0], jnp.ones((1, GD), F32)], axis=0)

    g_repl = summed[0:16]
    g_shard = lax.dynamic_slice_in_dim(summed[16:64], chip * GD, GD, axis=1)
    g_repl, d_repl, m_repl, v_repl = _adamw([(g_repl, repl_stack(w), repl_stack(mom), repl_stack(var))], "adamw_repl")[0]
    g_shard, d_shard, m_shard, v_shard = _adamw(
        [(g_shard, shard_stack(w), shard_stack(mom), shard_stack(var))], "adamw_cols")[0]

    done_first = [updates[n][1] for n in first] + [d_repl, d_shard]
    last_half = reduce_finish(("w_in",), halves_in, sc_in, done_first, "in")
    reduced["w_in"] = _join_halves(last_half, "join_halves_in")[0]
    updates["w_in"] = _adamw([(reduced["w_in"], w["w_in"][0], mom["w_in"][0], var["w_in"][0])], "adamw_w_in")[0]

    def unpack(name, repl, shard):
        if name == "meta_tokens":
            return shard[0:N_META]
        if name == "w_dw":
            return shard[N_META:N_META + KW].reshape(1, KW, GD)
        row = {"g_mix": 0, "b_gate": 1, "b_dw": 3, "ln_g": 4, "ln_b": 5, "pool_scale": 6, "g_ffn": 7, "g_final": 8}[name]
        if name == "b_gate":
            return repl[1:3].reshape(1, 2 * D)
        if name == "g_final":
            return repl[8]
        return repl[row:row + 1]

    out_g, out_d, out_m, out_v = {}, {}, {}, {}
    for n in WEIGHT_ORDER:
        if n in BIG:
            g, d_, m_, v_ = updates[n]
            shape = w[n].shape
            out_g[n], out_d[n], out_m[n], out_v[n] = (_unshard2d(n, a, shape) for a in (g, d_, m_, v_))
        else:
            out_g[n] = unpack(n, g_repl, g_shard)
            out_d[n] = unpack(n, d_repl, d_shard)
            out_m[n] = unpack(n, m_repl, m_shard)
            out_v[n] = unpack(n, v_repl, v_shard)
    return (loss, grad_x[None], *[out_g[n] for n in WEIGHT_ORDER], *[out_d[n] for n in WEIGHT_ORDER],
            *[out_m[n] for n in WEIGHT_ORDER], *[out_v[n] for n in WEIGHT_ORDER])
```

```python
import jax
import jax.numpy as jnp
from jax import lax
from jax.experimental import pallas as pl
from jax.experimental.pallas import tpu as pltpu

F32 = jnp.float32
BF16 = jnp.bfloat16
MESH = pl.DeviceIdType.MESH

D = 1024
N_META = 16
KW = 31
POOL_WINDOWS = (2, 4, 8, 16)
GD = 256
D_IN = 5 * D
D_FF = 2816
N_SHARD = 4
BR = 256
HALO = 16
PAD = BR - N_META
EXT = BR + 2 * HALO
RMS_EPS = 1e-6
LN_EPS = 1e-5
LR, B1, B2, ADAM_EPS, WD, STEP = 0.001, 0.9, 0.999, 1e-08, 0.01, 10
VMEM_LIMIT = 56 * 1024 * 1024


def _cparams(sem, vmem=VMEM_LIMIT):
    return pltpu.CompilerParams(dimension_semantics=sem, vmem_limit_bytes=vmem)


def _dot(a, b):
    return jnp.dot(a, b, preferred_element_type=F32)


def _dot_nt(a, b):
    return lax.dot_general(a, b, (((1,), (1,)), ((), ())), preferred_element_type=F32)


def _dot_tn(a, b):
    return lax.dot_general(a, b, (((0,), (0,)), ((), ())), preferred_element_type=F32)


def _sigmoid(x):
    return 0.5 * jnp.tanh(0.5 * x) + 0.5


def _row_ids(i, n, offset=0):
    return lax.broadcasted_iota(jnp.int32, (n, 1), 0) + (i * BR + offset - PAD)


def _pool_cnt(t, w, t_total):
    left = w // 2
    right = w - 1 - left
    lo = jnp.clip(t - left, 0, t_total)
    hi = jnp.clip(t + right + 1, 0, t_total)
    return jnp.maximum(hi - lo, 1).astype(F32)


def _halo_specs(nb, halo_width=D, width=D):
    last = nb * (BR // HALO) - 1
    return [
        pl.BlockSpec((HALO, halo_width), lambda i: (jnp.maximum(i * (BR // HALO) - 1, 0), 0)),
        pl.BlockSpec((BR, width), lambda i: (i, 0)),
        pl.BlockSpec((HALO, halo_width), lambda i: (jnp.minimum((i + 1) * (BR // HALO), last), 0)),
    ]


def _cols(ref, n):
    return [ref.at[:, k * D:(k + 1) * D] for k in range(n)]


def _fill_ext(ext_ref, prev, cur, nxt, i, nb):
    ext_ref[0:HALO, :] = jnp.where(i > 0, prev, 0.0)
    ext_ref[HALO:HALO + BR, :] = cur
    ext_ref[HALO + BR:EXT, :] = jnp.where(i < nb - 1, nxt, 0.0)


ROT_ROWS = EXT - 8


def _fill_rot(rot_ref, ext_ref, lanes):
    for r in range(1, 8):
        rot_ref[r] = ext_ref[pl.ds(r, ROT_ROWS), lanes]


def _tap(rot_ref, ext_ref, lanes, offset):
    q, r = divmod(offset, 8)
    if r == 0:
        return ext_ref[pl.ds(8 * q, BR), lanes]
    return rot_ref[r, pl.ds(8 * q, BR), :]


def _row_spec(width=D):
    return pl.BlockSpec((BR, width), lambda i: (i, 0))


def _x_spec():
    return pl.BlockSpec((BR, D), lambda i: (jnp.maximum(i - 1, 0), 0))


def _const_spec(shape):
    nd = len(shape)
    return pl.BlockSpec(shape, lambda i: (0,) * nd)


def _rms_u(head, x, g_mix, nb):
    def body(head_ref, x_ref, g_ref, u_ref):
        i = pl.program_id(0)
        h = jnp.where(i == 0, head_ref[...], x_ref[...])
        r = lax.rsqrt(jnp.mean(h * h, axis=-1, keepdims=True) + RMS_EPS)
        u_ref[...] = ((h * r) * g_ref[...]).astype(BF16)

    return pl.pallas_call(
        body, name="rms_u",
        grid=(nb,),
        in_specs=[_const_spec((BR, D)), _x_spec(), _const_spec((1, D))],
        out_specs=_row_spec(),
        out_shape=jax.ShapeDtypeStruct((nb * BR, D), BF16),
        compiler_params=_cparams(("arbitrary",)),
    )(head, x, g_mix)


def _in_proj_rows(tp):
    return tp // 4 if (tp // 4) % 16 == 0 else BR


def _in_proj_own(u, w_own, idx, nb):
    tp = nb * BR
    wcols = w_own.shape[1]
    rows = _in_proj_rows(tp)

    def body(idx_ref, u_ref, w_ref, z_ref, wb_ref):
        @pl.when(pl.program_id(0) == 0)
        def _():
            wb_ref[...] = w_ref[...].astype(BF16)

        z_ref[...] = _dot(u_ref[...], wb_ref[...])

    return pl.pallas_call(
        body, name="in_proj_own",
        grid_spec=pltpu.PrefetchScalarGridSpec(
            num_scalar_prefetch=1, grid=(tp // rows,),
            in_specs=[pl.BlockSpec((rows, D), lambda i, idx_ref: (i, 0)),
                      pl.BlockSpec((D, wcols), lambda i, idx_ref: (0, 0))],
            out_specs=pl.BlockSpec((rows, wcols), lambda i, idx_ref: (i, idx_ref[0])),
            scratch_shapes=[pltpu.VMEM((D, wcols), BF16)]),
        out_shape=jax.ShapeDtypeStruct((tp, N_SHARD * wcols), F32),
        compiler_params=_cparams(("arbitrary",)),
    )(idx, u, w_own)


def _in_proj_rest(u, w_in_b, z, idx, dep, nb):
    tp = nb * BR
    wcols = w_in_b.shape[2]
    rows = _in_proj_rows(tp)

    def body(idx_ref, u_ref, w_ref, z_in, dep_ref, z_ref):
        z_ref[...] = _dot(u_ref[...], w_ref[...])

    any_spec = pl.BlockSpec(memory_space=pl.ANY)
    return pl.pallas_call(
        body, name="in_proj_rest",
        grid_spec=pltpu.PrefetchScalarGridSpec(
            num_scalar_prefetch=1, grid=(N_SHARD - 1, tp // rows),
            in_specs=[pl.BlockSpec((rows, D), lambda s, i, idx_ref: (i, 0)),
                      pl.BlockSpec((None, D, wcols), lambda s, i, idx_ref: (idx_ref[1 + s], 0, 0)),
                      any_spec, any_spec],
            out_specs=pl.BlockSpec((rows, wcols), lambda s, i, idx_ref: (i, idx_ref[1 + s]))),
        out_shape=jax.ShapeDtypeStruct(z.shape, F32),
        input_output_aliases={3: 0},
        compiler_params=_cparams(("arbitrary", "arbitrary")),
    )(idx, u, w_in_b, z, dep)


def _mixers_fwd(z, head, x, b_gate, w_dw, b_dw, ln_g, ln_b, pool_scale, w_co, w_pool, w_po, w_o, nb, t_total):
    tp = nb * BR

    def body(z_prev, z_cur, z_next, head_ref, x_ref, bg_ref, wdw_ref, bdw_ref,
             lng_ref, lnb_ref, ps_ref, wco_ref, wpool_ref, wpo_ref, wo_ref,
             h1_ref, yc_ref, yp_ref, mg_ref, ca_ref, cpre_ref, m_ref, mw_ref, m2b_ref, ext_ref, pext_ref, rot_ref):
        i = pl.program_id(0)
        avp, agp, pp = _cols(z_prev, 3)
        av, ag, pc, za, zb = _cols(z_cur, 5)
        avn, agn, pn = _cols(z_next, 3)
        _fill_ext(ext_ref, avp[...] * _sigmoid(agp[...]), av[...] * _sigmoid(ag[...]),
                  avn[...] * _sigmoid(agn[...]), i, nb)
        _fill_ext(pext_ref, pp[...], pc[...], pn[...], i, nb)

        def conv_chunk(c, carry):
            lanes = pl.ds(pl.multiple_of(c * 128, 128), 128)
            _fill_rot(rot_ref, ext_ref, lanes)
            acc = jnp.broadcast_to(bdw_ref[:, lanes], (BR, 128))
            for k in range(KW):
                acc = acc + wdw_ref[k:k + 1, lanes] * _tap(rot_ref, ext_ref, lanes, 1 + k)
            cpre_ref[:, lanes] = acc
            return carry
        lax.fori_loop(0, D // 128, conv_chunk, 0)

        conv = cpre_ref[...]
        mu = jnp.mean(conv, axis=-1, keepdims=True)
        xc = conv - mu
        rstd = lax.rsqrt(jnp.mean(xc * xc, axis=-1, keepdims=True) + LN_EPS)
        ln = (xc * rstd) * lng_ref[...] + lnb_ref[...]
        cact = (ln * _sigmoid(ln)).astype(BF16)
        ca_ref[...] = cact
        y_conv = _dot(cact, wco_ref[...])
        yc_ref[...] = y_conv

        t = _row_ids(i, BR)
        for gi, w in enumerate(POOL_WINDOWS):
            left = w // 2
            right = w - 1 - left
            lanes = slice(gi * GD, (gi + 1) * GD)
            s = pext_ref[pl.ds(HALO - left, BR), lanes]
            for j in range(-left + 1, right + 1):
                s = s + pext_ref[pl.ds(HALO + j, BR), lanes]
            m = (s / _pool_cnt(t, w, t_total) - pext_ref[HALO:HALO + BR, lanes]).astype(BF16)
            m_ref[:, lanes] = m
            mw_ref[:, lanes] = _dot(m, wpool_ref[gi])
        mw = mw_ref[...]
        m2b = (mw * ps_ref[...]).astype(BF16)
        m2b_ref[...] = m2b
        y_pool = _dot(m2b, wpo_ref[...])
        yp_ref[...] = y_pool

        s_a = _sigmoid(za[...] + bg_ref[:, 0:D])
        s_b = _sigmoid(zb[...] + bg_ref[:, D:2 * D])
        merged = (s_a * y_conv + s_b * y_pool).astype(BF16)
        mg_ref[...] = merged
        h0 = jnp.where(i == 0, head_ref[...], x_ref[...])
        h1_ref[...] = h0 + _dot(merged, wo_ref[...])

    in_specs = (_halo_specs(nb, 3 * D, 5 * D)
                + [_const_spec((BR, D)), _x_spec(), _const_spec((1, 2 * D)), _const_spec((32, D)),
                   _const_spec((1, D)), _const_spec((1, D)), _const_spec((1, D)), _const_spec((1, D)),
                   _const_spec((D, D)), _const_spec((4, GD, GD)), _const_spec((D, D)), _const_spec((D, D))])
    outs = [(F32, "h1"), (F32, "yc"), (F32, "yp"), (BF16, "mg"), (BF16, "ca"), (F32, "cpre"), (BF16, "m"), (F32, "mw"),
            (BF16, "m2b")]
    return pl.pallas_call(
        body, name="mixers_fwd",
        grid=(nb,),
        in_specs=in_specs,
        out_specs=[_row_spec() for _ in outs],
        out_shape=[jax.ShapeDtypeStruct((tp, D), dt) for dt, _ in outs],
        scratch_shapes=[pltpu.VMEM((EXT, D), F32), pltpu.VMEM((EXT, D), F32), pltpu.VMEM((8, ROT_ROWS, 128), F32)],
        compiler_params=_cparams(("arbitrary",)),
    )(z, z, z, head, x, b_gate, w_dw, b_dw, ln_g, ln_b, pool_scale, w_co, w_pool, w_po, w_o)


def _ffn_fwd_bwd(h1, target, g_ffn, g_final, w_g, w_u, w_d, nb):
    tp = nb * BR

    def body(h1_ref, tgt_ref, gf_ref, gfin_ref, wg_hbm, wu_hbm, wd_hbm,
             dh1_ref, dh1b_ref, vb_ref, fb_ref, dgb_ref, dub_ref, dh2b_ref, loss_ref, dgf_ref, dgfin_ref,
             wg_ref, wu_ref, wd_ref, sem):
        i = pl.program_id(0)

        @pl.when(i == 0)
        def _():
            copies = [pltpu.make_async_copy(wg_hbm, wg_ref, sem.at[0]),
                      pltpu.make_async_copy(wu_hbm, wu_ref, sem.at[1]),
                      pltpu.make_async_copy(wd_hbm, wd_ref, sem.at[2])]
            for cp in copies:
                cp.start()
            loss_ref[...] = jnp.zeros_like(loss_ref)
            dgf_ref[...] = jnp.zeros_like(dgf_ref)
            dgfin_ref[...] = jnp.zeros_like(dgfin_ref)
            for cp in copies:
                cp.wait()

        h1 = h1_ref[...]
        r1 = lax.rsqrt(jnp.mean(h1 * h1, axis=-1, keepdims=True) + RMS_EPS)
        vn = h1 * r1
        vb = (vn * gf_ref[...]).astype(BF16)
        vb_ref[...] = vb
        g = _dot_nt(vb, wg_ref[...])
        up = _dot_nt(vb, wu_ref[...])
        sg = _sigmoid(g)
        sl = g * sg
        fb = (sl * up).astype(BF16)
        fb_ref[...] = fb
        h2 = h1 + _dot(fb, wd_ref[...])
        r2 = lax.rsqrt(jnp.mean(h2 * h2, axis=-1, keepdims=True) + RMS_EPS)
        yn = h2 * r2
        valid = i > 0
        diff = jnp.where(valid, yn * gfin_ref[...] - tgt_ref[...], 0.0)
        loss_ref[...] += 0.5 * jnp.sum(jnp.mean(diff * diff, axis=-1, keepdims=True))
        dy = diff * (1.0 / D)
        dgfin_ref[...] += jnp.sum(dy * yn, axis=0, keepdims=True)
        gd = dy * gfin_ref[...]
        dh2 = r2 * (gd - yn * jnp.mean(yn * gd, axis=-1, keepdims=True))
        dh2b = dh2.astype(BF16)
        dh2b_ref[...] = dh2b
        df = _dot_nt(dh2b, wd_ref[...])
        dub = (df * sl).astype(BF16)
        dgb = (df * up * (sg * (1.0 + g * (1.0 - sg)))).astype(BF16)
        dub_ref[...] = dub
        dgb_ref[...] = dgb
        dv = _dot(dgb, wg_ref[...]) + _dot(dub, wu_ref[...])
        dgf_ref[...] += jnp.sum(dv * vn, axis=0, keepdims=True)
        gd1 = dv * gf_ref[...]
        dh1 = dh2 + r1 * (gd1 - vn * jnp.mean(vn * gd1, axis=-1, keepdims=True))
        dh1_ref[...] = dh1
        dh1b_ref[...] = dh1.astype(BF16)

    any_spec = pl.BlockSpec(memory_space=pl.ANY)
    return pl.pallas_call(
        body, name="ffn_fwd_bwd",
        grid=(nb,),
        in_specs=[_row_spec(), _x_spec(), _const_spec((1, D)), _const_spec((1, D)), any_spec, any_spec, any_spec],
        out_specs=[_row_spec(), _row_spec(), _row_spec(), _row_spec(D_FF), _row_spec(D_FF), _row_spec(D_FF), _row_spec(),
                   _const_spec((1, 1)), _const_spec((1, D)), _const_spec((1, D))],
        out_shape=[jax.ShapeDtypeStruct((tp, D), F32), jax.ShapeDtypeStruct((tp, D), BF16),
                   jax.ShapeDtypeStruct((tp, D), BF16), jax.ShapeDtypeStruct((tp, D_FF), BF16),
                   jax.ShapeDtypeStruct((tp, D_FF), BF16), jax.ShapeDtypeStruct((tp, D_FF), BF16),
                   jax.ShapeDtypeStruct((tp, D), BF16), jax.ShapeDtypeStruct((1, 1), F32),
                   jax.ShapeDtypeStruct((1, D), F32), jax.ShapeDtypeStruct((1, D), F32)],
        scratch_shapes=[pltpu.VMEM((D_FF, D), BF16), pltpu.VMEM((D_FF, D), BF16), pltpu.VMEM((D_FF, D), BF16),
                        pltpu.SemaphoreType.DMA((3,))],
        compiler_params=_cparams(("arbitrary",)),
    )(h1, target, g_ffn, g_final, w_g, w_u, w_d)


def _mixers_bwd_rows(dh1b, yc, yp, z, b_gate, cpre, ln_g, ln_b, mw, pool_scale, w_o, w_co, w_po, w_pool, dep, nb):
    tp = nb * BR

    def body(dh1b_ref, yc_ref, yp_ref, za, zb, bg_ref, cpre_ref, lng_ref, lnb_ref, mw_ref, ps_ref,
             wo_ref, wco_ref, wpo_ref, wpool_ref, dep_ref,
             dycb_ref, dypb_ref, dzg_ref, dconv_ref, dmwb_ref, dm_ref, dbg_ref, dlng_ref, dlnb_ref, dbdw_ref, dps_ref):
        i = pl.program_id(0)

        @pl.when(i == 0)
        def _():
            for r in (dbg_ref, dlng_ref, dlnb_ref, dbdw_ref, dps_ref):
                r[...] = jnp.zeros_like(r)

        dmg = _dot_nt(dh1b_ref[...], wo_ref[...])
        s_a = _sigmoid(za[...] + bg_ref[:, 0:D])
        s_b = _sigmoid(zb[...] + bg_ref[:, D:2 * D])
        dycb = (dmg * s_a).astype(BF16)
        dypb = (dmg * s_b).astype(BF16)
        dycb_ref[...] = dycb
        dypb_ref[...] = dypb
        dza = dmg * yc_ref[...] * (s_a * (1.0 - s_a))
        dzb = dmg * yp_ref[...] * (s_b * (1.0 - s_b))
        dzg_ref[:, 0:D] = dza.astype(BF16)
        dzg_ref[:, D:2 * D] = dzb.astype(BF16)
        dbg_ref[:, 0:D] += jnp.sum(dza, axis=0, keepdims=True)
        dbg_ref[:, D:2 * D] += jnp.sum(dzb, axis=0, keepdims=True)

        dca = _dot_nt(dycb, wco_ref[...])
        conv = cpre_ref[...]
        mu = jnp.mean(conv, axis=-1, keepdims=True)
        xc = conv - mu
        rstd = lax.rsqrt(jnp.mean(xc * xc, axis=-1, keepdims=True) + LN_EPS)
        xhat = xc * rstd
        ln = xhat * lng_ref[...] + lnb_ref[...]
        sg = _sigmoid(ln)
        dln = dca * (sg * (1.0 + ln * (1.0 - sg)))
        dlng_ref[...] += jnp.sum(dln * xhat, axis=0, keepdims=True)
        dlnb_ref[...] += jnp.sum(dln, axis=0, keepdims=True)
        dxh = dln * lng_ref[...]
        dconv = rstd * (dxh - jnp.mean(dxh, axis=-1, keepdims=True)
                        - xhat * jnp.mean(dxh * xhat, axis=-1, keepdims=True))
        dconv_ref[...] = dconv
        dbdw_ref[...] += jnp.sum(dconv, axis=0, keepdims=True)

        dm2 = _dot_nt(dypb, wpo_ref[...])
        dps_ref[...] += jnp.sum(dm2 * mw_ref[...], axis=0, keepdims=True)
        dmwb = (dm2 * ps_ref[...]).astype(BF16)
        dmwb_ref[...] = dmwb
        for gi in range(len(POOL_WINDOWS)):
            lanes = slice(gi * GD, (gi + 1) * GD)
            dm_ref[:, lanes] = _dot_nt(dmwb[:, lanes], wpool_ref[gi])

    in_specs = [_row_spec(), _row_spec(), _row_spec(),
                pl.BlockSpec((BR, D), lambda i: (i, 3)), pl.BlockSpec((BR, D), lambda i: (i, 4)),
                _const_spec((1, 2 * D)), _row_spec(), _const_spec((1, D)), _const_spec((1, D)), _row_spec(),
                _const_spec((1, D)), _const_spec((D, D)), _const_spec((D, D)), _const_spec((D, D)),
                _const_spec((4, GD, GD)), pl.BlockSpec(memory_space=pl.ANY)]
    return pl.pallas_call(
        body, name="mixers_bwd_rows",
        grid=(nb,),
        in_specs=in_specs,
        out_specs=[_row_spec(), _row_spec(), _row_spec(2 * D), _row_spec(), _row_spec(), _row_spec(),
                   _const_spec((1, 2 * D)), _const_spec((1, D)), _const_spec((1, D)), _const_spec((1, D)),
                   _const_spec((1, D))],
        out_shape=[jax.ShapeDtypeStruct((tp, D), BF16), jax.ShapeDtypeStruct((tp, D), BF16),
                   jax.ShapeDtypeStruct((tp, 2 * D), BF16), jax.ShapeDtypeStruct((tp, D), F32),
                   jax.ShapeDtypeStruct((tp, D), BF16), jax.ShapeDtypeStruct((tp, D), F32),
                   jax.ShapeDtypeStruct((1, 2 * D), F32), jax.ShapeDtypeStruct((1, D), F32),
                   jax.ShapeDtypeStruct((1, D), F32), jax.ShapeDtypeStruct((1, D), F32),
                   jax.ShapeDtypeStruct((1, D), F32)],
        compiler_params=_cparams(("arbitrary",)),
    )(dh1b, yc, yp, z, z, b_gate, cpre, ln_g, ln_b, mw, pool_scale, w_o, w_co, w_po, w_pool, dep)


def _mixers_bwd_halo(dconv, dm, z, dzg, w_dw, head, x, g_mix, dh1, w_in_b, dep, nb, t_total):
    tp = nb * BR
    ns = w_in_b.shape[0]
    wcols = w_in_b.shape[2]
    seq = x.shape[0]

    def body(dcp, dcc, dcn, dmp, dmc, dmn, z_prev, z_cur, z_next, dzg_ref, wdw_ref, head_ref, x_ref, g_ref,
             dh1_ref, w_hbm, dep_ref,
             dzb_ref, gx_ref, dhead_ref, dwdw_ref, dgmix_ref,
             w_ref, sem, aext_ref, dext_ref, qext_ref, da_ref, rot_ref, dwp_ref):
        i = pl.program_id(0)
        (avp, agp), (av, ag), (avn, agn) = _cols(z_prev, 2), _cols(z_cur, 2), _cols(z_next, 2)

        @pl.when(i == 0)
        def _():
            cp = pltpu.make_async_copy(w_hbm, w_ref, sem.at[0])
            cp.start()
            dwp_ref[...] = jnp.zeros_like(dwp_ref)
            dgmix_ref[...] = jnp.zeros_like(dgmix_ref)
            cp.wait()

        sig_g = _sigmoid(ag[...])
        _fill_ext(aext_ref, avp[...] * _sigmoid(agp[...]), av[...] * sig_g, avn[...] * _sigmoid(agn[...]), i, nb)
        _fill_ext(dext_ref, dcp[...], dcc[...], dcn[...], i, nb)
        _fill_ext(qext_ref, dmp[...], dmc[...], dmn[...], i, nb)

        def conv_chunk(c, carry):
            lanes = pl.ds(pl.multiple_of(c * 128, 128), 128)
            _fill_rot(rot_ref, dext_ref, lanes)
            acc = jnp.zeros((BR, 128), F32)
            for k in range(KW):
                acc = acc + wdw_ref[k:k + 1, lanes] * _tap(rot_ref, dext_ref, lanes, KW - k)
            da_ref[:, lanes] = acc
            _fill_rot(rot_ref, aext_ref, lanes)
            dcv = dext_ref[HALO:HALO + BR, lanes]
            for k in range(KW):
                prod = _tap(rot_ref, aext_ref, lanes, 1 + k) * dcv
                dwp_ref[k, :, lanes] += jnp.sum(prod.reshape(BR // 8, 8, 128), axis=0)
            return carry
        lax.fori_loop(0, D // 128, conv_chunk, 0)

        @pl.when(i == nb - 1)
        def _():
            dwdw_ref[...] = jnp.sum(dwp_ref[...], axis=1)

        da = da_ref[...]
        a_val = av[...]
        dzb_ref[:, 0:D] = (da * sig_g).astype(BF16)
        dzb_ref[:, D:2 * D] = (da * a_val * (sig_g * (1.0 - sig_g))).astype(BF16)

        t_ext = _row_ids(i, EXT, -HALO)
        for gi, w in enumerate(POOL_WINDOWS):
            left = w // 2
            right = w - 1 - left
            lanes = slice(gi * GD, (gi + 1) * GD)
            qext_ref[:, lanes] = qext_ref[:, lanes] / _pool_cnt(t_ext, w, t_total)
            s = qext_ref[pl.ds(HALO - right, BR), lanes]
            for j in range(-right + 1, left + 1):
                s = s + qext_ref[pl.ds(HALO + j, BR), lanes]
            dzb_ref[:, 2 * D + gi * GD:2 * D + (gi + 1) * GD] = (s - dmc[:, lanes]).astype(BF16)
        dzb_ref[:, 3 * D:5 * D] = dzg_ref[...]

        du = _dot_nt(dzb_ref[:, 0:wcols], w_ref[0])
        for s_i in range(1, ns):
            du = du + _dot_nt(dzb_ref[:, s_i * wcols:(s_i + 1) * wcols], w_ref[s_i])
        h0 = jnp.where(i == 0, head_ref[...], x_ref[...])
        r0 = lax.rsqrt(jnp.mean(h0 * h0, axis=-1, keepdims=True) + RMS_EPS)
        un = h0 * r0
        dgmix_ref[...] += jnp.sum(du * un, axis=0, keepdims=True)
        gd = du * g_ref[...]
        dh0 = dh1_ref[...] + r0 * (gd - un * jnp.mean(un * gd, axis=-1, keepdims=True))
        gx_ref[...] = dh0

        @pl.when(i == 0)
        def _():
            dhead_ref[...] = dh0

    any_spec = pl.BlockSpec(memory_space=pl.ANY)
    in_specs = (_halo_specs(nb) + _halo_specs(nb) + _halo_specs(nb, 2 * D, 2 * D)
                + [_row_spec(2 * D), _const_spec((32, D)), _const_spec((BR, D)), _x_spec(), _const_spec((1, D)),
                   _row_spec(), any_spec, any_spec])
    return pl.pallas_call(
        body, name="mixers_bwd_halo",
        grid=(nb,),
        in_specs=in_specs,
        out_specs=[_row_spec(D_IN), _x_spec(), _const_spec((BR, D)), _const_spec((32, D)), _const_spec((1, D))],
        out_shape=[jax.ShapeDtypeStruct((tp, D_IN), BF16), jax.ShapeDtypeStruct((seq, D), F32),
                   jax.ShapeDtypeStruct((BR, D), F32), jax.ShapeDtypeStruct((32, D), F32),
                   jax.ShapeDtypeStruct((1, D), F32)],
        scratch_shapes=[pltpu.VMEM((ns, D, wcols), BF16), pltpu.SemaphoreType.DMA((1,)),
                        pltpu.VMEM((EXT, D), F32), pltpu.VMEM((EXT, D), F32), pltpu.VMEM((EXT, D), F32),
                        pltpu.VMEM((BR, D), F32), pltpu.VMEM((8, ROT_ROWS, 128), F32), pltpu.VMEM((32, 8, D), F32)],
        compiler_params=_cparams(("arbitrary",)),
    )(dconv, dconv, dconv, dm, dm, dm, z, z, z, dzg, w_dw, head, x, g_mix, dh1, w_in_b, dep)


def _wgrad(a, c, tm, tn, tk, name, diag=False, col_major=False, dep=None):
    tp, m = a.shape
    n = c.shape[1]
    nk = tp // tk
    gm, gn = m // tm, n // tn

    def body(a_ref, c_ref, *rest):
        o_ref, ob_ref = rest[-2:]
        k = pl.program_id(2)

        @pl.when(k == 0)
        def _():
            o_ref[...] = jnp.zeros_like(o_ref)

        o_ref[...] += _dot_tn(a_ref[...], c_ref[...])

        @pl.when(k == nk - 1)
        def _():
            ob_ref[...] = o_ref[...].astype(BF16)

    c_map = lambda i, j, k: (k, j)
    grid = (gm, gn, nk)
    deps = [] if dep is None else [dep]
    if diag:
        grid = (gm, 1, nk)
        c_map = lambda i, j, k: (k, i)
        o_spec = pl.BlockSpec((tm, tn), lambda i, j, k: (i, 0))
        o_shape = (m, tn)
    elif col_major:
        o_spec = pl.BlockSpec((None, tm, tn), lambda i, j, k: (j, i, 0))
        o_shape = (gn, m, tn)
    else:
        o_spec = pl.BlockSpec((tm, tn), lambda i, j, k: (i, j))
        o_shape = (m, n)
    return pl.pallas_call(
        body, name=name,
        grid=grid,
        in_specs=[pl.BlockSpec((tk, tm), lambda i, j, k: (k, i)), pl.BlockSpec((tk, tn), c_map)]
        + [pl.BlockSpec(memory_space=pl.ANY)] * len(deps),
        out_specs=[o_spec, o_spec],
        out_shape=[jax.ShapeDtypeStruct(o_shape, F32), jax.ShapeDtypeStruct(o_shape, BF16)],
        compiler_params=_cparams(("arbitrary", "arbitrary", "arbitrary")),
    )(a, c, *deps)


def _place():
    x, y, c = lax.axis_index("x"), lax.axis_index("y"), lax.axis_index("c")
    others = [(1 - x, y), (x, 1 - y), (1 - x, 1 - y)]
    return x, y, c, others


def _split2(a, axis=0):
    return a.reshape(a.shape[:axis] + (2, a.shape[axis] // 2) + a.shape[axis + 1:])


def _merge2(a, axis=0):
    return a.reshape(a.shape[:axis] + (2 * a.shape[axis + 1],) + a.shape[axis + 2:])


def _cast_into_slot(shards, chip, dep, name):
    n = len(shards)
    r, c = shards[0].shape
    r2 = r // 2

    def body(chip_ref, *refs):
        for a in range(n):
            refs[n + 1 + a][...] = refs[a][...].astype(BF16)

    out = pl.pallas_call(
        body, name=name,
        grid_spec=pltpu.PrefetchScalarGridSpec(
            num_scalar_prefetch=1, grid=(2,),
            in_specs=[pl.BlockSpec((r2, c), lambda h, chip_ref: (h, 0))] * n + [pl.BlockSpec(memory_space=pl.ANY)],
            out_specs=[pl.BlockSpec((None, None, r2, c), lambda h, chip_ref: (chip_ref[0], h, 0, 0))] * n),
        out_shape=[jax.ShapeDtypeStruct((N_SHARD, 2, r2, c), BF16)] * n,
        compiler_params=_cparams(("arbitrary",)),
    )(chip, *shards, dep)
    return list(out)


GATHER_ID, SMALL_ID, TINY_ID = 1, 2, 6
SCATTER_IDS = {"ffn": 3, "mix": 4, "in": 5}
FLIPS = [(dx, dy, dc) for dx in (0, 1) for dy in (0, 1) for dc in (0, 1)][1:]
HBM_SPEC = pl.BlockSpec(memory_space=pltpu.HBM)
SEM_SPEC = pl.BlockSpec(memory_space=pltpu.SEMAPHORE)
DATAFLOW = pltpu.SideEffectType.DATAFLOW_SIDE_EFFECTING
TOKEN = jax.ShapeDtypeStruct((8, 128), F32)


def _in_hbm(a):
    return pltpu.with_memory_space_constraint(a, pltpu.HBM)


def _tiny_views(ins):
    return [lambda frm, to, c: ins[0]], [lambda frm, to, c: ins[1].at[frm]]


def _tiny_start(v, name):
    land = lax.empty((N_SHARD,) + v.shape, v.dtype)
    return _split_start(lambda ins: _tiny_views(ins)[0], lambda ins: _tiny_views(ins)[1], [v, land], 1, name, TINY_ID)


def _tiny_wait(send_sems, recv_sems, arrays, after, name):
    return _split_wait(lambda ins: _tiny_views(ins)[0], lambda ins: _tiny_views(ins)[1], send_sems, recv_sems,
                       arrays, after, name)[1]


def _ici_copies(srcs, dsts, send_sems, recv_sems, started):
    x, y, c, others = _place()
    mine = 2 * x + y
    copies = []
    for a in range(len(srcs)):
        for j, chip in enumerate(others):
            there = 2 * chip[0] + chip[1]
            src, dst = srcs[a](mine, there, c), dsts[a](mine, there, c)
            if not started:
                dst = dsts[a](there, mine, c)
            copies.append(pltpu.make_async_remote_copy(
                src_ref=src, dst_ref=dst, send_sem=send_sems.at[a * 3 + j], recv_sem=recv_sems.at[a * 3 + j],
                device_id=(*chip, c), device_id_type=MESH))
    return copies


def _same_core_peers():
    x, y, c, others = _place()
    return [(*chip, c) for chip in others]


def _all_peers():
    x, y, c, _ = _place()
    return [(jnp.bitwise_xor(x, dx), jnp.bitwise_xor(y, dy), jnp.bitwise_xor(c, dc)) for dx, dy, dc in FLIPS]


def _split_start(srcs_of, dsts_of, arrays, n_src, name, collective_id, peers_of=_same_core_peers,
                 copies_of=None, n_sems=None, dep=None):
    n = len(arrays)
    n_sems = n_sems or 3 * n_src
    copies_of = copies_of or (lambda ins, ss, rs, started: _ici_copies(srcs_of(ins), dsts_of(ins), ss, rs, started))
    deps = [] if dep is None else [dep]
    nd = len(deps)

    def body(*refs):
        ins = refs[:n]
        send_sems, recv_sems = refs[n + nd], refs[n + nd + 1]
        token = refs[2 * n + nd + 2]
        peers = peers_of()
        barrier = pltpu.get_barrier_semaphore()
        for peer in peers:
            pl.semaphore_signal(barrier, inc=1, device_id=peer, device_id_type=MESH)
        pl.semaphore_wait(barrier, len(peers))
        for cp in copies_of(ins, send_sems, recv_sems, True):
            cp.start()
        token[...] = jnp.zeros_like(token)

    out = pl.pallas_call(
        body, name=name,
        in_specs=[HBM_SPEC] * n + [pl.BlockSpec(memory_space=pl.ANY)] * nd,
        out_specs=(SEM_SPEC, SEM_SPEC, *([HBM_SPEC] * n), pl.BlockSpec(memory_space=pltpu.VMEM)),
        out_shape=(pltpu.SemaphoreType.DMA((n_sems,)), pltpu.SemaphoreType.DMA((n_sems,)),
                   *[pltpu.HBM(a.shape, a.dtype) for a in arrays], TOKEN),
        input_output_aliases={a: 2 + a for a in range(n)},
        compiler_params=pltpu.CompilerParams(has_side_effects=DATAFLOW, collective_id=collective_id),
    )(*[_in_hbm(a) for a in arrays], *deps)
    return out[0], out[1], list(out[2:2 + n]), out[2 + n]


def _split_wait(srcs_of, dsts_of, send_sems, recv_sems, arrays, after, name, copies_of=None):
    n = len(arrays)
    copies_of = copies_of or (lambda ins, ss, rs, started: _ici_copies(srcs_of(ins), dsts_of(ins), ss, rs, started))

    def body(*refs):
        ins = refs[:n]
        send_sems, recv_sems = refs[n], refs[n + 1]
        for cp in copies_of(ins, send_sems, recv_sems, False):
            cp.wait_send()
            cp.wait_recv()

    return pl.pallas_call(
        body, name=name,
        in_specs=[HBM_SPEC] * n + [SEM_SPEC, SEM_SPEC] + [pl.BlockSpec(memory_space=pl.ANY)] * len(after),
        out_specs=[HBM_SPEC] * n,
        out_shape=[pltpu.HBM(a.shape, a.dtype) for a in arrays],
        input_output_aliases={a: a for a in range(n)},
        compiler_params=pltpu.CompilerParams(has_side_effects=DATAFLOW),
    )(*arrays, send_sems, recv_sems, *after)


def _gather_views(ins):
    view = [lambda frm, to, c, r=r: r.at[frm, c] for r in ins]
    return view


def _gather_start(bufs, dep, name):
    return _split_start(_gather_views, _gather_views, bufs, len(bufs), name, GATHER_ID, dep=dep)


def _gather_wait(send_sems, recv_sems, bufs, after, name):
    return _split_wait(_gather_views, _gather_views, send_sems, recv_sems, bufs, after, name)


SIBLING_ONLY = pltpu.CompilerParams(collective_id=0)


def _sibling_handshake(x, y, c):
    barrier = pltpu.get_barrier_semaphore()
    pl.semaphore_signal(barrier, inc=1, device_id=(x, y, 1 - c), device_id_type=MESH)
    pl.semaphore_wait(barrier, 1)


def _forward_halves(bufs, name):
    n = len(bufs)
    any_spec = pl.BlockSpec(memory_space=pl.ANY)

    def body(*refs):
        outs = refs[n:2 * n]
        send_sems, recv_sems = refs[2 * n:]
        x, y, c, others = _place()
        _sibling_handshake(x, y, c)
        copies = []
        for a in range(n):
            for j, chip in enumerate(others):
                landed = outs[a].at[2 * chip[0] + chip[1], c]
                copies.append(pltpu.make_async_remote_copy(
                    src_ref=landed, dst_ref=landed, send_sem=send_sems.at[a * 3 + j], recv_sem=recv_sems.at[a * 3 + j],
                    device_id=(x, y, 1 - c), device_id_type=MESH))
        for cp in copies:
            cp.start()
        for a in range(n):
            for j, chip in enumerate(others):
                landed = outs[a].at[2 * chip[0] + chip[1], 1 - c]
                pltpu.make_async_remote_copy(
                    src_ref=landed, dst_ref=landed, send_sem=send_sems.at[a * 3 + j], recv_sem=recv_sems.at[a * 3 + j],
                    device_id=(x, y, c), device_id_type=MESH).wait_recv()
        for cp in copies:
            cp.wait_send()

    out = pl.pallas_call(
        body, name=name,
        in_specs=[any_spec] * n, out_specs=[any_spec] * n,
        out_shape=[jax.ShapeDtypeStruct(b.shape, b.dtype) for b in bufs],
        input_output_aliases={a: a for a in range(n)},
        scratch_shapes=[pltpu.SemaphoreType.DMA((3 * n,)), pltpu.SemaphoreType.DMA((3 * n,))],
        compiler_params=SIBLING_ONLY,
    )(*bufs)
    return [_merge2(o, 1) for o in out]


def _swap_halves_bf16(gbs, name):
    n = len(gbs)
    any_spec = pl.BlockSpec(memory_space=pl.ANY)

    def body(*refs):
        ins, outs = refs[:n], refs[n:2 * n]
        send_sems, recv_sems = refs[2 * n:]
        x, y, c, _ = _place()
        _sibling_handshake(x, y, c)
        copies = []
        for a in range(n):
            copies.append(pltpu.make_async_remote_copy(
                src_ref=ins[a].at[:, 1 - c], dst_ref=outs[a], send_sem=send_sems.at[a], recv_sem=recv_sems.at[a],
                device_id=(x, y, 1 - c), device_id_type=MESH))
        for cp in copies:
            cp.start()
        for cp in copies:
            cp.wait()

    return pl.pallas_call(
        body, name=name,
        in_specs=[any_spec] * n, out_specs=[any_spec] * n,
        out_shape=[jax.ShapeDtypeStruct((g.shape[0], g.shape[1] // 2, g.shape[2]), g.dtype) for g in gbs],
        scratch_shapes=[pltpu.SemaphoreType.DMA((n,)), pltpu.SemaphoreType.DMA((n,))],
        compiler_params=SIBLING_ONLY,
    )(*[_split2(g, 1) for g in gbs])


def _scatter_srcs(n):
    return lambda ins: [lambda frm, to, c, r=r: r.at[to] for r in ins[:n]]


def _scatter_dsts(n):
    return lambda ins: [lambda frm, to, c, r=r: r.at[frm] for r in ins[n:]]


def _scatter_start(hbs, name, collective_id):
    n = len(hbs)
    lands = [lax.empty(h.shape, h.dtype) for h in hbs]
    return _split_start(_scatter_srcs(n), _scatter_dsts(n), list(hbs) + lands, n, name, collective_id)


def _scatter_wait(send_sems, recv_sems, arrays, after, name):
    n = len(arrays) // 2
    return _split_wait(_scatter_srcs(n), _scatter_dsts(n), send_sems, recv_sems, arrays, after, name)[n:]


def _join_halves(rhs, name):
    n = len(rhs)
    any_spec = pl.BlockSpec(memory_space=pl.ANY)

    def body(*refs):
        outs = refs[n:2 * n]
        send_sems, recv_sems = refs[2 * n:]
        x, y, c, _ = _place()
        _sibling_handshake(x, y, c)
        copies = []
        for a in range(n):
            copies.append(pltpu.make_async_remote_copy(
                src_ref=outs[a].at[c], dst_ref=outs[a].at[c], send_sem=send_sems.at[a],
                recv_sem=recv_sems.at[a], device_id=(x, y, 1 - c), device_id_type=MESH))
        for cp in copies:
            cp.start()
        for a in range(n):
            landed = outs[a].at[1 - c]
            pltpu.make_async_remote_copy(
                src_ref=landed, dst_ref=landed, send_sem=send_sems.at[a], recv_sem=recv_sems.at[a],
                device_id=(x, y, c), device_id_type=MESH).wait_recv()
        for cp in copies:
            cp.wait_send()

    out = pl.pallas_call(
        body, name=name,
        in_specs=[any_spec] * n, out_specs=[any_spec] * n,
        out_shape=[jax.ShapeDtypeStruct(r.shape, r.dtype) for r in rhs],
        input_output_aliases={a: a for a in range(n)},
        scratch_shapes=[pltpu.SemaphoreType.DMA((n,)), pltpu.SemaphoreType.DMA((n,))],
        compiler_params=SIBLING_ONLY,
    )(*rhs)
    return [_merge2(o) for o in out]


def _peer_copies(ins, send_sems, recv_sems, started):
    x, y, c, _ = _place()
    copies = []
    for k, (dx, dy, dc) in enumerate(FLIPS):
        px, py, pc = jnp.bitwise_xor(x, dx), jnp.bitwise_xor(y, dy), jnp.bitwise_xor(c, dc)
        slot = 4 * x + 2 * y + c if started else 4 * px + 2 * py + pc
        copies.append(pltpu.make_async_remote_copy(
            src_ref=ins[0], dst_ref=ins[1].at[slot], send_sem=send_sems.at[k], recv_sem=recv_sems.at[k],
            device_id=(px, py, pc), device_id_type=MESH))
    return copies


def _small_start(v, name):
    land = lax.empty((8,) + v.shape, v.dtype)
    return _split_start(None, None, [v, land], 0, name, SMALL_ID, peers_of=_all_peers, copies_of=_peer_copies,
                        n_sems=len(FLIPS))


def _small_wait(send_sems, recv_sems, arrays, after, name):
    return _split_wait(None, None, send_sems, recv_sems, arrays, after, name, copies_of=_peer_copies)[1]


def _sum_slots(land, v, me):
    rows, cols = v.shape

    def body(me_ref, land_ref, v_ref, o_ref):
        o_ref[...] = jnp.zeros_like(o_ref)
        for d in range(8):
            @pl.when(me_ref[0] == d)
            def _():
                o_ref[...] += v_ref[...]

            @pl.when(me_ref[0] != d)
            def _():
                o_ref[...] += land_ref[d]

    return pl.pallas_call(
        body, name="sum_slots",
        grid_spec=pltpu.PrefetchScalarGridSpec(
            num_scalar_prefetch=1, grid=(1,),
            in_specs=[pl.BlockSpec((8, rows, cols), lambda i, me_ref: (0, 0, 0)),
                      pl.BlockSpec((rows, cols), lambda i, me_ref: (0, 0))],
            out_specs=pl.BlockSpec((rows, cols), lambda i, me_ref: (0, 0))),
        out_shape=jax.ShapeDtypeStruct((rows, cols), F32),
        compiler_params=_cparams(("arbitrary",)),
    )(me, land, v)


def _by_shape(arrays):
    groups = {}
    for k, a in enumerate(arrays):
        groups.setdefault(a.shape, []).append(k)
    return list(groups.values())


def _add_sibling_half(gs, sbs, idx, name):
    n = len(gs)
    ns, r, c = gs[0].shape
    r2 = r // 2

    def body(idx_ref, *refs):
        for a in range(n):
            g_ref, sb_ref, hown_ref, hb_ref = refs[a], refs[n + a], refs[2 * n + a], refs[3 * n + a]
            h = g_ref[...] + sb_ref[...].astype(F32)
            hb_ref[...] = h.astype(BF16)

            @pl.when(pl.program_id(0) == idx_ref[0])
            def _():
                hown_ref[...] = h

    spec = pl.BlockSpec((None, r2, c), lambda s, idx_ref: (s, 0, 0))
    out = pl.pallas_call(
        body, name=name,
        grid_spec=pltpu.PrefetchScalarGridSpec(
            num_scalar_prefetch=1, grid=(ns,),
            in_specs=[pl.BlockSpec((None, r2, c), lambda s, idx_ref: (s, idx_ref[4], 0))] * n + [spec] * n,
            out_specs=[pl.BlockSpec((r2, c), lambda s, idx_ref: (0, 0))] * n + [spec] * n),
        out_shape=[jax.ShapeDtypeStruct((r2, c), F32)] * n + [jax.ShapeDtypeStruct((ns, r2, c), BF16)] * n,
        compiler_params=_cparams(("arbitrary",)),
    )(idx, *gs, *sbs)
    return [(out[a], out[n + a]) for a in range(n)]


def _add_chip_slabs(hs, rbs, idx, name):
    n = len(hs)
    r2, c = hs[0].shape

    def body(idx_ref, *refs):
        for a in range(n):
            h_ref, r0_ref, r1_ref, r2_ref = refs[4 * a:4 * a + 4]
            refs[4 * n + a][...] = ((h_ref[...] + r0_ref[...].astype(F32)) + r1_ref[...].astype(F32)) + r2_ref[...].astype(F32)

    def pick(k):
        return pl.BlockSpec((None, r2, c), lambda i, idx_ref: (idx_ref[k], 0, 0))

    operands = []
    for h, rb in zip(hs, rbs):
        operands += [h, rb, rb, rb]
    out = pl.pallas_call(
        body, name=name,
        grid_spec=pltpu.PrefetchScalarGridSpec(
            num_scalar_prefetch=1, grid=(1,),
            in_specs=[pl.BlockSpec((r2, c), lambda i, idx_ref: (0, 0)), pick(1), pick(2), pick(3)] * n,
            out_specs=[pl.BlockSpec((None, r2, c), lambda i, idx_ref: (idx_ref[4], 0, 0))] * n),
        out_shape=[jax.ShapeDtypeStruct((2, r2, c), F32)] * n,
        compiler_params=_cparams(("arbitrary",)),
    )(idx, *operands)
    return list(out)


ELEMENTWISE_VMEM = 16 * 1024 * 1024


def _adamw(items, name):
    n = len(items)
    r, c = items[0][0].shape
    br = max(b for b in range(8, r + 1, 8) if r % b == 0 and n * 16 * b * c * 4 <= ELEMENTWISE_VMEM) if r % 8 == 0 else r

    def body(*refs):
        for a in range(n):
            g_ref, w_ref, m_ref, v_ref = refs[4 * a:4 * a + 4]
            go_ref, d_ref, nm_ref, nv_ref = refs[4 * n + 4 * a:4 * n + 4 * a + 4]
            gg = g_ref[...]
            go_ref[...] = gg
            nm = B1 * m_ref[...] + (1.0 - B1) * gg
            nv = B2 * v_ref[...] + (1.0 - B2) * jnp.square(gg)
            m_hat = nm / (1.0 - B1 ** STEP)
            v_hat = nv / (1.0 - B2 ** STEP)
            d_ref[...] = -LR * (m_hat / (jnp.sqrt(v_hat) + ADAM_EPS) + WD * w_ref[...])
            nm_ref[...] = nm
            nv_ref[...] = nv

    spec = pl.BlockSpec((br, c), lambda i: (i, 0))
    out = pl.pallas_call(
        body, name=name,
        grid=(r // br,),
        in_specs=[spec] * (4 * n), out_specs=[spec] * (4 * n),
        out_shape=[jax.ShapeDtypeStruct((r, c), F32)] * (4 * n),
        compiler_params=_cparams(("arbitrary",)),
    )(*[a for item in items for a in item])
    return [tuple(out[4 * a:4 * a + 4]) for a in range(n)]


BIG = ("w_in", "w_conv_out", "w_pool", "w_pool_out", "w_o", "w_ffn_gate", "w_ffn_up", "w_ffn_down")
REPL = ("g_mix", "b_gate", "b_dw", "ln_g", "ln_b", "pool_scale", "g_ffn", "g_final")
GROUP_MIX = ("w_conv_out", "w_pool", "w_pool_out", "w_o")
GROUP_FFN = ("w_ffn_gate", "w_ffn_up", "w_ffn_down")
TRANSPOSED = ("w_ffn_gate", "w_ffn_up")
WEIGHT_ORDER = ("meta_tokens", "g_mix", "w_in", "b_gate", "w_dw", "b_dw", "ln_g", "ln_b", "w_conv_out", "w_pool",
                "pool_scale", "w_pool_out", "w_o", "g_ffn", "w_ffn_gate", "w_ffn_up", "w_ffn_down", "g_final")


def _shard2d(name, a):
    a = a[0]
    if name == "w_pool":
        return a.reshape(4 * 64, GD)
    if name in TRANSPOSED:
        return a.T
    return a


def _unshard2d(name, a, shape):
    return a.T.reshape(shape) if name in TRANSPOSED else a.reshape(shape)


def _slabs_to_cols(a):
    ns, m, c = a.shape
    return a.transpose(1, 0, 2).reshape(m, ns * c)


def kernel(x, meta_tokens, g_mix, w_in, b_gate, w_dw, b_dw, ln_g, ln_b, w_conv_out, w_pool, pool_scale, w_pool_out, w_o, g_ffn, w_ffn_gate, w_ffn_up, w_ffn_down, g_final, loss_target, m_meta_tokens, m_g_mix, m_w_in, m_b_gate, m_w_dw, m_b_dw, m_ln_g, m_ln_b, m_w_conv_out, m_w_pool, m_pool_scale, m_w_pool_out, m_w_o, m_g_ffn, m_w_ffn_gate, m_w_ffn_up, m_w_ffn_down, m_g_final, v_meta_tokens, v_g_mix, v_w_in, v_b_gate, v_w_dw, v_b_dw, v_ln_g, v_ln_b, v_w_conv_out, v_w_pool, v_pool_scale, v_w_pool_out, v_w_o, v_g_ffn, v_w_ffn_gate, v_w_ffn_up, v_w_ffn_down, v_g_final):
    args = dict(locals())
    w = {n: args[n] for n in WEIGHT_ORDER}
    mom = {n: args["m_" + n] for n in WEIGHT_ORDER}
    var = {n: args["v_" + n] for n in WEIGHT_ORDER}
    seq = x.shape[1]
    nb = seq // BR + 1
    tp = nb * BR
    tk = tp // 2 if (tp // 2) % 16 == 0 else BR
    t_total = seq + N_META
    cx, cy, cc = lax.axis_index("x"), lax.axis_index("y"), lax.axis_index("c")
    chip = 2 * cx + cy
    chip1 = jnp.reshape(chip, (1,)).astype(jnp.int32)
    core = jnp.reshape(cc, (1,)).astype(jnp.int32)
    others = jnp.sort(jnp.stack([2 * (1 - cx) + cy, 2 * cx + (1 - cy), 2 * (1 - cx) + (1 - cy)]))
    idx = jnp.concatenate([chip1, others.astype(jnp.int32), core])
    xs, target = x[0], loss_target[0]

    tiny_own = jnp.concatenate([w["meta_tokens"], w["w_dw"][0], jnp.zeros((1, GD), F32)], axis=0)
    st_tiny = _tiny_start(tiny_own, "tiny_start")
    small = {n: w[n] for n in REPL if n != "g_final"}
    small["g_final"] = w["g_final"].reshape(1, D)

    def cast(group, dep):
        shards = [_shard2d(n, w[n]) for n in group]
        bufs = [None] * len(group)
        for ks in _by_shape(shards):
            done = _cast_into_slot([shards[k] for k in ks], chip1, dep, "cast_" + group[ks[0]])
            for k, b in zip(ks, done):
                bufs[k] = b
        return bufs

    def gather_finish(group, start, after, name):
        landed = _gather_wait(start[0], start[1], start[2], after, "gather_wait_" + name)
        return dict(zip(group, _forward_halves(landed, "forward_" + name)))

    st_in = _gather_start(cast(("w_in",), st_tiny[3]), None, "gather_start_in")
    tiny = _tiny_wait(st_tiny[0], st_tiny[1], st_tiny[2], [st_in[3]], "tiny_wait")
    tiny = lax.dynamic_update_slice(tiny, tiny_own[None], (chip, 0, 0))
    small["w_dw"] = _slabs_to_cols(tiny[:, N_META:])
    head = jnp.concatenate([jnp.zeros((PAD, D), F32), _slabs_to_cols(tiny[:, :N_META])], axis=0)
    bufs_mix, bufs_ffn = cast(GROUP_MIX, st_in[3]), cast(GROUP_FFN, st_in[3])
    u = _rms_u(head, xs, small["g_mix"] + st_in[3][0:1, 0:1], nb)
    z_own = _in_proj_own(u, w["w_in"][0], idx, nb)
    gw = gather_finish(("w_in",), st_in, [z_own] + bufs_mix + bufs_ffn, "in")
    st_mix = _gather_start(bufs_mix, gw["w_in"], "gather_start_mix")
    z = _in_proj_rest(u, gw["w_in"], z_own, idx, st_mix[3], nb)
    gw.update(gather_finish(GROUP_MIX, st_mix, [z], "mix"))
    st_ffn = _gather_start(bufs_ffn, gw["w_o"], "gather_start_ffn")
    w_pool_b = gw["w_pool"].reshape(N_SHARD, 4, 64, GD).transpose(1, 0, 2, 3).reshape(4, GD, GD)
    w_co_b, w_po_b, w_o_b = (gw[n].reshape(D, D) for n in ("w_conv_out", "w_pool_out", "w_o"))
    h1, yc, yp, mg, ca, cpre, m, mw, m2b = _mixers_fwd(
        z, head, xs, small["b_gate"] + st_ffn[3][0, 0], small["w_dw"], small["b_dw"], small["ln_g"], small["ln_b"],
        small["pool_scale"], w_co_b, w_pool_b, w_po_b, w_o_b, nb, t_total)
    gw.update(gather_finish(GROUP_FFN, st_ffn, [h1], "ffn"))

    dh1, dh1b, vb, fb, dgb, dub, dh2b, loss, dg_ffn, dg_final = _ffn_fwd_bwd(
        h1, target, small["g_ffn"], small["g_final"], gw["w_ffn_gate"].reshape(D_FF, D),
        gw["w_ffn_up"].reshape(D_FF, D), gw["w_ffn_down"].reshape(D_FF, D), nb)

    def slabs(name, g):
        if name == "w_in":
            return g
        if name == "w_pool":
            return g.reshape(4, N_SHARD, 64, GD).transpose(1, 0, 2, 3).reshape(N_SHARD, 4 * 64, GD)
        return g.reshape(N_SHARD, g.shape[0] // N_SHARD, g.shape[1])

    def reduce_start(group, grads, name):
        g32 = [slabs(n, grads[n][0]) for n in group]
        g16 = [slabs(n, grads[n][1]) for n in group]
        from_sibling = _swap_halves_bf16(g16, "swap_halves_" + name)
        halves = [None] * len(group)
        for ks in _by_shape(g32):
            done = _add_sibling_half([g32[k] for k in ks], [from_sibling[k] for k in ks], idx, "add_sibling_" + group[ks[0]])
            for k, pair in zip(ks, done):
                halves[k] = pair
        return [h for h, _ in halves], _scatter_start([hb for _, hb in halves], "scatter_start_" + name, SCATTER_IDS[name])

    def reduce_finish(group, halves, start, after, name):
        from_chips = _scatter_wait(start[0], start[1], start[2], after, "scatter_wait_" + name)
        reduced = [None] * len(group)
        for ks in _by_shape(halves):
            done = _add_chip_slabs([halves[k] for k in ks], [from_chips[k] for k in ks], idx, "add_chips_" + group[ks[0]])
            for k, r in zip(ks, done):
                reduced[k] = r
        return reduced

    half_ff = D_FF // 2
    grads_ffn = {
        "w_ffn_gate": _wgrad(dgb, vb, half_ff, D, tk, "wgrad_ffn_gate"),
        "w_ffn_up": _wgrad(dub, vb, half_ff, D, tk, "wgrad_ffn_up"),
        "w_ffn_down": _wgrad(fb, dh2b, half_ff, D, tk, "wgrad_ffn_down"),
    }
    halves_ffn, sc_ffn = reduce_start(GROUP_FFN, grads_ffn, "ffn")

    dycb, dypb, dzg, dconv, dmwb, dm, db_gate, dln_g, dln_b, db_dw, dps = _mixers_bwd_rows(
        dh1b, yc, yp, z, small["b_gate"], cpre, small["ln_g"], small["ln_b"], mw, small["pool_scale"],
        w_o_b, w_co_b, w_po_b, w_pool_b, sc_ffn[3], nb)
    grads_mix = {
        "w_conv_out": _wgrad(ca, dycb, D, D, tk, "wgrad_conv_out"),
        "w_pool": _wgrad(m, dmwb, GD, GD, tp, "wgrad_pool", diag=True),
        "w_pool_out": _wgrad(m2b, dypb, D, D, tk, "wgrad_pool_out"),
        "w_o": _wgrad(mg, dh1b, D, D, tk, "wgrad_o"),
    }
    halves_mix, sc_mix = reduce_start(GROUP_MIX, grads_mix, "mix")
    dzb, grad_x, dhead, dw_dw, dg_mix = _mixers_bwd_halo(
        dconv, dm, z, dzg, small["w_dw"], head, xs, small["g_mix"], dh1, gw["w_in"], sc_mix[3], nb, t_total)
    packed = jnp.concatenate(
        [dg_mix, db_gate.reshape(2, D), db_dw, dln_g, dln_b, dps, dg_ffn, dg_final,
         jnp.broadcast_to(loss, (1, D)), jnp.zeros((6, D), F32), dhead[PAD:], dw_dw], axis=0)
    sm = _small_start(packed, "small_start")
    grads_in = {"w_in": _wgrad(u, dzb, D, D_IN // N_SHARD, tk, "wgrad_in", col_major=True, dep=sm[3])}
    halves_in, sc_in = reduce_start(("w_in",), grads_in, "in")

    land = _small_wait(sm[0], sm[1], sm[2], [sc_in[3]], "small_wait")
    summed = _sum_slots(land, packed, jnp.reshape(4 * cx + 2 * cy + cc, (1,)).astype(jnp.int32))
    loss = summed[9, 0]

    first = GROUP_FFN + GROUP_MIX
    reduced_half = reduce_finish(GROUP_FFN, halves_ffn, sc_ffn, [summed], "ffn")
    reduced_half += reduce_finish(GROUP_MIX, halves_mix, sc_mix, [summed], "mix")
    reduced = dict(zip(first, _join_halves(reduced_half, "join_halves_first")))
    updates = {}
    for ks in _by_shape([reduced[n] for n in first]):
        names = [first[k] for k in ks]
        done = _adamw([(reduced[n], _shard2d(n, w[n]), _shard2d(n, mom[n]), _shard2d(n, var[n])) for n in names],
                      "adamw_" + names[0])
        updates.update(zip(names, done))

    def repl_stack(d):
        return jnp.concatenate([d["g_mix"], d["b_gate"].reshape(2, D), d["b_dw"], d["ln_g"], d["ln_b"],
                                d["pool_scale"], d["g_ffn"], d["g_final"].reshape(1, D), jnp.ones((7, D), F32)], axis=0)

    def shard_stack(d):
        return jnp.concatenate([d["meta_tokens"], d["w_dw"][0], jnp.ones((1, GD), F32)], axis=0)

    g_repl = summed[0:16]
    g_shard = lax.dynamic_slice_in_dim(summed[16:64], chip * GD, GD, axis=1)
    g_repl, d_repl, m_repl, v_repl = _adamw([(g_repl, repl_stack(w), repl_stack(mom), repl_stack(var))], "adamw_repl")[0]
    g_shard, d_shard, m_shard, v_shard = _adamw(
        [(g_shard, shard_stack(w), shard_stack(mom), shard_stack(var))], "adamw_cols")[0]

    done_first = [updates[n][1] for n in first] + [d_repl, d_shard]
    last_half = reduce_finish(("w_in",), halves_in, sc_in, done_first, "in")
    reduced["w_in"] = _join_halves(last_half, "join_halves_in")[0]
    updates["w_in"] = _adamw([(reduced["w_in"], w["w_in"][0], mom["w_in"][0], var["w_in"][0])], "adamw_w_in")[0]

    def unpack(name, repl, shard):
        if name == "meta_tokens":
            return shard[0:N_META]
        if name == "w_dw":
            return shard[N_META:N_META + KW].reshape(1, KW, GD)
        row = {"g_mix": 0, "b_gate": 1, "b_dw": 3, "ln_g": 4, "ln_b": 5, "pool_scale": 6, "g_ffn": 7, "g_final": 8}[name]
        if name == "b_gate":
            return repl[1:3].reshape(1, 2 * D)
        if name == "g_final":
            return repl[8]
        return repl[row:row + 1]

    out_g, out_d, out_m, out_v = {}, {}, {}, {}
    for n in WEIGHT_ORDER:
        if n in BIG:
            g, d_, m_, v_ = updates[n]
            shape = w[n].shape
            out_g[n], out_d[n], out_m[n], out_v[n] = (_unshard2d(n, a, shape) for a in (g, d_, m_, v_))
        else:
            out_g[n] = unpack(n, g_repl, g_shard)
            out_d[n] = unpack(n, d_repl, d_shard)
            out_m[n] = unpack(n, m_repl, m_shard)
            out_v[n] = unpack(n, v_repl, v_shard)
    return (loss, grad_x[None], *[out_g[n] for n in WEIGHT_ORDER], *[out_d[n] for n in WEIGHT_ORDER],
            *[out_m[n] for n in WEIGHT_ORDER], *[out_v[n] for n in WEIGHT_ORDER])
```

```python
import jax
import jax.numpy as jnp
from jax import lax
from jax.experimental import pallas as pl
from jax.experimental.pallas import tpu as pltpu

F32 = jnp.float32
BF16 = jnp.bfloat16
MESH = pl.DeviceIdType.MESH

D = 1024
N_META = 16
KW = 31
POOL_WINDOWS = (2, 4, 8, 16)
GD = 256
D_IN = 5 * D
D_FF = 2816
N_SHARD = 4
BR = 256
HALO = 16
PAD = BR - N_META
EXT = BR + 2 * HALO
RMS_EPS = 1e-6
LN_EPS = 1e-5
LR, B1, B2, ADAM_EPS, WD, STEP = 0.001, 0.9, 0.999, 1e-08, 0.01, 10
VMEM_LIMIT = 56 * 1024 * 1024


def _cparams(sem, vmem=VMEM_LIMIT):
    return pltpu.CompilerParams(dimension_semantics=sem, vmem_limit_bytes=vmem)


def _dot(a, b):
    return jnp.dot(a, b, preferred_element_type=F32)


def _dot_nt(a, b):
    return lax.dot_general(a, b, (((1,), (1,)), ((), ())), preferred_element_type=F32)


def _dot_tn(a, b):
    return lax.dot_general(a, b, (((0,), (0,)), ((), ())), preferred_element_type=F32)


def _sigmoid(x):
    return 0.5 * jnp.tanh(0.5 * x) + 0.5


def _row_ids(i, n, offset=0):
    return lax.broadcasted_iota(jnp.int32, (n, 1), 0) + (i * BR + offset - PAD)


def _pool_cnt(t, w, t_total):
    left = w // 2
    right = w - 1 - left
    lo = jnp.clip(t - left, 0, t_total)
    hi = jnp.clip(t + right + 1, 0, t_total)
    return jnp.maximum(hi - lo, 1).astype(F32)


def _halo_specs(nb, halo_width=D, width=D):
    last = nb * (BR // HALO) - 1
    return [
        pl.BlockSpec((HALO, halo_width), lambda i: (jnp.maximum(i * (BR // HALO) - 1, 0), 0)),
        pl.BlockSpec((BR, width), lambda i: (i, 0)),
        pl.BlockSpec((HALO, halo_width), lambda i: (jnp.minimum((i + 1) * (BR // HALO), last), 0)),
    ]


def _cols(ref, n):
    return [ref.at[:, k * D:(k + 1) * D] for k in range(n)]


def _fill_ext(ext_ref, prev, cur, nxt, i, nb):
    ext_ref[0:HALO, :] = jnp.where(i > 0, prev, 0.0)
    ext_ref[HALO:HALO + BR, :] = cur
    ext_ref[HALO + BR:EXT, :] = jnp.where(i < nb - 1, nxt, 0.0)


ROT_ROWS = EXT - 8


def _fill_rot(rot_ref, ext_ref, lanes):
    for r in range(1, 8):
        rot_ref[r] = ext_ref[pl.ds(r, ROT_ROWS), lanes]


def _tap(rot_ref, ext_ref, lanes, offset):
    q, r = divmod(offset, 8)
    if r == 0:
        return ext_ref[pl.ds(8 * q, BR), lanes]
    return rot_ref[r, pl.ds(8 * q, BR), :]


def _row_spec(width=D):
    return pl.BlockSpec((BR, width), lambda i: (i, 0))


def _x_spec():
    return pl.BlockSpec((BR, D), lambda i: (jnp.maximum(i - 1, 0), 0))


def _const_spec(shape):
    nd = len(shape)
    return pl.BlockSpec(shape, lambda i: (0,) * nd)


def _rms_u(head, x, g_mix, nb):
    def body(head_ref, x_ref, g_ref, u_ref):
        i = pl.program_id(0)
        h = jnp.where(i == 0, head_ref[...], x_ref[...])
        r = lax.rsqrt(jnp.mean(h * h, axis=-1, keepdims=True) + RMS_EPS)
        u_ref[...] = ((h * r) * g_ref[...]).astype(BF16)

    return pl.pallas_call(
        body, name="rms_u",
        grid=(nb,),
        in_specs=[_const_spec((BR, D)), _x_spec(), _const_spec((1, D))],
        out_specs=_row_spec(),
        out_shape=jax.ShapeDtypeStruct((nb * BR, D), BF16),
        compiler_params=_cparams(("arbitrary",)),
    )(head, x, g_mix)


def _in_proj_rows(tp):
    return tp // 2 if (tp // 2) % 16 == 0 else BR


def _in_proj_own(u, w_own, idx, nb):
    tp = nb * BR
    wcols = w_own.shape[1]
    rows = _in_proj_rows(tp)

    def body(idx_ref, u_ref, w_ref, z_ref, wb_ref):
        @pl.when(pl.program_id(0) == 0)
        def _():
            wb_ref[...] = w_ref[...].astype(BF16)

        z_ref[...] = _dot(u_ref[...], wb_ref[...])

    return pl.pallas_call(
        body, name="in_proj_own",
        grid_spec=pltpu.PrefetchScalarGridSpec(
            num_scalar_prefetch=1, grid=(tp // rows,),
            in_specs=[pl.BlockSpec((rows, D), lambda i, idx_ref: (i, 0)),
                      pl.BlockSpec((D, wcols), lambda i, idx_ref: (0, 0))],
            out_specs=pl.BlockSpec((rows, wcols), lambda i, idx_ref: (i, idx_ref[0])),
            scratch_shapes=[pltpu.VMEM((D, wcols), BF16)]),
        out_shape=jax.ShapeDtypeStruct((tp, N_SHARD * wcols), F32),
        compiler_params=_cparams(("arbitrary",)),
    )(idx, u, w_own)


def _in_proj_rest(u, w_in_b, z, idx, dep, nb):
    tp = nb * BR
    wcols = w_in_b.shape[2]
    rows = _in_proj_rows(tp)

    def body(idx_ref, u_ref, w_ref, z_in, dep_ref, z_ref):
        z_ref[...] = _dot(u_ref[...], w_ref[...])

    any_spec = pl.BlockSpec(memory_space=pl.ANY)
    return pl.pallas_call(
        body, name="in_proj_rest",
        grid_spec=pltpu.PrefetchScalarGridSpec(
            num_scalar_prefetch=1, grid=(N_SHARD - 1, tp // rows),
            in_specs=[pl.BlockSpec((rows, D), lambda s, i, idx_ref: (i, 0)),
                      pl.BlockSpec((None, D, wcols), lambda s, i, idx_ref: (idx_ref[1 + s], 0, 0)),
                      any_spec, any_spec],
            out_specs=pl.BlockSpec((rows, wcols), lambda s, i, idx_ref: (i, idx_ref[1 + s]))),
        out_shape=jax.ShapeDtypeStruct(z.shape, F32),
        input_output_aliases={3: 0},
        compiler_params=_cparams(("arbitrary", "arbitrary")),
    )(idx, u, w_in_b, z, dep)


def _mixers_fwd(z, head, x, b_gate, w_dw, b_dw, ln_g, ln_b, pool_scale, w_co, w_pool, w_po, w_o, nb, t_total):
    tp = nb * BR

    def body(z_prev, z_cur, z_next, head_ref, x_ref, bg_ref, wdw_ref, bdw_ref,
             lng_ref, lnb_ref, ps_ref, wco_ref, wpool_ref, wpo_ref, wo_ref,
             h1_ref, yc_ref, yp_ref, mg_ref, ca_ref, cpre_ref, m_ref, mw_ref, m2b_ref, ext_ref, pext_ref, rot_ref):
        i = pl.program_id(0)
        avp, agp, pp = _cols(z_prev, 3)
        av, ag, pc, za, zb = _cols(z_cur, 5)
        avn, agn, pn = _cols(z_next, 3)
        _fill_ext(ext_ref, avp[...] * _sigmoid(agp[...]), av[...] * _sigmoid(ag[...]),
                  avn[...] * _sigmoid(agn[...]), i, nb)
        _fill_ext(pext_ref, pp[...], pc[...], pn[...], i, nb)

        def conv_chunk(c, carry):
            lanes = pl.ds(pl.multiple_of(c * 128, 128), 128)
            _fill_rot(rot_ref, ext_ref, lanes)
            acc = jnp.broadcast_to(bdw_ref[:, lanes], (BR, 128))
            for k in range(KW):
                acc = acc + wdw_ref[k:k + 1, lanes] * _tap(rot_ref, ext_ref, lanes, 1 + k)
            cpre_ref[:, lanes] = acc
            return carry
        lax.fori_loop(0, D // 128, conv_chunk, 0)

        conv = cpre_ref[...]
        mu = jnp.mean(conv, axis=-1, keepdims=True)
        xc = conv - mu
        rstd = lax.rsqrt(jnp.mean(xc * xc, axis=-1, keepdims=True) + LN_EPS)
        ln = (xc * rstd) * lng_ref[...] + lnb_ref[...]
        cact = (ln * _sigmoid(ln)).astype(BF16)
        ca_ref[...] = cact
        y_conv = _dot(cact, wco_ref[...])
        yc_ref[...] = y_conv

        t = _row_ids(i, BR)
        for gi, w in enumerate(POOL_WINDOWS):
            left = w // 2
            right = w - 1 - left
            lanes = slice(gi * GD, (gi + 1) * GD)
            s = pext_ref[pl.ds(HALO - left, BR), lanes]
            for j in range(-left + 1, right + 1):
                s = s + pext_ref[pl.ds(HALO + j, BR), lanes]
            m = (s / _pool_cnt(t, w, t_total) - pext_ref[HALO:HALO + BR, lanes]).astype(BF16)
            m_ref[:, lanes] = m
            mw_ref[:, lanes] = _dot(m, wpool_ref[gi])
        mw = mw_ref[...]
        m2b = (mw * ps_ref[...]).astype(BF16)
        m2b_ref[...] = m2b
        y_pool = _dot(m2b, wpo_ref[...])
        yp_ref[...] = y_pool

        s_a = _sigmoid(za[...] + bg_ref[:, 0:D])
        s_b = _sigmoid(zb[...] + bg_ref[:, D:2 * D])
        merged = (s_a * y_conv + s_b * y_pool).astype(BF16)
        mg_ref[...] = merged
        h0 = jnp.where(i == 0, head_ref[...], x_ref[...])
        h1_ref[...] = h0 + _dot(merged, wo_ref[...])

    in_specs = (_halo_specs(nb, 3 * D, 5 * D)
                + [_const_spec((BR, D)), _x_spec(), _const_spec((1, 2 * D)), _const_spec((32, D)),
                   _const_spec((1, D)), _const_spec((1, D)), _const_spec((1, D)), _const_spec((1, D)),
                   _const_spec((D, D)), _const_spec((4, GD, GD)), _const_spec((D, D)), _const_spec((D, D))])
    outs = [(F32, "h1"), (F32, "yc"), (F32, "yp"), (BF16, "mg"), (BF16, "ca"), (F32, "cpre"), (BF16, "m"), (F32, "mw"),
            (BF16, "m2b")]
    return pl.pallas_call(
        body, name="mixers_fwd",
        grid=(nb,),
        in_specs=in_specs,
        out_specs=[_row_spec() for _ in outs],
        out_shape=[jax.ShapeDtypeStruct((tp, D), dt) for dt, _ in outs],
        scratch_shapes=[pltpu.VMEM((EXT, D), F32), pltpu.VMEM((EXT, D), F32), pltpu.VMEM((8, ROT_ROWS, 128), F32)],
        compiler_params=_cparams(("arbitrary",)),
    )(z, z, z, head, x, b_gate, w_dw, b_dw, ln_g, ln_b, pool_scale, w_co, w_pool, w_po, w_o)


def _ffn_fwd_bwd(h1, target, g_ffn, g_final, w_g, w_u, w_d, nb):
    tp = nb * BR

    def body(h1_ref, tgt_ref, gf_ref, gfin_ref, wg_hbm, wu_hbm, wd_hbm,
             dh1_ref, dh1b_ref, vb_ref, fb_ref, dgb_ref, dub_ref, dh2b_ref, loss_ref, dgf_ref, dgfin_ref,
             wg_ref, wu_ref, wd_ref, sem):
        i = pl.program_id(0)

        @pl.when(i == 0)
        def _():
            copies = [pltpu.make_async_copy(wg_hbm, wg_ref, sem.at[0]),
                      pltpu.make_async_copy(wu_hbm, wu_ref, sem.at[1]),
                      pltpu.make_async_copy(wd_hbm, wd_ref, sem.at[2])]
            for cp in copies:
                cp.start()
            loss_ref[...] = jnp.zeros_like(loss_ref)
            dgf_ref[...] = jnp.zeros_like(dgf_ref)
            dgfin_ref[...] = jnp.zeros_like(dgfin_ref)
            for cp in copies:
                cp.wait()

        h1 = h1_ref[...]
        r1 = lax.rsqrt(jnp.mean(h1 * h1, axis=-1, keepdims=True) + RMS_EPS)
        vn = h1 * r1
        vb = (vn * gf_ref[...]).astype(BF16)
        vb_ref[...] = vb
        g = _dot_nt(vb, wg_ref[...])
        up = _dot_nt(vb, wu_ref[...])
        sg = _sigmoid(g)
        sl = g * sg
        fb = (sl * up).astype(BF16)
        fb_ref[...] = fb
        h2 = h1 + _dot(fb, wd_ref[...])
        r2 = lax.rsqrt(jnp.mean(h2 * h2, axis=-1, keepdims=True) + RMS_EPS)
        yn = h2 * r2
        valid = i > 0
        diff = jnp.where(valid, yn * gfin_ref[...] - tgt_ref[...], 0.0)
        loss_ref[...] += 0.5 * jnp.sum(jnp.mean(diff * diff, axis=-1, keepdims=True))
        dy = diff * (1.0 / D)
        dgfin_ref[...] += jnp.sum(dy * yn, axis=0, keepdims=True)
        gd = dy * gfin_ref[...]
        dh2 = r2 * (gd - yn * jnp.mean(yn * gd, axis=-1, keepdims=True))
        dh2b = dh2.astype(BF16)
        dh2b_ref[...] = dh2b
        df = _dot_nt(dh2b, wd_ref[...])
        dub = (df * sl).astype(BF16)
        dgb = (df * up * (sg * (1.0 + g * (1.0 - sg)))).astype(BF16)
        dub_ref[...] = dub
        dgb_ref[...] = dgb
        dv = _dot(dgb, wg_ref[...]) + _dot(dub, wu_ref[...])
        dgf_ref[...] += jnp.sum(dv * vn, axis=0, keepdims=True)
        gd1 = dv * gf_ref[...]
        dh1 = dh2 + r1 * (gd1 - vn * jnp.mean(vn * gd1, axis=-1, keepdims=True))
        dh1_ref[...] = dh1
        dh1b_ref[...] = dh1.astype(BF16)

    any_spec = pl.BlockSpec(memory_space=pl.ANY)
    return pl.pallas_call(
        body, name="ffn_fwd_bwd",
        grid=(nb,),
        in_specs=[_row_spec(), _x_spec(), _const_spec((1, D)), _const_spec((1, D)), any_spec, any_spec, any_spec],
        out_specs=[_row_spec(), _row_spec(), _row_spec(), _row_spec(D_FF), _row_spec(D_FF), _row_spec(D_FF), _row_spec(),
                   _const_spec((1, 1)), _const_spec((1, D)), _const_spec((1, D))],
        out_shape=[jax.ShapeDtypeStruct((tp, D), F32), jax.ShapeDtypeStruct((tp, D), BF16),
                   jax.ShapeDtypeStruct((tp, D), BF16), jax.ShapeDtypeStruct((tp, D_FF), BF16),
                   jax.ShapeDtypeStruct((tp, D_FF), BF16), jax.ShapeDtypeStruct((tp, D_FF), BF16),
                   jax.ShapeDtypeStruct((tp, D), BF16), jax.ShapeDtypeStruct((1, 1), F32),
                   jax.ShapeDtypeStruct((1, D), F32), jax.ShapeDtypeStruct((1, D), F32)],
        scratch_shapes=[pltpu.VMEM((D_FF, D), BF16), pltpu.VMEM((D_FF, D), BF16), pltpu.VMEM((D_FF, D), BF16),
                        pltpu.SemaphoreType.DMA((3,))],
        compiler_params=_cparams(("arbitrary",)),
    )(h1, target, g_ffn, g_final, w_g, w_u, w_d)


def _mixers_bwd_rows(dh1b, yc, yp, z, b_gate, cpre, ln_g, ln_b, mw, pool_scale, w_o, w_co, w_po, w_pool, dep, nb):
    tp = nb * BR

    def body(dh1b_ref, yc_ref, yp_ref, za, zb, bg_ref, cpre_ref, lng_ref, lnb_ref, mw_ref, ps_ref,
             wo_ref, wco_ref, wpo_ref, wpool_ref, dep_ref,
             dycb_ref, dypb_ref, dzg_ref, dconv_ref, dmwb_ref, dm_ref, dbg_ref, dlng_ref, dlnb_ref, dbdw_ref, dps_ref):
        i = pl.program_id(0)

        @pl.when(i == 0)
        def _():
            for r in (dbg_ref, dlng_ref, dlnb_ref, dbdw_ref, dps_ref):
                r[...] = jnp.zeros_like(r)

        dmg = _dot_nt(dh1b_ref[...], wo_ref[...])
        s_a = _sigmoid(za[...] + bg_ref[:, 0:D])
        s_b = _sigmoid(zb[...] + bg_ref[:, D:2 * D])
        dycb = (dmg * s_a).astype(BF16)
        dypb = (dmg * s_b).astype(BF16)
        dycb_ref[...] = dycb
        dypb_ref[...] = dypb
        dza = dmg * yc_ref[...] * (s_a * (1.0 - s_a))
        dzb = dmg * yp_ref[...] * (s_b * (1.0 - s_b))
        dzg_ref[:, 0:D] = dza.astype(BF16)
        dzg_ref[:, D:2 * D] = dzb.astype(BF16)
        dbg_ref[:, 0:D] += jnp.sum(dza, axis=0, keepdims=True)
        dbg_ref[:, D:2 * D] += jnp.sum(dzb, axis=0, keepdims=True)

        dca = _dot_nt(dycb, wco_ref[...])
        conv = cpre_ref[...]
        mu = jnp.mean(conv, axis=-1, keepdims=True)
        xc = conv - mu
        rstd = lax.rsqrt(jnp.mean(xc * xc, axis=-1, keepdims=True) + LN_EPS)
        xhat = xc * rstd
        ln = xhat * lng_ref[...] + lnb_ref[...]
        sg = _sigmoid(ln)
        dln = dca * (sg * (1.0 + ln * (1.0 - sg)))
        dlng_ref[...] += jnp.sum(dln * xhat, axis=0, keepdims=True)
        dlnb_ref[...] += jnp.sum(dln, axis=0, keepdims=True)
        dxh = dln * lng_ref[...]
        dconv = rstd * (dxh - jnp.mean(dxh, axis=-1, keepdims=True)
                        - xhat * jnp.mean(dxh * xhat, axis=-1, keepdims=True))
        dconv_ref[...] = dconv
        dbdw_ref[...] += jnp.sum(dconv, axis=0, keepdims=True)

        dm2 = _dot_nt(dypb, wpo_ref[...])
        dps_ref[...] += jnp.sum(dm2 * mw_ref[...], axis=0, keepdims=True)
        dmwb = (dm2 * ps_ref[...]).astype(BF16)
        dmwb_ref[...] = dmwb
        for gi in range(len(POOL_WINDOWS)):
            lanes = slice(gi * GD, (gi + 1) * GD)
            dm_ref[:, lanes] = _dot_nt(dmwb[:, lanes], wpool_ref[gi])

    in_specs = [_row_spec(), _row_spec(), _row_spec(),
                pl.BlockSpec((BR, D), lambda i: (i, 3)), pl.BlockSpec((BR, D), lambda i: (i, 4)),
                _const_spec((1, 2 * D)), _row_spec(), _const_spec((1, D)), _const_spec((1, D)), _row_spec(),
                _const_spec((1, D)), _const_spec((D, D)), _const_spec((D, D)), _const_spec((D, D)),
                _const_spec((4, GD, GD)), pl.BlockSpec(memory_space=pl.ANY)]
    return pl.pallas_call(
        body, name="mixers_bwd_rows",
        grid=(nb,),
        in_specs=in_specs,
        out_specs=[_row_spec(), _row_spec(), _row_spec(2 * D), _row_spec(), _row_spec(), _row_spec(),
                   _const_spec((1, 2 * D)), _const_spec((1, D)), _const_spec((1, D)), _const_spec((1, D)),
                   _const_spec((1, D))],
        out_shape=[jax.ShapeDtypeStruct((tp, D), BF16), jax.ShapeDtypeStruct((tp, D), BF16),
                   jax.ShapeDtypeStruct((tp, 2 * D), BF16), jax.ShapeDtypeStruct((tp, D), F32),
                   jax.ShapeDtypeStruct((tp, D), BF16), jax.ShapeDtypeStruct((tp, D), F32),
                   jax.ShapeDtypeStruct((1, 2 * D), F32), jax.ShapeDtypeStruct((1, D), F32),
                   jax.ShapeDtypeStruct((1, D), F32), jax.ShapeDtypeStruct((1, D), F32),
                   jax.ShapeDtypeStruct((1, D), F32)],
        compiler_params=_cparams(("arbitrary",)),
    )(dh1b, yc, yp, z, z, b_gate, cpre, ln_g, ln_b, mw, pool_scale, w_o, w_co, w_po, w_pool, dep)


def _mixers_bwd_halo(dconv, dm, z, dzg, w_dw, head, x, g_mix, dh1, w_in_b, dep, nb, t_total):
    tp = nb * BR
    ns = w_in_b.shape[0]
    wcols = w_in_b.shape[2]
    seq = x.shape[0]

    def body(dcp, dcc, dcn, dmp, dmc, dmn, z_prev, z_cur, z_next, dzg_ref, wdw_ref, head_ref, x_ref, g_ref,
             dh1_ref, w_hbm, dep_ref,
             dzb_ref, gx_ref, dhead_ref, dwdw_ref, dgmix_ref,
             w_ref, sem, aext_ref, dext_ref, qext_ref, da_ref, rot_ref, dwp_ref):
        i = pl.program_id(0)
        (avp, agp), (av, ag), (avn, agn) = _cols(z_prev, 2), _cols(z_cur, 2), _cols(z_next, 2)

        @pl.when(i == 0)
        def _():
            cp = pltpu.make_async_copy(w_hbm, w_ref, sem.at[0])
            cp.start()
            dwp_ref[...] = jnp.zeros_like(dwp_ref)
            dgmix_ref[...] = jnp.zeros_like(dgmix_ref)
            cp.wait()

        sig_g = _sigmoid(ag[...])
        _fill_ext(aext_ref, avp[...] * _sigmoid(agp[...]), av[...] * sig_g, avn[...] * _sigmoid(agn[...]), i, nb)
        _fill_ext(dext_ref, dcp[...], dcc[...], dcn[...], i, nb)
        _fill_ext(qext_ref, dmp[...], dmc[...], dmn[...], i, nb)

        def conv_chunk(c, carry):
            lanes = pl.ds(pl.multiple_of(c * 128, 128), 128)
            _fill_rot(rot_ref, dext_ref, lanes)
            acc = jnp.zeros((BR, 128), F32)
            for k in range(KW):
                acc = acc + wdw_ref[k:k + 1, lanes] * _tap(rot_ref, dext_ref, lanes, KW - k)
            da_ref[:, lanes] = acc
            _fill_rot(rot_ref, aext_ref, lanes)
            dcv = dext_ref[HALO:HALO + BR, lanes]
            for k in range(KW):
                prod = _tap(rot_ref, aext_ref, lanes, 1 + k) * dcv
                dwp_ref[k, :, lanes] += jnp.sum(prod.reshape(BR // 8, 8, 128), axis=0)
            return carry
        lax.fori_loop(0, D // 128, conv_chunk, 0)

        @pl.when(i == nb - 1)
        def _():
            dwdw_ref[...] = jnp.sum(dwp_ref[...], axis=1)

        da = da_ref[...]
        a_val = av[...]
        dzb_ref[:, 0:D] = (da * sig_g).astype(BF16)
        dzb_ref[:, D:2 * D] = (da * a_val * (sig_g * (1.0 - sig_g))).astype(BF16)

        t_ext = _row_ids(i, EXT, -HALO)
        for gi, w in enumerate(POOL_WINDOWS):
            left = w // 2
            right = w - 1 - left
            lanes = slice(gi * GD, (gi + 1) * GD)
            qext_ref[:, lanes] = qext_ref[:, lanes] / _pool_cnt(t_ext, w, t_total)
            s = qext_ref[pl.ds(HALO - right, BR), lanes]
            for j in range(-right + 1, left + 1):
                s = s + qext_ref[pl.ds(HALO + j, BR), lanes]
            dzb_ref[:, 2 * D + gi * GD:2 * D + (gi + 1) * GD] = (s - dmc[:, lanes]).astype(BF16)
        dzb_ref[:, 3 * D:5 * D] = dzg_ref[...]

        du = _dot_nt(dzb_ref[:, 0:wcols], w_ref[0])
        for s_i in range(1, ns):
            du = du + _dot_nt(dzb_ref[:, s_i * wcols:(s_i + 1) * wcols], w_ref[s_i])
        h0 = jnp.where(i == 0, head_ref[...], x_ref[...])
        r0 = lax.rsqrt(jnp.mean(h0 * h0, axis=-1, keepdims=True) + RMS_EPS)
        un = h0 * r0
        dgmix_ref[...] += jnp.sum(du * un, axis=0, keepdims=True)
        gd = du * g_ref[...]
        dh0 = dh1_ref[...] + r0 * (gd - un * jnp.mean(un * gd, axis=-1, keepdims=True))
        gx_ref[...] = dh0

        @pl.when(i == 0)
        def _():
            dhead_ref[...] = dh0

    any_spec = pl.BlockSpec(memory_space=pl.ANY)
    in_specs = (_halo_specs(nb) + _halo_specs(nb) + _halo_specs(nb, 2 * D, 2 * D)
                + [_row_spec(2 * D), _const_spec((32, D)), _const_spec((BR, D)), _x_spec(), _const_spec((1, D)),
                   _row_spec(), any_spec, any_spec])
    return pl.pallas_call(
        body, name="mixers_bwd_halo",
        grid=(nb,),
        in_specs=in_specs,
        out_specs=[_row_spec(D_IN), _x_spec(), _const_spec((BR, D)), _const_spec((32, D)), _const_spec((1, D))],
        out_shape=[jax.ShapeDtypeStruct((tp, D_IN), BF16), jax.ShapeDtypeStruct((seq, D), F32),
                   jax.ShapeDtypeStruct((BR, D), F32), jax.ShapeDtypeStruct((32, D), F32),
                   jax.ShapeDtypeStruct((1, D), F32)],
        scratch_shapes=[pltpu.VMEM((ns, D, wcols), BF16), pltpu.SemaphoreType.DMA((1,)),
                        pltpu.VMEM((EXT, D), F32), pltpu.VMEM((EXT, D), F32), pltpu.VMEM((EXT, D), F32),
                        pltpu.VMEM((BR, D), F32), pltpu.VMEM((8, ROT_ROWS, 128), F32), pltpu.VMEM((32, 8, D), F32)],
        compiler_params=_cparams(("arbitrary",)),
    )(dconv, dconv, dconv, dm, dm, dm, z, z, z, dzg, w_dw, head, x, g_mix, dh1, w_in_b, dep)


def _wgrad(a, c, tm, tn, tk, name, diag=False, col_major=False, dep=None):
    tp, m = a.shape
    n = c.shape[1]
    nk = tp // tk
    gm, gn = m // tm, n // tn

    def body(a_ref, c_ref, *rest):
        o_ref, ob_ref = rest[-2:]
        k = pl.program_id(2)

        @pl.when(k == 0)
        def _():
            o_ref[...] = jnp.zeros_like(o_ref)

        o_ref[...] += _dot_tn(a_ref[...], c_ref[...])

        @pl.when(k == nk - 1)
        def _():
            ob_ref[...] = o_ref[...].astype(BF16)

    c_map = lambda i, j, k: (k, j)
    grid = (gm, gn, nk)
    deps = [] if dep is None else [dep]
    if diag:
        grid = (gm, 1, nk)
        c_map = lambda i, j, k: (k, i)
        o_spec = pl.BlockSpec((tm, tn), lambda i, j, k: (i, 0))
        o_shape = (m, tn)
    elif col_major:
        o_spec = pl.BlockSpec((None, tm, tn), lambda i, j, k: (j, i, 0))
        o_shape = (gn, m, tn)
    else:
        o_spec = pl.BlockSpec((tm, tn), lambda i, j, k: (i, j))
        o_shape = (m, n)
    return pl.pallas_call(
        body, name=name,
        grid=grid,
        in_specs=[pl.BlockSpec((tk, tm), lambda i, j, k: (k, i)), pl.BlockSpec((tk, tn), c_map)]
        + [pl.BlockSpec(memory_space=pl.ANY)] * len(deps),
        out_specs=[o_spec, o_spec],
        out_shape=[jax.ShapeDtypeStruct(o_shape, F32), jax.ShapeDtypeStruct(o_shape, BF16)],
        compiler_params=_cparams(("arbitrary", "arbitrary", "arbitrary")),
    )(a, c, *deps)


def _place():
    x, y, c = lax.axis_index("x"), lax.axis_index("y"), lax.axis_index("c")
    others = [(1 - x, y), (x, 1 - y), (1 - x, 1 - y)]
    return x, y, c, others


def _split2(a, axis=0):
    return a.reshape(a.shape[:axis] + (2, a.shape[axis] // 2) + a.shape[axis + 1:])


def _merge2(a, axis=0):
    return a.reshape(a.shape[:axis] + (2 * a.shape[axis + 1],) + a.shape[axis + 2:])


def _cast_into_slot(shards, chip, dep, name):
    n = len(shards)
    r, c = shards[0].shape
    r2 = r // 2

    def body(chip_ref, *refs):
        for a in range(n):
            refs[n + 1 + a][...] = refs[a][...].astype(BF16)

    out = pl.pallas_call(
        body, name=name,
        grid_spec=pltpu.PrefetchScalarGridSpec(
            num_scalar_prefetch=1, grid=(2,),
            in_specs=[pl.BlockSpec((r2, c), lambda h, chip_ref: (h, 0))] * n + [pl.BlockSpec(memory_space=pl.ANY)],
            out_specs=[pl.BlockSpec((None, None, r2, c), lambda h, chip_ref: (chip_ref[0], h, 0, 0))] * n),
        out_shape=[jax.ShapeDtypeStruct((N_SHARD, 2, r2, c), BF16)] * n,
        compiler_params=_cparams(("arbitrary",)),
    )(chip, *shards, dep)
    return list(out)


GATHER_ID, SMALL_ID, TINY_ID = 1, 2, 6
SCATTER_IDS = {"ffn": 3, "mix": 4, "in": 5}
FLIPS = [(dx, dy, dc) for dx in (0, 1) for dy in (0, 1) for dc in (0, 1)][1:]
HBM_SPEC = pl.BlockSpec(memory_space=pltpu.HBM)
SEM_SPEC = pl.BlockSpec(memory_space=pltpu.SEMAPHORE)
DATAFLOW = pltpu.SideEffectType.DATAFLOW_SIDE_EFFECTING
TOKEN = jax.ShapeDtypeStruct((8, 128), F32)


def _in_hbm(a):
    return pltpu.with_memory_space_constraint(a, pltpu.HBM)


def _tiny_views(ins):
    return [lambda frm, to, c: ins[0]], [lambda frm, to, c: ins[1].at[frm]]


def _tiny_start(v, name):
    land = lax.empty((N_SHARD,) + v.shape, v.dtype)
    return _split_start(lambda ins: _tiny_views(ins)[0], lambda ins: _tiny_views(ins)[1], [v, land], 1, name, TINY_ID)


def _tiny_wait(send_sems, recv_sems, arrays, after, name):
    return _split_wait(lambda ins: _tiny_views(ins)[0], lambda ins: _tiny_views(ins)[1], send_sems, recv_sems,
                       arrays, after, name)[1]


def _ici_copies(srcs, dsts, send_sems, recv_sems, started):
    x, y, c, others = _place()
    mine = 2 * x + y
    copies = []
    for a in range(len(srcs)):
        for j, chip in enumerate(others):
            there = 2 * chip[0] + chip[1]
            src, dst = srcs[a](mine, there, c), dsts[a](mine, there, c)
            if not started:
                dst = dsts[a](there, mine, c)
            copies.append(pltpu.make_async_remote_copy(
                src_ref=src, dst_ref=dst, send_sem=send_sems.at[a * 3 + j], recv_sem=recv_sems.at[a * 3 + j],
                device_id=(*chip, c), device_id_type=MESH))
    return copies


def _same_core_peers():
    x, y, c, others = _place()
    return [(*chip, c) for chip in others]


def _all_peers():
    x, y, c, _ = _place()
    return [(jnp.bitwise_xor(x, dx), jnp.bitwise_xor(y, dy), jnp.bitwise_xor(c, dc)) for dx, dy, dc in FLIPS]


def _split_start(srcs_of, dsts_of, arrays, n_src, name, collective_id, peers_of=_same_core_peers,
                 copies_of=None, n_sems=None, dep=None):
    n = len(arrays)
    n_sems = n_sems or 3 * n_src
    copies_of = copies_of or (lambda ins, ss, rs, started: _ici_copies(srcs_of(ins), dsts_of(ins), ss, rs, started))
    deps = [] if dep is None else [dep]
    nd = len(deps)

    def body(*refs):
        ins = refs[:n]
        send_sems, recv_sems = refs[n + nd], refs[n + nd + 1]
        token = refs[2 * n + nd + 2]
        peers = peers_of()
        barrier = pltpu.get_barrier_semaphore()
        for peer in peers:
            pl.semaphore_signal(barrier, inc=1, device_id=peer, device_id_type=MESH)
        pl.semaphore_wait(barrier, len(peers))
        for cp in copies_of(ins, send_sems, recv_sems, True):
            cp.start()
        token[...] = jnp.zeros_like(token)

    out = pl.pallas_call(
        body, name=name,
        in_specs=[HBM_SPEC] * n + [pl.BlockSpec(memory_space=pl.ANY)] * nd,
        out_specs=(SEM_SPEC, SEM_SPEC, *([HBM_SPEC] * n), pl.BlockSpec(memory_space=pltpu.VMEM)),
        out_shape=(pltpu.SemaphoreType.DMA((n_sems,)), pltpu.SemaphoreType.DMA((n_sems,)),
                   *[pltpu.HBM(a.shape, a.dtype) for a in arrays], TOKEN),
        input_output_aliases={a: 2 + a for a in range(n)},
        compiler_params=pltpu.CompilerParams(has_side_effects=DATAFLOW, collective_id=collective_id),
    )(*[_in_hbm(a) for a in arrays], *deps)
    return out[0], out[1], list(out[2:2 + n]), out[2 + n]


def _split_wait(srcs_of, dsts_of, send_sems, recv_sems, arrays, after, name, copies_of=None):
    n = len(arrays)
    copies_of = copies_of or (lambda ins, ss, rs, started: _ici_copies(srcs_of(ins), dsts_of(ins), ss, rs, started))

    def body(*refs):
        ins = refs[:n]
        send_sems, recv_sems = refs[n], refs[n + 1]
        for cp in copies_of(ins, send_sems, recv_sems, False):
            cp.wait_send()
            cp.wait_recv()

    return pl.pallas_call(
        body, name=name,
        in_specs=[HBM_SPEC] * n + [SEM_SPEC, SEM_SPEC] + [pl.BlockSpec(memory_space=pl.ANY)] * len(after),
        out_specs=[HBM_SPEC] * n,
        out_shape=[pltpu.HBM(a.shape, a.dtype) for a in arrays],
        input_output_aliases={a: a for a in range(n)},
        compiler_params=pltpu.CompilerParams(has_side_effects=DATAFLOW),
    )(*arrays, send_sems, recv_sems, *after)


def _gather_views(ins):
    view = [lambda frm, to, c, r=r: r.at[frm, c] for r in ins]
    return view


def _gather_start(bufs, dep, name):
    return _split_start(_gather_views, _gather_views, bufs, len(bufs), name, GATHER_ID, dep=dep)


def _gather_wait(send_sems, recv_sems, bufs, after, name):
    return _split_wait(_gather_views, _gather_views, send_sems, recv_sems, bufs, after, name)


SIBLING_ONLY = pltpu.CompilerParams(collective_id=0)


def _sibling_handshake(x, y, c):
    barrier = pltpu.get_barrier_semaphore()
    pl.semaphore_signal(barrier, inc=1, device_id=(x, y, 1 - c), device_id_type=MESH)
    pl.semaphore_wait(barrier, 1)


def _forward_halves(bufs, name):
    n = len(bufs)
    any_spec = pl.BlockSpec(memory_space=pl.ANY)

    def body(*refs):
        outs = refs[n:2 * n]
        send_sems, recv_sems = refs[2 * n:]
        x, y, c, others = _place()
        _sibling_handshake(x, y, c)
        copies = []
        for a in range(n):
            for j, chip in enumerate(others):
                landed = outs[a].at[2 * chip[0] + chip[1], c]
                copies.append(pltpu.make_async_remote_copy(
                    src_ref=landed, dst_ref=landed, send_sem=send_sems.at[a * 3 + j], recv_sem=recv_sems.at[a * 3 + j],
                    device_id=(x, y, 1 - c), device_id_type=MESH))
        for cp in copies:
            cp.start()
        for a in range(n):
            for j, chip in enumerate(others):
                landed = outs[a].at[2 * chip[0] + chip[1], 1 - c]
                pltpu.make_async_remote_copy(
                    src_ref=landed, dst_ref=landed, send_sem=send_sems.at[a * 3 + j], recv_sem=recv_sems.at[a * 3 + j],
                    device_id=(x, y, c), device_id_type=MESH).wait_recv()
        for cp in copies:
            cp.wait_send()

    out = pl.pallas_call(
        body, name=name,
        in_specs=[any_spec] * n, out_specs=[any_spec] * n,
        out_shape=[jax.ShapeDtypeStruct(b.shape, b.dtype) for b in bufs],
        input_output_aliases={a: a for a in range(n)},
        scratch_shapes=[pltpu.SemaphoreType.DMA((3 * n,)), pltpu.SemaphoreType.DMA((3 * n,))],
        compiler_params=SIBLING_ONLY,
    )(*bufs)
    return [_merge2(o, 1) for o in out]


def _swap_halves_bf16(gbs, name):
    n = len(gbs)
    any_spec = pl.BlockSpec(memory_space=pl.ANY)

    def body(*refs):
        ins, outs = refs[:n], refs[n:2 * n]
        send_sems, recv_sems = refs[2 * n:]
        x, y, c, _ = _place()
        _sibling_handshake(x, y, c)
        copies = []
        for a in range(n):
            copies.append(pltpu.make_async_remote_copy(
                src_ref=ins[a].at[:, 1 - c], dst_ref=outs[a], send_sem=send_sems.at[a], recv_sem=recv_sems.at[a],
                device_id=(x, y, 1 - c), device_id_type=MESH))
        for cp in copies:
            cp.start()
        for cp in copies:
            cp.wait()

    return pl.pallas_call(
        body, name=name,
        in_specs=[any_spec] * n, out_specs=[any_spec] * n,
        out_shape=[jax.ShapeDtypeStruct((g.shape[0], g.shape[1] // 2, g.shape[2]), g.dtype) for g in gbs],
        scratch_shapes=[pltpu.SemaphoreType.DMA((n,)), pltpu.SemaphoreType.DMA((n,))],
        compiler_params=SIBLING_ONLY,
    )(*[_split2(g, 1) for g in gbs])


def _scatter_srcs(n):
    return lambda ins: [lambda frm, to, c, r=r: r.at[to] for r in ins[:n]]


def _scatter_dsts(n):
    return lambda ins: [lambda frm, to, c, r=r: r.at[frm] for r in ins[n:]]


def _scatter_start(hbs, name, collective_id):
    n = len(hbs)
    lands = [lax.empty(h.shape, h.dtype) for h in hbs]
    return _split_start(_scatter_srcs(n), _scatter_dsts(n), list(hbs) + lands, n, name, collective_id)


def _scatter_wait(send_sems, recv_sems, arrays, after, name):
    n = len(arrays) // 2
    return _split_wait(_scatter_srcs(n), _scatter_dsts(n), send_sems, recv_sems, arrays, after, name)[n:]


def _join_halves(rhs, name):
    n = len(rhs)
    any_spec = pl.BlockSpec(memory_space=pl.ANY)

    def body(*refs):
        outs = refs[n:2 * n]
        send_sems, recv_sems = refs[2 * n:]
        x, y, c, _ = _place()
        _sibling_handshake(x, y, c)
        copies = []
        for a in range(n):
            copies.append(pltpu.make_async_remote_copy(
                src_ref=outs[a].at[c], dst_ref=outs[a].at[c], send_sem=send_sems.at[a],
                recv_sem=recv_sems.at[a], device_id=(x, y, 1 - c), device_id_type=MESH))
        for cp in copies:
            cp.start()
        for a in range(n):
            landed = outs[a].at[1 - c]
            pltpu.make_async_remote_copy(
                src_ref=landed, dst_ref=landed, send_sem=send_sems.at[a], recv_sem=recv_sems.at[a],
                device_id=(x, y, c), device_id_type=MESH).wait_recv()
        for cp in copies:
            cp.wait_send()

    out = pl.pallas_call(
        body, name=name,
        in_specs=[any_spec] * n, out_specs=[any_spec] * n,
        out_shape=[jax.ShapeDtypeStruct(r.shape, r.dtype) for r in rhs],
        input_output_aliases={a: a for a in range(n)},
        scratch_shapes=[pltpu.SemaphoreType.DMA((n,)), pltpu.SemaphoreType.DMA((n,))],
        compiler_params=SIBLING_ONLY,
    )(*rhs)
    return [_merge2(o) for o in out]


def _peer_copies(ins, send_sems, recv_sems, started):
    x, y, c, _ = _place()
    copies = []
    for k, (dx, dy, dc) in enumerate(FLIPS):
        px, py, pc = jnp.bitwise_xor(x, dx), jnp.bitwise_xor(y, dy), jnp.bitwise_xor(c, dc)
        slot = 4 * x + 2 * y + c if started else 4 * px + 2 * py + pc
        copies.append(pltpu.make_async_remote_copy(
            src_ref=ins[0], dst_ref=ins[1].at[slot], send_sem=send_sems.at[k], recv_sem=recv_sems.at[k],
            device_id=(px, py, pc), device_id_type=MESH))
    return copies


def _small_start(v, name):
    land = lax.empty((8,) + v.shape, v.dtype)
    return _split_start(None, None, [v, land], 0, name, SMALL_ID, peers_of=_all_peers, copies_of=_peer_copies,
                        n_sems=len(FLIPS))


def _small_wait(send_sems, recv_sems, arrays, after, name):
    return _split_wait(None, None, send_sems, recv_sems, arrays, after, name, copies_of=_peer_copies)[1]


def _sum_slots(land, v, me):
    rows, cols = v.shape

    def body(me_ref, land_ref, v_ref, o_ref):
        o_ref[...] = jnp.zeros_like(o_ref)
        for d in range(8):
            @pl.when(me_ref[0] == d)
            def _():
                o_ref[...] += v_ref[...]

            @pl.when(me_ref[0] != d)
            def _():
                o_ref[...] += land_ref[d]

    return pl.pallas_call(
        body, name="sum_slots",
        grid_spec=pltpu.PrefetchScalarGridSpec(
            num_scalar_prefetch=1, grid=(1,),
            in_specs=[pl.BlockSpec((8, rows, cols), lambda i, me_ref: (0, 0, 0)),
                      pl.BlockSpec((rows, cols), lambda i, me_ref: (0, 0))],
            out_specs=pl.BlockSpec((rows, cols), lambda i, me_ref: (0, 0))),
        out_shape=jax.ShapeDtypeStruct((rows, cols), F32),
        compiler_params=_cparams(("arbitrary",)),
    )(me, land, v)


def _by_shape(arrays):
    groups = {}
    for k, a in enumerate(arrays):
        groups.setdefault(a.shape, []).append(k)
    return list(groups.values())


def _add_sibling_half(gs, sbs, idx, name):
    n = len(gs)
    ns, r, c = gs[0].shape
    r2 = r // 2

    def body(idx_ref, *refs):
        for a in range(n):
            g_ref, sb_ref, hown_ref, hb_ref = refs[a], refs[n + a], refs[2 * n + a], refs[3 * n + a]
            h = g_ref[...] + sb_ref[...].astype(F32)
            hb_ref[...] = h.astype(BF16)

            @pl.when(pl.program_id(0) == idx_ref[0])
            def _():
                hown_ref[...] = h

    spec = pl.BlockSpec((None, r2, c), lambda s, idx_ref: (s, 0, 0))
    out = pl.pallas_call(
        body, name=name,
        grid_spec=pltpu.PrefetchScalarGridSpec(
            num_scalar_prefetch=1, grid=(ns,),
            in_specs=[pl.BlockSpec((None, r2, c), lambda s, idx_ref: (s, idx_ref[4], 0))] * n + [spec] * n,
            out_specs=[pl.BlockSpec((r2, c), lambda s, idx_ref: (0, 0))] * n + [spec] * n),
        out_shape=[jax.ShapeDtypeStruct((r2, c), F32)] * n + [jax.ShapeDtypeStruct((ns, r2, c), BF16)] * n,
        compiler_params=_cparams(("arbitrary",)),
    )(idx, *gs, *sbs)
    return [(out[a], out[n + a]) for a in range(n)]


def _add_chip_slabs(hs, rbs, idx, name):
    n = len(hs)
    r2, c = hs[0].shape

    def body(idx_ref, *refs):
        for a in range(n):
            h_ref, r0_ref, r1_ref, r2_ref = refs[4 * a:4 * a + 4]
            refs[4 * n + a][...] = ((h_ref[...] + r0_ref[...].astype(F32)) + r1_ref[...].astype(F32)) + r2_ref[...].astype(F32)

    def pick(k):
        return pl.BlockSpec((None, r2, c), lambda i, idx_ref: (idx_ref[k], 0, 0))

    operands = []
    for h, rb in zip(hs, rbs):
        operands += [h, rb, rb, rb]
    out = pl.pallas_call(
        body, name=name,
        grid_spec=pltpu.PrefetchScalarGridSpec(
            num_scalar_prefetch=1, grid=(1,),
            in_specs=[pl.BlockSpec((r2, c), lambda i, idx_ref: (0, 0)), pick(1), pick(2), pick(3)] * n,
            out_specs=[pl.BlockSpec((None, r2, c), lambda i, idx_ref: (idx_ref[4], 0, 0))] * n),
        out_shape=[jax.ShapeDtypeStruct((2, r2, c), F32)] * n,
        compiler_params=_cparams(("arbitrary",)),
    )(idx, *operands)
    return list(out)


ELEMENTWISE_VMEM = 16 * 1024 * 1024


def _adamw(items, name):
    n = len(items)
    r, c = items[0][0].shape
    br = max(b for b in range(8, r + 1, 8) if r % b == 0 and n * 16 * b * c * 4 <= ELEMENTWISE_VMEM) if r % 8 == 0 else r

    def body(*refs):
        for a in range(n):
            g_ref, w_ref, m_ref, v_ref = refs[4 * a:4 * a + 4]
            go_ref, d_ref, nm_ref, nv_ref = refs[4 * n + 4 * a:4 * n + 4 * a + 4]
            gg = g_ref[...]
            go_ref[...] = gg
            nm = B1 * m_ref[...] + (1.0 - B1) * gg
            nv = B2 * v_ref[...] + (1.0 - B2) * jnp.square(gg)
            m_hat = nm / (1.0 - B1 ** STEP)
            v_hat = nv / (1.0 - B2 ** STEP)
            d_ref[...] = -LR * (m_hat / (jnp.sqrt(v_hat) + ADAM_EPS) + WD * w_ref[...])
            nm_ref[...] = nm
            nv_ref[...] = nv

    spec = pl.BlockSpec((br, c), lambda i: (i, 0))
    out = pl.pallas_call(
        body, name=name,
        grid=(r // br,),
        in_specs=[spec] * (4 * n), out_specs=[spec] * (4 * n),
        out_shape=[jax.ShapeDtypeStruct((r, c), F32)] * (4 * n),
        compiler_params=_cparams(("arbitrary",)),
    )(*[a for item in items for a in item])
    return [tuple(out[4 * a:4 * a + 4]) for a in range(n)]


BIG = ("w_in", "w_conv_out", "w_pool", "w_pool_out", "w_o", "w_ffn_gate", "w_ffn_up", "w_ffn_down")
REPL = ("g_mix", "b_gate", "b_dw", "ln_g", "ln_b", "pool_scale", "g_ffn", "g_final")
GROUP_MIX = ("w_conv_out", "w_pool", "w_pool_out", "w_o")
GROUP_FFN = ("w_ffn_gate", "w_ffn_up", "w_ffn_down")
TRANSPOSED = ("w_ffn_gate", "w_ffn_up")
WEIGHT_ORDER = ("meta_tokens", "g_mix", "w_in", "b_gate", "w_dw", "b_dw", "ln_g", "ln_b", "w_conv_out", "w_pool",
                "pool_scale", "w_pool_out", "w_o", "g_ffn", "w_ffn_gate", "w_ffn_up", "w_ffn_down", "g_final")


def _shard2d(name, a):
    a = a[0]
    if name == "w_pool":
        return a.reshape(4 * 64, GD)
    if name in TRANSPOSED:
        return a.T
    return a


def _unshard2d(name, a, shape):
    return a.T.reshape(shape) if name in TRANSPOSED else a.reshape(shape)


def _slabs_to_cols(a):
    ns, m, c = a.shape
    return a.transpose(1, 0, 2).reshape(m, ns * c)


def kernel(x, meta_tokens, g_mix, w_in, b_gate, w_dw, b_dw, ln_g, ln_b, w_conv_out, w_pool, pool_scale, w_pool_out, w_o, g_ffn, w_ffn_gate, w_ffn_up, w_ffn_down, g_final, loss_target, m_meta_tokens, m_g_mix, m_w_in, m_b_gate, m_w_dw, m_b_dw, m_ln_g, m_ln_b, m_w_conv_out, m_w_pool, m_pool_scale, m_w_pool_out, m_w_o, m_g_ffn, m_w_ffn_gate, m_w_ffn_up, m_w_ffn_down, m_g_final, v_meta_tokens, v_g_mix, v_w_in, v_b_gate, v_w_dw, v_b_dw, v_ln_g, v_ln_b, v_w_conv_out, v_w_pool, v_pool_scale, v_w_pool_out, v_w_o, v_g_ffn, v_w_ffn_gate, v_w_ffn_up, v_w_ffn_down, v_g_final):
    args = dict(locals())
    w = {n: args[n] for n in WEIGHT_ORDER}
    mom = {n: args["m_" + n] for n in WEIGHT_ORDER}
    var = {n: args["v_" + n] for n in WEIGHT_ORDER}
    seq = x.shape[1]
    nb = seq // BR + 1
    tp = nb * BR
    tk = tp // 2 if (tp // 2) % 16 == 0 else BR
    t_total = seq + N_META
    cx, cy, cc = lax.axis_index("x"), lax.axis_index("y"), lax.axis_index("c")
    chip = 2 * cx + cy
    chip1 = jnp.reshape(chip, (1,)).astype(jnp.int32)
    core = jnp.reshape(cc, (1,)).astype(jnp.int32)
    others = jnp.sort(jnp.stack([2 * (1 - cx) + cy, 2 * cx + (1 - cy), 2 * (1 - cx) + (1 - cy)]))
    idx = jnp.concatenate([chip1, others.astype(jnp.int32), core])
    xs, target = x[0], loss_target[0]

    tiny_own = jnp.concatenate([w["meta_tokens"], w["w_dw"][0], jnp.zeros((1, GD), F32)], axis=0)
    st_tiny = _tiny_start(tiny_own, "tiny_start")
    small = {n: w[n] for n in REPL if n != "g_final"}
    small["g_final"] = w["g_final"].reshape(1, D)

    def cast(group, dep):
        shards = [_shard2d(n, w[n]) for n in group]
        bufs = [None] * len(group)
        for ks in _by_shape(shards):
            done = _cast_into_slot([shards[k] for k in ks], chip1, dep, "cast_" + group[ks[0]])
            for k, b in zip(ks, done):
                bufs[k] = b
        return bufs

    def gather_finish(group, start, after, name):
        landed = _gather_wait(start[0], start[1], start[2], after, "gather_wait_" + name)
        return dict(zip(group, _forward_halves(landed, "forward_" + name)))

    st_in = _gather_start(cast(("w_in",), st_tiny[3]), None, "gather_start_in")
    tiny = _tiny_wait(st_tiny[0], st_tiny[1], st_tiny[2], [st_in[3]], "tiny_wait")
    tiny = lax.dynamic_update_slice(tiny, tiny_own[None], (chip, 0, 0))
    small["w_dw"] = _slabs_to_cols(tiny[:, N_META:])
    head = jnp.concatenate([jnp.zeros((PAD, D), F32), _slabs_to_cols(tiny[:, :N_META])], axis=0)
    bufs_mix, bufs_ffn = cast(GROUP_MIX, st_in[3]), cast(GROUP_FFN, st_in[3])
    u = _rms_u(head, xs, small["g_mix"] + st_in[3][0:1, 0:1], nb)
    z_own = _in_proj_own(u, w["w_in"][0], idx, nb)
    gw = gather_finish(("w_in",), st_in, [z_own] + bufs_mix + bufs_ffn, "in")
    st_mix = _gather_start(bufs_mix, gw["w_in"], "gather_start_mix")
    z = _in_proj_rest(u, gw["w_in"], z_own, idx, st_mix[3], nb)
    gw.update(gather_finish(GROUP_MIX, st_mix, [z], "mix"))
    st_ffn = _gather_start(bufs_ffn, gw["w_o"], "gather_start_ffn")
    w_pool_b = gw["w_pool"].reshape(N_SHARD, 4, 64, GD).transpose(1, 0, 2, 3).reshape(4, GD, GD)
    w_co_b, w_po_b, w_o_b = (gw[n].reshape(D, D) for n in ("w_conv_out", "w_pool_out", "w_o"))
    h1, yc, yp, mg, ca, cpre, m, mw, m2b = _mixers_fwd(
        z, head, xs, small["b_gate"] + st_ffn[3][0, 0], small["w_dw"], small["b_dw"], small["ln_g"], small["ln_b"],
        small["pool_scale"], w_co_b, w_pool_b, w_po_b, w_o_b, nb, t_total)
    gw.update(gather_finish(GROUP_FFN, st_ffn, [h1], "ffn"))

    dh1, dh1b, vb, fb, dgb, dub, dh2b, loss, dg_ffn, dg_final = _ffn_fwd_bwd(
        h1, target, small["g_ffn"], small["g_final"], gw["w_ffn_gate"].reshape(D_FF, D),
        gw["w_ffn_up"].reshape(D_FF, D), gw["w_ffn_down"].reshape(D_FF, D), nb)

    def slabs(name, g):
        if name == "w_in":
            return g
        if name == "w_pool":
            return g.reshape(4, N_SHARD, 64, GD).transpose(1, 0, 2, 3).reshape(N_SHARD, 4 * 64, GD)
        return g.reshape(N_SHARD, g.shape[0] // N_SHARD, g.shape[1])

    def reduce_start(group, grads, name):
        g32 = [slabs(n, grads[n][0]) for n in group]
        g16 = [slabs(n, grads[n][1]) for n in group]
        from_sibling = _swap_halves_bf16(g16, "swap_halves_" + name)
        halves = [None] * len(group)
        for ks in _by_shape(g32):
            done = _add_sibling_half([g32[k] for k in ks], [from_sibling[k] for k in ks], idx, "add_sibling_" + group[ks[0]])
            for k, pair in zip(ks, done):
                halves[k] = pair
        return [h for h, _ in halves], _scatter_start([hb for _, hb in halves], "scatter_start_" + name, SCATTER_IDS[name])

    def reduce_finish(group, halves, start, after, name):
        from_chips = _scatter_wait(start[0], start[1], start[2], after, "scatter_wait_" + name)
        reduced = [None] * len(group)
        for ks in _by_shape(halves):
            done = _add_chip_slabs([halves[k] for k in ks], [from_chips[k] for k in ks], idx, "add_chips_" + group[ks[0]])
            for k, r in zip(ks, done):
                reduced[k] = r
        return reduced

    half_ff = D_FF // 2
    grads_ffn = {
        "w_ffn_gate": _wgrad(dgb, vb, half_ff, D, tk, "wgrad_ffn_gate"),
        "w_ffn_up": _wgrad(dub, vb, half_ff, D, tk, "wgrad_ffn_up"),
        "w_ffn_down": _wgrad(fb, dh2b, half_ff, D, tk, "wgrad_ffn_down"),
    }
    halves_ffn, sc_ffn = reduce_start(GROUP_FFN, grads_ffn, "ffn")

    dycb, dypb, dzg, dconv, dmwb, dm, db_gate, dln_g, dln_b, db_dw, dps = _mixers_bwd_rows(
        dh1b, yc, yp, z, small["b_gate"], cpre, small["ln_g"], small["ln_b"], mw, small["pool_scale"],
        w_o_b, w_co_b, w_po_b, w_pool_b, sc_ffn[3], nb)
    grads_mix = {
        "w_conv_out": _wgrad(ca, dycb, D, D, tk, "wgrad_conv_out"),
        "w_pool": _wgrad(m, dmwb, GD, GD, tp, "wgrad_pool", diag=True),
        "w_pool_out": _wgrad(m2b, dypb, D, D, tk, "wgrad_pool_out"),
        "w_o": _wgrad(mg, dh1b, D, D, tk, "wgrad_o"),
    }
    halves_mix, sc_mix = reduce_start(GROUP_MIX, grads_mix, "mix")
    dzb, grad_x, dhead, dw_dw, dg_mix = _mixers_bwd_halo(
        dconv, dm, z, dzg, small["w_dw"], head, xs, small["g_mix"], dh1, gw["w_in"], sc_mix[3], nb, t_total)
    packed = jnp.concatenate(
        [dg_mix, db_gate.reshape(2, D), db_dw, dln_g, dln_b, dps, dg_ffn, dg_final,
         jnp.broadcast_to(loss, (1, D)), jnp.zeros((6, D), F32), dhead[PAD:], dw_dw], axis=0)
    sm = _small_start(packed, "small_start")
    grads_in = {"w_in": _wgrad(u, dzb, D, D_IN // N_SHARD, tk, "wgrad_in", col_major=True, dep=sm[3])}
    halves_in, sc_in = reduce_start(("w_in",), grads_in, "in")

    land = _small_wait(sm[0], sm[1], sm[2], [sc_in[3]], "small_wait")
    summed = _sum_slots(land, packed, jnp.reshape(4 * cx + 2 * cy + cc, (1,)).astype(jnp.int32))
    loss = summed[9, 0]

    first = GROUP_FFN + GROUP_MIX
    reduced_half = reduce_finish(GROUP_FFN, halves_ffn, sc_ffn, [summed], "ffn")
    reduced_half += reduce_finish(GROUP_MIX, halves_mix, sc_mix, [summed], "mix")
    reduced = dict(zip(first, _join_halves(reduced_half, "join_halves_first")))
    updates = {}
    for ks in _by_shape([reduced[n] for n in first]):
        names = [first[k] for k in ks]
        done = _adamw([(reduced[n], _shard2d(n, w[n]), _shard2d(n, mom[n]), _shard2d(n, var[n])) for n in names],
                      "adamw_" + names[0])
        updates.update(zip(names, done))

    def repl_stack(d):
        return jnp.concatenate([d["g_mix"], d["b_gate"].reshape(2, D), d["b_dw"], d["ln_g"], d["ln_b"],
                                d["pool_scale"], d["g_ffn"], d["g_final"].reshape(1, D), jnp.ones((7, D), F32)], axis=0)

    def shard_stack(d):
        return jnp.concatenate([d["meta_tokens"], d["w_dw"][0], jnp.ones((1, GD), F32)], axis=0)

    g_repl = summed[0:16]
    g_shard = lax.dynamic_slice_in_dim(summed[16:64], chip * GD, GD, axis=1)
    g_repl, d_repl, m_repl, v_repl = _adamw([(g_repl, repl_stack(w), repl_stack(mom), repl_stack(var))], "adamw_repl")[0]
    g_shard, d_shard, m_shard, v_shard = _adamw(
        [(g_shard, shard_stack(w), shard_stack(mom), shard_stack(var))], "adamw_cols")[0]

    done_first = [updates[n][1] for n in first] + [d_repl, d_shard]
    last_half = reduce_finish(("w_in",), halves_in, sc_in, done_first, "in")
    reduced["w_in"] = _join_halves(last_half, "join_halves_in")[0]
    updates["w_in"] = _adamw([(reduced["w_in"], w["w_in"][0], mom["w_in"][0], var["w_in"][0])], "adamw_w_in")[0]

    def unpack(name, repl, shard):
        if name == "meta_tokens":
            return shard[0:N_META]
        if name == "w_dw":
            return shard[N_META:N_META + KW].reshape(1, KW, GD)
        row = {"g_mix": 0, "b_gate": 1, "b_dw": 3, "ln_g": 4, "ln_b": 5, "pool_scale": 6, "g_ffn": 7, "g_final": 8}[name]
        if name == "b_gate":
            return repl[1:3].reshape(1, 2 * D)
        if name == "g_final":
            return repl[8]
        return repl[row:row + 1]

    out_g, out_d, out_m, out_v = {}, {}, {}, {}
    for n in WEIGHT_ORDER:
        if n in BIG:
            g, d_, m_, v_ = updates[n]
            shape = w[n].shape
            out_g[n], out_d[n], out_m[n], out_v[n] = (_unshard2d(n, a, shape) for a in (g, d_, m_, v_))
        else:
            out_g[n] = unpack(n, g_repl, g_shard)
            out_d[n] = unpack(n, d_repl, d_shard)
            out_m[n] = unpack(n, m_repl, m_shard)
            out_v[n] = unpack(n, v_repl, v_shard)
    return (loss, grad_x[None], *[out_g[n] for n in WEIGHT_ORDER], *[out_d[n] for n in WEIGHT_ORDER],
            *[out_m[n] for n in WEIGHT_ORDER], *[out_v[n] for n in WEIGHT_ORDER])
```

```python
import jax
import jax.numpy as jnp
from jax import lax
from jax.experimental import pallas as pl
from jax.experimental.pallas import tpu as pltpu

F32 = jnp.float32
BF16 = jnp.bfloat16
MESH = pl.DeviceIdType.MESH

D = 1024
N_META = 16
KW = 31
POOL_WINDOWS = (2, 4, 8, 16)
GD = 256
D_IN = 5 * D
D_FF = 2816
N_SHARD = 4
BR = 256
HALO = 16
PAD = BR - N_META
EXT = BR + 2 * HALO
RMS_EPS = 1e-6
LN_EPS = 1e-5
LR, B1, B2, ADAM_EPS, WD, STEP = 0.001, 0.9, 0.999, 1e-08, 0.01, 10
VMEM_LIMIT = 56 * 1024 * 1024


def _cparams(sem, vmem=VMEM_LIMIT):
    return pltpu.CompilerParams(dimension_semantics=sem, vmem_limit_bytes=vmem)


def _dot(a, b):
    return jnp.dot(a, b, preferred_element_type=F32)


def _dot_nt(a, b):
    return lax.dot_general(a, b, (((1,), (1,)), ((), ())), preferred_element_type=F32)


def _dot_tn(a, b):
    return lax.dot_general(a, b, (((0,), (0,)), ((), ())), preferred_element_type=F32)


def _sigmoid(x):
    return 0.5 * jnp.tanh(0.5 * x) + 0.5


def _row_ids(i, n, offset=0):
    return lax.broadcasted_iota(jnp.int32, (n, 1), 0) + (i * BR + offset - PAD)


def _pool_cnt(t, w, t_total):
    left = w // 2
    right = w - 1 - left
    lo = jnp.clip(t - left, 0, t_total)
    hi = jnp.clip(t + right + 1, 0, t_total)
    return jnp.maximum(hi - lo, 1).astype(F32)


def _halo_specs(nb, halo_width=D, width=D):
    last = nb * (BR // HALO) - 1
    return [
        pl.BlockSpec((HALO, halo_width), lambda i: (jnp.maximum(i * (BR // HALO) - 1, 0), 0)),
        pl.BlockSpec((BR, width), lambda i: (i, 0)),
        pl.BlockSpec((HALO, halo_width), lambda i: (jnp.minimum((i + 1) * (BR // HALO), last), 0)),
    ]


def _cols(ref, n):
    return [ref.at[:, k * D:(k + 1) * D] for k in range(n)]


def _fill_ext(ext_ref, prev, cur, nxt, i, nb):
    ext_ref[0:HALO, :] = jnp.where(i > 0, prev, 0.0)
    ext_ref[HALO:HALO + BR, :] = cur
    ext_ref[HALO + BR:EXT, :] = jnp.where(i < nb - 1, nxt, 0.0)


ROT_ROWS = EXT - 8


def _fill_rot(rot_ref, ext_ref, lanes):
    for r in range(1, 8):
        rot_ref[r] = ext_ref[pl.ds(r, ROT_ROWS), lanes]


def _tap(rot_ref, ext_ref, lanes, offset):
    q, r = divmod(offset, 8)
    if r == 0:
        return ext_ref[pl.ds(8 * q, BR), lanes]
    return rot_ref[r, pl.ds(8 * q, BR), :]


def _row_spec(width=D):
    return pl.BlockSpec((BR, width), lambda i: (i, 0))


def _x_spec():
    return pl.BlockSpec((BR, D), lambda i: (jnp.maximum(i - 1, 0), 0))


def _const_spec(shape):
    nd = len(shape)
    return pl.BlockSpec(shape, lambda i: (0,) * nd)


def _rms_u(head, x, g_mix, nb):
    def body(head_ref, x_ref, g_ref, u_ref):
        i = pl.program_id(0)
        h = jnp.where(i == 0, head_ref[...], x_ref[...])
        r = lax.rsqrt(jnp.mean(h * h, axis=-1, keepdims=True) + RMS_EPS)
        u_ref[...] = ((h * r) * g_ref[...]).astype(BF16)

    return pl.pallas_call(
        body, name="rms_u",
        grid=(nb,),
        in_specs=[_const_spec((BR, D)), _x_spec(), _const_spec((1, D))],
        out_specs=_row_spec(),
        out_shape=jax.ShapeDtypeStruct((nb * BR, D), BF16),
        compiler_params=_cparams(("arbitrary",)),
    )(head, x, g_mix)


def _in_proj_rows(tp):
    return tp // 2 if (tp // 2) % 16 == 0 else BR


def _in_proj_own(u, w_own, idx, nb):
    tp = nb * BR
    wcols = w_own.shape[1]
    rows = _in_proj_rows(tp)

    def body(idx_ref, u_ref, w_ref, z_ref, wb_ref):
        @pl.when(pl.program_id(0) == 0)
        def _():
            wb_ref[...] = w_ref[...].astype(BF16)

        z_ref[...] = _dot(u_ref[...], wb_ref[...])

    return pl.pallas_call(
        body, name="in_proj_own",
        grid_spec=pltpu.PrefetchScalarGridSpec(
            num_scalar_prefetch=1, grid=(tp // rows,),
            in_specs=[pl.BlockSpec((rows, D), lambda i, idx_ref: (i, 0)),
                      pl.BlockSpec((D, wcols), lambda i, idx_ref: (0, 0))],
            out_specs=pl.BlockSpec((rows, wcols), lambda i, idx_ref: (i, idx_ref[0])),
            scratch_shapes=[pltpu.VMEM((D, wcols), BF16)]),
        out_shape=jax.ShapeDtypeStruct((tp, N_SHARD * wcols), F32),
        compiler_params=_cparams(("arbitrary",)),
    )(idx, u, w_own)


def _in_proj_rest(u, w_in_b, z, idx, dep, nb):
    tp = nb * BR
    wcols = w_in_b.shape[2]
    rows = _in_proj_rows(tp)

    def body(idx_ref, u_ref, w_ref, z_in, dep_ref, z_ref):
        z_ref[...] = _dot(u_ref[...], w_ref[...])

    any_spec = pl.BlockSpec(memory_space=pl.ANY)
    return pl.pallas_call(
        body, name="in_proj_rest",
        grid_spec=pltpu.PrefetchScalarGridSpec(
            num_scalar_prefetch=1, grid=(N_SHARD - 1, tp // rows),
            in_specs=[pl.BlockSpec((rows, D), lambda s, i, idx_ref: (i, 0)),
                      pl.BlockSpec((None, D, wcols), lambda s, i, idx_ref: (idx_ref[1 + s], 0, 0)),
                      any_spec, any_spec],
            out_specs=pl.BlockSpec((rows, wcols), lambda s, i, idx_ref: (i, idx_ref[1 + s]))),
        out_shape=jax.ShapeDtypeStruct(z.shape, F32),
        input_output_aliases={3: 0},
        compiler_params=_cparams(("arbitrary", "arbitrary")),
    )(idx, u, w_in_b, z, dep)


def _mixers_fwd(z, head, x, b_gate, w_dw, b_dw, ln_g, ln_b, pool_scale, w_co, w_pool, w_po, w_o, nb, t_total):
    tp = nb * BR

    def body(z_prev, z_cur, z_next, head_ref, x_ref, bg_ref, wdw_ref, bdw_ref,
             lng_ref, lnb_ref, ps_ref, wco_ref, wpool_ref, wpo_ref, wo_ref,
             h1_ref, yc_ref, yp_ref, mg_ref, ca_ref, cpre_ref, m_ref, mw_ref, m2b_ref, ext_ref, pext_ref, rot_ref):
        i = pl.program_id(0)
        avp, agp, pp = _cols(z_prev, 3)
        av, ag, pc, za, zb = _cols(z_cur, 5)
        avn, agn, pn = _cols(z_next, 3)
        _fill_ext(ext_ref, avp[...] * _sigmoid(agp[...]), av[...] * _sigmoid(ag[...]),
                  avn[...] * _sigmoid(agn[...]), i, nb)
        _fill_ext(pext_ref, pp[...], pc[...], pn[...], i, nb)

        def conv_chunk(c, carry):
            lanes = pl.ds(pl.multiple_of(c * 128, 128), 128)
            _fill_rot(rot_ref, ext_ref, lanes)
            acc = jnp.broadcast_to(bdw_ref[:, lanes], (BR, 128))
            for k in range(KW):
                acc = acc + wdw_ref[k:k + 1, lanes] * _tap(rot_ref, ext_ref, lanes, 1 + k)
            cpre_ref[:, lanes] = acc
            return carry
        lax.fori_loop(0, D // 128, conv_chunk, 0)

        conv = cpre_ref[...]
        mu = jnp.mean(conv, axis=-1, keepdims=True)
        xc = conv - mu
        rstd = lax.rsqrt(jnp.mean(xc * xc, axis=-1, keepdims=True) + LN_EPS)
        ln = (xc * rstd) * lng_ref[...] + lnb_ref[...]
        cact = (ln * _sigmoid(ln)).astype(BF16)
        ca_ref[...] = cact
        y_conv = _dot(cact, wco_ref[...])
        yc_ref[...] = y_conv

        t = _row_ids(i, BR)
        for gi, w in enumerate(POOL_WINDOWS):
            left = w // 2
            right = w - 1 - left
            lanes = slice(gi * GD, (gi + 1) * GD)
            s = pext_ref[pl.ds(HALO - left, BR), lanes]
            for j in range(-left + 1, right + 1):
                s = s + pext_ref[pl.ds(HALO + j, BR), lanes]
            m = (s / _pool_cnt(t, w, t_total) - pext_ref[HALO:HALO + BR, lanes]).astype(BF16)
            m_ref[:, lanes] = m
            mw_ref[:, lanes] = _dot(m, wpool_ref[gi])
        mw = mw_ref[...]
        m2b = (mw * ps_ref[...]).astype(BF16)
        m2b_ref[...] = m2b
        y_pool = _dot(m2b, wpo_ref[...])
        yp_ref[...] = y_pool

        s_a = _sigmoid(za[...] + bg_ref[:, 0:D])
        s_b = _sigmoid(zb[...] + bg_ref[:, D:2 * D])
        merged = (s_a * y_conv + s_b * y_pool).astype(BF16)
        mg_ref[...] = merged
        h0 = jnp.where(i == 0, head_ref[...], x_ref[...])
        h1_ref[...] = h0 + _dot(merged, wo_ref[...])

    in_specs = (_halo_specs(nb, 3 * D, 5 * D)
                + [_const_spec((BR, D)), _x_spec(), _const_spec((1, 2 * D)), _const_spec((32, D)),
                   _const_spec((1, D)), _const_spec((1, D)), _const_spec((1, D)), _const_spec((1, D)),
                   _const_spec((D, D)), _const_spec((4, GD, GD)), _const_spec((D, D)), _const_spec((D, D))])
    outs = [(F32, "h1"), (F32, "yc"), (F32, "yp"), (BF16, "mg"), (BF16, "ca"), (F32, "cpre"), (BF16, "m"), (F32, "mw"),
            (BF16, "m2b")]
    return pl.pallas_call(
        body, name="mixers_fwd",
        grid=(nb,),
        in_specs=in_specs,
        out_specs=[_row_spec() for _ in outs],
        out_shape=[jax.ShapeDtypeStruct((tp, D), dt) for dt, _ in outs],
        scratch_shapes=[pltpu.VMEM((EXT, D), F32), pltpu.VMEM((EXT, D), F32), pltpu.VMEM((8, ROT_ROWS, 128), F32)],
        compiler_params=_cparams(("arbitrary",)),
    )(z, z, z, head, x, b_gate, w_dw, b_dw, ln_g, ln_b, pool_scale, w_co, w_pool, w_po, w_o)


def _ffn_fwd_bwd(h1, target, g_ffn, g_final, w_g, w_u, w_d, nb):
    tp = nb * BR

    def body(h1_ref, tgt_ref, gf_ref, gfin_ref, wg_hbm, wu_hbm, wd_hbm,
             dh1_ref, dh1b_ref, vb_ref, fb_ref, dgb_ref, dub_ref, dh2b_ref, loss_ref, dgf_ref, dgfin_ref,
             wg_ref, wu_ref, wd_ref, sem):
        i = pl.program_id(0)

        @pl.when(i == 0)
        def _():
            copies = [pltpu.make_async_copy(wg_hbm, wg_ref, sem.at[0]),
                      pltpu.make_async_copy(wu_hbm, wu_ref, sem.at[1]),
                      pltpu.make_async_copy(wd_hbm, wd_ref, sem.at[2])]
            for cp in copies:
                cp.start()
            loss_ref[...] = jnp.zeros_like(loss_ref)
            dgf_ref[...] = jnp.zeros_like(dgf_ref)
            dgfin_ref[...] = jnp.zeros_like(dgfin_ref)
            for cp in copies:
                cp.wait()

        h1 = h1_ref[...]
        r1 = lax.rsqrt(jnp.mean(h1 * h1, axis=-1, keepdims=True) + RMS_EPS)
        vn = h1 * r1
        vb = (vn * gf_ref[...]).astype(BF16)
        vb_ref[...] = vb
        g = _dot_nt(vb, wg_ref[...])
        up = _dot_nt(vb, wu_ref[...])
        sg = _sigmoid(g)
        sl = g * sg
        fb = (sl * up).astype(BF16)
        fb_ref[...] = fb
        h2 = h1 + _dot(fb, wd_ref[...])
        r2 = lax.rsqrt(jnp.mean(h2 * h2, axis=-1, keepdims=True) + RMS_EPS)
        yn = h2 * r2
        valid = i > 0
        diff = jnp.where(valid, yn * gfin_ref[...] - tgt_ref[...], 0.0)
        loss_ref[...] += 0.5 * jnp.sum(jnp.mean(diff * diff, axis=-1, keepdims=True))
        dy = diff * (1.0 / D)
        dgfin_ref[...] += jnp.sum(dy * yn, axis=0, keepdims=True)
        gd = dy * gfin_ref[...]
        dh2 = r2 * (gd - yn * jnp.mean(yn * gd, axis=-1, keepdims=True))
        dh2b = dh2.astype(BF16)
        dh2b_ref[...] = dh2b
        df = _dot_nt(dh2b, wd_ref[...])
        dub = (df * sl).astype(BF16)
        dgb = (df * up * (sg * (1.0 + g * (1.0 - sg)))).astype(BF16)
        dub_ref[...] = dub
        dgb_ref[...] = dgb
        dv = _dot(dgb, wg_ref[...]) + _dot(dub, wu_ref[...])
        dgf_ref[...] += jnp.sum(dv * vn, axis=0, keepdims=True)
        gd1 = dv * gf_ref[...]
        dh1 = dh2 + r1 * (gd1 - vn * jnp.mean(vn * gd1, axis=-1, keepdims=True))
        dh1_ref[...] = dh1
        dh1b_ref[...] = dh1.astype(BF16)

    any_spec = pl.BlockSpec(memory_space=pl.ANY)
    return pl.pallas_call(
        body, name="ffn_fwd_bwd",
        grid=(nb,),
        in_specs=[_row_spec(), _x_spec(), _const_spec((1, D)), _const_spec((1, D)), any_spec, any_spec, any_spec],
        out_specs=[_row_spec(), _row_spec(), _row_spec(), _row_spec(D_FF), _row_spec(D_FF), _row_spec(D_FF), _row_spec(),
                   _const_spec((1, 1)), _const_spec((1, D)), _const_spec((1, D))],
        out_shape=[jax.ShapeDtypeStruct((tp, D), F32), jax.ShapeDtypeStruct((tp, D), BF16),
                   jax.ShapeDtypeStruct((tp, D), BF16), jax.ShapeDtypeStruct((tp, D_FF), BF16),
                   jax.ShapeDtypeStruct((tp, D_FF), BF16), jax.ShapeDtypeStruct((tp, D_FF), BF16),
                   jax.ShapeDtypeStruct((tp, D), BF16), jax.ShapeDtypeStruct((1, 1), F32),
                   jax.ShapeDtypeStruct((1, D), F32), jax.ShapeDtypeStruct((1, D), F32)],
        scratch_shapes=[pltpu.VMEM((D_FF, D), BF16), pltpu.VMEM((D_FF, D), BF16), pltpu.VMEM((D_FF, D), BF16),
                        pltpu.SemaphoreType.DMA((3,))],
        compiler_params=_cparams(("arbitrary",)),
    )(h1, target, g_ffn, g_final, w_g, w_u, w_d)


def _mixers_bwd_rows(dh1b, yc, yp, z, b_gate, cpre, ln_g, ln_b, mw, pool_scale, w_o, w_co, w_po, w_pool, dep, nb):
    tp = nb * BR

    def body(dh1b_ref, yc_ref, yp_ref, za, zb, bg_ref, cpre_ref, lng_ref, lnb_ref, mw_ref, ps_ref,
             wo_ref, wco_ref, wpo_ref, wpool_ref, dep_ref,
             dycb_ref, dypb_ref, dzg_ref, dconv_ref, dmwb_ref, dm_ref, dbg_ref, dlng_ref, dlnb_ref, dbdw_ref, dps_ref):
        i = pl.program_id(0)

        @pl.when(i == 0)
        def _():
            for r in (dbg_ref, dlng_ref, dlnb_ref, dbdw_ref, dps_ref):
                r[...] = jnp.zeros_like(r)

        dmg = _dot_nt(dh1b_ref[...], wo_ref[...])
        s_a = _sigmoid(za[...] + bg_ref[:, 0:D])
        s_b = _sigmoid(zb[...] + bg_ref[:, D:2 * D])
        dycb = (dmg * s_a).astype(BF16)
        dypb = (dmg * s_b).astype(BF16)
        dycb_ref[...] = dycb
        dypb_ref[...] = dypb
        dza = dmg * yc_ref[...] * (s_a * (1.0 - s_a))
        dzb = dmg * yp_ref[...] * (s_b * (1.0 - s_b))
        dzg_ref[:, 0:D] = dza.astype(BF16)
        dzg_ref[:, D:2 * D] = dzb.astype(BF16)
        dbg_ref[:, 0:D] += jnp.sum(dza, axis=0, keepdims=True)
        dbg_ref[:, D:2 * D] += jnp.sum(dzb, axis=0, keepdims=True)

        dca = _dot_nt(dycb, wco_ref[...])
        conv = cpre_ref[...]
        mu = jnp.mean(conv, axis=-1, keepdims=True)
        xc = conv - mu
        rstd = lax.rsqrt(jnp.mean(xc * xc, axis=-1, keepdims=True) + LN_EPS)
        xhat = xc * rstd
        ln = xhat * lng_ref[...] + lnb_ref[...]
        sg = _sigmoid(ln)
        dln = dca * (sg * (1.0 + ln * (1.0 - sg)))
        dlng_ref[...] += jnp.sum(dln * xhat, axis=0, keepdims=True)
        dlnb_ref[...] += jnp.sum(dln, axis=0, keepdims=True)
        dxh = dln * lng_ref[...]
        dconv = rstd * (dxh - jnp.mean(dxh, axis=-1, keepdims=True)
                        - xhat * jnp.mean(dxh * xhat, axis=-1, keepdims=True))
        dconv_ref[...] = dconv
        dbdw_ref[...] += jnp.sum(dconv, axis=0, keepdims=True)

        dm2 = _dot_nt(dypb, wpo_ref[...])
        dps_ref[...] += jnp.sum(dm2 * mw_ref[...], axis=0, keepdims=True)
        dmwb = (dm2 * ps_ref[...]).astype(BF16)
        dmwb_ref[...] = dmwb
        for gi in range(len(POOL_WINDOWS)):
            lanes = slice(gi * GD, (gi + 1) * GD)
            dm_ref[:, lanes] = _dot_nt(dmwb[:, lanes], wpool_ref[gi])

    in_specs = [_row_spec(), _row_spec(), _row_spec(),
                pl.BlockSpec((BR, D), lambda i: (i, 3)), pl.BlockSpec((BR, D), lambda i: (i, 4)),
                _const_spec((1, 2 * D)), _row_spec(), _const_spec((1, D)), _const_spec((1, D)), _row_spec(),
                _const_spec((1, D)), _const_spec((D, D)), _const_spec((D, D)), _const_spec((D, D)),
                _const_spec((4, GD, GD)), pl.BlockSpec(memory_space=pl.ANY)]
    return pl.pallas_call(
        body, name="mixers_bwd_rows",
        grid=(nb,),
        in_specs=in_specs,
        out_specs=[_row_spec(), _row_spec(), _row_spec(2 * D), _row_spec(), _row_spec(), _row_spec(),
                   _const_spec((1, 2 * D)), _const_spec((1, D)), _const_spec((1, D)), _const_spec((1, D)),
                   _const_spec((1, D))],
        out_shape=[jax.ShapeDtypeStruct((tp, D), BF16), jax.ShapeDtypeStruct((tp, D), BF16),
                   jax.ShapeDtypeStruct((tp, 2 * D), BF16), jax.ShapeDtypeStruct((tp, D), F32),
                   jax.ShapeDtypeStruct((tp, D), BF16), jax.ShapeDtypeStruct((tp, D), F32),
                   jax.ShapeDtypeStruct((1, 2 * D), F32), jax.ShapeDtypeStruct((1, D), F32),
                   jax.ShapeDtypeStruct((1, D), F32), jax.ShapeDtypeStruct((1, D), F32),
                   jax.ShapeDtypeStruct((1, D), F32)],
        compiler_params=_cparams(("arbitrary",)),
    )(dh1b, yc, yp, z, z, b_gate, cpre, ln_g, ln_b, mw, pool_scale, w_o, w_co, w_po, w_pool, dep)


def _mixers_bwd_halo(dconv, dm, z, dzg, w_dw, head, x, g_mix, dh1, w_in_b, dep, nb, t_total):
    tp = nb * BR
    ns = w_in_b.shape[0]
    wcols = w_in_b.shape[2]
    seq = x.shape[0]

    def body(dcp, dcc, dcn, dmp, dmc, dmn, z_prev, z_cur, z_next, dzg_ref, wdw_ref, head_ref, x_ref, g_ref,
             dh1_ref, w_hbm, dep_ref,
             dzb_ref, gx_ref, dhead_ref, dwdw_ref, dgmix_ref,
             w_ref, sem, aext_ref, dext_ref, qext_ref, da_ref, rot_ref, dwp_ref):
        i = pl.program_id(0)
        (avp, agp), (av, ag), (avn, agn) = _cols(z_prev, 2), _cols(z_cur, 2), _cols(z_next, 2)

        @pl.when(i == 0)
        def _():
            cp = pltpu.make_async_copy(w_hbm, w_ref, sem.at[0])
            cp.start()
            dwp_ref[...] = jnp.zeros_like(dwp_ref)
            dgmix_ref[...] = jnp.zeros_like(dgmix_ref)
            cp.wait()

        sig_g = _sigmoid(ag[...])
        _fill_ext(aext_ref, avp[...] * _sigmoid(agp[...]), av[...] * sig_g, avn[...] * _sigmoid(agn[...]), i, nb)
        _fill_ext(dext_ref, dcp[...], dcc[...], dcn[...], i, nb)
        _fill_ext(qext_ref, dmp[...], dmc[...], dmn[...], i, nb)

        def conv_chunk(c, carry):
            lanes = pl.ds(pl.multiple_of(c * 128, 128), 128)
            _fill_rot(rot_ref, dext_ref, lanes)
            acc = jnp.zeros((BR, 128), F32)
            for k in range(KW):
                acc = acc + wdw_ref[k:k + 1, lanes] * _tap(rot_ref, dext_ref, lanes, KW - k)
            da_ref[:, lanes] = acc
            _fill_rot(rot_ref, aext_ref, lanes)
            dcv = dext_ref[HALO:HALO + BR, lanes]
            for k in range(KW):
                prod = _tap(rot_ref, aext_ref, lanes, 1 + k) * dcv
                dwp_ref[k, :, lanes] += jnp.sum(prod.reshape(BR // 8, 8, 128), axis=0)
            return carry
        lax.fori_loop(0, D // 128, conv_chunk, 0)

        @pl.when(i == nb - 1)
        def _():
            dwdw_ref[...] = jnp.sum(dwp_ref[...], axis=1)

        da = da_ref[...]
        a_val = av[...]
        dzb_ref[:, 0:D] = (da * sig_g).astype(BF16)
        dzb_ref[:, D:2 * D] = (da * a_val * (sig_g * (1.0 - sig_g))).astype(BF16)

        t_ext = _row_ids(i, EXT, -HALO)
        for gi, w in enumerate(POOL_WINDOWS):
            left = w // 2
            right = w - 1 - left
            lanes = slice(gi * GD, (gi + 1) * GD)
            qext_ref[:, lanes] = qext_ref[:, lanes] / _pool_cnt(t_ext, w, t_total)
            s = qext_ref[pl.ds(HALO - right, BR), lanes]
            for j in range(-right + 1, left + 1):
                s = s + qext_ref[pl.ds(HALO + j, BR), lanes]
            dzb_ref[:, 2 * D + gi * GD:2 * D + (gi + 1) * GD] = (s - dmc[:, lanes]).astype(BF16)
        dzb_ref[:, 3 * D:5 * D] = dzg_ref[...]

        du = _dot_nt(dzb_ref[:, 0:wcols], w_ref[0])
        for s_i in range(1, ns):
            du = du + _dot_nt(dzb_ref[:, s_i * wcols:(s_i + 1) * wcols], w_ref[s_i])
        h0 = jnp.where(i == 0, head_ref[...], x_ref[...])
        r0 = lax.rsqrt(jnp.mean(h0 * h0, axis=-1, keepdims=True) + RMS_EPS)
        un = h0 * r0
        dgmix_ref[...] += jnp.sum(du * un, axis=0, keepdims=True)
        gd = du * g_ref[...]
        dh0 = dh1_ref[...] + r0 * (gd - un * jnp.mean(un * gd, axis=-1, keepdims=True))
        gx_ref[...] = dh0

        @pl.when(i == 0)
        def _():
            dhead_ref[...] = dh0

    any_spec = pl.BlockSpec(memory_space=pl.ANY)
    in_specs = (_halo_specs(nb) + _halo_specs(nb) + _halo_specs(nb, 2 * D, 2 * D)
                + [_row_spec(2 * D), _const_spec((32, D)), _const_spec((BR, D)), _x_spec(), _const_spec((1, D)),
                   _row_spec(), any_spec, any_spec])
    return pl.pallas_call(
        body, name="mixers_bwd_halo",
        grid=(nb,),
        in_specs=in_specs,
        out_specs=[_row_spec(D_IN), _x_spec(), _const_spec((BR, D)), _const_spec((32, D)), _const_spec((1, D))],
        out_shape=[jax.ShapeDtypeStruct((tp, D_IN), BF16), jax.ShapeDtypeStruct((seq, D), F32),
                   jax.ShapeDtypeStruct((BR, D), F32), jax.ShapeDtypeStruct((32, D), F32),
                   jax.ShapeDtypeStruct((1, D), F32)],
        scratch_shapes=[pltpu.VMEM((ns, D, wcols), BF16), pltpu.SemaphoreType.DMA((1,)),
                        pltpu.VMEM((EXT, D), F32), pltpu.VMEM((EXT, D), F32), pltpu.VMEM((EXT, D), F32),
                        pltpu.VMEM((BR, D), F32), pltpu.VMEM((8, ROT_ROWS, 128), F32), pltpu.VMEM((32, 8, D), F32)],
        compiler_params=_cparams(("arbitrary",)),
    )(dconv, dconv, dconv, dm, dm, dm, z, z, z, dzg, w_dw, head, x, g_mix, dh1, w_in_b, dep)


def _wgrad(a, c, tm, tn, tk, name, diag=False, col_major=False, dep=None):
    tp, m = a.shape
    n = c.shape[1]
    nk = tp // tk
    gm, gn = m // tm, n // tn

    def body(a_ref, c_ref, *rest):
        o_ref, ob_ref = rest[-2:]
        k = pl.program_id(2)

        @pl.when(k == 0)
        def _():
            o_ref[...] = jnp.zeros_like(o_ref)

        o_ref[...] += _dot_tn(a_ref[...], c_ref[...])

        @pl.when(k == nk - 1)
        def _():
            ob_ref[...] = o_ref[...].astype(BF16)

    c_map = lambda i, j, k: (k, j)
    grid = (gm, gn, nk)
    deps = [] if dep is None else [dep]
    if diag:
        grid = (gm, 1, nk)
        c_map = lambda i, j, k: (k, i)
        o_spec = pl.BlockSpec((tm, tn), lambda i, j, k: (i, 0))
        o_shape = (m, tn)
    elif col_major:
        o_spec = pl.BlockSpec((None, tm, tn), lambda i, j, k: (j, i, 0))
        o_shape = (gn, m, tn)
    else:
        o_spec = pl.BlockSpec((tm, tn), lambda i, j, k: (i, j))
        o_shape = (m, n)
    return pl.pallas_call(
        body, name=name,
        grid=grid,
        in_specs=[pl.BlockSpec((tk, tm), lambda i, j, k: (k, i)), pl.BlockSpec((tk, tn), c_map)]
        + [pl.BlockSpec(memory_space=pl.ANY)] * len(deps),
        out_specs=[o_spec, o_spec],
        out_shape=[jax.ShapeDtypeStruct(o_shape, F32), jax.ShapeDtypeStruct(o_shape, BF16)],
        compiler_params=_cparams(("arbitrary", "arbitrary", "arbitrary")),
    )(a, c, *deps)


def _place():
    x, y, c = lax.axis_index("x"), lax.axis_index("y"), lax.axis_index("c")
    others = [(1 - x, y), (x, 1 - y), (1 - x, 1 - y)]
    return x, y, c, others


def _split2(a, axis=0):
    return a.reshape(a.shape[:axis] + (2, a.shape[axis] // 2) + a.shape[axis + 1:])


def _merge2(a, axis=0):
    return a.reshape(a.shape[:axis] + (2 * a.shape[axis + 1],) + a.shape[axis + 2:])


def _cast_into_slot(shards, chip, dep, name):
    n = len(shards)
    r, c = shards[0].shape
    r2 = r // 2

    def body(chip_ref, *refs):
        for a in range(n):
            refs[n + 1 + a][...] = refs[a][...].astype(BF16)

    out = pl.pallas_call(
        body, name=name,
        grid_spec=pltpu.PrefetchScalarGridSpec(
            num_scalar_prefetch=1, grid=(2,),
            in_specs=[pl.BlockSpec((r2, c), lambda h, chip_ref: (h, 0))] * n + [pl.BlockSpec(memory_space=pl.ANY)],
            out_specs=[pl.BlockSpec((None, None, r2, c), lambda h, chip_ref: (chip_ref[0], h, 0, 0))] * n),
        out_shape=[jax.ShapeDtypeStruct((N_SHARD, 2, r2, c), BF16)] * n,
        compiler_params=_cparams(("arbitrary",)),
    )(chip, *shards, dep)
    return list(out)


GATHER_ID, SMALL_ID, TINY_ID = 1, 2, 6
SCATTER_IDS = {"ffn": 3, "mix": 4, "in": 5}
FLIPS = [(dx, dy, dc) for dx in (0, 1) for dy in (0, 1) for dc in (0, 1)][1:]
HBM_SPEC = pl.BlockSpec(memory_space=pltpu.HBM)
SEM_SPEC = pl.BlockSpec(memory_space=pltpu.SEMAPHORE)
DATAFLOW = pltpu.SideEffectType.DATAFLOW_SIDE_EFFECTING
TOKEN = jax.ShapeDtypeStruct((8, 128), F32)


def _in_hbm(a):
    return pltpu.with_memory_space_constraint(a, pltpu.HBM)


def _tiny_views(ins):
    return [lambda frm, to, c: ins[0]], [lambda frm, to, c: ins[1].at[frm]]


def _tiny_start(v, name):
    land = lax.empty((N_SHARD,) + v.shape, v.dtype)
    return _split_start(lambda ins: _tiny_views(ins)[0], lambda ins: _tiny_views(ins)[1], [v, land], 1, name, TINY_ID)


def _tiny_wait(send_sems, recv_sems, arrays, after, name):
    return _split_wait(lambda ins: _tiny_views(ins)[0], lambda ins: _tiny_views(ins)[1], send_sems, recv_sems,
                       arrays, after, name)[1]


def _ici_copies(srcs, dsts, send_sems, recv_sems, started):
    x, y, c, others = _place()
    mine = 2 * x + y
    copies = []
    for a in range(len(srcs)):
        for j, chip in enumerate(others):
            there = 2 * chip[0] + chip[1]
            src, dst = srcs[a](mine, there, c), dsts[a](mine, there, c)
            if not started:
                dst = dsts[a](there, mine, c)
            copies.append(pltpu.make_async_remote_copy(
                src_ref=src, dst_ref=dst, send_sem=send_sems.at[a * 3 + j], recv_sem=recv_sems.at[a * 3 + j],
                device_id=(*chip, c), device_id_type=MESH))
    return copies


def _same_core_peers():
    x, y, c, others = _place()
    return [(*chip, c) for chip in others]


def _all_peers():
    x, y, c, _ = _place()
    return [(jnp.bitwise_xor(x, dx), jnp.bitwise_xor(y, dy), jnp.bitwise_xor(c, dc)) for dx, dy, dc in FLIPS]


def _split_start(srcs_of, dsts_of, arrays, n_src, name, collective_id, peers_of=_same_core_peers,
                 copies_of=None, n_sems=None, dep=None):
    n = len(arrays)
    n_sems = n_sems or 3 * n_src
    copies_of = copies_of or (lambda ins, ss, rs, started: _ici_copies(srcs_of(ins), dsts_of(ins), ss, rs, started))
    deps = [] if dep is None else [dep]
    nd = len(deps)

    def body(*refs):
        ins = refs[:n]
        send_sems, recv_sems = refs[n + nd], refs[n + nd + 1]
        token = refs[2 * n + nd + 2]
        peers = peers_of()
        barrier = pltpu.get_barrier_semaphore()
        for peer in peers:
            pl.semaphore_signal(barrier, inc=1, device_id=peer, device_id_type=MESH)
        pl.semaphore_wait(barrier, len(peers))
        for cp in copies_of(ins, send_sems, recv_sems, True):
            cp.start()
        token[...] = jnp.zeros_like(token)

    out = pl.pallas_call(
        body, name=name,
        in_specs=[HBM_SPEC] * n + [pl.BlockSpec(memory_space=pl.ANY)] * nd,
        out_specs=(SEM_SPEC, SEM_SPEC, *([HBM_SPEC] * n), pl.BlockSpec(memory_space=pltpu.VMEM)),
        out_shape=(pltpu.SemaphoreType.DMA((n_sems,)), pltpu.SemaphoreType.DMA((n_sems,)),
                   *[pltpu.HBM(a.shape, a.dtype) for a in arrays], TOKEN),
        input_output_aliases={a: 2 + a for a in range(n)},
        compiler_params=pltpu.CompilerParams(has_side_effects=DATAFLOW, collective_id=collective_id),
    )(*[_in_hbm(a) for a in arrays], *deps)
    return out[0], out[1], list(out[2:2 + n]), out[2 + n]


def _split_wait(srcs_of, dsts_of, send_sems, recv_sems, arrays, after, name, copies_of=None):
    n = len(arrays)
    copies_of = copies_of or (lambda ins, ss, rs, started: _ici_copies(srcs_of(ins), dsts_of(ins), ss, rs, started))

    def body(*refs):
        ins = refs[:n]
        send_sems, recv_sems = refs[n], refs[n + 1]
        for cp in copies_of(ins, send_sems, recv_sems, False):
            cp.wait_send()
            cp.wait_recv()

    return pl.pallas_call(
        body, name=name,
        in_specs=[HBM_SPEC] * n + [SEM_SPEC, SEM_SPEC] + [pl.BlockSpec(memory_space=pl.ANY)] * len(after),
        out_specs=[HBM_SPEC] * n,
        out_shape=[pltpu.HBM(a.shape, a.dtype) for a in arrays],
        input_output_aliases={a: a for a in range(n)},
        compiler_params=pltpu.CompilerParams(has_side_effects=DATAFLOW),
    )(*arrays, send_sems, recv_sems, *after)


def _gather_views(ins):
    view = [lambda frm, to, c, r=r: r.at[frm, c] for r in ins]
    return view


def _gather_start(bufs, dep, name):
    return _split_start(_gather_views, _gather_views, bufs, len(bufs), name, GATHER_ID, dep=dep)


def _gather_wait(send_sems, recv_sems, bufs, after, name):
    return _split_wait(_gather_views, _gather_views, send_sems, recv_sems, bufs, after, name)


SIBLING_ONLY = pltpu.CompilerParams(collective_id=0)


def _sibling_handshake(x, y, c):
    barrier = pltpu.get_barrier_semaphore()
    pl.semaphore_signal(barrier, inc=1, device_id=(x, y, 1 - c), device_id_type=MESH)
    pl.semaphore_wait(barrier, 1)


def _forward_halves(bufs, name):
    n = len(bufs)
    any_spec = pl.BlockSpec(memory_space=pl.ANY)

    def body(*refs):
        outs = refs[n:2 * n]
        send_sems, recv_sems = refs[2 * n:]
        x, y, c, others = _place()
        _sibling_handshake(x, y, c)
        copies = []
        for a in range(n):
            for j, chip in enumerate(others):
                landed = outs[a].at[2 * chip[0] + chip[1], c]
                copies.append(pltpu.make_async_remote_copy(
                    src_ref=landed, dst_ref=landed, send_sem=send_sems.at[a * 3 + j], recv_sem=recv_sems.at[a * 3 + j],
                    device_id=(x, y, 1 - c), device_id_type=MESH))
        for cp in copies:
            cp.start()
        for a in range(n):
            for j, chip in enumerate(others):
                landed = outs[a].at[2 * chip[0] + chip[1], 1 - c]
                pltpu.make_async_remote_copy(
                    src_ref=landed, dst_ref=landed, send_sem=send_sems.at[a * 3 + j], recv_sem=recv_sems.at[a * 3 + j],
                    device_id=(x, y, c), device_id_type=MESH).wait_recv()
        for cp in copies:
            cp.wait_send()

    out = pl.pallas_call(
        body, name=name,
        in_specs=[any_spec] * n, out_specs=[any_spec] * n,
        out_shape=[jax.ShapeDtypeStruct(b.shape, b.dtype) for b in bufs],
        input_output_aliases={a: a for a in range(n)},
        scratch_shapes=[pltpu.SemaphoreType.DMA((3 * n,)), pltpu.SemaphoreType.DMA((3 * n,))],
        compiler_params=SIBLING_ONLY,
    )(*bufs)
    return [_merge2(o, 1) for o in out]


def _swap_halves_bf16(gbs, name):
    n = len(gbs)
    any_spec = pl.BlockSpec(memory_space=pl.ANY)

    def body(*refs):
        ins, outs = refs[:n], refs[n:2 * n]
        send_sems, recv_sems = refs[2 * n:]
        x, y, c, _ = _place()
        _sibling_handshake(x, y, c)
        copies = []
        for a in range(n):
            copies.append(pltpu.make_async_remote_copy(
                src_ref=ins[a].at[:, 1 - c], dst_ref=outs[a], send_sem=send_sems.at[a], recv_sem=recv_sems.at[a],
                device_id=(x, y, 1 - c), device_id_type=MESH))
        for cp in copies:
            cp.start()
        for cp in copies:
            cp.wait()

    return pl.pallas_call(
        body, name=name,
        in_specs=[any_spec] * n, out_specs=[any_spec] * n,
        out_shape=[jax.ShapeDtypeStruct((g.shape[0], g.shape[1] // 2, g.shape[2]), g.dtype) for g in gbs],
        scratch_shapes=[pltpu.SemaphoreType.DMA((n,)), pltpu.SemaphoreType.DMA((n,))],
        compiler_params=SIBLING_ONLY,
    )(*[_split2(g, 1) for g in gbs])


def _scatter_srcs(n):
    return lambda ins: [lambda frm, to, c, r=r: r.at[to] for r in ins[:n]]


def _scatter_dsts(n):
    return lambda ins: [lambda frm, to, c, r=r: r.at[frm] for r in ins[n:]]


def _scatter_start(hbs, name, collective_id):
    n = len(hbs)
    lands = [lax.empty(h.shape, h.dtype) for h in hbs]
    return _split_start(_scatter_srcs(n), _scatter_dsts(n), list(hbs) + lands, n, name, collective_id)


def _scatter_wait(send_sems, recv_sems, arrays, after, name):
    n = len(arrays) // 2
    return _split_wait(_scatter_srcs(n), _scatter_dsts(n), send_sems, recv_sems, arrays, after, name)[n:]


def _join_halves(rhs, name):
    n = len(rhs)
    any_spec = pl.BlockSpec(memory_space=pl.ANY)

    def body(*refs):
        outs = refs[n:2 * n]
        send_sems, recv_sems = refs[2 * n:]
        x, y, c, _ = _place()
        _sibling_handshake(x, y, c)
        copies = []
        for a in range(n):
            copies.append(pltpu.make_async_remote_copy(
                src_ref=outs[a].at[c], dst_ref=outs[a].at[c], send_sem=send_sems.at[a],
                recv_sem=recv_sems.at[a], device_id=(x, y, 1 - c), device_id_type=MESH))
        for cp in copies:
            cp.start()
        for a in range(n):
            landed = outs[a].at[1 - c]
            pltpu.make_async_remote_copy(
                src_ref=landed, dst_ref=landed, send_sem=send_sems.at[a], recv_sem=recv_sems.at[a],
                device_id=(x, y, c), device_id_type=MESH).wait_recv()
        for cp in copies:
            cp.wait_send()

    out = pl.pallas_call(
        body, name=name,
        in_specs=[any_spec] * n, out_specs=[any_spec] * n,
        out_shape=[jax.ShapeDtypeStruct(r.shape, r.dtype) for r in rhs],
        input_output_aliases={a: a for a in range(n)},
        scratch_shapes=[pltpu.SemaphoreType.DMA((n,)), pltpu.SemaphoreType.DMA((n,))],
        compiler_params=SIBLING_ONLY,
    )(*rhs)
    return [_merge2(o) for o in out]


def _peer_copies(ins, send_sems, recv_sems, started):
    x, y, c, _ = _place()
    copies = []
    for k, (dx, dy, dc) in enumerate(FLIPS):
        px, py, pc = jnp.bitwise_xor(x, dx), jnp.bitwise_xor(y, dy), jnp.bitwise_xor(c, dc)
        slot = 4 * x + 2 * y + c if started else 4 * px + 2 * py + pc
        copies.append(pltpu.make_async_remote_copy(
            src_ref=ins[0], dst_ref=ins[1].at[slot], send_sem=send_sems.at[k], recv_sem=recv_sems.at[k],
            device_id=(px, py, pc), device_id_type=MESH))
    return copies


def _small_start(v, name):
    land = lax.empty((8,) + v.shape, v.dtype)
    return _split_start(None, None, [v, land], 0, name, SMALL_ID, peers_of=_all_peers, copies_of=_peer_copies,
                        n_sems=len(FLIPS))


def _small_wait(send_sems, recv_sems, arrays, after, name):
    return _split_wait(None, None, send_sems, recv_sems, arrays, after, name, copies_of=_peer_copies)[1]


def _sum_slots(land, v, me):
    rows, cols = v.shape

    def body(me_ref, land_ref, v_ref, o_ref):
        o_ref[...] = jnp.zeros_like(o_ref)
        for d in range(8):
            @pl.when(me_ref[0] == d)
            def _():
                o_ref[...] += v_ref[...]

            @pl.when(me_ref[0] != d)
            def _():
                o_ref[...] += land_ref[d]

    return pl.pallas_call(
        body, name="sum_slots",
        grid_spec=pltpu.PrefetchScalarGridSpec(
            num_scalar_prefetch=1, grid=(1,),
            in_specs=[pl.BlockSpec((8, rows, cols), lambda i, me_ref: (0, 0, 0)),
                      pl.BlockSpec((rows, cols), lambda i, me_ref: (0, 0))],
            out_specs=pl.BlockSpec((rows, cols), lambda i, me_ref: (0, 0))),
        out_shape=jax.ShapeDtypeStruct((rows, cols), F32),
        compiler_params=_cparams(("arbitrary",)),
    )(me, land, v)


def _by_shape(arrays):
    groups = {}
    for k, a in enumerate(arrays):
        groups.setdefault(a.shape, []).append(k)
    return list(groups.values())


def _add_sibling_half(gs, sbs, idx, name):
    n = len(gs)
    ns, r, c = gs[0].shape
    r2 = r // 2

    def body(idx_ref, *refs):
        for a in range(n):
            g_ref, sb_ref, hown_ref, hb_ref = refs[a], refs[n + a], refs[2 * n + a], refs[3 * n + a]
            h = g_ref[...] + sb_ref[...].astype(F32)
            hb_ref[...] = h.astype(BF16)

            @pl.when(pl.program_id(0) == idx_ref[0])
            def _():
                hown_ref[...] = h

    spec = pl.BlockSpec((None, r2, c), lambda s, idx_ref: (s, 0, 0))
    out = pl.pallas_call(
        body, name=name,
        grid_spec=pltpu.PrefetchScalarGridSpec(
            num_scalar_prefetch=1, grid=(ns,),
            in_specs=[pl.BlockSpec((None, r2, c), lambda s, idx_ref: (s, idx_ref[4], 0))] * n + [spec] * n,
            out_specs=[pl.BlockSpec((r2, c), lambda s, idx_ref: (0, 0))] * n + [spec] * n),
        out_shape=[jax.ShapeDtypeStruct((r2, c), F32)] * n + [jax.ShapeDtypeStruct((ns, r2, c), BF16)] * n,
        compiler_params=_cparams(("arbitrary",)),
    )(idx, *gs, *sbs)
    return [(out[a], out[n + a]) for a in range(n)]


def _add_chip_slabs(hs, rbs, idx, name):
    n = len(hs)
    r2, c = hs[0].shape

    def body(idx_ref, *refs):
        for a in range(n):
            h_ref, r0_ref, r1_ref, r2_ref = refs[4 * a:4 * a + 4]
            refs[4 * n + a][...] = ((h_ref[...] + r0_ref[...].astype(F32)) + r1_ref[...].astype(F32)) + r2_ref[...].astype(F32)

    def pick(k):
        return pl.BlockSpec((None, r2, c), lambda i, idx_ref: (idx_ref[k], 0, 0))

    operands = []
    for h, rb in zip(hs, rbs):
        operands += [h, rb, rb, rb]
    out = pl.pallas_call(
        body, name=name,
        grid_spec=pltpu.PrefetchScalarGridSpec(
            num_scalar_prefetch=1, grid=(1,),
            in_specs=[pl.BlockSpec((r2, c), lambda i, idx_ref: (0, 0)), pick(1), pick(2), pick(3)] * n,
            out_specs=[pl.BlockSpec((None, r2, c), lambda i, idx_ref: (idx_ref[4], 0, 0))] * n),
        out_shape=[jax.ShapeDtypeStruct((2, r2, c), F32)] * n,
        compiler_params=_cparams(("arbitrary",)),
    )(idx, *operands)
    return list(out)


ELEMENTWISE_VMEM = 16 * 1024 * 1024


def _adamw(items, name):
    n = len(items)
    r, c = items[0][0].shape
    br = max(b for b in range(8, r + 1, 8) if r % b == 0 and n * 16 * b * c * 4 <= ELEMENTWISE_VMEM) if r % 8 == 0 else r

    def body(*refs):
        for a in range(n):
            g_ref, w_ref, m_ref, v_ref = refs[4 * a:4 * a + 4]
            go_ref, d_ref, nm_ref, nv_ref = refs[4 * n + 4 * a:4 * n + 4 * a + 4]
            gg = g_ref[...]
            go_ref[...] = gg
            nm = B1 * m_ref[...] + (1.0 - B1) * gg
            nv = B2 * v_ref[...] + (1.0 - B2) * jnp.square(gg)
            m_hat = nm / (1.0 - B1 ** STEP)
            v_hat = nv / (1.0 - B2 ** STEP)
            d_ref[...] = -LR * (m_hat / (jnp.sqrt(v_hat) + ADAM_EPS) + WD * w_ref[...])
            nm_ref[...] = nm
            nv_ref[...] = nv

    spec = pl.BlockSpec((br, c), lambda i: (i, 0))
    out = pl.pallas_call(
        body, name=name,
        grid=(r // br,),
        in_specs=[spec] * (4 * n), out_specs=[spec] * (4 * n),
        out_shape=[jax.ShapeDtypeStruct((r, c), F32)] * (4 * n),
        compiler_params=_cparams(("arbitrary",)),
    )(*[a for item in items for a in item])
    return [tuple(out[4 * a:4 * a + 4]) for a in range(n)]


BIG = ("w_in", "w_conv_out", "w_pool", "w_pool_out", "w_o", "w_ffn_gate", "w_ffn_up", "w_ffn_down")
REPL = ("g_mix", "b_gate", "b_dw", "ln_g", "ln_b", "pool_scale", "g_ffn", "g_final")
GROUP_MIX = ("w_conv_out", "w_pool", "w_pool_out", "w_o")
GROUP_FFN = ("w_ffn_gate", "w_ffn_up", "w_ffn_down")
TRANSPOSED = ("w_ffn_gate", "w_ffn_up")
WEIGHT_ORDER = ("meta_tokens", "g_mix", "w_in", "b_gate", "w_dw", "b_dw", "ln_g", "ln_b", "w_conv_out", "w_pool",
                "pool_scale", "w_pool_out", "w_o", "g_ffn", "w_ffn_gate", "w_ffn_up", "w_ffn_down", "g_final")


def _shard2d(name, a):
    a = a[0]
    if name == "w_pool":
        return a.reshape(4 * 64, GD)
    if name in TRANSPOSED:
        return a.T
    return a


def _unshard2d(name, a, shape):
    return a.T.reshape(shape) if name in TRANSPOSED else a.reshape(shape)


def _slabs_to_cols(a):
    ns, m, c = a.shape
    return a.transpose(1, 0, 2).reshape(m, ns * c)


def kernel(x, meta_tokens, g_mix, w_in, b_gate, w_dw, b_dw, ln_g, ln_b, w_conv_out, w_pool, pool_scale, w_pool_out, w_o, g_ffn, w_ffn_gate, w_ffn_up, w_ffn_down, g_final, loss_target, m_meta_tokens, m_g_mix, m_w_in, m_b_gate, m_w_dw, m_b_dw, m_ln_g, m_ln_b, m_w_conv_out, m_w_pool, m_pool_scale, m_w_pool_out, m_w_o, m_g_ffn, m_w_ffn_gate, m_w_ffn_up, m_w_ffn_down, m_g_final, v_meta_tokens, v_g_mix, v_w_in, v_b_gate, v_w_dw, v_b_dw, v_ln_g, v_ln_b, v_w_conv_out, v_w_pool, v_pool_scale, v_w_pool_out, v_w_o, v_g_ffn, v_w_ffn_gate, v_w_ffn_up, v_w_ffn_down, v_g_final):
    args = dict(locals())
    w = {n: args[n] for n in WEIGHT_ORDER}
    mom = {n: args["m_" + n] for n in WEIGHT_ORDER}
    var = {n: args["v_" + n] for n in WEIGHT_ORDER}
    seq = x.shape[1]
    nb = seq // BR + 1
    tp = nb * BR
    tk = tp // 2 if (tp // 2) % 16 == 0 else BR
    t_total = seq + N_META
    cx, cy, cc = lax.axis_index("x"), lax.axis_index("y"), lax.axis_index("c")
    chip = 2 * cx + cy
    chip1 = jnp.reshape(chip, (1,)).astype(jnp.int32)
    core = jnp.reshape(cc, (1,)).astype(jnp.int32)
    others = jnp.sort(jnp.stack([2 * (1 - cx) + cy, 2 * cx + (1 - cy), 2 * (1 - cx) + (1 - cy)]))
    idx = jnp.concatenate([chip1, others.astype(jnp.int32), core])
    xs, target = x[0], loss_target[0]

    tiny_own = jnp.concatenate([w["meta_tokens"], w["w_dw"][0], jnp.zeros((1, GD), F32)], axis=0)
    st_tiny = _tiny_start(tiny_own, "tiny_start")
    small = {n: w[n] for n in REPL if n != "g_final"}
    small["g_final"] = w["g_final"].reshape(1, D)

    def cast(group, dep):
        shards = [_shard2d(n, w[n]) for n in group]
        bufs = [None] * len(group)
        for ks in _by_shape(shards):
            done = _cast_into_slot([shards[k] for k in ks], chip1, dep, "cast_" + group[ks[0]])
            for k, b in zip(ks, done):
                bufs[k] = b
        return bufs

    def gather_finish(group, start, after, name):
        landed = _gather_wait(start[0], start[1], start[2], after, "gather_wait_" + name)
        return dict(zip(group, _forward_halves(landed, "forward_" + name)))

    st_in = _gather_start(cast(("w_in",), st_tiny[3]), None, "gather_start_in")
    tiny = _tiny_wait(st_tiny[0], st_tiny[1], st_tiny[2], [st_in[3]], "tiny_wait")
    tiny = lax.dynamic_update_slice(tiny, tiny_own[None], (chip, 0, 0))
    small["w_dw"] = _slabs_to_cols(tiny[:, N_META:])
    head = jnp.concatenate([jnp.zeros((PAD, D), F32), _slabs_to_cols(tiny[:, :N_META])], axis=0)
    bufs_mix, bufs_ffn = cast(GROUP_MIX, st_in[3]), cast(GROUP_FFN, st_in[3])
    u = _rms_u(head, xs, small["g_mix"] + st_in[3][0:1, 0:1], nb)
    z_own = _in_proj_own(u, w["w_in"][0], idx, nb)
    gw = gather_finish(("w_in",), st_in, [z_own] + bufs_mix + bufs_ffn, "in")
    st_mix = _gather_start(bufs_mix, gw["w_in"], "gather_start_mix")
    z = _in_proj_rest(u, gw["w_in"], z_own, idx, st_mix[3], nb)
    gw.update(gather_finish(GROUP_MIX, st_mix, [z], "mix"))
    st_ffn = _gather_start(bufs_ffn, gw["w_o"], "gather_start_ffn")
    w_pool_b = gw["w_pool"].reshape(N_SHARD, 4, 64, GD).transpose(1, 0, 2, 3).reshape(4, GD, GD)
    w_co_b, w_po_b, w_o_b = (gw[n].reshape(D, D) for n in ("w_conv_out", "w_pool_out", "w_o"))
    h1, yc, yp, mg, ca, cpre, m, mw, m2b = _mixers_fwd(
        z, head, xs, small["b_gate"] + st_ffn[3][0, 0], small["w_dw"], small["b_dw"], small["ln_g"], small["ln_b"],
        small["pool_scale"], w_co_b, w_pool_b, w_po_b, w_o_b, nb, t_total)
    gw.update(gather_finish(GROUP_FFN, st_ffn, [h1], "ffn"))

    dh1, dh1b, vb, fb, dgb, dub, dh2b, loss, dg_ffn, dg_final = _ffn_fwd_bwd(
        h1, target, small["g_ffn"], small["g_final"], gw["w_ffn_gate"].reshape(D_FF, D),
        gw["w_ffn_up"].reshape(D_FF, D), gw["w_ffn_down"].reshape(D_FF, D), nb)

    def slabs(name, g):
        if name == "w_in":
            return g
        if name == "w_pool":
            return g.reshape(4, N_SHARD, 64, GD).transpose(1, 0, 2, 3).reshape(N_SHARD, 4 * 64, GD)
        return g.reshape(N_SHARD, g.shape[0] // N_SHARD, g.shape[1])

    def reduce_start(group, grads, name):
        g32 = [slabs(n, grads[n][0]) for n in group]
        g16 = [slabs(n, grads[n][1]) for n in group]
        from_sibling = _swap_halves_bf16(g16, "swap_halves_" + name)
        halves = [None] * len(group)
        for ks in _by_shape(g32):
            done = _add_sibling_half([g32[k] for k in ks], [from_sibling[k] for k in ks], idx, "add_sibling_" + group[ks[0]])
            for k, pair in zip(ks, done):
                halves[k] = pair
        return [h for h, _ in halves], _scatter_start([hb for _, hb in halves], "scatter_start_" + name, SCATTER_IDS[name])

    def reduce_finish(group, halves, start, after, name):
        from_chips = _scatter_wait(start[0], start[1], start[2], after, "scatter_wait_" + name)
        reduced = [None] * len(group)
        for ks in _by_shape(halves):
            done = _add_chip_slabs([halves[k] for k in ks], [from_chips[k] for k in ks], idx, "add_chips_" + group[ks[0]])
            for k, r in zip(ks, done):
                reduced[k] = r
        return reduced

    half_ff = D_FF // 2
    grads_ffn = {
        "w_ffn_gate": _wgrad(dgb, vb, half_ff, D, tk, "wgrad_ffn_gate"),
        "w_ffn_up": _wgrad(dub, vb, half_ff, D, tk, "wgrad_ffn_up"),
        "w_ffn_down": _wgrad(fb, dh2b, half_ff, D, tk, "wgrad_ffn_down"),
    }
    halves_ffn, sc_ffn = reduce_start(GROUP_FFN, grads_ffn, "ffn")

    dycb, dypb, dzg, dconv, dmwb, dm, db_gate, dln_g, dln_b, db_dw, dps = _mixers_bwd_rows(
        dh1b, yc, yp, z, small["b_gate"], cpre, small["ln_g"], small["ln_b"], mw, small["pool_scale"],
        w_o_b, w_co_b, w_po_b, w_pool_b, sc_ffn[3], nb)
    grads_mix = {
        "w_conv_out": _wgrad(ca, dycb, D, D, tk, "wgrad_conv_out"),
        "w_pool": _wgrad(m, dmwb, GD, GD, tp, "wgrad_pool", diag=True),
        "w_pool_out": _wgrad(m2b, dypb, D, D, tk, "wgrad_pool_out"),
        "w_o": _wgrad(mg, dh1b, D, D, tk, "wgrad_o"),
    }
    halves_mix, sc_mix = reduce_start(GROUP_MIX, grads_mix, "mix")
    dzb, grad_x, dhead, dw_dw, dg_mix = _mixers_bwd_halo(
        dconv, dm, z, dzg, small["w_dw"], head, xs, small["g_mix"], dh1, gw["w_in"], sc_mix[3], nb, t_total)
    packed = jnp.concatenate(
        [dg_mix, db_gate.reshape(2, D), db_dw, dln_g, dln_b, dps, dg_ffn, dg_final,
         jnp.broadcast_to(loss, (1, D)), jnp.zeros((6, D), F32), dhead[PAD:], dw_dw], axis=0)
    sm = _small_start(packed, "small_start")
    grads_in = {"w_in": _wgrad(u, dzb, D // 2, D_IN // N_SHARD, tk, "wgrad_in", col_major=True, dep=sm[3])}
    halves_in, sc_in = reduce_start(("w_in",), grads_in, "in")

    land = _small_wait(sm[0], sm[1], sm[2], [sc_in[3]], "small_wait")
    summed = _sum_slots(land, packed, jnp.reshape(4 * cx + 2 * cy + cc, (1,)).astype(jnp.int32))
    loss = summed[9, 0]

    first = GROUP_FFN + GROUP_MIX
    reduced_half = reduce_finish(GROUP_FFN, halves_ffn, sc_ffn, [summed], "ffn")
    reduced_half += reduce_finish(GROUP_MIX, halves_mix, sc_mix, [summed], "mix")
    reduced = dict(zip(first, _join_halves(reduced_half, "join_halves_first")))
    updates = {}
    for ks in _by_shape([reduced[n] for n in first]):
        names = [first[k] for k in ks]
        done = _adamw([(reduced[n], _shard2d(n, w[n]), _shard2d(n, mom[n]), _shard2d(n, var[n])) for n in names],
                      "adamw_" + names[0])
        updates.update(zip(names, done))

    def repl_stack(d):
        return jnp.concatenate([d["g_mix"], d["b_gate"].reshape(2, D), d["b_dw"], d["ln_g"], d["ln_b"],
                                d["pool_scale"], d["g_ffn"], d["g_final"].reshape(1, D), jnp.ones((7, D), F32)], axis=0)

    def shard_stack(d):
        return jnp.concatenate([d["meta_tokens"], d["w_dw"][0], jnp.ones((1, GD), F32)], axis=0)

    g_repl = summed[0:16]
    g_shard = lax.dynamic_slice_in_dim(summed[16:64], chip * GD, GD, axis=1)
    g_repl, d_repl, m_repl, v_repl = _adamw([(g_repl, repl_stack(w), repl_stack(mom), repl_stack(var))], "adamw_repl")[0]
    g_shard, d_shard, m_shard, v_shard = _adamw(
        [(g_shard, shard_stack(w), shard_stack(mom), shard_stack(var))], "adamw_cols")[0]

    done_first = [updates[n][1] for n in first] + [d_repl, d_shard]
    last_half = reduce_finish(("w_in",), halves_in, sc_in, done_first, "in")
    reduced["w_in"] = _join_halves(last_half, "join_halves_in")[0]
    updates["w_in"] = _adamw([(reduced["w_in"], w["w_in"][0], mom["w_in"][0], var["w_in"][0])], "adamw_w_in")[0]

    def unpack(name, repl, shard):
        if name == "meta_tokens":
            return shard[0:N_META]
        if name == "w_dw":
            return shard[N_META:N_META + KW].reshape(1, KW, GD)
        row = {"g_mix": 0, "b_gate": 1, "b_dw": 3, "ln_g": 4, "ln_b": 5, "pool_scale": 6, "g_ffn": 7, "g_final": 8}[name]
        if name == "b_gate":
            return repl[1:3].reshape(1, 2 * D)
        if name == "g_final":
            return repl[8]
        return repl[row:row + 1]

    out_g, out_d, out_m, out_v = {}, {}, {}, {}
    for n in WEIGHT_ORDER:
        if n in BIG:
            g, d_, m_, v_ = updates[n]
            shape = w[n].shape
            out_g[n], out_d[n], out_m[n], out_v[n] = (_unshard2d(n, a, shape) for a in (g, d_, m_, v_))
        else:
            out_g[n] = unpack(n, g_repl, g_shard)
            out_d[n] = unpack(n, d_repl, d_shard)
            out_m[n] = unpack(n, m_repl, m_shard)
            out_v[n] = unpack(n, v_repl, v_shard)
    return (loss, grad_x[None], *[out_g[n] for n in WEIGHT_ORDER], *[out_d[n] for n in WEIGHT_ORDER],
            *[out_m[n] for n in WEIGHT_ORDER], *[out_v[n] for n in WEIGHT_ORDER])
```

```python
import jax
import jax.numpy as jnp
from jax import lax
from jax.experimental import pallas as pl
from jax.experimental.pallas import tpu as pltpu

F32 = jnp.float32
BF16 = jnp.bfloat16
MESH = pl.DeviceIdType.MESH

D = 1024
N_META = 16
KW = 31
POOL_WINDOWS = (2, 4, 8, 16)
GD = 256
D_IN = 5 * D
D_FF = 2816
N_SHARD = 4
BR = 256
HALO = 16
PAD = BR - N_META
EXT = BR + 2 * HALO
RMS_EPS = 1e-6
LN_EPS = 1e-5
LR, B1, B2, ADAM_EPS, WD, STEP = 0.001, 0.9, 0.999, 1e-08, 0.01, 10
VMEM_LIMIT = 56 * 1024 * 1024


def _cparams(sem, vmem=VMEM_LIMIT):
    return pltpu.CompilerParams(dimension_semantics=sem, vmem_limit_bytes=vmem)


def _dot(a, b):
    return jnp.dot(a, b, preferred_element_type=F32)


def _dot_nt(a, b):
    return lax.dot_general(a, b, (((1,), (1,)), ((), ())), preferred_element_type=F32)


def _dot_tn(a, b):
    return lax.dot_general(a, b, (((0,), (0,)), ((), ())), preferred_element_type=F32)


def _sigmoid(x):
    return 0.5 * jnp.tanh(0.5 * x) + 0.5


def _row_ids(i, n, offset=0):
    return lax.broadcasted_iota(jnp.int32, (n, 1), 0) + (i * BR + offset - PAD)


def _pool_cnt(t, w, t_total):
    left = w // 2
    right = w - 1 - left
    lo = jnp.clip(t - left, 0, t_total)
    hi = jnp.clip(t + right + 1, 0, t_total)
    return jnp.maximum(hi - lo, 1).astype(F32)


def _halo_specs(nb, halo_width=D, width=D):
    last = nb * (BR // HALO) - 1
    return [
        pl.BlockSpec((HALO, halo_width), lambda i: (jnp.maximum(i * (BR // HALO) - 1, 0), 0)),
        pl.BlockSpec((BR, width), lambda i: (i, 0)),
        pl.BlockSpec((HALO, halo_width), lambda i: (jnp.minimum((i + 1) * (BR // HALO), last), 0)),
    ]


def _cols(ref, n):
    return [ref.at[:, k * D:(k + 1) * D] for k in range(n)]


def _fill_ext(ext_ref, prev, cur, nxt, i, nb):
    ext_ref[0:HALO, :] = jnp.where(i > 0, prev, 0.0)
    ext_ref[HALO:HALO + BR, :] = cur
    ext_ref[HALO + BR:EXT, :] = jnp.where(i < nb - 1, nxt, 0.0)


ROT_ROWS = EXT - 8


def _fill_rot(rot_ref, ext_ref, lanes):
    for r in range(1, 8):
        rot_ref[r] = ext_ref[pl.ds(r, ROT_ROWS), lanes]


def _tap(rot_ref, ext_ref, lanes, offset):
    q, r = divmod(offset, 8)
    if r == 0:
        return ext_ref[pl.ds(8 * q, BR), lanes]
    return rot_ref[r, pl.ds(8 * q, BR), :]


def _row_spec(width=D):
    return pl.BlockSpec((BR, width), lambda i: (i, 0))


def _x_spec():
    return pl.BlockSpec((BR, D), lambda i: (jnp.maximum(i - 1, 0), 0))


def _const_spec(shape):
    nd = len(shape)
    return pl.BlockSpec(shape, lambda i: (0,) * nd)


def _rms_u(head, x, g_mix, nb):
    def body(head_ref, x_ref, g_ref, u_ref):
        i = pl.program_id(0)
        h = jnp.where(i == 0, head_ref[...], x_ref[...])
        r = lax.rsqrt(jnp.mean(h * h, axis=-1, keepdims=True) + RMS_EPS)
        u_ref[...] = ((h * r) * g_ref[...]).astype(BF16)

    return pl.pallas_call(
        body, name="rms_u",
        grid=(nb,),
        in_specs=[_const_spec((BR, D)), _x_spec(), _const_spec((1, D))],
        out_specs=_row_spec(),
        out_shape=jax.ShapeDtypeStruct((nb * BR, D), BF16),
        compiler_params=_cparams(("arbitrary",)),
    )(head, x, g_mix)


def _in_proj_rows(tp):
    return tp // 2 if (tp // 2) % 16 == 0 else BR


def _in_proj_own(u, w_own, idx, nb):
    tp = nb * BR
    wcols = w_own.shape[1]
    rows = _in_proj_rows(tp)

    def body(idx_ref, u_ref, w_ref, z_ref, wb_ref):
        @pl.when(pl.program_id(0) == 0)
        def _():
            wb_ref[...] = w_ref[...].astype(BF16)

        z_ref[...] = _dot(u_ref[...], wb_ref[...])

    return pl.pallas_call(
        body, name="in_proj_own",
        grid_spec=pltpu.PrefetchScalarGridSpec(
            num_scalar_prefetch=1, grid=(tp // rows,),
            in_specs=[pl.BlockSpec((rows, D), lambda i, idx_ref: (i, 0)),
                      pl.BlockSpec((D, wcols), lambda i, idx_ref: (0, 0))],
            out_specs=pl.BlockSpec((rows, wcols), lambda i, idx_ref: (i, idx_ref[0])),
            scratch_shapes=[pltpu.VMEM((D, wcols), BF16)]),
        out_shape=jax.ShapeDtypeStruct((tp, N_SHARD * wcols), F32),
        compiler_params=_cparams(("arbitrary",)),
    )(idx, u, w_own)


def _in_proj_rest(u, w_in_b, z, idx, dep, nb):
    tp = nb * BR
    wcols = w_in_b.shape[2]
    rows = _in_proj_rows(tp)

    def body(idx_ref, u_ref, w_ref, z_in, dep_ref, z_ref):
        z_ref[...] = _dot(u_ref[...], w_ref[...])

    any_spec = pl.BlockSpec(memory_space=pl.ANY)
    return pl.pallas_call(
        body, name="in_proj_rest",
        grid_spec=pltpu.PrefetchScalarGridSpec(
            num_scalar_prefetch=1, grid=(N_SHARD - 1, tp // rows),
            in_specs=[pl.BlockSpec((rows, D), lambda s, i, idx_ref: (i, 0)),
                      pl.BlockSpec((None, D, wcols), lambda s, i, idx_ref: (idx_ref[1 + s], 0, 0)),
                      any_spec, any_spec],
            out_specs=pl.BlockSpec((rows, wcols), lambda s, i, idx_ref: (i, idx_ref[1 + s]))),
        out_shape=jax.ShapeDtypeStruct(z.shape, F32),
        input_output_aliases={3: 0},
        compiler_params=_cparams(("arbitrary", "arbitrary")),
    )(idx, u, w_in_b, z, dep)


def _mixers_fwd(z, head, x, b_gate, w_dw, b_dw, ln_g, ln_b, pool_scale, w_co, w_pool, w_po, w_o, nb, t_total):
    tp = nb * BR

    def body(z_prev, z_cur, z_next, head_ref, x_ref, bg_ref, wdw_ref, bdw_ref,
             lng_ref, lnb_ref, ps_ref, wco_ref, wpool_ref, wpo_ref, wo_ref,
             h1_ref, yc_ref, yp_ref, mg_ref, ca_ref, cpre_ref, m_ref, mw_ref, m2b_ref, ext_ref, pext_ref, rot_ref):
        i = pl.program_id(0)
        avp, agp, pp = _cols(z_prev, 3)
        av, ag, pc, za, zb = _cols(z_cur, 5)
        avn, agn, pn = _cols(z_next, 3)
        _fill_ext(ext_ref, avp[...] * _sigmoid(agp[...]), av[...] * _sigmoid(ag[...]),
                  avn[...] * _sigmoid(agn[...]), i, nb)
        _fill_ext(pext_ref, pp[...], pc[...], pn[...], i, nb)

        def conv_chunk(c, carry):
            lanes = pl.ds(pl.multiple_of(c * 128, 128), 128)
            _fill_rot(rot_ref, ext_ref, lanes)
            acc = jnp.broadcast_to(bdw_ref[:, lanes], (BR, 128))
            for k in range(KW):
                acc = acc + wdw_ref[k:k + 1, lanes] * _tap(rot_ref, ext_ref, lanes, 1 + k)
            cpre_ref[:, lanes] = acc
            return carry
        lax.fori_loop(0, D // 128, conv_chunk, 0)

        conv = cpre_ref[...]
        mu = jnp.mean(conv, axis=-1, keepdims=True)
        xc = conv - mu
        rstd = lax.rsqrt(jnp.mean(xc * xc, axis=-1, keepdims=True) + LN_EPS)
        ln = (xc * rstd) * lng_ref[...] + lnb_ref[...]
        cact = (ln * _sigmoid(ln)).astype(BF16)
        ca_ref[...] = cact
        y_conv = _dot(cact, wco_ref[...])
        yc_ref[...] = y_conv

        t = _row_ids(i, BR)
        for gi, w in enumerate(POOL_WINDOWS):
            left = w // 2
            right = w - 1 - left
            lanes = slice(gi * GD, (gi + 1) * GD)
            s = pext_ref[pl.ds(HALO - left, BR), lanes]
            for j in range(-left + 1, right + 1):
                s = s + pext_ref[pl.ds(HALO + j, BR), lanes]
            m = (s / _pool_cnt(t, w, t_total) - pext_ref[HALO:HALO + BR, lanes]).astype(BF16)
            m_ref[:, lanes] = m
            mw_ref[:, lanes] = _dot(m, wpool_ref[gi])
        mw = mw_ref[...]
        m2b = (mw * ps_ref[...]).astype(BF16)
        m2b_ref[...] = m2b
        y_pool = _dot(m2b, wpo_ref[...])
        yp_ref[...] = y_pool

        s_a = _sigmoid(za[...] + bg_ref[:, 0:D])
        s_b = _sigmoid(zb[...] + bg_ref[:, D:2 * D])
        merged = (s_a * y_conv + s_b * y_pool).astype(BF16)
        mg_ref[...] = merged
        h0 = jnp.where(i == 0, head_ref[...], x_ref[...])
        h1_ref[...] = h0 + _dot(merged, wo_ref[...])

    in_specs = (_halo_specs(nb, 3 * D, 5 * D)
                + [_const_spec((BR, D)), _x_spec(), _const_spec((1, 2 * D)), _const_spec((32, D)),
                   _const_spec((1, D)), _const_spec((1, D)), _const_spec((1, D)), _const_spec((1, D)),
                   _const_spec((D, D)), _const_spec((4, GD, GD)), _const_spec((D, D)), _const_spec((D, D))])
    outs = [(F32, "h1"), (F32, "yc"), (F32, "yp"), (BF16, "mg"), (BF16, "ca"), (F32, "cpre"), (BF16, "m"), (F32, "mw"),
            (BF16, "m2b")]
    return pl.pallas_call(
        body, name="mixers_fwd",
        grid=(nb,),
        in_specs=in_specs,
        out_specs=[_row_spec() for _ in outs],
        out_shape=[jax.ShapeDtypeStruct((tp, D), dt) for dt, _ in outs],
        scratch_shapes=[pltpu.VMEM((EXT, D), F32), pltpu.VMEM((EXT, D), F32), pltpu.VMEM((8, ROT_ROWS, 128), F32)],
        compiler_params=_cparams(("arbitrary",)),
    )(z, z, z, head, x, b_gate, w_dw, b_dw, ln_g, ln_b, pool_scale, w_co, w_pool, w_po, w_o)


def _ffn_fwd_bwd(h1, target, g_ffn, g_final, w_g, w_u, w_d, nb):
    tp = nb * BR

    def body(h1_ref, tgt_ref, gf_ref, gfin_ref, wg_hbm, wu_hbm, wd_hbm,
             dh1_ref, dh1b_ref, vb_ref, fb_ref, dgb_ref, dub_ref, dh2b_ref, loss_ref, dgf_ref, dgfin_ref,
             wg_ref, wu_ref, wd_ref, sem):
        i = pl.program_id(0)

        @pl.when(i == 0)
        def _():
            copies = [pltpu.make_async_copy(wg_hbm, wg_ref, sem.at[0]),
                      pltpu.make_async_copy(wu_hbm, wu_ref, sem.at[1]),
                      pltpu.make_async_copy(wd_hbm, wd_ref, sem.at[2])]
            for cp in copies:
                cp.start()
            loss_ref[...] = jnp.zeros_like(loss_ref)
            dgf_ref[...] = jnp.zeros_like(dgf_ref)
            dgfin_ref[...] = jnp.zeros_like(dgfin_ref)
            for cp in copies:
                cp.wait()

        h1 = h1_ref[...]
        r1 = lax.rsqrt(jnp.mean(h1 * h1, axis=-1, keepdims=True) + RMS_EPS)
        vn = h1 * r1
        vb = (vn * gf_ref[...]).astype(BF16)
        vb_ref[...] = vb
        g = _dot_nt(vb, wg_ref[...])
        up = _dot_nt(vb, wu_ref[...])
        sg = _sigmoid(g)
        sl = g * sg
        fb = (sl * up).astype(BF16)
        fb_ref[...] = fb
        h2 = h1 + _dot(fb, wd_ref[...])
        r2 = lax.rsqrt(jnp.mean(h2 * h2, axis=-1, keepdims=True) + RMS_EPS)
        yn = h2 * r2
        valid = i > 0
        diff = jnp.where(valid, yn * gfin_ref[...] - tgt_ref[...], 0.0)
        loss_ref[...] += 0.5 * jnp.sum(jnp.mean(diff * diff, axis=-1, keepdims=True))
        dy = diff * (1.0 / D)
        dgfin_ref[...] += jnp.sum(dy * yn, axis=0, keepdims=True)
        gd = dy * gfin_ref[...]
        dh2 = r2 * (gd - yn * jnp.mean(yn * gd, axis=-1, keepdims=True))
        dh2b = dh2.astype(BF16)
        dh2b_ref[...] = dh2b
        df = _dot_nt(dh2b, wd_ref[...])
        dub = (df * sl).astype(BF16)
        dgb = (df * up * (sg * (1.0 + g * (1.0 - sg)))).astype(BF16)
        dub_ref[...] = dub
        dgb_ref[...] = dgb
        dv = _dot(dgb, wg_ref[...]) + _dot(dub, wu_ref[...])
        dgf_ref[...] += jnp.sum(dv * vn, axis=0, keepdims=True)
        gd1 = dv * gf_ref[...]
        dh1 = dh2 + r1 * (gd1 - vn * jnp.mean(vn * gd1, axis=-1, keepdims=True))
        dh1_ref[...] = dh1
        dh1b_ref[...] = dh1.astype(BF16)

    any_spec = pl.BlockSpec(memory_space=pl.ANY)
    return pl.pallas_call(
        body, name="ffn_fwd_bwd",
        grid=(nb,),
        in_specs=[_row_spec(), _x_spec(), _const_spec((1, D)), _const_spec((1, D)), any_spec, any_spec, any_spec],
        out_specs=[_row_spec(), _row_spec(), _row_spec(), _row_spec(D_FF), _row_spec(D_FF), _row_spec(D_FF), _row_spec(),
                   _const_spec((1, 1)), _const_spec((1, D)), _const_spec((1, D))],
        out_shape=[jax.ShapeDtypeStruct((tp, D), F32), jax.ShapeDtypeStruct((tp, D), BF16),
                   jax.ShapeDtypeStruct((tp, D), BF16), jax.ShapeDtypeStruct((tp, D_FF), BF16),
                   jax.ShapeDtypeStruct((tp, D_FF), BF16), jax.ShapeDtypeStruct((tp, D_FF), BF16),
                   jax.ShapeDtypeStruct((tp, D), BF16), jax.ShapeDtypeStruct((1, 1), F32),
                   jax.ShapeDtypeStruct((1, D), F32), jax.ShapeDtypeStruct((1, D), F32)],
        scratch_shapes=[pltpu.VMEM((D_FF, D), BF16), pltpu.VMEM((D_FF, D), BF16), pltpu.VMEM((D_FF, D), BF16),
                        pltpu.SemaphoreType.DMA((3,))],
        compiler_params=_cparams(("arbitrary",)),
    )(h1, target, g_ffn, g_final, w_g, w_u, w_d)


def _mixers_bwd_rows(dh1b, yc, yp, z, b_gate, cpre, ln_g, ln_b, mw, pool_scale, w_o, w_co, w_po, w_pool, dep, nb):
    tp = nb * BR

    def body(dh1b_ref, yc_ref, yp_ref, za, zb, bg_ref, cpre_ref, lng_ref, lnb_ref, mw_ref, ps_ref,
             wo_ref, wco_ref, wpo_ref, wpool_ref, dep_ref,
             dycb_ref, dypb_ref, dzg_ref, dconv_ref, dmwb_ref, dm_ref, dbg_ref, dlng_ref, dlnb_ref, dbdw_ref, dps_ref):
        i = pl.program_id(0)

        @pl.when(i == 0)
        def _():
            for r in (dbg_ref, dlng_ref, dlnb_ref, dbdw_ref, dps_ref):
                r[...] = jnp.zeros_like(r)

        dmg = _dot_nt(dh1b_ref[...], wo_ref[...])
        s_a = _sigmoid(za[...] + bg_ref[:, 0:D])
        s_b = _sigmoid(zb[...] + bg_ref[:, D:2 * D])
        dycb = (dmg * s_a).astype(BF16)
        dypb = (dmg * s_b).astype(BF16)
        dycb_ref[...] = dycb
        dypb_ref[...] = dypb
        dza = dmg * yc_ref[...] * (s_a * (1.0 - s_a))
        dzb = dmg * yp_ref[...] * (s_b * (1.0 - s_b))
        dzg_ref[:, 0:D] = dza.astype(BF16)
        dzg_ref[:, D:2 * D] = dzb.astype(BF16)
        dbg_ref[:, 0:D] += jnp.sum(dza, axis=0, keepdims=True)
        dbg_ref[:, D:2 * D] += jnp.sum(dzb, axis=0, keepdims=True)

        dca = _dot_nt(dycb, wco_ref[...])
        conv = cpre_ref[...]
        mu = jnp.mean(conv, axis=-1, keepdims=True)
        xc = conv - mu
        rstd = lax.rsqrt(jnp.mean(xc * xc, axis=-1, keepdims=True) + LN_EPS)
        xhat = xc * rstd
        ln = xhat * lng_ref[...] + lnb_ref[...]
        sg = _sigmoid(ln)
        dln = dca * (sg * (1.0 + ln * (1.0 - sg)))
        dlng_ref[...] += jnp.sum(dln * xhat, axis=0, keepdims=True)
        dlnb_ref[...] += jnp.sum(dln, axis=0, keepdims=True)
        dxh = dln * lng_ref[...]
        dconv = rstd * (dxh - jnp.mean(dxh, axis=-1, keepdims=True)
                        - xhat * jnp.mean(dxh * xhat, axis=-1, keepdims=True))
        dconv_ref[...] = dconv
        dbdw_ref[...] += jnp.sum(dconv, axis=0, keepdims=True)

        dm2 = _dot_nt(dypb, wpo_ref[...])
        dps_ref[...] += jnp.sum(dm2 * mw_ref[...], axis=0, keepdims=True)
        dmwb = (dm2 * ps_ref[...]).astype(BF16)
        dmwb_ref[...] = dmwb
        for gi in range(len(POOL_WINDOWS)):
            lanes = slice(gi * GD, (gi + 1) * GD)
            dm_ref[:, lanes] = _dot_nt(dmwb[:, lanes], wpool_ref[gi])

    in_specs = [_row_spec(), _row_spec(), _row_spec(),
                pl.BlockSpec((BR, D), lambda i: (i, 3)), pl.BlockSpec((BR, D), lambda i: (i, 4)),
                _const_spec((1, 2 * D)), _row_spec(), _const_spec((1, D)), _const_spec((1, D)), _row_spec(),
                _const_spec((1, D)), _const_spec((D, D)), _const_spec((D, D)), _const_spec((D, D)),
                _const_spec((4, GD, GD)), pl.BlockSpec(memory_space=pl.ANY)]
    return pl.pallas_call(
        body, name="mixers_bwd_rows",
        grid=(nb,),
        in_specs=in_specs,
        out_specs=[_row_spec(), _row_spec(), _row_spec(2 * D), _row_spec(), _row_spec(), _row_spec(),
                   _const_spec((1, 2 * D)), _const_spec((1, D)), _const_spec((1, D)), _const_spec((1, D)),
                   _const_spec((1, D))],
        out_shape=[jax.ShapeDtypeStruct((tp, D), BF16), jax.ShapeDtypeStruct((tp, D), BF16),
                   jax.ShapeDtypeStruct((tp, 2 * D), BF16), jax.ShapeDtypeStruct((tp, D), F32),
                   jax.ShapeDtypeStruct((tp, D), BF16), jax.ShapeDtypeStruct((tp, D), F32),
                   jax.ShapeDtypeStruct((1, 2 * D), F32), jax.ShapeDtypeStruct((1, D), F32),
                   jax.ShapeDtypeStruct((1, D), F32), jax.ShapeDtypeStruct((1, D), F32),
                   jax.ShapeDtypeStruct((1, D), F32)],
        compiler_params=_cparams(("arbitrary",)),
    )(dh1b, yc, yp, z, z, b_gate, cpre, ln_g, ln_b, mw, pool_scale, w_o, w_co, w_po, w_pool, dep)


def _mixers_bwd_halo(dconv, dm, z, dzg, w_dw, head, x, g_mix, dh1, w_in_b, dep, nb, t_total):
    tp = nb * BR
    ns = w_in_b.shape[0]
    wcols = w_in_b.shape[2]
    seq = x.shape[0]

    def body(dcp, dcc, dcn, dmp, dmc, dmn, z_prev, z_cur, z_next, dzg_ref, wdw_ref, head_ref, x_ref, g_ref,
             dh1_ref, w_hbm, dep_ref,
             dzb_ref, gx_ref, dhead_ref, dwdw_ref, dgmix_ref,
             w_ref, sem, aext_ref, dext_ref, qext_ref, da_ref, rot_ref, dwp_ref):
        i = pl.program_id(0)
        (avp, agp), (av, ag), (avn, agn) = _cols(z_prev, 2), _cols(z_cur, 2), _cols(z_next, 2)

        @pl.when(i == 0)
        def _():
            cp = pltpu.make_async_copy(w_hbm, w_ref, sem.at[0])
            cp.start()
            dwp_ref[...] = jnp.zeros_like(dwp_ref)
            dgmix_ref[...] = jnp.zeros_like(dgmix_ref)
            cp.wait()

        sig_g = _sigmoid(ag[...])
        _fill_ext(aext_ref, avp[...] * _sigmoid(agp[...]), av[...] * sig_g, avn[...] * _sigmoid(agn[...]), i, nb)
        _fill_ext(dext_ref, dcp[...], dcc[...], dcn[...], i, nb)
        _fill_ext(qext_ref, dmp[...], dmc[...], dmn[...], i, nb)

        def conv_chunk(c, carry):
            lanes = pl.ds(pl.multiple_of(c * 128, 128), 128)
            _fill_rot(rot_ref, dext_ref, lanes)
            acc = jnp.zeros((BR, 128), F32)
            for k in range(KW):
                acc = acc + wdw_ref[k:k + 1, lanes] * _tap(rot_ref, dext_ref, lanes, KW - k)
            da_ref[:, lanes] = acc
            _fill_rot(rot_ref, aext_ref, lanes)
            dcv = dext_ref[HALO:HALO + BR, lanes]
            for k in range(KW):
                prod = _tap(rot_ref, aext_ref, lanes, 1 + k) * dcv
                dwp_ref[k, :, lanes] += jnp.sum(prod.reshape(BR // 8, 8, 128), axis=0)
            return carry
        lax.fori_loop(0, D // 128, conv_chunk, 0)

        @pl.when(i == nb - 1)
        def _():
            dwdw_ref[...] = jnp.sum(dwp_ref[...], axis=1)

        da = da_ref[...]
        a_val = av[...]
        dzb_ref[:, 0:D] = (da * sig_g).astype(BF16)
        dzb_ref[:, D:2 * D] = (da * a_val * (sig_g * (1.0 - sig_g))).astype(BF16)

        t_ext = _row_ids(i, EXT, -HALO)
        for gi, w in enumerate(POOL_WINDOWS):
            left = w // 2
            right = w - 1 - left
            lanes = slice(gi * GD, (gi + 1) * GD)
            qext_ref[:, lanes] = qext_ref[:, lanes] / _pool_cnt(t_ext, w, t_total)
            s = qext_ref[pl.ds(HALO - right, BR), lanes]
            for j in range(-right + 1, left + 1):
                s = s + qext_ref[pl.ds(HALO + j, BR), lanes]
            dzb_ref[:, 2 * D + gi * GD:2 * D + (gi + 1) * GD] = (s - dmc[:, lanes]).astype(BF16)
        dzb_ref[:, 3 * D:5 * D] = dzg_ref[...]

        du = _dot_nt(dzb_ref[:, 0:wcols], w_ref[0])
        for s_i in range(1, ns):
            du = du + _dot_nt(dzb_ref[:, s_i * wcols:(s_i + 1) * wcols], w_ref[s_i])
        h0 = jnp.where(i == 0, head_ref[...], x_ref[...])
        r0 = lax.rsqrt(jnp.mean(h0 * h0, axis=-1, keepdims=True) + RMS_EPS)
        un = h0 * r0
        dgmix_ref[...] += jnp.sum(du * un, axis=0, keepdims=True)
        gd = du * g_ref[...]
        dh0 = dh1_ref[...] + r0 * (gd - un * jnp.mean(un * gd, axis=-1, keepdims=True))
        gx_ref[...] = dh0

        @pl.when(i == 0)
        def _():
            dhead_ref[...] = dh0

    any_spec = pl.BlockSpec(memory_space=pl.ANY)
    in_specs = (_halo_specs(nb) + _halo_specs(nb) + _halo_specs(nb, 2 * D, 2 * D)
                + [_row_spec(2 * D), _const_spec((32, D)), _const_spec((BR, D)), _x_spec(), _const_spec((1, D)),
                   _row_spec(), any_spec, any_spec])
    return pl.pallas_call(
        body, name="mixers_bwd_halo",
        grid=(nb,),
        in_specs=in_specs,
        out_specs=[_row_spec(D_IN), _x_spec(), _const_spec((BR, D)), _const_spec((32, D)), _const_spec((1, D))],
        out_shape=[jax.ShapeDtypeStruct((tp, D_IN), BF16), jax.ShapeDtypeStruct((seq, D), F32),
                   jax.ShapeDtypeStruct((BR, D), F32), jax.ShapeDtypeStruct((32, D), F32),
                   jax.ShapeDtypeStruct((1, D), F32)],
        scratch_shapes=[pltpu.VMEM((ns, D, wcols), BF16), pltpu.SemaphoreType.DMA((1,)),
                        pltpu.VMEM((EXT, D), F32), pltpu.VMEM((EXT, D), F32), pltpu.VMEM((EXT, D), F32),
                        pltpu.VMEM((BR, D), F32), pltpu.VMEM((8, ROT_ROWS, 128), F32), pltpu.VMEM((32, 8, D), F32)],
        compiler_params=_cparams(("arbitrary",)),
    )(dconv, dconv, dconv, dm, dm, dm, z, z, z, dzg, w_dw, head, x, g_mix, dh1, w_in_b, dep)


def _wgrad(a, c, tm, tn, tk, name, diag=False, col_major=False, dep=None):
    tp, m = a.shape
    n = c.shape[1]
    nk = tp // tk
    gm, gn = m // tm, n // tn

    def body(a_ref, c_ref, *rest):
        o_ref, ob_ref = rest[-2:]
        k = pl.program_id(2)

        @pl.when(k == 0)
        def _():
            o_ref[...] = jnp.zeros_like(o_ref)

        o_ref[...] += _dot_tn(a_ref[...], c_ref[...])

        @pl.when(k == nk - 1)
        def _():
            ob_ref[...] = o_ref[...].astype(BF16)

    c_map = lambda i, j, k: (k, j)
    grid = (gm, gn, nk)
    deps = [] if dep is None else [dep]
    if diag:
        grid = (gm, 1, nk)
        c_map = lambda i, j, k: (k, i)
        o_spec = pl.BlockSpec((tm, tn), lambda i, j, k: (i, 0))
        o_shape = (m, tn)
    elif col_major:
        o_spec = pl.BlockSpec((None, tm, tn), lambda i, j, k: (j, i, 0))
        o_shape = (gn, m, tn)
    else:
        o_spec = pl.BlockSpec((tm, tn), lambda i, j, k: (i, j))
        o_shape = (m, n)
    return pl.pallas_call(
        body, name=name,
        grid=grid,
        in_specs=[pl.BlockSpec((tk, tm), lambda i, j, k: (k, i)), pl.BlockSpec((tk, tn), c_map)]
        + [pl.BlockSpec(memory_space=pl.ANY)] * len(deps),
        out_specs=[o_spec, o_spec],
        out_shape=[jax.ShapeDtypeStruct(o_shape, F32), jax.ShapeDtypeStruct(o_shape, BF16)],
        compiler_params=_cparams(("arbitrary", "arbitrary", "arbitrary")),
    )(a, c, *deps)


def _place():
    x, y, c = lax.axis_index("x"), lax.axis_index("y"), lax.axis_index("c")
    others = [(1 - x, y), (x, 1 - y), (1 - x, 1 - y)]
    return x, y, c, others


def _split2(a, axis=0):
    return a.reshape(a.shape[:axis] + (2, a.shape[axis] // 2) + a.shape[axis + 1:])


def _merge2(a, axis=0):
    return a.reshape(a.shape[:axis] + (2 * a.shape[axis + 1],) + a.shape[axis + 2:])


def _cast_into_slot(shards, chip, dep, name):
    n = len(shards)
    r, c = shards[0].shape
    r2 = r // 2

    def body(chip_ref, *refs):
        for a in range(n):
            refs[n + 1 + a][...] = refs[a][...].astype(BF16)

    out = pl.pallas_call(
        body, name=name,
        grid_spec=pltpu.PrefetchScalarGridSpec(
            num_scalar_prefetch=1, grid=(2,),
            in_specs=[pl.BlockSpec((r2, c), lambda h, chip_ref: (h, 0))] * n + [pl.BlockSpec(memory_space=pl.ANY)],
            out_specs=[pl.BlockSpec((None, None, r2, c), lambda h, chip_ref: (chip_ref[0], h, 0, 0))] * n),
        out_shape=[jax.ShapeDtypeStruct((N_SHARD, 2, r2, c), BF16)] * n,
        compiler_params=_cparams(("arbitrary",)),
    )(chip, *shards, dep)
    return list(out)


GATHER_ID, SMALL_ID, TINY_ID = 1, 2, 6
SCATTER_IDS = {"ffn": 3, "mix": 4, "in": 5}
FLIPS = [(dx, dy, dc) for dx in (0, 1) for dy in (0, 1) for dc in (0, 1)][1:]
HBM_SPEC = pl.BlockSpec(memory_space=pltpu.HBM)
SEM_SPEC = pl.BlockSpec(memory_space=pltpu.SEMAPHORE)
DATAFLOW = pltpu.SideEffectType.DATAFLOW_SIDE_EFFECTING
TOKEN = jax.ShapeDtypeStruct((8, 128), F32)


def _in_hbm(a):
    return pltpu.with_memory_space_constraint(a, pltpu.HBM)


def _tiny_views(ins):
    return [lambda frm, to, c: ins[0]], [lambda frm, to, c: ins[1].at[frm]]


def _tiny_start(v, name):
    land = lax.empty((N_SHARD,) + v.shape, v.dtype)
    return _split_start(lambda ins: _tiny_views(ins)[0], lambda ins: _tiny_views(ins)[1], [v, land], 1, name, TINY_ID)


def _tiny_wait(send_sems, recv_sems, arrays, after, name):
    return _split_wait(lambda ins: _tiny_views(ins)[0], lambda ins: _tiny_views(ins)[1], send_sems, recv_sems,
                       arrays, after, name)[1]


def _ici_copies(srcs, dsts, send_sems, recv_sems, started):
    x, y, c, others = _place()
    mine = 2 * x + y
    copies = []
    for a in range(len(srcs)):
        for j, chip in enumerate(others):
            there = 2 * chip[0] + chip[1]
            src, dst = srcs[a](mine, there, c), dsts[a](mine, there, c)
            if not started:
                dst = dsts[a](there, mine, c)
            copies.append(pltpu.make_async_remote_copy(
                src_ref=src, dst_ref=dst, send_sem=send_sems.at[a * 3 + j], recv_sem=recv_sems.at[a * 3 + j],
                device_id=(*chip, c), device_id_type=MESH))
    return copies


def _same_core_peers():
    x, y, c, others = _place()
    return [(*chip, c) for chip in others]


def _all_peers():
    x, y, c, _ = _place()
    return [(jnp.bitwise_xor(x, dx), jnp.bitwise_xor(y, dy), jnp.bitwise_xor(c, dc)) for dx, dy, dc in FLIPS]


def _split_start(srcs_of, dsts_of, arrays, n_src, name, collective_id, peers_of=_same_core_peers,
                 copies_of=None, n_sems=None, dep=None):
    n = len(arrays)
    n_sems = n_sems or 3 * n_src
    copies_of = copies_of or (lambda ins, ss, rs, started: _ici_copies(srcs_of(ins), dsts_of(ins), ss, rs, started))
    deps = [] if dep is None else [dep]
    nd = len(deps)

    def body(*refs):
        ins = refs[:n]
        send_sems, recv_sems = refs[n + nd], refs[n + nd + 1]
        token = refs[2 * n + nd + 2]
        peers = peers_of()
        barrier = pltpu.get_barrier_semaphore()
        for peer in peers:
            pl.semaphore_signal(barrier, inc=1, device_id=peer, device_id_type=MESH)
        pl.semaphore_wait(barrier, len(peers))
        for cp in copies_of(ins, send_sems, recv_sems, True):
            cp.start()
        token[...] = jnp.zeros_like(token)

    out = pl.pallas_call(
        body, name=name,
        in_specs=[HBM_SPEC] * n + [pl.BlockSpec(memory_space=pl.ANY)] * nd,
        out_specs=(SEM_SPEC, SEM_SPEC, *([HBM_SPEC] * n), pl.BlockSpec(memory_space=pltpu.VMEM)),
        out_shape=(pltpu.SemaphoreType.DMA((n_sems,)), pltpu.SemaphoreType.DMA((n_sems,)),
                   *[pltpu.HBM(a.shape, a.dtype) for a in arrays], TOKEN),
        input_output_aliases={a: 2 + a for a in range(n)},
        compiler_params=pltpu.CompilerParams(has_side_effects=DATAFLOW, collective_id=collective_id),
    )(*[_in_hbm(a) for a in arrays], *deps)
    return out[0], out[1], list(out[2:2 + n]), out[2 + n]


def _split_wait(srcs_of, dsts_of, send_sems, recv_sems, arrays, after, name, copies_of=None):
    n = len(arrays)
    copies_of = copies_of or (lambda ins, ss, rs, started: _ici_copies(srcs_of(ins), dsts_of(ins), ss, rs, started))

    def body(*refs):
        ins = refs[:n]
        send_sems, recv_sems = refs[n], refs[n + 1]
        for cp in copies_of(ins, send_sems, recv_sems, False):
            cp.wait_send()
            cp.wait_recv()

    return pl.pallas_call(
        body, name=name,
        in_specs=[HBM_SPEC] * n + [SEM_SPEC, SEM_SPEC] + [pl.BlockSpec(memory_space=pl.ANY)] * len(after),
        out_specs=[HBM_SPEC] * n,
        out_shape=[pltpu.HBM(a.shape, a.dtype) for a in arrays],
        input_output_aliases={a: a for a in range(n)},
        compiler_params=pltpu.CompilerParams(has_side_effects=DATAFLOW),
    )(*arrays, send_sems, recv_sems, *after)


def _gather_views(ins):
    view = [lambda frm, to, c, r=r: r.at[frm, c] for r in ins]
    return view


def _gather_start(bufs, dep, name):
    return _split_start(_gather_views, _gather_views, bufs, len(bufs), name, GATHER_ID, dep=dep)


def _gather_wait(send_sems, recv_sems, bufs, after, name):
    return _split_wait(_gather_views, _gather_views, send_sems, recv_sems, bufs, after, name)


SIBLING_ONLY = pltpu.CompilerParams(collective_id=0)


def _sibling_handshake(x, y, c):
    barrier = pltpu.get_barrier_semaphore()
    pl.semaphore_signal(barrier, inc=1, device_id=(x, y, 1 - c), device_id_type=MESH)
    pl.semaphore_wait(barrier, 1)


def _forward_halves(bufs, name):
    n = len(bufs)
    any_spec = pl.BlockSpec(memory_space=pl.ANY)

    def body(*refs):
        outs = refs[n:2 * n]
        send_sems, recv_sems = refs[2 * n:]
        x, y, c, others = _place()
        _sibling_handshake(x, y, c)
        copies = []
        for a in range(n):
            for j, chip in enumerate(others):
                landed = outs[a].at[2 * chip[0] + chip[1], c]
                copies.append(pltpu.make_async_remote_copy(
                    src_ref=landed, dst_ref=landed, send_sem=send_sems.at[a * 3 + j], recv_sem=recv_sems.at[a * 3 + j],
                    device_id=(x, y, 1 - c), device_id_type=MESH))
        for cp in copies:
            cp.start()
        for a in range(n):
            for j, chip in enumerate(others):
                landed = outs[a].at[2 * chip[0] + chip[1], 1 - c]
                pltpu.make_async_remote_copy(
                    src_ref=landed, dst_ref=landed, send_sem=send_sems.at[a * 3 + j], recv_sem=recv_sems.at[a * 3 + j],
                    device_id=(x, y, c), device_id_type=MESH).wait_recv()
        for cp in copies:
            cp.wait_send()

    out = pl.pallas_call(
        body, name=name,
        in_specs=[any_spec] * n, out_specs=[any_spec] * n,
        out_shape=[jax.ShapeDtypeStruct(b.shape, b.dtype) for b in bufs],
        input_output_aliases={a: a for a in range(n)},
        scratch_shapes=[pltpu.SemaphoreType.DMA((3 * n,)), pltpu.SemaphoreType.DMA((3 * n,))],
        compiler_params=SIBLING_ONLY,
    )(*bufs)
    return [_merge2(o, 1) for o in out]


def _swap_halves_bf16(gbs, name):
    n = len(gbs)
    any_spec = pl.BlockSpec(memory_space=pl.ANY)

    def body(*refs):
        ins, outs = refs[:n], refs[n:2 * n]
        send_sems, recv_sems = refs[2 * n:]
        x, y, c, _ = _place()
        _sibling_handshake(x, y, c)
        copies = []
        for a in range(n):
            copies.append(pltpu.make_async_remote_copy(
                src_ref=ins[a].at[:, 1 - c], dst_ref=outs[a], send_sem=send_sems.at[a], recv_sem=recv_sems.at[a],
                device_id=(x, y, 1 - c), device_id_type=MESH))
        for cp in copies:
            cp.start()
        for cp in copies:
            cp.wait()

    return pl.pallas_call(
        body, name=name,
        in_specs=[any_spec] * n, out_specs=[any_spec] * n,
        out_shape=[jax.ShapeDtypeStruct((g.shape[0], g.shape[1] // 2, g.shape[2]), g.dtype) for g in gbs],
        scratch_shapes=[pltpu.SemaphoreType.DMA((n,)), pltpu.SemaphoreType.DMA((n,))],
        compiler_params=SIBLING_ONLY,
    )(*[_split2(g, 1) for g in gbs])


def _scatter_srcs(n):
    return lambda ins: [lambda frm, to, c, r=r: r.at[to] for r in ins[:n]]


def _scatter_dsts(n):
    return lambda ins: [lambda frm, to, c, r=r: r.at[frm] for r in ins[n:]]


def _scatter_start(hbs, name, collective_id):
    n = len(hbs)
    lands = [lax.empty(h.shape, h.dtype) for h in hbs]
    return _split_start(_scatter_srcs(n), _scatter_dsts(n), list(hbs) + lands, n, name, collective_id)


def _scatter_wait(send_sems, recv_sems, arrays, after, name):
    n = len(arrays) // 2
    return _split_wait(_scatter_srcs(n), _scatter_dsts(n), send_sems, recv_sems, arrays, after, name)[n:]


def _join_halves(rhs, name):
    n = len(rhs)
    any_spec = pl.BlockSpec(memory_space=pl.ANY)

    def body(*refs):
        outs = refs[n:2 * n]
        send_sems, recv_sems = refs[2 * n:]
        x, y, c, _ = _place()
        _sibling_handshake(x, y, c)
        copies = []
        for a in range(n):
            copies.append(pltpu.make_async_remote_copy(
                src_ref=outs[a].at[c], dst_ref=outs[a].at[c], send_sem=send_sems.at[a],
                recv_sem=recv_sems.at[a], device_id=(x, y, 1 - c), device_id_type=MESH))
        for cp in copies:
            cp.start()
        for a in range(n):
            landed = outs[a].at[1 - c]
            pltpu.make_async_remote_copy(
                src_ref=landed, dst_ref=landed, send_sem=send_sems.at[a], recv_sem=recv_sems.at[a],
                device_id=(x, y, c), device_id_type=MESH).wait_recv()
        for cp in copies:
            cp.wait_send()

    out = pl.pallas_call(
        body, name=name,
        in_specs=[any_spec] * n, out_specs=[any_spec] * n,
        out_shape=[jax.ShapeDtypeStruct(r.shape, r.dtype) for r in rhs],
        input_output_aliases={a: a for a in range(n)},
        scratch_shapes=[pltpu.SemaphoreType.DMA((n,)), pltpu.SemaphoreType.DMA((n,))],
        compiler_params=SIBLING_ONLY,
    )(*rhs)
    return [_merge2(o) for o in out]


def _peer_copies(ins, send_sems, recv_sems, started):
    x, y, c, _ = _place()
    copies = []
    for k, (dx, dy, dc) in enumerate(FLIPS):
        px, py, pc = jnp.bitwise_xor(x, dx), jnp.bitwise_xor(y, dy), jnp.bitwise_xor(c, dc)
        slot = 4 * x + 2 * y + c if started else 4 * px + 2 * py + pc
        copies.append(pltpu.make_async_remote_copy(
            src_ref=ins[0], dst_ref=ins[1].at[slot], send_sem=send_sems.at[k], recv_sem=recv_sems.at[k],
            device_id=(px, py, pc), device_id_type=MESH))
    return copies


def _small_start(v, name):
    land = lax.empty((8,) + v.shape, v.dtype)
    return _split_start(None, None, [v, land], 0, name, SMALL_ID, peers_of=_all_peers, copies_of=_peer_copies,
                        n_sems=len(FLIPS))


def _small_wait(send_sems, recv_sems, arrays, after, name):
    return _split_wait(None, None, send_sems, recv_sems, arrays, after, name, copies_of=_peer_copies)[1]


def _sum_slots(land, v, me):
    rows, cols = v.shape

    def body(me_ref, land_ref, v_ref, o_ref):
        o_ref[...] = jnp.zeros_like(o_ref)
        for d in range(8):
            @pl.when(me_ref[0] == d)
            def _():
                o_ref[...] += v_ref[...]

            @pl.when(me_ref[0] != d)
            def _():
                o_ref[...] += land_ref[d]

    return pl.pallas_call(
        body, name="sum_slots",
        grid_spec=pltpu.PrefetchScalarGridSpec(
            num_scalar_prefetch=1, grid=(1,),
            in_specs=[pl.BlockSpec((8, rows, cols), lambda i, me_ref: (0, 0, 0)),
                      pl.BlockSpec((rows, cols), lambda i, me_ref: (0, 0))],
            out_specs=pl.BlockSpec((rows, cols), lambda i, me_ref: (0, 0))),
        out_shape=jax.ShapeDtypeStruct((rows, cols), F32),
        compiler_params=_cparams(("arbitrary",)),
    )(me, land, v)


def _by_shape(arrays):
    groups = {}
    for k, a in enumerate(arrays):
        groups.setdefault(a.shape, []).append(k)
    return list(groups.values())


def _add_sibling_half(gs, sbs, idx, name):
    n = len(gs)
    ns, r, c = gs[0].shape
    r2 = r // 2

    def body(idx_ref, *refs):
        for a in range(n):
            g_ref, sb_ref, hown_ref, hb_ref = refs[a], refs[n + a], refs[2 * n + a], refs[3 * n + a]
            h = g_ref[...] + sb_ref[...].astype(F32)
            hb_ref[...] = h.astype(BF16)

            @pl.when(pl.program_id(0) == idx_ref[0])
            def _():
                hown_ref[...] = h

    spec = pl.BlockSpec((None, r2, c), lambda s, idx_ref: (s, 0, 0))
    out = pl.pallas_call(
        body, name=name,
        grid_spec=pltpu.PrefetchScalarGridSpec(
            num_scalar_prefetch=1, grid=(ns,),
            in_specs=[pl.BlockSpec((None, r2, c), lambda s, idx_ref: (s, idx_ref[4], 0))] * n + [spec] * n,
            out_specs=[pl.BlockSpec((r2, c), lambda s, idx_ref: (0, 0))] * n + [spec] * n),
        out_shape=[jax.ShapeDtypeStruct((r2, c), F32)] * n + [jax.ShapeDtypeStruct((ns, r2, c), BF16)] * n,
        compiler_params=_cparams(("arbitrary",)),
    )(idx, *gs, *sbs)
    return [(out[a], out[n + a]) for a in range(n)]


def _add_chip_slabs(hs, rbs, idx, name):
    n = len(hs)
    r2, c = hs[0].shape

    def body(idx_ref, *refs):
        for a in range(n):
            h_ref, r0_ref, r1_ref, r2_ref = refs[4 * a:4 * a + 4]
            refs[4 * n + a][...] = ((h_ref[...] + r0_ref[...].astype(F32)) + r1_ref[...].astype(F32)) + r2_ref[...].astype(F32)

    def pick(k):
        return pl.BlockSpec((None, r2, c), lambda i, idx_ref: (idx_ref[k], 0, 0))

    operands = []
    for h, rb in zip(hs, rbs):
        operands += [h, rb, rb, rb]
    out = pl.pallas_call(
        body, name=name,
        grid_spec=pltpu.PrefetchScalarGridSpec(
            num_scalar_prefetch=1, grid=(1,),
            in_specs=[pl.BlockSpec((r2, c), lambda i, idx_ref: (0, 0)), pick(1), pick(2), pick(3)] * n,
            out_specs=[pl.BlockSpec((None, r2, c), lambda i, idx_ref: (idx_ref[4], 0, 0))] * n),
        out_shape=[jax.ShapeDtypeStruct((2, r2, c), F32)] * n,
        compiler_params=_cparams(("arbitrary",)),
    )(idx, *operands)
    return list(out)


ELEMENTWISE_VMEM = 32 * 1024 * 1024


def _adamw(items, name):
    n = len(items)
    r, c = items[0][0].shape
    br = max(b for b in range(8, r + 1, 8) if r % b == 0 and n * 16 * b * c * 4 <= ELEMENTWISE_VMEM) if r % 8 == 0 else r

    def body(*refs):
        for a in range(n):
            g_ref, w_ref, m_ref, v_ref = refs[4 * a:4 * a + 4]
            go_ref, d_ref, nm_ref, nv_ref = refs[4 * n + 4 * a:4 * n + 4 * a + 4]
            gg = g_ref[...]
            go_ref[...] = gg
            nm = B1 * m_ref[...] + (1.0 - B1) * gg
            nv = B2 * v_ref[...] + (1.0 - B2) * jnp.square(gg)
            m_hat = nm / (1.0 - B1 ** STEP)
            v_hat = nv / (1.0 - B2 ** STEP)
            d_ref[...] = -LR * (m_hat / (jnp.sqrt(v_hat) + ADAM_EPS) + WD * w_ref[...])
            nm_ref[...] = nm
            nv_ref[...] = nv

    spec = pl.BlockSpec((br, c), lambda i: (i, 0))
    out = pl.pallas_call(
        body, name=name,
        grid=(r // br,),
        in_specs=[spec] * (4 * n), out_specs=[spec] * (4 * n),
        out_shape=[jax.ShapeDtypeStruct((r, c), F32)] * (4 * n),
        compiler_params=_cparams(("arbitrary",)),
    )(*[a for item in items for a in item])
    return [tuple(out[4 * a:4 * a + 4]) for a in range(n)]


BIG = ("w_in", "w_conv_out", "w_pool", "w_pool_out", "w_o", "w_ffn_gate", "w_ffn_up", "w_ffn_down")
REPL = ("g_mix", "b_gate", "b_dw", "ln_g", "ln_b", "pool_scale", "g_ffn", "g_final")
GROUP_MIX = ("w_conv_out", "w_pool", "w_pool_out", "w_o")
GROUP_FFN = ("w_ffn_gate", "w_ffn_up", "w_ffn_down")
TRANSPOSED = ("w_ffn_gate", "w_ffn_up")
WEIGHT_ORDER = ("meta_tokens", "g_mix", "w_in", "b_gate", "w_dw", "b_dw", "ln_g", "ln_b", "w_conv_out", "w_pool",
                "pool_scale", "w_pool_out", "w_o", "g_ffn", "w_ffn_gate", "w_ffn_up", "w_ffn_down", "g_final")


def _shard2d(name, a):
    a = a[0]
    if name == "w_pool":
        return a.reshape(4 * 64, GD)
    if name in TRANSPOSED:
        return a.T
    return a


def _unshard2d(name, a, shape):
    return a.T.reshape(shape) if name in TRANSPOSED else a.reshape(shape)


def _slabs_to_cols(a):
    ns, m, c = a.shape
    return a.transpose(1, 0, 2).reshape(m, ns * c)


def kernel(x, meta_tokens, g_mix, w_in, b_gate, w_dw, b_dw, ln_g, ln_b, w_conv_out, w_pool, pool_scale, w_pool_out, w_o, g_ffn, w_ffn_gate, w_ffn_up, w_ffn_down, g_final, loss_target, m_meta_tokens, m_g_mix, m_w_in, m_b_gate, m_w_dw, m_b_dw, m_ln_g, m_ln_b, m_w_conv_out, m_w_pool, m_pool_scale, m_w_pool_out, m_w_o, m_g_ffn, m_w_ffn_gate, m_w_ffn_up, m_w_ffn_down, m_g_final, v_meta_tokens, v_g_mix, v_w_in, v_b_gate, v_w_dw, v_b_dw, v_ln_g, v_ln_b, v_w_conv_out, v_w_pool, v_pool_scale, v_w_pool_out, v_w_o, v_g_ffn, v_w_ffn_gate, v_w_ffn_up, v_w_ffn_down, v_g_final):
    args = dict(locals())
    w = {n: args[n] for n in WEIGHT_ORDER}
    mom = {n: args["m_" + n] for n in WEIGHT_ORDER}
    var = {n: args["v_" + n] for n in WEIGHT_ORDER}
    seq = x.shape[1]
    nb = seq // BR + 1
    tp = nb * BR
    tk = tp // 2 if (tp // 2) % 16 == 0 else BR
    t_total = seq + N_META
    cx, cy, cc = lax.axis_index("x"), lax.axis_index("y"), lax.axis_index("c")
    chip = 2 * cx + cy
    chip1 = jnp.reshape(chip, (1,)).astype(jnp.int32)
    core = jnp.reshape(cc, (1,)).astype(jnp.int32)
    others = jnp.sort(jnp.stack([2 * (1 - cx) + cy, 2 * cx + (1 - cy), 2 * (1 - cx) + (1 - cy)]))
    idx = jnp.concatenate([chip1, others.astype(jnp.int32), core])
    xs, target = x[0], loss_target[0]

    tiny_own = jnp.concatenate([w["meta_tokens"], w["w_dw"][0], jnp.zeros((1, GD), F32)], axis=0)
    st_tiny = _tiny_start(tiny_own, "tiny_start")
    small = {n: w[n] for n in REPL if n != "g_final"}
    small["g_final"] = w["g_final"].reshape(1, D)

    def cast(group, dep):
        shards = [_shard2d(n, w[n]) for n in group]
        bufs = [None] * len(group)
        for ks in _by_shape(shards):
            done = _cast_into_slot([shards[k] for k in ks], chip1, dep, "cast_" + group[ks[0]])
            for k, b in zip(ks, done):
                bufs[k] = b
        return bufs

    def gather_finish(group, start, after, name):
        landed = _gather_wait(start[0], start[1], start[2], after, "gather_wait_" + name)
        return dict(zip(group, _forward_halves(landed, "forward_" + name)))

    st_in = _gather_start(cast(("w_in",), st_tiny[3]), None, "gather_start_in")
    tiny = _tiny_wait(st_tiny[0], st_tiny[1], st_tiny[2], [st_in[3]], "tiny_wait")
    tiny = lax.dynamic_update_slice(tiny, tiny_own[None], (chip, 0, 0))
    small["w_dw"] = _slabs_to_cols(tiny[:, N_META:])
    head = jnp.concatenate([jnp.zeros((PAD, D), F32), _slabs_to_cols(tiny[:, :N_META])], axis=0)
    bufs_mix, bufs_ffn = cast(GROUP_MIX, st_in[3]), cast(GROUP_FFN, st_in[3])
    u = _rms_u(head, xs, small["g_mix"] + st_in[3][0:1, 0:1], nb)
    z_own = _in_proj_own(u, w["w_in"][0], idx, nb)
    gw = gather_finish(("w_in",), st_in, [z_own] + bufs_mix + bufs_ffn, "in")
    st_mix = _gather_start(bufs_mix, gw["w_in"], "gather_start_mix")
    z = _in_proj_rest(u, gw["w_in"], z_own, idx, st_mix[3], nb)
    gw.update(gather_finish(GROUP_MIX, st_mix, [z], "mix"))
    st_ffn = _gather_start(bufs_ffn, gw["w_o"], "gather_start_ffn")
    w_pool_b = gw["w_pool"].reshape(N_SHARD, 4, 64, GD).transpose(1, 0, 2, 3).reshape(4, GD, GD)
    w_co_b, w_po_b, w_o_b = (gw[n].reshape(D, D) for n in ("w_conv_out", "w_pool_out", "w_o"))
    h1, yc, yp, mg, ca, cpre, m, mw, m2b = _mixers_fwd(
        z, head, xs, small["b_gate"] + st_ffn[3][0, 0], small["w_dw"], small["b_dw"], small["ln_g"], small["ln_b"],
        small["pool_scale"], w_co_b, w_pool_b, w_po_b, w_o_b, nb, t_total)
    gw.update(gather_finish(GROUP_FFN, st_ffn, [h1], "ffn"))

    dh1, dh1b, vb, fb, dgb, dub, dh2b, loss, dg_ffn, dg_final = _ffn_fwd_bwd(
        h1, target, small["g_ffn"], small["g_final"], gw["w_ffn_gate"].reshape(D_FF, D),
        gw["w_ffn_up"].reshape(D_FF, D), gw["w_ffn_down"].reshape(D_FF, D), nb)

    def slabs(name, g):
        if name == "w_in":
            return g
        if name == "w_pool":
            return g.reshape(4, N_SHARD, 64, GD).transpose(1, 0, 2, 3).reshape(N_SHARD, 4 * 64, GD)
        return g.reshape(N_SHARD, g.shape[0] // N_SHARD, g.shape[1])

    def reduce_start(group, grads, name):
        g32 = [slabs(n, grads[n][0]) for n in group]
        g16 = [slabs(n, grads[n][1]) for n in group]
        from_sibling = _swap_halves_bf16(g16, "swap_halves_" + name)
        halves = [None] * len(group)
        for ks in _by_shape(g32):
            done = _add_sibling_half([g32[k] for k in ks], [from_sibling[k] for k in ks], idx, "add_sibling_" + group[ks[0]])
            for k, pair in zip(ks, done):
                halves[k] = pair
        return [h for h, _ in halves], _scatter_start([hb for _, hb in halves], "scatter_start_" + name, SCATTER_IDS[name])

    def reduce_finish(group, halves, start, after, name):
        from_chips = _scatter_wait(start[0], start[1], start[2], after, "scatter_wait_" + name)
        reduced = [None] * len(group)
        for ks in _by_shape(halves):
            done = _add_chip_slabs([halves[k] for k in ks], [from_chips[k] for k in ks], idx, "add_chips_" + group[ks[0]])
            for k, r in zip(ks, done):
                reduced[k] = r
        return reduced

    half_ff = D_FF // 2
    grads_ffn = {
        "w_ffn_gate": _wgrad(dgb, vb, half_ff, D, tk, "wgrad_ffn_gate"),
        "w_ffn_up": _wgrad(dub, vb, half_ff, D, tk, "wgrad_ffn_up"),
        "w_ffn_down": _wgrad(fb, dh2b, half_ff, D, tk, "wgrad_ffn_down"),
    }
    halves_ffn, sc_ffn = reduce_start(GROUP_FFN, grads_ffn, "ffn")

    dycb, dypb, dzg, dconv, dmwb, dm, db_gate, dln_g, dln_b, db_dw, dps = _mixers_bwd_rows(
        dh1b, yc, yp, z, small["b_gate"], cpre, small["ln_g"], small["ln_b"], mw, small["pool_scale"],
        w_o_b, w_co_b, w_po_b, w_pool_b, sc_ffn[3], nb)
    grads_mix = {
        "w_conv_out": _wgrad(ca, dycb, D, D, tk, "wgrad_conv_out"),
        "w_pool": _wgrad(m, dmwb, GD, GD, tp, "wgrad_pool", diag=True),
        "w_pool_out": _wgrad(m2b, dypb, D, D, tk, "wgrad_pool_out"),
        "w_o": _wgrad(mg, dh1b, D, D, tk, "wgrad_o"),
    }
    halves_mix, sc_mix = reduce_start(GROUP_MIX, grads_mix, "mix")
    dzb, grad_x, dhead, dw_dw, dg_mix = _mixers_bwd_halo(
        dconv, dm, z, dzg, small["w_dw"], head, xs, small["g_mix"], dh1, gw["w_in"], sc_mix[3], nb, t_total)
    packed = jnp.concatenate(
        [dg_mix, db_gate.reshape(2, D), db_dw, dln_g, dln_b, dps, dg_ffn, dg_final,
         jnp.broadcast_to(loss, (1, D)), jnp.zeros((6, D), F32), dhead[PAD:], dw_dw], axis=0)
    sm = _small_start(packed, "small_start")
    grads_in = {"w_in": _wgrad(u, dzb, D, D_IN // N_SHARD, tk, "wgrad_in", col_major=True, dep=sm[3])}
    halves_in, sc_in = reduce_start(("w_in",), grads_in, "in")

    land = _small_wait(sm[0], sm[1], sm[2], [sc_in[3]], "small_wait")
    summed = _sum_slots(land, packed, jnp.reshape(4 * cx + 2 * cy + cc, (1,)).astype(jnp.int32))
    loss = summed[9, 0]

    first = GROUP_FFN + GROUP_MIX
    reduced_half = reduce_finish(GROUP_FFN, halves_ffn, sc_ffn, [summed], "ffn")
    reduced_half += reduce_finish(GROUP_MIX, halves_mix, sc_mix, [summed], "mix")
    reduced = dict(zip(first, _join_halves(reduced_half, "join_halves_first")))
    updates = {}
    for ks in _by_shape([reduced[n] for n in first]):
        names = [first[k] for k in ks]
        done = _adamw([(reduced[n], _shard2d(n, w[n]), _shard2d(n, mom[n]), _shard2d(n, var[n])) for n in names],
                      "adamw_" + names[0])
        updates.update(zip(names, done))

    def repl_stack(d):
        return jnp.concatenate([d["g_mix"], d["b_gate"].reshape(2, D), d["b_dw"], d["ln_g"], d["ln_b"],
                                d["pool_scale"], d["g_ffn"], d["g_final"].reshape(1, D), jnp.ones((7, D), F32)], axis=0)

    def shard_stack(d):
        return jnp.concatenate([d["meta_tokens"], d["w_dw"][0], jnp.ones((1, GD), F32)], axis=0)

    g_repl = summed[0:16]
    g_shard = lax.dynamic_slice_in_dim(summed[16:64], chip * GD, GD, axis=1)
    g_repl, d_repl, m_repl, v_repl = _adamw([(g_repl, repl_stack(w), repl_stack(mom), repl_stack(var))], "adamw_repl")[0]
    g_shard, d_shard, m_shard, v_shard = _adamw(
        [(g_shard, shard_stack(w), shard_stack(mom), shard_stack(var))], "adamw_cols")[0]

    done_first = [updates[n][1] for n in first] + [d_repl, d_shard]
    last_half = reduce_finish(("w_in",), halves_in, sc_in, done_first, "in")
    reduced["w_in"] = _join_halves(last_half, "join_halves_in")[0]
    updates["w_in"] = _adamw([(reduced["w_in"], w["w_in"][0], mom["w_in"][0], var["w_in"][0])], "adamw_w_in")[0]

    def unpack(name, repl, shard):
        if name == "meta_tokens":
            return shard[0:N_META]
        if name == "w_dw":
            return shard[N_META:N_META + KW].reshape(1, KW, GD)
        row = {"g_mix": 0, "b_gate": 1, "b_dw": 3, "ln_g": 4, "ln_b": 5, "pool_scale": 6, "g_ffn": 7, "g_final": 8}[name]
        if name == "b_gate":
            return repl[1:3].reshape(1, 2 * D)
        if name == "g_final":
            return repl[8]
        return repl[row:row + 1]

    out_g, out_d, out_m, out_v = {}, {}, {}, {}
    for n in WEIGHT_ORDER:
        if n in BIG:
            g, d_, m_, v_ = updates[n]
            shape = w[n].shape
            out_g[n], out_d[n], out_m[n], out_v[n] = (_unshard2d(n, a, shape) for a in (g, d_, m_, v_))
        else:
            out_g[n] = unpack(n, g_repl, g_shard)
            out_d[n] = unpack(n, d_repl, d_shard)
            out_m[n] = unpack(n, m_repl, m_shard)
            out_v[n] = unpack(n, v_repl, v_shard)
    return (loss, grad_x[None], *[out_g[n] for n in WEIGHT_ORDER], *[out_d[n] for n in WEIGHT_ORDER],
            *[out_m[n] for n in WEIGHT_ORDER], *[out_v[n] for n in WEIGHT_ORDER])
```
